```python
import jax, jax.numpy as jnp
from jax import lax
import numpy as np

D_MODEL = 1024
BATCH = 32
SEQ = 2048
DEPTH = 2

N_A_LAYERS = DEPTH // 2
N_B_LAYERS = DEPTH - N_A_LAYERS
PLE_DIM = 256
CONV_WIDTH = 31
CONV_EXPAND = 2
CONV_CH = CONV_EXPAND * D_MODEL
HEAD_DIM = 64
N_HEADS = D_MODEL // HEAD_DIM
ATTN_WIDTH = N_HEADS * HEAD_DIM
DILATION_GROUPS = ((128, 1), (512, 4), (2048, 16))
N_GROUPS = len(DILATION_GROUPS)
ROPE_THETA = 500000.0
ROPE_DIM = HEAD_DIM // 4
EPS = 1e-6
NEG_INF = -1e30

kernel_name = "yoco_conformer_dilated_hybrid"


def rmsnorm(x, g):
    xf = x.astype(jnp.float32)
    y = xf * lax.rsqrt(jnp.mean(xf * xf, axis=-1, keepdims=True) + EPS)
    return (y * g.astype(jnp.float32)).astype(x.dtype)


def layernorm(x, g, b):
    xf = x.astype(jnp.float32)
    mu = jnp.mean(xf, axis=-1, keepdims=True)
    var = jnp.mean(jnp.square(xf - mu), axis=-1, keepdims=True)
    y = (xf - mu) * lax.rsqrt(var + EPS) * g.astype(jnp.float32) + b.astype(jnp.float32)
    return y.astype(x.dtype)


def rope_partial(x, pos):
    half = ROPE_DIM // 2
    inv = ROPE_THETA ** (-jnp.arange(half, dtype=jnp.float32) * (2.0 / ROPE_DIM))
    ang = pos.astype(jnp.float32)[:, None] * inv[None, :]
    cos = jnp.cos(ang)[None, :, None, :]
    sin = jnp.sin(ang)[None, :, None, :]
    xf = x.astype(jnp.float32)
    x1 = xf[..., :half]
    x2 = xf[..., half:ROPE_DIM]
    out = jnp.concatenate([x1 * cos - x2 * sin, x1 * sin + x2 * cos, xf[..., ROPE_DIM:]], axis=-1)
    return out.astype(x.dtype)


def mixer_a(u, w_in, conv_w, conv_b, ln_g, ln_b, w_out):
    a, b, z = jnp.split(u @ w_in, 3, axis=-1)
    glu = a * jax.nn.sigmoid(b)
    y = lax.conv_general_dilated(
        glu, conv_w[:, None, :], window_strides=(1,),
        padding=[(CONV_WIDTH - 1, 0)],
        dimension_numbers=('NWC', 'WIO', 'NWC'),
        feature_group_count=CONV_CH)
    y = jax.nn.silu(layernorm(y + conv_b, ln_g, ln_b))
    return (y * jax.nn.silu(z)) @ w_out


def dilated_attention(q, k, v, span, dil):
    B, S, H, Dh = q.shape
    L = S // dil
    nb = -(-L // span)
    Lp = nb * span
    pad = Lp - L
    N = B * dil

    def to_res(t):
        return t.reshape(B, L, dil, H, Dh).transpose(0, 2, 1, 3, 4).reshape(N, L, H, Dh)

    qb = jnp.pad(to_res(q), ((0, 0), (0, pad), (0, 0), (0, 0))).reshape(N, nb, span, H, Dh)
    kp = jnp.pad(to_res(k), ((0, 0), (span, pad), (0, 0), (0, 0)))
    vp = jnp.pad(to_res(v), ((0, 0), (span, pad), (0, 0), (0, 0)))

    def band(t):
        return jnp.concatenate([t[:, :Lp].reshape(N, nb, span, H, Dh),
                                t[:, span:].reshape(N, nb, span, H, Dh)], axis=2)

    kb, vb = band(kp), band(vp)
    s = jnp.einsum('nbqhd,nbkhd->nbhqk', qb, kb).astype(jnp.float32) * (Dh ** -0.5)
    qi = jnp.arange(span)[:, None]
    kj = jnp.arange(2 * span)[None, :]
    dist = qi - kj + span
    key_idx = jnp.arange(nb)[:, None] * span + jnp.arange(2 * span)[None, :] - span
    mask = ((dist >= 0) & (dist <= span))[None] & (key_idx >= 0)[:, None, :]
    s = jnp.where(mask[None, :, None], s, NEG_INF)
    lse = jax.nn.logsumexp(s, axis=-1)
    prob = jnp.exp(s - lse[..., None])
    o = jnp.einsum('nbhqk,nbkhd->nbqhd', prob.astype(v.dtype), vb)
    o = o.reshape(B, dil, Lp, H, Dh)[:, :, :L].transpose(0, 2, 1, 3, 4).reshape(B, S, H, Dh)
    lse = lse.transpose(0, 1, 3, 2).reshape(B, dil, Lp, H)[:, :, :L]
    lse = lse.transpose(0, 2, 1, 3).reshape(B, S, H)
    return o, lse


def mixer_b(u, w_in, q_g, w_out, k_sh, v_sh, pos):
    B, S, _ = u.shape
    proj = u @ w_in
    q_all = proj[..., :N_GROUPS * ATTN_WIDTH].reshape(B, S, N_GROUPS, N_HEADS, HEAD_DIM)
    gate = proj[..., N_GROUPS * ATTN_WIDTH:]
    outs, lses = [], []
    for g, (window, dil) in enumerate(DILATION_GROUPS):
        q = rope_partial(rmsnorm(q_all[:, :, g], q_g[g]), pos)
        o, lse = dilated_attention(q, k_sh, v_sh, window // dil, dil)
        outs.append(o)
        lses.append(lse)
    wts = jax.nn.softmax(jnp.stack(lses, axis=0), axis=0)
    o = jnp.einsum('gbsh,gbshd->bshd', wts.astype(outs[0].dtype), jnp.stack(outs, axis=0))
    return (o.reshape(B, S, ATTN_WIDTH) * jax.nn.silu(gate)) @ w_out


def _fwd_setup_inputs(seed: int = 0) -> dict:
    key = jax.random.key(seed)
    ks = jax.random.split(key, 20)
    f32 = jnp.float32

    def nrm(k, shape, fan_in):
        return jax.random.normal(k, shape, f32) * (fan_in ** -0.5)

    def gain(k, shape):
        return 1.0 + 0.02 * jax.random.normal(k, shape, f32)

    return {
        "x": jax.random.normal(ks[0], (BATCH, SEQ, D_MODEL), f32),
        "p": jax.random.normal(ks[1], (DEPTH, BATCH, SEQ, PLE_DIM), f32),
        "norm_g": gain(ks[2], (DEPTH, D_MODEL)),
        "w_in_a": nrm(ks[3], (N_A_LAYERS, D_MODEL, 3 * CONV_CH), D_MODEL),
        "conv_w": nrm(ks[4], (N_A_LAYERS, CONV_WIDTH, CONV_CH), CONV_WIDTH),
        "conv_b": 0.02 * jax.random.normal(ks[5], (N_A_LAYERS, CONV_CH), f32),
        "ln_g": gain(ks[6], (N_A_LAYERS, CONV_CH)),
        "ln_b": 0.02 * jax.random.normal(ks[7], (N_A_LAYERS, CONV_CH), f32),
        "w_out_a": nrm(ks[8], (N_A_LAYERS, CONV_CH, D_MODEL), CONV_CH),
        "kv_norm_g": gain(ks[9], (D_MODEL,)),
        "w_kv": nrm(ks[10], (D_MODEL, 2 * ATTN_WIDTH), D_MODEL),
        "k_norm_g": gain(ks[11], (HEAD_DIM,)),
        "w_in_b": nrm(ks[12], (N_B_LAYERS, D_MODEL, (N_GROUPS + 1) * ATTN_WIDTH), D_MODEL),
        "q_norm_g": gain(ks[13], (N_B_LAYERS, N_GROUPS, HEAD_DIM)),
        "w_out_b": nrm(ks[14], (N_B_LAYERS, ATTN_WIDTH, D_MODEL), ATTN_WIDTH),
        "ple_norm_g": gain(ks[15], (DEPTH, D_MODEL)),
        "w_ple_gate": nrm(ks[16], (DEPTH, D_MODEL, D_MODEL), D_MODEL),
        "w_ple_proj": nrm(ks[17], (DEPTH, PLE_DIM, D_MODEL), PLE_DIM),
    }


def _fwd_reference(x, p, norm_g, w_in_a, conv_w, conv_b, ln_g, ln_b, w_out_a,
              kv_norm_g, w_kv, k_norm_g, w_in_b, q_norm_g, w_out_b,
              ple_norm_g, w_ple_gate, w_ple_proj):
    B, S, _ = x.shape
    pos = jnp.arange(S)
    k_sh = None
    v_sh = None
    for i in range(DEPTH):
        u = rmsnorm(x, norm_g[i])
        if i < N_A_LAYERS:
            j = i
            h = x + mixer_a(u, w_in_a[j], conv_w[j], conv_b[j], ln_g[j], ln_b[j], w_out_a[j])
        else:
            j = i - N_A_LAYERS
            h = x + mixer_b(u, w_in_b[j], q_norm_g[j], w_out_b[j], k_sh, v_sh, pos)
        ple_gate = jax.nn.sigmoid(rmsnorm(h, ple_norm_g[i]) @ w_ple_gate[i])
        h = h + ple_gate * (p[i] @ w_ple_proj[i])
        if i == N_A_LAYERS - 1:
            k_sh, v_sh = jnp.split(rmsnorm(h, kv_norm_g) @ w_kv, 2, axis=-1)
            k_sh = rope_partial(rmsnorm(k_sh.reshape(B, S, N_HEADS, HEAD_DIM), k_norm_g), pos)
            v_sh = v_sh.reshape(B, S, N_HEADS, HEAD_DIM)
        x = h
    return x


import jax as _jax
import jax.numpy as _jnp

TWIN_FORMAT = 'train_step'
FWD_PARAMS = ['x', 'p', 'norm_g', 'w_in_a', 'conv_w', 'conv_b', 'ln_g', 'ln_b', 'w_out_a', 'kv_norm_g', 'w_kv', 'k_norm_g', 'w_in_b', 'q_norm_g', 'w_out_b', 'ple_norm_g', 'w_ple_gate', 'w_ple_proj']
TWIN_WEIGHTS = ['norm_g', 'w_in_a', 'conv_w', 'conv_b', 'ln_g', 'ln_b', 'w_out_a', 'kv_norm_g', 'w_kv', 'k_norm_g', 'w_in_b', 'q_norm_g', 'w_out_b', 'ple_norm_g', 'w_ple_gate', 'w_ple_proj']
TWIN_DIFF_INPUT = 'x'
TWIN_INPUTS = ['x', 'p', 'norm_g', 'w_in_a', 'conv_w', 'conv_b', 'ln_g', 'ln_b', 'w_out_a', 'kv_norm_g', 'w_kv', 'k_norm_g', 'w_in_b', 'q_norm_g', 'w_out_b', 'ple_norm_g', 'w_ple_gate', 'w_ple_proj', 'loss_target', 'm_norm_g', 'm_w_in_a', 'm_conv_w', 'm_conv_b', 'm_ln_g', 'm_ln_b', 'm_w_out_a', 'm_kv_norm_g', 'm_w_kv', 'm_k_norm_g', 'm_w_in_b', 'm_q_norm_g', 'm_w_out_b', 'm_ple_norm_g', 'm_w_ple_gate', 'm_w_ple_proj', 'v_norm_g', 'v_w_in_a', 'v_conv_w', 'v_conv_b', 'v_ln_g', 'v_ln_b', 'v_w_out_a', 'v_kv_norm_g', 'v_w_kv', 'v_k_norm_g', 'v_w_in_b', 'v_q_norm_g', 'v_w_out_b', 'v_ple_norm_g', 'v_w_ple_gate', 'v_w_ple_proj']
TWIN_OUTPUTS = ['loss', 'grad_x', 'grad_norm_g', 'grad_w_in_a', 'grad_conv_w', 'grad_conv_b', 'grad_ln_g', 'grad_ln_b', 'grad_w_out_a', 'grad_kv_norm_g', 'grad_w_kv', 'grad_k_norm_g', 'grad_w_in_b', 'grad_q_norm_g', 'grad_w_out_b', 'grad_ple_norm_g', 'grad_w_ple_gate', 'grad_w_ple_proj', 'delta_norm_g', 'delta_w_in_a', 'delta_conv_w', 'delta_conv_b', 'delta_ln_g', 'delta_ln_b', 'delta_w_out_a', 'delta_kv_norm_g', 'delta_w_kv', 'delta_k_norm_g', 'delta_w_in_b', 'delta_q_norm_g', 'delta_w_out_b', 'delta_ple_norm_g', 'delta_w_ple_gate', 'delta_w_ple_proj', 'new_m_norm_g', 'new_m_w_in_a', 'new_m_conv_w', 'new_m_conv_b', 'new_m_ln_g', 'new_m_ln_b', 'new_m_w_out_a', 'new_m_kv_norm_g', 'new_m_w_kv', 'new_m_k_norm_g', 'new_m_w_in_b', 'new_m_q_norm_g', 'new_m_w_out_b', 'new_m_ple_norm_g', 'new_m_w_ple_gate', 'new_m_w_ple_proj', 'new_v_norm_g', 'new_v_w_in_a', 'new_v_conv_w', 'new_v_conv_b', 'new_v_ln_g', 'new_v_ln_b', 'new_v_w_out_a', 'new_v_kv_norm_g', 'new_v_w_kv', 'new_v_k_norm_g', 'new_v_w_in_b', 'new_v_q_norm_g', 'new_v_w_out_b', 'new_v_ple_norm_g', 'new_v_w_ple_gate', 'new_v_w_ple_proj']
TWIN_LEAF_KINDS = {'loss': 'loss', 'grad_x': 'grad_x', 'grad_norm_g': 'grad_w', 'grad_w_in_a': 'grad_w', 'grad_conv_w': 'grad_w', 'grad_conv_b': 'grad_w', 'grad_ln_g': 'grad_w', 'grad_ln_b': 'grad_w', 'grad_w_out_a': 'grad_w', 'grad_kv_norm_g': 'grad_w', 'grad_w_kv': 'grad_w', 'grad_k_norm_g': 'grad_w', 'grad_w_in_b': 'grad_w', 'grad_q_norm_g': 'grad_w', 'grad_w_out_b': 'grad_w', 'grad_ple_norm_g': 'grad_w', 'grad_w_ple_gate': 'grad_w', 'grad_w_ple_proj': 'grad_w', 'delta_norm_g': 'delta_w', 'delta_w_in_a': 'delta_w', 'delta_conv_w': 'delta_w', 'delta_conv_b': 'delta_w', 'delta_ln_g': 'delta_w', 'delta_ln_b': 'delta_w', 'delta_w_out_a': 'delta_w', 'delta_kv_norm_g': 'delta_w', 'delta_w_kv': 'delta_w', 'delta_k_norm_g': 'delta_w', 'delta_w_in_b': 'delta_w', 'delta_q_norm_g': 'delta_w', 'delta_w_out_b': 'delta_w', 'delta_ple_norm_g': 'delta_w', 'delta_w_ple_gate': 'delta_w', 'delta_w_ple_proj': 'delta_w', 'new_m_norm_g': 'new_m', 'new_m_w_in_a': 'new_m', 'new_m_conv_w': 'new_m', 'new_m_conv_b': 'new_m', 'new_m_ln_g': 'new_m', 'new_m_ln_b': 'new_m', 'new_m_w_out_a': 'new_m', 'new_m_kv_norm_g': 'new_m', 'new_m_w_kv': 'new_m', 'new_m_k_norm_g': 'new_m', 'new_m_w_in_b': 'new_m', 'new_m_q_norm_g': 'new_m', 'new_m_w_out_b': 'new_m', 'new_m_ple_norm_g': 'new_m', 'new_m_w_ple_gate': 'new_m', 'new_m_w_ple_proj': 'new_m', 'new_v_norm_g': 'new_v', 'new_v_w_in_a': 'new_v', 'new_v_conv_w': 'new_v', 'new_v_conv_b': 'new_v', 'new_v_ln_g': 'new_v', 'new_v_ln_b': 'new_v', 'new_v_w_out_a': 'new_v', 'new_v_kv_norm_g': 'new_v', 'new_v_w_kv': 'new_v', 'new_v_k_norm_g': 'new_v', 'new_v_w_in_b': 'new_v', 'new_v_q_norm_g': 'new_v', 'new_v_w_out_b': 'new_v', 'new_v_ple_norm_g': 'new_v', 'new_v_w_ple_gate': 'new_v', 'new_v_w_ple_proj': 'new_v'}


def _forward(args):
    return _fwd_reference(*[args[k] for k in FWD_PARAMS])


def _output_shape():
    out = _jax.eval_shape(lambda: _forward(_fwd_setup_inputs(0)))
    return out.shape, out.dtype

N_MICROBATCH = 1
ADAM_LR = 0.001
ADAM_B1 = 0.9
ADAM_B2 = 0.999
ADAM_EPS = 1e-08
ADAM_WD = 0.01
ADAM_STEP = 10
PER_EXAMPLE_BATCH_AXIS = {'x': 0, 'p': 1, 'loss_target': 0}
SHARED_INPUTS = []
_WEIGHT_DTYPES = {'norm_g': _jnp.float32, 'w_in_a': _jnp.float32, 'conv_w': _jnp.float32, 'conv_b': _jnp.float32, 'ln_g': _jnp.float32, 'ln_b': _jnp.float32, 'w_out_a': _jnp.float32, 'kv_norm_g': _jnp.float32, 'w_kv': _jnp.float32, 'k_norm_g': _jnp.float32, 'w_in_b': _jnp.float32, 'q_norm_g': _jnp.float32, 'w_out_b': _jnp.float32, 'ple_norm_g': _jnp.float32, 'w_ple_gate': _jnp.float32, 'w_ple_proj': _jnp.float32}
MOMENT_SCALE = {'norm_g': 6.454719e+00, 'w_in_a': 1.226989e-01, 'conv_w': 1.422351e-01, 'conv_b': 9.854318e-01, 'ln_g': 4.686372e+00, 'ln_b': 3.109087e+00, 'w_out_a': 2.715918e-01, 'kv_norm_g': 1.556041e-01, 'w_kv': 5.468405e-02, 'k_norm_g': 1.132346e+00, 'w_in_b': 3.832054e-02, 'q_norm_g': 4.139655e-01, 'w_out_b': 5.696554e-02, 'ple_norm_g': 1.890821e+00, 'w_ple_gate': 1.522137e-01, 'w_ple_proj': 6.884557e-01}


def _to_microbatches(a, axis):
    t = _jnp.moveaxis(a, axis, 0)
    t = t.reshape((N_MICROBATCH, t.shape[0] // N_MICROBATCH) + t.shape[1:])
    return _jnp.moveaxis(t, 1, axis + 1)


def setup_inputs(seed: int = 0) -> dict:
    inp = _fwd_setup_inputs(seed)
    key = _jax.random.fold_in(_jax.random.key(seed), 7919)
    shape, _ = _output_shape()
    out = dict(inp)
    out["loss_target"] = _jax.random.normal(_jax.random.fold_in(key, 0), shape, _jnp.float32)
    for i, name in enumerate(TWIN_WEIGHTS):
        w = inp[name].astype(_jnp.float32)
        if MOMENT_SCALE is None:
            s = _jnp.sqrt(_jnp.mean(_jnp.square(w)) + 1e-30)
        else:
            s = MOMENT_SCALE[name]
        km, kv = _jax.random.split(_jax.random.fold_in(key, i + 1))
        out[name] = w
        out["m_" + name] = s * _jax.random.normal(km, w.shape, _jnp.float32)
        out["v_" + name] = (s * s) * _jax.random.uniform(kv, w.shape, _jnp.float32, 0.5, 1.5)
    if N_MICROBATCH > 1:
        for name, axis in PER_EXAMPLE_BATCH_AXIS.items():
            out[name] = _to_microbatches(out[name], axis)
    return {'x': out['x'], 'p': out['p'], 'norm_g': out['norm_g'], 'w_in_a': out['w_in_a'], 'conv_w': out['conv_w'], 'conv_b': out['conv_b'], 'ln_g': out['ln_g'], 'ln_b': out['ln_b'], 'w_out_a': out['w_out_a'], 'kv_norm_g': out['kv_norm_g'], 'w_kv': out['w_kv'], 'k_norm_g': out['k_norm_g'], 'w_in_b': out['w_in_b'], 'q_norm_g': out['q_norm_g'], 'w_out_b': out['w_out_b'], 'ple_norm_g': out['ple_norm_g'], 'w_ple_gate': out['w_ple_gate'], 'w_ple_proj': out['w_ple_proj'], 'loss_target': out['loss_target'], 'm_norm_g': out['m_norm_g'], 'm_w_in_a': out['m_w_in_a'], 'm_conv_w': out['m_conv_w'], 'm_conv_b': out['m_conv_b'], 'm_ln_g': out['m_ln_g'], 'm_ln_b': out['m_ln_b'], 'm_w_out_a': out['m_w_out_a'], 'm_kv_norm_g': out['m_kv_norm_g'], 'm_w_kv': out['m_w_kv'], 'm_k_norm_g': out['m_k_norm_g'], 'm_w_in_b': out['m_w_in_b'], 'm_q_norm_g': out['m_q_norm_g'], 'm_w_out_b': out['m_w_out_b'], 'm_ple_norm_g': out['m_ple_norm_g'], 'm_w_ple_gate': out['m_w_ple_gate'], 'm_w_ple_proj': out['m_w_ple_proj'], 'v_norm_g': out['v_norm_g'], 'v_w_in_a': out['v_w_in_a'], 'v_conv_w': out['v_conv_w'], 'v_conv_b': out['v_conv_b'], 'v_ln_g': out['v_ln_g'], 'v_ln_b': out['v_ln_b'], 'v_w_out_a': out['v_w_out_a'], 'v_kv_norm_g': out['v_kv_norm_g'], 'v_w_kv': out['v_w_kv'], 'v_k_norm_g': out['v_k_norm_g'], 'v_w_in_b': out['v_w_in_b'], 'v_q_norm_g': out['v_q_norm_g'], 'v_w_out_b': out['v_w_out_b'], 'v_ple_norm_g': out['v_ple_norm_g'], 'v_w_ple_gate': out['v_w_ple_gate'], 'v_w_ple_proj': out['v_w_ple_proj']}


def _loss(weights, diff, rest, loss_target):
    with _jax.named_scope("forward"):
        args = {**rest, TWIN_DIFF_INPUT: diff, **{k: w.astype(_WEIGHT_DTYPES[k]) for k, w in weights.items()}}
        y = _forward(args)
    with _jax.named_scope("loss_head"):
        err = _jnp.square(y.astype(_jnp.float32) - loss_target)
        return 0.5 * _jnp.sum(_jnp.mean(err, axis=-1)) if err.ndim else 0.5 * err


def _adamw(w, g, m, v):
    m = ADAM_B1 * m + (1.0 - ADAM_B1) * g
    v = ADAM_B2 * v + (1.0 - ADAM_B2) * _jnp.square(g)
    m_hat = m / (1.0 - ADAM_B1 ** ADAM_STEP)
    v_hat = v / (1.0 - ADAM_B2 ** ADAM_STEP)
    delta = -ADAM_LR * (m_hat / (_jnp.sqrt(v_hat) + ADAM_EPS) + ADAM_WD * w)
    return delta, m, v


def reference(x, p, norm_g, w_in_a, conv_w, conv_b, ln_g, ln_b, w_out_a, kv_norm_g, w_kv, k_norm_g, w_in_b, q_norm_g, w_out_b, ple_norm_g, w_ple_gate, w_ple_proj, loss_target, m_norm_g, m_w_in_a, m_conv_w, m_conv_b, m_ln_g, m_ln_b, m_w_out_a, m_kv_norm_g, m_w_kv, m_k_norm_g, m_w_in_b, m_q_norm_g, m_w_out_b, m_ple_norm_g, m_w_ple_gate, m_w_ple_proj, v_norm_g, v_w_in_a, v_conv_w, v_conv_b, v_ln_g, v_ln_b, v_w_out_a, v_kv_norm_g, v_w_kv, v_k_norm_g, v_w_in_b, v_q_norm_g, v_w_out_b, v_ple_norm_g, v_w_ple_gate, v_w_ple_proj):
    given = dict(x=x, p=p, norm_g=norm_g, w_in_a=w_in_a, conv_w=conv_w, conv_b=conv_b, ln_g=ln_g, ln_b=ln_b, w_out_a=w_out_a, kv_norm_g=kv_norm_g, w_kv=w_kv, k_norm_g=k_norm_g, w_in_b=w_in_b, q_norm_g=q_norm_g, w_out_b=w_out_b, ple_norm_g=ple_norm_g, w_ple_gate=w_ple_gate, w_ple_proj=w_ple_proj, loss_target=loss_target, m_norm_g=m_norm_g, m_w_in_a=m_w_in_a, m_conv_w=m_conv_w, m_conv_b=m_conv_b, m_ln_g=m_ln_g, m_ln_b=m_ln_b, m_w_out_a=m_w_out_a, m_kv_norm_g=m_kv_norm_g, m_w_kv=m_w_kv, m_k_norm_g=m_k_norm_g, m_w_in_b=m_w_in_b, m_q_norm_g=m_q_norm_g, m_w_out_b=m_w_out_b, m_ple_norm_g=m_ple_norm_g, m_w_ple_gate=m_w_ple_gate, m_w_ple_proj=m_w_ple_proj, v_norm_g=v_norm_g, v_w_in_a=v_w_in_a, v_conv_w=v_conv_w, v_conv_b=v_conv_b, v_ln_g=v_ln_g, v_ln_b=v_ln_b, v_w_out_a=v_w_out_a, v_kv_norm_g=v_kv_norm_g, v_w_kv=v_w_kv, v_k_norm_g=v_k_norm_g, v_w_in_b=v_w_in_b, v_q_norm_g=v_q_norm_g, v_w_out_b=v_w_out_b, v_ple_norm_g=v_ple_norm_g, v_w_ple_gate=v_w_ple_gate, v_w_ple_proj=v_w_ple_proj)
    weights = {n: given[n] for n in TWIN_WEIGHTS}
    shared = {n: given[n] for n in SHARED_INPUTS}
    per_example = {n: given[n] for n in ['x', 'p']}
    grad_fn = _jax.value_and_grad(_loss, argnums=(0, 1))

    def one_microbatch(ex, loss_target):
        ex = dict(ex)
        diff = ex.pop(TWIN_DIFF_INPUT)
        return grad_fn(weights, diff, {**shared, **ex}, loss_target)

    if N_MICROBATCH == 1:
        loss, (grad_w, grad_x) = one_microbatch(per_example, given["loss_target"])
    else:
        def body(carry, xs):
            loss_sum, grad_sum = carry
            l_k, (gw_k, gx_k) = one_microbatch(xs[0], xs[1])
            with _jax.named_scope("update"):
                return (loss_sum + l_k, _jax.tree.map(_jnp.add, grad_sum, gw_k)), gx_k

        init = (_jnp.zeros((), _jnp.float32), _jax.tree.map(_jnp.zeros_like, weights))
        (loss, grad_w), grad_x = _jax.lax.scan(body, init, (per_example, given["loss_target"]))
    with _jax.named_scope("update"):
        delta_w, new_m, new_v = {}, {}, {}
        for n in TWIN_WEIGHTS:
            delta_w[n], new_m[n], new_v[n] = _adamw(weights[n], grad_w[n], given["m_" + n], given["v_" + n])
    return (loss, grad_x, *[grad_w[n] for n in TWIN_WEIGHTS], *[delta_w[n] for n in TWIN_WEIGHTS],
            *[new_m[n] for n in TWIN_WEIGHTS], *[new_v[n] for n in TWIN_WEIGHTS])
```

```python
import functools

import jax
import jax.numpy as jnp
from jax import lax
from jax.experimental import pallas as pl
from jax.experimental.pallas import tpu as pltpu

F32 = jnp.float32
BF16 = jnp.bfloat16

N_DEV = 8
HEAD_DIM = 64
ROPE_DIM = 16
ROPE_THETA = 500000.0
EPS = 1e-6
NEG_INF = -1e30
SPAN = 128
DILATIONS = (1, 4, 16)
CONV_WIDTH = 31
HALO = 32
PACK_LANES = 1024
V7X_LANES = 128
V7X_SUBLANES = 8
VMEM_LIMIT_BYTES = 56 * 1024 * 1024

ADAM_LR = 0.001
ADAM_B1 = 0.9
ADAM_B2 = 0.999
ADAM_EPS = 1e-08
ADAM_WD = 0.01
ADAM_STEP = 10

WEIGHT_NAMES = ('norm_g', 'w_in_a', 'conv_w', 'conv_b', 'ln_g', 'ln_b', 'w_out_a', 'kv_norm_g', 'w_kv',
                'k_norm_g', 'w_in_b', 'q_norm_g', 'w_out_b', 'ple_norm_g', 'w_ple_gate', 'w_ple_proj')
SHARD_AXIS = {'norm_g': None, 'w_in_a': 2, 'conv_w': 2, 'conv_b': 1, 'ln_g': 1, 'ln_b': 1, 'w_out_a': 1,
              'kv_norm_g': None, 'w_kv': 1, 'k_norm_g': None, 'w_in_b': 2, 'q_norm_g': None, 'w_out_b': 1,
              'ple_norm_g': None, 'w_ple_gate': 1, 'w_ple_proj': 2}
MATMUL_WEIGHTS = ('w_in_a', 'w_out_a', 'w_kv', 'w_in_b', 'w_out_b', 'w_ple_gate', 'w_ple_proj')
VECTOR_WEIGHTS = ('conv_w', 'conv_b', 'ln_g', 'ln_b')


def _pick(n, target, mult):
    t = (min(target, n) // mult) * mult
    while t >= mult:
        if n % t == 0:
            return t
        t -= mult
    return n


def _params(n_grid):
    return pltpu.CompilerParams(dimension_semantics=("arbitrary",) * n_grid, vmem_limit_bytes=VMEM_LIMIT_BYTES)


def _sig(x):
    return 1.0 / (1.0 + jnp.exp(-x))


def _colsum8(v):
    r, w = v.shape
    return v.reshape(r // V7X_SUBLANES, V7X_SUBLANES, w).sum(axis=0)


def _rows(tm, w, col=0):
    return pl.BlockSpec((tm, w), lambda i: (i, col))


def _const(shape):
    nd = len(shape)
    return pl.BlockSpec(shape, lambda i: (0,) * nd)


def _segsum(v, e_ref):
    hi = v.astype(BF16)
    lo = (v - hi.astype(F32)).astype(BF16)
    e = e_ref[...]
    return jnp.dot(hi, e, preferred_element_type=F32) + jnp.dot(lo, e, preferred_element_type=F32)


def _matmul(a, b, mode, name, out_dtype=F32, add=None):
    if mode == 'nn':
        (m, k), (_, n) = a.shape, b.shape
    elif mode == 'nt':
        (m, k), (n, _) = a.shape, b.shape
    else:
        (k, m), (_, n) = a.shape, b.shape
    if mode == 'tn':
        tm, tn, tk = _pick(m, 1024, 128), _pick(n, 1024, 128), _pick(k, 512, 128)
        grid = (m // tm, n // tn, k // tk)
        a_spec = pl.BlockSpec((tk, tm), lambda i, j, kk: (kk, i))
        b_spec = pl.BlockSpec((tk, tn), lambda i, j, kk: (kk, j))
        o_spec = pl.BlockSpec((tm, tn), lambda i, j, kk: (i, j))
        dims = (((0,), (0,)), ((), ()))
    else:
        tm, tn, tk = _pick(m, 512, 128), _pick(n, 1024, 128), _pick(k, 1024, 128)
        grid = (n // tn, m // tm, k // tk)
        a_spec = pl.BlockSpec((tm, tk), lambda j, i, kk: (i, kk))
        o_spec = pl.BlockSpec((tm, tn), lambda j, i, kk: (i, j))
        if mode == 'nn':
            b_spec = pl.BlockSpec((tk, tn), lambda j, i, kk: (kk, j))
            dims = (((1,), (0,)), ((), ()))
        else:
            b_spec = pl.BlockSpec((tn, tk), lambda j, i, kk: (j, kk))
            dims = (((1,), (1,)), ((), ()))
    nk = grid[2]
    has_add = add is not None

    def body(*refs):
        a_ref, b_ref = refs[0], refs[1]
        add_ref = refs[2] if has_add else None
        o_ref = refs[2 + has_add]
        part = lax.dot_general(a_ref[...].astype(BF16), b_ref[...].astype(BF16), dims, preferred_element_type=F32)

        def finish(total):
            if has_add:
                total = total + add_ref[...]
            o_ref[...] = total.astype(out_dtype)

        if nk == 1:
            finish(part)
        else:
            acc_ref = refs[3 + has_add]
            kk = pl.program_id(2)

            @pl.when(kk == 0)
            def _():
                acc_ref[...] = part

            @pl.when(kk > 0)
            def _():
                acc_ref[...] += part

            @pl.when(kk == nk - 1)
            def _():
                finish(acc_ref[...])

    in_specs = [a_spec, b_spec] + ([o_spec] if has_add else [])
    args = [a, b] + ([add] if has_add else [])
    scratch = [pltpu.VMEM((tm, tn), F32)] if nk > 1 else []
    return pl.pallas_call(
        body, name=name, grid=grid, in_specs=in_specs, out_specs=o_spec,
        out_shape=jax.ShapeDtypeStruct((m, n), out_dtype), scratch_shapes=scratch,
        compiler_params=_params(3),
    )(*args)


def _rmsnorm_fwd(x, gains, name):
    t, d = x.shape
    tm = _pick(t, 512, 8)
    n = len(gains)

    def body(*refs):
        x_ref, g_refs, o_refs = refs[0], refs[1:1 + n], refs[1 + n:]
        xv = x_ref[...]
        y = xv * lax.rsqrt(jnp.mean(xv * xv, axis=-1, keepdims=True) + EPS)
        for g_ref, o_ref in zip(g_refs, o_refs):
            o_ref[...] = (y * g_ref[...]).astype(BF16)

    return pl.pallas_call(
        body, name=name, grid=(t // tm,),
        in_specs=[_rows(tm, d)] + [_const((1, d))] * n,
        out_specs=[_rows(tm, d)] * n,
        out_shape=[jax.ShapeDtypeStruct((t, d), BF16)] * n,
        compiler_params=_params(1),
    )(x, *gains)


def _rmsnorm_bwd(x, gains, dys, add, name):
    t, d = x.shape
    tm = _pick(t, 512, 8)
    n = len(gains)

    def body(*refs):
        x_ref, add_ref = refs[0], refs[1]
        g_refs, dy_refs = refs[2:2 + n], refs[2 + n:2 + 2 * n]
        dx_ref, dg_refs = refs[2 + 2 * n], refs[3 + 2 * n:]
        i = pl.program_id(0)
        xv = x_ref[...]
        r = lax.rsqrt(jnp.mean(xv * xv, axis=-1, keepdims=True) + EPS)
        xhat = xv * r
        dx = add_ref[...]
        for g_ref, dy_ref, dg_ref in zip(g_refs, dy_refs, dg_refs):
            dy = dy_ref[...]
            dyg = dy * g_ref[...]
            dx = dx + r * (dyg - xhat * jnp.mean(dyg * xhat, axis=-1, keepdims=True))
            part = _colsum8(dy * xhat)

            @pl.when(i == 0)
            def _():
                dg_ref[...] = part

            @pl.when(i > 0)
            def _():
                dg_ref[...] += part

        dx_ref[...] = dx

    outs = pl.pallas_call(
        body, name=name, grid=(t // tm,),
        in_specs=[_rows(tm, d), _rows(tm, d)] + [_const((1, d))] * n + [_rows(tm, d)] * n,
        out_specs=[_rows(tm, d)] + [_const((V7X_SUBLANES, d))] * n,
        out_shape=[jax.ShapeDtypeStruct((t, d), F32)] + [jax.ShapeDtypeStruct((V7X_SUBLANES, d), F32)] * n,
        compiler_params=_params(1),
    )(x, add, *gains, *dys)
    return outs[0], [o.sum(axis=0) for o in outs[1:]]


def _ple_fwd(h, gl, pp, name):
    t, d = h.shape
    tm = _pick(t, 512, 8)

    def body(h_ref, gl_ref, pp_ref, o_ref):
        o_ref[...] = h_ref[...] + _sig(gl_ref[...]) * pp_ref[...]

    return pl.pallas_call(
        body, name=name, grid=(t // tm,), in_specs=[_rows(tm, d)] * 3, out_specs=_rows(tm, d),
        out_shape=jax.ShapeDtypeStruct((t, d), F32), compiler_params=_params(1),
    )(h, gl, pp)


def _ple_bwd(dx, gl, pp, name):
    t, d = dx.shape
    tm = _pick(t, 512, 8)

    def body(dx_ref, gl_ref, pp_ref, dgl_ref, dpp_ref):
        dxv = dx_ref[...]
        sg = _sig(gl_ref[...])
        dpp_ref[...] = (dxv * sg).astype(BF16)
        dgl_ref[...] = (dxv * pp_ref[...] * sg * (1.0 - sg)).astype(BF16)

    return pl.pallas_call(
        body, name=name, grid=(t // tm,), in_specs=[_rows(tm, d)] * 3, out_specs=[_rows(tm, d)] * 2,
        out_shape=[jax.ShapeDtypeStruct((t, d), BF16)] * 2, compiler_params=_params(1),
    )(dx, gl, pp)


def _loss_fwd_bwd(y, target, name):
    t, d = y.shape
    tm = _pick(t, 512, 8)
    inv_d = 1.0 / d

    def body(y_ref, t_ref, dy_ref, l_ref):
        i = pl.program_id(0)
        e = y_ref[...] - t_ref[...]
        dy_ref[...] = e * inv_d
        part = _colsum8(e * e) * (0.5 * inv_d)

        @pl.when(i == 0)
        def _():
            l_ref[...] = part

        @pl.when(i > 0)
        def _():
            l_ref[...] += part

    return pl.pallas_call(
        body, name=name, grid=(t // tm,), in_specs=[_rows(tm, d)] * 2,
        out_specs=[_rows(tm, d), _const((V7X_SUBLANES, d))],
        out_shape=[jax.ShapeDtypeStruct((t, d), F32), jax.ShapeDtypeStruct((V7X_SUBLANES, d), F32)],
        compiler_params=_params(1),
    )(y, target)


def _conv_fwd(proj, conv_w, conv_b, ln_g, ln_b, seq, name):
    t, c3 = proj.shape
    c = c3 // 3
    ts = _pick(seq, 256, HALO)
    nsb = seq // ts
    cc = _pick(c, 512, V7X_LANES)
    hb = ts // HALO

    def body(a_ref, b_ref, z_ref, ap_ref, bp_ref, w_ref, cb_ref, g_ref, be_ref, m_ref, y_ref, win_ref):
        i = pl.program_id(0)
        first = (i % nsb) == 0
        win_ref[0:HALO, :] = jnp.where(first, 0.0, ap_ref[...] * _sig(bp_ref[...]))
        win_ref[HALO:, :] = a_ref[...] * _sig(b_ref[...])
        for ci in range(c // cc):
            cs = slice(ci * cc, (ci + 1) * cc)
            acc = jnp.zeros((ts, cc), F32) + cb_ref[:, cs]
            for k in range(CONV_WIDTH):
                acc = acc + w_ref[k:k + 1, cs] * win_ref[pl.ds(HALO - (CONV_WIDTH - 1) + k, ts), cs]
            y_ref[:, cs] = acc
        y = y_ref[...]
        mu = jnp.mean(y, axis=-1, keepdims=True)
        xc = y - mu
        rstd = lax.rsqrt(jnp.mean(xc * xc, axis=-1, keepdims=True) + EPS)
        ln = xc * rstd * g_ref[...] + be_ref[...]
        zz = z_ref[...]
        m_ref[...] = (ln * _sig(ln) * zz * _sig(zz)).astype(BF16)

    halo_a = pl.BlockSpec((HALO, c), lambda i: (jnp.maximum(i * hb - 1, 0), 0))
    halo_b = pl.BlockSpec((HALO, c), lambda i: (jnp.maximum(i * hb - 1, 0), 1))
    return pl.pallas_call(
        body, name=name, grid=(t // ts,),
        in_specs=[_rows(ts, c, 0), _rows(ts, c, 1), _rows(ts, c, 2), halo_a, halo_b,
                  _const((CONV_WIDTH, c)), _const((1, c)), _const((1, c)), _const((1, c))],
        out_specs=[_rows(ts, c), _rows(ts, c)],
        out_shape=[jax.ShapeDtypeStruct((t, c), BF16), jax.ShapeDtypeStruct((t, c), F32)],
        scratch_shapes=[pltpu.VMEM((HALO + ts, c), F32)],
        compiler_params=_params(1),
    )(proj, proj, proj, proj, proj, conv_w, conv_b, ln_g, ln_b)


def _ln_gate_bwd(dm, y, proj, ln_g, ln_b, name):
    t, c = y.shape
    tm = _pick(t, 256, 8)

    def body(dm_ref, y_ref, z_ref, g_ref, be_ref, dy_ref, dz_ref, dg_ref, db_ref, dcb_ref):
        i = pl.program_id(0)
        yv = y_ref[...]
        mu = jnp.mean(yv, axis=-1, keepdims=True)
        xc = yv - mu
        rstd = lax.rsqrt(jnp.mean(xc * xc, axis=-1, keepdims=True) + EPS)
        xhat = xc * rstd
        g = g_ref[...]
        ln = xhat * g + be_ref[...]
        sl = _sig(ln)
        zz = z_ref[...]
        sz = _sig(zz)
        dmv = dm_ref[...]
        dz_ref[...] = (dmv * (ln * sl) * (sz * (1.0 + zz * (1.0 - sz)))).astype(BF16)
        dln = dmv * (zz * sz) * (sl * (1.0 + ln * (1.0 - sl)))
        dxh = dln * g
        dyv = rstd * (dxh - jnp.mean(dxh, axis=-1, keepdims=True)
                      - xhat * jnp.mean(dxh * xhat, axis=-1, keepdims=True))
        dy_ref[...] = dyv
        parts = (_colsum8(dln * xhat), _colsum8(dln), _colsum8(dyv))

        @pl.when(i == 0)
        def _():
            for ref, part in zip((dg_ref, db_ref, dcb_ref), parts):
                ref[...] = part

        @pl.when(i > 0)
        def _():
            for ref, part in zip((dg_ref, db_ref, dcb_ref), parts):
                ref[...] += part

    acc = jax.ShapeDtypeStruct((V7X_SUBLANES, c), F32)
    outs = pl.pallas_call(
        body, name=name, grid=(t // tm,),
        in_specs=[_rows(tm, c), _rows(tm, c), _rows(tm, c, 2), _const((1, c)), _const((1, c))],
        out_specs=[_rows(tm, c), _rows(tm, c)] + [_const((V7X_SUBLANES, c))] * 3,
        out_shape=[jax.ShapeDtypeStruct((t, c), F32), jax.ShapeDtypeStruct((t, c), BF16), acc, acc, acc],
        compiler_params=_params(1),
    )(dm, y, proj, ln_g, ln_b)
    return outs[0], outs[1], outs[2].sum(axis=0), outs[3].sum(axis=0), outs[4].sum(axis=0)


def _conv_bwd(dy, dz, proj, conv_w, seq, name):
    t, c3 = proj.shape
    c = c3 // 3
    ts = _pick(seq, 256, HALO)
    nsb = seq // ts
    cc = _pick(c, 512, V7X_LANES)
    hb = ts // HALO
    last_halo = t // HALO - 1
    back = CONV_WIDTH - 1

    def body(dy_ref, dyn_ref, dz_ref, a_ref, b_ref, ap_ref, bp_ref, w_ref, o_ref, dw_ref, win_ref, dwin_ref):
        i = pl.program_id(0)
        first = (i % nsb) == 0
        last = (i % nsb) == nsb - 1
        sb = _sig(b_ref[...])
        av = a_ref[...]
        win_ref[0:HALO, :] = jnp.where(first, 0.0, ap_ref[...] * _sig(bp_ref[...]))
        win_ref[HALO:, :] = av * sb
        dwin_ref[0:ts, :] = dy_ref[...]
        dwin_ref[ts:, :] = jnp.where(last, 0.0, dyn_ref[...])

        @pl.when(i == 0)
        def _():
            dw_ref[...] = jnp.zeros_like(dw_ref)

        for ci in range(c // cc):
            cs = slice(ci * cc, (ci + 1) * cc)
            dcur = dwin_ref[0:ts, cs]
            dglu = jnp.zeros((ts, cc), F32)
            for k in range(CONV_WIDTH):
                dglu = dglu + w_ref[k:k + 1, cs] * dwin_ref[pl.ds(back - k, ts), cs]
                dw_ref[k * V7X_SUBLANES:(k + 1) * V7X_SUBLANES, cs] += _colsum8(
                    dcur * win_ref[pl.ds(HALO - back + k, ts), cs])
            sbc = sb[:, cs]
            o_ref[:, cs] = (dglu * sbc).astype(BF16)
            o_ref[:, c + ci * cc:c + (ci + 1) * cc] = (dglu * av[:, cs] * sbc * (1.0 - sbc)).astype(BF16)
        o_ref[:, 2 * c:] = dz_ref[...]

    halo_next = pl.BlockSpec((HALO, c), lambda i: (jnp.minimum((i + 1) * hb, last_halo), 0))
    halo_a = pl.BlockSpec((HALO, c), lambda i: (jnp.maximum(i * hb - 1, 0), 0))
    halo_b = pl.BlockSpec((HALO, c), lambda i: (jnp.maximum(i * hb - 1, 0), 1))
    dproj, dw = pl.pallas_call(
        body, name=name, grid=(t // ts,),
        in_specs=[_rows(ts, c), halo_next, _rows(ts, c), _rows(ts, c, 0), _rows(ts, c, 1), halo_a, halo_b,
                  _const((CONV_WIDTH, c))],
        out_specs=[_rows(ts, c3), _const((CONV_WIDTH * V7X_SUBLANES, c))],
        out_shape=[jax.ShapeDtypeStruct((t, c3), BF16),
                   jax.ShapeDtypeStruct((CONV_WIDTH * V7X_SUBLANES, c), F32)],
        scratch_shapes=[pltpu.VMEM((HALO + ts, c), F32), pltpu.VMEM((ts + HALO, c), F32)],
        compiler_params=_params(1),
    )(dy, dy, dz, proj, proj, proj, proj, conv_w)
    return dproj, dw.reshape(CONV_WIDTH, V7X_SUBLANES, c).sum(axis=1)


def _rope_tables(seq):
    half = ROPE_DIM // 2
    inv = ROPE_THETA ** (-jnp.arange(half, dtype=F32) * (2.0 / ROPE_DIM))
    ang = jnp.arange(seq).astype(F32)[:, None] * inv[None, :]
    cos, sin = jnp.cos(ang), jnp.sin(ang)
    zeros = jnp.zeros((seq, HEAD_DIM - ROPE_DIM), F32)
    zh = jnp.zeros((seq, half), F32)
    a = jnp.concatenate([cos, cos, zeros + 1.0], axis=1)
    b = jnp.concatenate([zh, sin, zeros], axis=1)
    c = jnp.concatenate([-sin, zh, zeros], axis=1)
    rep = V7X_LANES // HEAD_DIM
    return tuple(jnp.tile(v, (1, rep)) for v in (a, b, c))


def _head_ones(d):
    head = jnp.arange(d) // HEAD_DIM
    return (head[:, None] == head[None, :]).astype(BF16)


def _rope(ch, ta, tb, tc):
    return ta * ch + tb * pltpu.roll(ch, ROPE_DIM // 2, 1) + tc * pltpu.roll(ch, V7X_LANES - ROPE_DIM // 2, 1)


def _rope_t(ch, ta, tb, tc):
    return ta * ch + pltpu.roll(tb * ch, V7X_LANES - ROPE_DIM // 2, 1) + pltpu.roll(tc * ch, ROPE_DIM // 2, 1)


def _headnorm_fwd(src, width, d, gains, tables, ones, seq, name):
    t = src.shape[0]
    tm = _pick(seq, 256, 8)
    nsb = seq // tm

    def body(x_ref, g_ref, ta_ref, tb_ref, tc_ref, e_ref, o_ref):
        ta, tb, tc = ta_ref[...], tb_ref[...], tc_ref[...]
        for ci in range(width // d):
            xv = x_ref[:, ci * d:(ci + 1) * d]
            r = lax.rsqrt(_segsum(xv * xv, e_ref) * (1.0 / HEAD_DIM) + EPS)
            xn = xv * r * g_ref[:, ci * d:(ci + 1) * d]
            for j in range(d // V7X_LANES):
                ch = xn[:, j * V7X_LANES:(j + 1) * V7X_LANES]
                lo = ci * d + j * V7X_LANES
                o_ref[:, lo:lo + V7X_LANES] = _rope(ch, ta, tb, tc).astype(BF16)

    tab = pl.BlockSpec((tm, V7X_LANES), lambda i: (i % nsb, 0))
    return pl.pallas_call(
        body, name=name, grid=(t // tm,),
        in_specs=[_rows(tm, width), _const((1, width)), tab, tab, tab, _const((d, d))],
        out_specs=_rows(tm, width), out_shape=jax.ShapeDtypeStruct((t, width), BF16),
        compiler_params=_params(1),
    )(src, gains, *tables, ones)


def _headnorm_bwd_chunk(xv, dout, g, ta, tb, tc, e_ref, d):
    pieces = [_rope_t(dout[:, j * V7X_LANES:(j + 1) * V7X_LANES], ta, tb, tc) for j in range(d // V7X_LANES)]
    dxn = pieces[0] if len(pieces) == 1 else jnp.concatenate(pieces, axis=1)
    r = lax.rsqrt(_segsum(xv * xv, e_ref) * (1.0 / HEAD_DIM) + EPS)
    xhat = xv * r
    dxh = dxn * g
    dx = r * (dxh - xhat * (_segsum(dxh * xhat, e_ref) * (1.0 / HEAD_DIM)))
    return dx, _colsum8(dxn * xhat)


def _q_gate_bwd(dqs, proj, dao, o, gains, tables, ones, seq, name):
    t, d4 = proj.shape
    d = d4 // 4
    tm = _pick(seq, 256, 8)
    nsb = seq // tm

    def body(dq0_ref, dq1_ref, dq2_ref, p_ref, dao_ref, o_ref, g_ref, ta_ref, tb_ref, tc_ref, e_ref,
             out_ref, dg_ref):
        i = pl.program_id(0)
        ta, tb, tc = ta_ref[...], tb_ref[...], tc_ref[...]
        for ci, dq_ref in enumerate((dq0_ref, dq1_ref, dq2_ref)):
            cs = slice(ci * d, (ci + 1) * d)
            dx, part = _headnorm_bwd_chunk(p_ref[:, cs], dq_ref[...], g_ref[:, cs], ta, tb, tc, e_ref, d)
            out_ref[:, cs] = dx.astype(BF16)

            @pl.when(i == 0)
            def _():
                dg_ref[:, cs] = part

            @pl.when(i > 0)
            def _():
                dg_ref[:, cs] += part

        gate = p_ref[:, 3 * d:]
        sg = _sig(gate)
        out_ref[:, 3 * d:] = (dao_ref[...] * o_ref[...] * (sg * (1.0 + gate * (1.0 - sg)))).astype(BF16)

    tab = pl.BlockSpec((tm, V7X_LANES), lambda i: (i % nsb, 0))
    dproj, dg = pl.pallas_call(
        body, name=name, grid=(t // tm,),
        in_specs=[_rows(tm, d)] * 3 + [_rows(tm, d4), _rows(tm, d), _rows(tm, d), _const((1, 3 * d)),
                                        tab, tab, tab, _const((d, d))],
        out_specs=[_rows(tm, d4), _const((V7X_SUBLANES, 3 * d))],
        out_shape=[jax.ShapeDtypeStruct((t, d4), BF16), jax.ShapeDtypeStruct((V7X_SUBLANES, 3 * d), F32)],
        compiler_params=_params(1),
    )(*dqs, proj, dao, o, gains, *tables, ones)
    return dproj, dg.sum(axis=0)


def _kv_bwd(dks, dvs, kv, gains, tables, ones, seq, name):
    t, d2 = kv.shape
    d = d2 // 2
    tm = _pick(seq, 256, 8)
    nsb = seq // tm

    def body(dk0_ref, dk1_ref, dk2_ref, dv0_ref, dv1_ref, dv2_ref, kv_ref, g_ref, ta_ref, tb_ref, tc_ref,
             e_ref, out_ref, dg_ref):
        i = pl.program_id(0)
        dk = dk0_ref[...] + dk1_ref[...] + dk2_ref[...]
        dx, part = _headnorm_bwd_chunk(kv_ref[:, 0:d], dk, g_ref[...], ta_ref[...], tb_ref[...], tc_ref[...],
                                       e_ref, d)
        out_ref[:, 0:d] = dx.astype(BF16)
        out_ref[:, d:] = (dv0_ref[...] + dv1_ref[...] + dv2_ref[...]).astype(BF16)

        @pl.when(i == 0)
        def _():
            dg_ref[...] = part

        @pl.when(i > 0)
        def _():
            dg_ref[...] += part

    tab = pl.BlockSpec((tm, V7X_LANES), lambda i: (i % nsb, 0))
    dkv, dg = pl.pallas_call(
        body, name=name, grid=(t // tm,),
        in_specs=[_rows(tm, d)] * 6 + [_rows(tm, d2), _const((1, d)), tab, tab, tab, _const((d, d))],
        out_specs=[_rows(tm, d2), _const((V7X_SUBLANES, d))],
        out_shape=[jax.ShapeDtypeStruct((t, d2), BF16), jax.ShapeDtypeStruct((V7X_SUBLANES, d), F32)],
        compiler_params=_params(1),
    )(*dks, *dvs, kv, gains, *tables, ones)
    return dkv, dg.sum(axis=0)


def _attn_masks():
    qi = lax.broadcasted_iota(jnp.int32, (SPAN, 2 * SPAN), 0)
    kj = lax.broadcasted_iota(jnp.int32, (SPAN, 2 * SPAN), 1)
    both = jnp.logical_or(jnp.logical_and(kj < SPAN, kj >= qi), jnp.logical_and(kj >= SPAN, (kj - SPAN) <= qi))
    q1 = lax.broadcasted_iota(jnp.int32, (SPAN, SPAN), 0)
    k1 = lax.broadcasted_iota(jnp.int32, (SPAN, SPAN), 1)
    head0 = lax.broadcasted_iota(jnp.int32, (SPAN, V7X_LANES), 1) < HEAD_DIM
    return k1 <= q1, both, head0


def _nt(a, b):
    return lax.dot_general(a, b, (((1,), (1,)), ((), ())), preferred_element_type=F32)


def _tn(a, b):
    return lax.dot_general(a, b, (((0,), (0,)), ((), ())), preferred_element_type=F32)


def _for_blocks(nb, block, mask_first, mask_both):
    block(0, 0, SPAN, mask_first)
    if nb > 1:
        def step(i, carry):
            block(pl.multiple_of(i * SPAN, SPAN), pl.multiple_of((i - 1) * SPAN, SPAN), 2 * SPAN, mask_both)
            return carry
        lax.fori_loop(1, nb, step, 0)


def _attn_fwd(q3, k, kv, group, bsz, seq, name):
    t, d = k.shape
    dil = DILATIONS[group]
    length = seq // dil
    nb = length // SPAN
    nhp = d // V7X_LANES
    scale = HEAD_DIM ** -0.5

    def body(q_ref, k_ref, v_ref, o_ref, l_ref):
        mask_first, mask_both, head0 = _attn_masks()

        def block(qs, ks, nk, mask):
            q = q_ref[pl.ds(qs, SPAN), :]
            kk = k_ref[pl.ds(ks, nk), :]
            vv = v_ref[pl.ds(ks, nk), :].astype(BF16)
            outs, lses = [], []
            for hm in (head0, jnp.logical_not(head0)):
                s = _nt(jnp.where(hm, q, jnp.zeros_like(q)), kk) * scale
                s = jnp.where(mask, s, NEG_INF)
                mx = jnp.max(s, axis=1, keepdims=True)
                p = jnp.exp(s - mx)
                den = jnp.sum(p, axis=1, keepdims=True)
                outs.append(jnp.dot(p.astype(BF16), vv, preferred_element_type=F32) / den)
                lses.append(jnp.broadcast_to(mx + jnp.log(den), (SPAN, V7X_LANES)))
            o_ref[pl.ds(qs, SPAN), :] = jnp.where(head0, outs[0], outs[1])
            l_ref[pl.ds(qs, SPAN), :] = jnp.where(head0, lses[0], lses[1])

        _for_blocks(nb, block, mask_first, mask_both)

    blk = (None, length, V7X_LANES)
    out_spec = pl.BlockSpec(blk, lambda b, r, h: (b, 0, r * nhp + h))
    o, lse = pl.pallas_call(
        body, name=name, grid=(bsz, dil, nhp),
        in_specs=[pl.BlockSpec(blk, lambda b, r, h: (b, 0, r * 3 * nhp + group * nhp + h)),
                  pl.BlockSpec(blk, lambda b, r, h: (b, 0, r * nhp + h)),
                  pl.BlockSpec(blk, lambda b, r, h: (b, 0, r * 2 * nhp + nhp + h))],
        out_specs=[out_spec, out_spec],
        out_shape=[jax.ShapeDtypeStruct((bsz, length, dil * d), F32)] * 2,
        compiler_params=_params(3),
    )(q3.reshape(bsz, length, dil * 3 * d), k.reshape(bsz, length, dil * d), kv.reshape(bsz, length, dil * 2 * d))
    return o.reshape(t, d), lse.reshape(t, d)


def _attn_bwd(q3, k, kv, do, o, lse, group, bsz, seq, name):
    t, d = k.shape
    dil = DILATIONS[group]
    length = seq // dil
    nb = length // SPAN
    nhp = d // V7X_LANES
    scale = HEAD_DIM ** -0.5

    def body(q_ref, k_ref, v_ref, do_ref, o_ref, l_ref, dq_ref, dk_ref, dv_ref):
        mask_first, mask_both, head0 = _attn_masks()
        dk_ref[...] = jnp.zeros_like(dk_ref)
        dv_ref[...] = jnp.zeros_like(dv_ref)

        def block(qs, ks, nk, mask):
            q = q_ref[pl.ds(qs, SPAN), :]
            kk = k_ref[pl.ds(ks, nk), :]
            vv = v_ref[pl.ds(ks, nk), :].astype(BF16)
            dov = do_ref[pl.ds(qs, SPAN), :]
            doo = dov * o_ref[pl.ds(qs, SPAN), :]
            lv = l_ref[pl.ds(qs, SPAN), :]
            dqs = []
            dkk = jnp.zeros((nk, V7X_LANES), F32)
            dvv = jnp.zeros((nk, V7X_LANES), F32)
            for hi, hm in enumerate((head0, jnp.logical_not(head0))):
                qm = jnp.where(hm, q, jnp.zeros_like(q))
                dom = jnp.where(hm, dov, 0.0).astype(BF16)
                s = jnp.where(mask, _nt(qm, kk) * scale, NEG_INF)
                p = jnp.exp(s - lv[:, hi * HEAD_DIM:hi * HEAD_DIM + 1])
                delta = jnp.sum(jnp.where(hm, doo, 0.0), axis=1, keepdims=True)
                ds = (p * (_nt(dom, vv) - delta) * scale).astype(BF16)
                dqs.append(jnp.dot(ds, kk, preferred_element_type=F32))
                dkk = dkk + _tn(ds, qm)
                dvv = dvv + _tn(p.astype(BF16), dom)
            dq_ref[pl.ds(qs, SPAN), :] = jnp.where(head0, dqs[0], dqs[1])
            dk_ref[pl.ds(ks, nk), :] += dkk
            dv_ref[pl.ds(ks, nk), :] += dvv

        _for_blocks(nb, block, mask_first, mask_both)

    blk = (None, length, V7X_LANES)
    d_spec = pl.BlockSpec(blk, lambda b, r, h: (b, 0, r * nhp + h))
    view = (bsz, length, dil * d)
    dq, dk, dv = pl.pallas_call(
        body, name=name, grid=(bsz, dil, nhp),
        in_specs=[pl.BlockSpec(blk, lambda b, r, h: (b, 0, r * 3 * nhp + group * nhp + h)),
                  d_spec,
                  pl.BlockSpec(blk, lambda b, r, h: (b, 0, r * 2 * nhp + nhp + h)),
                  d_spec, d_spec, d_spec],
        out_specs=[d_spec, d_spec, d_spec],
        out_shape=[jax.ShapeDtypeStruct(view, F32)] * 3,
        compiler_params=_params(3),
    )(q3.reshape(bsz, length, dil * 3 * d), k.reshape(view), kv.reshape(bsz, length, dil * 2 * d),
      do.reshape(view), o.reshape(view), lse.reshape(view))
    return dq.reshape(t, d), dk.reshape(t, d), dv.reshape(t, d)


def _merge_fwd(os_, ls_, proj, name):
    t, d = os_[0].shape
    tm = _pick(t, 256, 8)

    def body(o0_ref, o1_ref, o2_ref, l0_ref, l1_ref, l2_ref, g_ref, o_ref, l_ref, ao_ref):
        ls = [l0_ref[...], l1_ref[...], l2_ref[...]]
        mx = jnp.maximum(jnp.maximum(ls[0], ls[1]), ls[2])
        es = [jnp.exp(v - mx) for v in ls]
        den = es[0] + es[1] + es[2]
        ov = (es[0] * o0_ref[...] + es[1] * o1_ref[...] + es[2] * o2_ref[...]) / den
        gate = g_ref[...]
        o_ref[...] = ov
        l_ref[...] = mx + jnp.log(den)
        ao_ref[...] = (ov * gate * _sig(gate)).astype(BF16)

    return pl.pallas_call(
        body, name=name, grid=(t // tm,),
        in_specs=[_rows(tm, d)] * 6 + [_rows(tm, d, 3)],
        out_specs=[_rows(tm, d)] * 3,
        out_shape=[jax.ShapeDtypeStruct((t, d), F32), jax.ShapeDtypeStruct((t, d), F32),
                   jax.ShapeDtypeStruct((t, d), BF16)],
        compiler_params=_params(1),
    )(*os_, *ls_, proj)


def _gate_bwd(dao, proj, name):
    t, d = dao.shape
    tm = _pick(t, 512, 8)

    def body(dao_ref, g_ref, do_ref):
        gate = g_ref[...]
        do_ref[...] = dao_ref[...] * gate * _sig(gate)

    return pl.pallas_call(
        body, name=name, grid=(t // tm,), in_specs=[_rows(tm, d), _rows(tm, d, 3)], out_specs=_rows(tm, d),
        out_shape=jax.ShapeDtypeStruct((t, d), F32), compiler_params=_params(1),
    )(dao, proj)


def _mesh_position():
    x, y, c = lax.axis_index("x"), lax.axis_index("y"), lax.axis_index("c")
    return x, y, c


def _peer(x, y, c, rel):
    return (1 - x if rel & 4 else x, 1 - y if rel & 2 else y, 1 - c if rel & 1 else c)


def _all_gather(arrs, name):
    n = len(arrs)

    def body(*refs):
        ins, outs = refs[:n], refs[n:2 * n]
        send_sems, recv_sems, local_sems = refs[2 * n:]
        x, y, c = _mesh_position()
        me = 4 * x + 2 * y + c
        copies, locals_ = [], []
        for a in range(n):
            loc = pltpu.make_async_copy(ins[a], outs[a].at[me], local_sems.at[a])
            loc.start()
            locals_.append(loc)
            for rel in range(1, N_DEV):
                s = a * (N_DEV - 1) + rel - 1
                cp = pltpu.make_async_remote_copy(
                    src_ref=ins[a], dst_ref=outs[a].at[me], send_sem=send_sems.at[s], recv_sem=recv_sems.at[s],
                    device_id=_peer(x, y, c, rel), device_id_type=pl.DeviceIdType.MESH)
                cp.start()
                copies.append(cp)
        for cp in copies:
            cp.wait_recv()
        for cp in copies:
            cp.wait_send()
        for loc in locals_:
            loc.wait()

    hbm = pl.BlockSpec(memory_space=pltpu.HBM)
    return pl.pallas_call(
        body, name=name, in_specs=[hbm] * n, out_specs=[hbm] * n,
        out_shape=[jax.ShapeDtypeStruct((N_DEV,) + a.shape, a.dtype) for a in arrs],
        scratch_shapes=[pltpu.SemaphoreType.DMA((n * (N_DEV - 1),)), pltpu.SemaphoreType.DMA((n * (N_DEV - 1),)),
                        pltpu.SemaphoreType.DMA((n,))],
    )(*arrs)


def _exchange_slots(pack, name):
    def body(p_ref, o_ref, send_sems, recv_sems, local_sem):
        x, y, c = _mesh_position()
        me = 4 * x + 2 * y + c
        loc = pltpu.make_async_copy(p_ref.at[me], o_ref.at[me], local_sem)
        loc.start()
        copies = []
        for rel in range(1, N_DEV):
            px, py, pc = _peer(x, y, c, rel)
            cp = pltpu.make_async_remote_copy(
                src_ref=p_ref.at[4 * px + 2 * py + pc], dst_ref=o_ref.at[me],
                send_sem=send_sems.at[rel - 1], recv_sem=recv_sems.at[rel - 1],
                device_id=(px, py, pc), device_id_type=pl.DeviceIdType.MESH)
            cp.start()
            copies.append(cp)
        for cp in copies:
            cp.wait_recv()
        for cp in copies:
            cp.wait_send()
        loc.wait()

    hbm = pl.BlockSpec(memory_space=pltpu.HBM)
    return pl.pallas_call(
        body, name=name, in_specs=[hbm], out_specs=hbm,
        out_shape=jax.ShapeDtypeStruct(pack.shape, pack.dtype),
        scratch_shapes=[pltpu.SemaphoreType.DMA((N_DEV - 1,)), pltpu.SemaphoreType.DMA((N_DEV - 1,)),
                        pltpu.SemaphoreType.DMA],
    )(pack)


def _sum_adamw(parts, w, m, v, name):
    _, r, wd = parts.shape
    tr = _pick(r, 256, 8)
    c1 = 1.0 - ADAM_B1 ** ADAM_STEP
    c2 = 1.0 - ADAM_B2 ** ADAM_STEP

    def body(p_ref, w_ref, m_ref, v_ref, g_ref, d_ref, nm_ref, nv_ref):
        g = p_ref[0]
        for s in range(1, N_DEV):
            g = g + p_ref[s]
        nm = ADAM_B1 * m_ref[...] + (1.0 - ADAM_B1) * g
        nv = ADAM_B2 * v_ref[...] + (1.0 - ADAM_B2) * (g * g)
        g_ref[...] = g
        nm_ref[...] = nm
        nv_ref[...] = nv
        d_ref[...] = -ADAM_LR * ((nm / c1) / (jnp.sqrt(nv / c2) + ADAM_EPS) + ADAM_WD * w_ref[...])

    row = pl.BlockSpec((tr, wd), lambda i: (i, 0))
    return pl.pallas_call(
        body, name=name, grid=(r // tr,),
        in_specs=[pl.BlockSpec((N_DEV, tr, wd), lambda i: (0, i, 0)), row, row, row],
        out_specs=[row] * 4, out_shape=[jax.ShapeDtypeStruct((r, wd), F32)] * 4,
        compiler_params=_params(1),
    )(parts, w, m, v)


def _pack_rows(size, row_mult):
    rows = -(-size // PACK_LANES)
    return -(-rows // row_mult) * row_mult


def _pack(flats, row_mult, dtype):
    out = []
    for f in flats:
        size = f.shape[-1]
        rows = _pack_rows(size, row_mult)
        pad = [(0, 0)] * (f.ndim - 1) + [(0, rows * PACK_LANES - size)]
        out.append(jnp.pad(f.astype(dtype), pad).reshape(f.shape[:-1] + (rows, PACK_LANES)))
    return jnp.concatenate(out, axis=-2)


def _unpack(buf, sizes, row_mult):
    out, row = [], 0
    for size in sizes:
        rows = _pack_rows(size, row_mult)
        part = buf[..., row:row + rows, :]
        out.append(part.reshape(buf.shape[:-2] + (rows * PACK_LANES,))[..., :size])
        row += rows
    return out


def _to_slots(full, axis):
    if axis is None:
        return jnp.broadcast_to(full.reshape(1, -1), (N_DEV, full.size))
    shape = full.shape
    split = full.reshape(shape[:axis] + (N_DEV, shape[axis] // N_DEV) + shape[axis + 1:])
    return jnp.moveaxis(split, axis, 0).reshape(N_DEV, -1)


def _from_slots(slots, axis, block_shape):
    split = jnp.moveaxis(slots.reshape((N_DEV,) + tuple(block_shape)), 0, axis)
    shape = list(block_shape)
    shape[axis] *= N_DEV
    return split.reshape(shape)


def kernel(x, p, norm_g, w_in_a, conv_w, conv_b, ln_g, ln_b, w_out_a, kv_norm_g, w_kv, k_norm_g, w_in_b, q_norm_g, w_out_b, ple_norm_g, w_ple_gate, w_ple_proj, loss_target, m_norm_g, m_w_in_a, m_conv_w, m_conv_b, m_ln_g, m_ln_b, m_w_out_a, m_kv_norm_g, m_w_kv, m_k_norm_g, m_w_in_b, m_q_norm_g, m_w_out_b, m_ple_norm_g, m_w_ple_gate, m_w_ple_proj, v_norm_g, v_w_in_a, v_conv_w, v_conv_b, v_ln_g, v_ln_b, v_w_out_a, v_kv_norm_g, v_w_kv, v_k_norm_g, v_w_in_b, v_q_norm_g, v_w_out_b, v_ple_norm_g, v_w_ple_gate, v_w_ple_proj):
    weights = dict(zip(WEIGHT_NAMES, (norm_g, w_in_a, conv_w, conv_b, ln_g, ln_b, w_out_a, kv_norm_g, w_kv, k_norm_g,
                                      w_in_b, q_norm_g, w_out_b, ple_norm_g, w_ple_gate, w_ple_proj)))
    mom_m = dict(zip(WEIGHT_NAMES, (m_norm_g, m_w_in_a, m_conv_w, m_conv_b, m_ln_g, m_ln_b, m_w_out_a, m_kv_norm_g,
                                    m_w_kv, m_k_norm_g, m_w_in_b, m_q_norm_g, m_w_out_b, m_ple_norm_g, m_w_ple_gate,
                                    m_w_ple_proj)))
    mom_v = dict(zip(WEIGHT_NAMES, (v_norm_g, v_w_in_a, v_conv_w, v_conv_b, v_ln_g, v_ln_b, v_w_out_a, v_kv_norm_g,
                                    v_w_kv, v_k_norm_g, v_w_in_b, v_q_norm_g, v_w_out_b, v_ple_norm_g, v_w_ple_gate,
                                    v_w_ple_proj)))
    bsz, seq, d = x.shape
    t = bsz * seq
    heads = d // HEAD_DIM
    c = conv_w.shape[2] * N_DEV
    assert seq % (max(DILATIONS) * SPAN) == 0 and d % V7X_LANES == 0

    wq_pack = _pack([weights[n].reshape(-1) for n in MATMUL_WEIGHTS], 16, BF16)
    wv_pack = _pack([weights[n].reshape(-1) for n in VECTOR_WEIGHTS], 8, F32)
    wq_all, wv_all = _all_gather([wq_pack, wv_pack], "gather_weights")
    full = {}
    for names, buf, mult in ((MATMUL_WEIGHTS, wq_all, 16), (VECTOR_WEIGHTS, wv_all, 8)):
        for n, slots in zip(names, _unpack(buf, [weights[n].size for n in names], mult)):
            full[n] = _from_slots(slots, SHARD_AXIS[n], weights[n].shape)
    wa_in, wa_out = full['w_in_a'][0], full['w_out_a'][0]
    wkv = full['w_kv']
    wb_in, wb_out = full['w_in_b'][0], full['w_out_b'][0]
    wg, wp = full['w_ple_gate'], full['w_ple_proj']
    cw, cb, lg, lb = full['conv_w'][0], full['conv_b'], full['ln_g'], full['ln_b']

    tables = _rope_tables(seq)
    ones = _head_ones(d)
    k_gain = jnp.tile(k_norm_g, heads).reshape(1, d)
    q_gain = jnp.tile(q_norm_g[0], (1, heads)).reshape(1, 3 * d)

    x0 = x.reshape(t, d)
    p0, p1 = p[0].reshape(t, -1), p[1].reshape(t, -1)
    target = loss_target.reshape(t, d)
    g_norm0, g_norm1 = norm_g[0:1], norm_g[1:2]
    g_ple0, g_ple1 = ple_norm_g[0:1], ple_norm_g[1:2]
    g_kv = kv_norm_g.reshape(1, d)

    (u0,) = _rmsnorm_fwd(x0, [g_norm0], "norm0")
    proj_a = _matmul(u0, wa_in, 'nn', "in_a")
    m_act, y_conv = _conv_fwd(proj_a, cw, cb, lg, lb, seq, "conv_fwd")
    h0 = _matmul(m_act, wa_out, 'nn', "out_a", add=x0)
    (pg0,) = _rmsnorm_fwd(h0, [g_ple0], "ple_norm0")
    gl0 = _matmul(pg0, wg[0], 'nn', "ple_gate0")
    pp0 = _matmul(p0, wp[0], 'nn', "ple_proj0")
    x1 = _ple_fwd(h0, gl0, pp0, "ple0")

    kvn, u1 = _rmsnorm_fwd(x1, [g_kv, g_norm1], "norm1")
    kv = _matmul(kvn, wkv, 'nn', "kv")
    k_rot = _headnorm_fwd(kv, d, d, k_gain, tables, ones, seq, "k_norm_rope")
    proj_b = _matmul(u1, wb_in, 'nn', "in_b")
    q_rot = _headnorm_fwd(proj_b, 3 * d, d, q_gain, tables, ones, seq, "q_norm_rope")
    o_g, l_g = [], []
    for g in range(len(DILATIONS)):
        o_i, l_i = _attn_fwd(q_rot, k_rot, kv, g, bsz, seq, "attn_fwd%d" % g)
        o_g.append(o_i)
        l_g.append(l_i)
    o_att, lse, ao = _merge_fwd(o_g, l_g, proj_b, "merge")
    h1 = _matmul(ao, wb_out, 'nn', "out_b", add=x1)
    (pg1,) = _rmsnorm_fwd(h1, [g_ple1], "ple_norm1")
    gl1 = _matmul(pg1, wg[1], 'nn', "ple_gate1")
    pp1 = _matmul(p1, wp[1], 'nn', "ple_proj1")
    x2 = _ple_fwd(h1, gl1, pp1, "ple1")

    dx2, loss_part = _loss_fwd_bwd(x2, target, "loss")
    loss = lax.psum(jnp.sum(loss_part), ("x", "y", "c"))

    grads = {}

    dgl1, dpp1 = _ple_bwd(dx2, gl1, pp1, "ple1_bwd")
    dwp1 = _matmul(p1, dpp1, 'tn', "d_ple_proj1")
    dwg1 = _matmul(pg1, dgl1, 'tn', "d_ple_gate1")
    dpg1 = _matmul(dgl1, wg[1], 'nt', "d_ple_norm1")
    dh1, (dg_ple1,) = _rmsnorm_bwd(h1, [g_ple1], [dpg1], dx2, "ple_norm1_bwd")
    grads['w_out_b'] = _matmul(ao, dh1, 'tn', "d_out_b")[None]
    dao = _matmul(dh1, wb_out, 'nt', "d_ao")
    do_att = _gate_bwd(dao, proj_b, "gate_bwd")
    dqs, dks, dvs = [], [], []
    for g in range(len(DILATIONS)):
        dq_i, dk_i, dv_i = _attn_bwd(q_rot, k_rot, kv, do_att, o_att, lse, g, bsz, seq, "attn_bwd%d" % g)
        dqs.append(dq_i)
        dks.append(dk_i)
        dvs.append(dv_i)
    dproj_b, dg_q = _q_gate_bwd(dqs, proj_b, dao, o_att, q_gain, tables, ones, seq, "q_gate_bwd")
    dkv, dg_k = _kv_bwd(dks, dvs, kv, k_gain, tables, ones, seq, "kv_bwd")
    grads['w_in_b'] = _matmul(u1, dproj_b, 'tn', "d_in_b")[None]
    du1 = _matmul(dproj_b, wb_in, 'nt', "d_u1")
    grads['w_kv'] = _matmul(kvn, dkv, 'tn', "d_kv")
    dkvn = _matmul(dkv, wkv, 'nt', "d_kvn")
    dx1, (dg_kv, dg_norm1) = _rmsnorm_bwd(x1, [g_kv, g_norm1], [dkvn, du1], dh1, "norm1_bwd")

    dgl0, dpp0 = _ple_bwd(dx1, gl0, pp0, "ple0_bwd")
    dwp0 = _matmul(p0, dpp0, 'tn', "d_ple_proj0")
    dwg0 = _matmul(pg0, dgl0, 'tn', "d_ple_gate0")
    dpg0 = _matmul(dgl0, wg[0], 'nt', "d_ple_norm0")
    dh0, (dg_ple0,) = _rmsnorm_bwd(h0, [g_ple0], [dpg0], dx1, "ple_norm0_bwd")
    grads['w_out_a'] = _matmul(m_act, dh0, 'tn', "d_out_a")[None]
    dm = _matmul(dh0, wa_out, 'nt', "d_m")
    dy_conv, dz, d_lg, d_lb, d_cb = _ln_gate_bwd(dm, y_conv, proj_a, lg, lb, "ln_gate_bwd")
    dproj_a, d_cw = _conv_bwd(dy_conv, dz, proj_a, cw, seq, "conv_bwd")
    grads['w_in_a'] = _matmul(u0, dproj_a, 'tn', "d_in_a")[None]
    du0 = _matmul(dproj_a, wa_in, 'nt', "d_u0")
    dx0, (dg_norm0,) = _rmsnorm_bwd(x0, [g_norm0], [du0], dh0, "norm0_bwd")

    grads['norm_g'] = jnp.stack([dg_norm0, dg_norm1])
    grads['conv_w'] = d_cw[None]
    grads['conv_b'] = d_cb[None]
    grads['ln_g'] = d_lg[None]
    grads['ln_b'] = d_lb[None]
    grads['kv_norm_g'] = dg_kv
    grads['k_norm_g'] = dg_k.reshape(heads, HEAD_DIM).sum(axis=0)
    grads['q_norm_g'] = dg_q.reshape(3, heads, HEAD_DIM).sum(axis=1)[None]
    grads['ple_norm_g'] = jnp.stack([dg_ple0, dg_ple1])
    grads['w_ple_gate'] = jnp.stack([dwg0, dwg1])
    grads['w_ple_proj'] = jnp.stack([dwp0, dwp1])

    sizes = [weights[n].size for n in WEIGHT_NAMES]
    g_pack = _pack([_to_slots(grads[n], SHARD_AXIS[n]) for n in WEIGHT_NAMES], 8, F32)
    parts = _exchange_slots(g_pack, "exchange_grads")
    w_pack = _pack([weights[n].reshape(-1) for n in WEIGHT_NAMES], 8, F32)
    m_pack = _pack([mom_m[n].reshape(-1) for n in WEIGHT_NAMES], 8, F32)
    v_pack = _pack([mom_v[n].reshape(-1) for n in WEIGHT_NAMES], 8, F32)
    outs = _sum_adamw(parts, w_pack, m_pack, v_pack, "sum_adamw")
    result = [loss, dx0.reshape(bsz, seq, d)]
    for buf in outs:
        for n, flat in zip(WEIGHT_NAMES, _unpack(buf, sizes, 8)):
            result.append(flat.reshape(weights[n].shape))
    return tuple(result)
```

```python
import jax
import jax.numpy as jnp
from jax import lax
from jax.experimental import pallas as pl
from jax.experimental.pallas import tpu as pltpu

F32 = jnp.float32
BF16 = jnp.bfloat16

N_DEV = 8
HEAD_DIM = 64
ROPE_DIM = 16
ROPE_THETA = 500000.0
EPS = 1e-6
NEG_INF = -1e30
SPAN = 128
DILATIONS = (1, 4, 16)
CONV_WIDTH = 31
HALO = 32
PACK_LANES = 1024
V7X_LANES = 128
V7X_SUBLANES = 8
VMEM_LIMIT_BYTES = 56 * 1024 * 1024

ADAM_LR = 0.001
ADAM_B1 = 0.9
ADAM_B2 = 0.999
ADAM_EPS = 1e-08
ADAM_WD = 0.01
ADAM_STEP = 10
ADAM_ROWS = 128

WEIGHT_NAMES = ('norm_g', 'w_in_a', 'conv_w', 'conv_b', 'ln_g', 'ln_b', 'w_out_a', 'kv_norm_g', 'w_kv',
                'k_norm_g', 'w_in_b', 'q_norm_g', 'w_out_b', 'ple_norm_g', 'w_ple_gate', 'w_ple_proj')
SHARD_AXIS = {'norm_g': None, 'w_in_a': 2, 'conv_w': 2, 'conv_b': 1, 'ln_g': 1, 'ln_b': 1, 'w_out_a': 1,
              'kv_norm_g': None, 'w_kv': 1, 'k_norm_g': None, 'w_in_b': 2, 'q_norm_g': None, 'w_out_b': 1,
              'ple_norm_g': None, 'w_ple_gate': 1, 'w_ple_proj': 2}
MATMUL_WEIGHTS = ('w_in_a', 'w_out_a', 'w_kv', 'w_in_b', 'w_out_b', 'w_ple_gate', 'w_ple_proj')
VECTOR_WEIGHTS = ('conv_w', 'conv_b', 'ln_g', 'ln_b')


def _pick(n, target, mult):
    t = (min(target, n) // mult) * mult
    while t >= mult:
        if n % t == 0:
            return t
        t -= mult
    return n


def _params(n_grid):
    return pltpu.CompilerParams(dimension_semantics=("arbitrary",) * n_grid, vmem_limit_bytes=VMEM_LIMIT_BYTES)


def _sig(x):
    return 1.0 / (1.0 + jnp.exp(-x))


def _colsum8(v):
    r, w = v.shape
    return v.reshape(r // V7X_SUBLANES, V7X_SUBLANES, w).sum(axis=0)


def _rows(tm, w, col=0):
    return pl.BlockSpec((tm, w), lambda i: (i, col))


def _const(shape):
    nd = len(shape)
    return pl.BlockSpec(shape, lambda i: (0,) * nd)


def _segsum(v, e_ref):
    hi = v.astype(BF16)
    lo = (v - hi.astype(F32)).astype(BF16)
    e = e_ref[...]
    return jnp.dot(hi, e, preferred_element_type=F32) + jnp.dot(lo, e, preferred_element_type=F32)


def _matmul(a, b, mode, name, out_dtype=F32, add=None):
    if mode == 'nn':
        (m, k), (_, n) = a.shape, b.shape
    elif mode == 'nt':
        (m, k), (n, _) = a.shape, b.shape
    else:
        (k, m), (_, n) = a.shape, b.shape
    if mode == 'tn':
        tm, tn, tk = _pick(m, 1024, 128), _pick(n, 1024, 128), _pick(k, 512, 128)
        grid = (m // tm, n // tn, k // tk)
        a_spec = pl.BlockSpec((tk, tm), lambda i, j, kk: (kk, i))
        b_spec = pl.BlockSpec((tk, tn), lambda i, j, kk: (kk, j))
        o_spec = pl.BlockSpec((tm, tn), lambda i, j, kk: (i, j))
        dims = (((0,), (0,)), ((), ()))
    else:
        tm, tn, tk = _pick(m, 512, 128), _pick(n, 1024, 128), _pick(k, 1024, 128)
        grid = (n // tn, m // tm, k // tk)
        a_spec = pl.BlockSpec((tm, tk), lambda j, i, kk: (i, kk))
        o_spec = pl.BlockSpec((tm, tn), lambda j, i, kk: (i, j))
        if mode == 'nn':
            b_spec = pl.BlockSpec((tk, tn), lambda j, i, kk: (kk, j))
            dims = (((1,), (0,)), ((), ()))
        else:
            b_spec = pl.BlockSpec((tn, tk), lambda j, i, kk: (j, kk))
            dims = (((1,), (1,)), ((), ()))
    nk = grid[2]
    has_add = add is not None

    def body(*refs):
        a_ref, b_ref = refs[0], refs[1]
        add_ref = refs[2] if has_add else None
        o_ref = refs[2 + has_add]
        part = lax.dot_general(a_ref[...].astype(BF16), b_ref[...].astype(BF16), dims, preferred_element_type=F32)

        def finish(total):
            if has_add:
                total = total + add_ref[...]
            o_ref[...] = total.astype(out_dtype)

        if nk == 1:
            finish(part)
        else:
            acc_ref = refs[3 + has_add]
            kk = pl.program_id(2)

            @pl.when(kk == 0)
            def _():
                acc_ref[...] = part

            @pl.when(kk > 0)
            def _():
                acc_ref[...] += part

            @pl.when(kk == nk - 1)
            def _():
                finish(acc_ref[...])

    in_specs = [a_spec, b_spec] + ([o_spec] if has_add else [])
    args = [a, b] + ([add] if has_add else [])
    scratch = [pltpu.VMEM((tm, tn), F32)] if nk > 1 else []
    return pl.pallas_call(
        body, name=name, grid=grid, in_specs=in_specs, out_specs=o_spec,
        out_shape=jax.ShapeDtypeStruct((m, n), out_dtype), scratch_shapes=scratch,
        compiler_params=_params(3),
    )(*args)


def _rmsnorm_fwd(x, gains, name):
    t, d = x.shape
    tm = _pick(t, 512, 8)
    n = len(gains)

    def body(*refs):
        x_ref, g_refs, o_refs = refs[0], refs[1:1 + n], refs[1 + n:]
        xv = x_ref[...]
        y = xv * lax.rsqrt(jnp.mean(xv * xv, axis=-1, keepdims=True) + EPS)
        for g_ref, o_ref in zip(g_refs, o_refs):
            o_ref[...] = (y * g_ref[...]).astype(BF16)

    return pl.pallas_call(
        body, name=name, grid=(t // tm,),
        in_specs=[_rows(tm, d)] + [_const((1, d))] * n,
        out_specs=[_rows(tm, d)] * n,
        out_shape=[jax.ShapeDtypeStruct((t, d), BF16)] * n,
        compiler_params=_params(1),
    )(x, *gains)


def _rmsnorm_bwd(x, gains, dys, add, name):
    t, d = x.shape
    tm = _pick(t, 512, 8)
    n = len(gains)

    def body(*refs):
        x_ref, add_ref = refs[0], refs[1]
        g_refs, dy_refs = refs[2:2 + n], refs[2 + n:2 + 2 * n]
        dx_ref, dg_refs = refs[2 + 2 * n], refs[3 + 2 * n:]
        i = pl.program_id(0)
        xv = x_ref[...]
        r = lax.rsqrt(jnp.mean(xv * xv, axis=-1, keepdims=True) + EPS)
        xhat = xv * r
        dx = add_ref[...]
        for g_ref, dy_ref, dg_ref in zip(g_refs, dy_refs, dg_refs):
            dy = dy_ref[...]
            dyg = dy * g_ref[...]
            dx = dx + r * (dyg - xhat * jnp.mean(dyg * xhat, axis=-1, keepdims=True))
            part = _colsum8(dy * xhat)

            @pl.when(i == 0)
            def _():
                dg_ref[...] = part

            @pl.when(i > 0)
            def _():
                dg_ref[...] += part

        dx_ref[...] = dx

    outs = pl.pallas_call(
        body, name=name, grid=(t // tm,),
        in_specs=[_rows(tm, d), _rows(tm, d)] + [_const((1, d))] * n + [_rows(tm, d)] * n,
        out_specs=[_rows(tm, d)] + [_const((V7X_SUBLANES, d))] * n,
        out_shape=[jax.ShapeDtypeStruct((t, d), F32)] + [jax.ShapeDtypeStruct((V7X_SUBLANES, d), F32)] * n,
        compiler_params=_params(1),
    )(x, add, *gains, *dys)
    return outs[0], [o.sum(axis=0) for o in outs[1:]]


def _ple_fwd(h, gl, pp, name):
    t, d = h.shape
    tm = _pick(t, 512, 8)

    def body(h_ref, gl_ref, pp_ref, o_ref):
        o_ref[...] = h_ref[...] + _sig(gl_ref[...]) * pp_ref[...]

    return pl.pallas_call(
        body, name=name, grid=(t // tm,), in_specs=[_rows(tm, d)] * 3, out_specs=_rows(tm, d),
        out_shape=jax.ShapeDtypeStruct((t, d), F32), compiler_params=_params(1),
    )(h, gl, pp)


def _ple_bwd(dx, gl, pp, name):
    t, d = dx.shape
    tm = _pick(t, 512, 8)

    def body(dx_ref, gl_ref, pp_ref, dgl_ref, dpp_ref):
        dxv = dx_ref[...]
        sg = _sig(gl_ref[...])
        dpp_ref[...] = (dxv * sg).astype(BF16)
        dgl_ref[...] = (dxv * pp_ref[...] * sg * (1.0 - sg)).astype(BF16)

    return pl.pallas_call(
        body, name=name, grid=(t // tm,), in_specs=[_rows(tm, d)] * 3, out_specs=[_rows(tm, d)] * 2,
        out_shape=[jax.ShapeDtypeStruct((t, d), BF16)] * 2, compiler_params=_params(1),
    )(dx, gl, pp)


def _loss_fwd_bwd(y, target, name):
    t, d = y.shape
    tm = _pick(t, 512, 8)
    inv_d = 1.0 / d

    def body(y_ref, t_ref, dy_ref, l_ref):
        i = pl.program_id(0)
        e = y_ref[...] - t_ref[...]
        dy_ref[...] = e * inv_d
        part = _colsum8(e * e) * (0.5 * inv_d)

        @pl.when(i == 0)
        def _():
            l_ref[...] = part

        @pl.when(i > 0)
        def _():
            l_ref[...] += part

    return pl.pallas_call(
        body, name=name, grid=(t // tm,), in_specs=[_rows(tm, d)] * 2,
        out_specs=[_rows(tm, d), _const((V7X_SUBLANES, d))],
        out_shape=[jax.ShapeDtypeStruct((t, d), F32), jax.ShapeDtypeStruct((V7X_SUBLANES, d), F32)],
        compiler_params=_params(1),
    )(y, target)


def _conv_fwd(proj, conv_w, conv_b, ln_g, ln_b, seq, name):
    t, c3 = proj.shape
    c = c3 // 3
    ts = _pick(seq, 256, HALO)
    nsb = seq // ts
    cc = _pick(c, 512, V7X_LANES)
    hb = ts // HALO

    def body(a_ref, b_ref, z_ref, ap_ref, bp_ref, w_ref, cb_ref, g_ref, be_ref, m_ref, y_ref, win_ref):
        i = pl.program_id(0)
        first = (i % nsb) == 0
        win_ref[0:HALO, :] = jnp.where(first, 0.0, ap_ref[...] * _sig(bp_ref[...]))
        win_ref[HALO:, :] = a_ref[...] * _sig(b_ref[...])
        for ci in range(c // cc):
            cs = slice(ci * cc, (ci + 1) * cc)
            acc = jnp.zeros((ts, cc), F32) + cb_ref[:, cs]
            for k in range(CONV_WIDTH):
                acc = acc + w_ref[k:k + 1, cs] * win_ref[pl.ds(HALO - (CONV_WIDTH - 1) + k, ts), cs]
            y_ref[:, cs] = acc
        y = y_ref[...]
        mu = jnp.mean(y, axis=-1, keepdims=True)
        xc = y - mu
        rstd = lax.rsqrt(jnp.mean(xc * xc, axis=-1, keepdims=True) + EPS)
        ln = xc * rstd * g_ref[...] + be_ref[...]
        zz = z_ref[...]
        m_ref[...] = (ln * _sig(ln) * zz * _sig(zz)).astype(BF16)

    halo_a = pl.BlockSpec((HALO, c), lambda i: (jnp.maximum(i * hb - 1, 0), 0))
    halo_b = pl.BlockSpec((HALO, c), lambda i: (jnp.maximum(i * hb - 1, 0), 1))
    return pl.pallas_call(
        body, name=name, grid=(t // ts,),
        in_specs=[_rows(ts, c, 0), _rows(ts, c, 1), _rows(ts, c, 2), halo_a, halo_b,
                  _const((CONV_WIDTH, c)), _const((1, c)), _const((1, c)), _const((1, c))],
        out_specs=[_rows(ts, c), _rows(ts, c)],
        out_shape=[jax.ShapeDtypeStruct((t, c), BF16), jax.ShapeDtypeStruct((t, c), F32)],
        scratch_shapes=[pltpu.VMEM((HALO + ts, c), F32)],
        compiler_params=_params(1),
    )(proj, proj, proj, proj, proj, conv_w, conv_b, ln_g, ln_b)


def _ln_gate_bwd(dm, y, proj, ln_g, ln_b, name):
    t, c = y.shape
    tm = _pick(t, 256, 8)

    def body(dm_ref, y_ref, z_ref, g_ref, be_ref, dy_ref, dz_ref, dg_ref, db_ref, dcb_ref):
        i = pl.program_id(0)
        yv = y_ref[...]
        mu = jnp.mean(yv, axis=-1, keepdims=True)
        xc = yv - mu
        rstd = lax.rsqrt(jnp.mean(xc * xc, axis=-1, keepdims=True) + EPS)
        xhat = xc * rstd
        g = g_ref[...]
        ln = xhat * g + be_ref[...]
        sl = _sig(ln)
        zz = z_ref[...]
        sz = _sig(zz)
        dmv = dm_ref[...]
        dz_ref[...] = (dmv * (ln * sl) * (sz * (1.0 + zz * (1.0 - sz)))).astype(BF16)
        dln = dmv * (zz * sz) * (sl * (1.0 + ln * (1.0 - sl)))
        dxh = dln * g
        dyv = rstd * (dxh - jnp.mean(dxh, axis=-1, keepdims=True)
                      - xhat * jnp.mean(dxh * xhat, axis=-1, keepdims=True))
        dy_ref[...] = dyv
        parts = (_colsum8(dln * xhat), _colsum8(dln), _colsum8(dyv))

        @pl.when(i == 0)
        def _():
            for ref, part in zip((dg_ref, db_ref, dcb_ref), parts):
                ref[...] = part

        @pl.when(i > 0)
        def _():
            for ref, part in zip((dg_ref, db_ref, dcb_ref), parts):
                ref[...] += part

    acc = jax.ShapeDtypeStruct((V7X_SUBLANES, c), F32)
    outs = pl.pallas_call(
        body, name=name, grid=(t // tm,),
        in_specs=[_rows(tm, c), _rows(tm, c), _rows(tm, c, 2), _const((1, c)), _const((1, c))],
        out_specs=[_rows(tm, c), _rows(tm, c)] + [_const((V7X_SUBLANES, c))] * 3,
        out_shape=[jax.ShapeDtypeStruct((t, c), F32), jax.ShapeDtypeStruct((t, c), BF16), acc, acc, acc],
        compiler_params=_params(1),
    )(dm, y, proj, ln_g, ln_b)
    return outs[0], outs[1], outs[2].sum(axis=0), outs[3].sum(axis=0), outs[4].sum(axis=0)


def _conv_bwd(dy, dz, proj, conv_w, seq, name):
    t, c3 = proj.shape
    c = c3 // 3
    ts = _pick(seq, 256, HALO)
    nsb = seq // ts
    cc = _pick(c, 512, V7X_LANES)
    hb = ts // HALO
    last_halo = t // HALO - 1
    back = CONV_WIDTH - 1

    def body(dy_ref, dyn_ref, dz_ref, a_ref, b_ref, ap_ref, bp_ref, w_ref, o_ref, dw_ref, win_ref, dwin_ref):
        i = pl.program_id(0)
        first = (i % nsb) == 0
        last = (i % nsb) == nsb - 1
        sb = _sig(b_ref[...])
        av = a_ref[...]
        win_ref[0:HALO, :] = jnp.where(first, 0.0, ap_ref[...] * _sig(bp_ref[...]))
        win_ref[HALO:, :] = av * sb
        dwin_ref[0:ts, :] = dy_ref[...]
        dwin_ref[ts:, :] = jnp.where(last, 0.0, dyn_ref[...])

        @pl.when(i == 0)
        def _():
            dw_ref[...] = jnp.zeros_like(dw_ref)

        for ci in range(c // cc):
            cs = slice(ci * cc, (ci + 1) * cc)
            dcur = dwin_ref[0:ts, cs]
            dglu = jnp.zeros((ts, cc), F32)
            for k in range(CONV_WIDTH):
                dglu = dglu + w_ref[k:k + 1, cs] * dwin_ref[pl.ds(back - k, ts), cs]
                dw_ref[k * V7X_SUBLANES:(k + 1) * V7X_SUBLANES, cs] += _colsum8(
                    dcur * win_ref[pl.ds(HALO - back + k, ts), cs])
            sbc = sb[:, cs]
            o_ref[:, cs] = (dglu * sbc).astype(BF16)
            o_ref[:, c + ci * cc:c + (ci + 1) * cc] = (dglu * av[:, cs] * sbc * (1.0 - sbc)).astype(BF16)
        o_ref[:, 2 * c:] = dz_ref[...]

    halo_next = pl.BlockSpec((HALO, c), lambda i: (jnp.minimum((i + 1) * hb, last_halo), 0))
    halo_a = pl.BlockSpec((HALO, c), lambda i: (jnp.maximum(i * hb - 1, 0), 0))
    halo_b = pl.BlockSpec((HALO, c), lambda i: (jnp.maximum(i * hb - 1, 0), 1))
    dproj, dw = pl.pallas_call(
        body, name=name, grid=(t // ts,),
        in_specs=[_rows(ts, c), halo_next, _rows(ts, c), _rows(ts, c, 0), _rows(ts, c, 1), halo_a, halo_b,
                  _const((CONV_WIDTH, c))],
        out_specs=[_rows(ts, c3), _const((CONV_WIDTH * V7X_SUBLANES, c))],
        out_shape=[jax.ShapeDtypeStruct((t, c3), BF16),
                   jax.ShapeDtypeStruct((CONV_WIDTH * V7X_SUBLANES, c), F32)],
        scratch_shapes=[pltpu.VMEM((HALO + ts, c), F32), pltpu.VMEM((ts + HALO, c), F32)],
        compiler_params=_params(1),
    )(dy, dy, dz, proj, proj, proj, proj, conv_w)
    return dproj, dw.reshape(CONV_WIDTH, V7X_SUBLANES, c).sum(axis=1)


def _rope_tables(seq):
    half = ROPE_DIM // 2
    inv = ROPE_THETA ** (-jnp.arange(half, dtype=F32) * (2.0 / ROPE_DIM))
    ang = jnp.arange(seq).astype(F32)[:, None] * inv[None, :]
    cos, sin = jnp.cos(ang), jnp.sin(ang)
    zeros = jnp.zeros((seq, HEAD_DIM - ROPE_DIM), F32)
    zh = jnp.zeros((seq, half), F32)
    a = jnp.concatenate([cos, cos, zeros + 1.0], axis=1)
    b = jnp.concatenate([zh, sin, zeros], axis=1)
    c = jnp.concatenate([-sin, zh, zeros], axis=1)
    rep = V7X_LANES // HEAD_DIM
    return tuple(jnp.tile(v, (1, rep)) for v in (a, b, c))


def _head_ones(d):
    head = jnp.arange(d) // HEAD_DIM
    return (head[:, None] == head[None, :]).astype(BF16)


def _rope(ch, ta, tb, tc):
    return ta * ch + tb * pltpu.roll(ch, ROPE_DIM // 2, 1) + tc * pltpu.roll(ch, V7X_LANES - ROPE_DIM // 2, 1)


def _rope_t(ch, ta, tb, tc):
    return ta * ch + pltpu.roll(tb * ch, V7X_LANES - ROPE_DIM // 2, 1) + pltpu.roll(tc * ch, ROPE_DIM // 2, 1)


def _norm_rope(xv, gain, ta, tb, tc, e_ref):
    r = lax.rsqrt(_segsum(xv * xv, e_ref) * (1.0 / HEAD_DIM) + EPS)
    return _rope(xv * r * gain, ta, tb, tc)


def _norm_rope_bwd(xv, dout, gain, ta, tb, tc, e_ref):
    dxn = _rope_t(dout, ta, tb, tc)
    r = lax.rsqrt(_segsum(xv * xv, e_ref) * (1.0 / HEAD_DIM) + EPS)
    xhat = xv * r
    dxh = dxn * gain
    dx = r * (dxh - xhat * (_segsum(dxh * xhat, e_ref) * (1.0 / HEAD_DIM)))
    return dx, _colsum8(dxn * xhat)


ATTN_PIECE = 256


def _pieces(dil, seq):
    length = seq // dil
    rows = min(length, ATTN_PIECE)
    return [(r + dil * ci * rows, r * length + ci * rows, rows) for r in range(dil) for ci in range(length // rows)]


def _strided(ref, start, rows, dil):
    if dil == 1:
        return ref[pl.ds(start, rows), :]
    return ref[pl.ds(start, rows, stride=dil), :]


def _strided_set(ref, start, rows, dil, val):
    if dil == 1:
        ref[pl.ds(start, rows), :] = val
    else:
        ref[pl.ds(start, rows, stride=dil), :] = val


def _nt(a, b):
    return lax.dot_general(a, b, (((1,), (1,)), ((), ())), preferred_element_type=F32)


def _tn(a, b):
    return lax.dot_general(a, b, (((0,), (0,)), ((), ())), preferred_element_type=F32)


def _set_bias(bias_ref):
    qi = lax.broadcasted_iota(jnp.int32, (2 * SPAN, 2 * SPAN), 0) & (SPAN - 1)
    kj = lax.broadcasted_iota(jnp.int32, (2 * SPAN, 2 * SPAN), 1)
    band = jnp.logical_and(kj >= qi, (kj - SPAN) <= qi)
    bias_ref[1] = jnp.where(band, 0.0, NEG_INF)
    bias_ref[0] = jnp.where(jnp.logical_and(band, kj >= SPAN), 0.0, NEG_INF)


def _stack_heads(v, head0):
    zero = jnp.zeros_like(v)
    return jnp.concatenate([jnp.where(head0, v, zero), jnp.where(head0, zero, v)], axis=0)


def _unstack_heads(v2, head0):
    return jnp.where(head0, v2[:SPAN], v2[SPAN:])


def _head_cols(v):
    return jnp.concatenate([v[:, 0:1], v[:, HEAD_DIM:HEAD_DIM + 1]], axis=0)


def _attn_fwd(proj_b, kv, gains, tables, ones, bsz, seq, name):
    t, d4 = proj_b.shape
    d = d4 // 4
    nhp = d // V7X_LANES
    nblk = seq // SPAN
    scale = HEAD_DIM ** -0.5
    n_groups = len(DILATIONS)

    def body(q0_ref, q1_ref, q2_ref, k_ref, v_ref, gate_ref, gain_ref, ta_ref, tb_ref, tc_ref, e_ref,
             o_ref, l_ref, ao_ref, qd, kd, vd, od, ld, on0, on1, on2, ln0, ln1, ln2, bias):
        head0 = lax.broadcasted_iota(jnp.int32, (SPAN, V7X_LANES), 1) < HEAD_DIM

        @pl.when(jnp.logical_and(pl.program_id(0) == 0, pl.program_id(1) == 0))
        def _():
            _set_bias(bias)

        kd[0:SPAN, :] = jnp.zeros((SPAN, V7X_LANES), BF16)
        vd[0:SPAN, :] = jnp.zeros((SPAN, V7X_LANES), BF16)
        for g, (q_ref, on, ln) in enumerate(((q0_ref, on0, ln0), (q1_ref, on1, ln1), (q2_ref, on2, ln2))):
            dil = DILATIONS[g]
            nb = seq // dil // SPAN
            for ns, rs, rows in _pieces(dil, seq):
                ta, tb, tc = (_strided(r_, ns, rows, dil) for r_ in (ta_ref, tb_ref, tc_ref))
                qd[rs:rs + rows, :] = _norm_rope(_strided(q_ref, ns, rows, dil), gain_ref[g:g + 1, :],
                                                 ta, tb, tc, e_ref).astype(BF16)
                kd[SPAN + rs:SPAN + rs + rows, :] = _norm_rope(_strided(k_ref, ns, rows, dil),
                                                                gain_ref[n_groups:n_groups + 1, :],
                                                                ta, tb, tc, e_ref).astype(BF16)
                vd[SPAN + rs:SPAN + rs + rows, :] = _strided(v_ref, ns, rows, dil).astype(BF16)

            def block(j, carry):
                qs = pl.multiple_of(j * SPAN, SPAN)
                q2 = _stack_heads(qd[pl.ds(qs, SPAN), :], head0)
                kk = kd[pl.ds(qs, 2 * SPAN), :]
                vv = vd[pl.ds(qs, 2 * SPAN), :]
                s = _nt(q2, kk) * scale + bias[jnp.minimum(j & (nb - 1), 1)]
                mx = jnp.max(s, axis=1, keepdims=True)
                p = jnp.exp(s - mx)
                den = jnp.sum(p, axis=1, keepdims=True)
                o2 = jnp.dot(p.astype(BF16), vv, preferred_element_type=F32) / den
                l2 = jnp.broadcast_to(mx + jnp.log(den), (2 * SPAN, V7X_LANES))
                od[pl.ds(qs, SPAN), :] = _unstack_heads(o2, head0)
                ld[pl.ds(qs, SPAN), :] = _unstack_heads(l2, head0)
                return carry

            lax.fori_loop(0, nblk, block, 0)
            for ns, rs, rows in _pieces(dil, seq):
                _strided_set(on, ns, rows, dil, od[rs:rs + rows, :])
                _strided_set(ln, ns, rows, dil, ld[rs:rs + rows, :])

        def merge(ci, carry):
            rows = pl.ds(pl.multiple_of(ci * ATTN_PIECE, ATTN_PIECE), ATTN_PIECE)
            ls = [ln0[rows, :], ln1[rows, :], ln2[rows, :]]
            mx = jnp.maximum(jnp.maximum(ls[0], ls[1]), ls[2])
            es = [jnp.exp(v - mx) for v in ls]
            den = es[0] + es[1] + es[2]
            ov = (es[0] * on0[rows, :] + es[1] * on1[rows, :] + es[2] * on2[rows, :]) / den
            gate = gate_ref[rows, :]
            o_ref[rows, :] = ov
            l_ref[rows, :] = mx + jnp.log(den)
            ao_ref[rows, :] = (ov * gate * _sig(gate)).astype(BF16)
            return carry

        lax.fori_loop(0, seq // ATTN_PIECE, merge, 0)

    blk = (None, seq, V7X_LANES)
    pview = proj_b.reshape(bsz, seq, d4)
    kview = kv.reshape(bsz, seq, 2 * d)
    out_spec = pl.BlockSpec(blk, lambda b, h: (b, 0, h))
    tab = pl.BlockSpec((seq, V7X_LANES), lambda b, h: (0, 0))
    nat = pltpu.VMEM((seq, V7X_LANES), F32)
    o, lse, ao = pl.pallas_call(
        body, name=name, grid=(bsz, nhp),
        in_specs=[pl.BlockSpec(blk, lambda b, h: (b, 0, h)),
                  pl.BlockSpec(blk, lambda b, h: (b, 0, nhp + h)),
                  pl.BlockSpec(blk, lambda b, h: (b, 0, 2 * nhp + h)),
                  pl.BlockSpec(blk, lambda b, h: (b, 0, h)),
                  pl.BlockSpec(blk, lambda b, h: (b, 0, nhp + h)),
                  pl.BlockSpec(blk, lambda b, h: (b, 0, 3 * nhp + h)),
                  pl.BlockSpec((n_groups + 1, V7X_LANES), lambda b, h: (0, 0)),
                  tab, tab, tab,
                  pl.BlockSpec((V7X_LANES, V7X_LANES), lambda b, h: (0, 0))],
        out_specs=[out_spec, out_spec, out_spec],
        out_shape=[jax.ShapeDtypeStruct((bsz, seq, d), F32), jax.ShapeDtypeStruct((bsz, seq, d), F32),
                   jax.ShapeDtypeStruct((bsz, seq, d), BF16)],
        scratch_shapes=[pltpu.VMEM((seq, V7X_LANES), BF16), pltpu.VMEM((SPAN + seq, V7X_LANES), BF16),
                        pltpu.VMEM((SPAN + seq, V7X_LANES), BF16), nat, nat, nat, nat, nat, nat, nat, nat,
                        pltpu.VMEM((2, 2 * SPAN, 2 * SPAN), F32)],
        compiler_params=_params(2),
    )(pview, pview, pview, kview, kview, pview, gains, *tables, ones)
    return o.reshape(t, d), lse.reshape(t, d), ao.reshape(t, d)


def _attn_bwd(proj_b, kv, dao, o, lse, gains, tables, ones, bsz, seq, name):
    t, d4 = proj_b.shape
    d = d4 // 4
    nhp = d // V7X_LANES
    nblk = seq // SPAN
    scale = HEAD_DIM ** -0.5
    n_groups = len(DILATIONS)
    n_chunks = seq // ATTN_PIECE

    def body(q_ref, k_ref, v_ref, gate_ref, dao_ref, o_ref, l_ref, gain_ref, ta_ref, tb_ref, tc_ref, e_ref,
             dproj_ref, dkv_ref, dg_ref, qd, kd, vd, dod, ld, deld, dqd, dkd, dvd, dqn, dkn, dvn, bias):
        head0 = lax.broadcasted_iota(jnp.int32, (SPAN, V7X_LANES), 1) < HEAD_DIM
        g = pl.program_id(2)

        @pl.when(jnp.logical_and(jnp.logical_and(pl.program_id(0) == 0, pl.program_id(1) == 0), g == 0))
        def _():
            _set_bias(bias)
            dg_ref[...] = jnp.zeros_like(dg_ref)

        @pl.when(g == 0)
        def _():
            dkn[...] = jnp.zeros_like(dkn)
            dvn[...] = jnp.zeros_like(dvn)

        def norm_bwd_chunks(x_ref, dn_ref, out_ref, gi):
            def chunk(ci, carry):
                rows = pl.ds(pl.multiple_of(ci * ATTN_PIECE, ATTN_PIECE), ATTN_PIECE)
                dx, part = _norm_rope_bwd(x_ref[rows, :], dn_ref[rows, :], gain_ref[gi:gi + 1, :],
                                          ta_ref[rows, :], tb_ref[rows, :], tc_ref[rows, :], e_ref)
                out_ref[rows, :] = dx.astype(BF16)
                dg_ref[gi] += part
                return carry
            lax.fori_loop(0, n_chunks, chunk, 0)

        def group(gi):
            dil = DILATIONS[gi]
            nb = seq // dil // SPAN
            kd[0:SPAN, :] = jnp.zeros((SPAN, V7X_LANES), BF16)
            vd[0:SPAN, :] = jnp.zeros((SPAN, V7X_LANES), BF16)
            dkd[...] = jnp.zeros_like(dkd)
            dvd[...] = jnp.zeros_like(dvd)
            for ns, rs, rows in _pieces(dil, seq):
                ta, tb, tc = (_strided(r_, ns, rows, dil) for r_ in (ta_ref, tb_ref, tc_ref))
                qd[rs:rs + rows, :] = _norm_rope(_strided(q_ref, ns, rows, dil), gain_ref[gi:gi + 1, :],
                                                 ta, tb, tc, e_ref).astype(BF16)
                kd[SPAN + rs:SPAN + rs + rows, :] = _norm_rope(_strided(k_ref, ns, rows, dil),
                                                                gain_ref[n_groups:n_groups + 1, :],
                                                                ta, tb, tc, e_ref).astype(BF16)
                vd[SPAN + rs:SPAN + rs + rows, :] = _strided(v_ref, ns, rows, dil).astype(BF16)
                gate = _strided(gate_ref, ns, rows, dil)
                dov = _strided(dao_ref, ns, rows, dil) * gate * _sig(gate)
                dod[rs:rs + rows, :] = dov.astype(BF16)
                deld[rs:rs + rows, :] = _segsum(dov * _strided(o_ref, ns, rows, dil), e_ref)
                ld[rs:rs + rows, :] = _strided(l_ref, ns, rows, dil)

            def block(j, carry):
                qs = pl.multiple_of(j * SPAN, SPAN)
                q2 = _stack_heads(qd[pl.ds(qs, SPAN), :], head0)
                do2 = _stack_heads(dod[pl.ds(qs, SPAN), :], head0)
                kk = kd[pl.ds(qs, 2 * SPAN), :]
                vv = vd[pl.ds(qs, 2 * SPAN), :]
                s = _nt(q2, kk) * scale + bias[jnp.minimum(j & (nb - 1), 1)]
                p = jnp.exp(s - _head_cols(ld[pl.ds(qs, SPAN), :]))
                ds = (p * (_nt(do2, vv) - _head_cols(deld[pl.ds(qs, SPAN), :])) * scale).astype(BF16)
                dqd[pl.ds(qs, SPAN), :] = _unstack_heads(jnp.dot(ds, kk, preferred_element_type=F32), head0)
                dkd[pl.ds(qs, 2 * SPAN), :] += _tn(ds, q2)
                dvd[pl.ds(qs, 2 * SPAN), :] += _tn(p.astype(BF16), do2)
                return carry

            lax.fori_loop(0, nblk, block, 0)
            for ns, rs, rows in _pieces(dil, seq):
                _strided_set(dqn, ns, rows, dil, dqd[rs:rs + rows, :])
                _strided_set(dkn, ns, rows, dil,
                             _strided(dkn, ns, rows, dil) + dkd[SPAN + rs:SPAN + rs + rows, :])
                _strided_set(dvn, ns, rows, dil,
                             _strided(dvn, ns, rows, dil) + dvd[SPAN + rs:SPAN + rs + rows, :])
            norm_bwd_chunks(q_ref, dqn, dproj_ref, gi)

        for gi in range(n_groups):
            @pl.when(g == gi)
            def _():
                group(gi)

        @pl.when(g == n_groups - 1)
        def _():
            norm_bwd_chunks(k_ref, dkn, dkv_ref, n_groups)

        @pl.when(g == n_groups)
        def _():
            def chunk(ci, carry):
                rows = pl.ds(pl.multiple_of(ci * ATTN_PIECE, ATTN_PIECE), ATTN_PIECE)
                gate = gate_ref[rows, :]
                sg = _sig(gate)
                dproj_ref[rows, :] = (dao_ref[rows, :] * o_ref[rows, :]
                                      * (sg * (1.0 + gate * (1.0 - sg)))).astype(BF16)
                dkv_ref[rows, :] = dvn[rows, :].astype(BF16)
                return carry
            lax.fori_loop(0, n_chunks, chunk, 0)

    blk = (None, seq, V7X_LANES)
    pview = proj_b.reshape(bsz, seq, d4)
    kview = kv.reshape(bsz, seq, 2 * d)
    dview = (bsz, seq, d)
    d_spec = pl.BlockSpec(blk, lambda b, h, g: (b, 0, h))
    tab = pl.BlockSpec((seq, V7X_LANES), lambda b, h, g: (0, 0))
    nat = pltpu.VMEM((seq, V7X_LANES), F32)
    natb = pltpu.VMEM((seq, V7X_LANES), BF16)
    pad = pltpu.VMEM((SPAN + seq, V7X_LANES), F32)
    padb = pltpu.VMEM((SPAN + seq, V7X_LANES), BF16)
    dproj, dkv, dg = pl.pallas_call(
        body, name=name, grid=(bsz, nhp, n_groups + 1),
        in_specs=[pl.BlockSpec(blk, lambda b, h, g: (b, 0, jnp.minimum(g, n_groups - 1) * nhp + h)),
                  pl.BlockSpec(blk, lambda b, h, g: (b, 0, h)),
                  pl.BlockSpec(blk, lambda b, h, g: (b, 0, nhp + h)),
                  pl.BlockSpec(blk, lambda b, h, g: (b, 0, n_groups * nhp + h)),
                  d_spec, d_spec, d_spec,
                  pl.BlockSpec((n_groups + 1, V7X_LANES), lambda b, h, g: (0, 0)),
                  tab, tab, tab,
                  pl.BlockSpec((V7X_LANES, V7X_LANES), lambda b, h, g: (0, 0))],
        out_specs=[pl.BlockSpec(blk, lambda b, h, g: (b, 0, g * nhp + h)),
                   pl.BlockSpec(blk, lambda b, h, g: (b, 0, (g // n_groups) * nhp + h)),
                   pl.BlockSpec((n_groups + 1, V7X_SUBLANES, V7X_LANES), lambda b, h, g: (0, 0, 0))],
        out_shape=[jax.ShapeDtypeStruct((bsz, seq, d4), BF16), jax.ShapeDtypeStruct((bsz, seq, 2 * d), BF16),
                   jax.ShapeDtypeStruct((n_groups + 1, V7X_SUBLANES, V7X_LANES), F32)],
        scratch_shapes=[natb, padb, padb, natb, nat, nat, nat, pad, pad, nat, nat, nat,
                        pltpu.VMEM((2, 2 * SPAN, 2 * SPAN), F32)],
        compiler_params=_params(3),
    )(pview, kview, kview, pview, dao.reshape(dview), o.reshape(dview), lse.reshape(dview), gains, *tables, ones)
    dgain = dg.sum(axis=1).reshape(n_groups + 1, V7X_LANES // HEAD_DIM, HEAD_DIM).sum(axis=1)
    return dproj.reshape(t, d4), dkv.reshape(t, 2 * d), dgain


def _mesh_position():
    x, y, c = lax.axis_index("x"), lax.axis_index("y"), lax.axis_index("c")
    return x, y, c


def _peer(x, y, c, rel):
    return (1 - x if rel & 4 else x, 1 - y if rel & 2 else y, 1 - c if rel & 1 else c)


def _all_gather(arrs, name):
    n = len(arrs)

    def body(*refs):
        ins, outs = refs[:n], refs[n:2 * n]
        send_sems, recv_sems, local_sems = refs[2 * n:]
        x, y, c = _mesh_position()
        me = 4 * x + 2 * y + c
        copies, locals_ = [], []
        for a in range(n):
            loc = pltpu.make_async_copy(ins[a], outs[a].at[me], local_sems.at[a])
            loc.start()
            locals_.append(loc)
            for rel in range(1, N_DEV):
                s = a * (N_DEV - 1) + rel - 1
                cp = pltpu.make_async_remote_copy(
                    src_ref=ins[a], dst_ref=outs[a].at[me], send_sem=send_sems.at[s], recv_sem=recv_sems.at[s],
                    device_id=_peer(x, y, c, rel), device_id_type=pl.DeviceIdType.MESH)
                cp.start()
                copies.append(cp)
        for cp in copies:
            cp.wait_recv()
        for cp in copies:
            cp.wait_send()
        for loc in locals_:
            loc.wait()

    hbm = pl.BlockSpec(memory_space=pltpu.HBM)
    return pl.pallas_call(
        body, name=name, in_specs=[hbm] * n, out_specs=[hbm] * n,
        out_shape=[jax.ShapeDtypeStruct((N_DEV,) + a.shape, a.dtype) for a in arrs],
        scratch_shapes=[pltpu.SemaphoreType.DMA((n * (N_DEV - 1),)), pltpu.SemaphoreType.DMA((n * (N_DEV - 1),)),
                        pltpu.SemaphoreType.DMA((n,))],
    )(*arrs)


def _exchange_slots(pack, name):
    def body(p_ref, o_ref, send_sems, recv_sems, local_sem):
        x, y, c = _mesh_position()
        me = 4 * x + 2 * y + c
        loc = pltpu.make_async_copy(p_ref.at[me], o_ref.at[me], local_sem)
        loc.start()
        copies = []
        for rel in range(1, N_DEV):
            px, py, pc = _peer(x, y, c, rel)
            cp = pltpu.make_async_remote_copy(
                src_ref=p_ref.at[4 * px + 2 * py + pc], dst_ref=o_ref.at[me],
                send_sem=send_sems.at[rel - 1], recv_sem=recv_sems.at[rel - 1],
                device_id=(px, py, pc), device_id_type=pl.DeviceIdType.MESH)
            cp.start()
            copies.append(cp)
        for cp in copies:
            cp.wait_recv()
        for cp in copies:
            cp.wait_send()
        loc.wait()

    hbm = pl.BlockSpec(memory_space=pltpu.HBM)
    return pl.pallas_call(
        body, name=name, in_specs=[hbm], out_specs=hbm,
        out_shape=jax.ShapeDtypeStruct(pack.shape, pack.dtype),
        scratch_shapes=[pltpu.SemaphoreType.DMA((N_DEV - 1,)), pltpu.SemaphoreType.DMA((N_DEV - 1,)),
                        pltpu.SemaphoreType.DMA],
    )(pack)


def _sum_adamw(parts, w, m, v, name):
    _, r, wd = parts.shape
    tr = ADAM_ROWS
    c1 = 1.0 - ADAM_B1 ** ADAM_STEP
    c2 = 1.0 - ADAM_B2 ** ADAM_STEP

    def body(p_ref, w_ref, m_ref, v_ref, g_ref, d_ref, nm_ref, nv_ref):
        g = p_ref[0].astype(F32)
        for s in range(1, N_DEV):
            g = g + p_ref[s].astype(F32)
        nm = ADAM_B1 * m_ref[...] + (1.0 - ADAM_B1) * g
        nv = ADAM_B2 * v_ref[...] + (1.0 - ADAM_B2) * (g * g)
        g_ref[...] = g
        nm_ref[...] = nm
        nv_ref[...] = nv
        d_ref[...] = -ADAM_LR * ((nm / c1) / (jnp.sqrt(nv / c2) + ADAM_EPS) + ADAM_WD * w_ref[...])

    row = pl.BlockSpec((tr, wd), lambda i: (i, 0))
    return pl.pallas_call(
        body, name=name, grid=(r // tr,),
        in_specs=[pl.BlockSpec((N_DEV, tr, wd), lambda i: (0, i, 0)), row, row, row],
        out_specs=[row] * 4, out_shape=[jax.ShapeDtypeStruct((r, wd), F32)] * 4,
        compiler_params=_params(1),
    )(parts, w, m, v)


def _pack_rows(size, row_mult):
    rows = -(-size // PACK_LANES)
    return -(-rows // row_mult) * row_mult


def _pack(flats, row_mult, dtype, total_mult=None):
    out = []
    for f in flats:
        size = f.shape[-1]
        rows = _pack_rows(size, row_mult)
        pad = [(0, 0)] * (f.ndim - 1) + [(0, rows * PACK_LANES - size)]
        out.append(jnp.pad(f.astype(dtype), pad).reshape(f.shape[:-1] + (rows, PACK_LANES)))
    if total_mult is not None:
        total = sum(o.shape[-2] for o in out)
        extra = -(-total // total_mult) * total_mult - total
        if extra:
            out.append(jnp.zeros(out[0].shape[:-2] + (extra, PACK_LANES), dtype))
    return jnp.concatenate(out, axis=-2)


def _unpack(buf, sizes, row_mult):
    out, row = [], 0
    for size in sizes:
        rows = _pack_rows(size, row_mult)
        part = buf[..., row:row + rows, :]
        out.append(part.reshape(buf.shape[:-2] + (rows * PACK_LANES,))[..., :size])
        row += rows
    return out


def _to_slots(full, axis):
    if axis is None:
        return jnp.broadcast_to(full.reshape(1, -1), (N_DEV, full.size))
    shape = full.shape
    split = full.reshape(shape[:axis] + (N_DEV, shape[axis] // N_DEV) + shape[axis + 1:])
    return jnp.moveaxis(split, axis, 0).reshape(N_DEV, -1)


def _from_slots(slots, axis, block_shape):
    split = jnp.moveaxis(slots.reshape((N_DEV,) + tuple(block_shape)), 0, axis)
    shape = list(block_shape)
    shape[axis] *= N_DEV
    return split.reshape(shape)


def kernel(x, p, norm_g, w_in_a, conv_w, conv_b, ln_g, ln_b, w_out_a, kv_norm_g, w_kv, k_norm_g, w_in_b, q_norm_g, w_out_b, ple_norm_g, w_ple_gate, w_ple_proj, loss_target, m_norm_g, m_w_in_a, m_conv_w, m_conv_b, m_ln_g, m_ln_b, m_w_out_a, m_kv_norm_g, m_w_kv, m_k_norm_g, m_w_in_b, m_q_norm_g, m_w_out_b, m_ple_norm_g, m_w_ple_gate, m_w_ple_proj, v_norm_g, v_w_in_a, v_conv_w, v_conv_b, v_ln_g, v_ln_b, v_w_out_a, v_kv_norm_g, v_w_kv, v_k_norm_g, v_w_in_b, v_q_norm_g, v_w_out_b, v_ple_norm_g, v_w_ple_gate, v_w_ple_proj):
    weights = dict(zip(WEIGHT_NAMES, (norm_g, w_in_a, conv_w, conv_b, ln_g, ln_b, w_out_a, kv_norm_g, w_kv, k_norm_g,
                                      w_in_b, q_norm_g, w_out_b, ple_norm_g, w_ple_gate, w_ple_proj)))
    mom_m = dict(zip(WEIGHT_NAMES, (m_norm_g, m_w_in_a, m_conv_w, m_conv_b, m_ln_g, m_ln_b, m_w_out_a, m_kv_norm_g,
                                    m_w_kv, m_k_norm_g, m_w_in_b, m_q_norm_g, m_w_out_b, m_ple_norm_g, m_w_ple_gate,
                                    m_w_ple_proj)))
    mom_v = dict(zip(WEIGHT_NAMES, (v_norm_g, v_w_in_a, v_conv_w, v_conv_b, v_ln_g, v_ln_b, v_w_out_a, v_kv_norm_g,
                                    v_w_kv, v_k_norm_g, v_w_in_b, v_q_norm_g, v_w_out_b, v_ple_norm_g, v_w_ple_gate,
                                    v_w_ple_proj)))
    bsz, seq, d = x.shape
    t = bsz * seq
    assert seq % (max(DILATIONS) * SPAN) == 0 and d % V7X_LANES == 0

    wq_pack = _pack([weights[n].reshape(-1) for n in MATMUL_WEIGHTS], 16, BF16)
    wv_pack = _pack([weights[n].reshape(-1) for n in VECTOR_WEIGHTS], 8, F32)
    wq_all, wv_all = _all_gather([wq_pack, wv_pack], "gather_weights")
    full = {}
    for names, buf, mult in ((MATMUL_WEIGHTS, wq_all, 16), (VECTOR_WEIGHTS, wv_all, 8)):
        for n, slots in zip(names, _unpack(buf, [weights[n].size for n in names], mult)):
            full[n] = _from_slots(slots, SHARD_AXIS[n], weights[n].shape)
    wa_in, wa_out = full['w_in_a'][0], full['w_out_a'][0]
    wkv = full['w_kv']
    wb_in, wb_out = full['w_in_b'][0], full['w_out_b'][0]
    wg, wp = full['w_ple_gate'], full['w_ple_proj']
    cw, cb, lg, lb = full['conv_w'][0], full['conv_b'], full['ln_g'], full['ln_b']

    tables = _rope_tables(seq)
    ones = _head_ones(V7X_LANES)
    rep = V7X_LANES // HEAD_DIM
    head_gain = jnp.concatenate([jnp.tile(q_norm_g[0], (1, rep)), jnp.tile(k_norm_g, rep)[None]], axis=0)

    x0 = x.reshape(t, d)
    p0, p1 = p[0].reshape(t, -1), p[1].reshape(t, -1)
    target = loss_target.reshape(t, d)
    g_norm0, g_norm1 = norm_g[0:1], norm_g[1:2]
    g_ple0, g_ple1 = ple_norm_g[0:1], ple_norm_g[1:2]
    g_kv = kv_norm_g.reshape(1, d)

    (u0,) = _rmsnorm_fwd(x0, [g_norm0], "norm0")
    proj_a = _matmul(u0, wa_in, 'nn', "in_a")
    m_act, y_conv = _conv_fwd(proj_a, cw, cb, lg, lb, seq, "conv_fwd")
    h0 = _matmul(m_act, wa_out, 'nn', "out_a", add=x0)
    (pg0,) = _rmsnorm_fwd(h0, [g_ple0], "ple_norm0")
    gl0 = _matmul(pg0, wg[0], 'nn', "ple_gate0")
    pp0 = _matmul(p0, wp[0], 'nn', "ple_proj0")
    x1 = _ple_fwd(h0, gl0, pp0, "ple0")

    kvn, u1 = _rmsnorm_fwd(x1, [g_kv, g_norm1], "norm1")
    kv = _matmul(kvn, wkv, 'nn', "kv")
    proj_b = _matmul(u1, wb_in, 'nn', "in_b")
    o_att, lse, ao = _attn_fwd(proj_b, kv, head_gain, tables, ones, bsz, seq, "attn_fwd")
    h1 = _matmul(ao, wb_out, 'nn', "out_b", add=x1)
    (pg1,) = _rmsnorm_fwd(h1, [g_ple1], "ple_norm1")
    gl1 = _matmul(pg1, wg[1], 'nn', "ple_gate1")
    pp1 = _matmul(p1, wp[1], 'nn', "ple_proj1")
    x2 = _ple_fwd(h1, gl1, pp1, "ple1")

    dx2, loss_part = _loss_fwd_bwd(x2, target, "loss")
    loss = lax.psum(jnp.sum(loss_part), ("x", "y", "c"))

    grads = {}

    dgl1, dpp1 = _ple_bwd(dx2, gl1, pp1, "ple1_bwd")
    dwp1 = _matmul(p1, dpp1, 'tn', "d_ple_proj1")
    dwg1 = _matmul(pg1, dgl1, 'tn', "d_ple_gate1")
    dpg1 = _matmul(dgl1, wg[1], 'nt', "d_ple_norm1")
    dh1, (dg_ple1,) = _rmsnorm_bwd(h1, [g_ple1], [dpg1], dx2, "ple_norm1_bwd")
    grads['w_out_b'] = _matmul(ao, dh1, 'tn', "d_out_b")[None]
    dao = _matmul(dh1, wb_out, 'nt', "d_ao")
    dproj_b, dkv, dg_head = _attn_bwd(proj_b, kv, dao, o_att, lse, head_gain, tables, ones, bsz, seq, "attn_bwd")
    grads['w_in_b'] = _matmul(u1, dproj_b, 'tn', "d_in_b")[None]
    du1 = _matmul(dproj_b, wb_in, 'nt', "d_u1")
    grads['w_kv'] = _matmul(kvn, dkv, 'tn', "d_kv")
    dkvn = _matmul(dkv, wkv, 'nt', "d_kvn")
    dx1, (dg_kv, dg_norm1) = _rmsnorm_bwd(x1, [g_kv, g_norm1], [dkvn, du1], dh1, "norm1_bwd")

    dgl0, dpp0 = _ple_bwd(dx1, gl0, pp0, "ple0_bwd")
    dwp0 = _matmul(p0, dpp0, 'tn', "d_ple_proj0")
    dwg0 = _matmul(pg0, dgl0, 'tn', "d_ple_gate0")
    dpg0 = _matmul(dgl0, wg[0], 'nt', "d_ple_norm0")
    dh0, (dg_ple0,) = _rmsnorm_bwd(h0, [g_ple0], [dpg0], dx1, "ple_norm0_bwd")
    grads['w_out_a'] = _matmul(m_act, dh0, 'tn', "d_out_a")[None]
    dm = _matmul(dh0, wa_out, 'nt', "d_m")
    dy_conv, dz, d_lg, d_lb, d_cb = _ln_gate_bwd(dm, y_conv, proj_a, lg, lb, "ln_gate_bwd")
    dproj_a, d_cw = _conv_bwd(dy_conv, dz, proj_a, cw, seq, "conv_bwd")
    grads['w_in_a'] = _matmul(u0, dproj_a, 'tn', "d_in_a")[None]
    du0 = _matmul(dproj_a, wa_in, 'nt', "d_u0")
    dx0, (dg_norm0,) = _rmsnorm_bwd(x0, [g_norm0], [du0], dh0, "norm0_bwd")

    grads['norm_g'] = jnp.stack([dg_norm0, dg_norm1])
    grads['conv_w'] = d_cw[None]
    grads['conv_b'] = d_cb[None]
    grads['ln_g'] = d_lg[None]
    grads['ln_b'] = d_lb[None]
    grads['kv_norm_g'] = dg_kv
    grads['k_norm_g'] = dg_head[3]
    grads['q_norm_g'] = dg_head[0:3][None]
    grads['ple_norm_g'] = jnp.stack([dg_ple0, dg_ple1])
    grads['w_ple_gate'] = jnp.stack([dwg0, dwg1])
    grads['w_ple_proj'] = jnp.stack([dwp0, dwp1])

    sizes = [weights[n].size for n in WEIGHT_NAMES]
    g_pack = _pack([_to_slots(grads[n], SHARD_AXIS[n]) for n in WEIGHT_NAMES], 16, BF16, ADAM_ROWS)
    parts = _exchange_slots(g_pack, "exchange_grads")
    w_pack = _pack([weights[n].reshape(-1) for n in WEIGHT_NAMES], 16, F32, ADAM_ROWS)
    m_pack = _pack([mom_m[n].reshape(-1) for n in WEIGHT_NAMES], 16, F32, ADAM_ROWS)
    v_pack = _pack([mom_v[n].reshape(-1) for n in WEIGHT_NAMES], 16, F32, ADAM_ROWS)
    outs = _sum_adamw(parts, w_pack, m_pack, v_pack, "sum_adamw")
    result = [loss, dx0.reshape(bsz, seq, d)]
    for buf in outs:
        for n, flat in zip(WEIGHT_NAMES, _unpack(buf, sizes, 16)):
            result.append(flat.reshape(weights[n].shape))
    return tuple(result)
```

```python
import jax
import jax.numpy as jnp
from jax import lax
from jax.experimental import pallas as pl
from jax.experimental.pallas import tpu as pltpu

F32 = jnp.float32
BF16 = jnp.bfloat16

N_DEV = 8
HEAD_DIM = 64
ROPE_DIM = 16
ROPE_THETA = 500000.0
EPS = 1e-6
NEG_INF = -1e30
SPAN = 128
DILATIONS = (1, 4, 16)
CONV_WIDTH = 31
HALO = 32
PACK_LANES = 1024
V7X_LANES = 128
V7X_SUBLANES = 8
VMEM_LIMIT_BYTES = 56 * 1024 * 1024

ADAM_LR = 0.001
ADAM_B1 = 0.9
ADAM_B2 = 0.999
ADAM_EPS = 1e-08
ADAM_WD = 0.01
ADAM_STEP = 10
ADAM_ROWS = 128

WEIGHT_NAMES = ('norm_g', 'w_in_a', 'conv_w', 'conv_b', 'ln_g', 'ln_b', 'w_out_a', 'kv_norm_g', 'w_kv',
                'k_norm_g', 'w_in_b', 'q_norm_g', 'w_out_b', 'ple_norm_g', 'w_ple_gate', 'w_ple_proj')
SHARD_AXIS = {'norm_g': None, 'w_in_a': 2, 'conv_w': 2, 'conv_b': 1, 'ln_g': 1, 'ln_b': 1, 'w_out_a': 1,
              'kv_norm_g': None, 'w_kv': 1, 'k_norm_g': None, 'w_in_b': 2, 'q_norm_g': None, 'w_out_b': 1,
              'ple_norm_g': None, 'w_ple_gate': 1, 'w_ple_proj': 2}
MATMUL_WEIGHTS = ('w_in_a', 'w_out_a', 'w_kv', 'w_in_b', 'w_out_b', 'w_ple_gate', 'w_ple_proj')
VECTOR_WEIGHTS = ('conv_w', 'conv_b', 'ln_g', 'ln_b')


def _pick(n, target, mult):
    t = (min(target, n) // mult) * mult
    while t >= mult:
        if n % t == 0:
            return t
        t -= mult
    return n


def _params(n_grid):
    return pltpu.CompilerParams(dimension_semantics=("arbitrary",) * n_grid, vmem_limit_bytes=VMEM_LIMIT_BYTES)


def _sig(x):
    return 1.0 / (1.0 + jnp.exp(-x))


def _colsum8(v):
    r, w = v.shape
    return v.reshape(r // V7X_SUBLANES, V7X_SUBLANES, w).sum(axis=0)


def _rows(tm, w, col=0):
    return pl.BlockSpec((tm, w), lambda i: (i, col))


def _const(shape):
    nd = len(shape)
    return pl.BlockSpec(shape, lambda i: (0,) * nd)


def _segsum(v, e_ref):
    hi = v.astype(BF16)
    lo = (v - hi.astype(F32)).astype(BF16)
    e = e_ref[...]
    return jnp.dot(hi, e, preferred_element_type=F32) + jnp.dot(lo, e, preferred_element_type=F32)


def _matmul(a, b, mode, name, out_dtype=F32, add=None):
    if mode == 'nn':
        (m, k), (_, n) = a.shape, b.shape
    elif mode == 'nt':
        (m, k), (n, _) = a.shape, b.shape
    else:
        (k, m), (_, n) = a.shape, b.shape
    if mode == 'tn':
        tm, tn, tk = _pick(m, 1024, 128), _pick(n, 1024, 128), _pick(k, 512, 128)
        grid = (m // tm, n // tn, k // tk)
        a_spec = pl.BlockSpec((tk, tm), lambda i, j, kk: (kk, i))
        b_spec = pl.BlockSpec((tk, tn), lambda i, j, kk: (kk, j))
        o_spec = pl.BlockSpec((tm, tn), lambda i, j, kk: (i, j))
        dims = (((0,), (0,)), ((), ()))
    else:
        tm, tn, tk = _pick(m, 512, 128), _pick(n, 1024, 128), _pick(k, 1024, 128)
        grid = (n // tn, m // tm, k // tk)
        a_spec = pl.BlockSpec((tm, tk), lambda j, i, kk: (i, kk))
        o_spec = pl.BlockSpec((tm, tn), lambda j, i, kk: (i, j))
        if mode == 'nn':
            b_spec = pl.BlockSpec((tk, tn), lambda j, i, kk: (kk, j))
            dims = (((1,), (0,)), ((), ()))
        else:
            b_spec = pl.BlockSpec((tn, tk), lambda j, i, kk: (j, kk))
            dims = (((1,), (1,)), ((), ()))
    nk = grid[2]
    has_add = add is not None

    def body(*refs):
        a_ref, b_ref = refs[0], refs[1]
        add_ref = refs[2] if has_add else None
        o_ref = refs[2 + has_add]
        part = lax.dot_general(a_ref[...].astype(BF16), b_ref[...].astype(BF16), dims, preferred_element_type=F32)

        def finish(total):
            if has_add:
                total = total + add_ref[...]
            o_ref[...] = total.astype(out_dtype)

        if nk == 1:
            finish(part)
        else:
            acc_ref = refs[3 + has_add]
            kk = pl.program_id(2)

            @pl.when(kk == 0)
            def _():
                acc_ref[...] = part

            @pl.when(kk > 0)
            def _():
                acc_ref[...] += part

            @pl.when(kk == nk - 1)
            def _():
                finish(acc_ref[...])

    in_specs = [a_spec, b_spec] + ([o_spec] if has_add else [])
    args = [a, b] + ([add] if has_add else [])
    scratch = [pltpu.VMEM((tm, tn), F32)] if nk > 1 else []
    return pl.pallas_call(
        body, name=name, grid=grid, in_specs=in_specs, out_specs=o_spec,
        out_shape=jax.ShapeDtypeStruct((m, n), out_dtype), scratch_shapes=scratch,
        compiler_params=_params(3),
    )(*args)


def _rmsnorm_fwd(x, gains, name):
    t, d = x.shape
    tm = _pick(t, 512, 8)
    n = len(gains)

    def body(*refs):
        x_ref, g_refs, o_refs = refs[0], refs[1:1 + n], refs[1 + n:]
        xv = x_ref[...]
        y = xv * lax.rsqrt(jnp.mean(xv * xv, axis=-1, keepdims=True) + EPS)
        for g_ref, o_ref in zip(g_refs, o_refs):
            o_ref[...] = (y * g_ref[...]).astype(BF16)

    return pl.pallas_call(
        body, name=name, grid=(t // tm,),
        in_specs=[_rows(tm, d)] + [_const((1, d))] * n,
        out_specs=[_rows(tm, d)] * n,
        out_shape=[jax.ShapeDtypeStruct((t, d), BF16)] * n,
        compiler_params=_params(1),
    )(x, *gains)


def _rmsnorm_bwd(x, gains, dys, add, name):
    t, d = x.shape
    tm = _pick(t, 512, 8)
    n = len(gains)

    def body(*refs):
        x_ref, add_ref = refs[0], refs[1]
        g_refs, dy_refs = refs[2:2 + n], refs[2 + n:2 + 2 * n]
        dx_ref, dg_refs = refs[2 + 2 * n], refs[3 + 2 * n:]
        i = pl.program_id(0)
        xv = x_ref[...]
        r = lax.rsqrt(jnp.mean(xv * xv, axis=-1, keepdims=True) + EPS)
        xhat = xv * r
        dx = add_ref[...]
        for g_ref, dy_ref, dg_ref in zip(g_refs, dy_refs, dg_refs):
            dy = dy_ref[...]
            dyg = dy * g_ref[...]
            dx = dx + r * (dyg - xhat * jnp.mean(dyg * xhat, axis=-1, keepdims=True))
            part = _colsum8(dy * xhat)

            @pl.when(i == 0)
            def _():
                dg_ref[...] = part

            @pl.when(i > 0)
            def _():
                dg_ref[...] += part

        dx_ref[...] = dx

    outs = pl.pallas_call(
        body, name=name, grid=(t // tm,),
        in_specs=[_rows(tm, d), _rows(tm, d)] + [_const((1, d))] * n + [_rows(tm, d)] * n,
        out_specs=[_rows(tm, d)] + [_const((V7X_SUBLANES, d))] * n,
        out_shape=[jax.ShapeDtypeStruct((t, d), F32)] + [jax.ShapeDtypeStruct((V7X_SUBLANES, d), F32)] * n,
        compiler_params=_params(1),
    )(x, add, *gains, *dys)
    return outs[0], [o.sum(axis=0) for o in outs[1:]]


def _ple_fwd(h, gl, pp, name):
    t, d = h.shape
    tm = _pick(t, 512, 8)

    def body(h_ref, gl_ref, pp_ref, o_ref):
        o_ref[...] = h_ref[...] + _sig(gl_ref[...]) * pp_ref[...]

    return pl.pallas_call(
        body, name=name, grid=(t // tm,), in_specs=[_rows(tm, d)] * 3, out_specs=_rows(tm, d),
        out_shape=jax.ShapeDtypeStruct((t, d), F32), compiler_params=_params(1),
    )(h, gl, pp)


def _ple_bwd(dx, gl, pp, name):
    t, d = dx.shape
    tm = _pick(t, 512, 8)

    def body(dx_ref, gl_ref, pp_ref, dgl_ref, dpp_ref):
        dxv = dx_ref[...]
        sg = _sig(gl_ref[...])
        dpp_ref[...] = (dxv * sg).astype(BF16)
        dgl_ref[...] = (dxv * pp_ref[...] * sg * (1.0 - sg)).astype(BF16)

    return pl.pallas_call(
        body, name=name, grid=(t // tm,), in_specs=[_rows(tm, d)] * 3, out_specs=[_rows(tm, d)] * 2,
        out_shape=[jax.ShapeDtypeStruct((t, d), BF16)] * 2, compiler_params=_params(1),
    )(dx, gl, pp)


def _loss_fwd_bwd(y, target, name):
    t, d = y.shape
    tm = _pick(t, 512, 8)
    inv_d = 1.0 / d

    def body(y_ref, t_ref, dy_ref, l_ref):
        i = pl.program_id(0)
        e = y_ref[...] - t_ref[...]
        dy_ref[...] = e * inv_d
        part = _colsum8(e * e) * (0.5 * inv_d)

        @pl.when(i == 0)
        def _():
            l_ref[...] = part

        @pl.when(i > 0)
        def _():
            l_ref[...] += part

    return pl.pallas_call(
        body, name=name, grid=(t // tm,), in_specs=[_rows(tm, d)] * 2,
        out_specs=[_rows(tm, d), _const((V7X_SUBLANES, d))],
        out_shape=[jax.ShapeDtypeStruct((t, d), F32), jax.ShapeDtypeStruct((V7X_SUBLANES, d), F32)],
        compiler_params=_params(1),
    )(y, target)


def _shift_scratch(ts, cc):
    return pltpu.VMEM((V7X_SUBLANES, ts + HALO - V7X_SUBLANES, cc), F32)


def _shifted_copies(sh_ref, win_ref, cs, ts):
    rows = ts + HALO - V7X_SUBLANES
    for s in range(1, V7X_SUBLANES):
        sh_ref[s] = win_ref[pl.ds(s, rows), cs]


def _tap(sh_ref, win_ref, cs, offset, ts):
    s = offset % V7X_SUBLANES
    if s == 0:
        return win_ref[pl.ds(offset, ts), cs]
    return sh_ref[s, pl.ds(offset - s, ts), :]


def _conv_fwd(proj, conv_w, conv_b, ln_g, ln_b, seq, name):
    t, c3 = proj.shape
    c = c3 // 3
    ts = _pick(seq, 256, HALO)
    nsb = seq // ts
    cc = _pick(c, 512, V7X_LANES)
    hb = ts // HALO

    def body(a_ref, b_ref, z_ref, ap_ref, bp_ref, w_ref, cb_ref, g_ref, be_ref, m_ref, y_ref, win_ref, sh_ref):
        i = pl.program_id(0)
        first = (i % nsb) == 0
        win_ref[0:HALO, :] = jnp.where(first, 0.0, ap_ref[...] * _sig(bp_ref[...]))
        win_ref[HALO:, :] = a_ref[...] * _sig(b_ref[...])
        for ci in range(c // cc):
            cs = slice(ci * cc, (ci + 1) * cc)
            _shifted_copies(sh_ref, win_ref, cs, ts)
            acc = jnp.zeros((ts, cc), F32) + cb_ref[:, cs]
            for k in range(CONV_WIDTH):
                acc = acc + w_ref[k:k + 1, cs] * _tap(sh_ref, win_ref, cs, HALO - (CONV_WIDTH - 1) + k, ts)
            y_ref[:, cs] = acc
        y = y_ref[...]
        mu = jnp.mean(y, axis=-1, keepdims=True)
        xc = y - mu
        rstd = lax.rsqrt(jnp.mean(xc * xc, axis=-1, keepdims=True) + EPS)
        ln = xc * rstd * g_ref[...] + be_ref[...]
        zz = z_ref[...]
        m_ref[...] = (ln * _sig(ln) * zz * _sig(zz)).astype(BF16)

    halo_a = pl.BlockSpec((HALO, c), lambda i: (jnp.maximum(i * hb - 1, 0), 0))
    halo_b = pl.BlockSpec((HALO, c), lambda i: (jnp.maximum(i * hb - 1, 0), 1))
    return pl.pallas_call(
        body, name=name, grid=(t // ts,),
        in_specs=[_rows(ts, c, 0), _rows(ts, c, 1), _rows(ts, c, 2), halo_a, halo_b,
                  _const((CONV_WIDTH, c)), _const((1, c)), _const((1, c)), _const((1, c))],
        out_specs=[_rows(ts, c), _rows(ts, c)],
        out_shape=[jax.ShapeDtypeStruct((t, c), BF16), jax.ShapeDtypeStruct((t, c), F32)],
        scratch_shapes=[pltpu.VMEM((HALO + ts, c), F32), _shift_scratch(ts, cc)],
        compiler_params=_params(1),
    )(proj, proj, proj, proj, proj, conv_w, conv_b, ln_g, ln_b)


def _ln_gate_bwd(dm, y, proj, ln_g, ln_b, name):
    t, c = y.shape
    tm = _pick(t, 256, 8)

    def body(dm_ref, y_ref, z_ref, g_ref, be_ref, dy_ref, dz_ref, dg_ref, db_ref, dcb_ref):
        i = pl.program_id(0)
        yv = y_ref[...]
        mu = jnp.mean(yv, axis=-1, keepdims=True)
        xc = yv - mu
        rstd = lax.rsqrt(jnp.mean(xc * xc, axis=-1, keepdims=True) + EPS)
        xhat = xc * rstd
        g = g_ref[...]
        ln = xhat * g + be_ref[...]
        sl = _sig(ln)
        zz = z_ref[...]
        sz = _sig(zz)
        dmv = dm_ref[...]
        dz_ref[...] = (dmv * (ln * sl) * (sz * (1.0 + zz * (1.0 - sz)))).astype(BF16)
        dln = dmv * (zz * sz) * (sl * (1.0 + ln * (1.0 - sl)))
        dxh = dln * g
        dyv = rstd * (dxh - jnp.mean(dxh, axis=-1, keepdims=True)
                      - xhat * jnp.mean(dxh * xhat, axis=-1, keepdims=True))
        dy_ref[...] = dyv
        parts = (_colsum8(dln * xhat), _colsum8(dln), _colsum8(dyv))

        @pl.when(i == 0)
        def _():
            for ref, part in zip((dg_ref, db_ref, dcb_ref), parts):
                ref[...] = part

        @pl.when(i > 0)
        def _():
            for ref, part in zip((dg_ref, db_ref, dcb_ref), parts):
                ref[...] += part

    acc = jax.ShapeDtypeStruct((V7X_SUBLANES, c), F32)
    outs = pl.pallas_call(
        body, name=name, grid=(t // tm,),
        in_specs=[_rows(tm, c), _rows(tm, c), _rows(tm, c, 2), _const((1, c)), _const((1, c))],
        out_specs=[_rows(tm, c), _rows(tm, c)] + [_const((V7X_SUBLANES, c))] * 3,
        out_shape=[jax.ShapeDtypeStruct((t, c), F32), jax.ShapeDtypeStruct((t, c), BF16), acc, acc, acc],
        compiler_params=_params(1),
    )(dm, y, proj, ln_g, ln_b)
    return outs[0], outs[1], outs[2].sum(axis=0), outs[3].sum(axis=0), outs[4].sum(axis=0)


def _conv_bwd(dy, dz, proj, conv_w, seq, name):
    t, c3 = proj.shape
    c = c3 // 3
    ts = _pick(seq, 256, HALO)
    nsb = seq // ts
    cc = _pick(c, 512, V7X_LANES)
    hb = ts // HALO
    last_halo = t // HALO - 1
    back = CONV_WIDTH - 1

    def body(dy_ref, dyn_ref, dz_ref, a_ref, b_ref, ap_ref, bp_ref, w_ref, o_ref, dw_ref, win_ref, dwin_ref,
             sh_ref, dsh_ref):
        i = pl.program_id(0)
        first = (i % nsb) == 0
        last = (i % nsb) == nsb - 1
        sb = _sig(b_ref[...])
        av = a_ref[...]
        win_ref[0:HALO, :] = jnp.where(first, 0.0, ap_ref[...] * _sig(bp_ref[...]))
        win_ref[HALO:, :] = av * sb
        dwin_ref[0:ts, :] = dy_ref[...]
        dwin_ref[ts:, :] = jnp.where(last, 0.0, dyn_ref[...])

        @pl.when(i == 0)
        def _():
            dw_ref[...] = jnp.zeros_like(dw_ref)

        for ci in range(c // cc):
            cs = slice(ci * cc, (ci + 1) * cc)
            _shifted_copies(sh_ref, win_ref, cs, ts)
            _shifted_copies(dsh_ref, dwin_ref, cs, ts)
            dcur = dwin_ref[0:ts, cs]
            dglu = jnp.zeros((ts, cc), F32)
            for k in range(CONV_WIDTH):
                dglu = dglu + w_ref[k:k + 1, cs] * _tap(dsh_ref, dwin_ref, cs, back - k, ts)
                dw_ref[k * V7X_SUBLANES:(k + 1) * V7X_SUBLANES, cs] += _colsum8(
                    dcur * _tap(sh_ref, win_ref, cs, HALO - back + k, ts))
            sbc = sb[:, cs]
            o_ref[:, cs] = (dglu * sbc).astype(BF16)
            o_ref[:, c + ci * cc:c + (ci + 1) * cc] = (dglu * av[:, cs] * sbc * (1.0 - sbc)).astype(BF16)
        o_ref[:, 2 * c:] = dz_ref[...]

    halo_next = pl.BlockSpec((HALO, c), lambda i: (jnp.minimum((i + 1) * hb, last_halo), 0))
    halo_a = pl.BlockSpec((HALO, c), lambda i: (jnp.maximum(i * hb - 1, 0), 0))
    halo_b = pl.BlockSpec((HALO, c), lambda i: (jnp.maximum(i * hb - 1, 0), 1))
    dproj, dw = pl.pallas_call(
        body, name=name, grid=(t // ts,),
        in_specs=[_rows(ts, c), halo_next, _rows(ts, c), _rows(ts, c, 0), _rows(ts, c, 1), halo_a, halo_b,
                  _const((CONV_WIDTH, c))],
        out_specs=[_rows(ts, c3), _const((CONV_WIDTH * V7X_SUBLANES, c))],
        out_shape=[jax.ShapeDtypeStruct((t, c3), BF16),
                   jax.ShapeDtypeStruct((CONV_WIDTH * V7X_SUBLANES, c), F32)],
        scratch_shapes=[pltpu.VMEM((HALO + ts, c), F32), pltpu.VMEM((ts + HALO, c), F32),
                        _shift_scratch(ts, cc), _shift_scratch(ts, cc)],
        compiler_params=_params(1),
    )(dy, dy, dz, proj, proj, proj, proj, conv_w)
    return dproj, dw.reshape(CONV_WIDTH, V7X_SUBLANES, c).sum(axis=1)


def _rope_tables(seq):
    half = ROPE_DIM // 2
    inv = ROPE_THETA ** (-jnp.arange(half, dtype=F32) * (2.0 / ROPE_DIM))
    ang = jnp.arange(seq).astype(F32)[:, None] * inv[None, :]
    cos, sin = jnp.cos(ang), jnp.sin(ang)
    zeros = jnp.zeros((seq, HEAD_DIM - ROPE_DIM), F32)
    zh = jnp.zeros((seq, half), F32)
    a = jnp.concatenate([cos, cos, zeros + 1.0], axis=1)
    b = jnp.concatenate([zh, sin, zeros], axis=1)
    c = jnp.concatenate([-sin, zh, zeros], axis=1)
    rep = V7X_LANES // HEAD_DIM
    return tuple(jnp.tile(v, (1, rep)) for v in (a, b, c))


def _head_ones(d):
    head = jnp.arange(d) // HEAD_DIM
    return (head[:, None] == head[None, :]).astype(BF16)


def _rope(ch, ta, tb, tc):
    return ta * ch + tb * pltpu.roll(ch, ROPE_DIM // 2, 1) + tc * pltpu.roll(ch, V7X_LANES - ROPE_DIM // 2, 1)


def _rope_t(ch, ta, tb, tc):
    return ta * ch + pltpu.roll(tb * ch, V7X_LANES - ROPE_DIM // 2, 1) + pltpu.roll(tc * ch, ROPE_DIM // 2, 1)


def _norm_rope(xv, gain, ta, tb, tc, e_ref):
    r = lax.rsqrt(_segsum(xv * xv, e_ref) * (1.0 / HEAD_DIM) + EPS)
    return _rope(xv * r * gain, ta, tb, tc)


def _norm_rope_bwd(xv, dout, gain, ta, tb, tc, e_ref):
    dxn = _rope_t(dout, ta, tb, tc)
    r = lax.rsqrt(_segsum(xv * xv, e_ref) * (1.0 / HEAD_DIM) + EPS)
    xhat = xv * r
    dxh = dxn * gain
    dx = r * (dxh - xhat * (_segsum(dxh * xhat, e_ref) * (1.0 / HEAD_DIM)))
    return dx, _colsum8(dxn * xhat)


def _norm_rope_rows(dst_ref, src_ref, gain, ta_ref, tb_ref, tc_ref, e_ref, seq):
    for r0 in range(0, seq, ATTN_PIECE):
        rows = slice(r0, r0 + ATTN_PIECE)
        dst_ref[rows, :] = _norm_rope(src_ref[rows, :], gain, ta_ref[rows, :], tb_ref[rows, :], tc_ref[rows, :], e_ref)


ATTN_PIECE = 256
ATTN_UNROLL = 4


def _pieces(dil, seq):
    length = seq // dil
    rows = min(length, ATTN_PIECE)
    return [(r + dil * ci * rows, r * length + ci * rows, rows) for r in range(dil) for ci in range(length // rows)]


def _strided(ref, start, rows, dil):
    if dil == 1:
        return ref[pl.ds(start, rows), :]
    return ref[pl.ds(start, rows, stride=dil), :]


def _strided_set(ref, start, rows, dil, val):
    if dil == 1:
        ref[pl.ds(start, rows), :] = val
    else:
        ref[pl.ds(start, rows, stride=dil), :] = val


def _nt(a, b):
    return lax.dot_general(a, b, (((1,), (1,)), ((), ())), preferred_element_type=F32)


def _tn(a, b):
    return lax.dot_general(a, b, (((0,), (0,)), ((), ())), preferred_element_type=F32)


def _set_bias(bias_ref):
    qi = lax.broadcasted_iota(jnp.int32, (2 * SPAN, 2 * SPAN), 0) & (SPAN - 1)
    kj = lax.broadcasted_iota(jnp.int32, (2 * SPAN, 2 * SPAN), 1)
    band = jnp.logical_and(kj >= qi, (kj - SPAN) <= qi)
    bias_ref[1] = jnp.where(band, 0.0, NEG_INF)
    bias_ref[0] = jnp.where(jnp.logical_and(band, kj >= SPAN), 0.0, NEG_INF)


def _stack_heads(v, head0):
    zero = jnp.zeros_like(v)
    return jnp.concatenate([jnp.where(head0, v, zero), jnp.where(head0, zero, v)], axis=0)


def _unstack_heads(v2, head0):
    return jnp.where(head0, v2[:SPAN], v2[SPAN:])


def _head_cols(v):
    return jnp.concatenate([v[:, 0:1], v[:, HEAD_DIM:HEAD_DIM + 1]], axis=0)


def _attn_fwd(proj_b, kv, gains, tables, ones, bsz, seq, name):
    t, d4 = proj_b.shape
    d = d4 // 4
    nhp = d // V7X_LANES
    nblk = seq // SPAN
    scale = HEAD_DIM ** -0.5
    n_groups = len(DILATIONS)

    def body(q0_ref, q1_ref, q2_ref, k_ref, v_ref, gate_ref, gain_ref, ta_ref, tb_ref, tc_ref, e_ref,
             o_ref, l_ref, ao_ref, qd, kd, vd, od, ld, on0, on1, on2, ln0, ln1, ln2, kn, bias):
        head0 = lax.broadcasted_iota(jnp.int32, (SPAN, V7X_LANES), 1) < HEAD_DIM

        @pl.when(jnp.logical_and(pl.program_id(0) == 0, pl.program_id(1) == 0))
        def _():
            _set_bias(bias)

        _norm_rope_rows(kn, k_ref, gain_ref[n_groups:n_groups + 1, :], ta_ref, tb_ref, tc_ref, e_ref, seq)
        kd[0:SPAN, :] = jnp.zeros((SPAN, V7X_LANES), BF16)
        vd[0:SPAN, :] = jnp.zeros((SPAN, V7X_LANES), BF16)
        for g, (q_ref, on, ln) in enumerate(((q0_ref, on0, ln0), (q1_ref, on1, ln1), (q2_ref, on2, ln2))):
            dil = DILATIONS[g]
            nb = seq // dil // SPAN
            for ns, rs, rows in _pieces(dil, seq):
                ta, tb, tc = (_strided(r_, ns, rows, dil) for r_ in (ta_ref, tb_ref, tc_ref))
                qd[rs:rs + rows, :] = _norm_rope(_strided(q_ref, ns, rows, dil), gain_ref[g:g + 1, :],
                                                 ta, tb, tc, e_ref).astype(BF16)
                kd[SPAN + rs:SPAN + rs + rows, :] = _strided(kn, ns, rows, dil).astype(BF16)
                vd[SPAN + rs:SPAN + rs + rows, :] = _strided(v_ref, ns, rows, dil).astype(BF16)

            def block(j, carry):
                qs = pl.multiple_of(j * SPAN, SPAN)
                q2 = _stack_heads(qd[pl.ds(qs, SPAN), :], head0)
                kk = kd[pl.ds(qs, 2 * SPAN), :]
                vv = vd[pl.ds(qs, 2 * SPAN), :]
                s = _nt(q2, kk) * scale + bias[jnp.minimum(j & (nb - 1), 1)]
                mx = jnp.max(s, axis=1, keepdims=True)
                p = jnp.exp(s - mx)
                den = jnp.sum(p, axis=1, keepdims=True)
                o2 = jnp.dot(p.astype(BF16), vv, preferred_element_type=F32) / den
                l2 = jnp.broadcast_to(mx + jnp.log(den), (2 * SPAN, V7X_LANES))
                od[pl.ds(qs, SPAN), :] = _unstack_heads(o2, head0)
                ld[pl.ds(qs, SPAN), :] = _unstack_heads(l2, head0)
                return carry

            lax.fori_loop(0, nblk, block, 0, unroll=ATTN_UNROLL)
            for ns, rs, rows in _pieces(dil, seq):
                _strided_set(on, ns, rows, dil, od[rs:rs + rows, :])
                _strided_set(ln, ns, rows, dil, ld[rs:rs + rows, :])

        def merge(ci, carry):
            rows = pl.ds(pl.multiple_of(ci * ATTN_PIECE, ATTN_PIECE), ATTN_PIECE)
            ls = [ln0[rows, :], ln1[rows, :], ln2[rows, :]]
            mx = jnp.maximum(jnp.maximum(ls[0], ls[1]), ls[2])
            es = [jnp.exp(v - mx) for v in ls]
            den = es[0] + es[1] + es[2]
            ov = (es[0] * on0[rows, :] + es[1] * on1[rows, :] + es[2] * on2[rows, :]) / den
            gate = gate_ref[rows, :]
            o_ref[rows, :] = ov
            l_ref[rows, :] = mx + jnp.log(den)
            ao_ref[rows, :] = (ov * gate * _sig(gate)).astype(BF16)
            return carry

        lax.fori_loop(0, seq // ATTN_PIECE, merge, 0)

    blk = (None, seq, V7X_LANES)
    pview = proj_b.reshape(bsz, seq, d4)
    kview = kv.reshape(bsz, seq, 2 * d)
    out_spec = pl.BlockSpec(blk, lambda b, h: (b, 0, h))
    tab = pl.BlockSpec((seq, V7X_LANES), lambda b, h: (0, 0))
    nat = pltpu.VMEM((seq, V7X_LANES), F32)
    o, lse, ao = pl.pallas_call(
        body, name=name, grid=(bsz, nhp),
        in_specs=[pl.BlockSpec(blk, lambda b, h: (b, 0, h)),
                  pl.BlockSpec(blk, lambda b, h: (b, 0, nhp + h)),
                  pl.BlockSpec(blk, lambda b, h: (b, 0, 2 * nhp + h)),
                  pl.BlockSpec(blk, lambda b, h: (b, 0, h)),
                  pl.BlockSpec(blk, lambda b, h: (b, 0, nhp + h)),
                  pl.BlockSpec(blk, lambda b, h: (b, 0, 3 * nhp + h)),
                  pl.BlockSpec((n_groups + 1, V7X_LANES), lambda b, h: (0, 0)),
                  tab, tab, tab,
                  pl.BlockSpec((V7X_LANES, V7X_LANES), lambda b, h: (0, 0))],
        out_specs=[out_spec, out_spec, out_spec],
        out_shape=[jax.ShapeDtypeStruct((bsz, seq, d), F32), jax.ShapeDtypeStruct((bsz, seq, d), F32),
                   jax.ShapeDtypeStruct((bsz, seq, d), BF16)],
        scratch_shapes=[pltpu.VMEM((seq, V7X_LANES), BF16), pltpu.VMEM((SPAN + seq, V7X_LANES), BF16),
                        pltpu.VMEM((SPAN + seq, V7X_LANES), BF16), nat, nat, nat, nat, nat, nat, nat, nat, nat,
                        pltpu.VMEM((2, 2 * SPAN, 2 * SPAN), F32)],
        compiler_params=_params(2),
    )(pview, pview, pview, kview, kview, pview, gains, *tables, ones)
    return o.reshape(t, d), lse.reshape(t, d), ao.reshape(t, d)


def _attn_bwd(proj_b, kv, dao, o, lse, gains, tables, ones, bsz, seq, name):
    t, d4 = proj_b.shape
    d = d4 // 4
    nhp = d // V7X_LANES
    nblk = seq // SPAN
    scale = HEAD_DIM ** -0.5
    n_groups = len(DILATIONS)
    n_chunks = seq // ATTN_PIECE

    def body(q_ref, k_ref, v_ref, gate_ref, dao_ref, o_ref, l_ref, gain_ref, ta_ref, tb_ref, tc_ref, e_ref,
             dproj_ref, dkv_ref, dg_ref, qd, kd, vd, dod, ld, deld, dqd, dkd, dvd, dqn, dkn, dvn, kn, bias):
        head0 = lax.broadcasted_iota(jnp.int32, (SPAN, V7X_LANES), 1) < HEAD_DIM
        g = pl.program_id(2)

        @pl.when(jnp.logical_and(jnp.logical_and(pl.program_id(0) == 0, pl.program_id(1) == 0), g == 0))
        def _():
            _set_bias(bias)
            dg_ref[...] = jnp.zeros_like(dg_ref)

        @pl.when(g == 0)
        def _():
            dkn[...] = jnp.zeros_like(dkn)
            dvn[...] = jnp.zeros_like(dvn)
            _norm_rope_rows(kn, k_ref, gain_ref[n_groups:n_groups + 1, :], ta_ref, tb_ref, tc_ref, e_ref, seq)

        def norm_bwd_chunks(x_ref, dn_ref, out_ref, gi):
            def chunk(ci, carry):
                rows = pl.ds(pl.multiple_of(ci * ATTN_PIECE, ATTN_PIECE), ATTN_PIECE)
                dx, part = _norm_rope_bwd(x_ref[rows, :], dn_ref[rows, :], gain_ref[gi:gi + 1, :],
                                          ta_ref[rows, :], tb_ref[rows, :], tc_ref[rows, :], e_ref)
                out_ref[rows, :] = dx.astype(BF16)
                dg_ref[gi] += part
                return carry
            lax.fori_loop(0, n_chunks, chunk, 0, unroll=ATTN_UNROLL)

        def group(gi):
            dil = DILATIONS[gi]
            nb = seq // dil // SPAN
            kd[0:SPAN, :] = jnp.zeros((SPAN, V7X_LANES), BF16)
            vd[0:SPAN, :] = jnp.zeros((SPAN, V7X_LANES), BF16)
            dkd[...] = jnp.zeros_like(dkd)
            dvd[...] = jnp.zeros_like(dvd)
            for ns, rs, rows in _pieces(dil, seq):
                ta, tb, tc = (_strided(r_, ns, rows, dil) for r_ in (ta_ref, tb_ref, tc_ref))
                qd[rs:rs + rows, :] = _norm_rope(_strided(q_ref, ns, rows, dil), gain_ref[gi:gi + 1, :],
                                                 ta, tb, tc, e_ref).astype(BF16)
                kd[SPAN + rs:SPAN + rs + rows, :] = _strided(kn, ns, rows, dil).astype(BF16)
                vd[SPAN + rs:SPAN + rs + rows, :] = _strided(v_ref, ns, rows, dil).astype(BF16)
                gate = _strided(gate_ref, ns, rows, dil)
                dov = _strided(dao_ref, ns, rows, dil) * gate * _sig(gate)
                dod[rs:rs + rows, :] = dov.astype(BF16)
                deld[rs:rs + rows, :] = _segsum(dov * _strided(o_ref, ns, rows, dil), e_ref)
                ld[rs:rs + rows, :] = _strided(l_ref, ns, rows, dil)

            def block(j, carry):
                qs = pl.multiple_of(j * SPAN, SPAN)
                q2 = _stack_heads(qd[pl.ds(qs, SPAN), :], head0)
                do2 = _stack_heads(dod[pl.ds(qs, SPAN), :], head0)
                kk = kd[pl.ds(qs, 2 * SPAN), :]
                vv = vd[pl.ds(qs, 2 * SPAN), :]
                s = _nt(q2, kk) * scale + bias[jnp.minimum(j & (nb - 1), 1)]
                p = jnp.exp(s - _head_cols(ld[pl.ds(qs, SPAN), :]))
                ds = (p * (_nt(do2, vv) - _head_cols(deld[pl.ds(qs, SPAN), :])) * scale).astype(BF16)
                dqd[pl.ds(qs, SPAN), :] = _unstack_heads(jnp.dot(ds, kk, preferred_element_type=F32), head0)
                dkd[pl.ds(qs, 2 * SPAN), :] += _tn(ds, q2)
                dvd[pl.ds(qs, 2 * SPAN), :] += _tn(p.astype(BF16), do2)
                return carry

            lax.fori_loop(0, nblk, block, 0, unroll=ATTN_UNROLL)
            for ns, rs, rows in _pieces(dil, seq):
                _strided_set(dqn, ns, rows, dil, dqd[rs:rs + rows, :])
                _strided_set(dkn, ns, rows, dil,
                             _strided(dkn, ns, rows, dil) + dkd[SPAN + rs:SPAN + rs + rows, :])
                _strided_set(dvn, ns, rows, dil,
                             _strided(dvn, ns, rows, dil) + dvd[SPAN + rs:SPAN + rs + rows, :])
            norm_bwd_chunks(q_ref, dqn, dproj_ref, gi)

        for gi in range(n_groups):
            @pl.when(g == gi)
            def _():
                group(gi)

        @pl.when(g == n_groups - 1)
        def _():
            norm_bwd_chunks(k_ref, dkn, dkv_ref, n_groups)

        @pl.when(g == n_groups)
        def _():
            def chunk(ci, carry):
                rows = pl.ds(pl.multiple_of(ci * ATTN_PIECE, ATTN_PIECE), ATTN_PIECE)
                gate = gate_ref[rows, :]
                sg = _sig(gate)
                dproj_ref[rows, :] = (dao_ref[rows, :] * o_ref[rows, :]
                                      * (sg * (1.0 + gate * (1.0 - sg)))).astype(BF16)
                dkv_ref[rows, :] = dvn[rows, :].astype(BF16)
                return carry
            lax.fori_loop(0, n_chunks, chunk, 0, unroll=ATTN_UNROLL)

    blk = (None, seq, V7X_LANES)
    pview = proj_b.reshape(bsz, seq, d4)
    kview = kv.reshape(bsz, seq, 2 * d)
    dview = (bsz, seq, d)
    d_spec = pl.BlockSpec(blk, lambda b, h, g: (b, 0, h))
    tab = pl.BlockSpec((seq, V7X_LANES), lambda b, h, g: (0, 0))
    nat = pltpu.VMEM((seq, V7X_LANES), F32)
    natb = pltpu.VMEM((seq, V7X_LANES), BF16)
    pad = pltpu.VMEM((SPAN + seq, V7X_LANES), F32)
    padb = pltpu.VMEM((SPAN + seq, V7X_LANES), BF16)
    dproj, dkv, dg = pl.pallas_call(
        body, name=name, grid=(bsz, nhp, n_groups + 1),
        in_specs=[pl.BlockSpec(blk, lambda b, h, g: (b, 0, jnp.minimum(g, n_groups - 1) * nhp + h)),
                  pl.BlockSpec(blk, lambda b, h, g: (b, 0, h)),
                  pl.BlockSpec(blk, lambda b, h, g: (b, 0, nhp + h)),
                  pl.BlockSpec(blk, lambda b, h, g: (b, 0, n_groups * nhp + h)),
                  d_spec, d_spec, d_spec,
                  pl.BlockSpec((n_groups + 1, V7X_LANES), lambda b, h, g: (0, 0)),
                  tab, tab, tab,
                  pl.BlockSpec((V7X_LANES, V7X_LANES), lambda b, h, g: (0, 0))],
        out_specs=[pl.BlockSpec(blk, lambda b, h, g: (b, 0, g * nhp + h)),
                   pl.BlockSpec(blk, lambda b, h, g: (b, 0, (g // n_groups) * nhp + h)),
                   pl.BlockSpec((n_groups + 1, V7X_SUBLANES, V7X_LANES), lambda b, h, g: (0, 0, 0))],
        out_shape=[jax.ShapeDtypeStruct((bsz, seq, d4), BF16), jax.ShapeDtypeStruct((bsz, seq, 2 * d), BF16),
                   jax.ShapeDtypeStruct((n_groups + 1, V7X_SUBLANES, V7X_LANES), F32)],
        scratch_shapes=[natb, padb, padb, natb, nat, nat, nat, pad, pad, nat, nat, nat, nat,
                        pltpu.VMEM((2, 2 * SPAN, 2 * SPAN), F32)],
        compiler_params=_params(3),
    )(pview, kview, kview, pview, dao.reshape(dview), o.reshape(dview), lse.reshape(dview), gains, *tables, ones)
    dgain = dg.sum(axis=1).reshape(n_groups + 1, V7X_LANES // HEAD_DIM, HEAD_DIM).sum(axis=1)
    return dproj.reshape(t, d4), dkv.reshape(t, 2 * d), dgain


def _mesh_position():
    x, y, c = lax.axis_index("x"), lax.axis_index("y"), lax.axis_index("c")
    return x, y, c


def _peer(x, y, c, rel):
    return (1 - x if rel & 4 else x, 1 - y if rel & 2 else y, 1 - c if rel & 1 else c)


def _all_gather(arrs, name):
    n = len(arrs)

    def body(*refs):
        ins, outs = refs[:n], refs[n:2 * n]
        send_sems, recv_sems, local_sems = refs[2 * n:]
        x, y, c = _mesh_position()
        me = 4 * x + 2 * y + c
        copies, locals_ = [], []
        for a in range(n):
            loc = pltpu.make_async_copy(ins[a], outs[a].at[me], local_sems.at[a])
            loc.start()
            locals_.append(loc)
            for rel in range(1, N_DEV):
                s = a * (N_DEV - 1) + rel - 1
                cp = pltpu.make_async_remote_copy(
                    src_ref=ins[a], dst_ref=outs[a].at[me], send_sem=send_sems.at[s], recv_sem=recv_sems.at[s],
                    device_id=_peer(x, y, c, rel), device_id_type=pl.DeviceIdType.MESH)
                cp.start()
                copies.append(cp)
        for cp in copies:
            cp.wait_recv()
        for cp in copies:
            cp.wait_send()
        for loc in locals_:
            loc.wait()

    hbm = pl.BlockSpec(memory_space=pltpu.HBM)
    return pl.pallas_call(
        body, name=name, in_specs=[hbm] * n, out_specs=[hbm] * n,
        out_shape=[jax.ShapeDtypeStruct((N_DEV,) + a.shape, a.dtype) for a in arrs],
        scratch_shapes=[pltpu.SemaphoreType.DMA((n * (N_DEV - 1),)), pltpu.SemaphoreType.DMA((n * (N_DEV - 1),)),
                        pltpu.SemaphoreType.DMA((n,))],
    )(*arrs)


def _exchange_slots(pack, name):
    def body(p_ref, o_ref, send_sems, recv_sems, local_sem):
        x, y, c = _mesh_position()
        me = 4 * x + 2 * y + c
        loc = pltpu.make_async_copy(p_ref.at[me], o_ref.at[me], local_sem)
        loc.start()
        copies = []
        for rel in range(1, N_DEV):
            px, py, pc = _peer(x, y, c, rel)
            cp = pltpu.make_async_remote_copy(
                src_ref=p_ref.at[4 * px + 2 * py + pc], dst_ref=o_ref.at[me],
                send_sem=send_sems.at[rel - 1], recv_sem=recv_sems.at[rel - 1],
                device_id=(px, py, pc), device_id_type=pl.DeviceIdType.MESH)
            cp.start()
            copies.append(cp)
        for cp in copies:
            cp.wait_recv()
        for cp in copies:
            cp.wait_send()
        loc.wait()

    hbm = pl.BlockSpec(memory_space=pltpu.HBM)
    return pl.pallas_call(
        body, name=name, in_specs=[hbm], out_specs=hbm,
        out_shape=jax.ShapeDtypeStruct(pack.shape, pack.dtype),
        scratch_shapes=[pltpu.SemaphoreType.DMA((N_DEV - 1,)), pltpu.SemaphoreType.DMA((N_DEV - 1,)),
                        pltpu.SemaphoreType.DMA],
    )(pack)


def _sum_adamw(parts, w, m, v, name):
    _, r, wd = parts.shape
    tr = ADAM_ROWS
    c1 = 1.0 - ADAM_B1 ** ADAM_STEP
    c2 = 1.0 - ADAM_B2 ** ADAM_STEP

    def body(p_ref, w_ref, m_ref, v_ref, g_ref, d_ref, nm_ref, nv_ref):
        g = p_ref[0].astype(F32)
        for s in range(1, N_DEV):
            g = g + p_ref[s].astype(F32)
        nm = ADAM_B1 * m_ref[...] + (1.0 - ADAM_B1) * g
        nv = ADAM_B2 * v_ref[...] + (1.0 - ADAM_B2) * (g * g)
        g_ref[...] = g
        nm_ref[...] = nm
        nv_ref[...] = nv
        d_ref[...] = -ADAM_LR * ((nm / c1) / (jnp.sqrt(nv / c2) + ADAM_EPS) + ADAM_WD * w_ref[...])

    row = pl.BlockSpec((tr, wd), lambda i: (i, 0))
    return pl.pallas_call(
        body, name=name, grid=(r // tr,),
        in_specs=[pl.BlockSpec((N_DEV, tr, wd), lambda i: (0, i, 0)), row, row, row],
        out_specs=[row] * 4, out_shape=[jax.ShapeDtypeStruct((r, wd), F32)] * 4,
        compiler_params=_params(1),
    )(parts, w, m, v)


def _pack_rows(size, row_mult):
    rows = -(-size // PACK_LANES)
    return -(-rows // row_mult) * row_mult


def _pack(flats, row_mult, dtype, total_mult=None):
    out = []
    for f in flats:
        size = f.shape[-1]
        rows = _pack_rows(size, row_mult)
        pad = [(0, 0)] * (f.ndim - 1) + [(0, rows * PACK_LANES - size)]
        out.append(jnp.pad(f.astype(dtype), pad).reshape(f.shape[:-1] + (rows, PACK_LANES)))
    if total_mult is not None:
        total = sum(o.shape[-2] for o in out)
        extra = -(-total // total_mult) * total_mult - total
        if extra:
            out.append(jnp.zeros(out[0].shape[:-2] + (extra, PACK_LANES), dtype))
    return jnp.concatenate(out, axis=-2)


def _unpack(buf, sizes, row_mult):
    out, row = [], 0
    for size in sizes:
        rows = _pack_rows(size, row_mult)
        part = buf[..., row:row + rows, :]
        out.append(part.reshape(buf.shape[:-2] + (rows * PACK_LANES,))[..., :size])
        row += rows
    return out


def _to_slots(full, axis):
    if axis is None:
        return jnp.broadcast_to(full.reshape(1, -1), (N_DEV, full.size))
    shape = full.shape
    split = full.reshape(shape[:axis] + (N_DEV, shape[axis] // N_DEV) + shape[axis + 1:])
    return jnp.moveaxis(split, axis, 0).reshape(N_DEV, -1)


def _from_slots(slots, axis, block_shape):
    split = jnp.moveaxis(slots.reshape((N_DEV,) + tuple(block_shape)), 0, axis)
    shape = list(block_shape)
    shape[axis] *= N_DEV
    return split.reshape(shape)


def kernel(x, p, norm_g, w_in_a, conv_w, conv_b, ln_g, ln_b, w_out_a, kv_norm_g, w_kv, k_norm_g, w_in_b, q_norm_g, w_out_b, ple_norm_g, w_ple_gate, w_ple_proj, loss_target, m_norm_g, m_w_in_a, m_conv_w, m_conv_b, m_ln_g, m_ln_b, m_w_out_a, m_kv_norm_g, m_w_kv, m_k_norm_g, m_w_in_b, m_q_norm_g, m_w_out_b, m_ple_norm_g, m_w_ple_gate, m_w_ple_proj, v_norm_g, v_w_in_a, v_conv_w, v_conv_b, v_ln_g, v_ln_b, v_w_out_a, v_kv_norm_g, v_w_kv, v_k_norm_g, v_w_in_b, v_q_norm_g, v_w_out_b, v_ple_norm_g, v_w_ple_gate, v_w_ple_proj):
    weights = dict(zip(WEIGHT_NAMES, (norm_g, w_in_a, conv_w, conv_b, ln_g, ln_b, w_out_a, kv_norm_g, w_kv, k_norm_g,
                                      w_in_b, q_norm_g, w_out_b, ple_norm_g, w_ple_gate, w_ple_proj)))
    mom_m = dict(zip(WEIGHT_NAMES, (m_norm_g, m_w_in_a, m_conv_w, m_conv_b, m_ln_g, m_ln_b, m_w_out_a, m_kv_norm_g,
                                    m_w_kv, m_k_norm_g, m_w_in_b, m_q_norm_g, m_w_out_b, m_ple_norm_g, m_w_ple_gate,
                                    m_w_ple_proj)))
    mom_v = dict(zip(WEIGHT_NAMES, (v_norm_g, v_w_in_a, v_conv_w, v_conv_b, v_ln_g, v_ln_b, v_w_out_a, v_kv_norm_g,
                                    v_w_kv, v_k_norm_g, v_w_in_b, v_q_norm_g, v_w_out_b, v_ple_norm_g, v_w_ple_gate,
                                    v_w_ple_proj)))
    bsz, seq, d = x.shape
    t = bsz * seq
    assert seq % (max(DILATIONS) * SPAN) == 0 and d % V7X_LANES == 0

    wq_pack = _pack([weights[n].reshape(-1) for n in MATMUL_WEIGHTS], 16, BF16)
    wv_pack = _pack([weights[n].reshape(-1) for n in VECTOR_WEIGHTS], 8, F32)
    wq_all, wv_all = _all_gather([wq_pack, wv_pack], "gather_weights")
    full = {}
    for names, buf, mult in ((MATMUL_WEIGHTS, wq_all, 16), (VECTOR_WEIGHTS, wv_all, 8)):
        for n, slots in zip(names, _unpack(buf, [weights[n].size for n in names], mult)):
            full[n] = _from_slots(slots, SHARD_AXIS[n], weights[n].shape)
    wa_in, wa_out = full['w_in_a'][0], full['w_out_a'][0]
    wkv = full['w_kv']
    wb_in, wb_out = full['w_in_b'][0], full['w_out_b'][0]
    wg, wp = full['w_ple_gate'], full['w_ple_proj']
    cw, cb, lg, lb = full['conv_w'][0], full['conv_b'], full['ln_g'], full['ln_b']

    tables = _rope_tables(seq)
    ones = _head_ones(V7X_LANES)
    rep = V7X_LANES // HEAD_DIM
    head_gain = jnp.concatenate([jnp.tile(q_norm_g[0], (1, rep)), jnp.tile(k_norm_g, rep)[None]], axis=0)

    x0 = x.reshape(t, d)
    p0, p1 = p[0].reshape(t, -1), p[1].reshape(t, -1)
    target = loss_target.reshape(t, d)
    g_norm0, g_norm1 = norm_g[0:1], norm_g[1:2]
    g_ple0, g_ple1 = ple_norm_g[0:1], ple_norm_g[1:2]
    g_kv = kv_norm_g.reshape(1, d)

    (u0,) = _rmsnorm_fwd(x0, [g_norm0], "norm0")
    proj_a = _matmul(u0, wa_in, 'nn', "in_a")
    m_act, y_conv = _conv_fwd(proj_a, cw, cb, lg, lb, seq, "conv_fwd")
    h0 = _matmul(m_act, wa_out, 'nn', "out_a", add=x0)
    (pg0,) = _rmsnorm_fwd(h0, [g_ple0], "ple_norm0")
    gl0 = _matmul(pg0, wg[0], 'nn', "ple_gate0")
    pp0 = _matmul(p0, wp[0], 'nn', "ple_proj0")
    x1 = _ple_fwd(h0, gl0, pp0, "ple0")

    kvn, u1 = _rmsnorm_fwd(x1, [g_kv, g_norm1], "norm1")
    kv = _matmul(kvn, wkv, 'nn', "kv")
    proj_b = _matmul(u1, wb_in, 'nn', "in_b")
    o_att, lse, ao = _attn_fwd(proj_b, kv, head_gain, tables, ones, bsz, seq, "attn_fwd")
    h1 = _matmul(ao, wb_out, 'nn', "out_b", add=x1)
    (pg1,) = _rmsnorm_fwd(h1, [g_ple1], "ple_norm1")
    gl1 = _matmul(pg1, wg[1], 'nn', "ple_gate1")
    pp1 = _matmul(p1, wp[1], 'nn', "ple_proj1")
    x2 = _ple_fwd(h1, gl1, pp1, "ple1")

    dx2, loss_part = _loss_fwd_bwd(x2, target, "loss")
    loss = lax.psum(jnp.sum(loss_part), ("x", "y", "c"))

    grads = {}

    dgl1, dpp1 = _ple_bwd(dx2, gl1, pp1, "ple1_bwd")
    dwp1 = _matmul(p1, dpp1, 'tn', "d_ple_proj1")
    dwg1 = _matmul(pg1, dgl1, 'tn', "d_ple_gate1")
    dpg1 = _matmul(dgl1, wg[1], 'nt', "d_ple_norm1")
    dh1, (dg_ple1,) = _rmsnorm_bwd(h1, [g_ple1], [dpg1], dx2, "ple_norm1_bwd")
    grads['w_out_b'] = _matmul(ao, dh1, 'tn', "d_out_b")[None]
    dao = _matmul(dh1, wb_out, 'nt', "d_ao")
    dproj_b, dkv, dg_head = _attn_bwd(proj_b, kv, dao, o_att, lse, head_gain, tables, ones, bsz, seq, "attn_bwd")
    grads['w_in_b'] = _matmul(u1, dproj_b, 'tn', "d_in_b")[None]
    du1 = _matmul(dproj_b, wb_in, 'nt', "d_u1")
    grads['w_kv'] = _matmul(kvn, dkv, 'tn', "d_kv")
    dkvn = _matmul(dkv, wkv, 'nt', "d_kvn")
    dx1, (dg_kv, dg_norm1) = _rmsnorm_bwd(x1, [g_kv, g_norm1], [dkvn, du1], dh1, "norm1_bwd")

    dgl0, dpp0 = _ple_bwd(dx1, gl0, pp0, "ple0_bwd")
    dwp0 = _matmul(p0, dpp0, 'tn', "d_ple_proj0")
    dwg0 = _matmul(pg0, dgl0, 'tn', "d_ple_gate0")
    dpg0 = _matmul(dgl0, wg[0], 'nt', "d_ple_norm0")
    dh0, (dg_ple0,) = _rmsnorm_bwd(h0, [g_ple0], [dpg0], dx1, "ple_norm0_bwd")
    grads['w_out_a'] = _matmul(m_act, dh0, 'tn', "d_out_a")[None]
    dm = _matmul(dh0, wa_out, 'nt', "d_m")
    dy_conv, dz, d_lg, d_lb, d_cb = _ln_gate_bwd(dm, y_conv, proj_a, lg, lb, "ln_gate_bwd")
    dproj_a, d_cw = _conv_bwd(dy_conv, dz, proj_a, cw, seq, "conv_bwd")
    grads['w_in_a'] = _matmul(u0, dproj_a, 'tn', "d_in_a")[None]
    du0 = _matmul(dproj_a, wa_in, 'nt', "d_u0")
    dx0, (dg_norm0,) = _rmsnorm_bwd(x0, [g_norm0], [du0], dh0, "norm0_bwd")

    grads['norm_g'] = jnp.stack([dg_norm0, dg_norm1])
    grads['conv_w'] = d_cw[None]
    grads['conv_b'] = d_cb[None]
    grads['ln_g'] = d_lg[None]
    grads['ln_b'] = d_lb[None]
    grads['kv_norm_g'] = dg_kv
    grads['k_norm_g'] = dg_head[3]
    grads['q_norm_g'] = dg_head[0:3][None]
    grads['ple_norm_g'] = jnp.stack([dg_ple0, dg_ple1])
    grads['w_ple_gate'] = jnp.stack([dwg0, dwg1])
    grads['w_ple_proj'] = jnp.stack([dwp0, dwp1])

    sizes = [weights[n].size for n in WEIGHT_NAMES]
    g_pack = _pack([_to_slots(grads[n], SHARD_AXIS[n]) for n in WEIGHT_NAMES], 16, BF16, ADAM_ROWS)
    parts = _exchange_slots(g_pack, "exchange_grads")
    w_pack = _pack([weights[n].reshape(-1) for n in WEIGHT_NAMES], 16, F32, ADAM_ROWS)
    m_pack = _pack([mom_m[n].reshape(-1) for n in WEIGHT_NAMES], 16, F32, ADAM_ROWS)
    v_pack = _pack([mom_v[n].reshape(-1) for n in WEIGHT_NAMES], 16, F32, ADAM_ROWS)
    outs = _sum_adamw(parts, w_pack, m_pack, v_pack, "sum_adamw")
    result = [loss, dx0.reshape(bsz, seq, d)]
    for buf in outs:
        for n, flat in zip(WEIGHT_NAMES, _unpack(buf, sizes, 16)):
            result.append(flat.reshape(weights[n].shape))
    return tuple(result)
```

```python
import functools

import jax
import jax.numpy as jnp
from jax import lax
from jax.experimental import pallas as pl
from jax.experimental.pallas import tpu as pltpu

F32 = jnp.float32
BF16 = jnp.bfloat16

N_DEV = 8
HEAD_DIM = 64
ROPE_DIM = 16
ROPE_THETA = 500000.0
EPS = 1e-6
NEG_INF = -1e30
SPAN = 128
DILATIONS = (1, 4, 16)
CONV_WIDTH = 31
HALO = 32
PACK_LANES = 1024
V7X_LANES = 128
V7X_SUBLANES = 8
VMEM_LIMIT_BYTES = 56 * 1024 * 1024

ADAM_LR = 0.001
ADAM_B1 = 0.9
ADAM_B2 = 0.999
ADAM_EPS = 1e-08
ADAM_WD = 0.01
ADAM_STEP = 10
ADAM_ROWS = 128

WEIGHT_NAMES = ('norm_g', 'w_in_a', 'conv_w', 'conv_b', 'ln_g', 'ln_b', 'w_out_a', 'kv_norm_g', 'w_kv',
                'k_norm_g', 'w_in_b', 'q_norm_g', 'w_out_b', 'ple_norm_g', 'w_ple_gate', 'w_ple_proj')
SHARD_AXIS = {'norm_g': None, 'w_in_a': 2, 'conv_w': 2, 'conv_b': 1, 'ln_g': 1, 'ln_b': 1, 'w_out_a': 1,
              'kv_norm_g': None, 'w_kv': 1, 'k_norm_g': None, 'w_in_b': 2, 'q_norm_g': None, 'w_out_b': 1,
              'ple_norm_g': None, 'w_ple_gate': 1, 'w_ple_proj': 2}
VECTOR_WEIGHTS = ('conv_w', 'conv_b', 'ln_g', 'ln_b')
GROUP_FIRST = ('w_in_a',)
GROUP_REST = ('w_out_a', 'w_kv', 'w_in_b', 'w_out_b', 'w_ple_gate', 'w_ple_proj')
GROUP_SMALL = ('norm_g', 'conv_w', 'conv_b', 'ln_g', 'ln_b', 'kv_norm_g', 'k_norm_g', 'q_norm_g', 'ple_norm_g')


def _pick(n, target, mult):
    t = (min(target, n) // mult) * mult
    while t >= mult:
        if n % t == 0:
            return t
        t -= mult
    return n


def _params(n_grid):
    return pltpu.CompilerParams(dimension_semantics=("arbitrary",) * n_grid, vmem_limit_bytes=VMEM_LIMIT_BYTES)


def _sig(x):
    return 1.0 / (1.0 + jnp.exp(-x))


def _colsum8(v):
    r, w = v.shape
    return v.reshape(r // V7X_SUBLANES, V7X_SUBLANES, w).sum(axis=0)


def _rows(tm, w, col=0):
    return pl.BlockSpec((tm, w), lambda i: (i, col))


def _const(shape):
    nd = len(shape)
    return pl.BlockSpec(shape, lambda i: (0,) * nd)


def _segsum(v, e_ref):
    hi = v.astype(BF16)
    lo = (v - hi.astype(F32)).astype(BF16)
    e = e_ref[...]
    return jnp.dot(hi, e, preferred_element_type=F32) + jnp.dot(lo, e, preferred_element_type=F32)


def _matmul(a, b, mode, name, out_dtype=F32, add=None, ex=None):
    if mode == 'nn':
        (m, k), (_, n) = a.shape, b.shape
    elif mode == 'nt':
        (m, k), (n, _) = a.shape, b.shape
    else:
        (k, m), (_, n) = a.shape, b.shape
    if mode == 'tn':
        tm, tn, tk = _pick(m, 1024, 128), _pick(n, 1024, 128), _pick(k, 512, 128)
        grid = (m // tm, n // tn, k // tk)
        a_spec = pl.BlockSpec((tk, tm), lambda i, j, kk: (kk, i))
        b_spec = pl.BlockSpec((tk, tn), lambda i, j, kk: (kk, j))
        o_spec = pl.BlockSpec((tm, tn), lambda i, j, kk: (i, j))
        dims = (((0,), (0,)), ((), ()))
    else:
        tm, tn, tk = _pick(m, 512, 128), _pick(n, 1024, 128), _pick(k, 1024, 128)
        grid = (n // tn, m // tm, k // tk)
        a_spec = pl.BlockSpec((tm, tk), lambda j, i, kk: (i, kk))
        o_spec = pl.BlockSpec((tm, tn), lambda j, i, kk: (i, j))
        if mode == 'nn':
            b_spec = pl.BlockSpec((tk, tn), lambda j, i, kk: (kk, j))
            dims = (((1,), (0,)), ((), ()))
        else:
            b_spec = pl.BlockSpec((tn, tk), lambda j, i, kk: (j, kk))
            dims = (((1,), (1,)), ((), ()))
    nk = grid[2]
    has_add = add is not None

    def body(*refs):
        a_ref, b_ref = refs[0], refs[1]
        add_ref = refs[2] if has_add else None
        o_ref = refs[2 + has_add]
        part = lax.dot_general(a_ref[...].astype(BF16), b_ref[...].astype(BF16), dims, preferred_element_type=F32)

        def finish(total):
            if has_add:
                total = total + add_ref[...]
            o_ref[...] = total.astype(out_dtype)

        if nk == 1:
            finish(part)
        else:
            acc_ref = refs[3 + has_add]
            kk = pl.program_id(2)

            @pl.when(kk == 0)
            def _():
                acc_ref[...] = part

            @pl.when(kk > 0)
            def _():
                acc_ref[...] += part

            @pl.when(kk == nk - 1)
            def _():
                finish(acc_ref[...])

    in_specs = [a_spec, b_spec] + ([o_spec] if has_add else [])
    args = [a, b] + ([add] if has_add else [])
    scratch = [pltpu.VMEM((tm, tn), F32)] if nk > 1 else []
    (out,), moved = _hosted_call(body, ex, name, grid, in_specs, [o_spec],
                                 [jax.ShapeDtypeStruct((m, n), out_dtype)], scratch, args)
    return out if ex is None else (out, moved)


def _rmsnorm_fwd(x, gains, name):
    t, d = x.shape
    tm = _pick(t, 512, 8)
    n = len(gains)

    def body(*refs):
        x_ref, g_refs, o_refs = refs[0], refs[1:1 + n], refs[1 + n:]
        xv = x_ref[...]
        y = xv * lax.rsqrt(jnp.mean(xv * xv, axis=-1, keepdims=True) + EPS)
        for g_ref, o_ref in zip(g_refs, o_refs):
            o_ref[...] = (y * g_ref[...]).astype(BF16)

    return pl.pallas_call(
        body, name=name, grid=(t // tm,),
        in_specs=[_rows(tm, d)] + [_const((1, d))] * n,
        out_specs=[_rows(tm, d)] * n,
        out_shape=[jax.ShapeDtypeStruct((t, d), BF16)] * n,
        compiler_params=_params(1),
    )(x, *gains)


def _rmsnorm_bwd(x, gains, dys, add, name):
    t, d = x.shape
    tm = _pick(t, 512, 8)
    n = len(gains)

    def body(*refs):
        x_ref, add_ref = refs[0], refs[1]
        g_refs, dy_refs = refs[2:2 + n], refs[2 + n:2 + 2 * n]
        dx_ref, dg_refs = refs[2 + 2 * n], refs[3 + 2 * n:]
        i = pl.program_id(0)
        xv = x_ref[...]
        r = lax.rsqrt(jnp.mean(xv * xv, axis=-1, keepdims=True) + EPS)
        xhat = xv * r
        dx = add_ref[...]
        for g_ref, dy_ref, dg_ref in zip(g_refs, dy_refs, dg_refs):
            dy = dy_ref[...]
            dyg = dy * g_ref[...]
            dx = dx + r * (dyg - xhat * jnp.mean(dyg * xhat, axis=-1, keepdims=True))
            part = _colsum8(dy * xhat)

            @pl.when(i == 0)
            def _():
                dg_ref[...] = part

            @pl.when(i > 0)
            def _():
                dg_ref[...] += part

        dx_ref[...] = dx

    outs = pl.pallas_call(
        body, name=name, grid=(t // tm,),
        in_specs=[_rows(tm, d), _rows(tm, d)] + [_const((1, d))] * n + [_rows(tm, d)] * n,
        out_specs=[_rows(tm, d)] + [_const((V7X_SUBLANES, d))] * n,
        out_shape=[jax.ShapeDtypeStruct((t, d), F32)] + [jax.ShapeDtypeStruct((V7X_SUBLANES, d), F32)] * n,
        compiler_params=_params(1),
    )(x, add, *gains, *dys)
    return outs[0], [o.sum(axis=0) for o in outs[1:]]


def _ple_fwd(h, gl, pp, name):
    t, d = h.shape
    tm = _pick(t, 512, 8)

    def body(h_ref, gl_ref, pp_ref, o_ref):
        o_ref[...] = h_ref[...] + _sig(gl_ref[...]) * pp_ref[...]

    return pl.pallas_call(
        body, name=name, grid=(t // tm,), in_specs=[_rows(tm, d)] * 3, out_specs=_rows(tm, d),
        out_shape=jax.ShapeDtypeStruct((t, d), F32), compiler_params=_params(1),
    )(h, gl, pp)


def _ple_bwd(dx, gl, pp, name):
    t, d = dx.shape
    tm = _pick(t, 512, 8)

    def body(dx_ref, gl_ref, pp_ref, dgl_ref, dpp_ref):
        dxv = dx_ref[...]
        sg = _sig(gl_ref[...])
        dpp_ref[...] = (dxv * sg).astype(BF16)
        dgl_ref[...] = (dxv * pp_ref[...] * sg * (1.0 - sg)).astype(BF16)

    return pl.pallas_call(
        body, name=name, grid=(t // tm,), in_specs=[_rows(tm, d)] * 3, out_specs=[_rows(tm, d)] * 2,
        out_shape=[jax.ShapeDtypeStruct((t, d), BF16)] * 2, compiler_params=_params(1),
    )(dx, gl, pp)


def _loss_fwd_bwd(y, target, name):
    t, d = y.shape
    tm = _pick(t, 512, 8)
    inv_d = 1.0 / d

    def body(y_ref, t_ref, dy_ref, l_ref):
        i = pl.program_id(0)
        e = y_ref[...] - t_ref[...]
        dy_ref[...] = e * inv_d
        part = _colsum8(e * e) * (0.5 * inv_d)

        @pl.when(i == 0)
        def _():
            l_ref[...] = part

        @pl.when(i > 0)
        def _():
            l_ref[...] += part

    return pl.pallas_call(
        body, name=name, grid=(t // tm,), in_specs=[_rows(tm, d)] * 2,
        out_specs=[_rows(tm, d), _const((V7X_SUBLANES, d))],
        out_shape=[jax.ShapeDtypeStruct((t, d), F32), jax.ShapeDtypeStruct((V7X_SUBLANES, d), F32)],
        compiler_params=_params(1),
    )(y, target)


def _shift_scratch(ts, cc):
    return pltpu.VMEM((V7X_SUBLANES, ts + HALO - V7X_SUBLANES, cc), F32)


def _shifted_copies(sh_ref, win_ref, cs, ts):
    rows = ts + HALO - V7X_SUBLANES
    for s in range(1, V7X_SUBLANES):
        sh_ref[s] = win_ref[pl.ds(s, rows), cs]


def _tap(sh_ref, win_ref, cs, offset, ts):
    s = offset % V7X_SUBLANES
    if s == 0:
        return win_ref[pl.ds(offset, ts), cs]
    return sh_ref[s, pl.ds(offset - s, ts), :]


def _conv_fwd(proj, conv_w, conv_b, ln_g, ln_b, seq, name, ex=None):
    t, c3 = proj.shape
    c = c3 // 3
    ts = _pick(seq, 256, HALO)
    nsb = seq // ts
    cc = _pick(c, 512, V7X_LANES)
    hb = ts // HALO

    def body(a_ref, b_ref, z_ref, ap_ref, bp_ref, w_ref, cb_ref, g_ref, be_ref, m_ref, y_ref, win_ref, sh_ref):
        i = pl.program_id(0)
        first = (i % nsb) == 0
        win_ref[0:HALO, :] = jnp.where(first, 0.0, ap_ref[...] * _sig(bp_ref[...]))
        win_ref[HALO:, :] = a_ref[...] * _sig(b_ref[...])
        for ci in range(c // cc):
            cs = slice(ci * cc, (ci + 1) * cc)
            _shifted_copies(sh_ref, win_ref, cs, ts)
            acc = jnp.zeros((ts, cc), F32) + cb_ref[:, cs]
            for k in range(CONV_WIDTH):
                acc = acc + w_ref[k:k + 1, cs] * _tap(sh_ref, win_ref, cs, HALO - (CONV_WIDTH - 1) + k, ts)
            y_ref[:, cs] = acc
        y = y_ref[...]
        mu = jnp.mean(y, axis=-1, keepdims=True)
        xc = y - mu
        rstd = lax.rsqrt(jnp.mean(xc * xc, axis=-1, keepdims=True) + EPS)
        ln = xc * rstd * g_ref[...] + be_ref[...]
        zz = z_ref[...]
        m_ref[...] = (ln * _sig(ln) * zz * _sig(zz)).astype(BF16)

    halo_a = pl.BlockSpec((HALO, c), lambda i: (jnp.maximum(i * hb - 1, 0), 0))
    halo_b = pl.BlockSpec((HALO, c), lambda i: (jnp.maximum(i * hb - 1, 0), 1))
    (m_act, y), moved = _hosted_call(
        body, ex, name, (t // ts,),
        [_rows(ts, c, 0), _rows(ts, c, 1), _rows(ts, c, 2), halo_a, halo_b,
         _const((CONV_WIDTH, c)), _const((1, c)), _const((1, c)), _const((1, c))],
        [_rows(ts, c), _rows(ts, c)],
        [jax.ShapeDtypeStruct((t, c), BF16), jax.ShapeDtypeStruct((t, c), F32)],
        [pltpu.VMEM((HALO + ts, c), F32), _shift_scratch(ts, cc)],
        (proj, proj, proj, proj, proj, conv_w, conv_b, ln_g, ln_b))
    return m_act, y, moved


def _ln_gate_bwd(dm, y, proj, ln_g, ln_b, name):
    t, c = y.shape
    tm = _pick(t, 256, 8)

    def body(dm_ref, y_ref, z_ref, g_ref, be_ref, dy_ref, dz_ref, dg_ref, db_ref, dcb_ref):
        i = pl.program_id(0)
        yv = y_ref[...]
        mu = jnp.mean(yv, axis=-1, keepdims=True)
        xc = yv - mu
        rstd = lax.rsqrt(jnp.mean(xc * xc, axis=-1, keepdims=True) + EPS)
        xhat = xc * rstd
        g = g_ref[...]
        ln = xhat * g + be_ref[...]
        sl = _sig(ln)
        zz = z_ref[...]
        sz = _sig(zz)
        dmv = dm_ref[...]
        dz_ref[...] = (dmv * (ln * sl) * (sz * (1.0 + zz * (1.0 - sz)))).astype(BF16)
        dln = dmv * (zz * sz) * (sl * (1.0 + ln * (1.0 - sl)))
        dxh = dln * g
        dyv = rstd * (dxh - jnp.mean(dxh, axis=-1, keepdims=True)
                      - xhat * jnp.mean(dxh * xhat, axis=-1, keepdims=True))
        dy_ref[...] = dyv
        parts = (_colsum8(dln * xhat), _colsum8(dln), _colsum8(dyv))

        @pl.when(i == 0)
        def _():
            for ref, part in zip((dg_ref, db_ref, dcb_ref), parts):
                ref[...] = part

        @pl.when(i > 0)
        def _():
            for ref, part in zip((dg_ref, db_ref, dcb_ref), parts):
                ref[...] += part

    acc = jax.ShapeDtypeStruct((V7X_SUBLANES, c), F32)
    outs = pl.pallas_call(
        body, name=name, grid=(t // tm,),
        in_specs=[_rows(tm, c), _rows(tm, c), _rows(tm, c, 2), _const((1, c)), _const((1, c))],
        out_specs=[_rows(tm, c), _rows(tm, c)] + [_const((V7X_SUBLANES, c))] * 3,
        out_shape=[jax.ShapeDtypeStruct((t, c), F32), jax.ShapeDtypeStruct((t, c), BF16), acc, acc, acc],
        compiler_params=_params(1),
    )(dm, y, proj, ln_g, ln_b)
    return outs[0], outs[1], outs[2].sum(axis=0), outs[3].sum(axis=0), outs[4].sum(axis=0)


def _conv_bwd(dy, dz, proj, conv_w, seq, name, ex=None):
    t, c3 = proj.shape
    c = c3 // 3
    ts = _pick(seq, 256, HALO)
    nsb = seq // ts
    cc = _pick(c, 512, V7X_LANES)
    hb = ts // HALO
    last_halo = t // HALO - 1
    back = CONV_WIDTH - 1

    def body(dy_ref, dyn_ref, dz_ref, a_ref, b_ref, ap_ref, bp_ref, w_ref, o_ref, dw_ref, win_ref, dwin_ref,
             sh_ref, dsh_ref):
        i = pl.program_id(0)
        first = (i % nsb) == 0
        last = (i % nsb) == nsb - 1
        sb = _sig(b_ref[...])
        av = a_ref[...]
        win_ref[0:HALO, :] = jnp.where(first, 0.0, ap_ref[...] * _sig(bp_ref[...]))
        win_ref[HALO:, :] = av * sb
        dwin_ref[0:ts, :] = dy_ref[...]
        dwin_ref[ts:, :] = jnp.where(last, 0.0, dyn_ref[...])

        @pl.when(i == 0)
        def _():
            dw_ref[...] = jnp.zeros_like(dw_ref)

        for ci in range(c // cc):
            cs = slice(ci * cc, (ci + 1) * cc)
            _shifted_copies(sh_ref, win_ref, cs, ts)
            _shifted_copies(dsh_ref, dwin_ref, cs, ts)
            dcur = dwin_ref[0:ts, cs]
            dglu = jnp.zeros((ts, cc), F32)
            for k in range(CONV_WIDTH):
                dglu = dglu + w_ref[k:k + 1, cs] * _tap(dsh_ref, dwin_ref, cs, back - k, ts)
                dw_ref[k * V7X_SUBLANES:(k + 1) * V7X_SUBLANES, cs] += _colsum8(
                    dcur * _tap(sh_ref, win_ref, cs, HALO - back + k, ts))
            sbc = sb[:, cs]
            o_ref[:, cs] = (dglu * sbc).astype(BF16)
            o_ref[:, c + ci * cc:c + (ci + 1) * cc] = (dglu * av[:, cs] * sbc * (1.0 - sbc)).astype(BF16)
        o_ref[:, 2 * c:] = dz_ref[...]

    halo_next = pl.BlockSpec((HALO, c), lambda i: (jnp.minimum((i + 1) * hb, last_halo), 0))
    halo_a = pl.BlockSpec((HALO, c), lambda i: (jnp.maximum(i * hb - 1, 0), 0))
    halo_b = pl.BlockSpec((HALO, c), lambda i: (jnp.maximum(i * hb - 1, 0), 1))
    (dproj, dw), moved = _hosted_call(
        body, ex, name, (t // ts,),
        [_rows(ts, c), halo_next, _rows(ts, c), _rows(ts, c, 0), _rows(ts, c, 1), halo_a, halo_b,
         _const((CONV_WIDTH, c))],
        [_rows(ts, c3), _const((CONV_WIDTH * V7X_SUBLANES, c))],
        [jax.ShapeDtypeStruct((t, c3), BF16), jax.ShapeDtypeStruct((CONV_WIDTH * V7X_SUBLANES, c), F32)],
        [pltpu.VMEM((HALO + ts, c), F32), pltpu.VMEM((ts + HALO, c), F32),
         _shift_scratch(ts, cc), _shift_scratch(ts, cc)],
        (dy, dy, dz, proj, proj, proj, proj, conv_w))
    return dproj, dw.reshape(CONV_WIDTH, V7X_SUBLANES, c).sum(axis=1), moved


def _rope_tables(seq):
    half = ROPE_DIM // 2
    inv = ROPE_THETA ** (-jnp.arange(half, dtype=F32) * (2.0 / ROPE_DIM))
    ang = jnp.arange(seq).astype(F32)[:, None] * inv[None, :]
    cos, sin = jnp.cos(ang), jnp.sin(ang)
    zeros = jnp.zeros((seq, HEAD_DIM - ROPE_DIM), F32)
    zh = jnp.zeros((seq, half), F32)
    a = jnp.concatenate([cos, cos, zeros + 1.0], axis=1)
    b = jnp.concatenate([zh, sin, zeros], axis=1)
    c = jnp.concatenate([-sin, zh, zeros], axis=1)
    rep = V7X_LANES // HEAD_DIM
    return tuple(jnp.tile(v, (1, rep)) for v in (a, b, c))


def _head_ones(d):
    head = jnp.arange(d) // HEAD_DIM
    return (head[:, None] == head[None, :]).astype(BF16)


def _rope(ch, ta, tb, tc):
    return ta * ch + tb * pltpu.roll(ch, ROPE_DIM // 2, 1) + tc * pltpu.roll(ch, V7X_LANES - ROPE_DIM // 2, 1)


def _rope_t(ch, ta, tb, tc):
    return ta * ch + pltpu.roll(tb * ch, V7X_LANES - ROPE_DIM // 2, 1) + pltpu.roll(tc * ch, ROPE_DIM // 2, 1)


def _norm_rope(xv, gain, ta, tb, tc, e_ref):
    r = lax.rsqrt(_segsum(xv * xv, e_ref) * (1.0 / HEAD_DIM) + EPS)
    return _rope(xv * r * gain, ta, tb, tc)


def _norm_rope_bwd(xv, dout, gain, ta, tb, tc, e_ref):
    dxn = _rope_t(dout, ta, tb, tc)
    r = lax.rsqrt(_segsum(xv * xv, e_ref) * (1.0 / HEAD_DIM) + EPS)
    xhat = xv * r
    dxh = dxn * gain
    dx = r * (dxh - xhat * (_segsum(dxh * xhat, e_ref) * (1.0 / HEAD_DIM)))
    return dx, _colsum8(dxn * xhat)


def _norm_rope_rows(dst_ref, src_ref, gain, ta_ref, tb_ref, tc_ref, e_ref, seq):
    for r0 in range(0, seq, ATTN_PIECE):
        rows = slice(r0, r0 + ATTN_PIECE)
        dst_ref[rows, :] = _norm_rope(src_ref[rows, :], gain, ta_ref[rows, :], tb_ref[rows, :], tc_ref[rows, :], e_ref)


ATTN_PIECE = 256
ATTN_UNROLL = 4


def _pieces(dil, seq):
    length = seq // dil
    rows = min(length, ATTN_PIECE)
    return [(r + dil * ci * rows, r * length + ci * rows, rows) for r in range(dil) for ci in range(length // rows)]


def _strided(ref, start, rows, dil):
    if dil == 1:
        return ref[pl.ds(start, rows), :]
    return ref[pl.ds(start, rows, stride=dil), :]


def _strided_set(ref, start, rows, dil, val):
    if dil == 1:
        ref[pl.ds(start, rows), :] = val
    else:
        ref[pl.ds(start, rows, stride=dil), :] = val


def _nt(a, b):
    return lax.dot_general(a, b, (((1,), (1,)), ((), ())), preferred_element_type=F32)


def _tn(a, b):
    return lax.dot_general(a, b, (((0,), (0,)), ((), ())), preferred_element_type=F32)


def _set_bias(bias_ref):
    qi = lax.broadcasted_iota(jnp.int32, (2 * SPAN, 2 * SPAN), 0) & (SPAN - 1)
    kj = lax.broadcasted_iota(jnp.int32, (2 * SPAN, 2 * SPAN), 1)
    band = jnp.logical_and(kj >= qi, (kj - SPAN) <= qi)
    bias_ref[1] = jnp.where(band, 0.0, NEG_INF)
    bias_ref[0] = jnp.where(jnp.logical_and(band, kj >= SPAN), 0.0, NEG_INF)


def _stack_heads(v, head0):
    zero = jnp.zeros_like(v)
    return jnp.concatenate([jnp.where(head0, v, zero), jnp.where(head0, zero, v)], axis=0)


def _unstack_heads(v2, head0):
    return jnp.where(head0, v2[:SPAN], v2[SPAN:])


def _head_cols(v):
    return jnp.concatenate([v[:, 0:1], v[:, HEAD_DIM:HEAD_DIM + 1]], axis=0)


def _attn_fwd(proj_b, kv, gains, tables, ones, bsz, seq, name):
    t, d4 = proj_b.shape
    d = d4 // 4
    nhp = d // V7X_LANES
    nblk = seq // SPAN
    scale = HEAD_DIM ** -0.5
    n_groups = len(DILATIONS)

    def body(q0_ref, q1_ref, q2_ref, k_ref, v_ref, gate_ref, gain_ref, ta_ref, tb_ref, tc_ref, e_ref,
             o_ref, l_ref, ao_ref, qd, kd, vd, od, ld, on0, on1, on2, ln0, ln1, ln2, kn, bias):
        head0 = lax.broadcasted_iota(jnp.int32, (SPAN, V7X_LANES), 1) < HEAD_DIM

        @pl.when(jnp.logical_and(pl.program_id(0) == 0, pl.program_id(1) == 0))
        def _():
            _set_bias(bias)

        _norm_rope_rows(kn, k_ref, gain_ref[n_groups:n_groups + 1, :], ta_ref, tb_ref, tc_ref, e_ref, seq)
        kd[0:SPAN, :] = jnp.zeros((SPAN, V7X_LANES), BF16)
        vd[0:SPAN, :] = jnp.zeros((SPAN, V7X_LANES), BF16)
        for g, (q_ref, on, ln) in enumerate(((q0_ref, on0, ln0), (q1_ref, on1, ln1), (q2_ref, on2, ln2))):
            dil = DILATIONS[g]
            nb = seq // dil // SPAN
            for ns, rs, rows in _pieces(dil, seq):
                ta, tb, tc = (_strided(r_, ns, rows, dil) for r_ in (ta_ref, tb_ref, tc_ref))
                qd[rs:rs + rows, :] = _norm_rope(_strided(q_ref, ns, rows, dil), gain_ref[g:g + 1, :],
                                                 ta, tb, tc, e_ref).astype(BF16)
                kd[SPAN + rs:SPAN + rs + rows, :] = _strided(kn, ns, rows, dil).astype(BF16)
                vd[SPAN + rs:SPAN + rs + rows, :] = _strided(v_ref, ns, rows, dil).astype(BF16)

            def block(j, carry):
                qs = pl.multiple_of(j * SPAN, SPAN)
                q2 = _stack_heads(qd[pl.ds(qs, SPAN), :], head0)
                kk = kd[pl.ds(qs, 2 * SPAN), :]
                vv = vd[pl.ds(qs, 2 * SPAN), :]
                s = _nt(q2, kk) * scale + bias[jnp.minimum(j & (nb - 1), 1)]
                mx = jnp.max(s, axis=1, keepdims=True)
                p = jnp.exp(s - mx)
                den = jnp.sum(p, axis=1, keepdims=True)
                o2 = jnp.dot(p.astype(BF16), vv, preferred_element_type=F32) / den
                l2 = jnp.broadcast_to(mx + jnp.log(den), (2 * SPAN, V7X_LANES))
                od[pl.ds(qs, SPAN), :] = _unstack_heads(o2, head0)
                ld[pl.ds(qs, SPAN), :] = _unstack_heads(l2, head0)
                return carry

            lax.fori_loop(0, nblk, block, 0, unroll=ATTN_UNROLL)
            for ns, rs, rows in _pieces(dil, seq):
                _strided_set(on, ns, rows, dil, od[rs:rs + rows, :])
                _strided_set(ln, ns, rows, dil, ld[rs:rs + rows, :])

        def merge(ci, carry):
            rows = pl.ds(pl.multiple_of(ci * ATTN_PIECE, ATTN_PIECE), ATTN_PIECE)
            ls = [ln0[rows, :], ln1[rows, :], ln2[rows, :]]
            mx = jnp.maximum(jnp.maximum(ls[0], ls[1]), ls[2])
            es = [jnp.exp(v - mx) for v in ls]
            den = es[0] + es[1] + es[2]
            ov = (es[0] * on0[rows, :] + es[1] * on1[rows, :] + es[2] * on2[rows, :]) / den
            gate = gate_ref[rows, :]
            o_ref[rows, :] = ov
            l_ref[rows, :] = mx + jnp.log(den)
            ao_ref[rows, :] = (ov * gate * _sig(gate)).astype(BF16)
            return carry

        lax.fori_loop(0, seq // ATTN_PIECE, merge, 0)

    blk = (None, seq, V7X_LANES)
    pview = proj_b.reshape(bsz, seq, d4)
    kview = kv.reshape(bsz, seq, 2 * d)
    out_spec = pl.BlockSpec(blk, lambda b, h: (b, 0, h))
    tab = pl.BlockSpec((seq, V7X_LANES), lambda b, h: (0, 0))
    nat = pltpu.VMEM((seq, V7X_LANES), F32)
    o, lse, ao = pl.pallas_call(
        body, name=name, grid=(bsz, nhp),
        in_specs=[pl.BlockSpec(blk, lambda b, h: (b, 0, h)),
                  pl.BlockSpec(blk, lambda b, h: (b, 0, nhp + h)),
                  pl.BlockSpec(blk, lambda b, h: (b, 0, 2 * nhp + h)),
                  pl.BlockSpec(blk, lambda b, h: (b, 0, h)),
                  pl.BlockSpec(blk, lambda b, h: (b, 0, nhp + h)),
                  pl.BlockSpec(blk, lambda b, h: (b, 0, 3 * nhp + h)),
                  pl.BlockSpec((n_groups + 1, V7X_LANES), lambda b, h: (0, 0)),
                  tab, tab, tab,
                  pl.BlockSpec((V7X_LANES, V7X_LANES), lambda b, h: (0, 0))],
        out_specs=[out_spec, out_spec, out_spec],
        out_shape=[jax.ShapeDtypeStruct((bsz, seq, d), F32), jax.ShapeDtypeStruct((bsz, seq, d), F32),
                   jax.ShapeDtypeStruct((bsz, seq, d), BF16)],
        scratch_shapes=[pltpu.VMEM((seq, V7X_LANES), BF16), pltpu.VMEM((SPAN + seq, V7X_LANES), BF16),
                        pltpu.VMEM((SPAN + seq, V7X_LANES), BF16), nat, nat, nat, nat, nat, nat, nat, nat, nat,
                        pltpu.VMEM((2, 2 * SPAN, 2 * SPAN), F32)],
        compiler_params=_params(2),
    )(pview, pview, pview, kview, kview, pview, gains, *tables, ones)
    return o.reshape(t, d), lse.reshape(t, d), ao.reshape(t, d)


def _attn_bwd(proj_b, kv, dao, o, lse, gains, tables, ones, bsz, seq, name):
    t, d4 = proj_b.shape
    d = d4 // 4
    nhp = d // V7X_LANES
    nblk = seq // SPAN
    scale = HEAD_DIM ** -0.5
    n_groups = len(DILATIONS)
    n_chunks = seq // ATTN_PIECE

    def body(q_ref, k_ref, v_ref, gate_ref, dao_ref, o_ref, l_ref, gain_ref, ta_ref, tb_ref, tc_ref, e_ref,
             dproj_ref, dkv_ref, dg_ref, qd, kd, vd, dod, ld, deld, dqd, dkd, dvd, dqn, dkn, dvn, kn, bias):
        head0 = lax.broadcasted_iota(jnp.int32, (SPAN, V7X_LANES), 1) < HEAD_DIM
        g = pl.program_id(2)

        @pl.when(jnp.logical_and(jnp.logical_and(pl.program_id(0) == 0, pl.program_id(1) == 0), g == 0))
        def _():
            _set_bias(bias)
            dg_ref[...] = jnp.zeros_like(dg_ref)

        @pl.when(g == 0)
        def _():
            dkn[...] = jnp.zeros_like(dkn)
            dvn[...] = jnp.zeros_like(dvn)
            _norm_rope_rows(kn, k_ref, gain_ref[n_groups:n_groups + 1, :], ta_ref, tb_ref, tc_ref, e_ref, seq)

        def norm_bwd_chunks(x_ref, dn_ref, out_ref, gi):
            def chunk(ci, carry):
                rows = pl.ds(pl.multiple_of(ci * ATTN_PIECE, ATTN_PIECE), ATTN_PIECE)
                dx, part = _norm_rope_bwd(x_ref[rows, :], dn_ref[rows, :], gain_ref[gi:gi + 1, :],
                                          ta_ref[rows, :], tb_ref[rows, :], tc_ref[rows, :], e_ref)
                out_ref[rows, :] = dx.astype(BF16)
                dg_ref[gi] += part
                return carry
            lax.fori_loop(0, n_chunks, chunk, 0, unroll=ATTN_UNROLL)

        def group(gi):
            dil = DILATIONS[gi]
            nb = seq // dil // SPAN
            kd[0:SPAN, :] = jnp.zeros((SPAN, V7X_LANES), BF16)
            vd[0:SPAN, :] = jnp.zeros((SPAN, V7X_LANES), BF16)
            dkd[...] = jnp.zeros_like(dkd)
            dvd[...] = jnp.zeros_like(dvd)
            for ns, rs, rows in _pieces(dil, seq):
                ta, tb, tc = (_strided(r_, ns, rows, dil) for r_ in (ta_ref, tb_ref, tc_ref))
                qd[rs:rs + rows, :] = _norm_rope(_strided(q_ref, ns, rows, dil), gain_ref[gi:gi + 1, :],
                                                 ta, tb, tc, e_ref).astype(BF16)
                kd[SPAN + rs:SPAN + rs + rows, :] = _strided(kn, ns, rows, dil).astype(BF16)
                vd[SPAN + rs:SPAN + rs + rows, :] = _strided(v_ref, ns, rows, dil).astype(BF16)
                gate = _strided(gate_ref, ns, rows, dil)
                dov = _strided(dao_ref, ns, rows, dil) * gate * _sig(gate)
                dod[rs:rs + rows, :] = dov.astype(BF16)
                deld[rs:rs + rows, :] = _segsum(dov * _strided(o_ref, ns, rows, dil), e_ref)
                ld[rs:rs + rows, :] = _strided(l_ref, ns, rows, dil)

            def block(j, carry):
                qs = pl.multiple_of(j * SPAN, SPAN)
                q2 = _stack_heads(qd[pl.ds(qs, SPAN), :], head0)
                do2 = _stack_heads(dod[pl.ds(qs, SPAN), :], head0)
                kk = kd[pl.ds(qs, 2 * SPAN), :]
                vv = vd[pl.ds(qs, 2 * SPAN), :]
                s = _nt(q2, kk) * scale + bias[jnp.minimum(j & (nb - 1), 1)]
                p = jnp.exp(s - _head_cols(ld[pl.ds(qs, SPAN), :]))
                ds = (p * (_nt(do2, vv) - _head_cols(deld[pl.ds(qs, SPAN), :])) * scale).astype(BF16)
                dqd[pl.ds(qs, SPAN), :] = _unstack_heads(jnp.dot(ds, kk, preferred_element_type=F32), head0)
                dkd[pl.ds(qs, 2 * SPAN), :] += _tn(ds, q2)
                dvd[pl.ds(qs, 2 * SPAN), :] += _tn(p.astype(BF16), do2)
                return carry

            lax.fori_loop(0, nblk, block, 0, unroll=ATTN_UNROLL)
            for ns, rs, rows in _pieces(dil, seq):
                _strided_set(dqn, ns, rows, dil, dqd[rs:rs + rows, :])
                _strided_set(dkn, ns, rows, dil,
                             _strided(dkn, ns, rows, dil) + dkd[SPAN + rs:SPAN + rs + rows, :])
                _strided_set(dvn, ns, rows, dil,
                             _strided(dvn, ns, rows, dil) + dvd[SPAN + rs:SPAN + rs + rows, :])
            norm_bwd_chunks(q_ref, dqn, dproj_ref, gi)

        for gi in range(n_groups):
            @pl.when(g == gi)
            def _():
                group(gi)

        @pl.when(g == n_groups - 1)
        def _():
            norm_bwd_chunks(k_ref, dkn, dkv_ref, n_groups)

        @pl.when(g == n_groups)
        def _():
            def chunk(ci, carry):
                rows = pl.ds(pl.multiple_of(ci * ATTN_PIECE, ATTN_PIECE), ATTN_PIECE)
                gate = gate_ref[rows, :]
                sg = _sig(gate)
                dproj_ref[rows, :] = (dao_ref[rows, :] * o_ref[rows, :]
                                      * (sg * (1.0 + gate * (1.0 - sg)))).astype(BF16)
                dkv_ref[rows, :] = dvn[rows, :].astype(BF16)
                return carry
            lax.fori_loop(0, n_chunks, chunk, 0, unroll=ATTN_UNROLL)

    blk = (None, seq, V7X_LANES)
    pview = proj_b.reshape(bsz, seq, d4)
    kview = kv.reshape(bsz, seq, 2 * d)
    dview = (bsz, seq, d)
    d_spec = pl.BlockSpec(blk, lambda b, h, g: (b, 0, h))
    tab = pl.BlockSpec((seq, V7X_LANES), lambda b, h, g: (0, 0))
    nat = pltpu.VMEM((seq, V7X_LANES), F32)
    natb = pltpu.VMEM((seq, V7X_LANES), BF16)
    pad = pltpu.VMEM((SPAN + seq, V7X_LANES), F32)
    padb = pltpu.VMEM((SPAN + seq, V7X_LANES), BF16)
    dproj, dkv, dg = pl.pallas_call(
        body, name=name, grid=(bsz, nhp, n_groups + 1),
        in_specs=[pl.BlockSpec(blk, lambda b, h, g: (b, 0, jnp.minimum(g, n_groups - 1) * nhp + h)),
                  pl.BlockSpec(blk, lambda b, h, g: (b, 0, h)),
                  pl.BlockSpec(blk, lambda b, h, g: (b, 0, nhp + h)),
                  pl.BlockSpec(blk, lambda b, h, g: (b, 0, n_groups * nhp + h)),
                  d_spec, d_spec, d_spec,
                  pl.BlockSpec((n_groups + 1, V7X_LANES), lambda b, h, g: (0, 0)),
                  tab, tab, tab,
                  pl.BlockSpec((V7X_LANES, V7X_LANES), lambda b, h, g: (0, 0))],
        out_specs=[pl.BlockSpec(blk, lambda b, h, g: (b, 0, g * nhp + h)),
                   pl.BlockSpec(blk, lambda b, h, g: (b, 0, (g // n_groups) * nhp + h)),
                   pl.BlockSpec((n_groups + 1, V7X_SUBLANES, V7X_LANES), lambda b, h, g: (0, 0, 0))],
        out_shape=[jax.ShapeDtypeStruct((bsz, seq, d4), BF16), jax.ShapeDtypeStruct((bsz, seq, 2 * d), BF16),
                   jax.ShapeDtypeStruct((n_groups + 1, V7X_SUBLANES, V7X_LANES), F32)],
        scratch_shapes=[natb, padb, padb, natb, nat, nat, nat, pad, pad, nat, nat, nat, nat,
                        pltpu.VMEM((2, 2 * SPAN, 2 * SPAN), F32)],
        compiler_params=_params(3),
    )(pview, kview, kview, pview, dao.reshape(dview), o.reshape(dview), lse.reshape(dview), gains, *tables, ones)
    dgain = dg.sum(axis=1).reshape(n_groups + 1, V7X_LANES // HEAD_DIM, HEAD_DIM).sum(axis=1)
    return dproj.reshape(t, d4), dkv.reshape(t, 2 * d), dgain


def _mesh_position():
    x, y, c = lax.axis_index("x"), lax.axis_index("y"), lax.axis_index("c")
    return x, y, c


def _peer(x, y, c, rel):
    return (1 - x if rel & 4 else x, 1 - y if rel & 2 else y, 1 - c if rel & 1 else c)


class _Exchange:
    def __init__(self, srcs, gather):
        self.srcs = list(srcs)
        self.gather = gather
        n = self.n = len(self.srcs)
        hbm = pl.BlockSpec(memory_space=pltpu.HBM)
        self.in_specs = [hbm] * n
        self.out_specs = [hbm] * n
        self.out_shape = [jax.ShapeDtypeStruct(((N_DEV,) + a.shape) if gather else a.shape, a.dtype)
                          for a in self.srcs]
        self.scratch = [pltpu.SemaphoreType.DMA((n * (N_DEV - 1),)), pltpu.SemaphoreType.DMA((n * (N_DEV - 1),)),
                        pltpu.SemaphoreType.DMA((n,))]

    def _copies(self, ins, outs, sems):
        send_sems, recv_sems, local_sems = sems
        x, y, c = _mesh_position()
        me = 4 * x + 2 * y + c
        remote, local = [], []
        for a in range(self.n):
            mine = ins[a] if self.gather else ins[a].at[me]
            local.append(pltpu.make_async_copy(mine, outs[a].at[me], local_sems.at[a]))
            for rel in range(1, N_DEV):
                px, py, pc = _peer(x, y, c, rel)
                s = a * (N_DEV - 1) + rel - 1
                src = ins[a] if self.gather else ins[a].at[4 * px + 2 * py + pc]
                remote.append(pltpu.make_async_remote_copy(
                    src_ref=src, dst_ref=outs[a].at[me], send_sem=send_sems.at[s], recv_sem=recv_sems.at[s],
                    device_id=(px, py, pc), device_id_type=pl.DeviceIdType.MESH))
        return remote, local

    def start(self, ins, outs, sems):
        remote, local = self._copies(ins, outs, sems)
        for cp in local + remote:
            cp.start()

    def wait(self, ins, outs, sems):
        remote, local = self._copies(ins, outs, sems)
        for cp in remote:
            cp.wait_recv()
        for cp in remote:
            cp.wait_send()
        for cp in local:
            cp.wait()


def _run_exchange(ex, name):
    n = ex.n

    def body(*refs):
        ins, outs, sems = refs[:n], refs[n:2 * n], refs[2 * n:]
        ex.start(ins, outs, sems)
        ex.wait(ins, outs, sems)

    return pl.pallas_call(body, name=name, in_specs=ex.in_specs, out_specs=ex.out_specs, out_shape=ex.out_shape,
                          scratch_shapes=ex.scratch)(*ex.srcs)


def _hosted_call(body, ex, name, grid, in_specs, out_specs, out_shape, scratch_shapes, args):
    if ex is None:
        outs = pl.pallas_call(body, name=name, grid=grid, in_specs=in_specs, out_specs=out_specs, out_shape=out_shape,
                              scratch_shapes=scratch_shapes, compiler_params=_params(len(grid)))(*args)
        return list(outs), []
    n_in, n_out, n_scr, n = len(in_specs), len(out_specs), len(scratch_shapes), ex.n

    def hosted(*refs):
        h_in, e_in = refs[:n_in], refs[n_in:n_in + n]
        o0 = n_in + n
        h_out, e_out = refs[o0:o0 + n_out], refs[o0 + n_out:o0 + n_out + n]
        s0 = o0 + n_out + n
        h_scr, e_scr = refs[s0:s0 + n_scr], refs[s0 + n_scr:]
        ids = [pl.program_id(a) for a in range(len(grid))]
        first = functools.reduce(jnp.logical_and, [i == 0 for i in ids])
        last = functools.reduce(jnp.logical_and, [i == g - 1 for i, g in zip(ids, grid)])

        @pl.when(first)
        def _():
            ex.start(e_in, e_out, e_scr)

        body(*h_in, *h_out, *h_scr)

        @pl.when(last)
        def _():
            ex.wait(e_in, e_out, e_scr)

    outs = pl.pallas_call(
        hosted, name=name, grid=grid, in_specs=list(in_specs) + ex.in_specs,
        out_specs=list(out_specs) + ex.out_specs, out_shape=list(out_shape) + ex.out_shape,
        scratch_shapes=list(scratch_shapes) + ex.scratch, compiler_params=_params(len(grid)),
    )(*args, *ex.srcs)
    return list(outs[:n_out]), list(outs[n_out:])


def _sum_adamw(parts, w, m, v, name):
    _, r, wd = parts.shape
    tr = ADAM_ROWS
    c1 = 1.0 - ADAM_B1 ** ADAM_STEP
    c2 = 1.0 - ADAM_B2 ** ADAM_STEP

    def body(p_ref, w_ref, m_ref, v_ref, g_ref, d_ref, nm_ref, nv_ref):
        g = p_ref[0].astype(F32)
        for s in range(1, N_DEV):
            g = g + p_ref[s].astype(F32)
        nm = ADAM_B1 * m_ref[...] + (1.0 - ADAM_B1) * g
        nv = ADAM_B2 * v_ref[...] + (1.0 - ADAM_B2) * (g * g)
        g_ref[...] = g
        nm_ref[...] = nm
        nv_ref[...] = nv
        d_ref[...] = -ADAM_LR * ((nm / c1) / (jnp.sqrt(nv / c2) + ADAM_EPS) + ADAM_WD * w_ref[...])

    row = pl.BlockSpec((tr, wd), lambda i: (i, 0))
    return pl.pallas_call(
        body, name=name, grid=(r // tr,),
        in_specs=[pl.BlockSpec((N_DEV, tr, wd), lambda i: (0, i, 0)), row, row, row],
        out_specs=[row] * 4, out_shape=[jax.ShapeDtypeStruct((r, wd), F32)] * 4,
        compiler_params=_params(1),
    )(parts, w, m, v)


def _pack_rows(size, row_mult):
    rows = -(-size // PACK_LANES)
    return -(-rows // row_mult) * row_mult


def _pack(flats, row_mult, dtype, total_mult=None):
    out = []
    for f in flats:
        size = f.shape[-1]
        rows = _pack_rows(size, row_mult)
        pad = [(0, 0)] * (f.ndim - 1) + [(0, rows * PACK_LANES - size)]
        out.append(jnp.pad(f.astype(dtype), pad).reshape(f.shape[:-1] + (rows, PACK_LANES)))
    if total_mult is not None:
        total = sum(o.shape[-2] for o in out)
        extra = -(-total // total_mult) * total_mult - total
        if extra:
            out.append(jnp.zeros(out[0].shape[:-2] + (extra, PACK_LANES), dtype))
    return jnp.concatenate(out, axis=-2)


def _unpack(buf, sizes, row_mult):
    out, row = [], 0
    for size in sizes:
        rows = _pack_rows(size, row_mult)
        part = buf[..., row:row + rows, :]
        out.append(part.reshape(buf.shape[:-2] + (rows * PACK_LANES,))[..., :size])
        row += rows
    return out


def _to_slots(full, axis):
    if axis is None:
        return jnp.broadcast_to(full.reshape(1, -1), (N_DEV, full.size))
    shape = full.shape
    split = full.reshape(shape[:axis] + (N_DEV, shape[axis] // N_DEV) + shape[axis + 1:])
    return jnp.moveaxis(split, axis, 0).reshape(N_DEV, -1)


def _from_slots(slots, axis, block_shape):
    split = jnp.moveaxis(slots.reshape((N_DEV,) + tuple(block_shape)), 0, axis)
    shape = list(block_shape)
    shape[axis] *= N_DEV
    return split.reshape(shape)


def kernel(x, p, norm_g, w_in_a, conv_w, conv_b, ln_g, ln_b, w_out_a, kv_norm_g, w_kv, k_norm_g, w_in_b, q_norm_g, w_out_b, ple_norm_g, w_ple_gate, w_ple_proj, loss_target, m_norm_g, m_w_in_a, m_conv_w, m_conv_b, m_ln_g, m_ln_b, m_w_out_a, m_kv_norm_g, m_w_kv, m_k_norm_g, m_w_in_b, m_q_norm_g, m_w_out_b, m_ple_norm_g, m_w_ple_gate, m_w_ple_proj, v_norm_g, v_w_in_a, v_conv_w, v_conv_b, v_ln_g, v_ln_b, v_w_out_a, v_kv_norm_g, v_w_kv, v_k_norm_g, v_w_in_b, v_q_norm_g, v_w_out_b, v_ple_norm_g, v_w_ple_gate, v_w_ple_proj):
    weights = dict(zip(WEIGHT_NAMES, (norm_g, w_in_a, conv_w, conv_b, ln_g, ln_b, w_out_a, kv_norm_g, w_kv, k_norm_g,
                                      w_in_b, q_norm_g, w_out_b, ple_norm_g, w_ple_gate, w_ple_proj)))
    mom_m = dict(zip(WEIGHT_NAMES, (m_norm_g, m_w_in_a, m_conv_w, m_conv_b, m_ln_g, m_ln_b, m_w_out_a, m_kv_norm_g,
                                    m_w_kv, m_k_norm_g, m_w_in_b, m_q_norm_g, m_w_out_b, m_ple_norm_g, m_w_ple_gate,
                                    m_w_ple_proj)))
    mom_v = dict(zip(WEIGHT_NAMES, (v_norm_g, v_w_in_a, v_conv_w, v_conv_b, v_ln_g, v_ln_b, v_w_out_a, v_kv_norm_g,
                                    v_w_kv, v_k_norm_g, v_w_in_b, v_q_norm_g, v_w_out_b, v_ple_norm_g, v_w_ple_gate,
                                    v_w_ple_proj)))
    bsz, seq, d = x.shape
    t = bsz * seq
    assert seq % (max(DILATIONS) * SPAN) == 0 and d % V7X_LANES == 0

    full = {}

    def gathered(names, buf, mult):
        for n, slots in zip(names, _unpack(buf, [weights[n].size for n in names], mult)):
            full[n] = _from_slots(slots, SHARD_AXIS[n], weights[n].shape)

    def packed(source, names, dtype, total_mult=None):
        return _pack([source[n].reshape(-1) for n in names], 16, dtype, total_mult)

    w1_all, wv_all = _run_exchange(_Exchange([packed(weights, GROUP_FIRST, BF16),
                                              _pack([weights[n].reshape(-1) for n in VECTOR_WEIGHTS], 8, F32)],
                                             gather=True), "gather_first")
    gathered(GROUP_FIRST, w1_all, 16)
    gathered(VECTOR_WEIGHTS, wv_all, 8)
    gather_rest = _Exchange([packed(weights, GROUP_REST, BF16)], gather=True)
    wa_in = full['w_in_a'][0]
    cw, cb, lg, lb = full['conv_w'][0], full['conv_b'], full['ln_g'], full['ln_b']

    tables = _rope_tables(seq)
    ones = _head_ones(V7X_LANES)
    rep = V7X_LANES // HEAD_DIM
    head_gain = jnp.concatenate([jnp.tile(q_norm_g[0], (1, rep)), jnp.tile(k_norm_g, rep)[None]], axis=0)

    x0 = x.reshape(t, d)
    p0, p1 = p[0].reshape(t, -1), p[1].reshape(t, -1)
    target = loss_target.reshape(t, d)
    g_norm0, g_norm1 = norm_g[0:1], norm_g[1:2]
    g_ple0, g_ple1 = ple_norm_g[0:1], ple_norm_g[1:2]
    g_kv = kv_norm_g.reshape(1, d)

    (u0,) = _rmsnorm_fwd(x0, [g_norm0], "norm0")
    proj_a = _matmul(u0, wa_in, 'nn', "in_a")
    m_act, y_conv, (w2_all,) = _conv_fwd(proj_a, cw, cb, lg, lb, seq, "conv_fwd", ex=gather_rest)
    gathered(GROUP_REST, w2_all, 16)
    wa_out = full['w_out_a'][0]
    wkv = full['w_kv']
    wb_in, wb_out = full['w_in_b'][0], full['w_out_b'][0]
    wg, wp = full['w_ple_gate'], full['w_ple_proj']
    h0 = _matmul(m_act, wa_out, 'nn', "out_a", add=x0)
    (pg0,) = _rmsnorm_fwd(h0, [g_ple0], "ple_norm0")
    gl0 = _matmul(pg0, wg[0], 'nn', "ple_gate0")
    pp0 = _matmul(p0, wp[0], 'nn', "ple_proj0")
    x1 = _ple_fwd(h0, gl0, pp0, "ple0")

    kvn, u1 = _rmsnorm_fwd(x1, [g_kv, g_norm1], "norm1")
    kv = _matmul(kvn, wkv, 'nn', "kv")
    proj_b = _matmul(u1, wb_in, 'nn', "in_b")
    o_att, lse, ao = _attn_fwd(proj_b, kv, head_gain, tables, ones, bsz, seq, "attn_fwd")
    h1 = _matmul(ao, wb_out, 'nn', "out_b", add=x1)
    (pg1,) = _rmsnorm_fwd(h1, [g_ple1], "ple_norm1")
    gl1 = _matmul(pg1, wg[1], 'nn', "ple_gate1")
    pp1 = _matmul(p1, wp[1], 'nn', "ple_proj1")
    x2 = _ple_fwd(h1, gl1, pp1, "ple1")

    dx2, loss_part = _loss_fwd_bwd(x2, target, "loss")
    loss = lax.psum(jnp.sum(loss_part), ("x", "y", "c"))

    grads = {}

    dgl1, dpp1 = _ple_bwd(dx2, gl1, pp1, "ple1_bwd")
    dwp1 = _matmul(p1, dpp1, 'tn', "d_ple_proj1")
    dwg1 = _matmul(pg1, dgl1, 'tn', "d_ple_gate1")
    dpg1 = _matmul(dgl1, wg[1], 'nt', "d_ple_norm1")
    dh1, (dg_ple1,) = _rmsnorm_bwd(h1, [g_ple1], [dpg1], dx2, "ple_norm1_bwd")
    grads['w_out_b'] = _matmul(ao, dh1, 'tn', "d_out_b")[None]
    dao = _matmul(dh1, wb_out, 'nt', "d_ao")
    dproj_b, dkv, dg_head = _attn_bwd(proj_b, kv, dao, o_att, lse, head_gain, tables, ones, bsz, seq, "attn_bwd")
    grads['w_in_b'] = _matmul(u1, dproj_b, 'tn', "d_in_b")[None]
    du1 = _matmul(dproj_b, wb_in, 'nt', "d_u1")
    grads['w_kv'] = _matmul(kvn, dkv, 'tn', "d_kv")
    dkvn = _matmul(dkv, wkv, 'nt', "d_kvn")
    dx1, (dg_kv, dg_norm1) = _rmsnorm_bwd(x1, [g_kv, g_norm1], [dkvn, du1], dh1, "norm1_bwd")

    dgl0, dpp0 = _ple_bwd(dx1, gl0, pp0, "ple0_bwd")
    dwp0 = _matmul(p0, dpp0, 'tn', "d_ple_proj0")
    dwg0 = _matmul(pg0, dgl0, 'tn', "d_ple_gate0")
    dpg0 = _matmul(dgl0, wg[0], 'nt', "d_ple_norm0")
    dh0, (dg_ple0,) = _rmsnorm_bwd(h0, [g_ple0], [dpg0], dx1, "ple_norm0_bwd")
    grads['w_out_a'] = _matmul(m_act, dh0, 'tn', "d_out_a")[None]
    dm = _matmul(dh0, wa_out, 'nt', "d_m")
    dy_conv, dz, d_lg, d_lb, d_cb = _ln_gate_bwd(dm, y_conv, proj_a, lg, lb, "ln_gate_bwd")
    grads['w_ple_gate'] = jnp.stack([dwg0, dwg1])
    grads['w_ple_proj'] = jnp.stack([dwp0, dwp1])

    def slotted(names):
        return _pack([_to_slots(grads[n], SHARD_AXIS[n]) for n in names], 16, BF16, ADAM_ROWS)

    dproj_a, d_cw, (parts_rest,) = _conv_bwd(dy_conv, dz, proj_a, cw, seq, "conv_bwd",
                                             ex=_Exchange([slotted(GROUP_REST)], gather=False))
    grads['w_in_a'] = _matmul(u0, dproj_a, 'tn', "d_in_a")[None]
    du0, (parts_first,) = _matmul(dproj_a, wa_in, 'nt', "d_u0", ex=_Exchange([slotted(GROUP_FIRST)], gather=False))
    dx0, (dg_norm0,) = _rmsnorm_bwd(x0, [g_norm0], [du0], dh0, "norm0_bwd")

    grads['norm_g'] = jnp.stack([dg_norm0, dg_norm1])
    grads['conv_w'] = d_cw[None]
    grads['conv_b'] = d_cb[None]
    grads['ln_g'] = d_lg[None]
    grads['ln_b'] = d_lb[None]
    grads['kv_norm_g'] = dg_kv
    grads['k_norm_g'] = dg_head[3]
    grads['q_norm_g'] = dg_head[0:3][None]
    grads['ple_norm_g'] = jnp.stack([dg_ple0, dg_ple1])
    (parts_small,) = _run_exchange(_Exchange([slotted(GROUP_SMALL)], gather=False), "exchange_small")

    updated = {}
    for names, parts, tag in ((GROUP_REST, parts_rest, "rest"), (GROUP_FIRST, parts_first, "first"),
                              (GROUP_SMALL, parts_small, "small")):
        outs = _sum_adamw(parts, packed(weights, names, F32, ADAM_ROWS), packed(mom_m, names, F32, ADAM_ROWS),
                          packed(mom_v, names, F32, ADAM_ROWS), "sum_adamw_" + tag)
        sizes = [weights[n].size for n in names]
        for kind, buf in enumerate(outs):
            for n, flat in zip(names, _unpack(buf, sizes, 16)):
                updated[kind, n] = flat.reshape(weights[n].shape)
    result = [loss, dx0.reshape(bsz, seq, d)]
    for kind in range(4):
        result.extend(updated[kind, n] for n in WEIGHT_NAMES)
    return tuple(result)
```

```python
import functools

import jax
import jax.numpy as jnp
from jax import lax
from jax.experimental import pallas as pl
from jax.experimental.pallas import tpu as pltpu

F32 = jnp.float32
BF16 = jnp.bfloat16

N_DEV = 8
HEAD_DIM = 64
ROPE_DIM = 16
ROPE_THETA = 500000.0
EPS = 1e-6
NEG_INF = -1e30
SPAN = 128
DILATIONS = (1, 4, 16)
CONV_WIDTH = 31
HALO = 32
CONV_ROWS = 32
CONV_W_ROWS = 64
PACK_LANES = 1024
V7X_LANES = 128
V7X_SUBLANES = 8
VMEM_LIMIT_BYTES = 56 * 1024 * 1024

ADAM_LR = 0.001
ADAM_B1 = 0.9
ADAM_B2 = 0.999
ADAM_EPS = 1e-08
ADAM_WD = 0.01
ADAM_STEP = 10
ADAM_ROWS = 256

WEIGHT_NAMES = ('norm_g', 'w_in_a', 'conv_w', 'conv_b', 'ln_g', 'ln_b', 'w_out_a', 'kv_norm_g', 'w_kv',
                'k_norm_g', 'w_in_b', 'q_norm_g', 'w_out_b', 'ple_norm_g', 'w_ple_gate', 'w_ple_proj')
SHARD_AXIS = {'norm_g': None, 'w_in_a': 2, 'conv_w': 2, 'conv_b': 1, 'ln_g': 1, 'ln_b': 1, 'w_out_a': 1,
              'kv_norm_g': None, 'w_kv': 1, 'k_norm_g': None, 'w_in_b': 2, 'q_norm_g': None, 'w_out_b': 1,
              'ple_norm_g': None, 'w_ple_gate': 1, 'w_ple_proj': 2}
VECTOR_WEIGHTS = ('conv_w', 'conv_b', 'ln_g', 'ln_b')
GROUP_FIRST = ('w_in_a',)
GROUP_REST = ('w_out_a', 'w_kv', 'w_in_b', 'w_out_b', 'w_ple_gate', 'w_ple_proj')
GROUP_SMALL = ('norm_g', 'conv_w', 'conv_b', 'ln_g', 'ln_b', 'kv_norm_g', 'k_norm_g', 'q_norm_g', 'ple_norm_g')


def _pick(n, target, mult):
    t = (min(target, n) // mult) * mult
    while t >= mult:
        if n % t == 0:
            return t
        t -= mult
    return n


def _params(n_grid):
    return pltpu.CompilerParams(dimension_semantics=("arbitrary",) * n_grid, vmem_limit_bytes=VMEM_LIMIT_BYTES)


def _sig(x):
    return 1.0 / (1.0 + jnp.exp(-x))


def _colsum8(v):
    r, w = v.shape
    return v.reshape(r // V7X_SUBLANES, V7X_SUBLANES, w).sum(axis=0)


def _rows(tm, w, col=0):
    return pl.BlockSpec((tm, w), lambda i: (i, col))


def _const(shape):
    nd = len(shape)
    return pl.BlockSpec(shape, lambda i: (0,) * nd)


def _segsum(v, e_ref):
    hi = v.astype(BF16)
    lo = (v - hi.astype(F32)).astype(BF16)
    e = e_ref[...]
    return jnp.dot(hi, e, preferred_element_type=F32) + jnp.dot(lo, e, preferred_element_type=F32)


MM_TILE = 1024
MM_TILE_K = 2048


def _matmul(a, b, mode, name, out_dtype=F32, add=None, ex=None, slot_cols=None):
    if mode == 'nn':
        (m, k), (_, n) = a.shape, b.shape
    elif mode == 'nt':
        (m, k), (n, _) = a.shape, b.shape
    else:
        (k, m), (_, n) = a.shape, b.shape
    out_struct = jax.ShapeDtypeStruct((m, n), out_dtype)
    n_slots = 0
    if mode == 'tn':
        tm, tn, tk = _pick(m, MM_TILE, 128), _pick(n, MM_TILE, 128), _pick(k, MM_TILE_K, 128)
        o_spec = pl.BlockSpec((tm, tn), lambda i, j, kk: (i, j))
        if slot_cols is not None:
            assert n == N_DEV * slot_cols
            n_slots = max(s for s in (1, 2, 4, 8) if s == 1 or slot_cols * s <= MM_TILE)
            tn = slot_cols * n_slots
            o_spec = pl.BlockSpec((n_slots, tm, slot_cols), lambda i, j, kk: (j, i, 0))
            out_struct = jax.ShapeDtypeStruct((N_DEV, m, slot_cols), out_dtype)
        grid = (m // tm, n // tn, k // tk)
        a_spec = pl.BlockSpec((tk, tm), lambda i, j, kk: (kk, i))
        b_spec = pl.BlockSpec((tk, tn), lambda i, j, kk: (kk, j))
        dims = (((0,), (0,)), ((), ()))
    else:
        tm, tn, tk = _pick(m, MM_TILE, 128), _pick(n, MM_TILE, 128), _pick(k, MM_TILE_K, 128)
        grid = (n // tn, m // tm, k // tk)
        a_spec = pl.BlockSpec((tm, tk), lambda j, i, kk: (i, kk))
        o_spec = pl.BlockSpec((tm, tn), lambda j, i, kk: (i, j))
        if mode == 'nn':
            b_spec = pl.BlockSpec((tk, tn), lambda j, i, kk: (kk, j))
            dims = (((1,), (0,)), ((), ()))
        else:
            b_spec = pl.BlockSpec((tn, tk), lambda j, i, kk: (j, kk))
            dims = (((1,), (1,)), ((), ()))
    nk = grid[2]
    has_add = add is not None

    def body(*refs):
        a_ref, b_ref = refs[0], refs[1]
        add_ref = refs[2] if has_add else None
        o_ref = refs[2 + has_add]
        part = lax.dot_general(a_ref[...].astype(BF16), b_ref[...].astype(BF16), dims, preferred_element_type=F32)

        def finish(total):
            if has_add:
                total = total + add_ref[...]
            if n_slots:
                for s in range(n_slots):
                    o_ref[s] = total[:, s * slot_cols:(s + 1) * slot_cols].astype(out_dtype)
            else:
                o_ref[...] = total.astype(out_dtype)

        if nk == 1:
            finish(part)
        else:
            acc_ref = refs[3 + has_add]
            kk = pl.program_id(2)

            @pl.when(kk == 0)
            def _():
                acc_ref[...] = part

            @pl.when(kk > 0)
            def _():
                acc_ref[...] += part

            @pl.when(kk == nk - 1)
            def _():
                finish(acc_ref[...])

    in_specs = [a_spec, b_spec] + ([o_spec] if has_add else [])
    args = [a, b] + ([add] if has_add else [])
    scratch = [pltpu.VMEM((tm, tn), F32)] if nk > 1 else []
    (out,), moved = _hosted_call(body, ex, name, grid, in_specs, [o_spec], [out_struct], scratch, args)
    return out if ex is None else (out, moved)


def _rmsnorm_fwd(x, gains, name):
    t, d = x.shape
    tm = _pick(t, 512, 8)
    n = len(gains)

    def body(*refs):
        x_ref, g_refs, o_refs = refs[0], refs[1:1 + n], refs[1 + n:]
        xv = x_ref[...]
        y = xv * lax.rsqrt(jnp.mean(xv * xv, axis=-1, keepdims=True) + EPS)
        for g_ref, o_ref in zip(g_refs, o_refs):
            o_ref[...] = (y * g_ref[...]).astype(BF16)

    return pl.pallas_call(
        body, name=name, grid=(t // tm,),
        in_specs=[_rows(tm, d)] + [_const((1, d))] * n,
        out_specs=[_rows(tm, d)] * n,
        out_shape=[jax.ShapeDtypeStruct((t, d), BF16)] * n,
        compiler_params=_params(1),
    )(x, *gains)


def _rmsnorm_bwd(x, gains, dys, add, name):
    t, d = x.shape
    tm = _pick(t, 512, 8)
    n = len(gains)

    def body(*refs):
        x_ref, add_ref = refs[0], refs[1]
        g_refs, dy_refs = refs[2:2 + n], refs[2 + n:2 + 2 * n]
        dx_ref, dg_refs = refs[2 + 2 * n], refs[3 + 2 * n:]
        i = pl.program_id(0)
        xv = x_ref[...]
        r = lax.rsqrt(jnp.mean(xv * xv, axis=-1, keepdims=True) + EPS)
        xhat = xv * r
        dx = add_ref[...]
        for g_ref, dy_ref, dg_ref in zip(g_refs, dy_refs, dg_refs):
            dy = dy_ref[...]
            dyg = dy * g_ref[...]
            dx = dx + r * (dyg - xhat * jnp.mean(dyg * xhat, axis=-1, keepdims=True))
            part = _colsum8(dy * xhat)

            @pl.when(i == 0)
            def _():
                dg_ref[...] = part

            @pl.when(i > 0)
            def _():
                dg_ref[...] += part

        dx_ref[...] = dx

    outs = pl.pallas_call(
        body, name=name, grid=(t // tm,),
        in_specs=[_rows(tm, d), _rows(tm, d)] + [_const((1, d))] * n + [_rows(tm, d)] * n,
        out_specs=[_rows(tm, d)] + [_const((V7X_SUBLANES, d))] * n,
        out_shape=[jax.ShapeDtypeStruct((t, d), F32)] + [jax.ShapeDtypeStruct((V7X_SUBLANES, d), F32)] * n,
        compiler_params=_params(1),
    )(x, add, *gains, *dys)
    return outs[0], [o.sum(axis=0) for o in outs[1:]]


def _ple_fwd(h, gl, pp, name):
    t, d = h.shape
    tm = _pick(t, 512, 8)

    def body(h_ref, gl_ref, pp_ref, o_ref):
        o_ref[...] = h_ref[...] + _sig(gl_ref[...]) * pp_ref[...]

    return pl.pallas_call(
        body, name=name, grid=(t // tm,), in_specs=[_rows(tm, d)] * 3, out_specs=_rows(tm, d),
        out_shape=jax.ShapeDtypeStruct((t, d), F32), compiler_params=_params(1),
    )(h, gl, pp)


def _ple_bwd(dx, gl, pp, name):
    t, d = dx.shape
    tm = _pick(t, 512, 8)

    def body(dx_ref, gl_ref, pp_ref, dgl_ref, dpp_ref):
        dxv = dx_ref[...]
        sg = _sig(gl_ref[...])
        dpp_ref[...] = (dxv * sg).astype(BF16)
        dgl_ref[...] = (dxv * pp_ref[...] * sg * (1.0 - sg)).astype(BF16)

    return pl.pallas_call(
        body, name=name, grid=(t // tm,), in_specs=[_rows(tm, d)] * 3, out_specs=[_rows(tm, d)] * 2,
        out_shape=[jax.ShapeDtypeStruct((t, d), BF16)] * 2, compiler_params=_params(1),
    )(dx, gl, pp)


def _loss_fwd_bwd(y, target, name):
    t, d = y.shape
    tm = _pick(t, 512, 8)
    inv_d = 1.0 / d

    def body(y_ref, t_ref, dy_ref, l_ref):
        i = pl.program_id(0)
        e = y_ref[...] - t_ref[...]
        dy_ref[...] = e * inv_d
        part = _colsum8(e * e) * (0.5 * inv_d)

        @pl.when(i == 0)
        def _():
            l_ref[...] = part

        @pl.when(i > 0)
        def _():
            l_ref[...] += part

    return pl.pallas_call(
        body, name=name, grid=(t // tm,), in_specs=[_rows(tm, d)] * 2,
        out_specs=[_rows(tm, d), _const((V7X_SUBLANES, d))],
        out_shape=[jax.ShapeDtypeStruct((t, d), F32), jax.ShapeDtypeStruct((V7X_SUBLANES, d), F32)],
        compiler_params=_params(1),
    )(y, target)


def _shift_scratch(ts, cc):
    return pltpu.VMEM((V7X_SUBLANES, ts + HALO - V7X_SUBLANES, cc), F32)


def _shifted_copies(sh_ref, win_ref, cs, ts):
    rows = ts + HALO - V7X_SUBLANES
    for s in range(1, V7X_SUBLANES):
        sh_ref[s] = win_ref[pl.ds(s, rows), cs]


def _tap(sh_ref, win_ref, cs, offset, rows, r0):
    s = offset % V7X_SUBLANES
    start = pl.multiple_of(r0 + (offset - s), V7X_SUBLANES)
    if s == 0:
        return win_ref[pl.ds(start, rows), cs]
    return sh_ref[s, pl.ds(start, rows), :]


def _conv_fwd(proj, conv_w, conv_b, ln_g, ln_b, seq, name, ex=None):
    t, c3 = proj.shape
    c = c3 // 3
    ts = _pick(seq, 256, HALO)
    nsb = seq // ts
    cc = _pick(c, 512, V7X_LANES)
    hb = ts // HALO

    def body(a_ref, b_ref, z_ref, ap_ref, bp_ref, w_ref, cb_ref, g_ref, be_ref, m_ref, y_ref, win_ref, sh_ref):
        i = pl.program_id(0)
        first = (i % nsb) == 0
        win_ref[0:HALO, :] = jnp.where(first, 0.0, ap_ref[...] * _sig(bp_ref[...]))
        win_ref[HALO:, :] = a_ref[...] * _sig(b_ref[...])
        for ci in range(c // cc):
            cs = slice(ci * cc, (ci + 1) * cc)
            _shifted_copies(sh_ref, win_ref, cs, ts)

            def out_rows(rb, carry, cs=cs):
                r0 = rb * CONV_ROWS
                acc = jnp.zeros((CONV_ROWS, cc), F32) + cb_ref[:, cs]
                for k in range(CONV_WIDTH):
                    acc = acc + w_ref[k:k + 1, cs] * _tap(sh_ref, win_ref, cs, HALO - (CONV_WIDTH - 1) + k,
                                                           CONV_ROWS, r0)
                y_ref[pl.ds(pl.multiple_of(r0, CONV_ROWS), CONV_ROWS), cs] = acc
                return carry

            lax.fori_loop(0, ts // CONV_ROWS, out_rows, 0, unroll=2)
        y = y_ref[...]
        mu = jnp.mean(y, axis=-1, keepdims=True)
        xc = y - mu
        rstd = lax.rsqrt(jnp.mean(xc * xc, axis=-1, keepdims=True) + EPS)
        ln = xc * rstd * g_ref[...] + be_ref[...]
        zz = z_ref[...]
        m_ref[...] = (ln * _sig(ln) * zz * _sig(zz)).astype(BF16)

    halo_a = pl.BlockSpec((HALO, c), lambda i: (jnp.maximum(i * hb - 1, 0), 0))
    halo_b = pl.BlockSpec((HALO, c), lambda i: (jnp.maximum(i * hb - 1, 0), 1))
    (m_act, y), moved = _hosted_call(
        body, ex, name, (t // ts,),
        [_rows(ts, c, 0), _rows(ts, c, 1), _rows(ts, c, 2), halo_a, halo_b,
         _const((CONV_WIDTH, c)), _const((1, c)), _const((1, c)), _const((1, c))],
        [_rows(ts, c), _rows(ts, c)],
        [jax.ShapeDtypeStruct((t, c), BF16), jax.ShapeDtypeStruct((t, c), F32)],
        [pltpu.VMEM((HALO + ts, c), F32), _shift_scratch(ts, cc)],
        (proj, proj, proj, proj, proj, conv_w, conv_b, ln_g, ln_b))
    return m_act, y, moved


def _ln_gate_bwd(dm, y, proj, ln_g, ln_b, name):
    t, c = y.shape
    tm = _pick(t, 256, 8)

    def body(dm_ref, y_ref, z_ref, g_ref, be_ref, dy_ref, dz_ref, dg_ref, db_ref, dcb_ref):
        i = pl.program_id(0)
        yv = y_ref[...]
        mu = jnp.mean(yv, axis=-1, keepdims=True)
        xc = yv - mu
        rstd = lax.rsqrt(jnp.mean(xc * xc, axis=-1, keepdims=True) + EPS)
        xhat = xc * rstd
        g = g_ref[...]
        ln = xhat * g + be_ref[...]
        sl = _sig(ln)
        zz = z_ref[...]
        sz = _sig(zz)
        dmv = dm_ref[...]
        dz_ref[...] = (dmv * (ln * sl) * (sz * (1.0 + zz * (1.0 - sz)))).astype(BF16)
        dln = dmv * (zz * sz) * (sl * (1.0 + ln * (1.0 - sl)))
        dxh = dln * g
        dyv = rstd * (dxh - jnp.mean(dxh, axis=-1, keepdims=True)
                      - xhat * jnp.mean(dxh * xhat, axis=-1, keepdims=True))
        dy_ref[...] = dyv
        parts = (_colsum8(dln * xhat), _colsum8(dln), _colsum8(dyv))

        @pl.when(i == 0)
        def _():
            for ref, part in zip((dg_ref, db_ref, dcb_ref), parts):
                ref[...] = part

        @pl.when(i > 0)
        def _():
            for ref, part in zip((dg_ref, db_ref, dcb_ref), parts):
                ref[...] += part

    acc = jax.ShapeDtypeStruct((V7X_SUBLANES, c), F32)
    outs = pl.pallas_call(
        body, name=name, grid=(t // tm,),
        in_specs=[_rows(tm, c), _rows(tm, c), _rows(tm, c, 2), _const((1, c)), _const((1, c))],
        out_specs=[_rows(tm, c), _rows(tm, c)] + [_const((V7X_SUBLANES, c))] * 3,
        out_shape=[jax.ShapeDtypeStruct((t, c), F32), jax.ShapeDtypeStruct((t, c), BF16), acc, acc, acc],
        compiler_params=_params(1),
    )(dm, y, proj, ln_g, ln_b)
    return outs[0], outs[1], outs[2].sum(axis=0), outs[3].sum(axis=0), outs[4].sum(axis=0)


def _conv_bwd(dy, dz, proj, conv_w, seq, name, ex=None):
    t, c3 = proj.shape
    c = c3 // 3
    ts = _pick(seq, 256, HALO)
    nsb = seq // ts
    cc = _pick(c, 512, V7X_LANES)
    hb = ts // HALO
    last_halo = t // HALO - 1
    back = CONV_WIDTH - 1

    def body(dy_ref, dyn_ref, dz_ref, a_ref, b_ref, ap_ref, bp_ref, w_ref, o_ref, dw_ref, win_ref, dwin_ref,
             sh_ref, dsh_ref):
        i = pl.program_id(0)
        first = (i % nsb) == 0
        last = (i % nsb) == nsb - 1
        win_ref[0:HALO, :] = jnp.where(first, 0.0, ap_ref[...] * _sig(bp_ref[...]))
        win_ref[HALO:, :] = a_ref[...] * _sig(b_ref[...])
        dwin_ref[0:ts, :] = dy_ref[...]
        dwin_ref[ts:, :] = jnp.where(last, 0.0, dyn_ref[...])

        @pl.when(i == 0)
        def _():
            dw_ref[...] = jnp.zeros_like(dw_ref)

        for ci in range(c // cc):
            cs = slice(ci * cc, (ci + 1) * cc)
            _shifted_copies(sh_ref, win_ref, cs, ts)
            _shifted_copies(dsh_ref, dwin_ref, cs, ts)

            def in_grad_rows(rb, carry, cs=cs, ci=ci):
                r0 = rb * CONV_ROWS
                rows = pl.ds(pl.multiple_of(r0, CONV_ROWS), CONV_ROWS)
                dglu = jnp.zeros((CONV_ROWS, cc), F32)
                for k in range(CONV_WIDTH):
                    dglu = dglu + w_ref[k:k + 1, cs] * _tap(dsh_ref, dwin_ref, cs, back - k, CONV_ROWS, r0)
                sbc = _sig(b_ref[rows, cs])
                o_ref[rows, cs] = (dglu * sbc).astype(BF16)
                o_ref[rows, c + ci * cc:c + (ci + 1) * cc] = (dglu * a_ref[rows, cs] * sbc * (1.0 - sbc)).astype(BF16)
                return carry

            def w_grad_rows(rb, carry, cs=cs):
                r0 = rb * CONV_W_ROWS
                dcur = dwin_ref[pl.ds(pl.multiple_of(r0, CONV_W_ROWS), CONV_W_ROWS), cs]
                for k in range(CONV_WIDTH):
                    dw_ref[k * V7X_SUBLANES:(k + 1) * V7X_SUBLANES, cs] += _colsum8(
                        dcur * _tap(sh_ref, win_ref, cs, HALO - back + k, CONV_W_ROWS, r0))
                return carry

            lax.fori_loop(0, ts // CONV_ROWS, in_grad_rows, 0, unroll=2)
            lax.fori_loop(0, ts // CONV_W_ROWS, w_grad_rows, 0)
        o_ref[:, 2 * c:] = dz_ref[...]

    halo_next = pl.BlockSpec((HALO, c), lambda i: (jnp.minimum((i + 1) * hb, last_halo), 0))
    halo_a = pl.BlockSpec((HALO, c), lambda i: (jnp.maximum(i * hb - 1, 0), 0))
    halo_b = pl.BlockSpec((HALO, c), lambda i: (jnp.maximum(i * hb - 1, 0), 1))
    (dproj, dw), moved = _hosted_call(
        body, ex, name, (t // ts,),
        [_rows(ts, c), halo_next, _rows(ts, c), _rows(ts, c, 0), _rows(ts, c, 1), halo_a, halo_b,
         _const((CONV_WIDTH, c))],
        [_rows(ts, c3), _const((CONV_WIDTH * V7X_SUBLANES, c))],
        [jax.ShapeDtypeStruct((t, c3), BF16), jax.ShapeDtypeStruct((CONV_WIDTH * V7X_SUBLANES, c), F32)],
        [pltpu.VMEM((HALO + ts, c), F32), pltpu.VMEM((ts + HALO, c), F32),
         _shift_scratch(ts, cc), _shift_scratch(ts, cc)],
        (dy, dy, dz, proj, proj, proj, proj, conv_w))
    return dproj, dw.reshape(CONV_WIDTH, V7X_SUBLANES, c).sum(axis=1), moved


def _rope_tables(seq):
    half = ROPE_DIM // 2
    inv = ROPE_THETA ** (-jnp.arange(half, dtype=F32) * (2.0 / ROPE_DIM))
    ang = jnp.arange(seq).astype(F32)[:, None] * inv[None, :]
    cos, sin = jnp.cos(ang), jnp.sin(ang)
    zeros = jnp.zeros((seq, HEAD_DIM - ROPE_DIM), F32)
    zh = jnp.zeros((seq, half), F32)
    a = jnp.concatenate([cos, cos, zeros + 1.0], axis=1)
    b = jnp.concatenate([zh, sin, zeros], axis=1)
    c = jnp.concatenate([-sin, zh, zeros], axis=1)
    rep = V7X_LANES // HEAD_DIM
    return tuple(jnp.tile(v, (1, rep)) for v in (a, b, c))


def _head_ones(d):
    head = jnp.arange(d) // HEAD_DIM
    return (head[:, None] == head[None, :]).astype(BF16)


def _rope(ch, ta, tb, tc):
    return ta * ch + tb * pltpu.roll(ch, ROPE_DIM // 2, 1) + tc * pltpu.roll(ch, V7X_LANES - ROPE_DIM // 2, 1)


def _rope_t(ch, ta, tb, tc):
    return ta * ch + pltpu.roll(tb * ch, V7X_LANES - ROPE_DIM // 2, 1) + pltpu.roll(tc * ch, ROPE_DIM // 2, 1)


def _norm_rope(xv, gain, ta, tb, tc, e_ref):
    r = lax.rsqrt(_segsum(xv * xv, e_ref) * (1.0 / HEAD_DIM) + EPS)
    return _rope(xv * r * gain, ta, tb, tc)


def _norm_rope_bwd(xv, dout, gain, ta, tb, tc, e_ref):
    dxn = _rope_t(dout, ta, tb, tc)
    r = lax.rsqrt(_segsum(xv * xv, e_ref) * (1.0 / HEAD_DIM) + EPS)
    xhat = xv * r
    dxh = dxn * gain
    dx = r * (dxh - xhat * (_segsum(dxh * xhat, e_ref) * (1.0 / HEAD_DIM)))
    return dx, _colsum8(dxn * xhat)


def _norm_rope_rows(dst_ref, src_ref, gain, ta_ref, tb_ref, tc_ref, e_ref, seq):
    for r0 in range(0, seq, ATTN_PIECE):
        rows = slice(r0, r0 + ATTN_PIECE)
        dst_ref[rows, :] = _norm_rope(src_ref[rows, :], gain, ta_ref[rows, :], tb_ref[rows, :], tc_ref[rows, :], e_ref)


ATTN_PIECE = 256
ATTN_UNROLL = 4


def _pieces(dil, seq):
    length = seq // dil
    rows = min(length, ATTN_PIECE)
    return [(r + dil * ci * rows, r * length + ci * rows, rows) for r in range(dil) for ci in range(length // rows)]


def _strided(ref, start, rows, dil):
    if dil == 1:
        return ref[pl.ds(start, rows), :]
    return ref[pl.ds(start, rows, stride=dil), :]


def _strided_set(ref, start, rows, dil, val):
    if dil == 1:
        ref[pl.ds(start, rows), :] = val
    else:
        ref[pl.ds(start, rows, stride=dil), :] = val


def _nt(a, b):
    return lax.dot_general(a, b, (((1,), (1,)), ((), ())), preferred_element_type=F32)


def _tn(a, b):
    return lax.dot_general(a, b, (((0,), (0,)), ((), ())), preferred_element_type=F32)


def _set_bias(bias_ref):
    qi = lax.broadcasted_iota(jnp.int32, (2 * SPAN, 2 * SPAN), 0) & (SPAN - 1)
    kj = lax.broadcasted_iota(jnp.int32, (2 * SPAN, 2 * SPAN), 1)
    band = jnp.logical_and(kj >= qi, (kj - SPAN) <= qi)
    bias_ref[1] = jnp.where(band, 0.0, NEG_INF)
    bias_ref[0] = jnp.where(jnp.logical_and(band, kj >= SPAN), 0.0, NEG_INF)


def _stack_heads(v, head0):
    zero = jnp.zeros_like(v)
    return jnp.concatenate([jnp.where(head0, v, zero), jnp.where(head0, zero, v)], axis=0)


def _unstack_heads(v2, head0):
    return jnp.where(head0, v2[:SPAN], v2[SPAN:])


def _head_cols(v):
    return jnp.concatenate([v[:, 0:1], v[:, HEAD_DIM:HEAD_DIM + 1]], axis=0)


def _attn_fwd(proj_b, kv, gains, tables, ones, bsz, seq, name):
    t, d4 = proj_b.shape
    d = d4 // 4
    nhp = d // V7X_LANES
    nblk = seq // SPAN
    scale = HEAD_DIM ** -0.5
    n_groups = len(DILATIONS)

    def body(q0_ref, q1_ref, q2_ref, k_ref, v_ref, gate_ref, gain_ref, ta_ref, tb_ref, tc_ref, e_ref,
             o_ref, l_ref, ao_ref, qd, kd, vd, od, ld, on0, on1, on2, ln0, ln1, ln2, kn, bias):
        head0 = lax.broadcasted_iota(jnp.int32, (SPAN, V7X_LANES), 1) < HEAD_DIM

        @pl.when(jnp.logical_and(pl.program_id(0) == 0, pl.program_id(1) == 0))
        def _():
            _set_bias(bias)

        _norm_rope_rows(kn, k_ref, gain_ref[n_groups:n_groups + 1, :], ta_ref, tb_ref, tc_ref, e_ref, seq)
        kd[0:SPAN, :] = jnp.zeros((SPAN, V7X_LANES), BF16)
        vd[0:SPAN, :] = jnp.zeros((SPAN, V7X_LANES), BF16)
        for g, (q_ref, on, ln) in enumerate(((q0_ref, on0, ln0), (q1_ref, on1, ln1), (q2_ref, on2, ln2))):
            dil = DILATIONS[g]
            nb = seq // dil // SPAN
            for ns, rs, rows in _pieces(dil, seq):
                ta, tb, tc = (_strided(r_, ns, rows, dil) for r_ in (ta_ref, tb_ref, tc_ref))
                qd[rs:rs + rows, :] = _norm_rope(_strided(q_ref, ns, rows, dil), gain_ref[g:g + 1, :],
                                                 ta, tb, tc, e_ref).astype(BF16)
                kd[SPAN + rs:SPAN + rs + rows, :] = _strided(kn, ns, rows, dil).astype(BF16)
                vd[SPAN + rs:SPAN + rs + rows, :] = _strided(v_ref, ns, rows, dil).astype(BF16)

            def block(j, carry):
                qs = pl.multiple_of(j * SPAN, SPAN)
                q2 = _stack_heads(qd[pl.ds(qs, SPAN), :], head0)
                kk = kd[pl.ds(qs, 2 * SPAN), :]
                vv = vd[pl.ds(qs, 2 * SPAN), :]
                s = _nt(q2, kk) * scale + bias[jnp.minimum(j & (nb - 1), 1)]
                mx = jnp.max(s, axis=1, keepdims=True)
                p = jnp.exp(s - mx)
                den = jnp.sum(p, axis=1, keepdims=True)
                o2 = jnp.dot(p.astype(BF16), vv, preferred_element_type=F32) / den
                l2 = jnp.broadcast_to(mx + jnp.log(den), (2 * SPAN, V7X_LANES))
                od[pl.ds(qs, SPAN), :] = _unstack_heads(o2, head0)
                ld[pl.ds(qs, SPAN), :] = _unstack_heads(l2, head0)
                return carry

            lax.fori_loop(0, nblk, block, 0, unroll=ATTN_UNROLL)
            for ns, rs, rows in _pieces(dil, seq):
                _strided_set(on, ns, rows, dil, od[rs:rs + rows, :])
                _strided_set(ln, ns, rows, dil, ld[rs:rs + rows, :])

        def merge(ci, carry):
            rows = pl.ds(pl.multiple_of(ci * ATTN_PIECE, ATTN_PIECE), ATTN_PIECE)
            ls = [ln0[rows, :], ln1[rows, :], ln2[rows, :]]
            mx = jnp.maximum(jnp.maximum(ls[0], ls[1]), ls[2])
            es = [jnp.exp(v - mx) for v in ls]
            den = es[0] + es[1] + es[2]
            ov = (es[0] * on0[rows, :] + es[1] * on1[rows, :] + es[2] * on2[rows, :]) / den
            gate = gate_ref[rows, :]
            o_ref[rows, :] = ov
            l_ref[rows, :] = mx + jnp.log(den)
            ao_ref[rows, :] = (ov * gate * _sig(gate)).astype(BF16)
            return carry

        lax.fori_loop(0, seq // ATTN_PIECE, merge, 0)

    blk = (None, seq, V7X_LANES)
    pview = proj_b.reshape(bsz, seq, d4)
    kview = kv.reshape(bsz, seq, 2 * d)
    out_spec = pl.BlockSpec(blk, lambda b, h: (b, 0, h))
    tab = pl.BlockSpec((seq, V7X_LANES), lambda b, h: (0, 0))
    nat = pltpu.VMEM((seq, V7X_LANES), F32)
    o, lse, ao = pl.pallas_call(
        body, name=name, grid=(bsz, nhp),
        in_specs=[pl.BlockSpec(blk, lambda b, h: (b, 0, h)),
                  pl.BlockSpec(blk, lambda b, h: (b, 0, nhp + h)),
                  pl.BlockSpec(blk, lambda b, h: (b, 0, 2 * nhp + h)),
                  pl.BlockSpec(blk, lambda b, h: (b, 0, h)),
                  pl.BlockSpec(blk, lambda b, h: (b, 0, nhp + h)),
                  pl.BlockSpec(blk, lambda b, h: (b, 0, 3 * nhp + h)),
                  pl.BlockSpec((n_groups + 1, V7X_LANES), lambda b, h: (0, 0)),
                  tab, tab, tab,
                  pl.BlockSpec((V7X_LANES, V7X_LANES), lambda b, h: (0, 0))],
        out_specs=[out_spec, out_spec, out_spec],
        out_shape=[jax.ShapeDtypeStruct((bsz, seq, d), F32), jax.ShapeDtypeStruct((bsz, seq, d), F32),
                   jax.ShapeDtypeStruct((bsz, seq, d), BF16)],
        scratch_shapes=[pltpu.VMEM((seq, V7X_LANES), BF16), pltpu.VMEM((SPAN + seq, V7X_LANES), BF16),
                        pltpu.VMEM((SPAN + seq, V7X_LANES), BF16), nat, nat, nat, nat, nat, nat, nat, nat, nat,
                        pltpu.VMEM((2, 2 * SPAN, 2 * SPAN), F32)],
        compiler_params=_params(2),
    )(pview, pview, pview, kview, kview, pview, gains, *tables, ones)
    return o.reshape(t, d), lse.reshape(t, d), ao.reshape(t, d)


def _attn_bwd(proj_b, kv, dao, o, lse, gains, tables, ones, bsz, seq, name):
    t, d4 = proj_b.shape
    d = d4 // 4
    nhp = d // V7X_LANES
    nblk = seq // SPAN
    scale = HEAD_DIM ** -0.5
    n_groups = len(DILATIONS)
    n_chunks = seq // ATTN_PIECE

    def body(q_ref, k_ref, v_ref, gate_ref, dao_ref, o_ref, l_ref, gain_ref, ta_ref, tb_ref, tc_ref, e_ref,
             dproj_ref, dkv_ref, dg_ref, qd, kd, vd, dod, ld, deld, dqd, dkd, dvd, dqn, dkn, dvn, kn, bias):
        head0 = lax.broadcasted_iota(jnp.int32, (SPAN, V7X_LANES), 1) < HEAD_DIM
        g = pl.program_id(2)

        @pl.when(jnp.logical_and(jnp.logical_and(pl.program_id(0) == 0, pl.program_id(1) == 0), g == 0))
        def _():
            _set_bias(bias)
            dg_ref[...] = jnp.zeros_like(dg_ref)

        @pl.when(g == 0)
        def _():
            dkn[...] = jnp.zeros_like(dkn)
            dvn[...] = jnp.zeros_like(dvn)
            _norm_rope_rows(kn, k_ref, gain_ref[n_groups:n_groups + 1, :], ta_ref, tb_ref, tc_ref, e_ref, seq)

        def norm_bwd_chunks(x_ref, dn_ref, out_ref, gi):
            def chunk(ci, carry):
                rows = pl.ds(pl.multiple_of(ci * ATTN_PIECE, ATTN_PIECE), ATTN_PIECE)
                dx, part = _norm_rope_bwd(x_ref[rows, :], dn_ref[rows, :], gain_ref[gi:gi + 1, :],
                                          ta_ref[rows, :], tb_ref[rows, :], tc_ref[rows, :], e_ref)
                out_ref[rows, :] = dx.astype(BF16)
                dg_ref[gi] += part
                return carry
            lax.fori_loop(0, n_chunks, chunk, 0, unroll=ATTN_UNROLL)

        def group(gi):
            dil = DILATIONS[gi]
            nb = seq // dil // SPAN
            kd[0:SPAN, :] = jnp.zeros((SPAN, V7X_LANES), BF16)
            vd[0:SPAN, :] = jnp.zeros((SPAN, V7X_LANES), BF16)
            dkd[...] = jnp.zeros_like(dkd)
            dvd[...] = jnp.zeros_like(dvd)
            for ns, rs, rows in _pieces(dil, seq):
                ta, tb, tc = (_strided(r_, ns, rows, dil) for r_ in (ta_ref, tb_ref, tc_ref))
                qd[rs:rs + rows, :] = _norm_rope(_strided(q_ref, ns, rows, dil), gain_ref[gi:gi + 1, :],
                                                 ta, tb, tc, e_ref).astype(BF16)
                kd[SPAN + rs:SPAN + rs + rows, :] = _strided(kn, ns, rows, dil).astype(BF16)
                vd[SPAN + rs:SPAN + rs + rows, :] = _strided(v_ref, ns, rows, dil).astype(BF16)
                gate = _strided(gate_ref, ns, rows, dil)
                dov = _strided(dao_ref, ns, rows, dil) * gate * _sig(gate)
                dod[rs:rs + rows, :] = dov.astype(BF16)
                deld[rs:rs + rows, :] = _segsum(dov * _strided(o_ref, ns, rows, dil), e_ref)
                ld[rs:rs + rows, :] = _strided(l_ref, ns, rows, dil)

            def block(j, carry):
                qs = pl.multiple_of(j * SPAN, SPAN)
                q2 = _stack_heads(qd[pl.ds(qs, SPAN), :], head0)
                do2 = _stack_heads(dod[pl.ds(qs, SPAN), :], head0)
                kk = kd[pl.ds(qs, 2 * SPAN), :]
                vv = vd[pl.ds(qs, 2 * SPAN), :]
                s = _nt(q2, kk) * scale + bias[jnp.minimum(j & (nb - 1), 1)]
                p = jnp.exp(s - _head_cols(ld[pl.ds(qs, SPAN), :]))
                ds = (p * (_nt(do2, vv) - _head_cols(deld[pl.ds(qs, SPAN), :])) * scale).astype(BF16)
                dqd[pl.ds(qs, SPAN), :] = _unstack_heads(jnp.dot(ds, kk, preferred_element_type=F32), head0)
                dkd[pl.ds(qs, 2 * SPAN), :] += _tn(ds, q2)
                dvd[pl.ds(qs, 2 * SPAN), :] += _tn(p.astype(BF16), do2)
                return carry

            lax.fori_loop(0, nblk, block, 0, unroll=ATTN_UNROLL)
            for ns, rs, rows in _pieces(dil, seq):
                _strided_set(dqn, ns, rows, dil, dqd[rs:rs + rows, :])
                _strided_set(dkn, ns, rows, dil,
                             _strided(dkn, ns, rows, dil) + dkd[SPAN + rs:SPAN + rs + rows, :])
                _strided_set(dvn, ns, rows, dil,
                             _strided(dvn, ns, rows, dil) + dvd[SPAN + rs:SPAN + rs + rows, :])
            norm_bwd_chunks(q_ref, dqn, dproj_ref, gi)

        for gi in range(n_groups):
            @pl.when(g == gi)
            def _():
                group(gi)

        @pl.when(g == n_groups - 1)
        def _():
            norm_bwd_chunks(k_ref, dkn, dkv_ref, n_groups)

        @pl.when(g == n_groups)
        def _():
            def chunk(ci, carry):
                rows = pl.ds(pl.multiple_of(ci * ATTN_PIECE, ATTN_PIECE), ATTN_PIECE)
                gate = gate_ref[rows, :]
                sg = _sig(gate)
                dproj_ref[rows, :] = (dao_ref[rows, :] * o_ref[rows, :]
                                      * (sg * (1.0 + gate * (1.0 - sg)))).astype(BF16)
                dkv_ref[rows, :] = dvn[rows, :].astype(BF16)
                return carry
            lax.fori_loop(0, n_chunks, chunk, 0, unroll=ATTN_UNROLL)

    blk = (None, seq, V7X_LANES)
    pview = proj_b.reshape(bsz, seq, d4)
    kview = kv.reshape(bsz, seq, 2 * d)
    dview = (bsz, seq, d)
    d_spec = pl.BlockSpec(blk, lambda b, h, g: (b, 0, h))
    tab = pl.BlockSpec((seq, V7X_LANES), lambda b, h, g: (0, 0))
    nat = pltpu.VMEM((seq, V7X_LANES), F32)
    natb = pltpu.VMEM((seq, V7X_LANES), BF16)
    pad = pltpu.VMEM((SPAN + seq, V7X_LANES), F32)
    padb = pltpu.VMEM((SPAN + seq, V7X_LANES), BF16)
    dproj, dkv, dg = pl.pallas_call(
        body, name=name, grid=(bsz, nhp, n_groups + 1),
        in_specs=[pl.BlockSpec(blk, lambda b, h, g: (b, 0, jnp.minimum(g, n_groups - 1) * nhp + h)),
                  pl.BlockSpec(blk, lambda b, h, g: (b, 0, h)),
                  pl.BlockSpec(blk, lambda b, h, g: (b, 0, nhp + h)),
                  pl.BlockSpec(blk, lambda b, h, g: (b, 0, n_groups * nhp + h)),
                  d_spec, d_spec, d_spec,
                  pl.BlockSpec((n_groups + 1, V7X_LANES), lambda b, h, g: (0, 0)),
                  tab, tab, tab,
                  pl.BlockSpec((V7X_LANES, V7X_LANES), lambda b, h, g: (0, 0))],
        out_specs=[pl.BlockSpec(blk, lambda b, h, g: (b, 0, g * nhp + h)),
                   pl.BlockSpec(blk, lambda b, h, g: (b, 0, (g // n_groups) * nhp + h)),
                   pl.BlockSpec((n_groups + 1, V7X_SUBLANES, V7X_LANES), lambda b, h, g: (0, 0, 0))],
        out_shape=[jax.ShapeDtypeStruct((bsz, seq, d4), BF16), jax.ShapeDtypeStruct((bsz, seq, 2 * d), BF16),
                   jax.ShapeDtypeStruct((n_groups + 1, V7X_SUBLANES, V7X_LANES), F32)],
        scratch_shapes=[natb, padb, padb, natb, nat, nat, nat, pad, pad, nat, nat, nat, nat,
                        pltpu.VMEM((2, 2 * SPAN, 2 * SPAN), F32)],
        compiler_params=_params(3),
    )(pview, kview, kview, pview, dao.reshape(dview), o.reshape(dview), lse.reshape(dview), gains, *tables, ones)
    dgain = dg.sum(axis=1).reshape(n_groups + 1, V7X_LANES // HEAD_DIM, HEAD_DIM).sum(axis=1)
    return dproj.reshape(t, d4), dkv.reshape(t, 2 * d), dgain


def _mesh_position():
    x, y, c = lax.axis_index("x"), lax.axis_index("y"), lax.axis_index("c")
    return x, y, c


def _peer(x, y, c, rel):
    return (1 - x if rel & 4 else x, 1 - y if rel & 2 else y, 1 - c if rel & 1 else c)


class _Exchange:
    def __init__(self, srcs, gather):
        self.srcs = list(srcs)
        self.gather = gather
        n = self.n = len(self.srcs)
        hbm = pl.BlockSpec(memory_space=pltpu.HBM)
        self.in_specs = [hbm] * n
        self.out_specs = [hbm] * n
        self.out_shape = [jax.ShapeDtypeStruct(((N_DEV,) + a.shape) if gather else a.shape, a.dtype)
                          for a in self.srcs]
        self.scratch = [pltpu.SemaphoreType.DMA((n * (N_DEV - 1),)), pltpu.SemaphoreType.DMA((n * (N_DEV - 1),)),
                        pltpu.SemaphoreType.DMA((n,))]

    def _copies(self, ins, outs, sems):
        send_sems, recv_sems, local_sems = sems
        x, y, c = _mesh_position()
        me = 4 * x + 2 * y + c
        remote, local = [], []
        for a in range(self.n):
            mine = ins[a] if self.gather else ins[a].at[me]
            local.append(pltpu.make_async_copy(mine, outs[a].at[me], local_sems.at[a]))
            for rel in range(1, N_DEV):
                px, py, pc = _peer(x, y, c, rel)
                s = a * (N_DEV - 1) + rel - 1
                src = ins[a] if self.gather else ins[a].at[4 * px + 2 * py + pc]
                remote.append(pltpu.make_async_remote_copy(
                    src_ref=src, dst_ref=outs[a].at[me], send_sem=send_sems.at[s], recv_sem=recv_sems.at[s],
                    device_id=(px, py, pc), device_id_type=pl.DeviceIdType.MESH))
        return remote, local

    def start(self, ins, outs, sems):
        remote, local = self._copies(ins, outs, sems)
        for cp in local + remote:
            cp.start()

    def wait(self, ins, outs, sems):
        remote, local = self._copies(ins, outs, sems)
        for cp in remote:
            cp.wait_recv()
        for cp in remote:
            cp.wait_send()
        for cp in local:
            cp.wait()


def _run_exchange(ex, name):
    n = ex.n

    def body(*refs):
        ins, outs, sems = refs[:n], refs[n:2 * n], refs[2 * n:]
        ex.start(ins, outs, sems)
        ex.wait(ins, outs, sems)

    return pl.pallas_call(body, name=name, in_specs=ex.in_specs, out_specs=ex.out_specs, out_shape=ex.out_shape,
                          scratch_shapes=ex.scratch)(*ex.srcs)


def _hosted_call(body, ex, name, grid, in_specs, out_specs, out_shape, scratch_shapes, args):
    if ex is None:
        outs = pl.pallas_call(body, name=name, grid=grid, in_specs=in_specs, out_specs=out_specs, out_shape=out_shape,
                              scratch_shapes=scratch_shapes, compiler_params=_params(len(grid)))(*args)
        return list(outs), []
    n_in, n_out, n_scr, n = len(in_specs), len(out_specs), len(scratch_shapes), ex.n

    def hosted(*refs):
        h_in, e_in = refs[:n_in], refs[n_in:n_in + n]
        o0 = n_in + n
        h_out, e_out = refs[o0:o0 + n_out], refs[o0 + n_out:o0 + n_out + n]
        s0 = o0 + n_out + n
        h_scr, e_scr = refs[s0:s0 + n_scr], refs[s0 + n_scr:]
        ids = [pl.program_id(a) for a in range(len(grid))]
        first = functools.reduce(jnp.logical_and, [i == 0 for i in ids])
        last = functools.reduce(jnp.logical_and, [i == g - 1 for i, g in zip(ids, grid)])

        @pl.when(first)
        def _():
            ex.start(e_in, e_out, e_scr)

        body(*h_in, *h_out, *h_scr)

        @pl.when(last)
        def _():
            ex.wait(e_in, e_out, e_scr)

    outs = pl.pallas_call(
        hosted, name=name, grid=grid, in_specs=list(in_specs) + ex.in_specs,
        out_specs=list(out_specs) + ex.out_specs, out_shape=list(out_shape) + ex.out_shape,
        scratch_shapes=list(scratch_shapes) + ex.scratch, compiler_params=_params(len(grid)),
    )(*args, *ex.srcs)
    return list(outs[:n_out]), list(outs[n_out:])


def _sum_adamw(parts, w, m, v, name):
    _, r, wd = parts.shape
    tr = _pick(r, ADAM_ROWS, 8)
    c1 = 1.0 - ADAM_B1 ** ADAM_STEP
    c2 = 1.0 - ADAM_B2 ** ADAM_STEP

    def body(p_ref, w_ref, m_ref, v_ref, g_ref, d_ref, nm_ref, nv_ref):
        g = p_ref[0].astype(F32)
        for s in range(1, N_DEV):
            g = g + p_ref[s].astype(F32)
        nm = ADAM_B1 * m_ref[...] + (1.0 - ADAM_B1) * g
        nv = ADAM_B2 * v_ref[...] + (1.0 - ADAM_B2) * (g * g)
        g_ref[...] = g
        nm_ref[...] = nm
        nv_ref[...] = nv
        d_ref[...] = -ADAM_LR * ((nm / c1) / (jnp.sqrt(nv / c2) + ADAM_EPS) + ADAM_WD * w_ref[...])

    row = pl.BlockSpec((tr, wd), lambda i: (i, 0))
    return pl.pallas_call(
        body, name=name, grid=(r // tr,),
        in_specs=[pl.BlockSpec((N_DEV, tr, wd), lambda i: (0, i, 0)), row, row, row],
        out_specs=[row] * 4, out_shape=[jax.ShapeDtypeStruct((r, wd), F32)] * 4,
        compiler_params=_params(1),
    )(parts, w, m, v)


def _pack_rows(size, row_mult):
    rows = -(-size // PACK_LANES)
    return -(-rows // row_mult) * row_mult


def _pack(flats, row_mult, dtype, total_mult=None):
    out = []
    for f in flats:
        size = f.shape[-1]
        rows = _pack_rows(size, row_mult)
        pad = [(0, 0)] * (f.ndim - 1) + [(0, rows * PACK_LANES - size)]
        out.append(jnp.pad(f.astype(dtype), pad).reshape(f.shape[:-1] + (rows, PACK_LANES)))
    if total_mult is not None:
        total = sum(o.shape[-2] for o in out)
        extra = -(-total // total_mult) * total_mult - total
        if extra:
            out.append(jnp.zeros(out[0].shape[:-2] + (extra, PACK_LANES), dtype))
    return jnp.concatenate(out, axis=-2)


def _unpack(buf, sizes, row_mult):
    out, row = [], 0
    for size in sizes:
        rows = _pack_rows(size, row_mult)
        part = buf[..., row:row + rows, :]
        out.append(part.reshape(buf.shape[:-2] + (rows * PACK_LANES,))[..., :size])
        row += rows
    return out


def _to_slots(full, axis):
    if axis is None:
        return jnp.broadcast_to(full.reshape(1, -1), (N_DEV, full.size))
    shape = full.shape
    split = full.reshape(shape[:axis] + (N_DEV, shape[axis] // N_DEV) + shape[axis + 1:])
    return jnp.moveaxis(split, axis, 0).reshape(N_DEV, -1)


def _from_slots(slots, axis, block_shape):
    split = jnp.moveaxis(slots, 0, axis)
    shape = list(block_shape)
    shape[axis] *= N_DEV
    return split.reshape(shape)


def kernel(x, p, norm_g, w_in_a, conv_w, conv_b, ln_g, ln_b, w_out_a, kv_norm_g, w_kv, k_norm_g, w_in_b, q_norm_g, w_out_b, ple_norm_g, w_ple_gate, w_ple_proj, loss_target, m_norm_g, m_w_in_a, m_conv_w, m_conv_b, m_ln_g, m_ln_b, m_w_out_a, m_kv_norm_g, m_w_kv, m_k_norm_g, m_w_in_b, m_q_norm_g, m_w_out_b, m_ple_norm_g, m_w_ple_gate, m_w_ple_proj, v_norm_g, v_w_in_a, v_conv_w, v_conv_b, v_ln_g, v_ln_b, v_w_out_a, v_kv_norm_g, v_w_kv, v_k_norm_g, v_w_in_b, v_q_norm_g, v_w_out_b, v_ple_norm_g, v_w_ple_gate, v_w_ple_proj):
    weights = dict(zip(WEIGHT_NAMES, (norm_g, w_in_a, conv_w, conv_b, ln_g, ln_b, w_out_a, kv_norm_g, w_kv, k_norm_g,
                                      w_in_b, q_norm_g, w_out_b, ple_norm_g, w_ple_gate, w_ple_proj)))
    mom_m = dict(zip(WEIGHT_NAMES, (m_norm_g, m_w_in_a, m_conv_w, m_conv_b, m_ln_g, m_ln_b, m_w_out_a, m_kv_norm_g,
                                    m_w_kv, m_k_norm_g, m_w_in_b, m_q_norm_g, m_w_out_b, m_ple_norm_g, m_w_ple_gate,
                                    m_w_ple_proj)))
    mom_v = dict(zip(WEIGHT_NAMES, (v_norm_g, v_w_in_a, v_conv_w, v_conv_b, v_ln_g, v_ln_b, v_w_out_a, v_kv_norm_g,
                                    v_w_kv, v_k_norm_g, v_w_in_b, v_q_norm_g, v_w_out_b, v_ple_norm_g, v_w_ple_gate,
                                    v_w_ple_proj)))
    bsz, seq, d = x.shape
    t = bsz * seq
    assert seq % (max(DILATIONS) * SPAN) == 0 and d % V7X_LANES == 0

    full = {}

    def rows2d(a):
        return a.reshape(-1, a.shape[-1])

    def packed(source, names, dtype, total_mult=None):
        return _pack([source[n].reshape(-1) for n in names], 16, dtype, total_mult)

    def gathered(names, bufs):
        for n, buf in zip(names, bufs):
            full[n] = _from_slots(buf.reshape((N_DEV,) + weights[n].shape), SHARD_AXIS[n], weights[n].shape)

    w1_all, wv_all = _run_exchange(_Exchange([rows2d(weights['w_in_a']).astype(BF16),
                                              _pack([weights[n].reshape(-1) for n in VECTOR_WEIGHTS], 8, F32)],
                                             gather=True), "gather_first")
    gathered(GROUP_FIRST, [w1_all])
    for n, slots in zip(VECTOR_WEIGHTS, _unpack(wv_all, [weights[n].size for n in VECTOR_WEIGHTS], 8)):
        full[n] = _from_slots(slots.reshape((N_DEV,) + weights[n].shape), SHARD_AXIS[n], weights[n].shape)
    gather_rest = _Exchange([rows2d(weights[n]).astype(BF16) for n in GROUP_REST], gather=True)
    wa_in = full['w_in_a'][0]
    cw, cb, lg, lb = full['conv_w'][0], full['conv_b'], full['ln_g'], full['ln_b']

    tables = _rope_tables(seq)
    ones = _head_ones(V7X_LANES)
    rep = V7X_LANES // HEAD_DIM
    head_gain = jnp.concatenate([jnp.tile(q_norm_g[0], (1, rep)), jnp.tile(k_norm_g, rep)[None]], axis=0)

    x0 = x.reshape(t, d)
    p0, p1 = p[0].reshape(t, -1), p[1].reshape(t, -1)
    target = loss_target.reshape(t, d)
    g_norm0, g_norm1 = norm_g[0:1], norm_g[1:2]
    g_ple0, g_ple1 = ple_norm_g[0:1], ple_norm_g[1:2]
    g_kv = kv_norm_g.reshape(1, d)

    (u0,) = _rmsnorm_fwd(x0, [g_norm0], "norm0")
    proj_a = _matmul(u0, wa_in, 'nn', "in_a")
    m_act, y_conv, w2_all = _conv_fwd(proj_a, cw, cb, lg, lb, seq, "conv_fwd", ex=gather_rest)
    gathered(GROUP_REST, w2_all)
    wa_out = full['w_out_a'][0]
    wkv = full['w_kv']
    wb_in, wb_out = full['w_in_b'][0], full['w_out_b'][0]
    wg, wp = full['w_ple_gate'], full['w_ple_proj']
    h0 = _matmul(m_act, wa_out, 'nn', "out_a", add=x0)
    (pg0,) = _rmsnorm_fwd(h0, [g_ple0], "ple_norm0")
    gl0 = _matmul(pg0, wg[0], 'nn', "ple_gate0")
    pp0 = _matmul(p0, wp[0], 'nn', "ple_proj0")
    x1 = _ple_fwd(h0, gl0, pp0, "ple0")

    kvn, u1 = _rmsnorm_fwd(x1, [g_kv, g_norm1], "norm1")
    kv = _matmul(kvn, wkv, 'nn', "kv")
    proj_b = _matmul(u1, wb_in, 'nn', "in_b")
    o_att, lse, ao = _attn_fwd(proj_b, kv, head_gain, tables, ones, bsz, seq, "attn_fwd")
    h1 = _matmul(ao, wb_out, 'nn', "out_b", add=x1)
    (pg1,) = _rmsnorm_fwd(h1, [g_ple1], "ple_norm1")
    gl1 = _matmul(pg1, wg[1], 'nn', "ple_gate1")
    pp1 = _matmul(p1, wp[1], 'nn', "ple_proj1")
    x2 = _ple_fwd(h1, gl1, pp1, "ple1")

    dx2, loss_part = _loss_fwd_bwd(x2, target, "loss")
    loss = lax.psum(jnp.sum(loss_part), ("x", "y", "c"))

    grads = {}
    slot = {}

    dgl1, dpp1 = _ple_bwd(dx2, gl1, pp1, "ple1_bwd")
    dwp1 = _matmul(p1, dpp1, 'tn', "d_ple_proj1", out_dtype=BF16, slot_cols=d // N_DEV)
    dwg1 = _matmul(pg1, dgl1, 'tn', "d_ple_gate1", out_dtype=BF16)
    dpg1 = _matmul(dgl1, wg[1], 'nt', "d_ple_norm1")
    dh1, (dg_ple1,) = _rmsnorm_bwd(h1, [g_ple1], [dpg1], dx2, "ple_norm1_bwd")
    slot['w_out_b'] = _matmul(ao, dh1, 'tn', "d_out_b", out_dtype=BF16).reshape(N_DEV, -1, d)
    dao = _matmul(dh1, wb_out, 'nt', "d_ao")
    dproj_b, dkv, dg_head = _attn_bwd(proj_b, kv, dao, o_att, lse, head_gain, tables, ones, bsz, seq, "attn_bwd")
    slot['w_in_b'] = _matmul(u1, dproj_b, 'tn', "d_in_b", out_dtype=BF16, slot_cols=4 * d // N_DEV)
    du1 = _matmul(dproj_b, wb_in, 'nt', "d_u1")
    slot['w_kv'] = _matmul(kvn, dkv, 'tn', "d_kv", out_dtype=BF16, slot_cols=2 * d // N_DEV)
    dkvn = _matmul(dkv, wkv, 'nt', "d_kvn")
    dx1, (dg_kv, dg_norm1) = _rmsnorm_bwd(x1, [g_kv, g_norm1], [dkvn, du1], dh1, "norm1_bwd")

    dgl0, dpp0 = _ple_bwd(dx1, gl0, pp0, "ple0_bwd")
    dwp0 = _matmul(p0, dpp0, 'tn', "d_ple_proj0", out_dtype=BF16, slot_cols=d // N_DEV)
    dwg0 = _matmul(pg0, dgl0, 'tn', "d_ple_gate0", out_dtype=BF16)
    dpg0 = _matmul(dgl0, wg[0], 'nt', "d_ple_norm0")
    dh0, (dg_ple0,) = _rmsnorm_bwd(h0, [g_ple0], [dpg0], dx1, "ple_norm0_bwd")
    slot['w_out_a'] = _matmul(m_act, dh0, 'tn', "d_out_a", out_dtype=BF16).reshape(N_DEV, -1, d)
    dm = _matmul(dh0, wa_out, 'nt', "d_m")
    dy_conv, dz, d_lg, d_lb, d_cb = _ln_gate_bwd(dm, y_conv, proj_a, lg, lb, "ln_gate_bwd")
    slot['w_ple_gate'] = jnp.stack([dwg0.reshape(N_DEV, -1, d), dwg1.reshape(N_DEV, -1, d)],
                                   axis=1).reshape(N_DEV, -1, d)
    slot['w_ple_proj'] = jnp.stack([dwp0, dwp1], axis=1).reshape(N_DEV, -1, d // N_DEV)

    dproj_a, d_cw, parts_rest = _conv_bwd(dy_conv, dz, proj_a, cw, seq, "conv_bwd",
                                          ex=_Exchange([slot[n] for n in GROUP_REST], gather=False))
    slot['w_in_a'] = _matmul(u0, dproj_a, 'tn', "d_in_a", out_dtype=BF16, slot_cols=wa_in.shape[1] // N_DEV)
    du0, parts_first = _matmul(dproj_a, wa_in, 'nt', "d_u0", ex=_Exchange([slot['w_in_a']], gather=False))
    dx0, (dg_norm0,) = _rmsnorm_bwd(x0, [g_norm0], [du0], dh0, "norm0_bwd")

    grads['norm_g'] = jnp.stack([dg_norm0, dg_norm1])
    grads['conv_w'] = d_cw[None]
    grads['conv_b'] = d_cb[None]
    grads['ln_g'] = d_lg[None]
    grads['ln_b'] = d_lb[None]
    grads['kv_norm_g'] = dg_kv
    grads['k_norm_g'] = dg_head[3]
    grads['q_norm_g'] = dg_head[0:3][None]
    grads['ple_norm_g'] = jnp.stack([dg_ple0, dg_ple1])
    small_pack = _pack([_to_slots(grads[n], SHARD_AXIS[n]) for n in GROUP_SMALL], 16, BF16)
    (parts_small,) = _run_exchange(_Exchange([small_pack], gather=False), "exchange_small")

    updated = {}
    for n, parts in zip(GROUP_REST + GROUP_FIRST, parts_rest + parts_first):
        outs = _sum_adamw(parts, rows2d(weights[n]), rows2d(mom_m[n]), rows2d(mom_v[n]), "sum_adamw_" + n)
        for kind, buf in enumerate(outs):
            updated[kind, n] = buf.reshape(weights[n].shape)
    outs = _sum_adamw(parts_small, packed(weights, GROUP_SMALL, F32), packed(mom_m, GROUP_SMALL, F32),
                      packed(mom_v, GROUP_SMALL, F32), "sum_adamw_small")
    sizes = [weights[n].size for n in GROUP_SMALL]
    for kind, buf in enumerate(outs):
        for n, flat in zip(GROUP_SMALL, _unpack(buf, sizes, 16)):
            updated[kind, n] = flat.reshape(weights[n].shape)
    result = [loss, dx0.reshape(bsz, seq, d)]
    for kind in range(4):
        result.extend(updated[kind, n] for n in WEIGHT_NAMES)
    return tuple(result)
```

```python
import functools

import jax
import jax.numpy as jnp
from jax import lax
from jax.experimental import pallas as pl
from jax.experimental.pallas import tpu as pltpu

F32 = jnp.float32
BF16 = jnp.bfloat16

N_DEV = 8
HEAD_DIM = 64
ROPE_DIM = 16
ROPE_THETA = 500000.0
EPS = 1e-6
NEG_INF = -1e30
SPAN = 128
DILATIONS = (1, 4, 16)
CONV_WIDTH = 31
HALO = 32
CONV_ROWS = 32
CONV_W_ROWS = 64
PACK_LANES = 1024
V7X_LANES = 128
V7X_SUBLANES = 8
VMEM_LIMIT_BYTES = 56 * 1024 * 1024

ADAM_LR = 0.001
ADAM_B1 = 0.9
ADAM_B2 = 0.999
ADAM_EPS = 1e-08
ADAM_WD = 0.01
ADAM_STEP = 10
ADAM_ROWS = 256

WEIGHT_NAMES = ('norm_g', 'w_in_a', 'conv_w', 'conv_b', 'ln_g', 'ln_b', 'w_out_a', 'kv_norm_g', 'w_kv',
                'k_norm_g', 'w_in_b', 'q_norm_g', 'w_out_b', 'ple_norm_g', 'w_ple_gate', 'w_ple_proj')
SHARD_AXIS = {'norm_g': None, 'w_in_a': 2, 'conv_w': 2, 'conv_b': 1, 'ln_g': 1, 'ln_b': 1, 'w_out_a': 1,
              'kv_norm_g': None, 'w_kv': 1, 'k_norm_g': None, 'w_in_b': 2, 'q_norm_g': None, 'w_out_b': 1,
              'ple_norm_g': None, 'w_ple_gate': 1, 'w_ple_proj': 2}
VECTOR_WEIGHTS = ('conv_w', 'conv_b', 'ln_g', 'ln_b')
GROUP_FIRST = ('w_in_a',)
GROUP_REST = ('w_out_a', 'w_kv', 'w_in_b', 'w_out_b', 'w_ple_gate', 'w_ple_proj')
GROUP_SMALL = ('norm_g', 'conv_w', 'conv_b', 'ln_g', 'ln_b', 'kv_norm_g', 'k_norm_g', 'q_norm_g', 'ple_norm_g')


def _pick(n, target, mult):
    t = (min(target, n) // mult) * mult
    while t >= mult:
        if n % t == 0:
            return t
        t -= mult
    return n


def _params(n_grid):
    return pltpu.CompilerParams(dimension_semantics=("arbitrary",) * n_grid, vmem_limit_bytes=VMEM_LIMIT_BYTES)


def _sig(x):
    return 0.5 * jnp.tanh(0.5 * x) + 0.5


def _colsum8(v):
    r, w = v.shape
    return v.reshape(r // V7X_SUBLANES, V7X_SUBLANES, w).sum(axis=0)


def _rows(tm, w, col=0):
    return pl.BlockSpec((tm, w), lambda i: (i, col))


def _const(shape):
    nd = len(shape)
    return pl.BlockSpec(shape, lambda i: (0,) * nd)


def _segsum(v, e_ref):
    hi = v.astype(BF16)
    lo = (v - hi.astype(F32)).astype(BF16)
    e = e_ref[...]
    return jnp.dot(hi, e, preferred_element_type=F32) + jnp.dot(lo, e, preferred_element_type=F32)


MM_TILE = 1024
MM_TILE_K = 2048


def _matmul(a, b, mode, name, out_dtype=F32, add=None, ex=None, slot_cols=None):
    if mode == 'nn':
        (m, k), (_, n) = a.shape, b.shape
    elif mode == 'nt':
        (m, k), (n, _) = a.shape, b.shape
    else:
        (k, m), (_, n) = a.shape, b.shape
    out_struct = jax.ShapeDtypeStruct((m, n), out_dtype)
    n_slots = 0
    if mode == 'tn':
        tm, tn, tk = _pick(m, MM_TILE, 128), _pick(n, MM_TILE, 128), _pick(k, MM_TILE_K, 128)
        o_spec = pl.BlockSpec((tm, tn), lambda i, j, kk: (i, j))
        if slot_cols is not None:
            assert n == N_DEV * slot_cols
            n_slots = max(s for s in (1, 2, 4, 8) if s == 1 or slot_cols * s <= MM_TILE)
            tn = slot_cols * n_slots
            o_spec = pl.BlockSpec((n_slots, tm, slot_cols), lambda i, j, kk: (j, i, 0))
            out_struct = jax.ShapeDtypeStruct((N_DEV, m, slot_cols), out_dtype)
        grid = (m // tm, n // tn, k // tk)
        a_spec = pl.BlockSpec((tk, tm), lambda i, j, kk: (kk, i))
        b_spec = pl.BlockSpec((tk, tn), lambda i, j, kk: (kk, j))
        dims = (((0,), (0,)), ((), ()))
    else:
        tm, tn, tk = _pick(m, MM_TILE, 128), _pick(n, MM_TILE, 128), _pick(k, MM_TILE_K, 128)
        grid = (n // tn, m // tm, k // tk)
        a_spec = pl.BlockSpec((tm, tk), lambda j, i, kk: (i, kk))
        o_spec = pl.BlockSpec((tm, tn), lambda j, i, kk: (i, j))
        if mode == 'nn':
            b_spec = pl.BlockSpec((tk, tn), lambda j, i, kk: (kk, j))
            dims = (((1,), (0,)), ((), ()))
        else:
            b_spec = pl.BlockSpec((tn, tk), lambda j, i, kk: (j, kk))
            dims = (((1,), (1,)), ((), ()))
    nk = grid[2]
    has_add = add is not None

    def body(*refs):
        a_ref, b_ref = refs[0], refs[1]
        add_ref = refs[2] if has_add else None
        o_ref = refs[2 + has_add]
        part = lax.dot_general(a_ref[...].astype(BF16), b_ref[...].astype(BF16), dims, preferred_element_type=F32)

        def finish(total):
            if has_add:
                total = total + add_ref[...]
            if n_slots:
                for s in range(n_slots):
                    o_ref[s] = total[:, s * slot_cols:(s + 1) * slot_cols].astype(out_dtype)
            else:
                o_ref[...] = total.astype(out_dtype)

        if nk == 1:
            finish(part)
        else:
            acc_ref = refs[3 + has_add]
            kk = pl.program_id(2)

            @pl.when(kk == 0)
            def _():
                acc_ref[...] = part

            @pl.when(kk > 0)
            def _():
                acc_ref[...] += part

            @pl.when(kk == nk - 1)
            def _():
                finish(acc_ref[...])

    in_specs = [a_spec, b_spec] + ([o_spec] if has_add else [])
    args = [a, b] + ([add] if has_add else [])
    scratch = [pltpu.VMEM((tm, tn), F32)] if nk > 1 else []
    (out,), moved = _hosted_call(body, ex, name, grid, in_specs, [o_spec], [out_struct], scratch, args)
    return out if ex is None else (out, moved)


def _rmsnorm_fwd(x, gains, name):
    t, d = x.shape
    tm = _pick(t, 512, 8)
    n = len(gains)

    def body(*refs):
        x_ref, g_refs, o_refs = refs[0], refs[1:1 + n], refs[1 + n:]
        xv = x_ref[...]
        y = xv * lax.rsqrt(jnp.mean(xv * xv, axis=-1, keepdims=True) + EPS)
        for g_ref, o_ref in zip(g_refs, o_refs):
            o_ref[...] = (y * g_ref[...]).astype(BF16)

    return pl.pallas_call(
        body, name=name, grid=(t // tm,),
        in_specs=[_rows(tm, d)] + [_const((1, d))] * n,
        out_specs=[_rows(tm, d)] * n,
        out_shape=[jax.ShapeDtypeStruct((t, d), BF16)] * n,
        compiler_params=_params(1),
    )(x, *gains)


def _rmsnorm_bwd(x, gains, dys, add, name):
    t, d = x.shape
    tm = _pick(t, 512, 8)
    n = len(gains)

    def body(*refs):
        x_ref, add_ref = refs[0], refs[1]
        g_refs, dy_refs = refs[2:2 + n], refs[2 + n:2 + 2 * n]
        dx_ref, dg_refs = refs[2 + 2 * n], refs[3 + 2 * n:]
        i = pl.program_id(0)
        xv = x_ref[...]
        r = lax.rsqrt(jnp.mean(xv * xv, axis=-1, keepdims=True) + EPS)
        xhat = xv * r
        dx = add_ref[...]
        for g_ref, dy_ref, dg_ref in zip(g_refs, dy_refs, dg_refs):
            dy = dy_ref[...]
            dyg = dy * g_ref[...]
            dx = dx + r * (dyg - xhat * jnp.mean(dyg * xhat, axis=-1, keepdims=True))
            part = _colsum8(dy * xhat)

            @pl.when(i == 0)
            def _():
                dg_ref[...] = part

            @pl.when(i > 0)
            def _():
                dg_ref[...] += part

        dx_ref[...] = dx

    outs = pl.pallas_call(
        body, name=name, grid=(t // tm,),
        in_specs=[_rows(tm, d), _rows(tm, d)] + [_const((1, d))] * n + [_rows(tm, d)] * n,
        out_specs=[_rows(tm, d)] + [_const((V7X_SUBLANES, d))] * n,
        out_shape=[jax.ShapeDtypeStruct((t, d), F32)] + [jax.ShapeDtypeStruct((V7X_SUBLANES, d), F32)] * n,
        compiler_params=_params(1),
    )(x, add, *gains, *dys)
    return outs[0], [o.sum(axis=0) for o in outs[1:]]


def _ple_fwd(h, gl, pp, name):
    t, d = h.shape
    tm = _pick(t, 512, 8)

    def body(h_ref, gl_ref, pp_ref, o_ref):
        o_ref[...] = h_ref[...] + _sig(gl_ref[...]) * pp_ref[...]

    return pl.pallas_call(
        body, name=name, grid=(t // tm,), in_specs=[_rows(tm, d)] * 3, out_specs=_rows(tm, d),
        out_shape=jax.ShapeDtypeStruct((t, d), F32), compiler_params=_params(1),
    )(h, gl, pp)


def _ple_bwd(dx, gl, pp, name):
    t, d = dx.shape
    tm = _pick(t, 512, 8)

    def body(dx_ref, gl_ref, pp_ref, dgl_ref, dpp_ref):
        dxv = dx_ref[...]
        sg = _sig(gl_ref[...])
        dpp_ref[...] = (dxv * sg).astype(BF16)
        dgl_ref[...] = (dxv * pp_ref[...] * sg * (1.0 - sg)).astype(BF16)

    return pl.pallas_call(
        body, name=name, grid=(t // tm,), in_specs=[_rows(tm, d)] * 3, out_specs=[_rows(tm, d)] * 2,
        out_shape=[jax.ShapeDtypeStruct((t, d), BF16)] * 2, compiler_params=_params(1),
    )(dx, gl, pp)


def _loss_fwd_bwd(y, target, name):
    t, d = y.shape
    tm = _pick(t, 512, 8)
    inv_d = 1.0 / d

    def body(y_ref, t_ref, dy_ref, l_ref):
        i = pl.program_id(0)
        e = y_ref[...] - t_ref[...]
        dy_ref[...] = e * inv_d
        part = _colsum8(e * e) * (0.5 * inv_d)

        @pl.when(i == 0)
        def _():
            l_ref[...] = part

        @pl.when(i > 0)
        def _():
            l_ref[...] += part

    return pl.pallas_call(
        body, name=name, grid=(t // tm,), in_specs=[_rows(tm, d)] * 2,
        out_specs=[_rows(tm, d), _const((V7X_SUBLANES, d))],
        out_shape=[jax.ShapeDtypeStruct((t, d), F32), jax.ShapeDtypeStruct((V7X_SUBLANES, d), F32)],
        compiler_params=_params(1),
    )(y, target)


def _shift_scratch(ts, cc):
    return pltpu.VMEM((V7X_SUBLANES, ts + HALO - V7X_SUBLANES, cc), F32)


def _shifted_copies(sh_ref, win_ref, cs, ts):
    rows = ts + HALO - V7X_SUBLANES
    for s in range(1, V7X_SUBLANES):
        sh_ref[s] = win_ref[pl.ds(s, rows), cs]


def _tap(sh_ref, win_ref, cs, offset, rows, r0):
    s = offset % V7X_SUBLANES
    start = pl.multiple_of(r0 + (offset - s), V7X_SUBLANES)
    if s == 0:
        return win_ref[pl.ds(start, rows), cs]
    return sh_ref[s, pl.ds(start, rows), :]


def _conv_fwd(proj, conv_w, conv_b, ln_g, ln_b, seq, name, ex=None):
    t, c3 = proj.shape
    c = c3 // 3
    ts = _pick(seq, 256, HALO)
    nsb = seq // ts
    cc = _pick(c, 512, V7X_LANES)
    hb = ts // HALO

    def body(a_ref, b_ref, z_ref, ap_ref, bp_ref, w_ref, cb_ref, g_ref, be_ref, m_ref, y_ref, win_ref, sh_ref):
        i = pl.program_id(0)
        first = (i % nsb) == 0
        win_ref[0:HALO, :] = jnp.where(first, 0.0, ap_ref[...] * _sig(bp_ref[...]))
        win_ref[HALO:, :] = a_ref[...] * _sig(b_ref[...])
        for ci in range(c // cc):
            cs = slice(ci * cc, (ci + 1) * cc)
            _shifted_copies(sh_ref, win_ref, cs, ts)

            def out_rows(rb, carry, cs=cs):
                r0 = rb * CONV_ROWS
                acc = jnp.zeros((CONV_ROWS, cc), F32) + cb_ref[:, cs]
                for k in range(CONV_WIDTH):
                    acc = acc + w_ref[k:k + 1, cs] * _tap(sh_ref, win_ref, cs, HALO - (CONV_WIDTH - 1) + k,
                                                           CONV_ROWS, r0)
                y_ref[pl.ds(pl.multiple_of(r0, CONV_ROWS), CONV_ROWS), cs] = acc
                return carry

            lax.fori_loop(0, ts // CONV_ROWS, out_rows, 0, unroll=2)
        y = y_ref[...]
        mu = jnp.mean(y, axis=-1, keepdims=True)
        xc = y - mu
        rstd = lax.rsqrt(jnp.mean(xc * xc, axis=-1, keepdims=True) + EPS)
        ln = xc * rstd * g_ref[...] + be_ref[...]
        zz = z_ref[...]
        m_ref[...] = (ln * _sig(ln) * zz * _sig(zz)).astype(BF16)

    halo_a = pl.BlockSpec((HALO, c), lambda i: (jnp.maximum(i * hb - 1, 0), 0))
    halo_b = pl.BlockSpec((HALO, c), lambda i: (jnp.maximum(i * hb - 1, 0), 1))
    (m_act, y), moved = _hosted_call(
        body, ex, name, (t // ts,),
        [_rows(ts, c, 0), _rows(ts, c, 1), _rows(ts, c, 2), halo_a, halo_b,
         _const((CONV_WIDTH, c)), _const((1, c)), _const((1, c)), _const((1, c))],
        [_rows(ts, c), _rows(ts, c)],
        [jax.ShapeDtypeStruct((t, c), BF16), jax.ShapeDtypeStruct((t, c), F32)],
        [pltpu.VMEM((HALO + ts, c), F32), _shift_scratch(ts, cc)],
        (proj, proj, proj, proj, proj, conv_w, conv_b, ln_g, ln_b))
    return m_act, y, moved


def _ln_gate_bwd(dm, y, proj, ln_g, ln_b, name):
    t, c = y.shape
    tm = _pick(t, 256, 8)

    def body(dm_ref, y_ref, z_ref, g_ref, be_ref, dy_ref, dz_ref, dg_ref, db_ref, dcb_ref):
        i = pl.program_id(0)
        yv = y_ref[...]
        mu = jnp.mean(yv, axis=-1, keepdims=True)
        xc = yv - mu
        rstd = lax.rsqrt(jnp.mean(xc * xc, axis=-1, keepdims=True) + EPS)
        xhat = xc * rstd
        g = g_ref[...]
        ln = xhat * g + be_ref[...]
        sl = _sig(ln)
        zz = z_ref[...]
        sz = _sig(zz)
        dmv = dm_ref[...]
        dz_ref[...] = (dmv * (ln * sl) * (sz * (1.0 + zz * (1.0 - sz)))).astype(BF16)
        dln = dmv * (zz * sz) * (sl * (1.0 + ln * (1.0 - sl)))
        dxh = dln * g
        dyv = rstd * (dxh - jnp.mean(dxh, axis=-1, keepdims=True)
                      - xhat * jnp.mean(dxh * xhat, axis=-1, keepdims=True))
        dy_ref[...] = dyv
        parts = (_colsum8(dln * xhat), _colsum8(dln), _colsum8(dyv))

        @pl.when(i == 0)
        def _():
            for ref, part in zip((dg_ref, db_ref, dcb_ref), parts):
                ref[...] = part

        @pl.when(i > 0)
        def _():
            for ref, part in zip((dg_ref, db_ref, dcb_ref), parts):
                ref[...] += part

    acc = jax.ShapeDtypeStruct((V7X_SUBLANES, c), F32)
    outs = pl.pallas_call(
        body, name=name, grid=(t // tm,),
        in_specs=[_rows(tm, c), _rows(tm, c), _rows(tm, c, 2), _const((1, c)), _const((1, c))],
        out_specs=[_rows(tm, c), _rows(tm, c)] + [_const((V7X_SUBLANES, c))] * 3,
        out_shape=[jax.ShapeDtypeStruct((t, c), F32), jax.ShapeDtypeStruct((t, c), BF16), acc, acc, acc],
        compiler_params=_params(1),
    )(dm, y, proj, ln_g, ln_b)
    return outs[0], outs[1], outs[2].sum(axis=0), outs[3].sum(axis=0), outs[4].sum(axis=0)


def _conv_bwd(dy, dz, proj, conv_w, seq, name, ex=None):
    t, c3 = proj.shape
    c = c3 // 3
    ts = _pick(seq, 256, HALO)
    nsb = seq // ts
    cc = _pick(c, 512, V7X_LANES)
    hb = ts // HALO
    last_halo = t // HALO - 1
    back = CONV_WIDTH - 1

    def body(dy_ref, dyn_ref, dz_ref, a_ref, b_ref, ap_ref, bp_ref, w_ref, o_ref, dw_ref, win_ref, dwin_ref,
             sh_ref, dsh_ref):
        i = pl.program_id(0)
        first = (i % nsb) == 0
        last = (i % nsb) == nsb - 1
        win_ref[0:HALO, :] = jnp.where(first, 0.0, ap_ref[...] * _sig(bp_ref[...]))
        win_ref[HALO:, :] = a_ref[...] * _sig(b_ref[...])
        dwin_ref[0:ts, :] = dy_ref[...]
        dwin_ref[ts:, :] = jnp.where(last, 0.0, dyn_ref[...])

        @pl.when(i == 0)
        def _():
            dw_ref[...] = jnp.zeros_like(dw_ref)

        for ci in range(c // cc):
            cs = slice(ci * cc, (ci + 1) * cc)
            _shifted_copies(sh_ref, win_ref, cs, ts)
            _shifted_copies(dsh_ref, dwin_ref, cs, ts)

            def in_grad_rows(rb, carry, cs=cs, ci=ci):
                r0 = rb * CONV_ROWS
                rows = pl.ds(pl.multiple_of(r0, CONV_ROWS), CONV_ROWS)
                dglu = jnp.zeros((CONV_ROWS, cc), F32)
                for k in range(CONV_WIDTH):
                    dglu = dglu + w_ref[k:k + 1, cs] * _tap(dsh_ref, dwin_ref, cs, back - k, CONV_ROWS, r0)
                sbc = _sig(b_ref[rows, cs])
                o_ref[rows, cs] = (dglu * sbc).astype(BF16)
                o_ref[rows, c + ci * cc:c + (ci + 1) * cc] = (dglu * a_ref[rows, cs] * sbc * (1.0 - sbc)).astype(BF16)
                return carry

            def w_grad_rows(rb, carry, cs=cs):
                r0 = rb * CONV_W_ROWS
                dcur = dwin_ref[pl.ds(pl.multiple_of(r0, CONV_W_ROWS), CONV_W_ROWS), cs]
                for k in range(CONV_WIDTH):
                    dw_ref[k * V7X_SUBLANES:(k + 1) * V7X_SUBLANES, cs] += _colsum8(
                        dcur * _tap(sh_ref, win_ref, cs, HALO - back + k, CONV_W_ROWS, r0))
                return carry

            lax.fori_loop(0, ts // CONV_ROWS, in_grad_rows, 0, unroll=2)
            lax.fori_loop(0, ts // CONV_W_ROWS, w_grad_rows, 0)
        o_ref[:, 2 * c:] = dz_ref[...]

    halo_next = pl.BlockSpec((HALO, c), lambda i: (jnp.minimum((i + 1) * hb, last_halo), 0))
    halo_a = pl.BlockSpec((HALO, c), lambda i: (jnp.maximum(i * hb - 1, 0), 0))
    halo_b = pl.BlockSpec((HALO, c), lambda i: (jnp.maximum(i * hb - 1, 0), 1))
    (dproj, dw), moved = _hosted_call(
        body, ex, name, (t // ts,),
        [_rows(ts, c), halo_next, _rows(ts, c), _rows(ts, c, 0), _rows(ts, c, 1), halo_a, halo_b,
         _const((CONV_WIDTH, c))],
        [_rows(ts, c3), _const((CONV_WIDTH * V7X_SUBLANES, c))],
        [jax.ShapeDtypeStruct((t, c3), BF16), jax.ShapeDtypeStruct((CONV_WIDTH * V7X_SUBLANES, c), F32)],
        [pltpu.VMEM((HALO + ts, c), F32), pltpu.VMEM((ts + HALO, c), F32),
         _shift_scratch(ts, cc), _shift_scratch(ts, cc)],
        (dy, dy, dz, proj, proj, proj, proj, conv_w))
    return dproj, dw.reshape(CONV_WIDTH, V7X_SUBLANES, c).sum(axis=1), moved


def _rope_tables(seq):
    half = ROPE_DIM // 2
    inv = ROPE_THETA ** (-jnp.arange(half, dtype=F32) * (2.0 / ROPE_DIM))
    ang = jnp.arange(seq).astype(F32)[:, None] * inv[None, :]
    cos, sin = jnp.cos(ang), jnp.sin(ang)
    zeros = jnp.zeros((seq, HEAD_DIM - ROPE_DIM), F32)
    zh = jnp.zeros((seq, half), F32)
    a = jnp.concatenate([cos, cos, zeros + 1.0], axis=1)
    b = jnp.concatenate([zh, sin, zeros], axis=1)
    c = jnp.concatenate([-sin, zh, zeros], axis=1)
    rep = V7X_LANES // HEAD_DIM
    return tuple(jnp.tile(v, (1, rep)) for v in (a, b, c))


def _head_ones(d):
    head = jnp.arange(d) // HEAD_DIM
    return (head[:, None] == head[None, :]).astype(BF16)


def _rope(ch, ta, tb, tc):
    return ta * ch + tb * pltpu.roll(ch, ROPE_DIM // 2, 1) + tc * pltpu.roll(ch, V7X_LANES - ROPE_DIM // 2, 1)


def _rope_t(ch, ta, tb, tc):
    return ta * ch + pltpu.roll(tb * ch, V7X_LANES - ROPE_DIM // 2, 1) + pltpu.roll(tc * ch, ROPE_DIM // 2, 1)


def _norm_rope_bwd(xhat, r, dout, gain, ta, tb, tc, e_ref):
    dxn = _rope_t(dout, ta, tb, tc)
    dxh = dxn * gain
    dx = r * (dxh - xhat * (_segsum(dxh * xhat, e_ref) * (1.0 / HEAD_DIM)))
    return dx, _colsum8(dxn * xhat)


def _norm_rope_rows(dst_ref, src_ref, gain, ta_ref, tb_ref, tc_ref, e_ref, seq, xhat_ref=None, r_ref=None):
    for r0 in range(0, seq, ATTN_PIECE):
        rows = slice(r0, r0 + ATTN_PIECE)
        xv = src_ref[rows, :]
        r = lax.rsqrt(_segsum(xv * xv, e_ref) * (1.0 / HEAD_DIM) + EPS)
        xhat = xv * r
        if xhat_ref is not None:
            xhat_ref[rows, :] = xhat
            r_ref[rows, :] = r
        dst_ref[rows, :] = _rope(xhat * gain, ta_ref[rows, :], tb_ref[rows, :], tc_ref[rows, :])


ATTN_PIECE = 256
ATTN_UNROLL = 4


def _pieces(dil, seq):
    length = seq // dil
    rows = min(length, ATTN_PIECE)
    return [(r + dil * ci * rows, r * length + ci * rows, rows) for r in range(dil) for ci in range(length // rows)]


def _strided(ref, start, rows, dil):
    if dil == 1:
        return ref[pl.ds(start, rows), :]
    return ref[pl.ds(start, rows, stride=dil), :]


def _strided_set(ref, start, rows, dil, val):
    if dil == 1:
        ref[pl.ds(start, rows), :] = val
    else:
        ref[pl.ds(start, rows, stride=dil), :] = val


def _nt(a, b):
    return lax.dot_general(a, b, (((1,), (1,)), ((), ())), preferred_element_type=F32)


def _tn(a, b):
    return lax.dot_general(a, b, (((0,), (0,)), ((), ())), preferred_element_type=F32)


def _set_bias(bias_ref):
    qi = lax.broadcasted_iota(jnp.int32, (2 * SPAN, 2 * SPAN), 0) & (SPAN - 1)
    kj = lax.broadcasted_iota(jnp.int32, (2 * SPAN, 2 * SPAN), 1)
    band = jnp.logical_and(kj >= qi, (kj - SPAN) <= qi)
    bias_ref[1] = jnp.where(band, 0.0, NEG_INF)
    bias_ref[0] = jnp.where(jnp.logical_and(band, kj >= SPAN), 0.0, NEG_INF)


def _stack_heads(v, head0):
    zero = jnp.zeros_like(v)
    return jnp.concatenate([jnp.where(head0, v, zero), jnp.where(head0, zero, v)], axis=0)


def _unstack_heads(v2, head0):
    return jnp.where(head0, v2[:SPAN], v2[SPAN:])


def _head_cols(v):
    return jnp.concatenate([v[:, 0:1], v[:, HEAD_DIM:HEAD_DIM + 1]], axis=0)


def _attn_fwd(proj_b, kv, gains, tables, ones, bsz, seq, name):
    t, d4 = proj_b.shape
    d = d4 // 4
    nhp = d // V7X_LANES
    nblk = seq // SPAN
    scale = HEAD_DIM ** -0.5
    n_groups = len(DILATIONS)

    def body(q0_ref, q1_ref, q2_ref, k_ref, v_ref, gate_ref, gain_ref, ta_ref, tb_ref, tc_ref, e_ref,
             o_ref, l_ref, ao_ref, qd, kd, vd, od, ld, on0, on1, on2, ln0, ln1, ln2, kn, qn, bias):
        head0 = lax.broadcasted_iota(jnp.int32, (SPAN, V7X_LANES), 1) < HEAD_DIM

        @pl.when(jnp.logical_and(pl.program_id(0) == 0, pl.program_id(1) == 0))
        def _():
            _set_bias(bias)

        _norm_rope_rows(kn, k_ref, gain_ref[n_groups:n_groups + 1, :], ta_ref, tb_ref, tc_ref, e_ref, seq)
        kd[0:SPAN, :] = jnp.zeros((SPAN, V7X_LANES), BF16)
        vd[0:SPAN, :] = jnp.zeros((SPAN, V7X_LANES), BF16)
        for g, (q_ref, on, ln) in enumerate(((q0_ref, on0, ln0), (q1_ref, on1, ln1), (q2_ref, on2, ln2))):
            dil = DILATIONS[g]
            nb = seq // dil // SPAN
            _norm_rope_rows(qn, q_ref, gain_ref[g:g + 1, :], ta_ref, tb_ref, tc_ref, e_ref, seq)
            for ns, rs, rows in _pieces(dil, seq):
                qd[rs:rs + rows, :] = _strided(qn, ns, rows, dil).astype(BF16)
                kd[SPAN + rs:SPAN + rs + rows, :] = _strided(kn, ns, rows, dil).astype(BF16)
                vd[SPAN + rs:SPAN + rs + rows, :] = _strided(v_ref, ns, rows, dil).astype(BF16)

            def block(j, carry):
                qs = pl.multiple_of(j * SPAN, SPAN)
                q2 = _stack_heads(qd[pl.ds(qs, SPAN), :], head0)
                kk = kd[pl.ds(qs, 2 * SPAN), :]
                vv = vd[pl.ds(qs, 2 * SPAN), :]
                s = _nt(q2, kk) * scale + bias[jnp.minimum(j & (nb - 1), 1)]
                mx = jnp.max(s, axis=1, keepdims=True)
                p = jnp.exp(s - mx)
                den = jnp.sum(p, axis=1, keepdims=True)
                o2 = jnp.dot(p.astype(BF16), vv, preferred_element_type=F32) / den
                l2 = jnp.broadcast_to(mx + jnp.log(den), (2 * SPAN, V7X_LANES))
                od[pl.ds(qs, SPAN), :] = _unstack_heads(o2, head0)
                ld[pl.ds(qs, SPAN), :] = _unstack_heads(l2, head0)
                return carry

            lax.fori_loop(0, nblk, block, 0, unroll=ATTN_UNROLL)
            for ns, rs, rows in _pieces(dil, seq):
                _strided_set(on, ns, rows, dil, od[rs:rs + rows, :])
                _strided_set(ln, ns, rows, dil, ld[rs:rs + rows, :])

        def merge(ci, carry):
            rows = pl.ds(pl.multiple_of(ci * ATTN_PIECE, ATTN_PIECE), ATTN_PIECE)
            ls = [ln0[rows, :], ln1[rows, :], ln2[rows, :]]
            mx = jnp.maximum(jnp.maximum(ls[0], ls[1]), ls[2])
            es = [jnp.exp(v - mx) for v in ls]
            den = es[0] + es[1] + es[2]
            ov = (es[0] * on0[rows, :] + es[1] * on1[rows, :] + es[2] * on2[rows, :]) / den
            gate = gate_ref[rows, :]
            o_ref[rows, :] = ov
            l_ref[rows, :] = mx + jnp.log(den)
            ao_ref[rows, :] = (ov * gate * _sig(gate)).astype(BF16)
            return carry

        lax.fori_loop(0, seq // ATTN_PIECE, merge, 0)

    blk = (None, seq, V7X_LANES)
    pview = proj_b.reshape(bsz, seq, d4)
    kview = kv.reshape(bsz, seq, 2 * d)
    out_spec = pl.BlockSpec(blk, lambda b, h: (b, 0, h))
    tab = pl.BlockSpec((seq, V7X_LANES), lambda b, h: (0, 0))
    nat = pltpu.VMEM((seq, V7X_LANES), F32)
    o, lse, ao = pl.pallas_call(
        body, name=name, grid=(bsz, nhp),
        in_specs=[pl.BlockSpec(blk, lambda b, h: (b, 0, h)),
                  pl.BlockSpec(blk, lambda b, h: (b, 0, nhp + h)),
                  pl.BlockSpec(blk, lambda b, h: (b, 0, 2 * nhp + h)),
                  pl.BlockSpec(blk, lambda b, h: (b, 0, h)),
                  pl.BlockSpec(blk, lambda b, h: (b, 0, nhp + h)),
                  pl.BlockSpec(blk, lambda b, h: (b, 0, 3 * nhp + h)),
                  pl.BlockSpec((n_groups + 1, V7X_LANES), lambda b, h: (0, 0)),
                  tab, tab, tab,
                  pl.BlockSpec((V7X_LANES, V7X_LANES), lambda b, h: (0, 0))],
        out_specs=[out_spec, out_spec, out_spec],
        out_shape=[jax.ShapeDtypeStruct((bsz, seq, d), F32), jax.ShapeDtypeStruct((bsz, seq, d), F32),
                   jax.ShapeDtypeStruct((bsz, seq, d), BF16)],
        scratch_shapes=[pltpu.VMEM((seq, V7X_LANES), BF16), pltpu.VMEM((SPAN + seq, V7X_LANES), BF16),
                        pltpu.VMEM((SPAN + seq, V7X_LANES), BF16), nat, nat, nat, nat, nat, nat, nat, nat, nat, nat,
                        pltpu.VMEM((2, 2 * SPAN, 2 * SPAN), F32)],
        compiler_params=_params(2),
    )(pview, pview, pview, kview, kview, pview, gains, *tables, ones)
    return o.reshape(t, d), lse.reshape(t, d), ao.reshape(t, d)


def _attn_bwd(proj_b, kv, dao, o, lse, gains, tables, ones, bsz, seq, name):
    t, d4 = proj_b.shape
    d = d4 // 4
    nhp = d // V7X_LANES
    nblk = seq // SPAN
    scale = HEAD_DIM ** -0.5
    n_groups = len(DILATIONS)
    n_chunks = seq // ATTN_PIECE

    def body(q_ref, k_ref, v_ref, gate_ref, dao_ref, o_ref, l_ref, gain_ref, ta_ref, tb_ref, tc_ref, e_ref,
             dproj_ref, dkv_ref, dg_ref, qd, kd, vd, dod, ld, deld, dqd, dkd, dvd, dqn, dkn, dvn, kn, kxh, krr,
             qn, qxh, qrr, don, deln, bias):
        head0 = lax.broadcasted_iota(jnp.int32, (SPAN, V7X_LANES), 1) < HEAD_DIM
        g = pl.program_id(2)

        @pl.when(jnp.logical_and(jnp.logical_and(pl.program_id(0) == 0, pl.program_id(1) == 0), g == 0))
        def _():
            _set_bias(bias)
            dg_ref[...] = jnp.zeros_like(dg_ref)

        @pl.when(g == 0)
        def _():
            dkn[...] = jnp.zeros_like(dkn)
            dvn[...] = jnp.zeros_like(dvn)
            _norm_rope_rows(kn, k_ref, gain_ref[n_groups:n_groups + 1, :], ta_ref, tb_ref, tc_ref, e_ref, seq,
                            kxh, krr)
            for r0 in range(0, seq, ATTN_PIECE):
                rows = slice(r0, r0 + ATTN_PIECE)
                gate = gate_ref[rows, :]
                dov = dao_ref[rows, :] * gate * _sig(gate)
                don[rows, :] = dov
                deln[rows, :] = _segsum(dov * o_ref[rows, :], e_ref)

        def norm_bwd_chunks(xhat_ref, r_ref, dn_ref, out_ref, gi):
            def chunk(ci, carry):
                rows = pl.ds(pl.multiple_of(ci * ATTN_PIECE, ATTN_PIECE), ATTN_PIECE)
                dx, part = _norm_rope_bwd(xhat_ref[rows, :], r_ref[rows, :], dn_ref[rows, :], gain_ref[gi:gi + 1, :],
                                          ta_ref[rows, :], tb_ref[rows, :], tc_ref[rows, :], e_ref)
                out_ref[rows, :] = dx.astype(BF16)
                dg_ref[gi] += part
                return carry
            lax.fori_loop(0, n_chunks, chunk, 0, unroll=ATTN_UNROLL)

        def group(gi):
            dil = DILATIONS[gi]
            nb = seq // dil // SPAN
            kd[0:SPAN, :] = jnp.zeros((SPAN, V7X_LANES), BF16)
            vd[0:SPAN, :] = jnp.zeros((SPAN, V7X_LANES), BF16)
            dkd[...] = jnp.zeros_like(dkd)
            dvd[...] = jnp.zeros_like(dvd)
            _norm_rope_rows(qn, q_ref, gain_ref[gi:gi + 1, :], ta_ref, tb_ref, tc_ref, e_ref, seq, qxh, qrr)
            for ns, rs, rows in _pieces(dil, seq):
                qd[rs:rs + rows, :] = _strided(qn, ns, rows, dil).astype(BF16)
                kd[SPAN + rs:SPAN + rs + rows, :] = _strided(kn, ns, rows, dil).astype(BF16)
                vd[SPAN + rs:SPAN + rs + rows, :] = _strided(v_ref, ns, rows, dil).astype(BF16)
                dod[rs:rs + rows, :] = _strided(don, ns, rows, dil).astype(BF16)
                deld[rs:rs + rows, :] = _strided(deln, ns, rows, dil)
                ld[rs:rs + rows, :] = _strided(l_ref, ns, rows, dil)

            def block(j, carry):
                qs = pl.multiple_of(j * SPAN, SPAN)
                q2 = _stack_heads(qd[pl.ds(qs, SPAN), :], head0)
                do2 = _stack_heads(dod[pl.ds(qs, SPAN), :], head0)
                kk = kd[pl.ds(qs, 2 * SPAN), :]
                vv = vd[pl.ds(qs, 2 * SPAN), :]
                s = _nt(q2, kk) * scale + bias[jnp.minimum(j & (nb - 1), 1)]
                p = jnp.exp(s - _head_cols(ld[pl.ds(qs, SPAN), :]))
                ds = (p * (_nt(do2, vv) - _head_cols(deld[pl.ds(qs, SPAN), :])) * scale).astype(BF16)
                dqd[pl.ds(qs, SPAN), :] = _unstack_heads(jnp.dot(ds, kk, preferred_element_type=F32), head0)
                dkd[pl.ds(qs, 2 * SPAN), :] += _tn(ds, q2)
                dvd[pl.ds(qs, 2 * SPAN), :] += _tn(p.astype(BF16), do2)
                return carry

            lax.fori_loop(0, nblk, block, 0, unroll=ATTN_UNROLL)
            for ns, rs, rows in _pieces(dil, seq):
                _strided_set(dqn, ns, rows, dil, dqd[rs:rs + rows, :])
                _strided_set(dkn, ns, rows, dil,
                             _strided(dkn, ns, rows, dil) + dkd[SPAN + rs:SPAN + rs + rows, :])
                _strided_set(dvn, ns, rows, dil,
                             _strided(dvn, ns, rows, dil) + dvd[SPAN + rs:SPAN + rs + rows, :])
            norm_bwd_chunks(qxh, qrr, dqn, dproj_ref, gi)

        for gi in range(n_groups):
            @pl.when(g == gi)
            def _():
                group(gi)

        @pl.when(g == n_groups - 1)
        def _():
            norm_bwd_chunks(kxh, krr, dkn, dkv_ref, n_groups)

        @pl.when(g == n_groups)
        def _():
            def chunk(ci, carry):
                rows = pl.ds(pl.multiple_of(ci * ATTN_PIECE, ATTN_PIECE), ATTN_PIECE)
                gate = gate_ref[rows, :]
                sg = _sig(gate)
                dproj_ref[rows, :] = (dao_ref[rows, :] * o_ref[rows, :]
                                      * (sg * (1.0 + gate * (1.0 - sg)))).astype(BF16)
                dkv_ref[rows, :] = dvn[rows, :].astype(BF16)
                return carry
            lax.fori_loop(0, n_chunks, chunk, 0, unroll=ATTN_UNROLL)

    blk = (None, seq, V7X_LANES)
    pview = proj_b.reshape(bsz, seq, d4)
    kview = kv.reshape(bsz, seq, 2 * d)
    dview = (bsz, seq, d)
    d_spec = pl.BlockSpec(blk, lambda b, h, g: (b, 0, h))
    tab = pl.BlockSpec((seq, V7X_LANES), lambda b, h, g: (0, 0))
    nat = pltpu.VMEM((seq, V7X_LANES), F32)
    natb = pltpu.VMEM((seq, V7X_LANES), BF16)
    pad = pltpu.VMEM((SPAN + seq, V7X_LANES), F32)
    padb = pltpu.VMEM((SPAN + seq, V7X_LANES), BF16)
    dproj, dkv, dg = pl.pallas_call(
        body, name=name, grid=(bsz, nhp, n_groups + 1),
        in_specs=[pl.BlockSpec(blk, lambda b, h, g: (b, 0, jnp.minimum(g, n_groups - 1) * nhp + h)),
                  pl.BlockSpec(blk, lambda b, h, g: (b, 0, h)),
                  pl.BlockSpec(blk, lambda b, h, g: (b, 0, nhp + h)),
                  pl.BlockSpec(blk, lambda b, h, g: (b, 0, n_groups * nhp + h)),
                  d_spec, d_spec, d_spec,
                  pl.BlockSpec((n_groups + 1, V7X_LANES), lambda b, h, g: (0, 0)),
                  tab, tab, tab,
                  pl.BlockSpec((V7X_LANES, V7X_LANES), lambda b, h, g: (0, 0))],
        out_specs=[pl.BlockSpec(blk, lambda b, h, g: (b, 0, g * nhp + h)),
                   pl.BlockSpec(blk, lambda b, h, g: (b, 0, (g // n_groups) * nhp + h)),
                   pl.BlockSpec((n_groups + 1, V7X_SUBLANES, V7X_LANES), lambda b, h, g: (0, 0, 0))],
        out_shape=[jax.ShapeDtypeStruct((bsz, seq, d4), BF16), jax.ShapeDtypeStruct((bsz, seq, 2 * d), BF16),
                   jax.ShapeDtypeStruct((n_groups + 1, V7X_SUBLANES, V7X_LANES), F32)],
        scratch_shapes=[natb, padb, padb, natb, nat, nat, nat, pad, pad, nat, nat, nat, nat, nat, nat,
                        nat, nat, nat, nat, nat,
                        pltpu.VMEM((2, 2 * SPAN, 2 * SPAN), F32)],
        compiler_params=_params(3),
    )(pview, kview, kview, pview, dao.reshape(dview), o.reshape(dview), lse.reshape(dview), gains, *tables, ones)
    dgain = dg.sum(axis=1).reshape(n_groups + 1, V7X_LANES // HEAD_DIM, HEAD_DIM).sum(axis=1)
    return dproj.reshape(t, d4), dkv.reshape(t, 2 * d), dgain


def _mesh_position():
    x, y, c = lax.axis_index("x"), lax.axis_index("y"), lax.axis_index("c")
    return x, y, c


def _peer(x, y, c, rel):
    return (1 - x if rel & 4 else x, 1 - y if rel & 2 else y, 1 - c if rel & 1 else c)


class _Exchange:
    def __init__(self, srcs, gather):
        self.srcs = list(srcs)
        self.gather = gather
        n = self.n = len(self.srcs)
        hbm = pl.BlockSpec(memory_space=pltpu.HBM)
        self.in_specs = [hbm] * n
        self.out_specs = [hbm] * n
        self.out_shape = [jax.ShapeDtypeStruct(((N_DEV,) + a.shape) if gather else a.shape, a.dtype)
                          for a in self.srcs]
        self.scratch = [pltpu.SemaphoreType.DMA((n * (N_DEV - 1),)), pltpu.SemaphoreType.DMA((n * (N_DEV - 1),)),
                        pltpu.SemaphoreType.DMA((n,))]

    def _copies(self, ins, outs, sems):
        send_sems, recv_sems, local_sems = sems
        x, y, c = _mesh_position()
        me = 4 * x + 2 * y + c
        remote, local = [], []
        for a in range(self.n):
            mine = ins[a] if self.gather else ins[a].at[me]
            local.append(pltpu.make_async_copy(mine, outs[a].at[me], local_sems.at[a]))
            for rel in range(1, N_DEV):
                px, py, pc = _peer(x, y, c, rel)
                s = a * (N_DEV - 1) + rel - 1
                src = ins[a] if self.gather else ins[a].at[4 * px + 2 * py + pc]
                remote.append(pltpu.make_async_remote_copy(
                    src_ref=src, dst_ref=outs[a].at[me], send_sem=send_sems.at[s], recv_sem=recv_sems.at[s],
                    device_id=(px, py, pc), device_id_type=pl.DeviceIdType.MESH))
        return remote, local

    def start(self, ins, outs, sems):
        remote, local = self._copies(ins, outs, sems)
        for cp in local + remote:
            cp.start()

    def wait(self, ins, outs, sems):
        remote, local = self._copies(ins, outs, sems)
        for cp in remote:
            cp.wait_recv()
        for cp in remote:
            cp.wait_send()
        for cp in local:
            cp.wait()


def _run_exchange(ex, name):
    n = ex.n

    def body(*refs):
        ins, outs, sems = refs[:n], refs[n:2 * n], refs[2 * n:]
        ex.start(ins, outs, sems)
        ex.wait(ins, outs, sems)

    return pl.pallas_call(body, name=name, in_specs=ex.in_specs, out_specs=ex.out_specs, out_shape=ex.out_shape,
                          scratch_shapes=ex.scratch)(*ex.srcs)


def _hosted_call(body, ex, name, grid, in_specs, out_specs, out_shape, scratch_shapes, args):
    if ex is None:
        outs = pl.pallas_call(body, name=name, grid=grid, in_specs=in_specs, out_specs=out_specs, out_shape=out_shape,
                              scratch_shapes=scratch_shapes, compiler_params=_params(len(grid)))(*args)
        return list(outs), []
    n_in, n_out, n_scr, n = len(in_specs), len(out_specs), len(scratch_shapes), ex.n

    def hosted(*refs):
        h_in, e_in = refs[:n_in], refs[n_in:n_in + n]
        o0 = n_in + n
        h_out, e_out = refs[o0:o0 + n_out], refs[o0 + n_out:o0 + n_out + n]
        s0 = o0 + n_out + n
        h_scr, e_scr = refs[s0:s0 + n_scr], refs[s0 + n_scr:]
        ids = [pl.program_id(a) for a in range(len(grid))]
        first = functools.reduce(jnp.logical_and, [i == 0 for i in ids])
        last = functools.reduce(jnp.logical_and, [i == g - 1 for i, g in zip(ids, grid)])

        @pl.when(first)
        def _():
            ex.start(e_in, e_out, e_scr)

        body(*h_in, *h_out, *h_scr)

        @pl.when(last)
        def _():
            ex.wait(e_in, e_out, e_scr)

    outs = pl.pallas_call(
        hosted, name=name, grid=grid, in_specs=list(in_specs) + ex.in_specs,
        out_specs=list(out_specs) + ex.out_specs, out_shape=list(out_shape) + ex.out_shape,
        scratch_shapes=list(scratch_shapes) + ex.scratch, compiler_params=_params(len(grid)),
    )(*args, *ex.srcs)
    return list(outs[:n_out]), list(outs[n_out:])


def _sum_adamw(parts, w, m, v, name):
    _, r, wd = parts.shape
    tr = _pick(r, ADAM_ROWS, 8)
    c1 = 1.0 - ADAM_B1 ** ADAM_STEP
    c2 = 1.0 - ADAM_B2 ** ADAM_STEP

    def body(p_ref, w_ref, m_ref, v_ref, g_ref, d_ref, nm_ref, nv_ref):
        g = p_ref[0].astype(F32)
        for s in range(1, N_DEV):
            g = g + p_ref[s].astype(F32)
        nm = ADAM_B1 * m_ref[...] + (1.0 - ADAM_B1) * g
        nv = ADAM_B2 * v_ref[...] + (1.0 - ADAM_B2) * (g * g)
        g_ref[...] = g
        nm_ref[...] = nm
        nv_ref[...] = nv
        d_ref[...] = -ADAM_LR * ((nm / c1) / (jnp.sqrt(nv / c2) + ADAM_EPS) + ADAM_WD * w_ref[...])

    row = pl.BlockSpec((tr, wd), lambda i: (i, 0))
    return pl.pallas_call(
        body, name=name, grid=(r // tr,),
        in_specs=[pl.BlockSpec((N_DEV, tr, wd), lambda i: (0, i, 0)), row, row, row],
        out_specs=[row] * 4, out_shape=[jax.ShapeDtypeStruct((r, wd), F32)] * 4,
        compiler_params=_params(1),
    )(parts, w, m, v)


def _pack_rows(size, row_mult):
    rows = -(-size // PACK_LANES)
    return -(-rows // row_mult) * row_mult


def _pack(flats, row_mult, dtype, total_mult=None):
    out = []
    for f in flats:
        size = f.shape[-1]
        rows = _pack_rows(size, row_mult)
        pad = [(0, 0)] * (f.ndim - 1) + [(0, rows * PACK_LANES - size)]
        out.append(jnp.pad(f.astype(dtype), pad).reshape(f.shape[:-1] + (rows, PACK_LANES)))
    if total_mult is not None:
        total = sum(o.shape[-2] for o in out)
        extra = -(-total // total_mult) * total_mult - total
        if extra:
            out.append(jnp.zeros(out[0].shape[:-2] + (extra, PACK_LANES), dtype))
    return jnp.concatenate(out, axis=-2)


def _unpack(buf, sizes, row_mult):
    out, row = [], 0
    for size in sizes:
        rows = _pack_rows(size, row_mult)
        part = buf[..., row:row + rows, :]
        out.append(part.reshape(buf.shape[:-2] + (rows * PACK_LANES,))[..., :size])
        row += rows
    return out


def _to_slots(full, axis):
    if axis is None:
        return jnp.broadcast_to(full.reshape(1, -1), (N_DEV, full.size))
    shape = full.shape
    split = full.reshape(shape[:axis] + (N_DEV, shape[axis] // N_DEV) + shape[axis + 1:])
    return jnp.moveaxis(split, axis, 0).reshape(N_DEV, -1)


def _from_slots(slots, axis, block_shape):
    split = jnp.moveaxis(slots, 0, axis)
    shape = list(block_shape)
    shape[axis] *= N_DEV
    return split.reshape(shape)


def kernel(x, p, norm_g, w_in_a, conv_w, conv_b, ln_g, ln_b, w_out_a, kv_norm_g, w_kv, k_norm_g, w_in_b, q_norm_g, w_out_b, ple_norm_g, w_ple_gate, w_ple_proj, loss_target, m_norm_g, m_w_in_a, m_conv_w, m_conv_b, m_ln_g, m_ln_b, m_w_out_a, m_kv_norm_g, m_w_kv, m_k_norm_g, m_w_in_b, m_q_norm_g, m_w_out_b, m_ple_norm_g, m_w_ple_gate, m_w_ple_proj, v_norm_g, v_w_in_a, v_conv_w, v_conv_b, v_ln_g, v_ln_b, v_w_out_a, v_kv_norm_g, v_w_kv, v_k_norm_g, v_w_in_b, v_q_norm_g, v_w_out_b, v_ple_norm_g, v_w_ple_gate, v_w_ple_proj):
    weights = dict(zip(WEIGHT_NAMES, (norm_g, w_in_a, conv_w, conv_b, ln_g, ln_b, w_out_a, kv_norm_g, w_kv, k_norm_g,
                                      w_in_b, q_norm_g, w_out_b, ple_norm_g, w_ple_gate, w_ple_proj)))
    mom_m = dict(zip(WEIGHT_NAMES, (m_norm_g, m_w_in_a, m_conv_w, m_conv_b, m_ln_g, m_ln_b, m_w_out_a, m_kv_norm_g,
                                    m_w_kv, m_k_norm_g, m_w_in_b, m_q_norm_g, m_w_out_b, m_ple_norm_g, m_w_ple_gate,
                                    m_w_ple_proj)))
    mom_v = dict(zip(WEIGHT_NAMES, (v_norm_g, v_w_in_a, v_conv_w, v_conv_b, v_ln_g, v_ln_b, v_w_out_a, v_kv_norm_g,
                                    v_w_kv, v_k_norm_g, v_w_in_b, v_q_norm_g, v_w_out_b, v_ple_norm_g, v_w_ple_gate,
                                    v_w_ple_proj)))
    bsz, seq, d = x.shape
    t = bsz * seq
    assert seq % (max(DILATIONS) * SPAN) == 0 and d % V7X_LANES == 0

    full = {}

    def rows2d(a):
        return a.reshape(-1, a.shape[-1])

    def packed(source, names, dtype, total_mult=None):
        return _pack([source[n].reshape(-1) for n in names], 16, dtype, total_mult)

    def gathered(names, bufs):
        for n, buf in zip(names, bufs):
            full[n] = _from_slots(buf.reshape((N_DEV,) + weights[n].shape), SHARD_AXIS[n], weights[n].shape)

    w1_all, wv_all = _run_exchange(_Exchange([rows2d(weights['w_in_a']).astype(BF16),
                                              _pack([weights[n].reshape(-1) for n in VECTOR_WEIGHTS], 8, F32)],
                                             gather=True), "gather_first")
    gathered(GROUP_FIRST, [w1_all])
    for n, slots in zip(VECTOR_WEIGHTS, _unpack(wv_all, [weights[n].size for n in VECTOR_WEIGHTS], 8)):
        full[n] = _from_slots(slots.reshape((N_DEV,) + weights[n].shape), SHARD_AXIS[n], weights[n].shape)
    gather_rest = _Exchange([rows2d(weights[n]).astype(BF16) for n in GROUP_REST], gather=True)
    wa_in = full['w_in_a'][0]
    cw, cb, lg, lb = full['conv_w'][0], full['conv_b'], full['ln_g'], full['ln_b']

    tables = _rope_tables(seq)
    ones = _head_ones(V7X_LANES)
    rep = V7X_LANES // HEAD_DIM
    head_gain = jnp.concatenate([jnp.tile(q_norm_g[0], (1, rep)), jnp.tile(k_norm_g, rep)[None]], axis=0)

    x0 = x.reshape(t, d)
    p0, p1 = p[0].reshape(t, -1), p[1].reshape(t, -1)
    target = loss_target.reshape(t, d)
    g_norm0, g_norm1 = norm_g[0:1], norm_g[1:2]
    g_ple0, g_ple1 = ple_norm_g[0:1], ple_norm_g[1:2]
    g_kv = kv_norm_g.reshape(1, d)

    (u0,) = _rmsnorm_fwd(x0, [g_norm0], "norm0")
    proj_a = _matmul(u0, wa_in, 'nn', "in_a")
    m_act, y_conv, w2_all = _conv_fwd(proj_a, cw, cb, lg, lb, seq, "conv_fwd", ex=gather_rest)
    gathered(GROUP_REST, w2_all)
    wa_out = full['w_out_a'][0]
    wkv = full['w_kv']
    wb_in, wb_out = full['w_in_b'][0], full['w_out_b'][0]
    wg, wp = full['w_ple_gate'], full['w_ple_proj']
    h0 = _matmul(m_act, wa_out, 'nn', "out_a", add=x0)
    (pg0,) = _rmsnorm_fwd(h0, [g_ple0], "ple_norm0")
    gl0 = _matmul(pg0, wg[0], 'nn', "ple_gate0")
    pp0 = _matmul(p0, wp[0], 'nn', "ple_proj0")
    x1 = _ple_fwd(h0, gl0, pp0, "ple0")

    kvn, u1 = _rmsnorm_fwd(x1, [g_kv, g_norm1], "norm1")
    kv = _matmul(kvn, wkv, 'nn', "kv")
    proj_b = _matmul(u1, wb_in, 'nn', "in_b")
    o_att, lse, ao = _attn_fwd(proj_b, kv, head_gain, tables, ones, bsz, seq, "attn_fwd")
    h1 = _matmul(ao, wb_out, 'nn', "out_b", add=x1)
    (pg1,) = _rmsnorm_fwd(h1, [g_ple1], "ple_norm1")
    gl1 = _matmul(pg1, wg[1], 'nn', "ple_gate1")
    pp1 = _matmul(p1, wp[1], 'nn', "ple_proj1")
    x2 = _ple_fwd(h1, gl1, pp1, "ple1")

    dx2, loss_part = _loss_fwd_bwd(x2, target, "loss")
    loss = lax.psum(jnp.sum(loss_part), ("x", "y", "c"))

    grads = {}
    slot = {}

    dgl1, dpp1 = _ple_bwd(dx2, gl1, pp1, "ple1_bwd")
    dwp1 = _matmul(p1, dpp1, 'tn', "d_ple_proj1", out_dtype=BF16, slot_cols=d // N_DEV)
    dwg1 = _matmul(pg1, dgl1, 'tn', "d_ple_gate1", out_dtype=BF16)
    dpg1 = _matmul(dgl1, wg[1], 'nt', "d_ple_norm1")
    dh1, (dg_ple1,) = _rmsnorm_bwd(h1, [g_ple1], [dpg1], dx2, "ple_norm1_bwd")
    slot['w_out_b'] = _matmul(ao, dh1, 'tn', "d_out_b", out_dtype=BF16).reshape(N_DEV, -1, d)
    dao = _matmul(dh1, wb_out, 'nt', "d_ao")
    dproj_b, dkv, dg_head = _attn_bwd(proj_b, kv, dao, o_att, lse, head_gain, tables, ones, bsz, seq, "attn_bwd")
    slot['w_in_b'] = _matmul(u1, dproj_b, 'tn', "d_in_b", out_dtype=BF16, slot_cols=4 * d // N_DEV)
    du1 = _matmul(dproj_b, wb_in, 'nt', "d_u1")
    slot['w_kv'] = _matmul(kvn, dkv, 'tn', "d_kv", out_dtype=BF16, slot_cols=2 * d // N_DEV)
    dkvn = _matmul(dkv, wkv, 'nt', "d_kvn")
    dx1, (dg_kv, dg_norm1) = _rmsnorm_bwd(x1, [g_kv, g_norm1], [dkvn, du1], dh1, "norm1_bwd")

    dgl0, dpp0 = _ple_bwd(dx1, gl0, pp0, "ple0_bwd")
    dwp0 = _matmul(p0, dpp0, 'tn', "d_ple_proj0", out_dtype=BF16, slot_cols=d // N_DEV)
    dwg0 = _matmul(pg0, dgl0, 'tn', "d_ple_gate0", out_dtype=BF16)
    dpg0 = _matmul(dgl0, wg[0], 'nt', "d_ple_norm0")
    dh0, (dg_ple0,) = _rmsnorm_bwd(h0, [g_ple0], [dpg0], dx1, "ple_norm0_bwd")
    slot['w_out_a'] = _matmul(m_act, dh0, 'tn', "d_out_a", out_dtype=BF16).reshape(N_DEV, -1, d)
    dm = _matmul(dh0, wa_out, 'nt', "d_m")
    dy_conv, dz, d_lg, d_lb, d_cb = _ln_gate_bwd(dm, y_conv, proj_a, lg, lb, "ln_gate_bwd")
    slot['w_ple_gate'] = jnp.stack([dwg0.reshape(N_DEV, -1, d), dwg1.reshape(N_DEV, -1, d)],
                                   axis=1).reshape(N_DEV, -1, d)
    slot['w_ple_proj'] = jnp.stack([dwp0, dwp1], axis=1).reshape(N_DEV, -1, d // N_DEV)

    dproj_a, d_cw, parts_rest = _conv_bwd(dy_conv, dz, proj_a, cw, seq, "conv_bwd",
                                          ex=_Exchange([slot[n] for n in GROUP_REST], gather=False))
    slot['w_in_a'] = _matmul(u0, dproj_a, 'tn', "d_in_a", out_dtype=BF16, slot_cols=wa_in.shape[1] // N_DEV)
    du0, parts_first = _matmul(dproj_a, wa_in, 'nt', "d_u0", ex=_Exchange([slot['w_in_a']], gather=False))
    dx0, (dg_norm0,) = _rmsnorm_bwd(x0, [g_norm0], [du0], dh0, "norm0_bwd")

    grads['norm_g'] = jnp.stack([dg_norm0, dg_norm1])
    grads['conv_w'] = d_cw[None]
    grads['conv_b'] = d_cb[None]
    grads['ln_g'] = d_lg[None]
    grads['ln_b'] = d_lb[None]
    grads['kv_norm_g'] = dg_kv
    grads['k_norm_g'] = dg_head[3]
    grads['q_norm_g'] = dg_head[0:3][None]
    grads['ple_norm_g'] = jnp.stack([dg_ple0, dg_ple1])
    small_pack = _pack([_to_slots(grads[n], SHARD_AXIS[n]) for n in GROUP_SMALL], 16, BF16)
    (parts_small,) = _run_exchange(_Exchange([small_pack], gather=False), "exchange_small")

    updated = {}
    for n, parts in zip(GROUP_REST + GROUP_FIRST, parts_rest + parts_first):
        outs = _sum_adamw(parts, rows2d(weights[n]), rows2d(mom_m[n]), rows2d(mom_v[n]), "sum_adamw_" + n)
        for kind, buf in enumerate(outs):
            updated[kind, n] = buf.reshape(weights[n].shape)
    outs = _sum_adamw(parts_small, packed(weights, GROUP_SMALL, F32), packed(mom_m, GROUP_SMALL, F32),
                      packed(mom_v, GROUP_SMALL, F32), "sum_adamw_small")
    sizes = [weights[n].size for n in GROUP_SMALL]
    for kind, buf in enumerate(outs):
        for n, flat in zip(GROUP_SMALL, _unpack(buf, sizes, 16)):
            updated[kind, n] = flat.reshape(weights[n].shape)
    result = [loss, dx0.reshape(bsz, seq, d)]
    for kind in range(4):
        result.extend(updated[kind, n] for n in WEIGHT_NAMES)
    return tuple(result)
```

```python
import functools

import jax
import jax.numpy as jnp
from jax import lax
from jax.experimental import pallas as pl
from jax.experimental.pallas import tpu as pltpu

F32 = jnp.float32
BF16 = jnp.bfloat16

N_DEV = 8
HEAD_DIM = 64
ROPE_DIM = 16
ROPE_THETA = 500000.0
EPS = 1e-6
NEG_INF = -1e30
SPAN = 128
DILATIONS = (1, 4, 16)
CONV_WIDTH = 31
HALO = 32
CONV_ROWS = 32
CONV_W_ROWS = 64
PACK_LANES = 1024
V7X_LANES = 128
V7X_SUBLANES = 8
VMEM_LIMIT_BYTES = 56 * 1024 * 1024

ADAM_LR = 0.001
ADAM_B1 = 0.9
ADAM_B2 = 0.999
ADAM_EPS = 1e-08
ADAM_WD = 0.01
ADAM_STEP = 10
ADAM_ROWS = 256

WEIGHT_NAMES = ('norm_g', 'w_in_a', 'conv_w', 'conv_b', 'ln_g', 'ln_b', 'w_out_a', 'kv_norm_g', 'w_kv',
                'k_norm_g', 'w_in_b', 'q_norm_g', 'w_out_b', 'ple_norm_g', 'w_ple_gate', 'w_ple_proj')
SHARD_AXIS = {'norm_g': None, 'w_in_a': 2, 'conv_w': 2, 'conv_b': 1, 'ln_g': 1, 'ln_b': 1, 'w_out_a': 1,
              'kv_norm_g': None, 'w_kv': 1, 'k_norm_g': None, 'w_in_b': 2, 'q_norm_g': None, 'w_out_b': 1,
              'ple_norm_g': None, 'w_ple_gate': 1, 'w_ple_proj': 2}
VECTOR_WEIGHTS = ('conv_w', 'conv_b', 'ln_g', 'ln_b')
GROUP_FIRST = ('w_in_a',)
GROUP_REST = ('w_out_a', 'w_kv', 'w_in_b', 'w_out_b', 'w_ple_gate', 'w_ple_proj')
GROUP_SMALL = ('norm_g', 'conv_w', 'conv_b', 'ln_g', 'ln_b', 'kv_norm_g', 'k_norm_g', 'q_norm_g', 'ple_norm_g')


def _pick(n, target, mult):
    t = (min(target, n) // mult) * mult
    while t >= mult:
        if n % t == 0:
            return t
        t -= mult
    return n


def _params(n_grid):
    return pltpu.CompilerParams(dimension_semantics=("arbitrary",) * n_grid, vmem_limit_bytes=VMEM_LIMIT_BYTES)


def _sig(x):
    return 0.5 * jnp.tanh(0.5 * x) + 0.5


def _colsum8(v):
    r, w = v.shape
    return v.reshape(r // V7X_SUBLANES, V7X_SUBLANES, w).sum(axis=0)


def _rows(tm, w, col=0):
    return pl.BlockSpec((tm, w), lambda i: (i, col))


def _const(shape):
    nd = len(shape)
    return pl.BlockSpec(shape, lambda i: (0,) * nd)


def _segsum(v, e_ref):
    hi = v.astype(BF16)
    lo = (v - hi.astype(F32)).astype(BF16)
    e = e_ref[...]
    return jnp.dot(hi, e, preferred_element_type=F32) + jnp.dot(lo, e, preferred_element_type=F32)


MM_TILE = 1024
MM_TILE_K = 2048


def _matmul(a, b, mode, name, out_dtype=F32, add=None, ex=None, slot_cols=None):
    if mode == 'nn':
        (m, k), (_, n) = a.shape, b.shape
    elif mode == 'nt':
        (m, k), (n, _) = a.shape, b.shape
    else:
        (k, m), (_, n) = a.shape, b.shape
    out_struct = jax.ShapeDtypeStruct((m, n), out_dtype)
    n_slots = 0
    if mode == 'tn':
        tm, tn, tk = _pick(m, MM_TILE, 128), _pick(n, MM_TILE, 128), _pick(k, MM_TILE_K, 128)
        o_spec = pl.BlockSpec((tm, tn), lambda i, j, kk: (i, j))
        if slot_cols is not None:
            assert n == N_DEV * slot_cols
            n_slots = max(s for s in (1, 2, 4, 8) if s == 1 or slot_cols * s <= MM_TILE)
            tn = slot_cols * n_slots
            o_spec = pl.BlockSpec((n_slots, tm, slot_cols), lambda i, j, kk: (j, i, 0))
            out_struct = jax.ShapeDtypeStruct((N_DEV, m, slot_cols), out_dtype)
        grid = (m // tm, n // tn, k // tk)
        a_spec = pl.BlockSpec((tk, tm), lambda i, j, kk: (kk, i))
        b_spec = pl.BlockSpec((tk, tn), lambda i, j, kk: (kk, j))
        dims = (((0,), (0,)), ((), ()))
    else:
        tm, tn, tk = _pick(m, MM_TILE, 128), _pick(n, MM_TILE, 128), _pick(k, MM_TILE_K, 128)
        grid = (n // tn, m // tm, k // tk)
        a_spec = pl.BlockSpec((tm, tk), lambda j, i, kk: (i, kk))
        o_spec = pl.BlockSpec((tm, tn), lambda j, i, kk: (i, j))
        if mode == 'nn':
            b_spec = pl.BlockSpec((tk, tn), lambda j, i, kk: (kk, j))
            dims = (((1,), (0,)), ((), ()))
        else:
            b_spec = pl.BlockSpec((tn, tk), lambda j, i, kk: (j, kk))
            dims = (((1,), (1,)), ((), ()))
    nk = grid[2]
    has_add = add is not None

    def body(*refs):
        a_ref, b_ref = refs[0], refs[1]
        add_ref = refs[2] if has_add else None
        o_ref = refs[2 + has_add]
        part = lax.dot_general(a_ref[...].astype(BF16), b_ref[...].astype(BF16), dims, preferred_element_type=F32)

        def finish(total):
            if has_add:
                total = total + add_ref[...]
            if n_slots:
                for s in range(n_slots):
                    o_ref[s] = total[:, s * slot_cols:(s + 1) * slot_cols].astype(out_dtype)
            else:
                o_ref[...] = total.astype(out_dtype)

        if nk == 1:
            finish(part)
        else:
            acc_ref = refs[3 + has_add]
            kk = pl.program_id(2)

            @pl.when(kk == 0)
            def _():
                acc_ref[...] = part

            @pl.when(kk > 0)
            def _():
                acc_ref[...] += part

            @pl.when(kk == nk - 1)
            def _():
                finish(acc_ref[...])

    in_specs = [a_spec, b_spec] + ([o_spec] if has_add else [])
    args = [a, b] + ([add] if has_add else [])
    scratch = [pltpu.VMEM((tm, tn), F32)] if nk > 1 else []
    (out,), moved = _hosted_call(body, ex, name, grid, in_specs, [o_spec], [out_struct], scratch, args)
    return out if ex is None else (out, moved)


def _rmsnorm_fwd(x, gains, name):
    t, d = x.shape
    tm = _pick(t, 512, 8)
    n = len(gains)

    def body(*refs):
        x_ref, g_refs, o_refs = refs[0], refs[1:1 + n], refs[1 + n:]
        xv = x_ref[...]
        y = xv * lax.rsqrt(jnp.mean(xv * xv, axis=-1, keepdims=True) + EPS)
        for g_ref, o_ref in zip(g_refs, o_refs):
            o_ref[...] = (y * g_ref[...]).astype(BF16)

    return pl.pallas_call(
        body, name=name, grid=(t // tm,),
        in_specs=[_rows(tm, d)] + [_const((1, d))] * n,
        out_specs=[_rows(tm, d)] * n,
        out_shape=[jax.ShapeDtypeStruct((t, d), BF16)] * n,
        compiler_params=_params(1),
    )(x, *gains)


def _ple_grads(dx, gl, pp):
    sg = _sig(gl)
    return (dx * pp * sg * (1.0 - sg)).astype(BF16), (dx * sg).astype(BF16)


def _rmsnorm_bwd(x, gains, dys, add, name, ple=None):
    t, d = x.shape
    tm = _pick(t, 512, 16)
    n = len(gains)
    n_ple = 0 if ple is None else 2

    def body(*refs):
        x_ref, add_ref = refs[0], refs[1]
        g_refs, dy_refs = refs[2:2 + n], refs[2 + n:2 + 2 * n]
        ple_refs = refs[2 + 2 * n:2 + 2 * n + n_ple]
        outs = refs[2 + 2 * n + n_ple:]
        dx_ref, dg_refs, dple_refs = outs[0], outs[1:1 + n], outs[1 + n:]
        i = pl.program_id(0)
        xv = x_ref[...]
        r = lax.rsqrt(jnp.mean(xv * xv, axis=-1, keepdims=True) + EPS)
        xhat = xv * r
        dx = add_ref[...]
        for g_ref, dy_ref, dg_ref in zip(g_refs, dy_refs, dg_refs):
            dy = dy_ref[...].astype(F32)
            dyg = dy * g_ref[...]
            dx = dx + r * (dyg - xhat * jnp.mean(dyg * xhat, axis=-1, keepdims=True))
            part = _colsum8(dy * xhat)

            @pl.when(i == 0)
            def _():
                dg_ref[...] = part

            @pl.when(i > 0)
            def _():
                dg_ref[...] += part

        dx_ref[...] = dx
        if n_ple:
            dple_refs[0][...], dple_refs[1][...] = _ple_grads(dx, ple_refs[0][...].astype(F32),
                                                               ple_refs[1][...].astype(F32))

    outs = pl.pallas_call(
        body, name=name, grid=(t // tm,),
        in_specs=[_rows(tm, d), _rows(tm, d)] + [_const((1, d))] * n + [_rows(tm, d)] * (n + n_ple),
        out_specs=[_rows(tm, d)] + [_const((V7X_SUBLANES, d))] * n + [_rows(tm, d)] * n_ple,
        out_shape=([jax.ShapeDtypeStruct((t, d), F32)] + [jax.ShapeDtypeStruct((V7X_SUBLANES, d), F32)] * n
                   + [jax.ShapeDtypeStruct((t, d), BF16)] * n_ple),
        compiler_params=_params(1),
    )(x, add, *gains, *dys, *(ple or ()))
    dgs = [o.sum(axis=0) for o in outs[1:1 + n]]
    return (outs[0], dgs) if ple is None else (outs[0], dgs, outs[1 + n], outs[2 + n])


def _ple_norm_fwd(h, gl, pp, gains, name):
    t, d = h.shape
    tm = _pick(t, 512, 16)
    n = len(gains)

    def body(*refs):
        h_ref, gl_ref, pp_ref = refs[:3]
        g_refs, x_ref, o_refs = refs[3:3 + n], refs[3 + n], refs[4 + n:]
        xv = h_ref[...] + _sig(gl_ref[...].astype(F32)) * pp_ref[...].astype(F32)
        x_ref[...] = xv
        y = xv * lax.rsqrt(jnp.mean(xv * xv, axis=-1, keepdims=True) + EPS)
        for g_ref, o_ref in zip(g_refs, o_refs):
            o_ref[...] = (y * g_ref[...]).astype(BF16)

    outs = pl.pallas_call(
        body, name=name, grid=(t // tm,),
        in_specs=[_rows(tm, d)] * 3 + [_const((1, d))] * n, out_specs=[_rows(tm, d)] * (1 + n),
        out_shape=[jax.ShapeDtypeStruct((t, d), F32)] + [jax.ShapeDtypeStruct((t, d), BF16)] * n,
        compiler_params=_params(1),
    )(h, gl, pp, *gains)
    return outs[0], outs[1:]


def _ple_loss(h, gl, pp, target, name):
    t, d = h.shape
    tm = _pick(t, 512, 16)
    inv_d = 1.0 / d

    def body(h_ref, gl_ref, pp_ref, t_ref, dy_ref, dgl_ref, dpp_ref, l_ref):
        i = pl.program_id(0)
        gl, pp = gl_ref[...].astype(F32), pp_ref[...].astype(F32)
        e = h_ref[...] + _sig(gl) * pp - t_ref[...]
        dy = e * inv_d
        dy_ref[...] = dy
        dgl_ref[...], dpp_ref[...] = _ple_grads(dy, gl, pp)
        part = _colsum8(e * e) * (0.5 * inv_d)

        @pl.when(i == 0)
        def _():
            l_ref[...] = part

        @pl.when(i > 0)
        def _():
            l_ref[...] += part

    return pl.pallas_call(
        body, name=name, grid=(t // tm,), in_specs=[_rows(tm, d)] * 4,
        out_specs=[_rows(tm, d)] * 3 + [_const((V7X_SUBLANES, d))],
        out_shape=[jax.ShapeDtypeStruct((t, d), F32), jax.ShapeDtypeStruct((t, d), BF16),
                   jax.ShapeDtypeStruct((t, d), BF16), jax.ShapeDtypeStruct((V7X_SUBLANES, d), F32)],
        compiler_params=_params(1),
    )(h, gl, pp, target)


def _shift_scratch(ts, cc):
    return pltpu.VMEM((V7X_SUBLANES, ts + HALO - V7X_SUBLANES, cc), F32)


def _shifted_copies(sh_ref, win_ref, cs, ts):
    rows = ts + HALO - V7X_SUBLANES
    for s in range(1, V7X_SUBLANES):
        sh_ref[s] = win_ref[pl.ds(s, rows), cs]


def _tap(sh_ref, win_ref, cs, offset, rows, r0):
    s = offset % V7X_SUBLANES
    start = pl.multiple_of(r0 + (offset - s), V7X_SUBLANES)
    if s == 0:
        return win_ref[pl.ds(start, rows), cs]
    return sh_ref[s, pl.ds(start, rows), :]


def _conv_fwd(proj, conv_w, conv_b, ln_g, ln_b, seq, name, ex=None):
    t, c3 = proj.shape
    c = c3 // 3
    ts = _pick(seq, 256, HALO)
    nsb = seq // ts
    cc = _pick(c, 512, V7X_LANES)
    hb = ts // HALO

    def body(a_ref, b_ref, z_ref, ap_ref, bp_ref, w_ref, cb_ref, g_ref, be_ref, m_ref, y_ref, win_ref, sh_ref):
        i = pl.program_id(0)
        first = (i % nsb) == 0
        win_ref[0:HALO, :] = jnp.where(first, 0.0, ap_ref[...] * _sig(bp_ref[...]))
        win_ref[HALO:, :] = a_ref[...] * _sig(b_ref[...])
        for ci in range(c // cc):
            cs = slice(ci * cc, (ci + 1) * cc)
            _shifted_copies(sh_ref, win_ref, cs, ts)

            def out_rows(rb, carry, cs=cs):
                r0 = rb * CONV_ROWS
                acc = jnp.zeros((CONV_ROWS, cc), F32) + cb_ref[:, cs]
                for k in range(CONV_WIDTH):
                    acc = acc + w_ref[k:k + 1, cs] * _tap(sh_ref, win_ref, cs, HALO - (CONV_WIDTH - 1) + k,
                                                           CONV_ROWS, r0)
                y_ref[pl.ds(pl.multiple_of(r0, CONV_ROWS), CONV_ROWS), cs] = acc
                return carry

            lax.fori_loop(0, ts // CONV_ROWS, out_rows, 0, unroll=2)
        y = y_ref[...]
        mu = jnp.mean(y, axis=-1, keepdims=True)
        xc = y - mu
        rstd = lax.rsqrt(jnp.mean(xc * xc, axis=-1, keepdims=True) + EPS)
        ln = xc * rstd * g_ref[...] + be_ref[...]
        zz = z_ref[...]
        m_ref[...] = (ln * _sig(ln) * zz * _sig(zz)).astype(BF16)

    halo_a = pl.BlockSpec((HALO, c), lambda i: (jnp.maximum(i * hb - 1, 0), 0))
    halo_b = pl.BlockSpec((HALO, c), lambda i: (jnp.maximum(i * hb - 1, 0), 1))
    (m_act, y), moved = _hosted_call(
        body, ex, name, (t // ts,),
        [_rows(ts, c, 0), _rows(ts, c, 1), _rows(ts, c, 2), halo_a, halo_b,
         _const((CONV_WIDTH, c)), _const((1, c)), _const((1, c)), _const((1, c))],
        [_rows(ts, c), _rows(ts, c)],
        [jax.ShapeDtypeStruct((t, c), BF16), jax.ShapeDtypeStruct((t, c), F32)],
        [pltpu.VMEM((HALO + ts, c), F32), _shift_scratch(ts, cc)],
        (proj, proj, proj, proj, proj, conv_w, conv_b, ln_g, ln_b))
    return m_act, y, moved


def _ln_gate_bwd(dm, y, proj, ln_g, ln_b, name):
    t, c = y.shape
    tm = _pick(t, 256, 8)

    def body(dm_ref, y_ref, z_ref, g_ref, be_ref, dy_ref, dz_ref, dg_ref, db_ref, dcb_ref):
        i = pl.program_id(0)
        yv = y_ref[...]
        mu = jnp.mean(yv, axis=-1, keepdims=True)
        xc = yv - mu
        rstd = lax.rsqrt(jnp.mean(xc * xc, axis=-1, keepdims=True) + EPS)
        xhat = xc * rstd
        g = g_ref[...]
        ln = xhat * g + be_ref[...]
        sl = _sig(ln)
        zz = z_ref[...]
        sz = _sig(zz)
        dmv = dm_ref[...].astype(F32)
        dz_ref[...] = (dmv * (ln * sl) * (sz * (1.0 + zz * (1.0 - sz)))).astype(BF16)
        dln = dmv * (zz * sz) * (sl * (1.0 + ln * (1.0 - sl)))
        dxh = dln * g
        dyv = rstd * (dxh - jnp.mean(dxh, axis=-1, keepdims=True)
                      - xhat * jnp.mean(dxh * xhat, axis=-1, keepdims=True))
        dy_ref[...] = dyv
        parts = (_colsum8(dln * xhat), _colsum8(dln), _colsum8(dyv))

        @pl.when(i == 0)
        def _():
            for ref, part in zip((dg_ref, db_ref, dcb_ref), parts):
                ref[...] = part

        @pl.when(i > 0)
        def _():
            for ref, part in zip((dg_ref, db_ref, dcb_ref), parts):
                ref[...] += part

    acc = jax.ShapeDtypeStruct((V7X_SUBLANES, c), F32)
    outs = pl.pallas_call(
        body, name=name, grid=(t // tm,),
        in_specs=[_rows(tm, c), _rows(tm, c), _rows(tm, c, 2), _const((1, c)), _const((1, c))],
        out_specs=[_rows(tm, c), _rows(tm, c)] + [_const((V7X_SUBLANES, c))] * 3,
        out_shape=[jax.ShapeDtypeStruct((t, c), F32), jax.ShapeDtypeStruct((t, c), BF16), acc, acc, acc],
        compiler_params=_params(1),
    )(dm, y, proj, ln_g, ln_b)
    return outs[0], outs[1], outs[2].sum(axis=0), outs[3].sum(axis=0), outs[4].sum(axis=0)


def _conv_bwd(dy, dz, proj, conv_w, seq, name, ex=None):
    t, c3 = proj.shape
    c = c3 // 3
    ts = _pick(seq, 256, HALO)
    nsb = seq // ts
    cc = _pick(c, 512, V7X_LANES)
    hb = ts // HALO
    last_halo = t // HALO - 1
    back = CONV_WIDTH - 1

    def body(dy_ref, dyn_ref, dz_ref, a_ref, b_ref, ap_ref, bp_ref, w_ref, o_ref, dw_ref, win_ref, dwin_ref,
             sh_ref, dsh_ref):
        i = pl.program_id(0)
        first = (i % nsb) == 0
        last = (i % nsb) == nsb - 1
        win_ref[0:HALO, :] = jnp.where(first, 0.0, ap_ref[...] * _sig(bp_ref[...]))
        win_ref[HALO:, :] = a_ref[...] * _sig(b_ref[...])
        dwin_ref[0:ts, :] = dy_ref[...]
        dwin_ref[ts:, :] = jnp.where(last, 0.0, dyn_ref[...])

        @pl.when(i == 0)
        def _():
            dw_ref[...] = jnp.zeros_like(dw_ref)

        for ci in range(c // cc):
            cs = slice(ci * cc, (ci + 1) * cc)
            _shifted_copies(sh_ref, win_ref, cs, ts)
            _shifted_copies(dsh_ref, dwin_ref, cs, ts)

            def in_grad_rows(rb, carry, cs=cs, ci=ci):
                r0 = rb * CONV_ROWS
                rows = pl.ds(pl.multiple_of(r0, CONV_ROWS), CONV_ROWS)
                dglu = jnp.zeros((CONV_ROWS, cc), F32)
                for k in range(CONV_WIDTH):
                    dglu = dglu + w_ref[k:k + 1, cs] * _tap(dsh_ref, dwin_ref, cs, back - k, CONV_ROWS, r0)
                sbc = _sig(b_ref[rows, cs])
                o_ref[rows, cs] = (dglu * sbc).astype(BF16)
                o_ref[rows, c + ci * cc:c + (ci + 1) * cc] = (dglu * a_ref[rows, cs] * sbc * (1.0 - sbc)).astype(BF16)
                return carry

            def w_grad_rows(rb, carry, cs=cs):
                r0 = rb * CONV_W_ROWS
                dcur = dwin_ref[pl.ds(pl.multiple_of(r0, CONV_W_ROWS), CONV_W_ROWS), cs]
                for k in range(CONV_WIDTH):
                    dw_ref[k * V7X_SUBLANES:(k + 1) * V7X_SUBLANES, cs] += _colsum8(
                        dcur * _tap(sh_ref, win_ref, cs, HALO - back + k, CONV_W_ROWS, r0))
                return carry

            lax.fori_loop(0, ts // CONV_ROWS, in_grad_rows, 0, unroll=2)
            lax.fori_loop(0, ts // CONV_W_ROWS, w_grad_rows, 0)
        o_ref[:, 2 * c:] = dz_ref[...]

    halo_next = pl.BlockSpec((HALO, c), lambda i: (jnp.minimum((i + 1) * hb, last_halo), 0))
    halo_a = pl.BlockSpec((HALO, c), lambda i: (jnp.maximum(i * hb - 1, 0), 0))
    halo_b = pl.BlockSpec((HALO, c), lambda i: (jnp.maximum(i * hb - 1, 0), 1))
    (dproj, dw), moved = _hosted_call(
        body, ex, name, (t // ts,),
        [_rows(ts, c), halo_next, _rows(ts, c), _rows(ts, c, 0), _rows(ts, c, 1), halo_a, halo_b,
         _const((CONV_WIDTH, c))],
        [_rows(ts, c3), _const((CONV_WIDTH * V7X_SUBLANES, c))],
        [jax.ShapeDtypeStruct((t, c3), BF16), jax.ShapeDtypeStruct((CONV_WIDTH * V7X_SUBLANES, c), F32)],
        [pltpu.VMEM((HALO + ts, c), F32), pltpu.VMEM((ts + HALO, c), F32),
         _shift_scratch(ts, cc), _shift_scratch(ts, cc)],
        (dy, dy, dz, proj, proj, proj, proj, conv_w))
    return dproj, dw.reshape(CONV_WIDTH, V7X_SUBLANES, c).sum(axis=1), moved


def _rope_tables(seq):
    half = ROPE_DIM // 2
    inv = ROPE_THETA ** (-jnp.arange(half, dtype=F32) * (2.0 / ROPE_DIM))
    ang = jnp.arange(seq).astype(F32)[:, None] * inv[None, :]
    cos, sin = jnp.cos(ang), jnp.sin(ang)
    zeros = jnp.zeros((seq, HEAD_DIM - ROPE_DIM), F32)
    zh = jnp.zeros((seq, half), F32)
    a = jnp.concatenate([cos, cos, zeros + 1.0], axis=1)
    b = jnp.concatenate([zh, sin, zeros], axis=1)
    c = jnp.concatenate([-sin, zh, zeros], axis=1)
    rep = V7X_LANES // HEAD_DIM
    return tuple(jnp.tile(v, (1, rep)) for v in (a, b, c))


def _head_ones(d):
    head = jnp.arange(d) // HEAD_DIM
    return (head[:, None] == head[None, :]).astype(BF16)


def _rope(ch, ta, tb, tc):
    return ta * ch + tb * pltpu.roll(ch, ROPE_DIM // 2, 1) + tc * pltpu.roll(ch, V7X_LANES - ROPE_DIM // 2, 1)


def _rope_t(ch, ta, tb, tc):
    return ta * ch + pltpu.roll(tb * ch, V7X_LANES - ROPE_DIM // 2, 1) + pltpu.roll(tc * ch, ROPE_DIM // 2, 1)


def _norm_rope_bwd(xhat, r, dout, gain, ta, tb, tc, e_ref):
    dxn = _rope_t(dout, ta, tb, tc)
    dxh = dxn * gain
    dx = r * (dxh - xhat * (_segsum(dxh * xhat, e_ref) * (1.0 / HEAD_DIM)))
    return dx, _colsum8(dxn * xhat)


def _norm_rope_rows(dst_ref, src_ref, gain, ta_ref, tb_ref, tc_ref, e_ref, seq, xhat_ref=None, r_ref=None):
    for r0 in range(0, seq, ATTN_PIECE):
        rows = slice(r0, r0 + ATTN_PIECE)
        xv = src_ref[rows, :]
        r = lax.rsqrt(_segsum(xv * xv, e_ref) * (1.0 / HEAD_DIM) + EPS)
        xhat = xv * r
        if xhat_ref is not None:
            xhat_ref[rows, :] = xhat
            r_ref[rows, :] = r
        dst_ref[rows, :] = _rope(xhat * gain, ta_ref[rows, :], tb_ref[rows, :], tc_ref[rows, :])


ATTN_PIECE = 256
ATTN_UNROLL = 4


def _pieces(dil, seq):
    length = seq // dil
    rows = min(length, ATTN_PIECE)
    return [(r + dil * ci * rows, r * length + ci * rows, rows) for r in range(dil) for ci in range(length // rows)]


def _strided(ref, start, rows, dil):
    if dil == 1:
        return ref[pl.ds(start, rows), :]
    return ref[pl.ds(start, rows, stride=dil), :]


def _strided_set(ref, start, rows, dil, val):
    if dil == 1:
        ref[pl.ds(start, rows), :] = val
    else:
        ref[pl.ds(start, rows, stride=dil), :] = val


def _nt(a, b):
    return lax.dot_general(a, b, (((1,), (1,)), ((), ())), preferred_element_type=F32)


def _tn(a, b):
    return lax.dot_general(a, b, (((0,), (0,)), ((), ())), preferred_element_type=F32)


def _set_bias(bias_ref):
    qi = lax.broadcasted_iota(jnp.int32, (2 * SPAN, 2 * SPAN), 0) & (SPAN - 1)
    kj = lax.broadcasted_iota(jnp.int32, (2 * SPAN, 2 * SPAN), 1)
    band = jnp.logical_and(kj >= qi, (kj - SPAN) <= qi)
    bias_ref[1] = jnp.where(band, 0.0, NEG_INF)
    bias_ref[0] = jnp.where(jnp.logical_and(band, kj >= SPAN), 0.0, NEG_INF)


def _block_keys(bias_ref, j, qs, nb):
    if nb == 1:
        return pl.ds(pl.multiple_of(qs + SPAN, SPAN), SPAN), bias_ref[1, :, SPAN:]
    return pl.ds(qs, 2 * SPAN), bias_ref[jnp.minimum(j & (nb - 1), 1)]


def _stack_heads(v, head0):
    zero = jnp.zeros_like(v)
    return jnp.concatenate([jnp.where(head0, v, zero), jnp.where(head0, zero, v)], axis=0)


def _unstack_heads(v2, head0):
    return jnp.where(head0, v2[:SPAN], v2[SPAN:])


def _head_cols(v):
    return jnp.concatenate([v[:, 0:1], v[:, HEAD_DIM:HEAD_DIM + 1]], axis=0)


def _attn_fwd(proj_b, kv, gains, tables, ones, bsz, seq, name):
    t, d4 = proj_b.shape
    d = d4 // 4
    nhp = d // V7X_LANES
    nblk = seq // SPAN
    scale = HEAD_DIM ** -0.5
    n_groups = len(DILATIONS)

    def body(q0_ref, q1_ref, q2_ref, k_ref, v_ref, gate_ref, gain_ref, ta_ref, tb_ref, tc_ref, e_ref,
             o_ref, l_ref, ao_ref, qd, kd, vd, od, ld, on0, on1, on2, ln0, ln1, ln2, kn, qn, bias):
        head0 = lax.broadcasted_iota(jnp.int32, (SPAN, V7X_LANES), 1) < HEAD_DIM

        @pl.when(jnp.logical_and(pl.program_id(0) == 0, pl.program_id(1) == 0))
        def _():
            _set_bias(bias)

        _norm_rope_rows(kn, k_ref, gain_ref[n_groups:n_groups + 1, :], ta_ref, tb_ref, tc_ref, e_ref, seq)
        kd[0:SPAN, :] = jnp.zeros((SPAN, V7X_LANES), BF16)
        vd[0:SPAN, :] = jnp.zeros((SPAN, V7X_LANES), BF16)
        for g, (q_ref, on, ln) in enumerate(((q0_ref, on0, ln0), (q1_ref, on1, ln1), (q2_ref, on2, ln2))):
            dil = DILATIONS[g]
            nb = seq // dil // SPAN
            _norm_rope_rows(qn, q_ref, gain_ref[g:g + 1, :], ta_ref, tb_ref, tc_ref, e_ref, seq)
            for ns, rs, rows in _pieces(dil, seq):
                qd[rs:rs + rows, :] = _strided(qn, ns, rows, dil).astype(BF16)
                kd[SPAN + rs:SPAN + rs + rows, :] = _strided(kn, ns, rows, dil).astype(BF16)
                vd[SPAN + rs:SPAN + rs + rows, :] = _strided(v_ref, ns, rows, dil).astype(BF16)

            def block(j, carry):
                qs = pl.multiple_of(j * SPAN, SPAN)
                q2 = _stack_heads(qd[pl.ds(qs, SPAN), :], head0)
                keys, mask = _block_keys(bias, j, qs, nb)
                kk = kd[keys, :]
                vv = vd[keys, :]
                s = _nt(q2, kk) * scale + mask
                mx = jnp.max(s, axis=1, keepdims=True)
                p = jnp.exp(s - mx)
                den = jnp.sum(p, axis=1, keepdims=True)
                o2 = jnp.dot(p.astype(BF16), vv, preferred_element_type=F32) / den
                l2 = jnp.broadcast_to(mx + jnp.log(den), (2 * SPAN, V7X_LANES))
                od[pl.ds(qs, SPAN), :] = _unstack_heads(o2, head0)
                ld[pl.ds(qs, SPAN), :] = _unstack_heads(l2, head0)
                return carry

            lax.fori_loop(0, nblk, block, 0, unroll=ATTN_UNROLL)
            for ns, rs, rows in _pieces(dil, seq):
                _strided_set(on, ns, rows, dil, od[rs:rs + rows, :])
                _strided_set(ln, ns, rows, dil, ld[rs:rs + rows, :])

        def merge(ci, carry):
            rows = pl.ds(pl.multiple_of(ci * ATTN_PIECE, ATTN_PIECE), ATTN_PIECE)
            ls = [ln0[rows, :], ln1[rows, :], ln2[rows, :]]
            mx = jnp.maximum(jnp.maximum(ls[0], ls[1]), ls[2])
            es = [jnp.exp(v - mx) for v in ls]
            den = es[0] + es[1] + es[2]
            ov = (es[0] * on0[rows, :] + es[1] * on1[rows, :] + es[2] * on2[rows, :]) / den
            gate = gate_ref[rows, :]
            o_ref[rows, :] = ov
            l_ref[rows, :] = mx + jnp.log(den)
            ao_ref[rows, :] = (ov * gate * _sig(gate)).astype(BF16)
            return carry

        lax.fori_loop(0, seq // ATTN_PIECE, merge, 0)

    blk = (None, seq, V7X_LANES)
    pview = proj_b.reshape(bsz, seq, d4)
    kview = kv.reshape(bsz, seq, 2 * d)
    out_spec = pl.BlockSpec(blk, lambda b, h: (b, 0, h))
    tab = pl.BlockSpec((seq, V7X_LANES), lambda b, h: (0, 0))
    nat = pltpu.VMEM((seq, V7X_LANES), F32)
    o, lse, ao = pl.pallas_call(
        body, name=name, grid=(bsz, nhp),
        in_specs=[pl.BlockSpec(blk, lambda b, h: (b, 0, h)),
                  pl.BlockSpec(blk, lambda b, h: (b, 0, nhp + h)),
                  pl.BlockSpec(blk, lambda b, h: (b, 0, 2 * nhp + h)),
                  pl.BlockSpec(blk, lambda b, h: (b, 0, h)),
                  pl.BlockSpec(blk, lambda b, h: (b, 0, nhp + h)),
                  pl.BlockSpec(blk, lambda b, h: (b, 0, 3 * nhp + h)),
                  pl.BlockSpec((n_groups + 1, V7X_LANES), lambda b, h: (0, 0)),
                  tab, tab, tab,
                  pl.BlockSpec((V7X_LANES, V7X_LANES), lambda b, h: (0, 0))],
        out_specs=[out_spec, out_spec, out_spec],
        out_shape=[jax.ShapeDtypeStruct((bsz, seq, d), F32), jax.ShapeDtypeStruct((bsz, seq, d), F32),
                   jax.ShapeDtypeStruct((bsz, seq, d), BF16)],
        scratch_shapes=[pltpu.VMEM((seq, V7X_LANES), BF16), pltpu.VMEM((SPAN + seq, V7X_LANES), BF16),
                        pltpu.VMEM((SPAN + seq, V7X_LANES), BF16), nat, nat, nat, nat, nat, nat, nat, nat, nat, nat,
                        pltpu.VMEM((2, 2 * SPAN, 2 * SPAN), F32)],
        compiler_params=_params(2),
    )(pview, pview, pview, kview, kview, pview, gains, *tables, ones)
    return o.reshape(t, d), lse.reshape(t, d), ao.reshape(t, d)


def _attn_bwd(proj_b, kv, dao, o, lse, gains, tables, ones, bsz, seq, name):
    t, d4 = proj_b.shape
    d = d4 // 4
    nhp = d // V7X_LANES
    nblk = seq // SPAN
    scale = HEAD_DIM ** -0.5
    n_groups = len(DILATIONS)
    n_chunks = seq // ATTN_PIECE

    def body(q_ref, k_ref, v_ref, gate_ref, dao_ref, o_ref, l_ref, gain_ref, ta_ref, tb_ref, tc_ref, e_ref,
             dproj_ref, dkv_ref, dg_ref, qd, kd, vd, dod, ld, deld, dqd, dkd, dvd, dqn, dkn, dvn, kn, kxh, krr,
             qn, qxh, qrr, don, deln, bias):
        head0 = lax.broadcasted_iota(jnp.int32, (SPAN, V7X_LANES), 1) < HEAD_DIM
        g = pl.program_id(2)

        @pl.when(jnp.logical_and(jnp.logical_and(pl.program_id(0) == 0, pl.program_id(1) == 0), g == 0))
        def _():
            _set_bias(bias)
            dg_ref[...] = jnp.zeros_like(dg_ref)

        @pl.when(g == 0)
        def _():
            dkn[...] = jnp.zeros_like(dkn)
            dvn[...] = jnp.zeros_like(dvn)
            _norm_rope_rows(kn, k_ref, gain_ref[n_groups:n_groups + 1, :], ta_ref, tb_ref, tc_ref, e_ref, seq,
                            kxh, krr)
            for r0 in range(0, seq, ATTN_PIECE):
                rows = slice(r0, r0 + ATTN_PIECE)
                gate = gate_ref[rows, :]
                dov = dao_ref[rows, :].astype(F32) * gate * _sig(gate)
                don[rows, :] = dov
                deln[rows, :] = _segsum(dov * o_ref[rows, :], e_ref)

        def norm_bwd_chunks(xhat_ref, r_ref, dn_ref, out_ref, gi):
            def chunk(ci, carry):
                rows = pl.ds(pl.multiple_of(ci * ATTN_PIECE, ATTN_PIECE), ATTN_PIECE)
                dx, part = _norm_rope_bwd(xhat_ref[rows, :], r_ref[rows, :], dn_ref[rows, :], gain_ref[gi:gi + 1, :],
                                          ta_ref[rows, :], tb_ref[rows, :], tc_ref[rows, :], e_ref)
                out_ref[rows, :] = dx.astype(BF16)
                dg_ref[gi] += part
                return carry
            lax.fori_loop(0, n_chunks, chunk, 0, unroll=ATTN_UNROLL)

        def group(gi):
            dil = DILATIONS[gi]
            nb = seq // dil // SPAN
            kd[0:SPAN, :] = jnp.zeros((SPAN, V7X_LANES), BF16)
            vd[0:SPAN, :] = jnp.zeros((SPAN, V7X_LANES), BF16)
            dkd[...] = jnp.zeros_like(dkd)
            dvd[...] = jnp.zeros_like(dvd)
            _norm_rope_rows(qn, q_ref, gain_ref[gi:gi + 1, :], ta_ref, tb_ref, tc_ref, e_ref, seq, qxh, qrr)
            for ns, rs, rows in _pieces(dil, seq):
                qd[rs:rs + rows, :] = _strided(qn, ns, rows, dil).astype(BF16)
                kd[SPAN + rs:SPAN + rs + rows, :] = _strided(kn, ns, rows, dil).astype(BF16)
                vd[SPAN + rs:SPAN + rs + rows, :] = _strided(v_ref, ns, rows, dil).astype(BF16)
                dod[rs:rs + rows, :] = _strided(don, ns, rows, dil).astype(BF16)
                deld[rs:rs + rows, :] = _strided(deln, ns, rows, dil)
                ld[rs:rs + rows, :] = _strided(l_ref, ns, rows, dil)

            def block(j, carry):
                qs = pl.multiple_of(j * SPAN, SPAN)
                q2 = _stack_heads(qd[pl.ds(qs, SPAN), :], head0)
                do2 = _stack_heads(dod[pl.ds(qs, SPAN), :], head0)
                keys, mask = _block_keys(bias, j, qs, nb)
                kk = kd[keys, :]
                vv = vd[keys, :]
                s = _nt(q2, kk) * scale + mask
                p = jnp.exp(s - _head_cols(ld[pl.ds(qs, SPAN), :]))
                ds = (p * (_nt(do2, vv) - _head_cols(deld[pl.ds(qs, SPAN), :])) * scale).astype(BF16)
                dqd[pl.ds(qs, SPAN), :] = _unstack_heads(jnp.dot(ds, kk, preferred_element_type=F32), head0)
                dkd[keys, :] += _tn(ds, q2)
                dvd[keys, :] += _tn(p.astype(BF16), do2)
                return carry

            lax.fori_loop(0, nblk, block, 0, unroll=ATTN_UNROLL)
            for ns, rs, rows in _pieces(dil, seq):
                _strided_set(dqn, ns, rows, dil, dqd[rs:rs + rows, :])
                _strided_set(dkn, ns, rows, dil,
                             _strided(dkn, ns, rows, dil) + dkd[SPAN + rs:SPAN + rs + rows, :])
                _strided_set(dvn, ns, rows, dil,
                             _strided(dvn, ns, rows, dil) + dvd[SPAN + rs:SPAN + rs + rows, :])
            norm_bwd_chunks(qxh, qrr, dqn, dproj_ref, gi)

        for gi in range(n_groups):
            @pl.when(g == gi)
            def _():
                group(gi)

        @pl.when(g == n_groups - 1)
        def _():
            norm_bwd_chunks(kxh, krr, dkn, dkv_ref, n_groups)

        @pl.when(g == n_groups)
        def _():
            def chunk(ci, carry):
                rows = pl.ds(pl.multiple_of(ci * ATTN_PIECE, ATTN_PIECE), ATTN_PIECE)
                gate = gate_ref[rows, :]
                sg = _sig(gate)
                dproj_ref[rows, :] = (dao_ref[rows, :].astype(F32) * o_ref[rows, :]
                                      * (sg * (1.0 + gate * (1.0 - sg)))).astype(BF16)
                dkv_ref[rows, :] = dvn[rows, :].astype(BF16)
                return carry
            lax.fori_loop(0, n_chunks, chunk, 0, unroll=ATTN_UNROLL)

    blk = (None, seq, V7X_LANES)
    pview = proj_b.reshape(bsz, seq, d4)
    kview = kv.reshape(bsz, seq, 2 * d)
    dview = (bsz, seq, d)
    d_spec = pl.BlockSpec(blk, lambda b, h, g: (b, 0, h))
    tab = pl.BlockSpec((seq, V7X_LANES), lambda b, h, g: (0, 0))
    nat = pltpu.VMEM((seq, V7X_LANES), F32)
    natb = pltpu.VMEM((seq, V7X_LANES), BF16)
    pad = pltpu.VMEM((SPAN + seq, V7X_LANES), F32)
    padb = pltpu.VMEM((SPAN + seq, V7X_LANES), BF16)
    dproj, dkv, dg = pl.pallas_call(
        body, name=name, grid=(bsz, nhp, n_groups + 1),
        in_specs=[pl.BlockSpec(blk, lambda b, h, g: (b, 0, jnp.minimum(g, n_groups - 1) * nhp + h)),
                  pl.BlockSpec(blk, lambda b, h, g: (b, 0, h)),
                  pl.BlockSpec(blk, lambda b, h, g: (b, 0, nhp + h)),
                  pl.BlockSpec(blk, lambda b, h, g: (b, 0, n_groups * nhp + h)),
                  d_spec, d_spec, d_spec,
                  pl.BlockSpec((n_groups + 1, V7X_LANES), lambda b, h, g: (0, 0)),
                  tab, tab, tab,
                  pl.BlockSpec((V7X_LANES, V7X_LANES), lambda b, h, g: (0, 0))],
        out_specs=[pl.BlockSpec(blk, lambda b, h, g: (b, 0, g * nhp + h)),
                   pl.BlockSpec(blk, lambda b, h, g: (b, 0, (g // n_groups) * nhp + h)),
                   pl.BlockSpec((n_groups + 1, V7X_SUBLANES, V7X_LANES), lambda b, h, g: (0, 0, 0))],
        out_shape=[jax.ShapeDtypeStruct((bsz, seq, d4), BF16), jax.ShapeDtypeStruct((bsz, seq, 2 * d), BF16),
                   jax.ShapeDtypeStruct((n_groups + 1, V7X_SUBLANES, V7X_LANES), F32)],
        scratch_shapes=[natb, padb, padb, natb, nat, nat, nat, pad, pad, nat, nat, nat, nat, nat, nat,
                        nat, nat, nat, nat, nat,
                        pltpu.VMEM((2, 2 * SPAN, 2 * SPAN), F32)],
        compiler_params=_params(3),
    )(pview, kview, kview, pview, dao.reshape(dview), o.reshape(dview), lse.reshape(dview), gains, *tables, ones)
    dgain = dg.sum(axis=1).reshape(n_groups + 1, V7X_LANES // HEAD_DIM, HEAD_DIM).sum(axis=1)
    return dproj.reshape(t, d4), dkv.reshape(t, 2 * d), dgain


def _mesh_position():
    x, y, c = lax.axis_index("x"), lax.axis_index("y"), lax.axis_index("c")
    return x, y, c


def _peer(x, y, c, rel):
    return (1 - x if rel & 4 else x, 1 - y if rel & 2 else y, 1 - c if rel & 1 else c)


class _Exchange:
    def __init__(self, srcs, gather):
        self.srcs = list(srcs)
        self.gather = gather
        n = self.n = len(self.srcs)
        hbm = pl.BlockSpec(memory_space=pltpu.HBM)
        self.in_specs = [hbm] * n
        self.out_specs = [hbm] * n
        self.out_shape = [jax.ShapeDtypeStruct(((N_DEV,) + a.shape) if gather else a.shape, a.dtype)
                          for a in self.srcs]
        self.scratch = [pltpu.SemaphoreType.DMA((n * (N_DEV - 1),)), pltpu.SemaphoreType.DMA((n * (N_DEV - 1),)),
                        pltpu.SemaphoreType.DMA((n,))]

    def _copies(self, ins, outs, sems):
        send_sems, recv_sems, local_sems = sems
        x, y, c = _mesh_position()
        me = 4 * x + 2 * y + c
        remote, local = [], []
        for a in range(self.n):
            mine = ins[a] if self.gather else ins[a].at[me]
            local.append(pltpu.make_async_copy(mine, outs[a].at[me], local_sems.at[a]))
            for rel in range(1, N_DEV):
                px, py, pc = _peer(x, y, c, rel)
                s = a * (N_DEV - 1) + rel - 1
                src = ins[a] if self.gather else ins[a].at[4 * px + 2 * py + pc]
                remote.append(pltpu.make_async_remote_copy(
                    src_ref=src, dst_ref=outs[a].at[me], send_sem=send_sems.at[s], recv_sem=recv_sems.at[s],
                    device_id=(px, py, pc), device_id_type=pl.DeviceIdType.MESH))
        return remote, local

    def start(self, ins, outs, sems):
        remote, local = self._copies(ins, outs, sems)
        for cp in local + remote:
            cp.start()

    def wait(self, ins, outs, sems):
        remote, local = self._copies(ins, outs, sems)
        for cp in remote:
            cp.wait_recv()
        for cp in remote:
            cp.wait_send()
        for cp in local:
            cp.wait()


def _run_exchange(ex, name):
    n = ex.n

    def body(*refs):
        ins, outs, sems = refs[:n], refs[n:2 * n], refs[2 * n:]
        ex.start(ins, outs, sems)
        ex.wait(ins, outs, sems)

    return pl.pallas_call(body, name=name, in_specs=ex.in_specs, out_specs=ex.out_specs, out_shape=ex.out_shape,
                          scratch_shapes=ex.scratch)(*ex.srcs)


def _hosted_call(body, ex, name, grid, in_specs, out_specs, out_shape, scratch_shapes, args):
    if ex is None:
        outs = pl.pallas_call(body, name=name, grid=grid, in_specs=in_specs, out_specs=out_specs, out_shape=out_shape,
                              scratch_shapes=scratch_shapes, compiler_params=_params(len(grid)))(*args)
        return list(outs), []
    n_in, n_out, n_scr, n = len(in_specs), len(out_specs), len(scratch_shapes), ex.n

    def hosted(*refs):
        h_in, e_in = refs[:n_in], refs[n_in:n_in + n]
        o0 = n_in + n
        h_out, e_out = refs[o0:o0 + n_out], refs[o0 + n_out:o0 + n_out + n]
        s0 = o0 + n_out + n
        h_scr, e_scr = refs[s0:s0 + n_scr], refs[s0 + n_scr:]
        ids = [pl.program_id(a) for a in range(len(grid))]
        first = functools.reduce(jnp.logical_and, [i == 0 for i in ids])
        last = functools.reduce(jnp.logical_and, [i == g - 1 for i, g in zip(ids, grid)])

        @pl.when(first)
        def _():
            ex.start(e_in, e_out, e_scr)

        body(*h_in, *h_out, *h_scr)

        @pl.when(last)
        def _():
            ex.wait(e_in, e_out, e_scr)

    outs = pl.pallas_call(
        hosted, name=name, grid=grid, in_specs=list(in_specs) + ex.in_specs,
        out_specs=list(out_specs) + ex.out_specs, out_shape=list(out_shape) + ex.out_shape,
        scratch_shapes=list(scratch_shapes) + ex.scratch, compiler_params=_params(len(grid)),
    )(*args, *ex.srcs)
    return list(outs[:n_out]), list(outs[n_out:])


def _sum_adamw(parts, w, m, v, name):
    _, r, wd = parts.shape
    tr = _pick(r, ADAM_ROWS, 8)
    c1 = 1.0 - ADAM_B1 ** ADAM_STEP
    c2 = 1.0 - ADAM_B2 ** ADAM_STEP

    def body(p_ref, w_ref, m_ref, v_ref, g_ref, d_ref, nm_ref, nv_ref):
        g = p_ref[0].astype(F32)
        for s in range(1, N_DEV):
            g = g + p_ref[s].astype(F32)
        nm = ADAM_B1 * m_ref[...] + (1.0 - ADAM_B1) * g
        nv = ADAM_B2 * v_ref[...] + (1.0 - ADAM_B2) * (g * g)
        g_ref[...] = g
        nm_ref[...] = nm
        nv_ref[...] = nv
        d_ref[...] = -ADAM_LR * ((nm / c1) / (jnp.sqrt(nv / c2) + ADAM_EPS) + ADAM_WD * w_ref[...])

    row = pl.BlockSpec((tr, wd), lambda i: (i, 0))
    return pl.pallas_call(
        body, name=name, grid=(r // tr,),
        in_specs=[pl.BlockSpec((N_DEV, tr, wd), lambda i: (0, i, 0)), row, row, row],
        out_specs=[row] * 4, out_shape=[jax.ShapeDtypeStruct((r, wd), F32)] * 4,
        compiler_params=_params(1),
    )(parts, w, m, v)


def _pack_rows(size, row_mult):
    rows = -(-size // PACK_LANES)
    return -(-rows // row_mult) * row_mult


def _pack(flats, row_mult, dtype, total_mult=None):
    out = []
    for f in flats:
        size = f.shape[-1]
        rows = _pack_rows(size, row_mult)
        pad = [(0, 0)] * (f.ndim - 1) + [(0, rows * PACK_LANES - size)]
        out.append(jnp.pad(f.astype(dtype), pad).reshape(f.shape[:-1] + (rows, PACK_LANES)))
    if total_mult is not None:
        total = sum(o.shape[-2] for o in out)
        extra = -(-total // total_mult) * total_mult - total
        if extra:
            out.append(jnp.zeros(out[0].shape[:-2] + (extra, PACK_LANES), dtype))
    return jnp.concatenate(out, axis=-2)


def _unpack(buf, sizes, row_mult):
    out, row = [], 0
    for size in sizes:
        rows = _pack_rows(size, row_mult)
        part = buf[..., row:row + rows, :]
        out.append(part.reshape(buf.shape[:-2] + (rows * PACK_LANES,))[..., :size])
        row += rows
    return out


def _to_slots(full, axis):
    if axis is None:
        return jnp.broadcast_to(full.reshape(1, -1), (N_DEV, full.size))
    shape = full.shape
    split = full.reshape(shape[:axis] + (N_DEV, shape[axis] // N_DEV) + shape[axis + 1:])
    return jnp.moveaxis(split, axis, 0).reshape(N_DEV, -1)


def _from_slots(slots, axis, block_shape):
    split = jnp.moveaxis(slots, 0, axis)
    shape = list(block_shape)
    shape[axis] *= N_DEV
    return split.reshape(shape)


def kernel(x, p, norm_g, w_in_a, conv_w, conv_b, ln_g, ln_b, w_out_a, kv_norm_g, w_kv, k_norm_g, w_in_b, q_norm_g, w_out_b, ple_norm_g, w_ple_gate, w_ple_proj, loss_target, m_norm_g, m_w_in_a, m_conv_w, m_conv_b, m_ln_g, m_ln_b, m_w_out_a, m_kv_norm_g, m_w_kv, m_k_norm_g, m_w_in_b, m_q_norm_g, m_w_out_b, m_ple_norm_g, m_w_ple_gate, m_w_ple_proj, v_norm_g, v_w_in_a, v_conv_w, v_conv_b, v_ln_g, v_ln_b, v_w_out_a, v_kv_norm_g, v_w_kv, v_k_norm_g, v_w_in_b, v_q_norm_g, v_w_out_b, v_ple_norm_g, v_w_ple_gate, v_w_ple_proj):
    weights = dict(zip(WEIGHT_NAMES, (norm_g, w_in_a, conv_w, conv_b, ln_g, ln_b, w_out_a, kv_norm_g, w_kv, k_norm_g,
                                      w_in_b, q_norm_g, w_out_b, ple_norm_g, w_ple_gate, w_ple_proj)))
    mom_m = dict(zip(WEIGHT_NAMES, (m_norm_g, m_w_in_a, m_conv_w, m_conv_b, m_ln_g, m_ln_b, m_w_out_a, m_kv_norm_g,
                                    m_w_kv, m_k_norm_g, m_w_in_b, m_q_norm_g, m_w_out_b, m_ple_norm_g, m_w_ple_gate,
                                    m_w_ple_proj)))
    mom_v = dict(zip(WEIGHT_NAMES, (v_norm_g, v_w_in_a, v_conv_w, v_conv_b, v_ln_g, v_ln_b, v_w_out_a, v_kv_norm_g,
                                    v_w_kv, v_k_norm_g, v_w_in_b, v_q_norm_g, v_w_out_b, v_ple_norm_g, v_w_ple_gate,
                                    v_w_ple_proj)))
    bsz, seq, d = x.shape
    t = bsz * seq
    assert seq % (max(DILATIONS) * SPAN) == 0 and d % V7X_LANES == 0

    full = {}

    def rows2d(a):
        return a.reshape(-1, a.shape[-1])

    def packed(source, names, dtype, total_mult=None):
        return _pack([source[n].reshape(-1) for n in names], 16, dtype, total_mult)

    def gathered(names, bufs):
        for n, buf in zip(names, bufs):
            full[n] = _from_slots(buf.reshape((N_DEV,) + weights[n].shape), SHARD_AXIS[n], weights[n].shape)

    w1_all, wv_all = _run_exchange(_Exchange([rows2d(weights['w_in_a']).astype(BF16),
                                              _pack([weights[n].reshape(-1) for n in VECTOR_WEIGHTS], 8, F32)],
                                             gather=True), "gather_first")
    gathered(GROUP_FIRST, [w1_all])
    for n, slots in zip(VECTOR_WEIGHTS, _unpack(wv_all, [weights[n].size for n in VECTOR_WEIGHTS], 8)):
        full[n] = _from_slots(slots.reshape((N_DEV,) + weights[n].shape), SHARD_AXIS[n], weights[n].shape)
    gather_rest = _Exchange([rows2d(weights[n]).astype(BF16) for n in GROUP_REST], gather=True)
    wa_in = full['w_in_a'][0]
    cw, cb, lg, lb = full['conv_w'][0], full['conv_b'], full['ln_g'], full['ln_b']

    tables = _rope_tables(seq)
    ones = _head_ones(V7X_LANES)
    rep = V7X_LANES // HEAD_DIM
    head_gain = jnp.concatenate([jnp.tile(q_norm_g[0], (1, rep)), jnp.tile(k_norm_g, rep)[None]], axis=0)

    x0 = x.reshape(t, d)
    p0, p1 = p[0].reshape(t, -1), p[1].reshape(t, -1)
    target = loss_target.reshape(t, d)
    g_norm0, g_norm1 = norm_g[0:1], norm_g[1:2]
    g_ple0, g_ple1 = ple_norm_g[0:1], ple_norm_g[1:2]
    g_kv = kv_norm_g.reshape(1, d)

    (u0,) = _rmsnorm_fwd(x0, [g_norm0], "norm0")
    proj_a = _matmul(u0, wa_in, 'nn', "in_a")
    m_act, y_conv, w2_all = _conv_fwd(proj_a, cw, cb, lg, lb, seq, "conv_fwd", ex=gather_rest)
    gathered(GROUP_REST, w2_all)
    wa_out = full['w_out_a'][0]
    wkv = full['w_kv']
    wb_in, wb_out = full['w_in_b'][0], full['w_out_b'][0]
    wg, wp = full['w_ple_gate'], full['w_ple_proj']
    h0 = _matmul(m_act, wa_out, 'nn', "out_a", add=x0)
    (pg0,) = _rmsnorm_fwd(h0, [g_ple0], "ple_norm0")
    gl0 = _matmul(pg0, wg[0], 'nn', "ple_gate0", out_dtype=BF16)
    pp0 = _matmul(p0, wp[0], 'nn', "ple_proj0", out_dtype=BF16)

    x1, (kvn, u1) = _ple_norm_fwd(h0, gl0, pp0, [g_kv, g_norm1], "ple0_norm1")
    kv = _matmul(kvn, wkv, 'nn', "kv")
    proj_b = _matmul(u1, wb_in, 'nn', "in_b")
    o_att, lse, ao = _attn_fwd(proj_b, kv, head_gain, tables, ones, bsz, seq, "attn_fwd")
    h1 = _matmul(ao, wb_out, 'nn', "out_b", add=x1)
    (pg1,) = _rmsnorm_fwd(h1, [g_ple1], "ple_norm1")
    gl1 = _matmul(pg1, wg[1], 'nn', "ple_gate1", out_dtype=BF16)
    pp1 = _matmul(p1, wp[1], 'nn', "ple_proj1", out_dtype=BF16)

    dx2, dgl1, dpp1, loss_part = _ple_loss(h1, gl1, pp1, target, "ple1_loss")
    loss = lax.psum(jnp.sum(loss_part), ("x", "y", "c"))

    grads = {}
    slot = {}

    dwp1 = _matmul(p1, dpp1, 'tn', "d_ple_proj1", out_dtype=BF16, slot_cols=d // N_DEV)
    dwg1 = _matmul(pg1, dgl1, 'tn', "d_ple_gate1", out_dtype=BF16)
    dpg1 = _matmul(dgl1, wg[1], 'nt', "d_ple_norm1", out_dtype=BF16)
    dh1, (dg_ple1,) = _rmsnorm_bwd(h1, [g_ple1], [dpg1], dx2, "ple_norm1_bwd")
    slot['w_out_b'] = _matmul(ao, dh1, 'tn', "d_out_b", out_dtype=BF16).reshape(N_DEV, -1, d)
    dao = _matmul(dh1, wb_out, 'nt', "d_ao", out_dtype=BF16)
    dproj_b, dkv, dg_head = _attn_bwd(proj_b, kv, dao, o_att, lse, head_gain, tables, ones, bsz, seq, "attn_bwd")
    slot['w_in_b'] = _matmul(u1, dproj_b, 'tn', "d_in_b", out_dtype=BF16, slot_cols=4 * d // N_DEV)
    du1 = _matmul(dproj_b, wb_in, 'nt', "d_u1", out_dtype=BF16)
    slot['w_kv'] = _matmul(kvn, dkv, 'tn', "d_kv", out_dtype=BF16, slot_cols=2 * d // N_DEV)
    dkvn = _matmul(dkv, wkv, 'nt', "d_kvn", out_dtype=BF16)
    dx1, (dg_kv, dg_norm1), dgl0, dpp0 = _rmsnorm_bwd(x1, [g_kv, g_norm1], [dkvn, du1], dh1, "norm1_bwd",
                                                      ple=(gl0, pp0))

    dwp0 = _matmul(p0, dpp0, 'tn', "d_ple_proj0", out_dtype=BF16, slot_cols=d // N_DEV)
    dwg0 = _matmul(pg0, dgl0, 'tn', "d_ple_gate0", out_dtype=BF16)
    dpg0 = _matmul(dgl0, wg[0], 'nt', "d_ple_norm0", out_dtype=BF16)
    dh0, (dg_ple0,) = _rmsnorm_bwd(h0, [g_ple0], [dpg0], dx1, "ple_norm0_bwd")
    slot['w_out_a'] = _matmul(m_act, dh0, 'tn', "d_out_a", out_dtype=BF16).reshape(N_DEV, -1, d)
    dm = _matmul(dh0, wa_out, 'nt', "d_m", out_dtype=BF16)
    dy_conv, dz, d_lg, d_lb, d_cb = _ln_gate_bwd(dm, y_conv, proj_a, lg, lb, "ln_gate_bwd")
    slot['w_ple_gate'] = jnp.stack([dwg0.reshape(N_DEV, -1, d), dwg1.reshape(N_DEV, -1, d)],
                                   axis=1).reshape(N_DEV, -1, d)
    slot['w_ple_proj'] = jnp.stack([dwp0, dwp1], axis=1).reshape(N_DEV, -1, d // N_DEV)

    dproj_a, d_cw, parts_rest = _conv_bwd(dy_conv, dz, proj_a, cw, seq, "conv_bwd",
                                          ex=_Exchange([slot[n] for n in GROUP_REST], gather=False))
    slot['w_in_a'] = _matmul(u0, dproj_a, 'tn', "d_in_a", out_dtype=BF16, slot_cols=wa_in.shape[1] // N_DEV)
    du0, parts_first = _matmul(dproj_a, wa_in, 'nt', "d_u0", out_dtype=BF16,
                               ex=_Exchange([slot['w_in_a']], gather=False))
    dx0, (dg_norm0,) = _rmsnorm_bwd(x0, [g_norm0], [du0], dh0, "norm0_bwd")

    grads['norm_g'] = jnp.stack([dg_norm0, dg_norm1])
    grads['conv_w'] = d_cw[None]
    grads['conv_b'] = d_cb[None]
    grads['ln_g'] = d_lg[None]
    grads['ln_b'] = d_lb[None]
    grads['kv_norm_g'] = dg_kv
    grads['k_norm_g'] = dg_head[3]
    grads['q_norm_g'] = dg_head[0:3][None]
    grads['ple_norm_g'] = jnp.stack([dg_ple0, dg_ple1])
    small_pack = _pack([_to_slots(grads[n], SHARD_AXIS[n]) for n in GROUP_SMALL], 16, BF16)
    (parts_small,) = _run_exchange(_Exchange([small_pack], gather=False), "exchange_small")

    updated = {}
    for n, parts in zip(GROUP_REST + GROUP_FIRST, parts_rest + parts_first):
        outs = _sum_adamw(parts, rows2d(weights[n]), rows2d(mom_m[n]), rows2d(mom_v[n]), "sum_adamw_" + n)
        for kind, buf in enumerate(outs):
            updated[kind, n] = buf.reshape(weights[n].shape)
    outs = _sum_adamw(parts_small, packed(weights, GROUP_SMALL, F32), packed(mom_m, GROUP_SMALL, F32),
                      packed(mom_v, GROUP_SMALL, F32), "sum_adamw_small")
    sizes = [weights[n].size for n in GROUP_SMALL]
    for kind, buf in enumerate(outs):
        for n, flat in zip(GROUP_SMALL, _unpack(buf, sizes, 16)):
            updated[kind, n] = flat.reshape(weights[n].shape)
    result = [loss, dx0.reshape(bsz, seq, d)]
    for kind in range(4):
        result.extend(updated[kind, n] for n in WEIGHT_NAMES)
    return tuple(result)
```

```python
import functools

import jax
import jax.numpy as jnp
from jax import lax
from jax.experimental import pallas as pl
from jax.experimental.pallas import tpu as pltpu

F32 = jnp.float32
BF16 = jnp.bfloat16

N_DEV = 8
HEAD_DIM = 64
ROPE_DIM = 16
ROPE_THETA = 500000.0
EPS = 1e-6
NEG_INF = -1e30
SPAN = 128
DILATIONS = (1, 4, 16)
CONV_WIDTH = 31
HALO = 32
CONV_ROWS = 32
CONV_W_ROWS = 64
PACK_LANES = 1024
V7X_LANES = 128
V7X_SUBLANES = 8
VMEM_LIMIT_BYTES = 56 * 1024 * 1024

ADAM_LR = 0.001
ADAM_B1 = 0.9
ADAM_B2 = 0.999
ADAM_EPS = 1e-08
ADAM_WD = 0.01
ADAM_STEP = 10
ADAM_ROWS = 256

WEIGHT_NAMES = ('norm_g', 'w_in_a', 'conv_w', 'conv_b', 'ln_g', 'ln_b', 'w_out_a', 'kv_norm_g', 'w_kv',
                'k_norm_g', 'w_in_b', 'q_norm_g', 'w_out_b', 'ple_norm_g', 'w_ple_gate', 'w_ple_proj')
SHARD_AXIS = {'norm_g': None, 'w_in_a': 2, 'conv_w': 2, 'conv_b': 1, 'ln_g': 1, 'ln_b': 1, 'w_out_a': 1,
              'kv_norm_g': None, 'w_kv': 1, 'k_norm_g': None, 'w_in_b': 2, 'q_norm_g': None, 'w_out_b': 1,
              'ple_norm_g': None, 'w_ple_gate': 1, 'w_ple_proj': 2}
VECTOR_WEIGHTS = ('conv_w', 'conv_b', 'ln_g', 'ln_b')
GROUP_FIRST = ('w_in_a',)
GROUP_REST = ('w_out_a', 'w_kv', 'w_in_b', 'w_out_b', 'w_ple_gate', 'w_ple_proj')
GROUP_SMALL = ('norm_g', 'conv_w', 'conv_b', 'ln_g', 'ln_b', 'kv_norm_g', 'k_norm_g', 'q_norm_g', 'ple_norm_g')


def _pick(n, target, mult):
    t = (min(target, n) // mult) * mult
    while t >= mult:
        if n % t == 0:
            return t
        t -= mult
    return n


def _params(n_grid):
    return pltpu.CompilerParams(dimension_semantics=("arbitrary",) * n_grid, vmem_limit_bytes=VMEM_LIMIT_BYTES)


def _sig(x):
    return 0.5 * jnp.tanh(0.5 * x) + 0.5


def _colsum8(v):
    r, w = v.shape
    return v.reshape(r // V7X_SUBLANES, V7X_SUBLANES, w).sum(axis=0)


def _rows(tm, w, col=0):
    return pl.BlockSpec((tm, w), lambda i: (i, col))


def _const(shape):
    nd = len(shape)
    return pl.BlockSpec(shape, lambda i: (0,) * nd)


def _segsum(v, e_ref):
    hi = v.astype(BF16)
    lo = (v - hi.astype(F32)).astype(BF16)
    e = e_ref[...]
    return jnp.dot(hi, e, preferred_element_type=F32) + jnp.dot(lo, e, preferred_element_type=F32)


MM_TILE = 1024
MM_TILE_K = 2048


def _matmul(a, b, mode, name, out_dtype=F32, add=None, ex=None, slot_cols=None):
    if mode == 'nn':
        (m, k), (_, n) = a.shape, b.shape
    elif mode == 'nt':
        (m, k), (n, _) = a.shape, b.shape
    else:
        (k, m), (_, n) = a.shape, b.shape
    out_struct = jax.ShapeDtypeStruct((m, n), out_dtype)
    n_slots = 0
    if mode == 'tn':
        tm, tn, tk = _pick(m, MM_TILE, 128), _pick(n, MM_TILE, 128), _pick(k, MM_TILE_K, 128)
        o_spec = pl.BlockSpec((tm, tn), lambda i, j, kk: (i, j))
        if slot_cols is not None:
            assert n == N_DEV * slot_cols
            n_slots = max(s for s in (1, 2, 4, 8) if s == 1 or slot_cols * s <= MM_TILE)
            tn = slot_cols * n_slots
            o_spec = pl.BlockSpec((n_slots, tm, slot_cols), lambda i, j, kk: (j, i, 0))
            out_struct = jax.ShapeDtypeStruct((N_DEV, m, slot_cols), out_dtype)
        grid = (m // tm, n // tn, k // tk)
        a_spec = pl.BlockSpec((tk, tm), lambda i, j, kk: (kk, i))
        b_spec = pl.BlockSpec((tk, tn), lambda i, j, kk: (kk, j))
        dims = (((0,), (0,)), ((), ()))
    else:
        tm, tn, tk = _pick(m, MM_TILE, 128), _pick(n, MM_TILE, 128), _pick(k, MM_TILE_K, 128)
        grid = (n // tn, m // tm, k // tk)
        a_spec = pl.BlockSpec((tm, tk), lambda j, i, kk: (i, kk))
        o_spec = pl.BlockSpec((tm, tn), lambda j, i, kk: (i, j))
        if mode == 'nn':
            b_spec = pl.BlockSpec((tk, tn), lambda j, i, kk: (kk, j))
            dims = (((1,), (0,)), ((), ()))
        else:
            b_spec = pl.BlockSpec((tn, tk), lambda j, i, kk: (j, kk))
            dims = (((1,), (1,)), ((), ()))
    nk = grid[2]
    has_add = add is not None

    def body(*refs):
        a_ref, b_ref = refs[0], refs[1]
        add_ref = refs[2] if has_add else None
        o_ref = refs[2 + has_add]
        part = lax.dot_general(a_ref[...].astype(BF16), b_ref[...].astype(BF16), dims, preferred_element_type=F32)

        def finish(total):
            if has_add:
                total = total + add_ref[...]
            if n_slots:
                for s in range(n_slots):
                    o_ref[s] = total[:, s * slot_cols:(s + 1) * slot_cols].astype(out_dtype)
            else:
                o_ref[...] = total.astype(out_dtype)

        if nk == 1:
            finish(part)
        else:
            acc_ref = refs[3 + has_add]
            kk = pl.program_id(2)

            @pl.when(kk == 0)
            def _():
                acc_ref[...] = part

            @pl.when(kk > 0)
            def _():
                acc_ref[...] += part

            @pl.when(kk == nk - 1)
            def _():
                finish(acc_ref[...])

    in_specs = [a_spec, b_spec] + ([o_spec] if has_add else [])
    args = [a, b] + ([add] if has_add else [])
    scratch = [pltpu.VMEM((tm, tn), F32)] if nk > 1 else []
    (out,), moved = _hosted_call(body, ex, name, grid, in_specs, [o_spec], [out_struct], scratch, args)
    return out if ex is None else (out, moved)


def _rmsnorm_fwd(x, gains, name):
    t, d = x.shape
    tm = _pick(t, 512, 8)
    n = len(gains)

    def body(*refs):
        x_ref, g_refs, o_refs = refs[0], refs[1:1 + n], refs[1 + n:]
        xv = x_ref[...]
        y = xv * lax.rsqrt(jnp.mean(xv * xv, axis=-1, keepdims=True) + EPS)
        for g_ref, o_ref in zip(g_refs, o_refs):
            o_ref[...] = (y * g_ref[...]).astype(BF16)

    return pl.pallas_call(
        body, name=name, grid=(t // tm,),
        in_specs=[_rows(tm, d)] + [_const((1, d))] * n,
        out_specs=[_rows(tm, d)] * n,
        out_shape=[jax.ShapeDtypeStruct((t, d), BF16)] * n,
        compiler_params=_params(1),
    )(x, *gains)


def _ple_grads(dx, gl, pp):
    sg = _sig(gl)
    return (dx * pp * sg * (1.0 - sg)).astype(BF16), (dx * sg).astype(BF16)


def _rmsnorm_bwd(x, gains, dys, add, name, ple=None):
    t, d = x.shape
    tm = _pick(t, 512, 16)
    n = len(gains)
    n_ple = 0 if ple is None else 2

    def body(*refs):
        x_ref, add_ref = refs[0], refs[1]
        g_refs, dy_refs = refs[2:2 + n], refs[2 + n:2 + 2 * n]
        ple_refs = refs[2 + 2 * n:2 + 2 * n + n_ple]
        outs = refs[2 + 2 * n + n_ple:]
        dx_ref, dg_refs, dple_refs = outs[0], outs[1:1 + n], outs[1 + n:]
        i = pl.program_id(0)
        xv = x_ref[...]
        r = lax.rsqrt(jnp.mean(xv * xv, axis=-1, keepdims=True) + EPS)
        xhat = xv * r
        dx = add_ref[...]
        for g_ref, dy_ref, dg_ref in zip(g_refs, dy_refs, dg_refs):
            dy = dy_ref[...].astype(F32)
            dyg = dy * g_ref[...]
            dx = dx + r * (dyg - xhat * jnp.mean(dyg * xhat, axis=-1, keepdims=True))
            part = _colsum8(dy * xhat)

            @pl.when(i == 0)
            def _():
                dg_ref[...] = part

            @pl.when(i > 0)
            def _():
                dg_ref[...] += part

        dx_ref[...] = dx
        if n_ple:
            dple_refs[0][...], dple_refs[1][...] = _ple_grads(dx, ple_refs[0][...].astype(F32),
                                                               ple_refs[1][...].astype(F32))

    outs = pl.pallas_call(
        body, name=name, grid=(t // tm,),
        in_specs=[_rows(tm, d), _rows(tm, d)] + [_const((1, d))] * n + [_rows(tm, d)] * (n + n_ple),
        out_specs=[_rows(tm, d)] + [_const((V7X_SUBLANES, d))] * n + [_rows(tm, d)] * n_ple,
        out_shape=([jax.ShapeDtypeStruct((t, d), F32)] + [jax.ShapeDtypeStruct((V7X_SUBLANES, d), F32)] * n
                   + [jax.ShapeDtypeStruct((t, d), BF16)] * n_ple),
        compiler_params=_params(1),
    )(x, add, *gains, *dys, *(ple or ()))
    dgs = [o.sum(axis=0) for o in outs[1:1 + n]]
    return (outs[0], dgs) if ple is None else (outs[0], dgs, outs[1 + n], outs[2 + n])


def _ple_norm_fwd(h, gl, pp, gains, name):
    t, d = h.shape
    tm = _pick(t, 512, 16)
    n = len(gains)

    def body(*refs):
        h_ref, gl_ref, pp_ref = refs[:3]
        g_refs, x_ref, o_refs = refs[3:3 + n], refs[3 + n], refs[4 + n:]
        xv = h_ref[...] + _sig(gl_ref[...].astype(F32)) * pp_ref[...].astype(F32)
        x_ref[...] = xv
        y = xv * lax.rsqrt(jnp.mean(xv * xv, axis=-1, keepdims=True) + EPS)
        for g_ref, o_ref in zip(g_refs, o_refs):
            o_ref[...] = (y * g_ref[...]).astype(BF16)

    outs = pl.pallas_call(
        body, name=name, grid=(t // tm,),
        in_specs=[_rows(tm, d)] * 3 + [_const((1, d))] * n, out_specs=[_rows(tm, d)] * (1 + n),
        out_shape=[jax.ShapeDtypeStruct((t, d), F32)] + [jax.ShapeDtypeStruct((t, d), BF16)] * n,
        compiler_params=_params(1),
    )(h, gl, pp, *gains)
    return outs[0], outs[1:]


def _ple_loss(h, gl, pp, target, name):
    t, d = h.shape
    tm = _pick(t, 512, 16)
    inv_d = 1.0 / d

    def body(h_ref, gl_ref, pp_ref, t_ref, dy_ref, dgl_ref, dpp_ref, l_ref):
        i = pl.program_id(0)
        gl, pp = gl_ref[...].astype(F32), pp_ref[...].astype(F32)
        e = h_ref[...] + _sig(gl) * pp - t_ref[...]
        dy = e * inv_d
        dy_ref[...] = dy
        dgl_ref[...], dpp_ref[...] = _ple_grads(dy, gl, pp)
        part = _colsum8(e * e) * (0.5 * inv_d)

        @pl.when(i == 0)
        def _():
            l_ref[...] = part

        @pl.when(i > 0)
        def _():
            l_ref[...] += part

    return pl.pallas_call(
        body, name=name, grid=(t // tm,), in_specs=[_rows(tm, d)] * 4,
        out_specs=[_rows(tm, d)] * 3 + [_const((V7X_SUBLANES, d))],
        out_shape=[jax.ShapeDtypeStruct((t, d), F32), jax.ShapeDtypeStruct((t, d), BF16),
                   jax.ShapeDtypeStruct((t, d), BF16), jax.ShapeDtypeStruct((V7X_SUBLANES, d), F32)],
        compiler_params=_params(1),
    )(h, gl, pp, target)


def _shift_scratch(ts, cc):
    return pltpu.VMEM((V7X_SUBLANES, ts + HALO - V7X_SUBLANES, cc), F32)


def _shifted_copies(sh_ref, win_ref, cs, ts):
    rows = ts + HALO - V7X_SUBLANES
    for s in range(1, V7X_SUBLANES):
        sh_ref[s] = win_ref[pl.ds(s, rows), cs]


def _tap(sh_ref, win_ref, cs, offset, rows, r0):
    s = offset % V7X_SUBLANES
    start = pl.multiple_of(r0 + (offset - s), V7X_SUBLANES)
    if s == 0:
        return win_ref[pl.ds(start, rows), cs]
    return sh_ref[s, pl.ds(start, rows), :]


def _conv_fwd(proj, conv_w, conv_b, ln_g, ln_b, seq, name, ex=None):
    t, c3 = proj.shape
    c = c3 // 3
    ts = _pick(seq, 256, HALO)
    nsb = seq // ts
    cc = _pick(c, 512, V7X_LANES)
    hb = ts // HALO

    def body(a_ref, b_ref, z_ref, ap_ref, bp_ref, w_ref, cb_ref, g_ref, be_ref, m_ref, y_ref, win_ref, sh_ref):
        i = pl.program_id(0)
        first = (i % nsb) == 0
        win_ref[0:HALO, :] = jnp.where(first, 0.0, ap_ref[...] * _sig(bp_ref[...]))
        win_ref[HALO:, :] = a_ref[...] * _sig(b_ref[...])
        for ci in range(c // cc):
            cs = slice(ci * cc, (ci + 1) * cc)
            _shifted_copies(sh_ref, win_ref, cs, ts)

            def out_rows(rb, carry, cs=cs):
                r0 = rb * CONV_ROWS
                acc = jnp.zeros((CONV_ROWS, cc), F32) + cb_ref[:, cs]
                for k in range(CONV_WIDTH):
                    acc = acc + w_ref[k:k + 1, cs] * _tap(sh_ref, win_ref, cs, HALO - (CONV_WIDTH - 1) + k,
                                                           CONV_ROWS, r0)
                y_ref[pl.ds(pl.multiple_of(r0, CONV_ROWS), CONV_ROWS), cs] = acc
                return carry

            lax.fori_loop(0, ts // CONV_ROWS, out_rows, 0, unroll=2)
        y = y_ref[...]
        mu = jnp.mean(y, axis=-1, keepdims=True)
        xc = y - mu
        rstd = lax.rsqrt(jnp.mean(xc * xc, axis=-1, keepdims=True) + EPS)
        ln = xc * rstd * g_ref[...] + be_ref[...]
        zz = z_ref[...]
        m_ref[...] = (ln * _sig(ln) * zz * _sig(zz)).astype(BF16)

    halo_a = pl.BlockSpec((HALO, c), lambda i: (jnp.maximum(i * hb - 1, 0), 0))
    halo_b = pl.BlockSpec((HALO, c), lambda i: (jnp.maximum(i * hb - 1, 0), 1))
    (m_act, y), moved = _hosted_call(
        body, ex, name, (t // ts,),
        [_rows(ts, c, 0), _rows(ts, c, 1), _rows(ts, c, 2), halo_a, halo_b,
         _const((CONV_WIDTH, c)), _const((1, c)), _const((1, c)), _const((1, c))],
        [_rows(ts, c), _rows(ts, c)],
        [jax.ShapeDtypeStruct((t, c), BF16), jax.ShapeDtypeStruct((t, c), F32)],
        [pltpu.VMEM((HALO + ts, c), F32), _shift_scratch(ts, cc)],
        (proj, proj, proj, proj, proj, conv_w, conv_b, ln_g, ln_b))
    return m_act, y, moved


def _ln_gate_bwd(dm, y, proj, ln_g, ln_b, name):
    t, c = y.shape
    tm = _pick(t, 256, 8)

    def body(dm_ref, y_ref, z_ref, g_ref, be_ref, dy_ref, dz_ref, dg_ref, db_ref, dcb_ref):
        i = pl.program_id(0)
        yv = y_ref[...]
        mu = jnp.mean(yv, axis=-1, keepdims=True)
        xc = yv - mu
        rstd = lax.rsqrt(jnp.mean(xc * xc, axis=-1, keepdims=True) + EPS)
        xhat = xc * rstd
        g = g_ref[...]
        ln = xhat * g + be_ref[...]
        sl = _sig(ln)
        zz = z_ref[...]
        sz = _sig(zz)
        dmv = dm_ref[...].astype(F32)
        dz_ref[...] = (dmv * (ln * sl) * (sz * (1.0 + zz * (1.0 - sz)))).astype(BF16)
        dln = dmv * (zz * sz) * (sl * (1.0 + ln * (1.0 - sl)))
        dxh = dln * g
        dyv = rstd * (dxh - jnp.mean(dxh, axis=-1, keepdims=True)
                      - xhat * jnp.mean(dxh * xhat, axis=-1, keepdims=True))
        dy_ref[...] = dyv
        parts = (_colsum8(dln * xhat), _colsum8(dln), _colsum8(dyv))

        @pl.when(i == 0)
        def _():
            for ref, part in zip((dg_ref, db_ref, dcb_ref), parts):
                ref[...] = part

        @pl.when(i > 0)
        def _():
            for ref, part in zip((dg_ref, db_ref, dcb_ref), parts):
                ref[...] += part

    acc = jax.ShapeDtypeStruct((V7X_SUBLANES, c), F32)
    outs = pl.pallas_call(
        body, name=name, grid=(t // tm,),
        in_specs=[_rows(tm, c), _rows(tm, c), _rows(tm, c, 2), _const((1, c)), _const((1, c))],
        out_specs=[_rows(tm, c), _rows(tm, c)] + [_const((V7X_SUBLANES, c))] * 3,
        out_shape=[jax.ShapeDtypeStruct((t, c), F32), jax.ShapeDtypeStruct((t, c), BF16), acc, acc, acc],
        compiler_params=_params(1),
    )(dm, y, proj, ln_g, ln_b)
    return outs[0], outs[1], outs[2].sum(axis=0), outs[3].sum(axis=0), outs[4].sum(axis=0)


def _conv_bwd(dy, dz, proj, conv_w, seq, name, ex=None):
    t, c3 = proj.shape
    c = c3 // 3
    ts = _pick(seq, 256, HALO)
    nsb = seq // ts
    cc = _pick(c, 512, V7X_LANES)
    hb = ts // HALO
    last_halo = t // HALO - 1
    back = CONV_WIDTH - 1

    def body(dy_ref, dyn_ref, dz_ref, a_ref, b_ref, ap_ref, bp_ref, w_ref, o_ref, dw_ref, win_ref, dwin_ref,
             sh_ref, dsh_ref):
        i = pl.program_id(0)
        first = (i % nsb) == 0
        last = (i % nsb) == nsb - 1
        win_ref[0:HALO, :] = jnp.where(first, 0.0, ap_ref[...] * _sig(bp_ref[...]))
        win_ref[HALO:, :] = a_ref[...] * _sig(b_ref[...])
        dwin_ref[0:ts, :] = dy_ref[...]
        dwin_ref[ts:, :] = jnp.where(last, 0.0, dyn_ref[...])

        @pl.when(i == 0)
        def _():
            dw_ref[...] = jnp.zeros_like(dw_ref)

        for ci in range(c // cc):
            cs = slice(ci * cc, (ci + 1) * cc)
            _shifted_copies(sh_ref, win_ref, cs, ts)
            _shifted_copies(dsh_ref, dwin_ref, cs, ts)

            def in_grad_rows(rb, carry, cs=cs, ci=ci):
                r0 = rb * CONV_ROWS
                rows = pl.ds(pl.multiple_of(r0, CONV_ROWS), CONV_ROWS)
                dglu = jnp.zeros((CONV_ROWS, cc), F32)
                for k in range(CONV_WIDTH):
                    dglu = dglu + w_ref[k:k + 1, cs] * _tap(dsh_ref, dwin_ref, cs, back - k, CONV_ROWS, r0)
                sbc = _sig(b_ref[rows, cs])
                o_ref[rows, cs] = (dglu * sbc).astype(BF16)
                o_ref[rows, c + ci * cc:c + (ci + 1) * cc] = (dglu * a_ref[rows, cs] * sbc * (1.0 - sbc)).astype(BF16)
                return carry

            def w_grad_rows(rb, carry, cs=cs):
                r0 = rb * CONV_W_ROWS
                dcur = dwin_ref[pl.ds(pl.multiple_of(r0, CONV_W_ROWS), CONV_W_ROWS), cs]
                for k in range(CONV_WIDTH):
                    dw_ref[k * V7X_SUBLANES:(k + 1) * V7X_SUBLANES, cs] += _colsum8(
                        dcur * _tap(sh_ref, win_ref, cs, HALO - back + k, CONV_W_ROWS, r0))
                return carry

            lax.fori_loop(0, ts // CONV_ROWS, in_grad_rows, 0, unroll=2)
            lax.fori_loop(0, ts // CONV_W_ROWS, w_grad_rows, 0)
        o_ref[:, 2 * c:] = dz_ref[...]

    halo_next = pl.BlockSpec((HALO, c), lambda i: (jnp.minimum((i + 1) * hb, last_halo), 0))
    halo_a = pl.BlockSpec((HALO, c), lambda i: (jnp.maximum(i * hb - 1, 0), 0))
    halo_b = pl.BlockSpec((HALO, c), lambda i: (jnp.maximum(i * hb - 1, 0), 1))
    (dproj, dw), moved = _hosted_call(
        body, ex, name, (t // ts,),
        [_rows(ts, c), halo_next, _rows(ts, c), _rows(ts, c, 0), _rows(ts, c, 1), halo_a, halo_b,
         _const((CONV_WIDTH, c))],
        [_rows(ts, c3), _const((CONV_WIDTH * V7X_SUBLANES, c))],
        [jax.ShapeDtypeStruct((t, c3), BF16), jax.ShapeDtypeStruct((CONV_WIDTH * V7X_SUBLANES, c), F32)],
        [pltpu.VMEM((HALO + ts, c), F32), pltpu.VMEM((ts + HALO, c), F32),
         _shift_scratch(ts, cc), _shift_scratch(ts, cc)],
        (dy, dy, dz, proj, proj, proj, proj, conv_w))
    return dproj, dw.reshape(CONV_WIDTH, V7X_SUBLANES, c).sum(axis=1), moved


def _rope_tables(seq):
    half = ROPE_DIM // 2
    inv = ROPE_THETA ** (-jnp.arange(half, dtype=F32) * (2.0 / ROPE_DIM))
    ang = jnp.arange(seq).astype(F32)[:, None] * inv[None, :]
    cos, sin = jnp.cos(ang), jnp.sin(ang)
    zeros = jnp.zeros((seq, HEAD_DIM - ROPE_DIM), F32)
    zh = jnp.zeros((seq, half), F32)
    a = jnp.concatenate([cos, cos, zeros + 1.0], axis=1)
    b = jnp.concatenate([zh, sin, zeros], axis=1)
    c = jnp.concatenate([-sin, zh, zeros], axis=1)
    rep = V7X_LANES // HEAD_DIM
    return tuple(jnp.tile(v, (1, rep)) for v in (a, b, c))


def _head_ones(d):
    head = jnp.arange(d) // HEAD_DIM
    return (head[:, None] == head[None, :]).astype(BF16)


def _rope(ch, ta, tb, tc):
    return ta * ch + tb * pltpu.roll(ch, ROPE_DIM // 2, 1) + tc * pltpu.roll(ch, V7X_LANES - ROPE_DIM // 2, 1)


def _rope_t(ch, ta, tb, tc):
    return ta * ch + pltpu.roll(tb * ch, V7X_LANES - ROPE_DIM // 2, 1) + pltpu.roll(tc * ch, ROPE_DIM // 2, 1)


def _norm_rope_bwd(xhat, r, dout, gain, ta, tb, tc, e_ref):
    dxn = _rope_t(dout, ta, tb, tc)
    dxh = dxn * gain
    dx = r * (dxh - xhat * (_segsum(dxh * xhat, e_ref) * (1.0 / HEAD_DIM)))
    return dx, _colsum8(dxn * xhat)


def _norm_rope_rows(dst_ref, src_ref, gain, ta_ref, tb_ref, tc_ref, e_ref, seq, xhat_ref=None, r_ref=None):
    for r0 in range(0, seq, ATTN_PIECE):
        rows = slice(r0, r0 + ATTN_PIECE)
        xv = src_ref[rows, :]
        r = lax.rsqrt(_segsum(xv * xv, e_ref) * (1.0 / HEAD_DIM) + EPS)
        xhat = xv * r
        if xhat_ref is not None:
            xhat_ref[rows, :] = xhat
            r_ref[rows, :] = r
        dst_ref[rows, :] = _rope(xhat * gain, ta_ref[rows, :], tb_ref[rows, :], tc_ref[rows, :])


ATTN_PIECE = 256
ATTN_UNROLL = 16
CHUNK_UNROLL = 4


def _pieces(dil, seq):
    length = seq // dil
    rows = min(length, ATTN_PIECE)
    return [(r + dil * ci * rows, r * length + ci * rows, rows) for r in range(dil) for ci in range(length // rows)]


def _strided(ref, start, rows, dil):
    if dil == 1:
        return ref[pl.ds(start, rows), :]
    return ref[pl.ds(start, rows, stride=dil), :]


def _strided_set(ref, start, rows, dil, val):
    if dil == 1:
        ref[pl.ds(start, rows), :] = val
    else:
        ref[pl.ds(start, rows, stride=dil), :] = val


def _nt(a, b):
    return lax.dot_general(a, b, (((1,), (1,)), ((), ())), preferred_element_type=F32)


def _tn(a, b):
    return lax.dot_general(a, b, (((0,), (0,)), ((), ())), preferred_element_type=F32)


def _set_bias(bias_ref):
    qi = lax.broadcasted_iota(jnp.int32, (2 * SPAN, 2 * SPAN), 0) & (SPAN - 1)
    kj = lax.broadcasted_iota(jnp.int32, (2 * SPAN, 2 * SPAN), 1)
    band = jnp.logical_and(kj >= qi, (kj - SPAN) <= qi)
    bias_ref[1] = jnp.where(band, 0.0, NEG_INF)
    bias_ref[0] = jnp.where(jnp.logical_and(band, kj >= SPAN), 0.0, NEG_INF)


def _block_keys(bias_ref, j, qs, nb):
    if nb == 1:
        return pl.ds(pl.multiple_of(qs + SPAN, SPAN), SPAN), bias_ref[1, :, SPAN:]
    return pl.ds(qs, 2 * SPAN), bias_ref[jnp.minimum(j & (nb - 1), 1)]


def _stack_heads(v, head0):
    zero = jnp.zeros_like(v)
    return jnp.concatenate([jnp.where(head0, v, zero), jnp.where(head0, zero, v)], axis=0)


def _unstack_heads(v2, head0):
    return jnp.where(head0, v2[:SPAN], v2[SPAN:])


def _head_cols(v):
    return jnp.concatenate([v[:, 0:1], v[:, HEAD_DIM:HEAD_DIM + 1]], axis=0)


def _attn_fwd(proj_b, kv, gains, tables, ones, bsz, seq, name):
    t, d4 = proj_b.shape
    d = d4 // 4
    nhp = d // V7X_LANES
    nblk = seq // SPAN
    scale = HEAD_DIM ** -0.5
    n_groups = len(DILATIONS)

    def body(q0_ref, q1_ref, q2_ref, k_ref, v_ref, gate_ref, gain_ref, ta_ref, tb_ref, tc_ref, e_ref,
             o_ref, l_ref, ao_ref, qd, kd, vd, od, ld, on0, on1, on2, ln0, ln1, ln2, kn, qn, bias):
        head0 = lax.broadcasted_iota(jnp.int32, (SPAN, V7X_LANES), 1) < HEAD_DIM

        @pl.when(jnp.logical_and(pl.program_id(0) == 0, pl.program_id(1) == 0))
        def _():
            _set_bias(bias)

        _norm_rope_rows(kn, k_ref, gain_ref[n_groups:n_groups + 1, :], ta_ref, tb_ref, tc_ref, e_ref, seq)
        kd[0:SPAN, :] = jnp.zeros((SPAN, V7X_LANES), BF16)
        vd[0:SPAN, :] = jnp.zeros((SPAN, V7X_LANES), BF16)
        for g, (q_ref, on, ln) in enumerate(((q0_ref, on0, ln0), (q1_ref, on1, ln1), (q2_ref, on2, ln2))):
            dil = DILATIONS[g]
            nb = seq // dil // SPAN
            _norm_rope_rows(qn, q_ref, gain_ref[g:g + 1, :], ta_ref, tb_ref, tc_ref, e_ref, seq)
            for ns, rs, rows in _pieces(dil, seq):
                qd[rs:rs + rows, :] = _strided(qn, ns, rows, dil).astype(BF16)
                kd[SPAN + rs:SPAN + rs + rows, :] = _strided(kn, ns, rows, dil).astype(BF16)
                vd[SPAN + rs:SPAN + rs + rows, :] = _strided(v_ref, ns, rows, dil).astype(BF16)

            def block(j, carry):
                qs = pl.multiple_of(j * SPAN, SPAN)
                q2 = _stack_heads(qd[pl.ds(qs, SPAN), :], head0)
                keys, mask = _block_keys(bias, j, qs, nb)
                kk = kd[keys, :]
                vv = vd[keys, :]
                s = _nt(q2, kk) * scale + mask
                mx = jnp.max(s, axis=1, keepdims=True)
                p = jnp.exp(s - mx)
                den = jnp.sum(p, axis=1, keepdims=True)
                o2 = jnp.dot(p.astype(BF16), vv, preferred_element_type=F32) / den
                l2 = jnp.broadcast_to(mx + jnp.log(den), (2 * SPAN, V7X_LANES))
                od[pl.ds(qs, SPAN), :] = _unstack_heads(o2, head0)
                ld[pl.ds(qs, SPAN), :] = _unstack_heads(l2, head0)
                return carry

            lax.fori_loop(0, nblk, block, 0, unroll=ATTN_UNROLL)
            for ns, rs, rows in _pieces(dil, seq):
                _strided_set(on, ns, rows, dil, od[rs:rs + rows, :])
                _strided_set(ln, ns, rows, dil, ld[rs:rs + rows, :])

        def merge(ci, carry):
            rows = pl.ds(pl.multiple_of(ci * ATTN_PIECE, ATTN_PIECE), ATTN_PIECE)
            ls = [ln0[rows, :], ln1[rows, :], ln2[rows, :]]
            mx = jnp.maximum(jnp.maximum(ls[0], ls[1]), ls[2])
            es = [jnp.exp(v - mx) for v in ls]
            den = es[0] + es[1] + es[2]
            ov = (es[0] * on0[rows, :] + es[1] * on1[rows, :] + es[2] * on2[rows, :]) / den
            gate = gate_ref[rows, :]
            o_ref[rows, :] = ov
            l_ref[rows, :] = mx + jnp.log(den)
            ao_ref[rows, :] = (ov * gate * _sig(gate)).astype(BF16)
            return carry

        lax.fori_loop(0, seq // ATTN_PIECE, merge, 0)

    blk = (None, seq, V7X_LANES)
    pview = proj_b.reshape(bsz, seq, d4)
    kview = kv.reshape(bsz, seq, 2 * d)
    out_spec = pl.BlockSpec(blk, lambda b, h: (b, 0, h))
    tab = pl.BlockSpec((seq, V7X_LANES), lambda b, h: (0, 0))
    nat = pltpu.VMEM((seq, V7X_LANES), F32)
    o, lse, ao = pl.pallas_call(
        body, name=name, grid=(bsz, nhp),
        in_specs=[pl.BlockSpec(blk, lambda b, h: (b, 0, h)),
                  pl.BlockSpec(blk, lambda b, h: (b, 0, nhp + h)),
                  pl.BlockSpec(blk, lambda b, h: (b, 0, 2 * nhp + h)),
                  pl.BlockSpec(blk, lambda b, h: (b, 0, h)),
                  pl.BlockSpec(blk, lambda b, h: (b, 0, nhp + h)),
                  pl.BlockSpec(blk, lambda b, h: (b, 0, 3 * nhp + h)),
                  pl.BlockSpec((n_groups + 1, V7X_LANES), lambda b, h: (0, 0)),
                  tab, tab, tab,
                  pl.BlockSpec((V7X_LANES, V7X_LANES), lambda b, h: (0, 0))],
        out_specs=[out_spec, out_spec, out_spec],
        out_shape=[jax.ShapeDtypeStruct((bsz, seq, d), F32), jax.ShapeDtypeStruct((bsz, seq, d), F32),
                   jax.ShapeDtypeStruct((bsz, seq, d), BF16)],
        scratch_shapes=[pltpu.VMEM((seq, V7X_LANES), BF16), pltpu.VMEM((SPAN + seq, V7X_LANES), BF16),
                        pltpu.VMEM((SPAN + seq, V7X_LANES), BF16), nat, nat, nat, nat, nat, nat, nat, nat, nat, nat,
                        pltpu.VMEM((2, 2 * SPAN, 2 * SPAN), F32)],
        compiler_params=_params(2),
    )(pview, pview, pview, kview, kview, pview, gains, *tables, ones)
    return o.reshape(t, d), lse.reshape(t, d), ao.reshape(t, d)


def _attn_bwd(proj_b, kv, dao, o, lse, gains, tables, ones, bsz, seq, name):
    t, d4 = proj_b.shape
    d = d4 // 4
    nhp = d // V7X_LANES
    nblk = seq // SPAN
    scale = HEAD_DIM ** -0.5
    n_groups = len(DILATIONS)
    n_chunks = seq // ATTN_PIECE

    def body(q_ref, k_ref, v_ref, gate_ref, dao_ref, o_ref, l_ref, gain_ref, ta_ref, tb_ref, tc_ref, e_ref,
             dproj_ref, dkv_ref, dg_ref, qd, kd, vd, dod, ld, deld, dqd, dkd, dvd, dqn, dkn, dvn, kn, kxh, krr,
             qn, qxh, qrr, don, deln, bias):
        head0 = lax.broadcasted_iota(jnp.int32, (SPAN, V7X_LANES), 1) < HEAD_DIM
        g = pl.program_id(2)

        @pl.when(jnp.logical_and(jnp.logical_and(pl.program_id(0) == 0, pl.program_id(1) == 0), g == 0))
        def _():
            _set_bias(bias)
            dg_ref[...] = jnp.zeros_like(dg_ref)

        @pl.when(g == 0)
        def _():
            dkn[...] = jnp.zeros_like(dkn)
            dvn[...] = jnp.zeros_like(dvn)
            _norm_rope_rows(kn, k_ref, gain_ref[n_groups:n_groups + 1, :], ta_ref, tb_ref, tc_ref, e_ref, seq,
                            kxh, krr)
            for r0 in range(0, seq, ATTN_PIECE):
                rows = slice(r0, r0 + ATTN_PIECE)
                gate = gate_ref[rows, :]
                dov = dao_ref[rows, :].astype(F32) * gate * _sig(gate)
                don[rows, :] = dov
                deln[rows, :] = _segsum(dov * o_ref[rows, :], e_ref)

        def norm_bwd_chunks(xhat_ref, r_ref, dn_ref, out_ref, gi):
            def chunk(ci, carry):
                rows = pl.ds(pl.multiple_of(ci * ATTN_PIECE, ATTN_PIECE), ATTN_PIECE)
                dx, part = _norm_rope_bwd(xhat_ref[rows, :], r_ref[rows, :], dn_ref[rows, :], gain_ref[gi:gi + 1, :],
                                          ta_ref[rows, :], tb_ref[rows, :], tc_ref[rows, :], e_ref)
                out_ref[rows, :] = dx.astype(BF16)
                dg_ref[gi] += part
                return carry
            lax.fori_loop(0, n_chunks, chunk, 0, unroll=CHUNK_UNROLL)

        def group(gi):
            dil = DILATIONS[gi]
            nb = seq // dil // SPAN
            kd[0:SPAN, :] = jnp.zeros((SPAN, V7X_LANES), BF16)
            vd[0:SPAN, :] = jnp.zeros((SPAN, V7X_LANES), BF16)
            dkd[...] = jnp.zeros_like(dkd)
            dvd[...] = jnp.zeros_like(dvd)
            _norm_rope_rows(qn, q_ref, gain_ref[gi:gi + 1, :], ta_ref, tb_ref, tc_ref, e_ref, seq, qxh, qrr)
            for ns, rs, rows in _pieces(dil, seq):
                qd[rs:rs + rows, :] = _strided(qn, ns, rows, dil).astype(BF16)
                kd[SPAN + rs:SPAN + rs + rows, :] = _strided(kn, ns, rows, dil).astype(BF16)
                vd[SPAN + rs:SPAN + rs + rows, :] = _strided(v_ref, ns, rows, dil).astype(BF16)
                dod[rs:rs + rows, :] = _strided(don, ns, rows, dil).astype(BF16)
                deld[rs:rs + rows, :] = _strided(deln, ns, rows, dil)
                ld[rs:rs + rows, :] = _strided(l_ref, ns, rows, dil)

            def block(j, carry):
                qs = pl.multiple_of(j * SPAN, SPAN)
                q2 = _stack_heads(qd[pl.ds(qs, SPAN), :], head0)
                do2 = _stack_heads(dod[pl.ds(qs, SPAN), :], head0)
                keys, mask = _block_keys(bias, j, qs, nb)
                kk = kd[keys, :]
                vv = vd[keys, :]
                s = _nt(q2, kk) * scale + mask
                p = jnp.exp(s - _head_cols(ld[pl.ds(qs, SPAN), :]))
                ds = (p * (_nt(do2, vv) - _head_cols(deld[pl.ds(qs, SPAN), :])) * scale).astype(BF16)
                dqd[pl.ds(qs, SPAN), :] = _unstack_heads(jnp.dot(ds, kk, preferred_element_type=F32), head0)
                dkd[keys, :] += _tn(ds, q2)
                dvd[keys, :] += _tn(p.astype(BF16), do2)
                return carry

            lax.fori_loop(0, nblk, block, 0, unroll=ATTN_UNROLL)
            for ns, rs, rows in _pieces(dil, seq):
                _strided_set(dqn, ns, rows, dil, dqd[rs:rs + rows, :])
                _strided_set(dkn, ns, rows, dil,
                             _strided(dkn, ns, rows, dil) + dkd[SPAN + rs:SPAN + rs + rows, :])
                _strided_set(dvn, ns, rows, dil,
                             _strided(dvn, ns, rows, dil) + dvd[SPAN + rs:SPAN + rs + rows, :])
            norm_bwd_chunks(qxh, qrr, dqn, dproj_ref, gi)

        for gi in range(n_groups):
            @pl.when(g == gi)
            def _():
                group(gi)

        @pl.when(g == n_groups - 1)
        def _():
            norm_bwd_chunks(kxh, krr, dkn, dkv_ref, n_groups)

        @pl.when(g == n_groups)
        def _():
            def chunk(ci, carry):
                rows = pl.ds(pl.multiple_of(ci * ATTN_PIECE, ATTN_PIECE), ATTN_PIECE)
                gate = gate_ref[rows, :]
                sg = _sig(gate)
                dproj_ref[rows, :] = (dao_ref[rows, :].astype(F32) * o_ref[rows, :]
                                      * (sg * (1.0 + gate * (1.0 - sg)))).astype(BF16)
                dkv_ref[rows, :] = dvn[rows, :].astype(BF16)
                return carry
            lax.fori_loop(0, n_chunks, chunk, 0, unroll=CHUNK_UNROLL)

    blk = (None, seq, V7X_LANES)
    pview = proj_b.reshape(bsz, seq, d4)
    kview = kv.reshape(bsz, seq, 2 * d)
    dview = (bsz, seq, d)
    d_spec = pl.BlockSpec(blk, lambda b, h, g: (b, 0, h))
    tab = pl.BlockSpec((seq, V7X_LANES), lambda b, h, g: (0, 0))
    nat = pltpu.VMEM((seq, V7X_LANES), F32)
    natb = pltpu.VMEM((seq, V7X_LANES), BF16)
    pad = pltpu.VMEM((SPAN + seq, V7X_LANES), F32)
    padb = pltpu.VMEM((SPAN + seq, V7X_LANES), BF16)
    dproj, dkv, dg = pl.pallas_call(
        body, name=name, grid=(bsz, nhp, n_groups + 1),
        in_specs=[pl.BlockSpec(blk, lambda b, h, g: (b, 0, jnp.minimum(g, n_groups - 1) * nhp + h)),
                  pl.BlockSpec(blk, lambda b, h, g: (b, 0, h)),
                  pl.BlockSpec(blk, lambda b, h, g: (b, 0, nhp + h)),
                  pl.BlockSpec(blk, lambda b, h, g: (b, 0, n_groups * nhp + h)),
                  d_spec, d_spec, d_spec,
                  pl.BlockSpec((n_groups + 1, V7X_LANES), lambda b, h, g: (0, 0)),
                  tab, tab, tab,
                  pl.BlockSpec((V7X_LANES, V7X_LANES), lambda b, h, g: (0, 0))],
        out_specs=[pl.BlockSpec(blk, lambda b, h, g: (b, 0, g * nhp + h)),
                   pl.BlockSpec(blk, lambda b, h, g: (b, 0, (g // n_groups) * nhp + h)),
                   pl.BlockSpec((n_groups + 1, V7X_SUBLANES, V7X_LANES), lambda b, h, g: (0, 0, 0))],
        out_shape=[jax.ShapeDtypeStruct((bsz, seq, d4), BF16), jax.ShapeDtypeStruct((bsz, seq, 2 * d), BF16),
                   jax.ShapeDtypeStruct((n_groups + 1, V7X_SUBLANES, V7X_LANES), F32)],
        scratch_shapes=[natb, padb, padb, natb, nat, nat, nat, pad, pad, nat, nat, nat, nat, nat, nat,
                        nat, nat, nat, nat, nat,
                        pltpu.VMEM((2, 2 * SPAN, 2 * SPAN), F32)],
        compiler_params=_params(3),
    )(pview, kview, kview, pview, dao.reshape(dview), o.reshape(dview), lse.reshape(dview), gains, *tables, ones)
    dgain = dg.sum(axis=1).reshape(n_groups + 1, V7X_LANES // HEAD_DIM, HEAD_DIM).sum(axis=1)
    return dproj.reshape(t, d4), dkv.reshape(t, 2 * d), dgain


def _mesh_position():
    x, y, c = lax.axis_index("x"), lax.axis_index("y"), lax.axis_index("c")
    return x, y, c


def _peer(x, y, c, rel):
    return (1 - x if rel & 4 else x, 1 - y if rel & 2 else y, 1 - c if rel & 1 else c)


class _Exchange:
    def __init__(self, srcs, gather):
        self.srcs = list(srcs)
        self.gather = gather
        n = self.n = len(self.srcs)
        hbm = pl.BlockSpec(memory_space=pltpu.HBM)
        self.in_specs = [hbm] * n
        self.out_specs = [hbm] * n
        self.out_shape = [jax.ShapeDtypeStruct(((N_DEV,) + a.shape) if gather else a.shape, a.dtype)
                          for a in self.srcs]
        self.scratch = [pltpu.SemaphoreType.DMA((n * (N_DEV - 1),)), pltpu.SemaphoreType.DMA((n * (N_DEV - 1),)),
                        pltpu.SemaphoreType.DMA((n,))]

    def _copies(self, ins, outs, sems):
        send_sems, recv_sems, local_sems = sems
        x, y, c = _mesh_position()
        me = 4 * x + 2 * y + c
        remote, local = [], []
        for a in range(self.n):
            mine = ins[a] if self.gather else ins[a].at[me]
            local.append(pltpu.make_async_copy(mine, outs[a].at[me], local_sems.at[a]))
            for rel in range(1, N_DEV):
                px, py, pc = _peer(x, y, c, rel)
                s = a * (N_DEV - 1) + rel - 1
                src = ins[a] if self.gather else ins[a].at[4 * px + 2 * py + pc]
                remote.append(pltpu.make_async_remote_copy(
                    src_ref=src, dst_ref=outs[a].at[me], send_sem=send_sems.at[s], recv_sem=recv_sems.at[s],
                    device_id=(px, py, pc), device_id_type=pl.DeviceIdType.MESH))
        return remote, local

    def start(self, ins, outs, sems):
        remote, local = self._copies(ins, outs, sems)
        for cp in local + remote:
            cp.start()

    def wait(self, ins, outs, sems):
        remote, local = self._copies(ins, outs, sems)
        for cp in remote:
            cp.wait_recv()
        for cp in remote:
            cp.wait_send()
        for cp in local:
            cp.wait()


def _run_exchange(ex, name):
    n = ex.n

    def body(*refs):
        ins, outs, sems = refs[:n], refs[n:2 * n], refs[2 * n:]
        ex.start(ins, outs, sems)
        ex.wait(ins, outs, sems)

    return pl.pallas_call(body, name=name, in_specs=ex.in_specs, out_specs=ex.out_specs, out_shape=ex.out_shape,
                          scratch_shapes=ex.scratch)(*ex.srcs)


def _hosted_call(body, ex, name, grid, in_specs, out_specs, out_shape, scratch_shapes, args):
    if ex is None:
        outs = pl.pallas_call(body, name=name, grid=grid, in_specs=in_specs, out_specs=out_specs, out_shape=out_shape,
                              scratch_shapes=scratch_shapes, compiler_params=_params(len(grid)))(*args)
        return list(outs), []
    n_in, n_out, n_scr, n = len(in_specs), len(out_specs), len(scratch_shapes), ex.n

    def hosted(*refs):
        h_in, e_in = refs[:n_in], refs[n_in:n_in + n]
        o0 = n_in + n
        h_out, e_out = refs[o0:o0 + n_out], refs[o0 + n_out:o0 + n_out + n]
        s0 = o0 + n_out + n
        h_scr, e_scr = refs[s0:s0 + n_scr], refs[s0 + n_scr:]
        ids = [pl.program_id(a) for a in range(len(grid))]
        first = functools.reduce(jnp.logical_and, [i == 0 for i in ids])
        last = functools.reduce(jnp.logical_and, [i == g - 1 for i, g in zip(ids, grid)])

        @pl.when(first)
        def _():
            ex.start(e_in, e_out, e_scr)

        body(*h_in, *h_out, *h_scr)

        @pl.when(last)
        def _():
            ex.wait(e_in, e_out, e_scr)

    outs = pl.pallas_call(
        hosted, name=name, grid=grid, in_specs=list(in_specs) + ex.in_specs,
        out_specs=list(out_specs) + ex.out_specs, out_shape=list(out_shape) + ex.out_shape,
        scratch_shapes=list(scratch_shapes) + ex.scratch, compiler_params=_params(len(grid)),
    )(*args, *ex.srcs)
    return list(outs[:n_out]), list(outs[n_out:])


def _sum_adamw(parts, w, m, v, name):
    _, r, wd = parts.shape
    tr = _pick(r, ADAM_ROWS, 8)
    c1 = 1.0 - ADAM_B1 ** ADAM_STEP
    c2 = 1.0 - ADAM_B2 ** ADAM_STEP

    def body(p_ref, w_ref, m_ref, v_ref, g_ref, d_ref, nm_ref, nv_ref):
        g = p_ref[0].astype(F32)
        for s in range(1, N_DEV):
            g = g + p_ref[s].astype(F32)
        nm = ADAM_B1 * m_ref[...] + (1.0 - ADAM_B1) * g
        nv = ADAM_B2 * v_ref[...] + (1.0 - ADAM_B2) * (g * g)
        g_ref[...] = g
        nm_ref[...] = nm
        nv_ref[...] = nv
        d_ref[...] = -ADAM_LR * ((nm / c1) / (jnp.sqrt(nv / c2) + ADAM_EPS) + ADAM_WD * w_ref[...])

    row = pl.BlockSpec((tr, wd), lambda i: (i, 0))
    return pl.pallas_call(
        body, name=name, grid=(r // tr,),
        in_specs=[pl.BlockSpec((N_DEV, tr, wd), lambda i: (0, i, 0)), row, row, row],
        out_specs=[row] * 4, out_shape=[jax.ShapeDtypeStruct((r, wd), F32)] * 4,
        compiler_params=_params(1),
    )(parts, w, m, v)


def _pack_rows(size, row_mult):
    rows = -(-size // PACK_LANES)
    return -(-rows // row_mult) * row_mult


def _pack(flats, row_mult, dtype, total_mult=None):
    out = []
    for f in flats:
        size = f.shape[-1]
        rows = _pack_rows(size, row_mult)
        pad = [(0, 0)] * (f.ndim - 1) + [(0, rows * PACK_LANES - size)]
        out.append(jnp.pad(f.astype(dtype), pad).reshape(f.shape[:-1] + (rows, PACK_LANES)))
    if total_mult is not None:
        total = sum(o.shape[-2] for o in out)
        extra = -(-total // total_mult) * total_mult - total
        if extra:
            out.append(jnp.zeros(out[0].shape[:-2] + (extra, PACK_LANES), dtype))
    return jnp.concatenate(out, axis=-2)


def _unpack(buf, sizes, row_mult):
    out, row = [], 0
    for size in sizes:
        rows = _pack_rows(size, row_mult)
        part = buf[..., row:row + rows, :]
        out.append(part.reshape(buf.shape[:-2] + (rows * PACK_LANES,))[..., :size])
        row += rows
    return out


def _to_slots(full, axis):
    if axis is None:
        return jnp.broadcast_to(full.reshape(1, -1), (N_DEV, full.size))
    shape = full.shape
    split = full.reshape(shape[:axis] + (N_DEV, shape[axis] // N_DEV) + shape[axis + 1:])
    return jnp.moveaxis(split, axis, 0).reshape(N_DEV, -1)


def _from_slots(slots, axis, block_shape):
    split = jnp.moveaxis(slots, 0, axis)
    shape = list(block_shape)
    shape[axis] *= N_DEV
    return split.reshape(shape)


def kernel(x, p, norm_g, w_in_a, conv_w, conv_b, ln_g, ln_b, w_out_a, kv_norm_g, w_kv, k_norm_g, w_in_b, q_norm_g, w_out_b, ple_norm_g, w_ple_gate, w_ple_proj, loss_target, m_norm_g, m_w_in_a, m_conv_w, m_conv_b, m_ln_g, m_ln_b, m_w_out_a, m_kv_norm_g, m_w_kv, m_k_norm_g, m_w_in_b, m_q_norm_g, m_w_out_b, m_ple_norm_g, m_w_ple_gate, m_w_ple_proj, v_norm_g, v_w_in_a, v_conv_w, v_conv_b, v_ln_g, v_ln_b, v_w_out_a, v_kv_norm_g, v_w_kv, v_k_norm_g, v_w_in_b, v_q_norm_g, v_w_out_b, v_ple_norm_g, v_w_ple_gate, v_w_ple_proj):
    weights = dict(zip(WEIGHT_NAMES, (norm_g, w_in_a, conv_w, conv_b, ln_g, ln_b, w_out_a, kv_norm_g, w_kv, k_norm_g,
                                      w_in_b, q_norm_g, w_out_b, ple_norm_g, w_ple_gate, w_ple_proj)))
    mom_m = dict(zip(WEIGHT_NAMES, (m_norm_g, m_w_in_a, m_conv_w, m_conv_b, m_ln_g, m_ln_b, m_w_out_a, m_kv_norm_g,
                                    m_w_kv, m_k_norm_g, m_w_in_b, m_q_norm_g, m_w_out_b, m_ple_norm_g, m_w_ple_gate,
                                    m_w_ple_proj)))
    mom_v = dict(zip(WEIGHT_NAMES, (v_norm_g, v_w_in_a, v_conv_w, v_conv_b, v_ln_g, v_ln_b, v_w_out_a, v_kv_norm_g,
                                    v_w_kv, v_k_norm_g, v_w_in_b, v_q_norm_g, v_w_out_b, v_ple_norm_g, v_w_ple_gate,
                                    v_w_ple_proj)))
    bsz, seq, d = x.shape
    t = bsz * seq
    assert seq % (max(DILATIONS) * SPAN) == 0 and d % V7X_LANES == 0

    full = {}

    def rows2d(a):
        return a.reshape(-1, a.shape[-1])

    def packed(source, names, dtype, total_mult=None):
        return _pack([source[n].reshape(-1) for n in names], 16, dtype, total_mult)

    def gathered(names, bufs):
        for n, buf in zip(names, bufs):
            full[n] = _from_slots(buf.reshape((N_DEV,) + weights[n].shape), SHARD_AXIS[n], weights[n].shape)

    w1_all, wv_all = _run_exchange(_Exchange([rows2d(weights['w_in_a']).astype(BF16),
                                              _pack([weights[n].reshape(-1) for n in VECTOR_WEIGHTS], 8, F32)],
                                             gather=True), "gather_first")
    gathered(GROUP_FIRST, [w1_all])
    for n, slots in zip(VECTOR_WEIGHTS, _unpack(wv_all, [weights[n].size for n in VECTOR_WEIGHTS], 8)):
        full[n] = _from_slots(slots.reshape((N_DEV,) + weights[n].shape), SHARD_AXIS[n], weights[n].shape)
    gather_rest = _Exchange([rows2d(weights[n]).astype(BF16) for n in GROUP_REST], gather=True)
    wa_in = full['w_in_a'][0]
    cw, cb, lg, lb = full['conv_w'][0], full['conv_b'], full['ln_g'], full['ln_b']

    tables = _rope_tables(seq)
    ones = _head_ones(V7X_LANES)
    rep = V7X_LANES // HEAD_DIM
    head_gain = jnp.concatenate([jnp.tile(q_norm_g[0], (1, rep)), jnp.tile(k_norm_g, rep)[None]], axis=0)

    x0 = x.reshape(t, d)
    p0, p1 = p[0].reshape(t, -1), p[1].reshape(t, -1)
    target = loss_target.reshape(t, d)
    g_norm0, g_norm1 = norm_g[0:1], norm_g[1:2]
    g_ple0, g_ple1 = ple_norm_g[0:1], ple_norm_g[1:2]
    g_kv = kv_norm_g.reshape(1, d)

    (u0,) = _rmsnorm_fwd(x0, [g_norm0], "norm0")
    proj_a = _matmul(u0, wa_in, 'nn', "in_a")
    m_act, y_conv, w2_all = _conv_fwd(proj_a, cw, cb, lg, lb, seq, "conv_fwd", ex=gather_rest)
    gathered(GROUP_REST, w2_all)
    wa_out = full['w_out_a'][0]
    wkv = full['w_kv']
    wb_in, wb_out = full['w_in_b'][0], full['w_out_b'][0]
    wg, wp = full['w_ple_gate'], full['w_ple_proj']
    h0 = _matmul(m_act, wa_out, 'nn', "out_a", add=x0)
    (pg0,) = _rmsnorm_fwd(h0, [g_ple0], "ple_norm0")
    gl0 = _matmul(pg0, wg[0], 'nn', "ple_gate0", out_dtype=BF16)
    pp0 = _matmul(p0, wp[0], 'nn', "ple_proj0", out_dtype=BF16)

    x1, (kvn, u1) = _ple_norm_fwd(h0, gl0, pp0, [g_kv, g_norm1], "ple0_norm1")
    kv = _matmul(kvn, wkv, 'nn', "kv")
    proj_b = _matmul(u1, wb_in, 'nn', "in_b")
    o_att, lse, ao = _attn_fwd(proj_b, kv, head_gain, tables, ones, bsz, seq, "attn_fwd")
    h1 = _matmul(ao, wb_out, 'nn', "out_b", add=x1)
    (pg1,) = _rmsnorm_fwd(h1, [g_ple1], "ple_norm1")
    gl1 = _matmul(pg1, wg[1], 'nn', "ple_gate1", out_dtype=BF16)
    pp1 = _matmul(p1, wp[1], 'nn', "ple_proj1", out_dtype=BF16)

    dx2, dgl1, dpp1, loss_part = _ple_loss(h1, gl1, pp1, target, "ple1_loss")
    loss = lax.psum(jnp.sum(loss_part), ("x", "y", "c"))

    grads = {}
    slot = {}

    dwp1 = _matmul(p1, dpp1, 'tn', "d_ple_proj1", out_dtype=BF16, slot_cols=d // N_DEV)
    dwg1 = _matmul(pg1, dgl1, 'tn', "d_ple_gate1", out_dtype=BF16)
    dpg1 = _matmul(dgl1, wg[1], 'nt', "d_ple_norm1", out_dtype=BF16)
    dh1, (dg_ple1,) = _rmsnorm_bwd(h1, [g_ple1], [dpg1], dx2, "ple_norm1_bwd")
    slot['w_out_b'] = _matmul(ao, dh1, 'tn', "d_out_b", out_dtype=BF16).reshape(N_DEV, -1, d)
    dao = _matmul(dh1, wb_out, 'nt', "d_ao", out_dtype=BF16)
    dproj_b, dkv, dg_head = _attn_bwd(proj_b, kv, dao, o_att, lse, head_gain, tables, ones, bsz, seq, "attn_bwd")
    slot['w_in_b'] = _matmul(u1, dproj_b, 'tn', "d_in_b", out_dtype=BF16, slot_cols=4 * d // N_DEV)
    du1 = _matmul(dproj_b, wb_in, 'nt', "d_u1", out_dtype=BF16)
    slot['w_kv'] = _matmul(kvn, dkv, 'tn', "d_kv", out_dtype=BF16, slot_cols=2 * d // N_DEV)
    dkvn = _matmul(dkv, wkv, 'nt', "d_kvn", out_dtype=BF16)
    dx1, (dg_kv, dg_norm1), dgl0, dpp0 = _rmsnorm_bwd(x1, [g_kv, g_norm1], [dkvn, du1], dh1, "norm1_bwd",
                                                      ple=(gl0, pp0))

    dwp0 = _matmul(p0, dpp0, 'tn', "d_ple_proj0", out_dtype=BF16, slot_cols=d // N_DEV)
    dwg0 = _matmul(pg0, dgl0, 'tn', "d_ple_gate0", out_dtype=BF16)
    dpg0 = _matmul(dgl0, wg[0], 'nt', "d_ple_norm0", out_dtype=BF16)
    dh0, (dg_ple0,) = _rmsnorm_bwd(h0, [g_ple0], [dpg0], dx1, "ple_norm0_bwd")
    slot['w_out_a'] = _matmul(m_act, dh0, 'tn', "d_out_a", out_dtype=BF16).reshape(N_DEV, -1, d)
    dm = _matmul(dh0, wa_out, 'nt', "d_m", out_dtype=BF16)
    dy_conv, dz, d_lg, d_lb, d_cb = _ln_gate_bwd(dm, y_conv, proj_a, lg, lb, "ln_gate_bwd")
    slot['w_ple_gate'] = jnp.stack([dwg0.reshape(N_DEV, -1, d), dwg1.reshape(N_DEV, -1, d)],
                                   axis=1).reshape(N_DEV, -1, d)
    slot['w_ple_proj'] = jnp.stack([dwp0, dwp1], axis=1).reshape(N_DEV, -1, d // N_DEV)

    dproj_a, d_cw, parts_rest = _conv_bwd(dy_conv, dz, proj_a, cw, seq, "conv_bwd",
                                          ex=_Exchange([slot[n] for n in GROUP_REST], gather=False))
    slot['w_in_a'] = _matmul(u0, dproj_a, 'tn', "d_in_a", out_dtype=BF16, slot_cols=wa_in.shape[1] // N_DEV)
    du0, parts_first = _matmul(dproj_a, wa_in, 'nt', "d_u0", out_dtype=BF16,
                               ex=_Exchange([slot['w_in_a']], gather=False))
    dx0, (dg_norm0,) = _rmsnorm_bwd(x0, [g_norm0], [du0], dh0, "norm0_bwd")

    grads['norm_g'] = jnp.stack([dg_norm0, dg_norm1])
    grads['conv_w'] = d_cw[None]
    grads['conv_b'] = d_cb[None]
    grads['ln_g'] = d_lg[None]
    grads['ln_b'] = d_lb[None]
    grads['kv_norm_g'] = dg_kv
    grads['k_norm_g'] = dg_head[3]
    grads['q_norm_g'] = dg_head[0:3][None]
    grads['ple_norm_g'] = jnp.stack([dg_ple0, dg_ple1])
    small_pack = _pack([_to_slots(grads[n], SHARD_AXIS[n]) for n in GROUP_SMALL], 16, BF16)
    (parts_small,) = _run_exchange(_Exchange([small_pack], gather=False), "exchange_small")

    updated = {}
    for n, parts in zip(GROUP_REST + GROUP_FIRST, parts_rest + parts_first):
        outs = _sum_adamw(parts, rows2d(weights[n]), rows2d(mom_m[n]), rows2d(mom_v[n]), "sum_adamw_" + n)
        for kind, buf in enumerate(outs):
            updated[kind, n] = buf.reshape(weights[n].shape)
    outs = _sum_adamw(parts_small, packed(weights, GROUP_SMALL, F32), packed(mom_m, GROUP_SMALL, F32),
                      packed(mom_v, GROUP_SMALL, F32), "sum_adamw_small")
    sizes = [weights[n].size for n in GROUP_SMALL]
    for kind, buf in enumerate(outs):
        for n, flat in zip(GROUP_SMALL, _unpack(buf, sizes, 16)):
            updated[kind, n] = flat.reshape(weights[n].shape)
    result = [loss, dx0.reshape(bsz, seq, d)]
    for kind in range(4):
        result.extend(updated[kind, n] for n in WEIGHT_NAMES)
    return tuple(result)
```

```python
import functools

import jax
import jax.numpy as jnp
from jax import lax
from jax.experimental import pallas as pl
from jax.experimental.pallas import tpu as pltpu

F32 = jnp.float32
BF16 = jnp.bfloat16

N_DEV = 8
HEAD_DIM = 64
ROPE_DIM = 16
ROPE_THETA = 500000.0
EPS = 1e-6
NEG_INF = -1e30
SPAN = 128
DILATIONS = (1, 4, 16)
CONV_WIDTH = 31
HALO = 32
CONV_ROWS = 32
CONV_W_ROWS = 64
PACK_LANES = 1024
V7X_LANES = 128
V7X_SUBLANES = 8
VMEM_LIMIT_BYTES = 56 * 1024 * 1024

ADAM_LR = 0.001
ADAM_B1 = 0.9
ADAM_B2 = 0.999
ADAM_EPS = 1e-08
ADAM_WD = 0.01
ADAM_STEP = 10
ADAM_ROWS = 256

WEIGHT_NAMES = ('norm_g', 'w_in_a', 'conv_w', 'conv_b', 'ln_g', 'ln_b', 'w_out_a', 'kv_norm_g', 'w_kv',
                'k_norm_g', 'w_in_b', 'q_norm_g', 'w_out_b', 'ple_norm_g', 'w_ple_gate', 'w_ple_proj')
SHARD_AXIS = {'norm_g': None, 'w_in_a': 2, 'conv_w': 2, 'conv_b': 1, 'ln_g': 1, 'ln_b': 1, 'w_out_a': 1,
              'kv_norm_g': None, 'w_kv': 1, 'k_norm_g': None, 'w_in_b': 2, 'q_norm_g': None, 'w_out_b': 1,
              'ple_norm_g': None, 'w_ple_gate': 1, 'w_ple_proj': 2}
VECTOR_WEIGHTS = ('conv_w', 'conv_b', 'ln_g', 'ln_b')
GROUP_FIRST = ('w_in_a',)
GROUP_REST = ('w_out_a', 'w_kv', 'w_in_b', 'w_out_b', 'w_ple_gate', 'w_ple_proj')
GROUP_SMALL = ('norm_g', 'conv_w', 'conv_b', 'ln_g', 'ln_b', 'kv_norm_g', 'k_norm_g', 'q_norm_g', 'ple_norm_g')


def _pick(n, target, mult):
    t = (min(target, n) // mult) * mult
    while t >= mult:
        if n % t == 0:
            return t
        t -= mult
    return n


def _params(n_grid):
    return pltpu.CompilerParams(dimension_semantics=("arbitrary",) * n_grid, vmem_limit_bytes=VMEM_LIMIT_BYTES)


def _sig(x):
    return 0.5 * jnp.tanh(0.5 * x) + 0.5


def _colsum8(v):
    r, w = v.shape
    return v.reshape(r // V7X_SUBLANES, V7X_SUBLANES, w).sum(axis=0)


def _rows(tm, w, col=0):
    return pl.BlockSpec((tm, w), lambda i: (i, col))


def _const(shape):
    nd = len(shape)
    return pl.BlockSpec(shape, lambda i: (0,) * nd)


def _segsum(v, e_ref):
    hi = v.astype(BF16)
    lo = (v - hi.astype(F32)).astype(BF16)
    e = e_ref[...]
    return jnp.dot(hi, e, preferred_element_type=F32) + jnp.dot(lo, e, preferred_element_type=F32)


MM_TILE = 1024
MM_TILE_K = 2048


def _matmul(a, b, mode, name, out_dtype=F32, add=None, ex=None, slot_cols=None, norm_gain=None):
    if mode == 'nn':
        (m, k), (_, n) = a.shape, b.shape
    elif mode == 'nt':
        (m, k), (n, _) = a.shape, b.shape
    else:
        (k, m), (_, n) = a.shape, b.shape
    out_struct = jax.ShapeDtypeStruct((m, n), out_dtype)
    n_slots = 0
    if mode == 'tn':
        tm, tn, tk = _pick(m, MM_TILE, 128), _pick(n, MM_TILE, 128), _pick(k, MM_TILE_K, 128)
        o_spec = pl.BlockSpec((tm, tn), lambda i, j, kk: (i, j))
        if slot_cols is not None:
            assert n == N_DEV * slot_cols
            n_slots = max(s for s in (1, 2, 4, 8) if s == 1 or slot_cols * s <= MM_TILE)
            tn = slot_cols * n_slots
            o_spec = pl.BlockSpec((n_slots, tm, slot_cols), lambda i, j, kk: (j, i, 0))
            out_struct = jax.ShapeDtypeStruct((N_DEV, m, slot_cols), out_dtype)
        grid = (m // tm, n // tn, k // tk)
        a_spec = pl.BlockSpec((tk, tm), lambda i, j, kk: (kk, i))
        b_spec = pl.BlockSpec((tk, tn), lambda i, j, kk: (kk, j))
        dims = (((0,), (0,)), ((), ()))
    else:
        tm, tn, tk = _pick(m, MM_TILE, 128), _pick(n, MM_TILE, 128), _pick(k, MM_TILE_K, 128)
        grid = (n // tn, m // tm, k // tk)
        a_spec = pl.BlockSpec((tm, tk), lambda j, i, kk: (i, kk))
        o_spec = pl.BlockSpec((tm, tn), lambda j, i, kk: (i, j))
        if mode == 'nn':
            b_spec = pl.BlockSpec((tk, tn), lambda j, i, kk: (kk, j))
            dims = (((1,), (0,)), ((), ()))
        else:
            b_spec = pl.BlockSpec((tn, tk), lambda j, i, kk: (j, kk))
            dims = (((1,), (1,)), ((), ()))
    nk = grid[2]
    has_add = add is not None
    has_norm = norm_gain is not None
    assert not has_norm or (tn == n and mode != 'tn')

    def body(*refs):
        a_ref, b_ref = refs[0], refs[1]
        add_ref = refs[2] if has_add else None
        gain_ref = refs[2 + has_add] if has_norm else None
        o_ref = refs[2 + has_add + has_norm]
        norm_ref = refs[3 + has_add + has_norm] if has_norm else None
        part = lax.dot_general(a_ref[...].astype(BF16), b_ref[...].astype(BF16), dims, preferred_element_type=F32)

        def finish(total):
            if has_add:
                total = total + add_ref[...]
            if n_slots:
                for s in range(n_slots):
                    o_ref[s] = total[:, s * slot_cols:(s + 1) * slot_cols].astype(out_dtype)
            else:
                o_ref[...] = total.astype(out_dtype)
            if has_norm:
                y = total * lax.rsqrt(jnp.mean(total * total, axis=-1, keepdims=True) + EPS)
                norm_ref[...] = (y * gain_ref[...]).astype(BF16)

        if nk == 1:
            finish(part)
        else:
            acc_ref = refs[3 + has_add + 2 * has_norm]
            kk = pl.program_id(2)

            @pl.when(kk == 0)
            def _():
                acc_ref[...] = part

            @pl.when(kk > 0)
            def _():
                acc_ref[...] += part

            @pl.when(kk == nk - 1)
            def _():
                finish(acc_ref[...])

    in_specs = [a_spec, b_spec] + ([o_spec] if has_add else [])
    args = [a, b] + ([add] if has_add else [])
    out_specs, out_structs = [o_spec], [out_struct]
    if has_norm:
        in_specs.append(pl.BlockSpec((1, n), lambda j, i, kk: (0, 0)))
        args.append(norm_gain)
        out_specs.append(o_spec)
        out_structs.append(jax.ShapeDtypeStruct((m, n), BF16))
    scratch = [pltpu.VMEM((tm, tn), F32)] if nk > 1 else []
    outs, moved = _hosted_call(body, ex, name, grid, in_specs, out_specs, out_structs, scratch, args)
    out = outs[0] if not has_norm else tuple(outs)
    return out if ex is None else (out, moved)


def _rmsnorm_fwd(x, gains, name):
    t, d = x.shape
    tm = _pick(t, 512, 8)
    n = len(gains)

    def body(*refs):
        x_ref, g_refs, o_refs = refs[0], refs[1:1 + n], refs[1 + n:]
        xv = x_ref[...]
        y = xv * lax.rsqrt(jnp.mean(xv * xv, axis=-1, keepdims=True) + EPS)
        for g_ref, o_ref in zip(g_refs, o_refs):
            o_ref[...] = (y * g_ref[...]).astype(BF16)

    return pl.pallas_call(
        body, name=name, grid=(t // tm,),
        in_specs=[_rows(tm, d)] + [_const((1, d))] * n,
        out_specs=[_rows(tm, d)] * n,
        out_shape=[jax.ShapeDtypeStruct((t, d), BF16)] * n,
        compiler_params=_params(1),
    )(x, *gains)


def _ple_grads(dx, gl, pp):
    sg = _sig(gl)
    return (dx * pp * sg * (1.0 - sg)).astype(BF16), (dx * sg).astype(BF16)


def _rmsnorm_bwd(x, gains, dys, add, name, ple=None):
    t, d = x.shape
    tm = _pick(t, 512, 16)
    n = len(gains)
    n_ple = 0 if ple is None else 2

    def body(*refs):
        x_ref, add_ref = refs[0], refs[1]
        g_refs, dy_refs = refs[2:2 + n], refs[2 + n:2 + 2 * n]
        ple_refs = refs[2 + 2 * n:2 + 2 * n + n_ple]
        outs = refs[2 + 2 * n + n_ple:]
        dx_ref, dg_refs, dple_refs = outs[0], outs[1:1 + n], outs[1 + n:]
        i = pl.program_id(0)
        xv = x_ref[...]
        r = lax.rsqrt(jnp.mean(xv * xv, axis=-1, keepdims=True) + EPS)
        xhat = xv * r
        dx = add_ref[...]
        for g_ref, dy_ref, dg_ref in zip(g_refs, dy_refs, dg_refs):
            dy = dy_ref[...].astype(F32)
            dyg = dy * g_ref[...]
            dx = dx + r * (dyg - xhat * jnp.mean(dyg * xhat, axis=-1, keepdims=True))
            part = _colsum8(dy * xhat)

            @pl.when(i == 0)
            def _():
                dg_ref[...] = part

            @pl.when(i > 0)
            def _():
                dg_ref[...] += part

        dx_ref[...] = dx
        if n_ple:
            dple_refs[0][...], dple_refs[1][...] = _ple_grads(dx, ple_refs[0][...].astype(F32),
                                                               ple_refs[1][...].astype(F32))

    outs = pl.pallas_call(
        body, name=name, grid=(t // tm,),
        in_specs=[_rows(tm, d), _rows(tm, d)] + [_const((1, d))] * n + [_rows(tm, d)] * (n + n_ple),
        out_specs=[_rows(tm, d)] + [_const((V7X_SUBLANES, d))] * n + [_rows(tm, d)] * n_ple,
        out_shape=([jax.ShapeDtypeStruct((t, d), F32)] + [jax.ShapeDtypeStruct((V7X_SUBLANES, d), F32)] * n
                   + [jax.ShapeDtypeStruct((t, d), BF16)] * n_ple),
        compiler_params=_params(1),
    )(x, add, *gains, *dys, *(ple or ()))
    dgs = [o.sum(axis=0) for o in outs[1:1 + n]]
    return (outs[0], dgs) if ple is None else (outs[0], dgs, outs[1 + n], outs[2 + n])


def _ple_norm_fwd(h, gl, pp, gains, name):
    t, d = h.shape
    tm = _pick(t, 512, 16)
    n = len(gains)

    def body(*refs):
        h_ref, gl_ref, pp_ref = refs[:3]
        g_refs, x_ref, o_refs = refs[3:3 + n], refs[3 + n], refs[4 + n:]
        xv = h_ref[...] + _sig(gl_ref[...].astype(F32)) * pp_ref[...].astype(F32)
        x_ref[...] = xv
        y = xv * lax.rsqrt(jnp.mean(xv * xv, axis=-1, keepdims=True) + EPS)
        for g_ref, o_ref in zip(g_refs, o_refs):
            o_ref[...] = (y * g_ref[...]).astype(BF16)

    outs = pl.pallas_call(
        body, name=name, grid=(t // tm,),
        in_specs=[_rows(tm, d)] * 3 + [_const((1, d))] * n, out_specs=[_rows(tm, d)] * (1 + n),
        out_shape=[jax.ShapeDtypeStruct((t, d), F32)] + [jax.ShapeDtypeStruct((t, d), BF16)] * n,
        compiler_params=_params(1),
    )(h, gl, pp, *gains)
    return outs[0], outs[1:]


def _ple_loss(h, gl, pp, target, name):
    t, d = h.shape
    tm = _pick(t, 512, 16)
    inv_d = 1.0 / d

    def body(h_ref, gl_ref, pp_ref, t_ref, dy_ref, dgl_ref, dpp_ref, l_ref):
        i = pl.program_id(0)
        gl, pp = gl_ref[...].astype(F32), pp_ref[...].astype(F32)
        e = h_ref[...] + _sig(gl) * pp - t_ref[...]
        dy = e * inv_d
        dy_ref[...] = dy
        dgl_ref[...], dpp_ref[...] = _ple_grads(dy, gl, pp)
        part = _colsum8(e * e) * (0.5 * inv_d)

        @pl.when(i == 0)
        def _():
            l_ref[...] = part

        @pl.when(i > 0)
        def _():
            l_ref[...] += part

    return pl.pallas_call(
        body, name=name, grid=(t // tm,), in_specs=[_rows(tm, d)] * 4,
        out_specs=[_rows(tm, d)] * 3 + [_const((V7X_SUBLANES, d))],
        out_shape=[jax.ShapeDtypeStruct((t, d), F32), jax.ShapeDtypeStruct((t, d), BF16),
                   jax.ShapeDtypeStruct((t, d), BF16), jax.ShapeDtypeStruct((V7X_SUBLANES, d), F32)],
        compiler_params=_params(1),
    )(h, gl, pp, target)


def _shift_scratch(ts, cc):
    return pltpu.VMEM((V7X_SUBLANES, ts + HALO - V7X_SUBLANES, cc), F32)


def _shifted_copies(sh_ref, win_ref, cs, ts):
    rows = ts + HALO - V7X_SUBLANES
    for s in range(1, V7X_SUBLANES):
        sh_ref[s] = win_ref[pl.ds(s, rows), cs]


def _tap(sh_ref, win_ref, cs, offset, rows, r0):
    s = offset % V7X_SUBLANES
    start = pl.multiple_of(r0 + (offset - s), V7X_SUBLANES)
    if s == 0:
        return win_ref[pl.ds(start, rows), cs]
    return sh_ref[s, pl.ds(start, rows), :]


def _conv_fwd(proj, conv_w, conv_b, ln_g, ln_b, seq, name, ex=None):
    t, c3 = proj.shape
    c = c3 // 3
    ts = _pick(seq, 256, HALO)
    nsb = seq // ts
    cc = _pick(c, 512, V7X_LANES)
    hb = ts // HALO

    def body(a_ref, b_ref, z_ref, ap_ref, bp_ref, w_ref, cb_ref, g_ref, be_ref, m_ref, y_ref, win_ref, sh_ref):
        i = pl.program_id(0)
        first = (i % nsb) == 0
        win_ref[0:HALO, :] = jnp.where(first, 0.0, ap_ref[...] * _sig(bp_ref[...]))
        win_ref[HALO:, :] = a_ref[...] * _sig(b_ref[...])
        for ci in range(c // cc):
            cs = slice(ci * cc, (ci + 1) * cc)
            _shifted_copies(sh_ref, win_ref, cs, ts)

            def out_rows(rb, carry, cs=cs):
                r0 = rb * CONV_ROWS
                acc = jnp.zeros((CONV_ROWS, cc), F32) + cb_ref[:, cs]
                for k in range(CONV_WIDTH):
                    acc = acc + w_ref[k:k + 1, cs] * _tap(sh_ref, win_ref, cs, HALO - (CONV_WIDTH - 1) + k,
                                                           CONV_ROWS, r0)
                y_ref[pl.ds(pl.multiple_of(r0, CONV_ROWS), CONV_ROWS), cs] = acc
                return carry

            lax.fori_loop(0, ts // CONV_ROWS, out_rows, 0, unroll=2)
        y = y_ref[...]
        mu = jnp.mean(y, axis=-1, keepdims=True)
        xc = y - mu
        rstd = lax.rsqrt(jnp.mean(xc * xc, axis=-1, keepdims=True) + EPS)
        ln = xc * rstd * g_ref[...] + be_ref[...]
        zz = z_ref[...]
        m_ref[...] = (ln * _sig(ln) * zz * _sig(zz)).astype(BF16)

    halo_a = pl.BlockSpec((HALO, c), lambda i: (jnp.maximum(i * hb - 1, 0), 0))
    halo_b = pl.BlockSpec((HALO, c), lambda i: (jnp.maximum(i * hb - 1, 0), 1))
    (m_act, y), moved = _hosted_call(
        body, ex, name, (t // ts,),
        [_rows(ts, c, 0), _rows(ts, c, 1), _rows(ts, c, 2), halo_a, halo_b,
         _const((CONV_WIDTH, c)), _const((1, c)), _const((1, c)), _const((1, c))],
        [_rows(ts, c), _rows(ts, c)],
        [jax.ShapeDtypeStruct((t, c), BF16), jax.ShapeDtypeStruct((t, c), F32)],
        [pltpu.VMEM((HALO + ts, c), F32), _shift_scratch(ts, cc)],
        (proj, proj, proj, proj, proj, conv_w, conv_b, ln_g, ln_b))
    return m_act, y, moved


def _ln_gate_bwd(dm, y, proj, ln_g, ln_b, name):
    t, c = y.shape
    tm = _pick(t, 256, 8)

    def body(dm_ref, y_ref, z_ref, g_ref, be_ref, dy_ref, dz_ref, dg_ref, db_ref, dcb_ref):
        i = pl.program_id(0)
        yv = y_ref[...]
        mu = jnp.mean(yv, axis=-1, keepdims=True)
        xc = yv - mu
        rstd = lax.rsqrt(jnp.mean(xc * xc, axis=-1, keepdims=True) + EPS)
        xhat = xc * rstd
        g = g_ref[...]
        ln = xhat * g + be_ref[...]
        sl = _sig(ln)
        zz = z_ref[...]
        sz = _sig(zz)
        dmv = dm_ref[...].astype(F32)
        dz_ref[...] = (dmv * (ln * sl) * (sz * (1.0 + zz * (1.0 - sz)))).astype(BF16)
        dln = dmv * (zz * sz) * (sl * (1.0 + ln * (1.0 - sl)))
        dxh = dln * g
        dyv = rstd * (dxh - jnp.mean(dxh, axis=-1, keepdims=True)
                      - xhat * jnp.mean(dxh * xhat, axis=-1, keepdims=True))
        dy_ref[...] = dyv
        parts = (_colsum8(dln * xhat), _colsum8(dln), _colsum8(dyv))

        @pl.when(i == 0)
        def _():
            for ref, part in zip((dg_ref, db_ref, dcb_ref), parts):
                ref[...] = part

        @pl.when(i > 0)
        def _():
            for ref, part in zip((dg_ref, db_ref, dcb_ref), parts):
                ref[...] += part

    acc = jax.ShapeDtypeStruct((V7X_SUBLANES, c), F32)
    outs = pl.pallas_call(
        body, name=name, grid=(t // tm,),
        in_specs=[_rows(tm, c), _rows(tm, c), _rows(tm, c, 2), _const((1, c)), _const((1, c))],
        out_specs=[_rows(tm, c), _rows(tm, c)] + [_const((V7X_SUBLANES, c))] * 3,
        out_shape=[jax.ShapeDtypeStruct((t, c), F32), jax.ShapeDtypeStruct((t, c), BF16), acc, acc, acc],
        compiler_params=_params(1),
    )(dm, y, proj, ln_g, ln_b)
    return outs[0], outs[1], outs[2].sum(axis=0), outs[3].sum(axis=0), outs[4].sum(axis=0)


def _conv_bwd(dy, dz, proj, conv_w, seq, name, ex=None):
    t, c3 = proj.shape
    c = c3 // 3
    ts = _pick(seq, 256, HALO)
    nsb = seq // ts
    cc = _pick(c, 512, V7X_LANES)
    hb = ts // HALO
    last_halo = t // HALO - 1
    back = CONV_WIDTH - 1

    def body(dy_ref, dyn_ref, dz_ref, a_ref, b_ref, ap_ref, bp_ref, w_ref, o_ref, dw_ref, win_ref, dwin_ref,
             sh_ref, dsh_ref):
        i = pl.program_id(0)
        first = (i % nsb) == 0
        last = (i % nsb) == nsb - 1
        win_ref[0:HALO, :] = jnp.where(first, 0.0, ap_ref[...] * _sig(bp_ref[...]))
        win_ref[HALO:, :] = a_ref[...] * _sig(b_ref[...])
        dwin_ref[0:ts, :] = dy_ref[...]
        dwin_ref[ts:, :] = jnp.where(last, 0.0, dyn_ref[...])

        @pl.when(i == 0)
        def _():
            dw_ref[...] = jnp.zeros_like(dw_ref)

        for ci in range(c // cc):
            cs = slice(ci * cc, (ci + 1) * cc)
            _shifted_copies(sh_ref, win_ref, cs, ts)
            _shifted_copies(dsh_ref, dwin_ref, cs, ts)

            def in_grad_rows(rb, carry, cs=cs, ci=ci):
                r0 = rb * CONV_ROWS
                rows = pl.ds(pl.multiple_of(r0, CONV_ROWS), CONV_ROWS)
                dglu = jnp.zeros((CONV_ROWS, cc), F32)
                for k in range(CONV_WIDTH):
                    dglu = dglu + w_ref[k:k + 1, cs] * _tap(dsh_ref, dwin_ref, cs, back - k, CONV_ROWS, r0)
                sbc = _sig(b_ref[rows, cs])
                o_ref[rows, cs] = (dglu * sbc).astype(BF16)
                o_ref[rows, c + ci * cc:c + (ci + 1) * cc] = (dglu * a_ref[rows, cs] * sbc * (1.0 - sbc)).astype(BF16)
                return carry

            def w_grad_rows(rb, carry, cs=cs):
                r0 = rb * CONV_W_ROWS
                dcur = dwin_ref[pl.ds(pl.multiple_of(r0, CONV_W_ROWS), CONV_W_ROWS), cs]
                for k in range(CONV_WIDTH):
                    dw_ref[k * V7X_SUBLANES:(k + 1) * V7X_SUBLANES, cs] += _colsum8(
                        dcur * _tap(sh_ref, win_ref, cs, HALO - back + k, CONV_W_ROWS, r0))
                return carry

            lax.fori_loop(0, ts // CONV_ROWS, in_grad_rows, 0, unroll=2)
            lax.fori_loop(0, ts // CONV_W_ROWS, w_grad_rows, 0)
        o_ref[:, 2 * c:] = dz_ref[...]

    halo_next = pl.BlockSpec((HALO, c), lambda i: (jnp.minimum((i + 1) * hb, last_halo), 0))
    halo_a = pl.BlockSpec((HALO, c), lambda i: (jnp.maximum(i * hb - 1, 0), 0))
    halo_b = pl.BlockSpec((HALO, c), lambda i: (jnp.maximum(i * hb - 1, 0), 1))
    (dproj, dw), moved = _hosted_call(
        body, ex, name, (t // ts,),
        [_rows(ts, c), halo_next, _rows(ts, c), _rows(ts, c, 0), _rows(ts, c, 1), halo_a, halo_b,
         _const((CONV_WIDTH, c))],
        [_rows(ts, c3), _const((CONV_WIDTH * V7X_SUBLANES, c))],
        [jax.ShapeDtypeStruct((t, c3), BF16), jax.ShapeDtypeStruct((CONV_WIDTH * V7X_SUBLANES, c), F32)],
        [pltpu.VMEM((HALO + ts, c), F32), pltpu.VMEM((ts + HALO, c), F32),
         _shift_scratch(ts, cc), _shift_scratch(ts, cc)],
        (dy, dy, dz, proj, proj, proj, proj, conv_w))
    return dproj, dw.reshape(CONV_WIDTH, V7X_SUBLANES, c).sum(axis=1), moved


def _rope_tables(seq):
    half = ROPE_DIM // 2
    inv = ROPE_THETA ** (-jnp.arange(half, dtype=F32) * (2.0 / ROPE_DIM))
    ang = jnp.arange(seq).astype(F32)[:, None] * inv[None, :]
    cos, sin = jnp.cos(ang), jnp.sin(ang)
    zeros = jnp.zeros((seq, HEAD_DIM - ROPE_DIM), F32)
    zh = jnp.zeros((seq, half), F32)
    a = jnp.concatenate([cos, cos, zeros + 1.0], axis=1)
    b = jnp.concatenate([zh, sin, zeros], axis=1)
    c = jnp.concatenate([-sin, zh, zeros], axis=1)
    rep = V7X_LANES // HEAD_DIM
    return tuple(jnp.tile(v, (1, rep)) for v in (a, b, c))


def _head_ones(d):
    head = jnp.arange(d) // HEAD_DIM
    return (head[:, None] == head[None, :]).astype(BF16)


def _rope(ch, ta, tb, tc):
    return ta * ch + tb * pltpu.roll(ch, ROPE_DIM // 2, 1) + tc * pltpu.roll(ch, V7X_LANES - ROPE_DIM // 2, 1)


def _rope_t(ch, ta, tb, tc):
    return ta * ch + pltpu.roll(tb * ch, V7X_LANES - ROPE_DIM // 2, 1) + pltpu.roll(tc * ch, ROPE_DIM // 2, 1)


def _norm_rope_bwd(xhat, r, dout, gain, ta, tb, tc, e_ref):
    dxn = _rope_t(dout, ta, tb, tc)
    dxh = dxn * gain
    dx = r * (dxh - xhat * (_segsum(dxh * xhat, e_ref) * (1.0 / HEAD_DIM)))
    return dx, _colsum8(dxn * xhat)


def _norm_rope_rows(dst_ref, src_ref, gain, ta_ref, tb_ref, tc_ref, e_ref, seq, xhat_ref=None, r_ref=None):
    for r0 in range(0, seq, ATTN_PIECE):
        rows = slice(r0, r0 + ATTN_PIECE)
        xv = src_ref[rows, :]
        r = lax.rsqrt(_segsum(xv * xv, e_ref) * (1.0 / HEAD_DIM) + EPS)
        xhat = xv * r
        if xhat_ref is not None:
            xhat_ref[rows, :] = xhat
            r_ref[rows, :] = r
        dst_ref[rows, :] = _rope(xhat * gain, ta_ref[rows, :], tb_ref[rows, :], tc_ref[rows, :])


ATTN_PIECE = 256
ATTN_UNROLL = 16
CHUNK_UNROLL = 4


def _pieces(dil, seq):
    length = seq // dil
    rows = min(length, ATTN_PIECE)
    return [(r + dil * ci * rows, r * length + ci * rows, rows) for r in range(dil) for ci in range(length // rows)]


def _strided(ref, start, rows, dil):
    if dil == 1:
        return ref[pl.ds(start, rows), :]
    return ref[pl.ds(start, rows, stride=dil), :]


def _strided_set(ref, start, rows, dil, val):
    if dil == 1:
        ref[pl.ds(start, rows), :] = val
    else:
        ref[pl.ds(start, rows, stride=dil), :] = val


def _nt(a, b):
    return lax.dot_general(a, b, (((1,), (1,)), ((), ())), preferred_element_type=F32)


def _tn(a, b):
    return lax.dot_general(a, b, (((0,), (0,)), ((), ())), preferred_element_type=F32)


def _set_bias(bias_ref):
    qi = lax.broadcasted_iota(jnp.int32, (2 * SPAN, 2 * SPAN), 0) & (SPAN - 1)
    kj = lax.broadcasted_iota(jnp.int32, (2 * SPAN, 2 * SPAN), 1)
    band = jnp.logical_and(kj >= qi, (kj - SPAN) <= qi)
    bias_ref[1] = jnp.where(band, 0.0, NEG_INF)
    bias_ref[0] = jnp.where(jnp.logical_and(band, kj >= SPAN), 0.0, NEG_INF)


def _block_keys(bias_ref, j, qs, nb):
    if nb == 1:
        return pl.ds(pl.multiple_of(qs + SPAN, SPAN), SPAN), bias_ref[1, :, SPAN:]
    return pl.ds(qs, 2 * SPAN), bias_ref[jnp.minimum(j & (nb - 1), 1)]


def _stack_heads(v, head0):
    zero = jnp.zeros_like(v)
    return jnp.concatenate([jnp.where(head0, v, zero), jnp.where(head0, zero, v)], axis=0)


def _unstack_heads(v2, head0):
    return jnp.where(head0, v2[:SPAN], v2[SPAN:])


def _head_cols(v):
    return jnp.concatenate([v[:, 0:1], v[:, HEAD_DIM:HEAD_DIM + 1]], axis=0)


def _attn_fwd(proj_b, kv, gains, tables, ones, bsz, seq, name):
    t, d4 = proj_b.shape
    d = d4 // 4
    nhp = d // V7X_LANES
    nblk = seq // SPAN
    scale = HEAD_DIM ** -0.5
    n_groups = len(DILATIONS)

    def body(q0_ref, q1_ref, q2_ref, k_ref, v_ref, gate_ref, gain_ref, ta_ref, tb_ref, tc_ref, e_ref,
             o_ref, l_ref, ao_ref, qd, kd, vd, od, ld, on0, on1, on2, ln0, ln1, ln2, kn, qn, bias):
        head0 = lax.broadcasted_iota(jnp.int32, (SPAN, V7X_LANES), 1) < HEAD_DIM

        @pl.when(jnp.logical_and(pl.program_id(0) == 0, pl.program_id(1) == 0))
        def _():
            _set_bias(bias)

        _norm_rope_rows(kn, k_ref, gain_ref[n_groups:n_groups + 1, :], ta_ref, tb_ref, tc_ref, e_ref, seq)
        kd[0:SPAN, :] = jnp.zeros((SPAN, V7X_LANES), BF16)
        vd[0:SPAN, :] = jnp.zeros((SPAN, V7X_LANES), BF16)
        for g, (q_ref, on, ln) in enumerate(((q0_ref, on0, ln0), (q1_ref, on1, ln1), (q2_ref, on2, ln2))):
            dil = DILATIONS[g]
            nb = seq // dil // SPAN
            _norm_rope_rows(qn, q_ref, gain_ref[g:g + 1, :], ta_ref, tb_ref, tc_ref, e_ref, seq)
            for ns, rs, rows in _pieces(dil, seq):
                qd[rs:rs + rows, :] = _strided(qn, ns, rows, dil).astype(BF16)
                kd[SPAN + rs:SPAN + rs + rows, :] = _strided(kn, ns, rows, dil).astype(BF16)
                vd[SPAN + rs:SPAN + rs + rows, :] = _strided(v_ref, ns, rows, dil).astype(BF16)

            def block(j, carry):
                qs = pl.multiple_of(j * SPAN, SPAN)
                q2 = _stack_heads(qd[pl.ds(qs, SPAN), :], head0)
                keys, mask = _block_keys(bias, j, qs, nb)
                kk = kd[keys, :]
                vv = vd[keys, :]
                s = _nt(q2, kk) * scale + mask
                mx = jnp.max(s, axis=1, keepdims=True)
                p = jnp.exp(s - mx)
                den = jnp.sum(p, axis=1, keepdims=True)
                o2 = jnp.dot(p.astype(BF16), vv, preferred_element_type=F32) / den
                l2 = jnp.broadcast_to(mx + jnp.log(den), (2 * SPAN, V7X_LANES))
                od[pl.ds(qs, SPAN), :] = _unstack_heads(o2, head0)
                ld[pl.ds(qs, SPAN), :] = _unstack_heads(l2, head0)
                return carry

            lax.fori_loop(0, nblk, block, 0, unroll=ATTN_UNROLL)
            for ns, rs, rows in _pieces(dil, seq):
                _strided_set(on, ns, rows, dil, od[rs:rs + rows, :])
                _strided_set(ln, ns, rows, dil, ld[rs:rs + rows, :])

        def merge(ci, carry):
            rows = pl.ds(pl.multiple_of(ci * ATTN_PIECE, ATTN_PIECE), ATTN_PIECE)
            ls = [ln0[rows, :], ln1[rows, :], ln2[rows, :]]
            mx = jnp.maximum(jnp.maximum(ls[0], ls[1]), ls[2])
            es = [jnp.exp(v - mx) for v in ls]
            den = es[0] + es[1] + es[2]
            ov = (es[0] * on0[rows, :] + es[1] * on1[rows, :] + es[2] * on2[rows, :]) / den
            gate = gate_ref[rows, :]
            o_ref[rows, :] = ov
            l_ref[rows, :] = mx + jnp.log(den)
            ao_ref[rows, :] = (ov * gate * _sig(gate)).astype(BF16)
            return carry

        lax.fori_loop(0, seq // ATTN_PIECE, merge, 0)

    blk = (None, seq, V7X_LANES)
    pview = proj_b.reshape(bsz, seq, d4)
    kview = kv.reshape(bsz, seq, 2 * d)
    out_spec = pl.BlockSpec(blk, lambda b, h: (b, 0, h))
    tab = pl.BlockSpec((seq, V7X_LANES), lambda b, h: (0, 0))
    nat = pltpu.VMEM((seq, V7X_LANES), F32)
    o, lse, ao = pl.pallas_call(
        body, name=name, grid=(bsz, nhp),
        in_specs=[pl.BlockSpec(blk, lambda b, h: (b, 0, h)),
                  pl.BlockSpec(blk, lambda b, h: (b, 0, nhp + h)),
                  pl.BlockSpec(blk, lambda b, h: (b, 0, 2 * nhp + h)),
                  pl.BlockSpec(blk, lambda b, h: (b, 0, h)),
                  pl.BlockSpec(blk, lambda b, h: (b, 0, nhp + h)),
                  pl.BlockSpec(blk, lambda b, h: (b, 0, 3 * nhp + h)),
                  pl.BlockSpec((n_groups + 1, V7X_LANES), lambda b, h: (0, 0)),
                  tab, tab, tab,
                  pl.BlockSpec((V7X_LANES, V7X_LANES), lambda b, h: (0, 0))],
        out_specs=[out_spec, out_spec, out_spec],
        out_shape=[jax.ShapeDtypeStruct((bsz, seq, d), F32), jax.ShapeDtypeStruct((bsz, seq, d), F32),
                   jax.ShapeDtypeStruct((bsz, seq, d), BF16)],
        scratch_shapes=[pltpu.VMEM((seq, V7X_LANES), BF16), pltpu.VMEM((SPAN + seq, V7X_LANES), BF16),
                        pltpu.VMEM((SPAN + seq, V7X_LANES), BF16), nat, nat, nat, nat, nat, nat, nat, nat, nat, nat,
                        pltpu.VMEM((2, 2 * SPAN, 2 * SPAN), F32)],
        compiler_params=_params(2),
    )(pview, pview, pview, kview, kview, pview, gains, *tables, ones)
    return o.reshape(t, d), lse.reshape(t, d), ao.reshape(t, d)


def _attn_bwd(proj_b, kv, dao, o, lse, gains, tables, ones, bsz, seq, name):
    t, d4 = proj_b.shape
    d = d4 // 4
    nhp = d // V7X_LANES
    nblk = seq // SPAN
    scale = HEAD_DIM ** -0.5
    n_groups = len(DILATIONS)
    n_chunks = seq // ATTN_PIECE

    def body(q_ref, k_ref, v_ref, gate_ref, dao_ref, o_ref, l_ref, gain_ref, ta_ref, tb_ref, tc_ref, e_ref,
             dproj_ref, dkv_ref, dg_ref, qd, kd, vd, dod, ld, deld, dqd, dkd, dvd, dqn, dkn, dvn, kn, kxh, krr,
             qn, qxh, qrr, don, deln, bias):
        head0 = lax.broadcasted_iota(jnp.int32, (SPAN, V7X_LANES), 1) < HEAD_DIM
        g = pl.program_id(2)

        @pl.when(jnp.logical_and(jnp.logical_and(pl.program_id(0) == 0, pl.program_id(1) == 0), g == 0))
        def _():
            _set_bias(bias)
            dg_ref[...] = jnp.zeros_like(dg_ref)

        @pl.when(g == 0)
        def _():
            dkn[...] = jnp.zeros_like(dkn)
            dvn[...] = jnp.zeros_like(dvn)
            _norm_rope_rows(kn, k_ref, gain_ref[n_groups:n_groups + 1, :], ta_ref, tb_ref, tc_ref, e_ref, seq,
                            kxh, krr)
            for r0 in range(0, seq, ATTN_PIECE):
                rows = slice(r0, r0 + ATTN_PIECE)
                gate = gate_ref[rows, :]
                dov = dao_ref[rows, :].astype(F32) * gate * _sig(gate)
                don[rows, :] = dov
                deln[rows, :] = _segsum(dov * o_ref[rows, :], e_ref)

        def norm_bwd_chunks(xhat_ref, r_ref, dn_ref, out_ref, gi):
            def chunk(ci, carry):
                rows = pl.ds(pl.multiple_of(ci * ATTN_PIECE, ATTN_PIECE), ATTN_PIECE)
                dx, part = _norm_rope_bwd(xhat_ref[rows, :], r_ref[rows, :], dn_ref[rows, :], gain_ref[gi:gi + 1, :],
                                          ta_ref[rows, :], tb_ref[rows, :], tc_ref[rows, :], e_ref)
                out_ref[rows, :] = dx.astype(BF16)
                dg_ref[gi] += part
                return carry
            lax.fori_loop(0, n_chunks, chunk, 0, unroll=CHUNK_UNROLL)

        def group(gi):
            dil = DILATIONS[gi]
            nb = seq // dil // SPAN
            kd[0:SPAN, :] = jnp.zeros((SPAN, V7X_LANES), BF16)
            vd[0:SPAN, :] = jnp.zeros((SPAN, V7X_LANES), BF16)
            dkd[...] = jnp.zeros_like(dkd)
            dvd[...] = jnp.zeros_like(dvd)
            _norm_rope_rows(qn, q_ref, gain_ref[gi:gi + 1, :], ta_ref, tb_ref, tc_ref, e_ref, seq, qxh, qrr)
            for ns, rs, rows in _pieces(dil, seq):
                qd[rs:rs + rows, :] = _strided(qn, ns, rows, dil).astype(BF16)
                kd[SPAN + rs:SPAN + rs + rows, :] = _strided(kn, ns, rows, dil).astype(BF16)
                vd[SPAN + rs:SPAN + rs + rows, :] = _strided(v_ref, ns, rows, dil).astype(BF16)
                dod[rs:rs + rows, :] = _strided(don, ns, rows, dil).astype(BF16)
                deld[rs:rs + rows, :] = _strided(deln, ns, rows, dil)
                ld[rs:rs + rows, :] = _strided(l_ref, ns, rows, dil)

            def block(j, carry):
                qs = pl.multiple_of(j * SPAN, SPAN)
                q2 = _stack_heads(qd[pl.ds(qs, SPAN), :], head0)
                do2 = _stack_heads(dod[pl.ds(qs, SPAN), :], head0)
                keys, mask = _block_keys(bias, j, qs, nb)
                kk = kd[keys, :]
                vv = vd[keys, :]
                s = _nt(q2, kk) * scale + mask
                p = jnp.exp(s - _head_cols(ld[pl.ds(qs, SPAN), :]))
                ds = (p * (_nt(do2, vv) - _head_cols(deld[pl.ds(qs, SPAN), :])) * scale).astype(BF16)
                dqd[pl.ds(qs, SPAN), :] = _unstack_heads(jnp.dot(ds, kk, preferred_element_type=F32), head0)
                dkd[keys, :] += _tn(ds, q2)
                dvd[keys, :] += _tn(p.astype(BF16), do2)
                return carry

            lax.fori_loop(0, nblk, block, 0, unroll=ATTN_UNROLL)
            for ns, rs, rows in _pieces(dil, seq):
                _strided_set(dqn, ns, rows, dil, dqd[rs:rs + rows, :])
                _strided_set(dkn, ns, rows, dil,
                             _strided(dkn, ns, rows, dil) + dkd[SPAN + rs:SPAN + rs + rows, :])
                _strided_set(dvn, ns, rows, dil,
                             _strided(dvn, ns, rows, dil) + dvd[SPAN + rs:SPAN + rs + rows, :])
            norm_bwd_chunks(qxh, qrr, dqn, dproj_ref, gi)

        for gi in range(n_groups):
            @pl.when(g == gi)
            def _():
                group(gi)

        @pl.when(g == n_groups - 1)
        def _():
            norm_bwd_chunks(kxh, krr, dkn, dkv_ref, n_groups)

        @pl.when(g == n_groups)
        def _():
            def chunk(ci, carry):
                rows = pl.ds(pl.multiple_of(ci * ATTN_PIECE, ATTN_PIECE), ATTN_PIECE)
                gate = gate_ref[rows, :]
                sg = _sig(gate)
                dproj_ref[rows, :] = (dao_ref[rows, :].astype(F32) * o_ref[rows, :]
                                      * (sg * (1.0 + gate * (1.0 - sg)))).astype(BF16)
                dkv_ref[rows, :] = dvn[rows, :].astype(BF16)
                return carry
            lax.fori_loop(0, n_chunks, chunk, 0, unroll=CHUNK_UNROLL)

    blk = (None, seq, V7X_LANES)
    pview = proj_b.reshape(bsz, seq, d4)
    kview = kv.reshape(bsz, seq, 2 * d)
    dview = (bsz, seq, d)
    d_spec = pl.BlockSpec(blk, lambda b, h, g: (b, 0, h))
    tab = pl.BlockSpec((seq, V7X_LANES), lambda b, h, g: (0, 0))
    nat = pltpu.VMEM((seq, V7X_LANES), F32)
    natb = pltpu.VMEM((seq, V7X_LANES), BF16)
    pad = pltpu.VMEM((SPAN + seq, V7X_LANES), F32)
    padb = pltpu.VMEM((SPAN + seq, V7X_LANES), BF16)
    dproj, dkv, dg = pl.pallas_call(
        body, name=name, grid=(bsz, nhp, n_groups + 1),
        in_specs=[pl.BlockSpec(blk, lambda b, h, g: (b, 0, jnp.minimum(g, n_groups - 1) * nhp + h)),
                  pl.BlockSpec(blk, lambda b, h, g: (b, 0, h)),
                  pl.BlockSpec(blk, lambda b, h, g: (b, 0, nhp + h)),
                  pl.BlockSpec(blk, lambda b, h, g: (b, 0, n_groups * nhp + h)),
                  d_spec, d_spec, d_spec,
                  pl.BlockSpec((n_groups + 1, V7X_LANES), lambda b, h, g: (0, 0)),
                  tab, tab, tab,
                  pl.BlockSpec((V7X_LANES, V7X_LANES), lambda b, h, g: (0, 0))],
        out_specs=[pl.BlockSpec(blk, lambda b, h, g: (b, 0, g * nhp + h)),
                   pl.BlockSpec(blk, lambda b, h, g: (b, 0, (g // n_groups) * nhp + h)),
                   pl.BlockSpec((n_groups + 1, V7X_SUBLANES, V7X_LANES), lambda b, h, g: (0, 0, 0))],
        out_shape=[jax.ShapeDtypeStruct((bsz, seq, d4), BF16), jax.ShapeDtypeStruct((bsz, seq, 2 * d), BF16),
                   jax.ShapeDtypeStruct((n_groups + 1, V7X_SUBLANES, V7X_LANES), F32)],
        scratch_shapes=[natb, padb, padb, natb, nat, nat, nat, pad, pad, nat, nat, nat, nat, nat, nat,
                        nat, nat, nat, nat, nat,
                        pltpu.VMEM((2, 2 * SPAN, 2 * SPAN), F32)],
        compiler_params=_params(3),
    )(pview, kview, kview, pview, dao.reshape(dview), o.reshape(dview), lse.reshape(dview), gains, *tables, ones)
    dgain = dg.sum(axis=1).reshape(n_groups + 1, V7X_LANES // HEAD_DIM, HEAD_DIM).sum(axis=1)
    return dproj.reshape(t, d4), dkv.reshape(t, 2 * d), dgain


def _mesh_position():
    x, y, c = lax.axis_index("x"), lax.axis_index("y"), lax.axis_index("c")
    return x, y, c


def _peer(x, y, c, rel):
    return (1 - x if rel & 4 else x, 1 - y if rel & 2 else y, 1 - c if rel & 1 else c)


class _Exchange:
    def __init__(self, srcs, gather):
        self.srcs = list(srcs)
        self.gather = gather
        n = self.n = len(self.srcs)
        hbm = pl.BlockSpec(memory_space=pltpu.HBM)
        self.in_specs = [hbm] * n
        self.out_specs = [hbm] * n
        self.out_shape = [jax.ShapeDtypeStruct(((N_DEV,) + a.shape) if gather else a.shape, a.dtype)
                          for a in self.srcs]
        self.scratch = [pltpu.SemaphoreType.DMA((n * (N_DEV - 1),)), pltpu.SemaphoreType.DMA((n * (N_DEV - 1),)),
                        pltpu.SemaphoreType.DMA((n,))]

    def _copies(self, ins, outs, sems):
        send_sems, recv_sems, local_sems = sems
        x, y, c = _mesh_position()
        me = 4 * x + 2 * y + c
        remote, local = [], []
        for a in range(self.n):
            mine = ins[a] if self.gather else ins[a].at[me]
            local.append(pltpu.make_async_copy(mine, outs[a].at[me], local_sems.at[a]))
            for rel in range(1, N_DEV):
                px, py, pc = _peer(x, y, c, rel)
                s = a * (N_DEV - 1) + rel - 1
                src = ins[a] if self.gather else ins[a].at[4 * px + 2 * py + pc]
                remote.append(pltpu.make_async_remote_copy(
                    src_ref=src, dst_ref=outs[a].at[me], send_sem=send_sems.at[s], recv_sem=recv_sems.at[s],
                    device_id=(px, py, pc), device_id_type=pl.DeviceIdType.MESH))
        return remote, local

    def start(self, ins, outs, sems):
        remote, local = self._copies(ins, outs, sems)
        for cp in local + remote:
            cp.start()

    def wait(self, ins, outs, sems):
        remote, local = self._copies(ins, outs, sems)
        for cp in remote:
            cp.wait_recv()
        for cp in remote:
            cp.wait_send()
        for cp in local:
            cp.wait()


def _gather_chip_once(arrs, name):
    n = len(arrs)
    per = N_DEV - 1

    def body(*refs):
        ins, outs = refs[:n], refs[n:2 * n]
        send_sems, recv_sems, local_sems = refs[2 * n:]
        x, y, c = _mesh_position()
        me, sibling = (x, y, c), (x, y, 1 - c)
        chips = [(1 - x, y), (x, 1 - y), (1 - x, 1 - y)]

        def copy(a, k, block, to, src=None):
            bx, by, bc = block
            dst = outs[a].at[4 * bx + 2 * by + bc]
            return pltpu.make_async_remote_copy(
                src_ref=dst if src is None else src, dst_ref=dst, send_sem=send_sems.at[a * per + k],
                recv_sem=recv_sems.at[a * per + k], device_id=to, device_id_type=pl.DeviceIdType.MESH)

        local, sent = [], []
        for a in range(n):
            mine = pltpu.make_async_copy(ins[a], outs[a].at[4 * x + 2 * y + c], local_sems.at[a])
            mine.start()
            local.append(mine)
            first = [copy(a, 0, me, sibling, src=ins[a])]
            first += [copy(a, 1 + j, me, chip + (c,), src=ins[a]) for j, chip in enumerate(chips)]
            for cp in first:
                cp.start()
            sent += first
        for a in range(n):
            for j, chip in enumerate(chips):
                copy(a, 1 + j, chip + (c,), me).wait_recv()
                passed = copy(a, 4 + j, chip + (c,), sibling)
                passed.start()
                sent.append(passed)
        for a in range(n):
            copy(a, 0, sibling, me).wait_recv()
            for j, chip in enumerate(chips):
                copy(a, 4 + j, chip + (1 - c,), me).wait_recv()
        for cp in sent:
            cp.wait_send()
        for cp in local:
            cp.wait()

    hbm = pl.BlockSpec(memory_space=pltpu.HBM)
    return pl.pallas_call(
        body, name=name, in_specs=[hbm] * n, out_specs=[hbm] * n,
        out_shape=[jax.ShapeDtypeStruct((N_DEV,) + a.shape, a.dtype) for a in arrs],
        scratch_shapes=[pltpu.SemaphoreType.DMA((n * per,)), pltpu.SemaphoreType.DMA((n * per,)),
                        pltpu.SemaphoreType.DMA((n,))],
    )(*arrs)


def _run_exchange(ex, name):
    n = ex.n

    def body(*refs):
        ins, outs, sems = refs[:n], refs[n:2 * n], refs[2 * n:]
        ex.start(ins, outs, sems)
        ex.wait(ins, outs, sems)

    return pl.pallas_call(body, name=name, in_specs=ex.in_specs, out_specs=ex.out_specs, out_shape=ex.out_shape,
                          scratch_shapes=ex.scratch)(*ex.srcs)


def _hosted_call(body, ex, name, grid, in_specs, out_specs, out_shape, scratch_shapes, args):
    if ex is None:
        outs = pl.pallas_call(body, name=name, grid=grid, in_specs=in_specs, out_specs=out_specs, out_shape=out_shape,
                              scratch_shapes=scratch_shapes, compiler_params=_params(len(grid)))(*args)
        return list(outs), []
    n_in, n_out, n_scr, n = len(in_specs), len(out_specs), len(scratch_shapes), ex.n

    def hosted(*refs):
        h_in, e_in = refs[:n_in], refs[n_in:n_in + n]
        o0 = n_in + n
        h_out, e_out = refs[o0:o0 + n_out], refs[o0 + n_out:o0 + n_out + n]
        s0 = o0 + n_out + n
        h_scr, e_scr = refs[s0:s0 + n_scr], refs[s0 + n_scr:]
        ids = [pl.program_id(a) for a in range(len(grid))]
        first = functools.reduce(jnp.logical_and, [i == 0 for i in ids])
        last = functools.reduce(jnp.logical_and, [i == g - 1 for i, g in zip(ids, grid)])

        @pl.when(first)
        def _():
            ex.start(e_in, e_out, e_scr)

        body(*h_in, *h_out, *h_scr)

        @pl.when(last)
        def _():
            ex.wait(e_in, e_out, e_scr)

    outs = pl.pallas_call(
        hosted, name=name, grid=grid, in_specs=list(in_specs) + ex.in_specs,
        out_specs=list(out_specs) + ex.out_specs, out_shape=list(out_shape) + ex.out_shape,
        scratch_shapes=list(scratch_shapes) + ex.scratch, compiler_params=_params(len(grid)),
    )(*args, *ex.srcs)
    return list(outs[:n_out]), list(outs[n_out:])


def _sum_adamw(parts, w, m, v, name):
    _, r, wd = parts.shape
    tr = _pick(r, ADAM_ROWS, 8)
    c1 = 1.0 - ADAM_B1 ** ADAM_STEP
    c2 = 1.0 - ADAM_B2 ** ADAM_STEP

    def body(p_ref, w_ref, m_ref, v_ref, g_ref, d_ref, nm_ref, nv_ref):
        g = p_ref[0].astype(F32)
        for s in range(1, N_DEV):
            g = g + p_ref[s].astype(F32)
        nm = ADAM_B1 * m_ref[...] + (1.0 - ADAM_B1) * g
        nv = ADAM_B2 * v_ref[...] + (1.0 - ADAM_B2) * (g * g)
        g_ref[...] = g
        nm_ref[...] = nm
        nv_ref[...] = nv
        d_ref[...] = -ADAM_LR * ((nm / c1) / (jnp.sqrt(nv / c2) + ADAM_EPS) + ADAM_WD * w_ref[...])

    row = pl.BlockSpec((tr, wd), lambda i: (i, 0))
    return pl.pallas_call(
        body, name=name, grid=(r // tr,),
        in_specs=[pl.BlockSpec((N_DEV, tr, wd), lambda i: (0, i, 0)), row, row, row],
        out_specs=[row] * 4, out_shape=[jax.ShapeDtypeStruct((r, wd), F32)] * 4,
        compiler_params=_params(1),
    )(parts, w, m, v)


def _pack_rows(size, row_mult):
    rows = -(-size // PACK_LANES)
    return -(-rows // row_mult) * row_mult


def _pack(flats, row_mult, dtype, total_mult=None):
    out = []
    for f in flats:
        size = f.shape[-1]
        rows = _pack_rows(size, row_mult)
        pad = [(0, 0)] * (f.ndim - 1) + [(0, rows * PACK_LANES - size)]
        out.append(jnp.pad(f.astype(dtype), pad).reshape(f.shape[:-1] + (rows, PACK_LANES)))
    if total_mult is not None:
        total = sum(o.shape[-2] for o in out)
        extra = -(-total // total_mult) * total_mult - total
        if extra:
            out.append(jnp.zeros(out[0].shape[:-2] + (extra, PACK_LANES), dtype))
    return jnp.concatenate(out, axis=-2)


def _unpack(buf, sizes, row_mult):
    out, row = [], 0
    for size in sizes:
        rows = _pack_rows(size, row_mult)
        part = buf[..., row:row + rows, :]
        out.append(part.reshape(buf.shape[:-2] + (rows * PACK_LANES,))[..., :size])
        row += rows
    return out


def _to_slots(full, axis):
    if axis is None:
        return jnp.broadcast_to(full.reshape(1, -1), (N_DEV, full.size))
    shape = full.shape
    split = full.reshape(shape[:axis] + (N_DEV, shape[axis] // N_DEV) + shape[axis + 1:])
    return jnp.moveaxis(split, axis, 0).reshape(N_DEV, -1)


def _from_slots(slots, axis, block_shape):
    split = jnp.moveaxis(slots, 0, axis)
    shape = list(block_shape)
    shape[axis] *= N_DEV
    return split.reshape(shape)


def kernel(x, p, norm_g, w_in_a, conv_w, conv_b, ln_g, ln_b, w_out_a, kv_norm_g, w_kv, k_norm_g, w_in_b, q_norm_g, w_out_b, ple_norm_g, w_ple_gate, w_ple_proj, loss_target, m_norm_g, m_w_in_a, m_conv_w, m_conv_b, m_ln_g, m_ln_b, m_w_out_a, m_kv_norm_g, m_w_kv, m_k_norm_g, m_w_in_b, m_q_norm_g, m_w_out_b, m_ple_norm_g, m_w_ple_gate, m_w_ple_proj, v_norm_g, v_w_in_a, v_conv_w, v_conv_b, v_ln_g, v_ln_b, v_w_out_a, v_kv_norm_g, v_w_kv, v_k_norm_g, v_w_in_b, v_q_norm_g, v_w_out_b, v_ple_norm_g, v_w_ple_gate, v_w_ple_proj):
    weights = dict(zip(WEIGHT_NAMES, (norm_g, w_in_a, conv_w, conv_b, ln_g, ln_b, w_out_a, kv_norm_g, w_kv, k_norm_g,
                                      w_in_b, q_norm_g, w_out_b, ple_norm_g, w_ple_gate, w_ple_proj)))
    mom_m = dict(zip(WEIGHT_NAMES, (m_norm_g, m_w_in_a, m_conv_w, m_conv_b, m_ln_g, m_ln_b, m_w_out_a, m_kv_norm_g,
                                    m_w_kv, m_k_norm_g, m_w_in_b, m_q_norm_g, m_w_out_b, m_ple_norm_g, m_w_ple_gate,
                                    m_w_ple_proj)))
    mom_v = dict(zip(WEIGHT_NAMES, (v_norm_g, v_w_in_a, v_conv_w, v_conv_b, v_ln_g, v_ln_b, v_w_out_a, v_kv_norm_g,
                                    v_w_kv, v_k_norm_g, v_w_in_b, v_q_norm_g, v_w_out_b, v_ple_norm_g, v_w_ple_gate,
                                    v_w_ple_proj)))
    bsz, seq, d = x.shape
    t = bsz * seq
    assert seq % (max(DILATIONS) * SPAN) == 0 and d % V7X_LANES == 0

    full = {}

    def rows2d(a):
        return a.reshape(-1, a.shape[-1])

    def packed(source, names, dtype, total_mult=None):
        return _pack([source[n].reshape(-1) for n in names], 16, dtype, total_mult)

    def gathered(names, bufs):
        for n, buf in zip(names, bufs):
            full[n] = _from_slots(buf.reshape((N_DEV,) + weights[n].shape), SHARD_AXIS[n], weights[n].shape)

    w1_all, wv_all = _gather_chip_once([rows2d(weights['w_in_a']).astype(BF16),
                                        _pack([weights[n].reshape(-1) for n in VECTOR_WEIGHTS], 8, F32)],
                                       "gather_first")
    gathered(GROUP_FIRST, [w1_all])
    for n, slots in zip(VECTOR_WEIGHTS, _unpack(wv_all, [weights[n].size for n in VECTOR_WEIGHTS], 8)):
        full[n] = _from_slots(slots.reshape((N_DEV,) + weights[n].shape), SHARD_AXIS[n], weights[n].shape)
    gather_rest = _Exchange([rows2d(weights[n]).astype(BF16) for n in GROUP_REST], gather=True)
    wa_in = full['w_in_a'][0]
    cw, cb, lg, lb = full['conv_w'][0], full['conv_b'], full['ln_g'], full['ln_b']

    tables = _rope_tables(seq)
    ones = _head_ones(V7X_LANES)
    rep = V7X_LANES // HEAD_DIM
    head_gain = jnp.concatenate([jnp.tile(q_norm_g[0], (1, rep)), jnp.tile(k_norm_g, rep)[None]], axis=0)

    x0 = x.reshape(t, d)
    p0, p1 = p[0].reshape(t, -1), p[1].reshape(t, -1)
    target = loss_target.reshape(t, d)
    g_norm0, g_norm1 = norm_g[0:1], norm_g[1:2]
    g_ple0, g_ple1 = ple_norm_g[0:1], ple_norm_g[1:2]
    g_kv = kv_norm_g.reshape(1, d)

    (u0,) = _rmsnorm_fwd(x0, [g_norm0], "norm0")
    proj_a = _matmul(u0, wa_in, 'nn', "in_a")
    m_act, y_conv, w2_all = _conv_fwd(proj_a, cw, cb, lg, lb, seq, "conv_fwd", ex=gather_rest)
    gathered(GROUP_REST, w2_all)
    wa_out = full['w_out_a'][0]
    wkv = full['w_kv']
    wb_in, wb_out = full['w_in_b'][0], full['w_out_b'][0]
    wg, wp = full['w_ple_gate'], full['w_ple_proj']
    h0, pg0 = _matmul(m_act, wa_out, 'nn', "out_a", add=x0, norm_gain=g_ple0)
    gl0 = _matmul(pg0, wg[0], 'nn', "ple_gate0", out_dtype=BF16)
    pp0 = _matmul(p0, wp[0], 'nn', "ple_proj0", out_dtype=BF16)

    x1, (kvn, u1) = _ple_norm_fwd(h0, gl0, pp0, [g_kv, g_norm1], "ple0_norm1")
    kv = _matmul(kvn, wkv, 'nn', "kv")
    proj_b = _matmul(u1, wb_in, 'nn', "in_b")
    o_att, lse, ao = _attn_fwd(proj_b, kv, head_gain, tables, ones, bsz, seq, "attn_fwd")
    h1, pg1 = _matmul(ao, wb_out, 'nn', "out_b", add=x1, norm_gain=g_ple1)
    gl1 = _matmul(pg1, wg[1], 'nn', "ple_gate1", out_dtype=BF16)
    pp1 = _matmul(p1, wp[1], 'nn', "ple_proj1", out_dtype=BF16)

    dx2, dgl1, dpp1, loss_part = _ple_loss(h1, gl1, pp1, target, "ple1_loss")
    loss = lax.psum(jnp.sum(loss_part), ("x", "y", "c"))

    grads = {}
    slot = {}

    dwp1 = _matmul(p1, dpp1, 'tn', "d_ple_proj1", out_dtype=BF16, slot_cols=d // N_DEV)
    dwg1 = _matmul(pg1, dgl1, 'tn', "d_ple_gate1", out_dtype=BF16)
    dpg1 = _matmul(dgl1, wg[1], 'nt', "d_ple_norm1", out_dtype=BF16)
    dh1, (dg_ple1,) = _rmsnorm_bwd(h1, [g_ple1], [dpg1], dx2, "ple_norm1_bwd")
    slot['w_out_b'] = _matmul(ao, dh1, 'tn', "d_out_b", out_dtype=BF16).reshape(N_DEV, -1, d)
    dao = _matmul(dh1, wb_out, 'nt', "d_ao", out_dtype=BF16)
    dproj_b, dkv, dg_head = _attn_bwd(proj_b, kv, dao, o_att, lse, head_gain, tables, ones, bsz, seq, "attn_bwd")
    slot['w_in_b'] = _matmul(u1, dproj_b, 'tn', "d_in_b", out_dtype=BF16, slot_cols=4 * d // N_DEV)
    du1 = _matmul(dproj_b, wb_in, 'nt', "d_u1", out_dtype=BF16)
    slot['w_kv'] = _matmul(kvn, dkv, 'tn', "d_kv", out_dtype=BF16, slot_cols=2 * d // N_DEV)
    dkvn = _matmul(dkv, wkv, 'nt', "d_kvn", out_dtype=BF16)
    dx1, (dg_kv, dg_norm1), dgl0, dpp0 = _rmsnorm_bwd(x1, [g_kv, g_norm1], [dkvn, du1], dh1, "norm1_bwd",
                                                      ple=(gl0, pp0))

    dwp0 = _matmul(p0, dpp0, 'tn', "d_ple_proj0", out_dtype=BF16, slot_cols=d // N_DEV)
    dwg0 = _matmul(pg0, dgl0, 'tn', "d_ple_gate0", out_dtype=BF16)
    dpg0 = _matmul(dgl0, wg[0], 'nt', "d_ple_norm0", out_dtype=BF16)
    dh0, (dg_ple0,) = _rmsnorm_bwd(h0, [g_ple0], [dpg0], dx1, "ple_norm0_bwd")
    slot['w_out_a'] = _matmul(m_act, dh0, 'tn', "d_out_a", out_dtype=BF16).reshape(N_DEV, -1, d)
    dm = _matmul(dh0, wa_out, 'nt', "d_m", out_dtype=BF16)
    dy_conv, dz, d_lg, d_lb, d_cb = _ln_gate_bwd(dm, y_conv, proj_a, lg, lb, "ln_gate_bwd")
    slot['w_ple_gate'] = jnp.stack([dwg0.reshape(N_DEV, -1, d), dwg1.reshape(N_DEV, -1, d)],
                                   axis=1).reshape(N_DEV, -1, d)
    slot['w_ple_proj'] = jnp.stack([dwp0, dwp1], axis=1).reshape(N_DEV, -1, d // N_DEV)

    dproj_a, d_cw, parts_rest = _conv_bwd(dy_conv, dz, proj_a, cw, seq, "conv_bwd",
                                          ex=_Exchange([slot[n] for n in GROUP_REST], gather=False))
    slot['w_in_a'] = _matmul(u0, dproj_a, 'tn', "d_in_a", out_dtype=BF16, slot_cols=wa_in.shape[1] // N_DEV)
    du0, parts_first = _matmul(dproj_a, wa_in, 'nt', "d_u0", out_dtype=BF16,
                               ex=_Exchange([slot['w_in_a']], gather=False))
    dx0, (dg_norm0,) = _rmsnorm_bwd(x0, [g_norm0], [du0], dh0, "norm0_bwd")

    grads['norm_g'] = jnp.stack([dg_norm0, dg_norm1])
    grads['conv_w'] = d_cw[None]
    grads['conv_b'] = d_cb[None]
    grads['ln_g'] = d_lg[None]
    grads['ln_b'] = d_lb[None]
    grads['kv_norm_g'] = dg_kv
    grads['k_norm_g'] = dg_head[3]
    grads['q_norm_g'] = dg_head[0:3][None]
    grads['ple_norm_g'] = jnp.stack([dg_ple0, dg_ple1])
    small_pack = _pack([_to_slots(grads[n], SHARD_AXIS[n]) for n in GROUP_SMALL], 16, BF16)
    (parts_small,) = _run_exchange(_Exchange([small_pack], gather=False), "exchange_small")

    updated = {}
    for n, parts in zip(GROUP_REST + GROUP_FIRST, parts_rest + parts_first):
        outs = _sum_adamw(parts, rows2d(weights[n]), rows2d(mom_m[n]), rows2d(mom_v[n]), "sum_adamw_" + n)
        for kind, buf in enumerate(outs):
            updated[kind, n] = buf.reshape(weights[n].shape)
    outs = _sum_adamw(parts_small, packed(weights, GROUP_SMALL, F32), packed(mom_m, GROUP_SMALL, F32),
                      packed(mom_v, GROUP_SMALL, F32), "sum_adamw_small")
    sizes = [weights[n].size for n in GROUP_SMALL]
    for kind, buf in enumerate(outs):
        for n, flat in zip(GROUP_SMALL, _unpack(buf, sizes, 16)):
            updated[kind, n] = flat.reshape(weights[n].shape)
    result = [loss, dx0.reshape(bsz, seq, d)]
    for kind in range(4):
        result.extend(updated[kind, n] for n in WEIGHT_NAMES)
    return tuple(result)
```

```python
import functools

import jax
import jax.numpy as jnp
from jax import lax
from jax.experimental import pallas as pl
from jax.experimental.pallas import tpu as pltpu

F32 = jnp.float32
BF16 = jnp.bfloat16

N_DEV = 8
HEAD_DIM = 64
ROPE_DIM = 16
ROPE_THETA = 500000.0
EPS = 1e-6
NEG_INF = -1e30
SPAN = 128
DILATIONS = (1, 4, 16)
CONV_WIDTH = 31
HALO = 32
CONV_ROWS = 32
CONV_W_ROWS = 64
PACK_LANES = 1024
V7X_LANES = 128
V7X_SUBLANES = 8
VMEM_LIMIT_BYTES = 56 * 1024 * 1024

ADAM_LR = 0.001
ADAM_B1 = 0.9
ADAM_B2 = 0.999
ADAM_EPS = 1e-08
ADAM_WD = 0.01
ADAM_STEP = 10
ADAM_ROWS = 256

WEIGHT_NAMES = ('norm_g', 'w_in_a', 'conv_w', 'conv_b', 'ln_g', 'ln_b', 'w_out_a', 'kv_norm_g', 'w_kv',
                'k_norm_g', 'w_in_b', 'q_norm_g', 'w_out_b', 'ple_norm_g', 'w_ple_gate', 'w_ple_proj')
SHARD_AXIS = {'norm_g': None, 'w_in_a': 2, 'conv_w': 2, 'conv_b': 1, 'ln_g': 1, 'ln_b': 1, 'w_out_a': 1,
              'kv_norm_g': None, 'w_kv': 1, 'k_norm_g': None, 'w_in_b': 2, 'q_norm_g': None, 'w_out_b': 1,
              'ple_norm_g': None, 'w_ple_gate': 1, 'w_ple_proj': 2}
VECTOR_WEIGHTS = ('conv_w', 'conv_b', 'ln_g', 'ln_b')
GROUP_FIRST = ('w_in_a',)
GROUP_REST = ('w_out_a', 'w_kv', 'w_in_b', 'w_out_b', 'w_ple_gate', 'w_ple_proj')
GROUP_SMALL = ('norm_g', 'conv_w', 'conv_b', 'ln_g', 'ln_b', 'kv_norm_g', 'k_norm_g', 'q_norm_g', 'ple_norm_g')


def _pick(n, target, mult):
    t = (min(target, n) // mult) * mult
    while t >= mult:
        if n % t == 0:
            return t
        t -= mult
    return n


def _params(n_grid):
    return pltpu.CompilerParams(dimension_semantics=("arbitrary",) * n_grid, vmem_limit_bytes=VMEM_LIMIT_BYTES)


def _sig(x):
    return 0.5 * jnp.tanh(0.5 * x) + 0.5


def _colsum8(v):
    r, w = v.shape
    return v.reshape(r // V7X_SUBLANES, V7X_SUBLANES, w).sum(axis=0)


def _rows(tm, w, col=0):
    return pl.BlockSpec((tm, w), lambda i: (i, col))


def _const(shape):
    nd = len(shape)
    return pl.BlockSpec(shape, lambda i: (0,) * nd)


def _segsum(v, e_ref):
    hi = v.astype(BF16)
    lo = (v - hi.astype(F32)).astype(BF16)
    e = e_ref[...]
    return jnp.dot(hi, e, preferred_element_type=F32) + jnp.dot(lo, e, preferred_element_type=F32)


MM_TILE = 1024
MM_TILE_K = 2048


def _matmul(a, b, mode, name, out_dtype=F32, add=None, ex=None, slot_cols=None, norm_gain=None):
    if mode == 'nn':
        (m, k), (_, n) = a.shape, b.shape
    elif mode == 'nt':
        (m, k), (n, _) = a.shape, b.shape
    else:
        (k, m), (_, n) = a.shape, b.shape
    out_struct = jax.ShapeDtypeStruct((m, n), out_dtype)
    n_slots = 0
    if mode == 'tn':
        tm, tn, tk = _pick(m, MM_TILE, 128), _pick(n, MM_TILE, 128), _pick(k, MM_TILE_K, 128)
        o_spec = pl.BlockSpec((tm, tn), lambda i, j, kk: (i, j))
        if slot_cols is not None:
            assert n == N_DEV * slot_cols
            n_slots = max(s for s in (1, 2, 4, 8) if s == 1 or slot_cols * s <= MM_TILE)
            tn = slot_cols * n_slots
            o_spec = pl.BlockSpec((n_slots, tm, slot_cols), lambda i, j, kk: (j, i, 0))
            out_struct = jax.ShapeDtypeStruct((N_DEV, m, slot_cols), out_dtype)
        grid = (m // tm, n // tn, k // tk)
        a_spec = pl.BlockSpec((tk, tm), lambda i, j, kk: (kk, i))
        b_spec = pl.BlockSpec((tk, tn), lambda i, j, kk: (kk, j))
        dims = (((0,), (0,)), ((), ()))
    else:
        tm, tn, tk = _pick(m, MM_TILE, 128), _pick(n, MM_TILE, 128), _pick(k, MM_TILE_K, 128)
        grid = (n // tn, m // tm, k // tk)
        a_spec = pl.BlockSpec((tm, tk), lambda j, i, kk: (i, kk))
        o_spec = pl.BlockSpec((tm, tn), lambda j, i, kk: (i, j))
        if mode == 'nn':
            b_spec = pl.BlockSpec((tk, tn), lambda j, i, kk: (kk, j))
            dims = (((1,), (0,)), ((), ()))
        else:
            b_spec = pl.BlockSpec((tn, tk), lambda j, i, kk: (j, kk))
            dims = (((1,), (1,)), ((), ()))
    nk = grid[2]
    has_add = add is not None
    has_norm = norm_gain is not None
    assert not has_norm or (tn == n and mode != 'tn')

    def body(*refs):
        a_ref, b_ref = refs[0], refs[1]
        add_ref = refs[2] if has_add else None
        gain_ref = refs[2 + has_add] if has_norm else None
        o_ref = refs[2 + has_add + has_norm]
        norm_ref = refs[3 + has_add + has_norm] if has_norm else None
        part = lax.dot_general(a_ref[...].astype(BF16), b_ref[...].astype(BF16), dims, preferred_element_type=F32)

        def finish(total):
            if has_add:
                total = total + add_ref[...]
            if n_slots:
                for s in range(n_slots):
                    o_ref[s] = total[:, s * slot_cols:(s + 1) * slot_cols].astype(out_dtype)
            else:
                o_ref[...] = total.astype(out_dtype)
            if has_norm:
                y = total * lax.rsqrt(jnp.mean(total * total, axis=-1, keepdims=True) + EPS)
                norm_ref[...] = (y * gain_ref[...]).astype(BF16)

        if nk == 1:
            finish(part)
        else:
            acc_ref = refs[3 + has_add + 2 * has_norm]
            kk = pl.program_id(2)

            @pl.when(kk == 0)
            def _():
                acc_ref[...] = part

            @pl.when(kk > 0)
            def _():
                acc_ref[...] += part

            @pl.when(kk == nk - 1)
            def _():
                finish(acc_ref[...])

    in_specs = [a_spec, b_spec] + ([o_spec] if has_add else [])
    args = [a, b] + ([add] if has_add else [])
    out_specs, out_structs = [o_spec], [out_struct]
    if has_norm:
        in_specs.append(pl.BlockSpec((1, n), lambda j, i, kk: (0, 0)))
        args.append(norm_gain)
        out_specs.append(o_spec)
        out_structs.append(jax.ShapeDtypeStruct((m, n), BF16))
    scratch = [pltpu.VMEM((tm, tn), F32)] if nk > 1 else []
    outs, moved = _hosted_call(body, ex, name, grid, in_specs, out_specs, out_structs, scratch, args)
    out = outs[0] if not has_norm else tuple(outs)
    return out if ex is None else (out, moved)


def _rmsnorm_fwd(x, gains, name):
    t, d = x.shape
    tm = _pick(t, 512, 8)
    n = len(gains)

    def body(*refs):
        x_ref, g_refs, o_refs = refs[0], refs[1:1 + n], refs[1 + n:]
        xv = x_ref[...]
        y = xv * lax.rsqrt(jnp.mean(xv * xv, axis=-1, keepdims=True) + EPS)
        for g_ref, o_ref in zip(g_refs, o_refs):
            o_ref[...] = (y * g_ref[...]).astype(BF16)

    return pl.pallas_call(
        body, name=name, grid=(t // tm,),
        in_specs=[_rows(tm, d)] + [_const((1, d))] * n,
        out_specs=[_rows(tm, d)] * n,
        out_shape=[jax.ShapeDtypeStruct((t, d), BF16)] * n,
        compiler_params=_params(1),
    )(x, *gains)


def _ple_grads(dx, gl, pp):
    sg = _sig(gl)
    return (dx * pp * sg * (1.0 - sg)).astype(BF16), (dx * sg).astype(BF16)


def _rmsnorm_bwd(x, gains, dys, add, name, ple=None):
    t, d = x.shape
    tm = _pick(t, 512, 16)
    n = len(gains)
    n_ple = 0 if ple is None else 2

    def body(*refs):
        x_ref, add_ref = refs[0], refs[1]
        g_refs, dy_refs = refs[2:2 + n], refs[2 + n:2 + 2 * n]
        ple_refs = refs[2 + 2 * n:2 + 2 * n + n_ple]
        outs = refs[2 + 2 * n + n_ple:]
        dx_ref, dg_refs, dple_refs = outs[0], outs[1:1 + n], outs[1 + n:]
        i = pl.program_id(0)
        xv = x_ref[...]
        r = lax.rsqrt(jnp.mean(xv * xv, axis=-1, keepdims=True) + EPS)
        xhat = xv * r
        dx = add_ref[...]
        for g_ref, dy_ref, dg_ref in zip(g_refs, dy_refs, dg_refs):
            dy = dy_ref[...].astype(F32)
            dyg = dy * g_ref[...]
            dx = dx + r * (dyg - xhat * jnp.mean(dyg * xhat, axis=-1, keepdims=True))
            part = _colsum8(dy * xhat)

            @pl.when(i == 0)
            def _():
                dg_ref[...] = part

            @pl.when(i > 0)
            def _():
                dg_ref[...] += part

        dx_ref[...] = dx
        if n_ple:
            dple_refs[0][...], dple_refs[1][...] = _ple_grads(dx, ple_refs[0][...].astype(F32),
                                                               ple_refs[1][...].astype(F32))

    outs = pl.pallas_call(
        body, name=name, grid=(t // tm,),
        in_specs=[_rows(tm, d), _rows(tm, d)] + [_const((1, d))] * n + [_rows(tm, d)] * (n + n_ple),
        out_specs=[_rows(tm, d)] + [_const((V7X_SUBLANES, d))] * n + [_rows(tm, d)] * n_ple,
        out_shape=([jax.ShapeDtypeStruct((t, d), F32)] + [jax.ShapeDtypeStruct((V7X_SUBLANES, d), F32)] * n
                   + [jax.ShapeDtypeStruct((t, d), BF16)] * n_ple),
        compiler_params=_params(1),
    )(x, add, *gains, *dys, *(ple or ()))
    dgs = [o.sum(axis=0) for o in outs[1:1 + n]]
    return (outs[0], dgs) if ple is None else (outs[0], dgs, outs[1 + n], outs[2 + n])


def _ple_norm_fwd(h, gl, pp, gains, name):
    t, d = h.shape
    tm = _pick(t, 512, 16)
    n = len(gains)

    def body(*refs):
        h_ref, gl_ref, pp_ref = refs[:3]
        g_refs, x_ref, o_refs = refs[3:3 + n], refs[3 + n], refs[4 + n:]
        xv = h_ref[...] + _sig(gl_ref[...].astype(F32)) * pp_ref[...].astype(F32)
        x_ref[...] = xv
        y = xv * lax.rsqrt(jnp.mean(xv * xv, axis=-1, keepdims=True) + EPS)
        for g_ref, o_ref in zip(g_refs, o_refs):
            o_ref[...] = (y * g_ref[...]).astype(BF16)

    outs = pl.pallas_call(
        body, name=name, grid=(t // tm,),
        in_specs=[_rows(tm, d)] * 3 + [_const((1, d))] * n, out_specs=[_rows(tm, d)] * (1 + n),
        out_shape=[jax.ShapeDtypeStruct((t, d), F32)] + [jax.ShapeDtypeStruct((t, d), BF16)] * n,
        compiler_params=_params(1),
    )(h, gl, pp, *gains)
    return outs[0], outs[1:]


def _ple_loss(h, gl, pp, target, name):
    t, d = h.shape
    tm = _pick(t, 512, 16)
    inv_d = 1.0 / d

    def body(h_ref, gl_ref, pp_ref, t_ref, dy_ref, dgl_ref, dpp_ref, l_ref):
        i = pl.program_id(0)
        gl, pp = gl_ref[...].astype(F32), pp_ref[...].astype(F32)
        e = h_ref[...] + _sig(gl) * pp - t_ref[...]
        dy = e * inv_d
        dy_ref[...] = dy
        dgl_ref[...], dpp_ref[...] = _ple_grads(dy, gl, pp)
        part = _colsum8(e * e) * (0.5 * inv_d)

        @pl.when(i == 0)
        def _():
            l_ref[...] = part

        @pl.when(i > 0)
        def _():
            l_ref[...] += part

    return pl.pallas_call(
        body, name=name, grid=(t // tm,), in_specs=[_rows(tm, d)] * 4,
        out_specs=[_rows(tm, d)] * 3 + [_const((V7X_SUBLANES, d))],
        out_shape=[jax.ShapeDtypeStruct((t, d), F32), jax.ShapeDtypeStruct((t, d), BF16),
                   jax.ShapeDtypeStruct((t, d), BF16), jax.ShapeDtypeStruct((V7X_SUBLANES, d), F32)],
        compiler_params=_params(1),
    )(h, gl, pp, target)


def _shift_scratch(ts, cc):
    return pltpu.VMEM((V7X_SUBLANES, ts + HALO - V7X_SUBLANES, cc), F32)


def _shifted_copies(sh_ref, win_ref, cs, ts):
    rows = ts + HALO - V7X_SUBLANES
    for s in range(1, V7X_SUBLANES):
        sh_ref[s] = win_ref[pl.ds(s, rows), cs]


def _tap(sh_ref, win_ref, cs, offset, rows, r0):
    s = offset % V7X_SUBLANES
    start = pl.multiple_of(r0 + (offset - s), V7X_SUBLANES)
    if s == 0:
        return win_ref[pl.ds(start, rows), cs]
    return sh_ref[s, pl.ds(start, rows), :]


def _conv_fwd(proj, conv_w, conv_b, ln_g, ln_b, seq, name, ex=None):
    t, c3 = proj.shape
    c = c3 // 3
    ts = _pick(seq, 256, HALO)
    nsb = seq // ts
    cc = _pick(c, 512, V7X_LANES)
    hb = ts // HALO

    def body(a_ref, b_ref, z_ref, ap_ref, bp_ref, w_ref, cb_ref, g_ref, be_ref, m_ref, y_ref, win_ref, sh_ref):
        i = pl.program_id(0)
        first = (i % nsb) == 0
        win_ref[0:HALO, :] = jnp.where(first, 0.0, ap_ref[...] * _sig(bp_ref[...]))
        win_ref[HALO:, :] = a_ref[...] * _sig(b_ref[...])
        for ci in range(c // cc):
            cs = slice(ci * cc, (ci + 1) * cc)
            _shifted_copies(sh_ref, win_ref, cs, ts)

            def out_rows(rb, carry, cs=cs):
                r0 = rb * CONV_ROWS
                acc = jnp.zeros((CONV_ROWS, cc), F32) + cb_ref[:, cs]
                for k in range(CONV_WIDTH):
                    acc = acc + w_ref[k:k + 1, cs] * _tap(sh_ref, win_ref, cs, HALO - (CONV_WIDTH - 1) + k,
                                                           CONV_ROWS, r0)
                y_ref[pl.ds(pl.multiple_of(r0, CONV_ROWS), CONV_ROWS), cs] = acc
                return carry

            lax.fori_loop(0, ts // CONV_ROWS, out_rows, 0, unroll=2)
        y = y_ref[...]
        mu = jnp.mean(y, axis=-1, keepdims=True)
        xc = y - mu
        rstd = lax.rsqrt(jnp.mean(xc * xc, axis=-1, keepdims=True) + EPS)
        ln = xc * rstd * g_ref[...] + be_ref[...]
        zz = z_ref[...]
        m_ref[...] = (ln * _sig(ln) * zz * _sig(zz)).astype(BF16)

    halo_a = pl.BlockSpec((HALO, c), lambda i: (jnp.maximum(i * hb - 1, 0), 0))
    halo_b = pl.BlockSpec((HALO, c), lambda i: (jnp.maximum(i * hb - 1, 0), 1))
    (m_act, y), moved = _hosted_call(
        body, ex, name, (t // ts,),
        [_rows(ts, c, 0), _rows(ts, c, 1), _rows(ts, c, 2), halo_a, halo_b,
         _const((CONV_WIDTH, c)), _const((1, c)), _const((1, c)), _const((1, c))],
        [_rows(ts, c), _rows(ts, c)],
        [jax.ShapeDtypeStruct((t, c), BF16), jax.ShapeDtypeStruct((t, c), F32)],
        [pltpu.VMEM((HALO + ts, c), F32), _shift_scratch(ts, cc)],
        (proj, proj, proj, proj, proj, conv_w, conv_b, ln_g, ln_b))
    return m_act, y, moved


def _ln_gate_bwd(dm, y, proj, ln_g, ln_b, name):
    t, c = y.shape
    tm = _pick(t, 256, 8)

    def body(dm_ref, y_ref, z_ref, g_ref, be_ref, dy_ref, dz_ref, dg_ref, db_ref, dcb_ref):
        i = pl.program_id(0)
        yv = y_ref[...]
        mu = jnp.mean(yv, axis=-1, keepdims=True)
        xc = yv - mu
        rstd = lax.rsqrt(jnp.mean(xc * xc, axis=-1, keepdims=True) + EPS)
        xhat = xc * rstd
        g = g_ref[...]
        ln = xhat * g + be_ref[...]
        sl = _sig(ln)
        zz = z_ref[...]
        sz = _sig(zz)
        dmv = dm_ref[...].astype(F32)
        dz_ref[...] = (dmv * (ln * sl) * (sz * (1.0 + zz * (1.0 - sz)))).astype(BF16)
        dln = dmv * (zz * sz) * (sl * (1.0 + ln * (1.0 - sl)))
        dxh = dln * g
        dyv = rstd * (dxh - jnp.mean(dxh, axis=-1, keepdims=True)
                      - xhat * jnp.mean(dxh * xhat, axis=-1, keepdims=True))
        dy_ref[...] = dyv
        parts = (_colsum8(dln * xhat), _colsum8(dln), _colsum8(dyv))

        @pl.when(i == 0)
        def _():
            for ref, part in zip((dg_ref, db_ref, dcb_ref), parts):
                ref[...] = part

        @pl.when(i > 0)
        def _():
            for ref, part in zip((dg_ref, db_ref, dcb_ref), parts):
                ref[...] += part

    acc = jax.ShapeDtypeStruct((V7X_SUBLANES, c), F32)
    outs = pl.pallas_call(
        body, name=name, grid=(t // tm,),
        in_specs=[_rows(tm, c), _rows(tm, c), _rows(tm, c, 2), _const((1, c)), _const((1, c))],
        out_specs=[_rows(tm, c), _rows(tm, c)] + [_const((V7X_SUBLANES, c))] * 3,
        out_shape=[jax.ShapeDtypeStruct((t, c), F32), jax.ShapeDtypeStruct((t, c), BF16), acc, acc, acc],
        compiler_params=_params(1),
    )(dm, y, proj, ln_g, ln_b)
    return outs[0], outs[1], outs[2].sum(axis=0), outs[3].sum(axis=0), outs[4].sum(axis=0)


def _conv_bwd(dy, dz, proj, conv_w, seq, name, ex=None):
    t, c3 = proj.shape
    c = c3 // 3
    ts = _pick(seq, 256, HALO)
    nsb = seq // ts
    cc = _pick(c, 512, V7X_LANES)
    hb = ts // HALO
    last_halo = t // HALO - 1
    back = CONV_WIDTH - 1

    def body(dy_ref, dyn_ref, dz_ref, a_ref, b_ref, ap_ref, bp_ref, w_ref, o_ref, dw_ref, win_ref, dwin_ref,
             sh_ref, dsh_ref):
        i = pl.program_id(0)
        first = (i % nsb) == 0
        last = (i % nsb) == nsb - 1
        win_ref[0:HALO, :] = jnp.where(first, 0.0, ap_ref[...] * _sig(bp_ref[...]))
        win_ref[HALO:, :] = a_ref[...] * _sig(b_ref[...])
        dwin_ref[0:ts, :] = dy_ref[...]
        dwin_ref[ts:, :] = jnp.where(last, 0.0, dyn_ref[...])

        @pl.when(i == 0)
        def _():
            dw_ref[...] = jnp.zeros_like(dw_ref)

        for ci in range(c // cc):
            cs = slice(ci * cc, (ci + 1) * cc)
            _shifted_copies(sh_ref, win_ref, cs, ts)
            _shifted_copies(dsh_ref, dwin_ref, cs, ts)

            def in_grad_rows(rb, carry, cs=cs, ci=ci):
                r0 = rb * CONV_ROWS
                rows = pl.ds(pl.multiple_of(r0, CONV_ROWS), CONV_ROWS)
                dglu = jnp.zeros((CONV_ROWS, cc), F32)
                for k in range(CONV_WIDTH):
                    dglu = dglu + w_ref[k:k + 1, cs] * _tap(dsh_ref, dwin_ref, cs, back - k, CONV_ROWS, r0)
                sbc = _sig(b_ref[rows, cs])
                o_ref[rows, cs] = (dglu * sbc).astype(BF16)
                o_ref[rows, c + ci * cc:c + (ci + 1) * cc] = (dglu * a_ref[rows, cs] * sbc * (1.0 - sbc)).astype(BF16)
                return carry

            def w_grad_rows(rb, carry, cs=cs):
                r0 = rb * CONV_W_ROWS
                dcur = dwin_ref[pl.ds(pl.multiple_of(r0, CONV_W_ROWS), CONV_W_ROWS), cs]
                for k in range(CONV_WIDTH):
                    dw_ref[k * V7X_SUBLANES:(k + 1) * V7X_SUBLANES, cs] += _colsum8(
                        dcur * _tap(sh_ref, win_ref, cs, HALO - back + k, CONV_W_ROWS, r0))
                return carry

            lax.fori_loop(0, ts // CONV_ROWS, in_grad_rows, 0, unroll=2)
            lax.fori_loop(0, ts // CONV_W_ROWS, w_grad_rows, 0)
        o_ref[:, 2 * c:] = dz_ref[...]

    halo_next = pl.BlockSpec((HALO, c), lambda i: (jnp.minimum((i + 1) * hb, last_halo), 0))
    halo_a = pl.BlockSpec((HALO, c), lambda i: (jnp.maximum(i * hb - 1, 0), 0))
    halo_b = pl.BlockSpec((HALO, c), lambda i: (jnp.maximum(i * hb - 1, 0), 1))
    (dproj, dw), moved = _hosted_call(
        body, ex, name, (t // ts,),
        [_rows(ts, c), halo_next, _rows(ts, c), _rows(ts, c, 0), _rows(ts, c, 1), halo_a, halo_b,
         _const((CONV_WIDTH, c))],
        [_rows(ts, c3), _const((CONV_WIDTH * V7X_SUBLANES, c))],
        [jax.ShapeDtypeStruct((t, c3), BF16), jax.ShapeDtypeStruct((CONV_WIDTH * V7X_SUBLANES, c), F32)],
        [pltpu.VMEM((HALO + ts, c), F32), pltpu.VMEM((ts + HALO, c), F32),
         _shift_scratch(ts, cc), _shift_scratch(ts, cc)],
        (dy, dy, dz, proj, proj, proj, proj, conv_w))
    return dproj, dw.reshape(CONV_WIDTH, V7X_SUBLANES, c).sum(axis=1), moved


def _rope_tables(seq):
    half = ROPE_DIM // 2
    inv = ROPE_THETA ** (-jnp.arange(half, dtype=F32) * (2.0 / ROPE_DIM))
    ang = jnp.arange(seq).astype(F32)[:, None] * inv[None, :]
    cos, sin = jnp.cos(ang), jnp.sin(ang)
    zeros = jnp.zeros((seq, HEAD_DIM - ROPE_DIM), F32)
    zh = jnp.zeros((seq, half), F32)
    a = jnp.concatenate([cos, cos, zeros + 1.0], axis=1)
    b = jnp.concatenate([zh, sin, zeros], axis=1)
    c = jnp.concatenate([-sin, zh, zeros], axis=1)
    rep = V7X_LANES // HEAD_DIM
    return tuple(jnp.tile(v, (1, rep)) for v in (a, b, c))


def _head_ones(d):
    head = jnp.arange(d) // HEAD_DIM
    return (head[:, None] == head[None, :]).astype(BF16)


def _rope(ch, ta, tb, tc):
    return ta * ch + tb * pltpu.roll(ch, ROPE_DIM // 2, 1) + tc * pltpu.roll(ch, V7X_LANES - ROPE_DIM // 2, 1)


def _rope_t(ch, ta, tb, tc):
    return ta * ch + pltpu.roll(tb * ch, V7X_LANES - ROPE_DIM // 2, 1) + pltpu.roll(tc * ch, ROPE_DIM // 2, 1)


def _norm_rope_bwd(xhat, r, dout, gain, ta, tb, tc, e_ref):
    dxn = _rope_t(dout, ta, tb, tc)
    dxh = dxn * gain
    dx = r * (dxh - xhat * (_segsum(dxh * xhat, e_ref) * (1.0 / HEAD_DIM)))
    return dx, _colsum8(dxn * xhat)


def _norm_rope_rows(dst_ref, src_ref, gain, ta_ref, tb_ref, tc_ref, e_ref, seq, xhat_ref=None, r_ref=None):
    for r0 in range(0, seq, ATTN_PIECE):
        rows = slice(r0, r0 + ATTN_PIECE)
        xv = src_ref[rows, :]
        r = lax.rsqrt(_segsum(xv * xv, e_ref) * (1.0 / HEAD_DIM) + EPS)
        xhat = xv * r
        if xhat_ref is not None:
            xhat_ref[rows, :] = xhat
            r_ref[rows, :] = r
        dst_ref[rows, :] = _rope(xhat * gain, ta_ref[rows, :], tb_ref[rows, :], tc_ref[rows, :])


ATTN_PIECE = 256
ATTN_UNROLL = 16
CHUNK_UNROLL = 4


def _pieces(dil, seq):
    length = seq // dil
    rows = min(length, ATTN_PIECE)
    return [(r + dil * ci * rows, r * length + ci * rows, rows) for r in range(dil) for ci in range(length // rows)]


def _strided(ref, start, rows, dil):
    if dil == 1:
        return ref[pl.ds(start, rows), :]
    return ref[pl.ds(start, rows, stride=dil), :]


def _strided_set(ref, start, rows, dil, val):
    if dil == 1:
        ref[pl.ds(start, rows), :] = val
    else:
        ref[pl.ds(start, rows, stride=dil), :] = val


def _nt(a, b):
    return lax.dot_general(a, b, (((1,), (1,)), ((), ())), preferred_element_type=F32)


def _tn(a, b):
    return lax.dot_general(a, b, (((0,), (0,)), ((), ())), preferred_element_type=F32)


def _set_bias(bias_ref):
    qi = lax.broadcasted_iota(jnp.int32, (2 * SPAN, 2 * SPAN), 0) & (SPAN - 1)
    kj = lax.broadcasted_iota(jnp.int32, (2 * SPAN, 2 * SPAN), 1)
    band = jnp.logical_and(kj >= qi, (kj - SPAN) <= qi)
    bias_ref[1] = jnp.where(band, 0.0, NEG_INF)
    bias_ref[0] = jnp.where(jnp.logical_and(band, kj >= SPAN), 0.0, NEG_INF)


def _block_keys(bias_ref, j, qs, nb):
    if nb == 1:
        return pl.ds(pl.multiple_of(qs + SPAN, SPAN), SPAN), bias_ref[1, :, SPAN:]
    return pl.ds(qs, 2 * SPAN), bias_ref[jnp.minimum(j & (nb - 1), 1)]


def _stack_heads(v, head0):
    zero = jnp.zeros_like(v)
    return jnp.concatenate([jnp.where(head0, v, zero), jnp.where(head0, zero, v)], axis=0)


def _unstack_heads(v2, head0):
    return jnp.where(head0, v2[:SPAN], v2[SPAN:])


def _head_cols(v, lane=0):
    return jnp.concatenate([v[:, lane:lane + 1], v[:, HEAD_DIM + lane:HEAD_DIM + lane + 1]], axis=0)


def _attn_fwd(proj_b, kv, gains, tables, ones, bsz, seq, name):
    t, d4 = proj_b.shape
    d = d4 // 4
    nhp = d // V7X_LANES
    nblk = seq // SPAN
    scale = HEAD_DIM ** -0.5
    n_groups = len(DILATIONS)

    def body(q0_ref, q1_ref, q2_ref, k_ref, v_ref, gate_ref, gain_ref, ta_ref, tb_ref, tc_ref, e_ref,
             o_ref, l_ref, ao_ref, qd, kd, vd, od, ld, on0, on1, on2, ln0, ln1, ln2, kn, qn, bias):
        head0 = lax.broadcasted_iota(jnp.int32, (SPAN, V7X_LANES), 1) < HEAD_DIM

        @pl.when(jnp.logical_and(pl.program_id(0) == 0, pl.program_id(1) == 0))
        def _():
            _set_bias(bias)

        _norm_rope_rows(kn, k_ref, gain_ref[n_groups:n_groups + 1, :], ta_ref, tb_ref, tc_ref, e_ref, seq)
        kd[0:SPAN, :] = jnp.zeros((SPAN, V7X_LANES), BF16)
        vd[0:SPAN, :] = jnp.zeros((SPAN, V7X_LANES), BF16)
        for g, (q_ref, on, ln) in enumerate(((q0_ref, on0, ln0), (q1_ref, on1, ln1), (q2_ref, on2, ln2))):
            dil = DILATIONS[g]
            nb = seq // dil // SPAN
            _norm_rope_rows(qn, q_ref, gain_ref[g:g + 1, :], ta_ref, tb_ref, tc_ref, e_ref, seq)
            for ns, rs, rows in _pieces(dil, seq):
                qd[rs:rs + rows, :] = _strided(qn, ns, rows, dil).astype(BF16)
                kd[SPAN + rs:SPAN + rs + rows, :] = _strided(kn, ns, rows, dil).astype(BF16)
                vd[SPAN + rs:SPAN + rs + rows, :] = _strided(v_ref, ns, rows, dil).astype(BF16)

            def block(j, carry):
                qs = pl.multiple_of(j * SPAN, SPAN)
                q2 = _stack_heads(qd[pl.ds(qs, SPAN), :], head0)
                keys, mask = _block_keys(bias, j, qs, nb)
                kk = kd[keys, :]
                vv = vd[keys, :]
                s = _nt(q2, kk) * scale + mask
                mx = jnp.max(s, axis=1, keepdims=True)
                p = jnp.exp(s - mx)
                den = jnp.sum(p, axis=1, keepdims=True)
                o2 = jnp.dot(p.astype(BF16), vv, preferred_element_type=F32) / den
                l2 = jnp.broadcast_to(mx + jnp.log(den), (2 * SPAN, V7X_LANES))
                od[pl.ds(qs, SPAN), :] = _unstack_heads(o2, head0)
                ld[pl.ds(qs, SPAN), :] = _unstack_heads(l2, head0)
                return carry

            lax.fori_loop(0, nblk, block, 0, unroll=ATTN_UNROLL)
            for ns, rs, rows in _pieces(dil, seq):
                _strided_set(on, ns, rows, dil, od[rs:rs + rows, :])
                _strided_set(ln, ns, rows, dil, ld[rs:rs + rows, :])

        def merge(ci, carry):
            rows = pl.ds(pl.multiple_of(ci * ATTN_PIECE, ATTN_PIECE), ATTN_PIECE)
            ls = [ln0[rows, :], ln1[rows, :], ln2[rows, :]]
            mx = jnp.maximum(jnp.maximum(ls[0], ls[1]), ls[2])
            es = [jnp.exp(v - mx) for v in ls]
            den = es[0] + es[1] + es[2]
            ov = (es[0] * on0[rows, :] + es[1] * on1[rows, :] + es[2] * on2[rows, :]) / den
            gate = gate_ref[rows, :]
            o_ref[rows, :] = ov
            l_ref[rows, :] = mx + jnp.log(den)
            ao_ref[rows, :] = (ov * gate * _sig(gate)).astype(BF16)
            return carry

        lax.fori_loop(0, seq // ATTN_PIECE, merge, 0)

    blk = (None, seq, V7X_LANES)
    pview = proj_b.reshape(bsz, seq, d4)
    kview = kv.reshape(bsz, seq, 2 * d)
    out_spec = pl.BlockSpec(blk, lambda b, h: (b, 0, h))
    tab = pl.BlockSpec((seq, V7X_LANES), lambda b, h: (0, 0))
    nat = pltpu.VMEM((seq, V7X_LANES), F32)
    o, lse, ao = pl.pallas_call(
        body, name=name, grid=(bsz, nhp),
        in_specs=[pl.BlockSpec(blk, lambda b, h: (b, 0, h)),
                  pl.BlockSpec(blk, lambda b, h: (b, 0, nhp + h)),
                  pl.BlockSpec(blk, lambda b, h: (b, 0, 2 * nhp + h)),
                  pl.BlockSpec(blk, lambda b, h: (b, 0, h)),
                  pl.BlockSpec(blk, lambda b, h: (b, 0, nhp + h)),
                  pl.BlockSpec(blk, lambda b, h: (b, 0, 3 * nhp + h)),
                  pl.BlockSpec((n_groups + 1, V7X_LANES), lambda b, h: (0, 0)),
                  tab, tab, tab,
                  pl.BlockSpec((V7X_LANES, V7X_LANES), lambda b, h: (0, 0))],
        out_specs=[out_spec, out_spec, out_spec],
        out_shape=[jax.ShapeDtypeStruct((bsz, seq, d), F32), jax.ShapeDtypeStruct((bsz, seq, d), F32),
                   jax.ShapeDtypeStruct((bsz, seq, d), BF16)],
        scratch_shapes=[pltpu.VMEM((seq, V7X_LANES), BF16), pltpu.VMEM((SPAN + seq, V7X_LANES), BF16),
                        pltpu.VMEM((SPAN + seq, V7X_LANES), BF16), nat, nat, nat, nat, nat, nat, nat, nat, nat, nat,
                        pltpu.VMEM((2, 2 * SPAN, 2 * SPAN), F32)],
        compiler_params=_params(2),
    )(pview, pview, pview, kview, kview, pview, gains, *tables, ones)
    return o.reshape(t, d), lse.reshape(t, d), ao.reshape(t, d)


def _attn_bwd(proj_b, kv, dao, o, lse, gains, tables, ones, bsz, seq, name):
    t, d4 = proj_b.shape
    d = d4 // 4
    nhp = d // V7X_LANES
    nblk = seq // SPAN
    scale = HEAD_DIM ** -0.5
    n_groups = len(DILATIONS)
    n_chunks = seq // ATTN_PIECE

    def body(q_ref, k_ref, v_ref, gate_ref, dao_ref, o_ref, l_ref, gain_ref, ta_ref, tb_ref, tc_ref, e_ref,
             dproj_ref, dkv_ref, dg_ref, qd, kd, vd, dod, std, dqd, dkd, dvd, dqn, dk0, dk1, dk2, dv0, dv1, dv2,
             kn, kxh, krr, qn, qxh, qrr, don, stn, bias):
        head0 = lax.broadcasted_iota(jnp.int32, (SPAN, V7X_LANES), 1) < HEAD_DIM
        g = pl.program_id(2)

        @pl.when(jnp.logical_and(jnp.logical_and(pl.program_id(0) == 0, pl.program_id(1) == 0), g == 0))
        def _():
            _set_bias(bias)
            dg_ref[...] = jnp.zeros_like(dg_ref)

        @pl.when(g == 0)
        def _():
            first_half = (lax.broadcasted_iota(jnp.int32, (ATTN_PIECE, V7X_LANES), 1) & (HEAD_DIM - 1)) < HEAD_DIM // 2
            _norm_rope_rows(kn, k_ref, gain_ref[n_groups:n_groups + 1, :], ta_ref, tb_ref, tc_ref, e_ref, seq,
                            kxh, krr)
            for r0 in range(0, seq, ATTN_PIECE):
                rows = slice(r0, r0 + ATTN_PIECE)
                gate = gate_ref[rows, :]
                dov = dao_ref[rows, :].astype(F32) * gate * _sig(gate)
                don[rows, :] = dov
                stn[rows, :] = jnp.where(first_half, l_ref[rows, :], _segsum(dov * o_ref[rows, :], e_ref))

        def norm_bwd_chunks(xhat_ref, r_ref, dn_refs, out_ref, gi):
            def chunk(ci, carry):
                rows = pl.ds(pl.multiple_of(ci * ATTN_PIECE, ATTN_PIECE), ATTN_PIECE)
                dn = functools.reduce(lambda u, w: u + w, [r_[rows, :] for r_ in dn_refs])
                dx, part = _norm_rope_bwd(xhat_ref[rows, :], r_ref[rows, :], dn, gain_ref[gi:gi + 1, :],
                                          ta_ref[rows, :], tb_ref[rows, :], tc_ref[rows, :], e_ref)
                out_ref[rows, :] = dx.astype(BF16)
                dg_ref[gi] += part
                return carry
            lax.fori_loop(0, n_chunks, chunk, 0, unroll=CHUNK_UNROLL)

        def group(gi):
            dil = DILATIONS[gi]
            nb = seq // dil // SPAN
            kd[0:SPAN, :] = jnp.zeros((SPAN, V7X_LANES), BF16)
            vd[0:SPAN, :] = jnp.zeros((SPAN, V7X_LANES), BF16)
            dkd[...] = jnp.zeros_like(dkd)
            dvd[...] = jnp.zeros_like(dvd)
            _norm_rope_rows(qn, q_ref, gain_ref[gi:gi + 1, :], ta_ref, tb_ref, tc_ref, e_ref, seq, qxh, qrr)
            for ns, rs, rows in _pieces(dil, seq):
                qd[rs:rs + rows, :] = _strided(qn, ns, rows, dil).astype(BF16)
                kd[SPAN + rs:SPAN + rs + rows, :] = _strided(kn, ns, rows, dil).astype(BF16)
                vd[SPAN + rs:SPAN + rs + rows, :] = _strided(v_ref, ns, rows, dil).astype(BF16)
                dod[rs:rs + rows, :] = _strided(don, ns, rows, dil).astype(BF16)
                std[rs:rs + rows, :] = _strided(stn, ns, rows, dil)

            def block(j, carry):
                qs = pl.multiple_of(j * SPAN, SPAN)
                q2 = _stack_heads(qd[pl.ds(qs, SPAN), :], head0)
                do2 = _stack_heads(dod[pl.ds(qs, SPAN), :], head0)
                keys, mask = _block_keys(bias, j, qs, nb)
                kk = kd[keys, :]
                vv = vd[keys, :]
                s = _nt(q2, kk) * scale + mask
                stv = std[pl.ds(qs, SPAN), :]
                p = jnp.exp(s - _head_cols(stv))
                ds = (p * (_nt(do2, vv) - _head_cols(stv, HEAD_DIM // 2)) * scale).astype(BF16)
                dqd[pl.ds(qs, SPAN), :] = _unstack_heads(jnp.dot(ds, kk, preferred_element_type=F32), head0)
                dkd[keys, :] += _tn(ds, q2)
                dvd[keys, :] += _tn(p.astype(BF16), do2)
                return carry

            lax.fori_loop(0, nblk, block, 0, unroll=ATTN_UNROLL)
            for ns, rs, rows in _pieces(dil, seq):
                _strided_set(dqn, ns, rows, dil, dqd[rs:rs + rows, :])
                _strided_set((dk0, dk1, dk2)[gi], ns, rows, dil, dkd[SPAN + rs:SPAN + rs + rows, :])
                _strided_set((dv0, dv1, dv2)[gi], ns, rows, dil, dvd[SPAN + rs:SPAN + rs + rows, :])
            norm_bwd_chunks(qxh, qrr, [dqn], dproj_ref, gi)

        for gi in range(n_groups):
            @pl.when(g == gi)
            def _():
                group(gi)

        @pl.when(g == n_groups - 1)
        def _():
            norm_bwd_chunks(kxh, krr, [dk0, dk1, dk2], dkv_ref, n_groups)

        @pl.when(g == n_groups)
        def _():
            def chunk(ci, carry):
                rows = pl.ds(pl.multiple_of(ci * ATTN_PIECE, ATTN_PIECE), ATTN_PIECE)
                gate = gate_ref[rows, :]
                sg = _sig(gate)
                dproj_ref[rows, :] = (dao_ref[rows, :].astype(F32) * o_ref[rows, :]
                                      * (sg * (1.0 + gate * (1.0 - sg)))).astype(BF16)
                dkv_ref[rows, :] = (dv0[rows, :] + dv1[rows, :] + dv2[rows, :]).astype(BF16)
                return carry
            lax.fori_loop(0, n_chunks, chunk, 0, unroll=CHUNK_UNROLL)

    blk = (None, seq, V7X_LANES)
    pview = proj_b.reshape(bsz, seq, d4)
    kview = kv.reshape(bsz, seq, 2 * d)
    dview = (bsz, seq, d)
    d_spec = pl.BlockSpec(blk, lambda b, h, g: (b, 0, h))
    tab = pl.BlockSpec((seq, V7X_LANES), lambda b, h, g: (0, 0))
    nat = pltpu.VMEM((seq, V7X_LANES), F32)
    natb = pltpu.VMEM((seq, V7X_LANES), BF16)
    pad = pltpu.VMEM((SPAN + seq, V7X_LANES), F32)
    padb = pltpu.VMEM((SPAN + seq, V7X_LANES), BF16)
    dproj, dkv, dg = pl.pallas_call(
        body, name=name, grid=(bsz, nhp, n_groups + 1),
        in_specs=[pl.BlockSpec(blk, lambda b, h, g: (b, 0, jnp.minimum(g, n_groups - 1) * nhp + h)),
                  pl.BlockSpec(blk, lambda b, h, g: (b, 0, h)),
                  pl.BlockSpec(blk, lambda b, h, g: (b, 0, nhp + h)),
                  pl.BlockSpec(blk, lambda b, h, g: (b, 0, n_groups * nhp + h)),
                  d_spec, d_spec, d_spec,
                  pl.BlockSpec((n_groups + 1, V7X_LANES), lambda b, h, g: (0, 0)),
                  tab, tab, tab,
                  pl.BlockSpec((V7X_LANES, V7X_LANES), lambda b, h, g: (0, 0))],
        out_specs=[pl.BlockSpec(blk, lambda b, h, g: (b, 0, g * nhp + h)),
                   pl.BlockSpec(blk, lambda b, h, g: (b, 0, (g // n_groups) * nhp + h)),
                   pl.BlockSpec((n_groups + 1, V7X_SUBLANES, V7X_LANES), lambda b, h, g: (0, 0, 0))],
        out_shape=[jax.ShapeDtypeStruct((bsz, seq, d4), BF16), jax.ShapeDtypeStruct((bsz, seq, 2 * d), BF16),
                   jax.ShapeDtypeStruct((n_groups + 1, V7X_SUBLANES, V7X_LANES), F32)],
        scratch_shapes=[natb, padb, padb, natb, nat, nat, pad, pad] + [nat] * 15 + [
                        pltpu.VMEM((2, 2 * SPAN, 2 * SPAN), F32)],
        compiler_params=_params(3),
    )(pview, kview, kview, pview, dao.reshape(dview), o.reshape(dview), lse.reshape(dview), gains, *tables, ones)
    dgain = dg.sum(axis=1).reshape(n_groups + 1, V7X_LANES // HEAD_DIM, HEAD_DIM).sum(axis=1)
    return dproj.reshape(t, d4), dkv.reshape(t, 2 * d), dgain


def _mesh_position():
    x, y, c = lax.axis_index("x"), lax.axis_index("y"), lax.axis_index("c")
    return x, y, c


def _peer(x, y, c, rel):
    return (1 - x if rel & 4 else x, 1 - y if rel & 2 else y, 1 - c if rel & 1 else c)


class _Exchange:
    def __init__(self, srcs, gather):
        self.srcs = list(srcs)
        self.gather = gather
        n = self.n = len(self.srcs)
        hbm = pl.BlockSpec(memory_space=pltpu.HBM)
        self.in_specs = [hbm] * n
        self.out_specs = [hbm] * n
        self.out_shape = [jax.ShapeDtypeStruct(((N_DEV,) + a.shape) if gather else a.shape, a.dtype)
                          for a in self.srcs]
        self.scratch = [pltpu.SemaphoreType.DMA((n * (N_DEV - 1),)), pltpu.SemaphoreType.DMA((n * (N_DEV - 1),)),
                        pltpu.SemaphoreType.DMA((n,))]

    def _copies(self, ins, outs, sems):
        send_sems, recv_sems, local_sems = sems
        x, y, c = _mesh_position()
        me = 4 * x + 2 * y + c
        remote, local = [], []
        for a in range(self.n):
            mine = ins[a] if self.gather else ins[a].at[me]
            local.append(pltpu.make_async_copy(mine, outs[a].at[me], local_sems.at[a]))
            for rel in range(1, N_DEV):
                px, py, pc = _peer(x, y, c, rel)
                s = a * (N_DEV - 1) + rel - 1
                src = ins[a] if self.gather else ins[a].at[4 * px + 2 * py + pc]
                remote.append(pltpu.make_async_remote_copy(
                    src_ref=src, dst_ref=outs[a].at[me], send_sem=send_sems.at[s], recv_sem=recv_sems.at[s],
                    device_id=(px, py, pc), device_id_type=pl.DeviceIdType.MESH))
        return remote, local

    def start(self, ins, outs, sems):
        remote, local = self._copies(ins, outs, sems)
        for cp in local + remote:
            cp.start()

    def wait(self, ins, outs, sems):
        remote, local = self._copies(ins, outs, sems)
        for cp in remote:
            cp.wait_recv()
        for cp in remote:
            cp.wait_send()
        for cp in local:
            cp.wait()


def _gather_chip_once(arrs, name):
    n = len(arrs)
    per = N_DEV - 1

    def body(*refs):
        ins, outs = refs[:n], refs[n:2 * n]
        send_sems, recv_sems, local_sems = refs[2 * n:]
        x, y, c = _mesh_position()
        me, sibling = (x, y, c), (x, y, 1 - c)
        chips = [(1 - x, y), (x, 1 - y), (1 - x, 1 - y)]

        def copy(a, k, block, to, src=None):
            bx, by, bc = block
            dst = outs[a].at[4 * bx + 2 * by + bc]
            return pltpu.make_async_remote_copy(
                src_ref=dst if src is None else src, dst_ref=dst, send_sem=send_sems.at[a * per + k],
                recv_sem=recv_sems.at[a * per + k], device_id=to, device_id_type=pl.DeviceIdType.MESH)

        local, sent = [], []
        for a in range(n):
            mine = pltpu.make_async_copy(ins[a], outs[a].at[4 * x + 2 * y + c], local_sems.at[a])
            mine.start()
            local.append(mine)
            first = [copy(a, 0, me, sibling, src=ins[a])]
            first += [copy(a, 1 + j, me, chip + (c,), src=ins[a]) for j, chip in enumerate(chips)]
            for cp in first:
                cp.start()
            sent += first
        for a in range(n):
            for j, chip in enumerate(chips):
                copy(a, 1 + j, chip + (c,), me).wait_recv()
                passed = copy(a, 4 + j, chip + (c,), sibling)
                passed.start()
                sent.append(passed)
        for a in range(n):
            copy(a, 0, sibling, me).wait_recv()
            for j, chip in enumerate(chips):
                copy(a, 4 + j, chip + (1 - c,), me).wait_recv()
        for cp in sent:
            cp.wait_send()
        for cp in local:
            cp.wait()

    hbm = pl.BlockSpec(memory_space=pltpu.HBM)
    return pl.pallas_call(
        body, name=name, in_specs=[hbm] * n, out_specs=[hbm] * n,
        out_shape=[jax.ShapeDtypeStruct((N_DEV,) + a.shape, a.dtype) for a in arrs],
        scratch_shapes=[pltpu.SemaphoreType.DMA((n * per,)), pltpu.SemaphoreType.DMA((n * per,)),
                        pltpu.SemaphoreType.DMA((n,))],
    )(*arrs)


def _run_exchange(ex, name):
    n = ex.n

    def body(*refs):
        ins, outs, sems = refs[:n], refs[n:2 * n], refs[2 * n:]
        ex.start(ins, outs, sems)
        ex.wait(ins, outs, sems)

    return pl.pallas_call(body, name=name, in_specs=ex.in_specs, out_specs=ex.out_specs, out_shape=ex.out_shape,
                          scratch_shapes=ex.scratch)(*ex.srcs)


def _hosted_call(body, ex, name, grid, in_specs, out_specs, out_shape, scratch_shapes, args):
    if ex is None:
        outs = pl.pallas_call(body, name=name, grid=grid, in_specs=in_specs, out_specs=out_specs, out_shape=out_shape,
                              scratch_shapes=scratch_shapes, compiler_params=_params(len(grid)))(*args)
        return list(outs), []
    n_in, n_out, n_scr, n = len(in_specs), len(out_specs), len(scratch_shapes), ex.n

    def hosted(*refs):
        h_in, e_in = refs[:n_in], refs[n_in:n_in + n]
        o0 = n_in + n
        h_out, e_out = refs[o0:o0 + n_out], refs[o0 + n_out:o0 + n_out + n]
        s0 = o0 + n_out + n
        h_scr, e_scr = refs[s0:s0 + n_scr], refs[s0 + n_scr:]
        ids = [pl.program_id(a) for a in range(len(grid))]
        first = functools.reduce(jnp.logical_and, [i == 0 for i in ids])
        last = functools.reduce(jnp.logical_and, [i == g - 1 for i, g in zip(ids, grid)])

        @pl.when(first)
        def _():
            ex.start(e_in, e_out, e_scr)

        body(*h_in, *h_out, *h_scr)

        @pl.when(last)
        def _():
            ex.wait(e_in, e_out, e_scr)

    outs = pl.pallas_call(
        hosted, name=name, grid=grid, in_specs=list(in_specs) + ex.in_specs,
        out_specs=list(out_specs) + ex.out_specs, out_shape=list(out_shape) + ex.out_shape,
        scratch_shapes=list(scratch_shapes) + ex.scratch, compiler_params=_params(len(grid)),
    )(*args, *ex.srcs)
    return list(outs[:n_out]), list(outs[n_out:])


def _sum_adamw(parts, w, m, v, name):
    _, r, wd = parts.shape
    tr = _pick(r, ADAM_ROWS, 8)
    c1 = 1.0 - ADAM_B1 ** ADAM_STEP
    c2 = 1.0 - ADAM_B2 ** ADAM_STEP

    def body(p_ref, w_ref, m_ref, v_ref, g_ref, d_ref, nm_ref, nv_ref):
        g = p_ref[0].astype(F32)
        for s in range(1, N_DEV):
            g = g + p_ref[s].astype(F32)
        nm = ADAM_B1 * m_ref[...] + (1.0 - ADAM_B1) * g
        nv = ADAM_B2 * v_ref[...] + (1.0 - ADAM_B2) * (g * g)
        g_ref[...] = g
        nm_ref[...] = nm
        nv_ref[...] = nv
        d_ref[...] = -ADAM_LR * ((nm / c1) / (jnp.sqrt(nv / c2) + ADAM_EPS) + ADAM_WD * w_ref[...])

    row = pl.BlockSpec((tr, wd), lambda i: (i, 0))
    return pl.pallas_call(
        body, name=name, grid=(r // tr,),
        in_specs=[pl.BlockSpec((N_DEV, tr, wd), lambda i: (0, i, 0)), row, row, row],
        out_specs=[row] * 4, out_shape=[jax.ShapeDtypeStruct((r, wd), F32)] * 4,
        compiler_params=_params(1),
    )(parts, w, m, v)


def _pack_rows(size, row_mult):
    rows = -(-size // PACK_LANES)
    return -(-rows // row_mult) * row_mult


def _pack(flats, row_mult, dtype, total_mult=None):
    out = []
    for f in flats:
        size = f.shape[-1]
        rows = _pack_rows(size, row_mult)
        pad = [(0, 0)] * (f.ndim - 1) + [(0, rows * PACK_LANES - size)]
        out.append(jnp.pad(f.astype(dtype), pad).reshape(f.shape[:-1] + (rows, PACK_LANES)))
    if total_mult is not None:
        total = sum(o.shape[-2] for o in out)
        extra = -(-total // total_mult) * total_mult - total
        if extra:
            out.append(jnp.zeros(out[0].shape[:-2] + (extra, PACK_LANES), dtype))
    return jnp.concatenate(out, axis=-2)


def _unpack(buf, sizes, row_mult):
    out, row = [], 0
    for size in sizes:
        rows = _pack_rows(size, row_mult)
        part = buf[..., row:row + rows, :]
        out.append(part.reshape(buf.shape[:-2] + (rows * PACK_LANES,))[..., :size])
        row += rows
    return out


def _to_slots(full, axis):
    if axis is None:
        return jnp.broadcast_to(full.reshape(1, -1), (N_DEV, full.size))
    shape = full.shape
    split = full.reshape(shape[:axis] + (N_DEV, shape[axis] // N_DEV) + shape[axis + 1:])
    return jnp.moveaxis(split, axis, 0).reshape(N_DEV, -1)


def _from_slots(slots, axis, block_shape):
    split = jnp.moveaxis(slots, 0, axis)
    shape = list(block_shape)
    shape[axis] *= N_DEV
    return split.reshape(shape)


def kernel(x, p, norm_g, w_in_a, conv_w, conv_b, ln_g, ln_b, w_out_a, kv_norm_g, w_kv, k_norm_g, w_in_b, q_norm_g, w_out_b, ple_norm_g, w_ple_gate, w_ple_proj, loss_target, m_norm_g, m_w_in_a, m_conv_w, m_conv_b, m_ln_g, m_ln_b, m_w_out_a, m_kv_norm_g, m_w_kv, m_k_norm_g, m_w_in_b, m_q_norm_g, m_w_out_b, m_ple_norm_g, m_w_ple_gate, m_w_ple_proj, v_norm_g, v_w_in_a, v_conv_w, v_conv_b, v_ln_g, v_ln_b, v_w_out_a, v_kv_norm_g, v_w_kv, v_k_norm_g, v_w_in_b, v_q_norm_g, v_w_out_b, v_ple_norm_g, v_w_ple_gate, v_w_ple_proj):
    weights = dict(zip(WEIGHT_NAMES, (norm_g, w_in_a, conv_w, conv_b, ln_g, ln_b, w_out_a, kv_norm_g, w_kv, k_norm_g,
                                      w_in_b, q_norm_g, w_out_b, ple_norm_g, w_ple_gate, w_ple_proj)))
    mom_m = dict(zip(WEIGHT_NAMES, (m_norm_g, m_w_in_a, m_conv_w, m_conv_b, m_ln_g, m_ln_b, m_w_out_a, m_kv_norm_g,
                                    m_w_kv, m_k_norm_g, m_w_in_b, m_q_norm_g, m_w_out_b, m_ple_norm_g, m_w_ple_gate,
                                    m_w_ple_proj)))
    mom_v = dict(zip(WEIGHT_NAMES, (v_norm_g, v_w_in_a, v_conv_w, v_conv_b, v_ln_g, v_ln_b, v_w_out_a, v_kv_norm_g,
                                    v_w_kv, v_k_norm_g, v_w_in_b, v_q_norm_g, v_w_out_b, v_ple_norm_g, v_w_ple_gate,
                                    v_w_ple_proj)))
    bsz, seq, d = x.shape
    t = bsz * seq
    assert seq % (max(DILATIONS) * SPAN) == 0 and d % V7X_LANES == 0

    full = {}

    def rows2d(a):
        return a.reshape(-1, a.shape[-1])

    def packed(source, names, dtype, total_mult=None):
        return _pack([source[n].reshape(-1) for n in names], 16, dtype, total_mult)

    def gathered(names, bufs):
        for n, buf in zip(names, bufs):
            full[n] = _from_slots(buf.reshape((N_DEV,) + weights[n].shape), SHARD_AXIS[n], weights[n].shape)

    w1_all, wv_all = _gather_chip_once([rows2d(weights['w_in_a']).astype(BF16),
                                        _pack([weights[n].reshape(-1) for n in VECTOR_WEIGHTS], 8, F32)],
                                       "gather_first")
    gathered(GROUP_FIRST, [w1_all])
    for n, slots in zip(VECTOR_WEIGHTS, _unpack(wv_all, [weights[n].size for n in VECTOR_WEIGHTS], 8)):
        full[n] = _from_slots(slots.reshape((N_DEV,) + weights[n].shape), SHARD_AXIS[n], weights[n].shape)
    gather_rest = _Exchange([rows2d(weights[n]).astype(BF16) for n in GROUP_REST], gather=True)
    wa_in = full['w_in_a'][0]
    cw, cb, lg, lb = full['conv_w'][0], full['conv_b'], full['ln_g'], full['ln_b']

    tables = _rope_tables(seq)
    ones = _head_ones(V7X_LANES)
    rep = V7X_LANES // HEAD_DIM
    head_gain = jnp.concatenate([jnp.tile(q_norm_g[0], (1, rep)), jnp.tile(k_norm_g, rep)[None]], axis=0)

    x0 = x.reshape(t, d)
    p0, p1 = p[0].reshape(t, -1), p[1].reshape(t, -1)
    target = loss_target.reshape(t, d)
    g_norm0, g_norm1 = norm_g[0:1], norm_g[1:2]
    g_ple0, g_ple1 = ple_norm_g[0:1], ple_norm_g[1:2]
    g_kv = kv_norm_g.reshape(1, d)

    (u0,) = _rmsnorm_fwd(x0, [g_norm0], "norm0")
    proj_a = _matmul(u0, wa_in, 'nn', "in_a")
    m_act, y_conv, w2_all = _conv_fwd(proj_a, cw, cb, lg, lb, seq, "conv_fwd", ex=gather_rest)
    gathered(GROUP_REST, w2_all)
    wa_out = full['w_out_a'][0]
    wkv = full['w_kv']
    wb_in, wb_out = full['w_in_b'][0], full['w_out_b'][0]
    wg, wp = full['w_ple_gate'], full['w_ple_proj']
    h0, pg0 = _matmul(m_act, wa_out, 'nn', "out_a", add=x0, norm_gain=g_ple0)
    gl0 = _matmul(pg0, wg[0], 'nn', "ple_gate0", out_dtype=BF16)
    pp0 = _matmul(p0, wp[0], 'nn', "ple_proj0", out_dtype=BF16)

    x1, (kvn, u1) = _ple_norm_fwd(h0, gl0, pp0, [g_kv, g_norm1], "ple0_norm1")
    kv = _matmul(kvn, wkv, 'nn', "kv")
    proj_b = _matmul(u1, wb_in, 'nn', "in_b")
    o_att, lse, ao = _attn_fwd(proj_b, kv, head_gain, tables, ones, bsz, seq, "attn_fwd")
    h1, pg1 = _matmul(ao, wb_out, 'nn', "out_b", add=x1, norm_gain=g_ple1)
    gl1 = _matmul(pg1, wg[1], 'nn', "ple_gate1", out_dtype=BF16)
    pp1 = _matmul(p1, wp[1], 'nn', "ple_proj1", out_dtype=BF16)

    dx2, dgl1, dpp1, loss_part = _ple_loss(h1, gl1, pp1, target, "ple1_loss")
    loss = lax.psum(jnp.sum(loss_part), ("x", "y", "c"))

    grads = {}
    slot = {}

    dwp1 = _matmul(p1, dpp1, 'tn', "d_ple_proj1", out_dtype=BF16, slot_cols=d // N_DEV)
    dwg1 = _matmul(pg1, dgl1, 'tn', "d_ple_gate1", out_dtype=BF16)
    dpg1 = _matmul(dgl1, wg[1], 'nt', "d_ple_norm1", out_dtype=BF16)
    dh1, (dg_ple1,) = _rmsnorm_bwd(h1, [g_ple1], [dpg1], dx2, "ple_norm1_bwd")
    slot['w_out_b'] = _matmul(ao, dh1, 'tn', "d_out_b", out_dtype=BF16).reshape(N_DEV, -1, d)
    dao = _matmul(dh1, wb_out, 'nt', "d_ao", out_dtype=BF16)
    dproj_b, dkv, dg_head = _attn_bwd(proj_b, kv, dao, o_att, lse, head_gain, tables, ones, bsz, seq, "attn_bwd")
    slot['w_in_b'] = _matmul(u1, dproj_b, 'tn', "d_in_b", out_dtype=BF16, slot_cols=4 * d // N_DEV)
    du1 = _matmul(dproj_b, wb_in, 'nt', "d_u1", out_dtype=BF16)
    slot['w_kv'] = _matmul(kvn, dkv, 'tn', "d_kv", out_dtype=BF16, slot_cols=2 * d // N_DEV)
    dkvn = _matmul(dkv, wkv, 'nt', "d_kvn", out_dtype=BF16)
    dx1, (dg_kv, dg_norm1), dgl0, dpp0 = _rmsnorm_bwd(x1, [g_kv, g_norm1], [dkvn, du1], dh1, "norm1_bwd",
                                                      ple=(gl0, pp0))

    dwp0 = _matmul(p0, dpp0, 'tn', "d_ple_proj0", out_dtype=BF16, slot_cols=d // N_DEV)
    dwg0 = _matmul(pg0, dgl0, 'tn', "d_ple_gate0", out_dtype=BF16)
    dpg0 = _matmul(dgl0, wg[0], 'nt', "d_ple_norm0", out_dtype=BF16)
    dh0, (dg_ple0,) = _rmsnorm_bwd(h0, [g_ple0], [dpg0], dx1, "ple_norm0_bwd")
    slot['w_out_a'] = _matmul(m_act, dh0, 'tn', "d_out_a", out_dtype=BF16).reshape(N_DEV, -1, d)
    dm = _matmul(dh0, wa_out, 'nt', "d_m", out_dtype=BF16)
    dy_conv, dz, d_lg, d_lb, d_cb = _ln_gate_bwd(dm, y_conv, proj_a, lg, lb, "ln_gate_bwd")
    slot['w_ple_gate'] = jnp.stack([dwg0.reshape(N_DEV, -1, d), dwg1.reshape(N_DEV, -1, d)],
                                   axis=1).reshape(N_DEV, -1, d)
    slot['w_ple_proj'] = jnp.stack([dwp0, dwp1], axis=1).reshape(N_DEV, -1, d // N_DEV)

    dproj_a, d_cw, parts_rest = _conv_bwd(dy_conv, dz, proj_a, cw, seq, "conv_bwd",
                                          ex=_Exchange([slot[n] for n in GROUP_REST], gather=False))
    slot['w_in_a'] = _matmul(u0, dproj_a, 'tn', "d_in_a", out_dtype=BF16, slot_cols=wa_in.shape[1] // N_DEV)
    du0, parts_first = _matmul(dproj_a, wa_in, 'nt', "d_u0", out_dtype=BF16,
                               ex=_Exchange([slot['w_in_a']], gather=False))
    dx0, (dg_norm0,) = _rmsnorm_bwd(x0, [g_norm0], [du0], dh0, "norm0_bwd")

    grads['norm_g'] = jnp.stack([dg_norm0, dg_norm1])
    grads['conv_w'] = d_cw[None]
    grads['conv_b'] = d_cb[None]
    grads['ln_g'] = d_lg[None]
    grads['ln_b'] = d_lb[None]
    grads['kv_norm_g'] = dg_kv
    grads['k_norm_g'] = dg_head[3]
    grads['q_norm_g'] = dg_head[0:3][None]
    grads['ple_norm_g'] = jnp.stack([dg_ple0, dg_ple1])
    small_pack = _pack([_to_slots(grads[n], SHARD_AXIS[n]) for n in GROUP_SMALL], 16, BF16)
    (parts_small,) = _run_exchange(_Exchange([small_pack], gather=False), "exchange_small")

    updated = {}
    for n, parts in zip(GROUP_REST + GROUP_FIRST, parts_rest + parts_first):
        outs = _sum_adamw(parts, rows2d(weights[n]), rows2d(mom_m[n]), rows2d(mom_v[n]), "sum_adamw_" + n)
        for kind, buf in enumerate(outs):
            updated[kind, n] = buf.reshape(weights[n].shape)
    outs = _sum_adamw(parts_small, packed(weights, GROUP_SMALL, F32), packed(mom_m, GROUP_SMALL, F32),
                      packed(mom_v, GROUP_SMALL, F32), "sum_adamw_small")
    sizes = [weights[n].size for n in GROUP_SMALL]
    for kind, buf in enumerate(outs):
        for n, flat in zip(GROUP_SMALL, _unpack(buf, sizes, 16)):
            updated[kind, n] = flat.reshape(weights[n].shape)
    result = [loss, dx0.reshape(bsz, seq, d)]
    for kind in range(4):
        result.extend(updated[kind, n] for n in WEIGHT_NAMES)
    return tuple(result)
```

```python
import functools

import jax
import jax.numpy as jnp
from jax import lax
from jax.experimental import pallas as pl
from jax.experimental.pallas import tpu as pltpu

F32 = jnp.float32
BF16 = jnp.bfloat16

N_DEV = 8
HEAD_DIM = 64
ROPE_DIM = 16
ROPE_THETA = 500000.0
EPS = 1e-6
NEG_INF = -1e30
SPAN = 128
DILATIONS = (1, 4, 16)
CONV_WIDTH = 31
HALO = 32
CONV_ROWS = 32
CONV_W_ROWS = 64
PACK_LANES = 1024
V7X_LANES = 128
V7X_SUBLANES = 8
VMEM_LIMIT_BYTES = 56 * 1024 * 1024

ADAM_LR = 0.001
ADAM_B1 = 0.9
ADAM_B2 = 0.999
ADAM_EPS = 1e-08
ADAM_WD = 0.01
ADAM_STEP = 10
ADAM_ROWS = 256

WEIGHT_NAMES = ('norm_g', 'w_in_a', 'conv_w', 'conv_b', 'ln_g', 'ln_b', 'w_out_a', 'kv_norm_g', 'w_kv',
                'k_norm_g', 'w_in_b', 'q_norm_g', 'w_out_b', 'ple_norm_g', 'w_ple_gate', 'w_ple_proj')
SHARD_AXIS = {'norm_g': None, 'w_in_a': 2, 'conv_w': 2, 'conv_b': 1, 'ln_g': 1, 'ln_b': 1, 'w_out_a': 1,
              'kv_norm_g': None, 'w_kv': 1, 'k_norm_g': None, 'w_in_b': 2, 'q_norm_g': None, 'w_out_b': 1,
              'ple_norm_g': None, 'w_ple_gate': 1, 'w_ple_proj': 2}
VECTOR_WEIGHTS = ('conv_w', 'conv_b', 'ln_g', 'ln_b')
GROUP_FIRST = ('w_in_a',)
GROUP_REST = ('w_out_a', 'w_kv', 'w_in_b', 'w_out_b', 'w_ple_gate', 'w_ple_proj')
GROUP_SMALL = ('norm_g', 'conv_w', 'conv_b', 'ln_g', 'ln_b', 'kv_norm_g', 'k_norm_g', 'q_norm_g', 'ple_norm_g')


def _pick(n, target, mult):
    t = (min(target, n) // mult) * mult
    while t >= mult:
        if n % t == 0:
            return t
        t -= mult
    return n


def _params(n_grid):
    return pltpu.CompilerParams(dimension_semantics=("arbitrary",) * n_grid, vmem_limit_bytes=VMEM_LIMIT_BYTES)


def _sig(x):
    return 0.5 * jnp.tanh(0.5 * x) + 0.5


def _colsum8(v):
    r, w = v.shape
    return v.reshape(r // V7X_SUBLANES, V7X_SUBLANES, w).sum(axis=0)


def _rows(tm, w, col=0):
    return pl.BlockSpec((tm, w), lambda i: (i, col))


def _const(shape):
    nd = len(shape)
    return pl.BlockSpec(shape, lambda i: (0,) * nd)


def _segsum(v, e_ref):
    hi = v.astype(BF16)
    lo = (v - hi.astype(F32)).astype(BF16)
    e = e_ref[...]
    return jnp.dot(hi, e, preferred_element_type=F32) + jnp.dot(lo, e, preferred_element_type=F32)


MM_TILE = 1024
MM_TILE_K = 2048


def _matmul(a, b, mode, name, out_dtype=F32, add=None, ex=None, slot_cols=None, norm_gain=None):
    if mode == 'nn':
        (m, k), (_, n) = a.shape, b.shape
    elif mode == 'nt':
        (m, k), (n, _) = a.shape, b.shape
    else:
        (k, m), (_, n) = a.shape, b.shape
    out_struct = jax.ShapeDtypeStruct((m, n), out_dtype)
    n_slots = 0
    if mode == 'tn':
        tm, tn, tk = _pick(m, MM_TILE, 128), _pick(n, MM_TILE, 128), _pick(k, MM_TILE_K, 128)
        o_spec = pl.BlockSpec((tm, tn), lambda i, j, kk: (i, j))
        if slot_cols is not None:
            assert n == N_DEV * slot_cols
            n_slots = max(s for s in (1, 2, 4, 8) if s == 1 or slot_cols * s <= MM_TILE)
            tn = slot_cols * n_slots
            o_spec = pl.BlockSpec((n_slots, tm, slot_cols), lambda i, j, kk: (j, i, 0))
            out_struct = jax.ShapeDtypeStruct((N_DEV, m, slot_cols), out_dtype)
        grid = (m // tm, n // tn, k // tk)
        a_spec = pl.BlockSpec((tk, tm), lambda i, j, kk: (kk, i))
        b_spec = pl.BlockSpec((tk, tn), lambda i, j, kk: (kk, j))
        dims = (((0,), (0,)), ((), ()))
    else:
        tm, tn, tk = _pick(m, MM_TILE, 128), _pick(n, MM_TILE, 128), _pick(k, MM_TILE_K, 128)
        grid = (n // tn, m // tm, k // tk)
        a_spec = pl.BlockSpec((tm, tk), lambda j, i, kk: (i, kk))
        o_spec = pl.BlockSpec((tm, tn), lambda j, i, kk: (i, j))
        if mode == 'nn':
            b_spec = pl.BlockSpec((tk, tn), lambda j, i, kk: (kk, j))
            dims = (((1,), (0,)), ((), ()))
        else:
            b_spec = pl.BlockSpec((tn, tk), lambda j, i, kk: (j, kk))
            dims = (((1,), (1,)), ((), ()))
    nk = grid[2]
    has_add = add is not None
    has_norm = norm_gain is not None
    assert not has_norm or (tn == n and mode != 'tn')

    def body(*refs):
        a_ref, b_ref = refs[0], refs[1]
        add_ref = refs[2] if has_add else None
        gain_ref = refs[2 + has_add] if has_norm else None
        o_ref = refs[2 + has_add + has_norm]
        norm_ref = refs[3 + has_add + has_norm] if has_norm else None
        part = lax.dot_general(a_ref[...].astype(BF16), b_ref[...].astype(BF16), dims, preferred_element_type=F32)

        def finish(total):
            if has_add:
                total = total + add_ref[...]
            if n_slots:
                for s in range(n_slots):
                    o_ref[s] = total[:, s * slot_cols:(s + 1) * slot_cols].astype(out_dtype)
            else:
                o_ref[...] = total.astype(out_dtype)
            if has_norm:
                y = total * lax.rsqrt(jnp.mean(total * total, axis=-1, keepdims=True) + EPS)
                norm_ref[...] = (y * gain_ref[...]).astype(BF16)

        if nk == 1:
            finish(part)
        else:
            acc_ref = refs[3 + has_add + 2 * has_norm]
            kk = pl.program_id(2)

            @pl.when(kk == 0)
            def _():
                acc_ref[...] = part

            @pl.when(kk > 0)
            def _():
                acc_ref[...] += part

            @pl.when(kk == nk - 1)
            def _():
                finish(acc_ref[...])

    in_specs = [a_spec, b_spec] + ([o_spec] if has_add else [])
    args = [a, b] + ([add] if has_add else [])
    out_specs, out_structs = [o_spec], [out_struct]
    if has_norm:
        in_specs.append(pl.BlockSpec((1, n), lambda j, i, kk: (0, 0)))
        args.append(norm_gain)
        out_specs.append(o_spec)
        out_structs.append(jax.ShapeDtypeStruct((m, n), BF16))
    scratch = [pltpu.VMEM((tm, tn), F32)] if nk > 1 else []
    outs, moved = _hosted_call(body, ex, name, grid, in_specs, out_specs, out_structs, scratch, args)
    out = outs[0] if not has_norm else tuple(outs)
    return out if ex is None else (out, moved)


def _rmsnorm_fwd(x, gains, name):
    t, d = x.shape
    tm = _pick(t, 512, 8)
    n = len(gains)

    def body(*refs):
        x_ref, g_refs, o_refs = refs[0], refs[1:1 + n], refs[1 + n:]
        xv = x_ref[...]
        y = xv * lax.rsqrt(jnp.mean(xv * xv, axis=-1, keepdims=True) + EPS)
        for g_ref, o_ref in zip(g_refs, o_refs):
            o_ref[...] = (y * g_ref[...]).astype(BF16)

    return pl.pallas_call(
        body, name=name, grid=(t // tm,),
        in_specs=[_rows(tm, d)] + [_const((1, d))] * n,
        out_specs=[_rows(tm, d)] * n,
        out_shape=[jax.ShapeDtypeStruct((t, d), BF16)] * n,
        compiler_params=_params(1),
    )(x, *gains)


def _ple_grads(dx, gl, pp):
    sg = _sig(gl)
    return (dx * pp * sg * (1.0 - sg)).astype(BF16), (dx * sg).astype(BF16)


def _rmsnorm_bwd(x, gains, dys, add, name, ple=None, dx_dtype=BF16):
    t, d = x.shape
    tm = _pick(t, 512, 16)
    n = len(gains)
    n_ple = 0 if ple is None else 2

    def body(*refs):
        x_ref, add_ref = refs[0], refs[1]
        g_refs, dy_refs = refs[2:2 + n], refs[2 + n:2 + 2 * n]
        ple_refs = refs[2 + 2 * n:2 + 2 * n + n_ple]
        outs = refs[2 + 2 * n + n_ple:]
        dx_ref, dg_refs, dple_refs = outs[0], outs[1:1 + n], outs[1 + n:]
        i = pl.program_id(0)
        xv = x_ref[...]
        r = lax.rsqrt(jnp.mean(xv * xv, axis=-1, keepdims=True) + EPS)
        xhat = xv * r
        dx = add_ref[...].astype(F32)
        for g_ref, dy_ref, dg_ref in zip(g_refs, dy_refs, dg_refs):
            dy = dy_ref[...].astype(F32)
            dyg = dy * g_ref[...]
            dx = dx + r * (dyg - xhat * jnp.mean(dyg * xhat, axis=-1, keepdims=True))
            part = _colsum8(dy * xhat)

            @pl.when(i == 0)
            def _():
                dg_ref[...] = part

            @pl.when(i > 0)
            def _():
                dg_ref[...] += part

        dx_ref[...] = dx.astype(dx_dtype)
        if n_ple:
            dple_refs[0][...], dple_refs[1][...] = _ple_grads(dx, ple_refs[0][...].astype(F32),
                                                               ple_refs[1][...].astype(F32))

    outs = pl.pallas_call(
        body, name=name, grid=(t // tm,),
        in_specs=[_rows(tm, d), _rows(tm, d)] + [_const((1, d))] * n + [_rows(tm, d)] * (n + n_ple),
        out_specs=[_rows(tm, d)] + [_const((V7X_SUBLANES, d))] * n + [_rows(tm, d)] * n_ple,
        out_shape=([jax.ShapeDtypeStruct((t, d), dx_dtype)] + [jax.ShapeDtypeStruct((V7X_SUBLANES, d), F32)] * n
                   + [jax.ShapeDtypeStruct((t, d), BF16)] * n_ple),
        compiler_params=_params(1),
    )(x, add, *gains, *dys, *(ple or ()))
    dgs = [o.sum(axis=0) for o in outs[1:1 + n]]
    return (outs[0], dgs) if ple is None else (outs[0], dgs, outs[1 + n], outs[2 + n])


def _ple_norm_fwd(h, gl, pp, gains, name):
    t, d = h.shape
    tm = _pick(t, 512, 16)
    n = len(gains)

    def body(*refs):
        h_ref, gl_ref, pp_ref = refs[:3]
        g_refs, x_ref, o_refs = refs[3:3 + n], refs[3 + n], refs[4 + n:]
        xv = h_ref[...] + _sig(gl_ref[...].astype(F32)) * pp_ref[...].astype(F32)
        x_ref[...] = xv
        y = xv * lax.rsqrt(jnp.mean(xv * xv, axis=-1, keepdims=True) + EPS)
        for g_ref, o_ref in zip(g_refs, o_refs):
            o_ref[...] = (y * g_ref[...]).astype(BF16)

    outs = pl.pallas_call(
        body, name=name, grid=(t // tm,),
        in_specs=[_rows(tm, d)] * 3 + [_const((1, d))] * n, out_specs=[_rows(tm, d)] * (1 + n),
        out_shape=[jax.ShapeDtypeStruct((t, d), F32)] + [jax.ShapeDtypeStruct((t, d), BF16)] * n,
        compiler_params=_params(1),
    )(h, gl, pp, *gains)
    return outs[0], outs[1:]


def _ple_loss(h, gl, pp, target, name):
    t, d = h.shape
    tm = _pick(t, 512, 16)
    inv_d = 1.0 / d

    def body(h_ref, gl_ref, pp_ref, t_ref, dy_ref, dgl_ref, dpp_ref, l_ref):
        i = pl.program_id(0)
        gl, pp = gl_ref[...].astype(F32), pp_ref[...].astype(F32)
        e = h_ref[...] + _sig(gl) * pp - t_ref[...]
        dy = e * inv_d
        dy_ref[...] = dy.astype(BF16)
        dgl_ref[...], dpp_ref[...] = _ple_grads(dy, gl, pp)
        part = _colsum8(e * e) * (0.5 * inv_d)

        @pl.when(i == 0)
        def _():
            l_ref[...] = part

        @pl.when(i > 0)
        def _():
            l_ref[...] += part

    return pl.pallas_call(
        body, name=name, grid=(t // tm,), in_specs=[_rows(tm, d)] * 4,
        out_specs=[_rows(tm, d)] * 3 + [_const((V7X_SUBLANES, d))],
        out_shape=[jax.ShapeDtypeStruct((t, d), BF16), jax.ShapeDtypeStruct((t, d), BF16),
                   jax.ShapeDtypeStruct((t, d), BF16), jax.ShapeDtypeStruct((V7X_SUBLANES, d), F32)],
        compiler_params=_params(1),
    )(h, gl, pp, target)


def _shift_scratch(ts, cc):
    return pltpu.VMEM((V7X_SUBLANES, ts + HALO - V7X_SUBLANES, cc), F32)


def _shifted_copies(sh_ref, win_ref, cs, ts):
    rows = ts + HALO - V7X_SUBLANES
    for s in range(1, V7X_SUBLANES):
        sh_ref[s] = win_ref[pl.ds(s, rows), cs]


def _tap(sh_ref, win_ref, cs, offset, rows, r0):
    s = offset % V7X_SUBLANES
    start = pl.multiple_of(r0 + (offset - s), V7X_SUBLANES)
    if s == 0:
        return win_ref[pl.ds(start, rows), cs]
    return sh_ref[s, pl.ds(start, rows), :]


def _conv_fwd(proj, conv_w, conv_b, ln_g, ln_b, seq, name, ex=None):
    t, c3 = proj.shape
    c = c3 // 3
    ts = _pick(seq, 256, HALO)
    nsb = seq // ts
    cc = _pick(c, 512, V7X_LANES)
    hb = ts // HALO

    def body(a_ref, b_ref, z_ref, ap_ref, bp_ref, w_ref, cb_ref, g_ref, be_ref, m_ref, y_ref, win_ref, sh_ref):
        i = pl.program_id(0)
        first = (i % nsb) == 0
        win_ref[0:HALO, :] = jnp.where(first, 0.0, ap_ref[...] * _sig(bp_ref[...]))
        win_ref[HALO:, :] = a_ref[...] * _sig(b_ref[...])
        for ci in range(c // cc):
            cs = slice(ci * cc, (ci + 1) * cc)
            _shifted_copies(sh_ref, win_ref, cs, ts)

            def out_rows(rb, carry, cs=cs):
                r0 = rb * CONV_ROWS
                acc = jnp.zeros((CONV_ROWS, cc), F32) + cb_ref[:, cs]
                for k in range(CONV_WIDTH):
                    acc = acc + w_ref[k:k + 1, cs] * _tap(sh_ref, win_ref, cs, HALO - (CONV_WIDTH - 1) + k,
                                                           CONV_ROWS, r0)
                y_ref[pl.ds(pl.multiple_of(r0, CONV_ROWS), CONV_ROWS), cs] = acc
                return carry

            lax.fori_loop(0, ts // CONV_ROWS, out_rows, 0, unroll=2)
        y = y_ref[...]
        mu = jnp.mean(y, axis=-1, keepdims=True)
        xc = y - mu
        rstd = lax.rsqrt(jnp.mean(xc * xc, axis=-1, keepdims=True) + EPS)
        ln = xc * rstd * g_ref[...] + be_ref[...]
        zz = z_ref[...]
        m_ref[...] = (ln * _sig(ln) * zz * _sig(zz)).astype(BF16)

    halo_a = pl.BlockSpec((HALO, c), lambda i: (jnp.maximum(i * hb - 1, 0), 0))
    halo_b = pl.BlockSpec((HALO, c), lambda i: (jnp.maximum(i * hb - 1, 0), 1))
    (m_act, y), moved = _hosted_call(
        body, ex, name, (t // ts,),
        [_rows(ts, c, 0), _rows(ts, c, 1), _rows(ts, c, 2), halo_a, halo_b,
         _const((CONV_WIDTH, c)), _const((1, c)), _const((1, c)), _const((1, c))],
        [_rows(ts, c), _rows(ts, c)],
        [jax.ShapeDtypeStruct((t, c), BF16), jax.ShapeDtypeStruct((t, c), F32)],
        [pltpu.VMEM((HALO + ts, c), F32), _shift_scratch(ts, cc)],
        (proj, proj, proj, proj, proj, conv_w, conv_b, ln_g, ln_b))
    return m_act, y, moved


def _ln_gate_bwd(dm, y, proj, ln_g, ln_b, name):
    t, c = y.shape
    tm = _pick(t, 256, 8)

    def body(dm_ref, y_ref, z_ref, g_ref, be_ref, dy_ref, dz_ref, dg_ref, db_ref, dcb_ref):
        i = pl.program_id(0)
        yv = y_ref[...]
        mu = jnp.mean(yv, axis=-1, keepdims=True)
        xc = yv - mu
        rstd = lax.rsqrt(jnp.mean(xc * xc, axis=-1, keepdims=True) + EPS)
        xhat = xc * rstd
        g = g_ref[...]
        ln = xhat * g + be_ref[...]
        sl = _sig(ln)
        zz = z_ref[...]
        sz = _sig(zz)
        dmv = dm_ref[...].astype(F32)
        dz_ref[...] = (dmv * (ln * sl) * (sz * (1.0 + zz * (1.0 - sz)))).astype(BF16)
        dln = dmv * (zz * sz) * (sl * (1.0 + ln * (1.0 - sl)))
        dxh = dln * g
        dyv = rstd * (dxh - jnp.mean(dxh, axis=-1, keepdims=True)
                      - xhat * jnp.mean(dxh * xhat, axis=-1, keepdims=True))
        dy_ref[...] = dyv
        parts = (_colsum8(dln * xhat), _colsum8(dln), _colsum8(dyv))

        @pl.when(i == 0)
        def _():
            for ref, part in zip((dg_ref, db_ref, dcb_ref), parts):
                ref[...] = part

        @pl.when(i > 0)
        def _():
            for ref, part in zip((dg_ref, db_ref, dcb_ref), parts):
                ref[...] += part

    acc = jax.ShapeDtypeStruct((V7X_SUBLANES, c), F32)
    outs = pl.pallas_call(
        body, name=name, grid=(t // tm,),
        in_specs=[_rows(tm, c), _rows(tm, c), _rows(tm, c, 2), _const((1, c)), _const((1, c))],
        out_specs=[_rows(tm, c), _rows(tm, c)] + [_const((V7X_SUBLANES, c))] * 3,
        out_shape=[jax.ShapeDtypeStruct((t, c), F32), jax.ShapeDtypeStruct((t, c), BF16), acc, acc, acc],
        compiler_params=_params(1),
    )(dm, y, proj, ln_g, ln_b)
    return outs[0], outs[1], outs[2].sum(axis=0), outs[3].sum(axis=0), outs[4].sum(axis=0)


def _conv_bwd(dy, dz, proj, conv_w, seq, name, ex=None):
    t, c3 = proj.shape
    c = c3 // 3
    ts = _pick(seq, 256, HALO)
    nsb = seq // ts
    cc = _pick(c, 512, V7X_LANES)
    hb = ts // HALO
    last_halo = t // HALO - 1
    back = CONV_WIDTH - 1

    def body(dy_ref, dyn_ref, dz_ref, a_ref, b_ref, ap_ref, bp_ref, w_ref, o_ref, dw_ref, win_ref, dwin_ref,
             sh_ref, dsh_ref):
        i = pl.program_id(0)
        first = (i % nsb) == 0
        last = (i % nsb) == nsb - 1
        win_ref[0:HALO, :] = jnp.where(first, 0.0, ap_ref[...] * _sig(bp_ref[...]))
        win_ref[HALO:, :] = a_ref[...] * _sig(b_ref[...])
        dwin_ref[0:ts, :] = dy_ref[...]
        dwin_ref[ts:, :] = jnp.where(last, 0.0, dyn_ref[...])

        @pl.when(i == 0)
        def _():
            dw_ref[...] = jnp.zeros_like(dw_ref)

        for ci in range(c // cc):
            cs = slice(ci * cc, (ci + 1) * cc)
            _shifted_copies(sh_ref, win_ref, cs, ts)
            _shifted_copies(dsh_ref, dwin_ref, cs, ts)

            def in_grad_rows(rb, carry, cs=cs, ci=ci):
                r0 = rb * CONV_ROWS
                rows = pl.ds(pl.multiple_of(r0, CONV_ROWS), CONV_ROWS)
                dglu = jnp.zeros((CONV_ROWS, cc), F32)
                for k in range(CONV_WIDTH):
                    dglu = dglu + w_ref[k:k + 1, cs] * _tap(dsh_ref, dwin_ref, cs, back - k, CONV_ROWS, r0)
                sbc = _sig(b_ref[rows, cs])
                o_ref[rows, cs] = (dglu * sbc).astype(BF16)
                o_ref[rows, c + ci * cc:c + (ci + 1) * cc] = (dglu * a_ref[rows, cs] * sbc * (1.0 - sbc)).astype(BF16)
                return carry

            def w_grad_rows(rb, carry, cs=cs):
                r0 = rb * CONV_W_ROWS
                dcur = dwin_ref[pl.ds(pl.multiple_of(r0, CONV_W_ROWS), CONV_W_ROWS), cs]
                for k in range(CONV_WIDTH):
                    dw_ref[k * V7X_SUBLANES:(k + 1) * V7X_SUBLANES, cs] += _colsum8(
                        dcur * _tap(sh_ref, win_ref, cs, HALO - back + k, CONV_W_ROWS, r0))
                return carry

            lax.fori_loop(0, ts // CONV_ROWS, in_grad_rows, 0, unroll=2)
            lax.fori_loop(0, ts // CONV_W_ROWS, w_grad_rows, 0)
        o_ref[:, 2 * c:] = dz_ref[...]

    halo_next = pl.BlockSpec((HALO, c), lambda i: (jnp.minimum((i + 1) * hb, last_halo), 0))
    halo_a = pl.BlockSpec((HALO, c), lambda i: (jnp.maximum(i * hb - 1, 0), 0))
    halo_b = pl.BlockSpec((HALO, c), lambda i: (jnp.maximum(i * hb - 1, 0), 1))
    (dproj, dw), moved = _hosted_call(
        body, ex, name, (t // ts,),
        [_rows(ts, c), halo_next, _rows(ts, c), _rows(ts, c, 0), _rows(ts, c, 1), halo_a, halo_b,
         _const((CONV_WIDTH, c))],
        [_rows(ts, c3), _const((CONV_WIDTH * V7X_SUBLANES, c))],
        [jax.ShapeDtypeStruct((t, c3), BF16), jax.ShapeDtypeStruct((CONV_WIDTH * V7X_SUBLANES, c), F32)],
        [pltpu.VMEM((HALO + ts, c), F32), pltpu.VMEM((ts + HALO, c), F32),
         _shift_scratch(ts, cc), _shift_scratch(ts, cc)],
        (dy, dy, dz, proj, proj, proj, proj, conv_w))
    return dproj, dw.reshape(CONV_WIDTH, V7X_SUBLANES, c).sum(axis=1), moved


def _rope_tables(seq):
    half = ROPE_DIM // 2
    inv = ROPE_THETA ** (-jnp.arange(half, dtype=F32) * (2.0 / ROPE_DIM))
    ang = jnp.arange(seq).astype(F32)[:, None] * inv[None, :]
    cos, sin = jnp.cos(ang), jnp.sin(ang)
    zeros = jnp.zeros((seq, HEAD_DIM - ROPE_DIM), F32)
    zh = jnp.zeros((seq, half), F32)
    a = jnp.concatenate([cos, cos, zeros + 1.0], axis=1)
    b = jnp.concatenate([zh, sin, zeros], axis=1)
    c = jnp.concatenate([-sin, zh, zeros], axis=1)
    rep = V7X_LANES // HEAD_DIM
    return tuple(jnp.tile(v, (1, rep)) for v in (a, b, c))


def _head_ones(d):
    head = jnp.arange(d) // HEAD_DIM
    return (head[:, None] == head[None, :]).astype(BF16)


def _rope(ch, ta, tb, tc):
    return ta * ch + tb * pltpu.roll(ch, ROPE_DIM // 2, 1) + tc * pltpu.roll(ch, V7X_LANES - ROPE_DIM // 2, 1)


def _rope_t(ch, ta, tb, tc):
    return ta * ch + pltpu.roll(tb * ch, V7X_LANES - ROPE_DIM // 2, 1) + pltpu.roll(tc * ch, ROPE_DIM // 2, 1)


def _norm_rope_bwd(xhat, r, dout, gain, ta, tb, tc, e_ref):
    dxn = _rope_t(dout, ta, tb, tc)
    dxh = dxn * gain
    dx = r * (dxh - xhat * (_segsum(dxh * xhat, e_ref) * (1.0 / HEAD_DIM)))
    return dx, _colsum8(dxn * xhat)


def _norm_rope_rows(dst_ref, src_ref, gain, ta_ref, tb_ref, tc_ref, e_ref, seq, xhat_ref=None, r_ref=None):
    for r0 in range(0, seq, ATTN_PIECE):
        rows = slice(r0, r0 + ATTN_PIECE)
        xv = src_ref[rows, :]
        r = lax.rsqrt(_segsum(xv * xv, e_ref) * (1.0 / HEAD_DIM) + EPS)
        xhat = xv * r
        if xhat_ref is not None:
            xhat_ref[rows, :] = xhat
            r_ref[rows, :] = r
        dst_ref[rows, :] = _rope(xhat * gain, ta_ref[rows, :], tb_ref[rows, :], tc_ref[rows, :])


ATTN_PIECE = 256
ATTN_UNROLL = 16
CHUNK_UNROLL = 4


def _pieces(dil, seq):
    length = seq // dil
    rows = min(length, ATTN_PIECE)
    return [(r + dil * ci * rows, r * length + ci * rows, rows) for r in range(dil) for ci in range(length // rows)]


def _strided(ref, start, rows, dil):
    if dil == 1:
        return ref[pl.ds(start, rows), :]
    return ref[pl.ds(start, rows, stride=dil), :]


def _strided_set(ref, start, rows, dil, val):
    if dil == 1:
        ref[pl.ds(start, rows), :] = val
    else:
        ref[pl.ds(start, rows, stride=dil), :] = val


def _nt(a, b):
    return lax.dot_general(a, b, (((1,), (1,)), ((), ())), preferred_element_type=F32)


def _tn(a, b):
    return lax.dot_general(a, b, (((0,), (0,)), ((), ())), preferred_element_type=F32)


def _set_bias(bias_ref):
    qi = lax.broadcasted_iota(jnp.int32, (2 * SPAN, 2 * SPAN), 0) & (SPAN - 1)
    kj = lax.broadcasted_iota(jnp.int32, (2 * SPAN, 2 * SPAN), 1)
    band = jnp.logical_and(kj >= qi, (kj - SPAN) <= qi)
    bias_ref[1] = jnp.where(band, 0.0, NEG_INF)
    bias_ref[0] = jnp.where(jnp.logical_and(band, kj >= SPAN), 0.0, NEG_INF)


def _block_keys(bias_ref, j, qs, nb):
    if nb == 1:
        return pl.ds(pl.multiple_of(qs + SPAN, SPAN), SPAN), bias_ref[1, :, SPAN:]
    return pl.ds(qs, 2 * SPAN), bias_ref[jnp.minimum(j & (nb - 1), 1)]


def _stack_heads(v, head0):
    zero = jnp.zeros_like(v)
    return jnp.concatenate([jnp.where(head0, v, zero), jnp.where(head0, zero, v)], axis=0)


def _unstack_heads(v2, head0):
    return jnp.where(head0, v2[:SPAN], v2[SPAN:])


def _head_cols(v, lane=0):
    return jnp.concatenate([v[:, lane:lane + 1], v[:, HEAD_DIM + lane:HEAD_DIM + lane + 1]], axis=0)


def _attn_fwd(proj_b, kv, gains, tables, ones, bsz, seq, name):
    t, d4 = proj_b.shape
    d = d4 // 4
    nhp = d // V7X_LANES
    nblk = seq // SPAN
    scale = HEAD_DIM ** -0.5
    n_groups = len(DILATIONS)

    def body(q0_ref, q1_ref, q2_ref, k_ref, v_ref, gate_ref, gain_ref, ta_ref, tb_ref, tc_ref, e_ref,
             o_ref, l_ref, ao_ref, qd, kd, vd, od, ld, on0, on1, on2, ln0, ln1, ln2, kn, qn, bias):
        head0 = lax.broadcasted_iota(jnp.int32, (SPAN, V7X_LANES), 1) < HEAD_DIM

        @pl.when(jnp.logical_and(pl.program_id(0) == 0, pl.program_id(1) == 0))
        def _():
            _set_bias(bias)

        _norm_rope_rows(kn, k_ref, gain_ref[n_groups:n_groups + 1, :], ta_ref, tb_ref, tc_ref, e_ref, seq)
        kd[0:SPAN, :] = jnp.zeros((SPAN, V7X_LANES), BF16)
        vd[0:SPAN, :] = jnp.zeros((SPAN, V7X_LANES), BF16)
        for g, (q_ref, on, ln) in enumerate(((q0_ref, on0, ln0), (q1_ref, on1, ln1), (q2_ref, on2, ln2))):
            dil = DILATIONS[g]
            nb = seq // dil // SPAN
            _norm_rope_rows(qn, q_ref, gain_ref[g:g + 1, :], ta_ref, tb_ref, tc_ref, e_ref, seq)
            for ns, rs, rows in _pieces(dil, seq):
                qd[rs:rs + rows, :] = _strided(qn, ns, rows, dil).astype(BF16)
                kd[SPAN + rs:SPAN + rs + rows, :] = _strided(kn, ns, rows, dil).astype(BF16)
                vd[SPAN + rs:SPAN + rs + rows, :] = _strided(v_ref, ns, rows, dil).astype(BF16)

            def block(j, carry):
                qs = pl.multiple_of(j * SPAN, SPAN)
                q2 = _stack_heads(qd[pl.ds(qs, SPAN), :], head0)
                keys, mask = _block_keys(bias, j, qs, nb)
                kk = kd[keys, :]
                vv = vd[keys, :]
                s = _nt(q2, kk) * scale + mask
                mx = jnp.max(s, axis=1, keepdims=True)
                p = jnp.exp(s - mx)
                den = jnp.sum(p, axis=1, keepdims=True)
                o2 = jnp.dot(p.astype(BF16), vv, preferred_element_type=F32) / den
                l2 = jnp.broadcast_to(mx + jnp.log(den), (2 * SPAN, V7X_LANES))
                od[pl.ds(qs, SPAN), :] = _unstack_heads(o2, head0)
                ld[pl.ds(qs, SPAN), :] = _unstack_heads(l2, head0)
                return carry

            lax.fori_loop(0, nblk, block, 0, unroll=ATTN_UNROLL)
            for ns, rs, rows in _pieces(dil, seq):
                _strided_set(on, ns, rows, dil, od[rs:rs + rows, :])
                _strided_set(ln, ns, rows, dil, ld[rs:rs + rows, :])

        def merge(ci, carry):
            rows = pl.ds(pl.multiple_of(ci * ATTN_PIECE, ATTN_PIECE), ATTN_PIECE)
            ls = [ln0[rows, :], ln1[rows, :], ln2[rows, :]]
            mx = jnp.maximum(jnp.maximum(ls[0], ls[1]), ls[2])
            es = [jnp.exp(v - mx) for v in ls]
            den = es[0] + es[1] + es[2]
            ov = (es[0] * on0[rows, :] + es[1] * on1[rows, :] + es[2] * on2[rows, :]) / den
            gate = gate_ref[rows, :]
            o_ref[rows, :] = ov
            l_ref[rows, :] = mx + jnp.log(den)
            ao_ref[rows, :] = (ov * gate * _sig(gate)).astype(BF16)
            return carry

        lax.fori_loop(0, seq // ATTN_PIECE, merge, 0)

    blk = (None, seq, V7X_LANES)
    pview = proj_b.reshape(bsz, seq, d4)
    kview = kv.reshape(bsz, seq, 2 * d)
    out_spec = pl.BlockSpec(blk, lambda b, h: (b, 0, h))
    tab = pl.BlockSpec((seq, V7X_LANES), lambda b, h: (0, 0))
    nat = pltpu.VMEM((seq, V7X_LANES), F32)
    o, lse, ao = pl.pallas_call(
        body, name=name, grid=(bsz, nhp),
        in_specs=[pl.BlockSpec(blk, lambda b, h: (b, 0, h)),
                  pl.BlockSpec(blk, lambda b, h: (b, 0, nhp + h)),
                  pl.BlockSpec(blk, lambda b, h: (b, 0, 2 * nhp + h)),
                  pl.BlockSpec(blk, lambda b, h: (b, 0, h)),
                  pl.BlockSpec(blk, lambda b, h: (b, 0, nhp + h)),
                  pl.BlockSpec(blk, lambda b, h: (b, 0, 3 * nhp + h)),
                  pl.BlockSpec((n_groups + 1, V7X_LANES), lambda b, h: (0, 0)),
                  tab, tab, tab,
                  pl.BlockSpec((V7X_LANES, V7X_LANES), lambda b, h: (0, 0))],
        out_specs=[out_spec, out_spec, out_spec],
        out_shape=[jax.ShapeDtypeStruct((bsz, seq, d), F32), jax.ShapeDtypeStruct((bsz, seq, d), F32),
                   jax.ShapeDtypeStruct((bsz, seq, d), BF16)],
        scratch_shapes=[pltpu.VMEM((seq, V7X_LANES), BF16), pltpu.VMEM((SPAN + seq, V7X_LANES), BF16),
                        pltpu.VMEM((SPAN + seq, V7X_LANES), BF16), nat, nat, nat, nat, nat, nat, nat, nat, nat, nat,
                        pltpu.VMEM((2, 2 * SPAN, 2 * SPAN), F32)],
        compiler_params=_params(2),
    )(pview, pview, pview, kview, kview, pview, gains, *tables, ones)
    return o.reshape(t, d), lse.reshape(t, d), ao.reshape(t, d)


def _attn_bwd(proj_b, kv, dao, o, lse, gains, tables, ones, bsz, seq, name):
    t, d4 = proj_b.shape
    d = d4 // 4
    nhp = d // V7X_LANES
    nblk = seq // SPAN
    scale = HEAD_DIM ** -0.5
    n_groups = len(DILATIONS)
    n_chunks = seq // ATTN_PIECE

    def body(q_ref, k_ref, v_ref, gate_ref, dao_ref, o_ref, l_ref, gain_ref, ta_ref, tb_ref, tc_ref, e_ref,
             dproj_ref, dkv_ref, dg_ref, qd, kd, vd, dod, std, dqd, dkd, dvd, dqn, dk0, dk1, dk2, dv0, dv1, dv2,
             kn, kxh, krr, qn, qxh, qrr, don, stn, bias):
        head0 = lax.broadcasted_iota(jnp.int32, (SPAN, V7X_LANES), 1) < HEAD_DIM
        g = pl.program_id(2)

        @pl.when(jnp.logical_and(jnp.logical_and(pl.program_id(0) == 0, pl.program_id(1) == 0), g == 0))
        def _():
            _set_bias(bias)
            dg_ref[...] = jnp.zeros_like(dg_ref)

        @pl.when(g == 0)
        def _():
            first_half = (lax.broadcasted_iota(jnp.int32, (ATTN_PIECE, V7X_LANES), 1) & (HEAD_DIM - 1)) < HEAD_DIM // 2
            _norm_rope_rows(kn, k_ref, gain_ref[n_groups:n_groups + 1, :], ta_ref, tb_ref, tc_ref, e_ref, seq,
                            kxh, krr)
            for r0 in range(0, seq, ATTN_PIECE):
                rows = slice(r0, r0 + ATTN_PIECE)
                gate = gate_ref[rows, :]
                dov = dao_ref[rows, :].astype(F32) * gate * _sig(gate)
                don[rows, :] = dov
                stn[rows, :] = jnp.where(first_half, l_ref[rows, :], _segsum(dov * o_ref[rows, :], e_ref))

        def norm_bwd_chunks(xhat_ref, r_ref, dn_refs, out_ref, gi):
            def chunk(ci, carry):
                rows = pl.ds(pl.multiple_of(ci * ATTN_PIECE, ATTN_PIECE), ATTN_PIECE)
                dn = functools.reduce(lambda u, w: u + w, [r_[rows, :] for r_ in dn_refs])
                dx, part = _norm_rope_bwd(xhat_ref[rows, :], r_ref[rows, :], dn, gain_ref[gi:gi + 1, :],
                                          ta_ref[rows, :], tb_ref[rows, :], tc_ref[rows, :], e_ref)
                out_ref[rows, :] = dx.astype(BF16)
                dg_ref[gi] += part
                return carry
            lax.fori_loop(0, n_chunks, chunk, 0, unroll=CHUNK_UNROLL)

        def group(gi):
            dil = DILATIONS[gi]
            nb = seq // dil // SPAN
            kd[0:SPAN, :] = jnp.zeros((SPAN, V7X_LANES), BF16)
            vd[0:SPAN, :] = jnp.zeros((SPAN, V7X_LANES), BF16)
            dkd[...] = jnp.zeros_like(dkd)
            dvd[...] = jnp.zeros_like(dvd)
            _norm_rope_rows(qn, q_ref, gain_ref[gi:gi + 1, :], ta_ref, tb_ref, tc_ref, e_ref, seq, qxh, qrr)
            for ns, rs, rows in _pieces(dil, seq):
                qd[rs:rs + rows, :] = _strided(qn, ns, rows, dil).astype(BF16)
                kd[SPAN + rs:SPAN + rs + rows, :] = _strided(kn, ns, rows, dil).astype(BF16)
                vd[SPAN + rs:SPAN + rs + rows, :] = _strided(v_ref, ns, rows, dil).astype(BF16)
                dod[rs:rs + rows, :] = _strided(don, ns, rows, dil).astype(BF16)
                std[rs:rs + rows, :] = _strided(stn, ns, rows, dil)

            def block(j, carry):
                qs = pl.multiple_of(j * SPAN, SPAN)
                q2 = _stack_heads(qd[pl.ds(qs, SPAN), :], head0)
                do2 = _stack_heads(dod[pl.ds(qs, SPAN), :], head0)
                keys, mask = _block_keys(bias, j, qs, nb)
                kk = kd[keys, :]
                vv = vd[keys, :]
                s = _nt(q2, kk) * scale + mask
                stv = std[pl.ds(qs, SPAN), :]
                p = jnp.exp(s - _head_cols(stv))
                ds = (p * (_nt(do2, vv) - _head_cols(stv, HEAD_DIM // 2)) * scale).astype(BF16)
                dqd[pl.ds(qs, SPAN), :] = _unstack_heads(jnp.dot(ds, kk, preferred_element_type=F32), head0)
                dkd[keys, :] += _tn(ds, q2)
                dvd[keys, :] += _tn(p.astype(BF16), do2)
                return carry

            lax.fori_loop(0, nblk, block, 0, unroll=ATTN_UNROLL)
            for ns, rs, rows in _pieces(dil, seq):
                _strided_set(dqn, ns, rows, dil, dqd[rs:rs + rows, :])
                _strided_set((dk0, dk1, dk2)[gi], ns, rows, dil, dkd[SPAN + rs:SPAN + rs + rows, :])
                _strided_set((dv0, dv1, dv2)[gi], ns, rows, dil, dvd[SPAN + rs:SPAN + rs + rows, :])
            norm_bwd_chunks(qxh, qrr, [dqn], dproj_ref, gi)

        for gi in range(n_groups):
            @pl.when(g == gi)
            def _():
                group(gi)

        @pl.when(g == n_groups - 1)
        def _():
            norm_bwd_chunks(kxh, krr, [dk0, dk1, dk2], dkv_ref, n_groups)

        @pl.when(g == n_groups)
        def _():
            def chunk(ci, carry):
                rows = pl.ds(pl.multiple_of(ci * ATTN_PIECE, ATTN_PIECE), ATTN_PIECE)
                gate = gate_ref[rows, :]
                sg = _sig(gate)
                dproj_ref[rows, :] = (dao_ref[rows, :].astype(F32) * o_ref[rows, :]
                                      * (sg * (1.0 + gate * (1.0 - sg)))).astype(BF16)
                dkv_ref[rows, :] = (dv0[rows, :] + dv1[rows, :] + dv2[rows, :]).astype(BF16)
                return carry
            lax.fori_loop(0, n_chunks, chunk, 0, unroll=CHUNK_UNROLL)

    blk = (None, seq, V7X_LANES)
    pview = proj_b.reshape(bsz, seq, d4)
    kview = kv.reshape(bsz, seq, 2 * d)
    dview = (bsz, seq, d)
    d_spec = pl.BlockSpec(blk, lambda b, h, g: (b, 0, h))
    tab = pl.BlockSpec((seq, V7X_LANES), lambda b, h, g: (0, 0))
    nat = pltpu.VMEM((seq, V7X_LANES), F32)
    natb = pltpu.VMEM((seq, V7X_LANES), BF16)
    pad = pltpu.VMEM((SPAN + seq, V7X_LANES), F32)
    padb = pltpu.VMEM((SPAN + seq, V7X_LANES), BF16)
    dproj, dkv, dg = pl.pallas_call(
        body, name=name, grid=(bsz, nhp, n_groups + 1),
        in_specs=[pl.BlockSpec(blk, lambda b, h, g: (b, 0, jnp.minimum(g, n_groups - 1) * nhp + h)),
                  pl.BlockSpec(blk, lambda b, h, g: (b, 0, h)),
                  pl.BlockSpec(blk, lambda b, h, g: (b, 0, nhp + h)),
                  pl.BlockSpec(blk, lambda b, h, g: (b, 0, n_groups * nhp + h)),
                  d_spec, d_spec, d_spec,
                  pl.BlockSpec((n_groups + 1, V7X_LANES), lambda b, h, g: (0, 0)),
                  tab, tab, tab,
                  pl.BlockSpec((V7X_LANES, V7X_LANES), lambda b, h, g: (0, 0))],
        out_specs=[pl.BlockSpec(blk, lambda b, h, g: (b, 0, g * nhp + h)),
                   pl.BlockSpec(blk, lambda b, h, g: (b, 0, (g // n_groups) * nhp + h)),
                   pl.BlockSpec((n_groups + 1, V7X_SUBLANES, V7X_LANES), lambda b, h, g: (0, 0, 0))],
        out_shape=[jax.ShapeDtypeStruct((bsz, seq, d4), BF16), jax.ShapeDtypeStruct((bsz, seq, 2 * d), BF16),
                   jax.ShapeDtypeStruct((n_groups + 1, V7X_SUBLANES, V7X_LANES), F32)],
        scratch_shapes=[natb, padb, padb, natb, nat, nat, pad, pad] + [nat] * 15 + [
                        pltpu.VMEM((2, 2 * SPAN, 2 * SPAN), F32)],
        compiler_params=_params(3),
    )(pview, kview, kview, pview, dao.reshape(dview), o.reshape(dview), lse.reshape(dview), gains, *tables, ones)
    dgain = dg.sum(axis=1).reshape(n_groups + 1, V7X_LANES // HEAD_DIM, HEAD_DIM).sum(axis=1)
    return dproj.reshape(t, d4), dkv.reshape(t, 2 * d), dgain


def _mesh_position():
    x, y, c = lax.axis_index("x"), lax.axis_index("y"), lax.axis_index("c")
    return x, y, c


def _peer(x, y, c, rel):
    return (1 - x if rel & 4 else x, 1 - y if rel & 2 else y, 1 - c if rel & 1 else c)


class _Exchange:
    def __init__(self, srcs, gather):
        self.srcs = list(srcs)
        self.gather = gather
        n = self.n = len(self.srcs)
        hbm = pl.BlockSpec(memory_space=pltpu.HBM)
        self.in_specs = [hbm] * n
        self.out_specs = [hbm] * n
        self.out_shape = [jax.ShapeDtypeStruct(((N_DEV,) + a.shape) if gather else a.shape, a.dtype)
                          for a in self.srcs]
        self.scratch = [pltpu.SemaphoreType.DMA((n * (N_DEV - 1),)), pltpu.SemaphoreType.DMA((n * (N_DEV - 1),)),
                        pltpu.SemaphoreType.DMA((n,))]

    def _copies(self, ins, outs, sems):
        send_sems, recv_sems, local_sems = sems
        x, y, c = _mesh_position()
        me = 4 * x + 2 * y + c
        remote, local = [], []
        for a in range(self.n):
            mine = ins[a] if self.gather else ins[a].at[me]
            local.append(pltpu.make_async_copy(mine, outs[a].at[me], local_sems.at[a]))
            for rel in range(1, N_DEV):
                px, py, pc = _peer(x, y, c, rel)
                s = a * (N_DEV - 1) + rel - 1
                src = ins[a] if self.gather else ins[a].at[4 * px + 2 * py + pc]
                remote.append(pltpu.make_async_remote_copy(
                    src_ref=src, dst_ref=outs[a].at[me], send_sem=send_sems.at[s], recv_sem=recv_sems.at[s],
                    device_id=(px, py, pc), device_id_type=pl.DeviceIdType.MESH))
        return remote, local

    def start(self, ins, outs, sems):
        remote, local = self._copies(ins, outs, sems)
        for cp in local + remote:
            cp.start()

    def wait(self, ins, outs, sems):
        remote, local = self._copies(ins, outs, sems)
        for cp in remote:
            cp.wait_recv()
        for cp in remote:
            cp.wait_send()
        for cp in local:
            cp.wait()


def _gather_chip_once(arrs, name):
    n = len(arrs)
    per = N_DEV - 1

    def body(*refs):
        ins, outs = refs[:n], refs[n:2 * n]
        send_sems, recv_sems, local_sems = refs[2 * n:]
        x, y, c = _mesh_position()
        me, sibling = (x, y, c), (x, y, 1 - c)
        chips = [(1 - x, y), (x, 1 - y), (1 - x, 1 - y)]

        def copy(a, k, block, to, src=None):
            bx, by, bc = block
            dst = outs[a].at[4 * bx + 2 * by + bc]
            return pltpu.make_async_remote_copy(
                src_ref=dst if src is None else src, dst_ref=dst, send_sem=send_sems.at[a * per + k],
                recv_sem=recv_sems.at[a * per + k], device_id=to, device_id_type=pl.DeviceIdType.MESH)

        local, sent = [], []
        for a in range(n):
            mine = pltpu.make_async_copy(ins[a], outs[a].at[4 * x + 2 * y + c], local_sems.at[a])
            mine.start()
            local.append(mine)
            first = [copy(a, 0, me, sibling, src=ins[a])]
            first += [copy(a, 1 + j, me, chip + (c,), src=ins[a]) for j, chip in enumerate(chips)]
            for cp in first:
                cp.start()
            sent += first
        for a in range(n):
            for j, chip in enumerate(chips):
                copy(a, 1 + j, chip + (c,), me).wait_recv()
                passed = copy(a, 4 + j, chip + (c,), sibling)
                passed.start()
                sent.append(passed)
        for a in range(n):
            copy(a, 0, sibling, me).wait_recv()
            for j, chip in enumerate(chips):
                copy(a, 4 + j, chip + (1 - c,), me).wait_recv()
        for cp in sent:
            cp.wait_send()
        for cp in local:
            cp.wait()

    hbm = pl.BlockSpec(memory_space=pltpu.HBM)
    return pl.pallas_call(
        body, name=name, in_specs=[hbm] * n, out_specs=[hbm] * n,
        out_shape=[jax.ShapeDtypeStruct((N_DEV,) + a.shape, a.dtype) for a in arrs],
        scratch_shapes=[pltpu.SemaphoreType.DMA((n * per,)), pltpu.SemaphoreType.DMA((n * per,)),
                        pltpu.SemaphoreType.DMA((n,))],
    )(*arrs)


def _run_exchange(ex, name):
    n = ex.n

    def body(*refs):
        ins, outs, sems = refs[:n], refs[n:2 * n], refs[2 * n:]
        ex.start(ins, outs, sems)
        ex.wait(ins, outs, sems)

    return pl.pallas_call(body, name=name, in_specs=ex.in_specs, out_specs=ex.out_specs, out_shape=ex.out_shape,
                          scratch_shapes=ex.scratch)(*ex.srcs)


def _hosted_call(body, ex, name, grid, in_specs, out_specs, out_shape, scratch_shapes, args):
    if ex is None:
        outs = pl.pallas_call(body, name=name, grid=grid, in_specs=in_specs, out_specs=out_specs, out_shape=out_shape,
                              scratch_shapes=scratch_shapes, compiler_params=_params(len(grid)))(*args)
        return list(outs), []
    n_in, n_out, n_scr, n = len(in_specs), len(out_specs), len(scratch_shapes), ex.n

    def hosted(*refs):
        h_in, e_in = refs[:n_in], refs[n_in:n_in + n]
        o0 = n_in + n
        h_out, e_out = refs[o0:o0 + n_out], refs[o0 + n_out:o0 + n_out + n]
        s0 = o0 + n_out + n
        h_scr, e_scr = refs[s0:s0 + n_scr], refs[s0 + n_scr:]
        ids = [pl.program_id(a) for a in range(len(grid))]
        first = functools.reduce(jnp.logical_and, [i == 0 for i in ids])
        last = functools.reduce(jnp.logical_and, [i == g - 1 for i, g in zip(ids, grid)])

        @pl.when(first)
        def _():
            ex.start(e_in, e_out, e_scr)

        body(*h_in, *h_out, *h_scr)

        @pl.when(last)
        def _():
            ex.wait(e_in, e_out, e_scr)

    outs = pl.pallas_call(
        hosted, name=name, grid=grid, in_specs=list(in_specs) + ex.in_specs,
        out_specs=list(out_specs) + ex.out_specs, out_shape=list(out_shape) + ex.out_shape,
        scratch_shapes=list(scratch_shapes) + ex.scratch, compiler_params=_params(len(grid)),
    )(*args, *ex.srcs)
    return list(outs[:n_out]), list(outs[n_out:])


def _sum_adamw(parts, w, m, v, name):
    _, r, wd = parts.shape
    tr = _pick(r, ADAM_ROWS, 8)
    c1 = 1.0 - ADAM_B1 ** ADAM_STEP
    c2 = 1.0 - ADAM_B2 ** ADAM_STEP

    def body(p_ref, w_ref, m_ref, v_ref, g_ref, d_ref, nm_ref, nv_ref):
        g = p_ref[0].astype(F32)
        for s in range(1, N_DEV):
            g = g + p_ref[s].astype(F32)
        nm = ADAM_B1 * m_ref[...] + (1.0 - ADAM_B1) * g
        nv = ADAM_B2 * v_ref[...] + (1.0 - ADAM_B2) * (g * g)
        g_ref[...] = g
        nm_ref[...] = nm
        nv_ref[...] = nv
        d_ref[...] = -ADAM_LR * ((nm / c1) / (jnp.sqrt(nv / c2) + ADAM_EPS) + ADAM_WD * w_ref[...])

    row = pl.BlockSpec((tr, wd), lambda i: (i, 0))
    return pl.pallas_call(
        body, name=name, grid=(r // tr,),
        in_specs=[pl.BlockSpec((N_DEV, tr, wd), lambda i: (0, i, 0)), row, row, row],
        out_specs=[row] * 4, out_shape=[jax.ShapeDtypeStruct((r, wd), F32)] * 4,
        compiler_params=_params(1),
    )(parts, w, m, v)


def _pack_rows(size, row_mult):
    rows = -(-size // PACK_LANES)
    return -(-rows // row_mult) * row_mult


def _pack(flats, row_mult, dtype, total_mult=None):
    out = []
    for f in flats:
        size = f.shape[-1]
        rows = _pack_rows(size, row_mult)
        pad = [(0, 0)] * (f.ndim - 1) + [(0, rows * PACK_LANES - size)]
        out.append(jnp.pad(f.astype(dtype), pad).reshape(f.shape[:-1] + (rows, PACK_LANES)))
    if total_mult is not None:
        total = sum(o.shape[-2] for o in out)
        extra = -(-total // total_mult) * total_mult - total
        if extra:
            out.append(jnp.zeros(out[0].shape[:-2] + (extra, PACK_LANES), dtype))
    return jnp.concatenate(out, axis=-2)


def _unpack(buf, sizes, row_mult):
    out, row = [], 0
    for size in sizes:
        rows = _pack_rows(size, row_mult)
        part = buf[..., row:row + rows, :]
        out.append(part.reshape(buf.shape[:-2] + (rows * PACK_LANES,))[..., :size])
        row += rows
    return out


def _to_slots(full, axis):
    if axis is None:
        return jnp.broadcast_to(full.reshape(1, -1), (N_DEV, full.size))
    shape = full.shape
    split = full.reshape(shape[:axis] + (N_DEV, shape[axis] // N_DEV) + shape[axis + 1:])
    return jnp.moveaxis(split, axis, 0).reshape(N_DEV, -1)


def _from_slots(slots, axis, block_shape):
    split = jnp.moveaxis(slots, 0, axis)
    shape = list(block_shape)
    shape[axis] *= N_DEV
    return split.reshape(shape)


def kernel(x, p, norm_g, w_in_a, conv_w, conv_b, ln_g, ln_b, w_out_a, kv_norm_g, w_kv, k_norm_g, w_in_b, q_norm_g, w_out_b, ple_norm_g, w_ple_gate, w_ple_proj, loss_target, m_norm_g, m_w_in_a, m_conv_w, m_conv_b, m_ln_g, m_ln_b, m_w_out_a, m_kv_norm_g, m_w_kv, m_k_norm_g, m_w_in_b, m_q_norm_g, m_w_out_b, m_ple_norm_g, m_w_ple_gate, m_w_ple_proj, v_norm_g, v_w_in_a, v_conv_w, v_conv_b, v_ln_g, v_ln_b, v_w_out_a, v_kv_norm_g, v_w_kv, v_k_norm_g, v_w_in_b, v_q_norm_g, v_w_out_b, v_ple_norm_g, v_w_ple_gate, v_w_ple_proj):
    weights = dict(zip(WEIGHT_NAMES, (norm_g, w_in_a, conv_w, conv_b, ln_g, ln_b, w_out_a, kv_norm_g, w_kv, k_norm_g,
                                      w_in_b, q_norm_g, w_out_b, ple_norm_g, w_ple_gate, w_ple_proj)))
    mom_m = dict(zip(WEIGHT_NAMES, (m_norm_g, m_w_in_a, m_conv_w, m_conv_b, m_ln_g, m_ln_b, m_w_out_a, m_kv_norm_g,
                                    m_w_kv, m_k_norm_g, m_w_in_b, m_q_norm_g, m_w_out_b, m_ple_norm_g, m_w_ple_gate,
                                    m_w_ple_proj)))
    mom_v = dict(zip(WEIGHT_NAMES, (v_norm_g, v_w_in_a, v_conv_w, v_conv_b, v_ln_g, v_ln_b, v_w_out_a, v_kv_norm_g,
                                    v_w_kv, v_k_norm_g, v_w_in_b, v_q_norm_g, v_w_out_b, v_ple_norm_g, v_w_ple_gate,
                                    v_w_ple_proj)))
    bsz, seq, d = x.shape
    t = bsz * seq
    assert seq % (max(DILATIONS) * SPAN) == 0 and d % V7X_LANES == 0

    full = {}

    def rows2d(a):
        return a.reshape(-1, a.shape[-1])

    def packed(source, names, dtype, total_mult=None):
        return _pack([source[n].reshape(-1) for n in names], 16, dtype, total_mult)

    def gathered(names, bufs):
        for n, buf in zip(names, bufs):
            full[n] = _from_slots(buf.reshape((N_DEV,) + weights[n].shape), SHARD_AXIS[n], weights[n].shape)

    w1_all, wv_all = _gather_chip_once([rows2d(weights['w_in_a']).astype(BF16),
                                        _pack([weights[n].reshape(-1) for n in VECTOR_WEIGHTS], 8, F32)],
                                       "gather_first")
    gathered(GROUP_FIRST, [w1_all])
    for n, slots in zip(VECTOR_WEIGHTS, _unpack(wv_all, [weights[n].size for n in VECTOR_WEIGHTS], 8)):
        full[n] = _from_slots(slots.reshape((N_DEV,) + weights[n].shape), SHARD_AXIS[n], weights[n].shape)
    gather_rest = _Exchange([rows2d(weights[n]).astype(BF16) for n in GROUP_REST], gather=True)
    wa_in = full['w_in_a'][0]
    cw, cb, lg, lb = full['conv_w'][0], full['conv_b'], full['ln_g'], full['ln_b']

    tables = _rope_tables(seq)
    ones = _head_ones(V7X_LANES)
    rep = V7X_LANES // HEAD_DIM
    head_gain = jnp.concatenate([jnp.tile(q_norm_g[0], (1, rep)), jnp.tile(k_norm_g, rep)[None]], axis=0)

    x0 = x.reshape(t, d)
    p0, p1 = p[0].reshape(t, -1), p[1].reshape(t, -1)
    target = loss_target.reshape(t, d)
    g_norm0, g_norm1 = norm_g[0:1], norm_g[1:2]
    g_ple0, g_ple1 = ple_norm_g[0:1], ple_norm_g[1:2]
    g_kv = kv_norm_g.reshape(1, d)

    (u0,) = _rmsnorm_fwd(x0, [g_norm0], "norm0")
    proj_a = _matmul(u0, wa_in, 'nn', "in_a")
    m_act, y_conv, w2_all = _conv_fwd(proj_a, cw, cb, lg, lb, seq, "conv_fwd", ex=gather_rest)
    gathered(GROUP_REST, w2_all)
    wa_out = full['w_out_a'][0]
    wkv = full['w_kv']
    wb_in, wb_out = full['w_in_b'][0], full['w_out_b'][0]
    wg, wp = full['w_ple_gate'], full['w_ple_proj']
    h0, pg0 = _matmul(m_act, wa_out, 'nn', "out_a", add=x0, norm_gain=g_ple0)
    gl0 = _matmul(pg0, wg[0], 'nn', "ple_gate0", out_dtype=BF16)
    pp0 = _matmul(p0, wp[0], 'nn', "ple_proj0", out_dtype=BF16)

    x1, (kvn, u1) = _ple_norm_fwd(h0, gl0, pp0, [g_kv, g_norm1], "ple0_norm1")
    kv = _matmul(kvn, wkv, 'nn', "kv")
    proj_b = _matmul(u1, wb_in, 'nn', "in_b")
    o_att, lse, ao = _attn_fwd(proj_b, kv, head_gain, tables, ones, bsz, seq, "attn_fwd")
    h1, pg1 = _matmul(ao, wb_out, 'nn', "out_b", add=x1, norm_gain=g_ple1)
    gl1 = _matmul(pg1, wg[1], 'nn', "ple_gate1", out_dtype=BF16)
    pp1 = _matmul(p1, wp[1], 'nn', "ple_proj1", out_dtype=BF16)

    dx2, dgl1, dpp1, loss_part = _ple_loss(h1, gl1, pp1, target, "ple1_loss")
    loss = lax.psum(jnp.sum(loss_part), ("x", "y", "c"))

    grads = {}
    slot = {}

    dwp1 = _matmul(p1, dpp1, 'tn', "d_ple_proj1", out_dtype=BF16, slot_cols=d // N_DEV)
    dwg1 = _matmul(pg1, dgl1, 'tn', "d_ple_gate1", out_dtype=BF16)
    dpg1 = _matmul(dgl1, wg[1], 'nt', "d_ple_norm1", out_dtype=BF16)
    dh1, (dg_ple1,) = _rmsnorm_bwd(h1, [g_ple1], [dpg1], dx2, "ple_norm1_bwd")
    slot['w_out_b'] = _matmul(ao, dh1, 'tn', "d_out_b", out_dtype=BF16).reshape(N_DEV, -1, d)
    dao = _matmul(dh1, wb_out, 'nt', "d_ao", out_dtype=BF16)
    dproj_b, dkv, dg_head = _attn_bwd(proj_b, kv, dao, o_att, lse, head_gain, tables, ones, bsz, seq, "attn_bwd")
    slot['w_in_b'] = _matmul(u1, dproj_b, 'tn', "d_in_b", out_dtype=BF16, slot_cols=4 * d // N_DEV)
    du1 = _matmul(dproj_b, wb_in, 'nt', "d_u1", out_dtype=BF16)
    slot['w_kv'] = _matmul(kvn, dkv, 'tn', "d_kv", out_dtype=BF16, slot_cols=2 * d // N_DEV)
    dkvn = _matmul(dkv, wkv, 'nt', "d_kvn", out_dtype=BF16)
    dx1, (dg_kv, dg_norm1), dgl0, dpp0 = _rmsnorm_bwd(x1, [g_kv, g_norm1], [dkvn, du1], dh1, "norm1_bwd",
                                                      ple=(gl0, pp0))

    dwp0 = _matmul(p0, dpp0, 'tn', "d_ple_proj0", out_dtype=BF16, slot_cols=d // N_DEV)
    dwg0 = _matmul(pg0, dgl0, 'tn', "d_ple_gate0", out_dtype=BF16)
    dpg0 = _matmul(dgl0, wg[0], 'nt', "d_ple_norm0", out_dtype=BF16)
    dh0, (dg_ple0,) = _rmsnorm_bwd(h0, [g_ple0], [dpg0], dx1, "ple_norm0_bwd")
    slot['w_out_a'] = _matmul(m_act, dh0, 'tn', "d_out_a", out_dtype=BF16).reshape(N_DEV, -1, d)
    dm = _matmul(dh0, wa_out, 'nt', "d_m", out_dtype=BF16)
    dy_conv, dz, d_lg, d_lb, d_cb = _ln_gate_bwd(dm, y_conv, proj_a, lg, lb, "ln_gate_bwd")
    slot['w_ple_gate'] = jnp.stack([dwg0.reshape(N_DEV, -1, d), dwg1.reshape(N_DEV, -1, d)],
                                   axis=1).reshape(N_DEV, -1, d)
    slot['w_ple_proj'] = jnp.stack([dwp0, dwp1], axis=1).reshape(N_DEV, -1, d // N_DEV)

    dproj_a, d_cw, parts_rest = _conv_bwd(dy_conv, dz, proj_a, cw, seq, "conv_bwd",
                                          ex=_Exchange([slot[n] for n in GROUP_REST], gather=False))
    slot['w_in_a'] = _matmul(u0, dproj_a, 'tn', "d_in_a", out_dtype=BF16, slot_cols=wa_in.shape[1] // N_DEV)
    du0, parts_first = _matmul(dproj_a, wa_in, 'nt', "d_u0", out_dtype=BF16,
                               ex=_Exchange([slot['w_in_a']], gather=False))
    dx0, (dg_norm0,) = _rmsnorm_bwd(x0, [g_norm0], [du0], dh0, "norm0_bwd", dx_dtype=F32)

    grads['norm_g'] = jnp.stack([dg_norm0, dg_norm1])
    grads['conv_w'] = d_cw[None]
    grads['conv_b'] = d_cb[None]
    grads['ln_g'] = d_lg[None]
    grads['ln_b'] = d_lb[None]
    grads['kv_norm_g'] = dg_kv
    grads['k_norm_g'] = dg_head[3]
    grads['q_norm_g'] = dg_head[0:3][None]
    grads['ple_norm_g'] = jnp.stack([dg_ple0, dg_ple1])
    small_pack = _pack([_to_slots(grads[n], SHARD_AXIS[n]) for n in GROUP_SMALL], 16, BF16)
    (parts_small,) = _run_exchange(_Exchange([small_pack], gather=False), "exchange_small")

    updated = {}
    for n, parts in zip(GROUP_REST + GROUP_FIRST, parts_rest + parts_first):
        outs = _sum_adamw(parts, rows2d(weights[n]), rows2d(mom_m[n]), rows2d(mom_v[n]), "sum_adamw_" + n)
        for kind, buf in enumerate(outs):
            updated[kind, n] = buf.reshape(weights[n].shape)
    outs = _sum_adamw(parts_small, packed(weights, GROUP_SMALL, F32), packed(mom_m, GROUP_SMALL, F32),
                      packed(mom_v, GROUP_SMALL, F32), "sum_adamw_small")
    sizes = [weights[n].size for n in GROUP_SMALL]
    for kind, buf in enumerate(outs):
        for n, flat in zip(GROUP_SMALL, _unpack(buf, sizes, 16)):
            updated[kind, n] = flat.reshape(weights[n].shape)
    result = [loss, dx0.reshape(bsz, seq, d)]
    for kind in range(4):
        result.extend(updated[kind, n] for n in WEIGHT_NAMES)
    return tuple(result)
```

```python
import functools

import jax
import jax.numpy as jnp
from jax import lax
from jax.experimental import pallas as pl
from jax.experimental.pallas import tpu as pltpu

F32 = jnp.float32
BF16 = jnp.bfloat16

N_DEV = 8
HEAD_DIM = 64
ROPE_DIM = 16
ROPE_THETA = 500000.0
EPS = 1e-6
NEG_INF = -1e30
SPAN = 128
DILATIONS = (1, 4, 16)
CONV_WIDTH = 31
HALO = 32
CONV_ROWS = 32
CONV_W_ROWS = 64
PACK_LANES = 1024
V7X_LANES = 128
V7X_SUBLANES = 8
VMEM_LIMIT_BYTES = 56 * 1024 * 1024

ADAM_LR = 0.001
ADAM_B1 = 0.9
ADAM_B2 = 0.999
ADAM_EPS = 1e-08
ADAM_WD = 0.01
ADAM_STEP = 10
ADAM_ROWS = 256

WEIGHT_NAMES = ('norm_g', 'w_in_a', 'conv_w', 'conv_b', 'ln_g', 'ln_b', 'w_out_a', 'kv_norm_g', 'w_kv',
                'k_norm_g', 'w_in_b', 'q_norm_g', 'w_out_b', 'ple_norm_g', 'w_ple_gate', 'w_ple_proj')
SHARD_AXIS = {'norm_g': None, 'w_in_a': 2, 'conv_w': 2, 'conv_b': 1, 'ln_g': 1, 'ln_b': 1, 'w_out_a': 1,
              'kv_norm_g': None, 'w_kv': 1, 'k_norm_g': None, 'w_in_b': 2, 'q_norm_g': None, 'w_out_b': 1,
              'ple_norm_g': None, 'w_ple_gate': 1, 'w_ple_proj': 2}
VECTOR_WEIGHTS = ('conv_w', 'conv_b', 'ln_g', 'ln_b')
GROUP_FIRST = ('w_in_a',)
GROUP_REST = ('w_out_a', 'w_kv', 'w_in_b', 'w_out_b', 'w_ple_gate', 'w_ple_proj')
GROUP_SMALL = ('norm_g', 'conv_w', 'conv_b', 'ln_g', 'ln_b', 'kv_norm_g', 'k_norm_g', 'q_norm_g', 'ple_norm_g')


def _pick(n, target, mult):
    t = (min(target, n) // mult) * mult
    while t >= mult:
        if n % t == 0:
            return t
        t -= mult
    return n


def _params(n_grid):
    return pltpu.CompilerParams(dimension_semantics=("arbitrary",) * n_grid, vmem_limit_bytes=VMEM_LIMIT_BYTES)


def _sig(x):
    return 0.5 * jnp.tanh(0.5 * x) + 0.5


def _colsum8(v):
    r, w = v.shape
    return v.reshape(r // V7X_SUBLANES, V7X_SUBLANES, w).sum(axis=0)


def _rows(tm, w, col=0):
    return pl.BlockSpec((tm, w), lambda i: (i, col))


def _const(shape):
    nd = len(shape)
    return pl.BlockSpec(shape, lambda i: (0,) * nd)


def _segsum(v, e_ref):
    hi = v.astype(BF16)
    lo = (v - hi.astype(F32)).astype(BF16)
    e = e_ref[...]
    return jnp.dot(hi, e, preferred_element_type=F32) + jnp.dot(lo, e, preferred_element_type=F32)


MM_TILE = 1024
MM_TILE_K = 2048
MM_TILE_WIDE = 2048


def _matmul(a, b, mode, name, out_dtype=F32, add=None, ex=None, slot_cols=None, norm_gain=None):
    if mode == 'nn':
        (m, k), (_, n) = a.shape, b.shape
    elif mode == 'nt':
        (m, k), (n, _) = a.shape, b.shape
    else:
        (k, m), (_, n) = a.shape, b.shape
    out_struct = jax.ShapeDtypeStruct((m, n), out_dtype)
    n_slots = 0
    if mode == 'tn':
        tm, tn, tk = _pick(m, MM_TILE, 128), _pick(n, MM_TILE, 128), _pick(k, MM_TILE_K, 128)
        o_spec = pl.BlockSpec((tm, tn), lambda i, j, kk: (i, j))
        if slot_cols is not None:
            assert n == N_DEV * slot_cols
            n_slots = max(s for s in (1, 2, 4, 8) if s == 1 or slot_cols * s <= MM_TILE)
            tn = slot_cols * n_slots
            o_spec = pl.BlockSpec((n_slots, tm, slot_cols), lambda i, j, kk: (j, i, 0))
            out_struct = jax.ShapeDtypeStruct((N_DEV, m, slot_cols), out_dtype)
        grid = (m // tm, n // tn, k // tk)
        a_spec = pl.BlockSpec((tk, tm), lambda i, j, kk: (kk, i))
        b_spec = pl.BlockSpec((tk, tn), lambda i, j, kk: (kk, j))
        dims = (((0,), (0,)), ((), ()))
    else:
        tn_max = MM_TILE_WIDE if k <= MM_TILE else MM_TILE
        tm, tn, tk = _pick(m, MM_TILE, 128), _pick(n, tn_max, 128), _pick(k, MM_TILE_K, 128)
        grid = (n // tn, m // tm, k // tk)
        a_spec = pl.BlockSpec((tm, tk), lambda j, i, kk: (i, kk))
        o_spec = pl.BlockSpec((tm, tn), lambda j, i, kk: (i, j))
        if mode == 'nn':
            b_spec = pl.BlockSpec((tk, tn), lambda j, i, kk: (kk, j))
            dims = (((1,), (0,)), ((), ()))
        else:
            b_spec = pl.BlockSpec((tn, tk), lambda j, i, kk: (j, kk))
            dims = (((1,), (1,)), ((), ()))
    nk = grid[2]
    has_add = add is not None
    has_norm = norm_gain is not None
    assert not has_norm or (tn == n and mode != 'tn')

    def body(*refs):
        a_ref, b_ref = refs[0], refs[1]
        add_ref = refs[2] if has_add else None
        gain_ref = refs[2 + has_add] if has_norm else None
        o_ref = refs[2 + has_add + has_norm]
        norm_ref = refs[3 + has_add + has_norm] if has_norm else None
        part = lax.dot_general(a_ref[...].astype(BF16), b_ref[...].astype(BF16), dims, preferred_element_type=F32)

        def finish(total):
            if has_add:
                total = total + add_ref[...]
            if n_slots:
                for s in range(n_slots):
                    o_ref[s] = total[:, s * slot_cols:(s + 1) * slot_cols].astype(out_dtype)
            else:
                o_ref[...] = total.astype(out_dtype)
            if has_norm:
                y = total * lax.rsqrt(jnp.mean(total * total, axis=-1, keepdims=True) + EPS)
                norm_ref[...] = (y * gain_ref[...]).astype(BF16)

        if nk == 1:
            finish(part)
        else:
            acc_ref = refs[3 + has_add + 2 * has_norm]
            kk = pl.program_id(2)

            @pl.when(kk == 0)
            def _():
                acc_ref[...] = part

            @pl.when(kk > 0)
            def _():
                acc_ref[...] += part

            @pl.when(kk == nk - 1)
            def _():
                finish(acc_ref[...])

    in_specs = [a_spec, b_spec] + ([o_spec] if has_add else [])
    args = [a, b] + ([add] if has_add else [])
    out_specs, out_structs = [o_spec], [out_struct]
    if has_norm:
        in_specs.append(pl.BlockSpec((1, n), lambda j, i, kk: (0, 0)))
        args.append(norm_gain)
        out_specs.append(o_spec)
        out_structs.append(jax.ShapeDtypeStruct((m, n), BF16))
    scratch = [pltpu.VMEM((tm, tn), F32)] if nk > 1 else []
    outs, moved = _hosted_call(body, ex, name, grid, in_specs, out_specs, out_structs, scratch, args)
    out = outs[0] if not has_norm else tuple(outs)
    return out if ex is None else (out, moved)


def _rmsnorm_fwd(x, gains, name):
    t, d = x.shape
    tm = _pick(t, 512, 8)
    n = len(gains)

    def body(*refs):
        x_ref, g_refs, o_refs = refs[0], refs[1:1 + n], refs[1 + n:]
        xv = x_ref[...]
        y = xv * lax.rsqrt(jnp.mean(xv * xv, axis=-1, keepdims=True) + EPS)
        for g_ref, o_ref in zip(g_refs, o_refs):
            o_ref[...] = (y * g_ref[...]).astype(BF16)

    return pl.pallas_call(
        body, name=name, grid=(t // tm,),
        in_specs=[_rows(tm, d)] + [_const((1, d))] * n,
        out_specs=[_rows(tm, d)] * n,
        out_shape=[jax.ShapeDtypeStruct((t, d), BF16)] * n,
        compiler_params=_params(1),
    )(x, *gains)


def _ple_grads(dx, gl, pp):
    sg = _sig(gl)
    return (dx * pp * sg * (1.0 - sg)).astype(BF16), (dx * sg).astype(BF16)


def _rmsnorm_bwd(x, gains, dys, add, name, ple=None, dx_dtype=BF16):
    t, d = x.shape
    tm = _pick(t, 512, 16)
    n = len(gains)
    n_ple = 0 if ple is None else 2

    def body(*refs):
        x_ref, add_ref = refs[0], refs[1]
        g_refs, dy_refs = refs[2:2 + n], refs[2 + n:2 + 2 * n]
        ple_refs = refs[2 + 2 * n:2 + 2 * n + n_ple]
        outs = refs[2 + 2 * n + n_ple:]
        dx_ref, dg_refs, dple_refs = outs[0], outs[1:1 + n], outs[1 + n:]
        i = pl.program_id(0)
        xv = x_ref[...]
        r = lax.rsqrt(jnp.mean(xv * xv, axis=-1, keepdims=True) + EPS)
        xhat = xv * r
        dx = add_ref[...].astype(F32)
        for g_ref, dy_ref, dg_ref in zip(g_refs, dy_refs, dg_refs):
            dy = dy_ref[...].astype(F32)
            dyg = dy * g_ref[...]
            dx = dx + r * (dyg - xhat * jnp.mean(dyg * xhat, axis=-1, keepdims=True))
            part = _colsum8(dy * xhat)

            @pl.when(i == 0)
            def _():
                dg_ref[...] = part

            @pl.when(i > 0)
            def _():
                dg_ref[...] += part

        dx_ref[...] = dx.astype(dx_dtype)
        if n_ple:
            dple_refs[0][...], dple_refs[1][...] = _ple_grads(dx, ple_refs[0][...].astype(F32),
                                                               ple_refs[1][...].astype(F32))

    outs = pl.pallas_call(
        body, name=name, grid=(t // tm,),
        in_specs=[_rows(tm, d), _rows(tm, d)] + [_const((1, d))] * n + [_rows(tm, d)] * (n + n_ple),
        out_specs=[_rows(tm, d)] + [_const((V7X_SUBLANES, d))] * n + [_rows(tm, d)] * n_ple,
        out_shape=([jax.ShapeDtypeStruct((t, d), dx_dtype)] + [jax.ShapeDtypeStruct((V7X_SUBLANES, d), F32)] * n
                   + [jax.ShapeDtypeStruct((t, d), BF16)] * n_ple),
        compiler_params=_params(1),
    )(x, add, *gains, *dys, *(ple or ()))
    dgs = [o.sum(axis=0) for o in outs[1:1 + n]]
    return (outs[0], dgs) if ple is None else (outs[0], dgs, outs[1 + n], outs[2 + n])


def _ple_norm_fwd(h, gl, pp, gains, name):
    t, d = h.shape
    tm = _pick(t, 512, 16)
    n = len(gains)

    def body(*refs):
        h_ref, gl_ref, pp_ref = refs[:3]
        g_refs, x_ref, o_refs = refs[3:3 + n], refs[3 + n], refs[4 + n:]
        xv = h_ref[...] + _sig(gl_ref[...].astype(F32)) * pp_ref[...].astype(F32)
        x_ref[...] = xv
        y = xv * lax.rsqrt(jnp.mean(xv * xv, axis=-1, keepdims=True) + EPS)
        for g_ref, o_ref in zip(g_refs, o_refs):
            o_ref[...] = (y * g_ref[...]).astype(BF16)

    outs = pl.pallas_call(
        body, name=name, grid=(t // tm,),
        in_specs=[_rows(tm, d)] * 3 + [_const((1, d))] * n, out_specs=[_rows(tm, d)] * (1 + n),
        out_shape=[jax.ShapeDtypeStruct((t, d), F32)] + [jax.ShapeDtypeStruct((t, d), BF16)] * n,
        compiler_params=_params(1),
    )(h, gl, pp, *gains)
    return outs[0], outs[1:]


def _ple_loss(h, gl, pp, target, name):
    t, d = h.shape
    tm = _pick(t, 512, 16)
    inv_d = 1.0 / d

    def body(h_ref, gl_ref, pp_ref, t_ref, dy_ref, dgl_ref, dpp_ref, l_ref):
        i = pl.program_id(0)
        gl, pp = gl_ref[...].astype(F32), pp_ref[...].astype(F32)
        e = h_ref[...] + _sig(gl) * pp - t_ref[...]
        dy = e * inv_d
        dy_ref[...] = dy.astype(BF16)
        dgl_ref[...], dpp_ref[...] = _ple_grads(dy, gl, pp)
        part = _colsum8(e * e) * (0.5 * inv_d)

        @pl.when(i == 0)
        def _():
            l_ref[...] = part

        @pl.when(i > 0)
        def _():
            l_ref[...] += part

    return pl.pallas_call(
        body, name=name, grid=(t // tm,), in_specs=[_rows(tm, d)] * 4,
        out_specs=[_rows(tm, d)] * 3 + [_const((V7X_SUBLANES, d))],
        out_shape=[jax.ShapeDtypeStruct((t, d), BF16), jax.ShapeDtypeStruct((t, d), BF16),
                   jax.ShapeDtypeStruct((t, d), BF16), jax.ShapeDtypeStruct((V7X_SUBLANES, d), F32)],
        compiler_params=_params(1),
    )(h, gl, pp, target)


def _shift_scratch(ts, cc):
    return pltpu.VMEM((V7X_SUBLANES, ts + HALO - V7X_SUBLANES, cc), F32)


def _shifted_copies(sh_ref, win_ref, cs, ts):
    rows = ts + HALO - V7X_SUBLANES
    for s in range(1, V7X_SUBLANES):
        sh_ref[s] = win_ref[pl.ds(s, rows), cs]


def _tap(sh_ref, win_ref, cs, offset, rows, r0):
    s = offset % V7X_SUBLANES
    start = pl.multiple_of(r0 + (offset - s), V7X_SUBLANES)
    if s == 0:
        return win_ref[pl.ds(start, rows), cs]
    return sh_ref[s, pl.ds(start, rows), :]


def _conv_fwd(proj, conv_w, conv_b, ln_g, ln_b, seq, name, ex=None):
    t, c3 = proj.shape
    c = c3 // 3
    ts = _pick(seq, 256, HALO)
    nsb = seq // ts
    cc = _pick(c, 512, V7X_LANES)
    hb = ts // HALO

    def body(a_ref, b_ref, z_ref, ap_ref, bp_ref, w_ref, cb_ref, g_ref, be_ref, m_ref, y_ref, win_ref, sh_ref):
        i = pl.program_id(0)
        first = (i % nsb) == 0
        win_ref[0:HALO, :] = jnp.where(first, 0.0, ap_ref[...] * _sig(bp_ref[...]))
        win_ref[HALO:, :] = a_ref[...] * _sig(b_ref[...])
        for ci in range(c // cc):
            cs = slice(ci * cc, (ci + 1) * cc)
            _shifted_copies(sh_ref, win_ref, cs, ts)

            def out_rows(rb, carry, cs=cs):
                r0 = rb * CONV_ROWS
                acc = jnp.zeros((CONV_ROWS, cc), F32) + cb_ref[:, cs]
                for k in range(CONV_WIDTH):
                    acc = acc + w_ref[k:k + 1, cs] * _tap(sh_ref, win_ref, cs, HALO - (CONV_WIDTH - 1) + k,
                                                           CONV_ROWS, r0)
                y_ref[pl.ds(pl.multiple_of(r0, CONV_ROWS), CONV_ROWS), cs] = acc
                return carry

            lax.fori_loop(0, ts // CONV_ROWS, out_rows, 0, unroll=2)
        y = y_ref[...]
        mu = jnp.mean(y, axis=-1, keepdims=True)
        xc = y - mu
        rstd = lax.rsqrt(jnp.mean(xc * xc, axis=-1, keepdims=True) + EPS)
        ln = xc * rstd * g_ref[...] + be_ref[...]
        zz = z_ref[...]
        m_ref[...] = (ln * _sig(ln) * zz * _sig(zz)).astype(BF16)

    halo_a = pl.BlockSpec((HALO, c), lambda i: (jnp.maximum(i * hb - 1, 0), 0))
    halo_b = pl.BlockSpec((HALO, c), lambda i: (jnp.maximum(i * hb - 1, 0), 1))
    (m_act, y), moved = _hosted_call(
        body, ex, name, (t // ts,),
        [_rows(ts, c, 0), _rows(ts, c, 1), _rows(ts, c, 2), halo_a, halo_b,
         _const((CONV_WIDTH, c)), _const((1, c)), _const((1, c)), _const((1, c))],
        [_rows(ts, c), _rows(ts, c)],
        [jax.ShapeDtypeStruct((t, c), BF16), jax.ShapeDtypeStruct((t, c), F32)],
        [pltpu.VMEM((HALO + ts, c), F32), _shift_scratch(ts, cc)],
        (proj, proj, proj, proj, proj, conv_w, conv_b, ln_g, ln_b))
    return m_act, y, moved


def _ln_gate_bwd(dm, y, proj, ln_g, ln_b, name):
    t, c = y.shape
    tm = _pick(t, 256, 8)

    def body(dm_ref, y_ref, z_ref, g_ref, be_ref, dy_ref, dz_ref, dg_ref, db_ref, dcb_ref):
        i = pl.program_id(0)
        yv = y_ref[...]
        mu = jnp.mean(yv, axis=-1, keepdims=True)
        xc = yv - mu
        rstd = lax.rsqrt(jnp.mean(xc * xc, axis=-1, keepdims=True) + EPS)
        xhat = xc * rstd
        g = g_ref[...]
        ln = xhat * g + be_ref[...]
        sl = _sig(ln)
        zz = z_ref[...]
        sz = _sig(zz)
        dmv = dm_ref[...].astype(F32)
        dz_ref[...] = (dmv * (ln * sl) * (sz * (1.0 + zz * (1.0 - sz)))).astype(BF16)
        dln = dmv * (zz * sz) * (sl * (1.0 + ln * (1.0 - sl)))
        dxh = dln * g
        dyv = rstd * (dxh - jnp.mean(dxh, axis=-1, keepdims=True)
                      - xhat * jnp.mean(dxh * xhat, axis=-1, keepdims=True))
        dy_ref[...] = dyv
        parts = (_colsum8(dln * xhat), _colsum8(dln), _colsum8(dyv))

        @pl.when(i == 0)
        def _():
            for ref, part in zip((dg_ref, db_ref, dcb_ref), parts):
                ref[...] = part

        @pl.when(i > 0)
        def _():
            for ref, part in zip((dg_ref, db_ref, dcb_ref), parts):
                ref[...] += part

    acc = jax.ShapeDtypeStruct((V7X_SUBLANES, c), F32)
    outs = pl.pallas_call(
        body, name=name, grid=(t // tm,),
        in_specs=[_rows(tm, c), _rows(tm, c), _rows(tm, c, 2), _const((1, c)), _const((1, c))],
        out_specs=[_rows(tm, c), _rows(tm, c)] + [_const((V7X_SUBLANES, c))] * 3,
        out_shape=[jax.ShapeDtypeStruct((t, c), F32), jax.ShapeDtypeStruct((t, c), BF16), acc, acc, acc],
        compiler_params=_params(1),
    )(dm, y, proj, ln_g, ln_b)
    return outs[0], outs[1], outs[2].sum(axis=0), outs[3].sum(axis=0), outs[4].sum(axis=0)


def _conv_bwd(dy, dz, proj, conv_w, seq, name, ex=None):
    t, c3 = proj.shape
    c = c3 // 3
    ts = _pick(seq, 256, HALO)
    nsb = seq // ts
    cc = _pick(c, 512, V7X_LANES)
    hb = ts // HALO
    last_halo = t // HALO - 1
    back = CONV_WIDTH - 1

    def body(dy_ref, dyn_ref, dz_ref, a_ref, b_ref, ap_ref, bp_ref, w_ref, o_ref, dw_ref, win_ref, dwin_ref,
             sh_ref, dsh_ref):
        i = pl.program_id(0)
        first = (i % nsb) == 0
        last = (i % nsb) == nsb - 1
        win_ref[0:HALO, :] = jnp.where(first, 0.0, ap_ref[...] * _sig(bp_ref[...]))
        win_ref[HALO:, :] = a_ref[...] * _sig(b_ref[...])
        dwin_ref[0:ts, :] = dy_ref[...]
        dwin_ref[ts:, :] = jnp.where(last, 0.0, dyn_ref[...])

        @pl.when(i == 0)
        def _():
            dw_ref[...] = jnp.zeros_like(dw_ref)

        for ci in range(c // cc):
            cs = slice(ci * cc, (ci + 1) * cc)
            _shifted_copies(sh_ref, win_ref, cs, ts)
            _shifted_copies(dsh_ref, dwin_ref, cs, ts)

            def in_grad_rows(rb, carry, cs=cs, ci=ci):
                r0 = rb * CONV_ROWS
                rows = pl.ds(pl.multiple_of(r0, CONV_ROWS), CONV_ROWS)
                dglu = jnp.zeros((CONV_ROWS, cc), F32)
                for k in range(CONV_WIDTH):
                    dglu = dglu + w_ref[k:k + 1, cs] * _tap(dsh_ref, dwin_ref, cs, back - k, CONV_ROWS, r0)
                sbc = _sig(b_ref[rows, cs])
                o_ref[rows, cs] = (dglu * sbc).astype(BF16)
                o_ref[rows, c + ci * cc:c + (ci + 1) * cc] = (dglu * a_ref[rows, cs] * sbc * (1.0 - sbc)).astype(BF16)
                return carry

            def w_grad_rows(rb, carry, cs=cs):
                r0 = rb * CONV_W_ROWS
                dcur = dwin_ref[pl.ds(pl.multiple_of(r0, CONV_W_ROWS), CONV_W_ROWS), cs]
                for k in range(CONV_WIDTH):
                    dw_ref[k * V7X_SUBLANES:(k + 1) * V7X_SUBLANES, cs] += _colsum8(
                        dcur * _tap(sh_ref, win_ref, cs, HALO - back + k, CONV_W_ROWS, r0))
                return carry

            lax.fori_loop(0, ts // CONV_ROWS, in_grad_rows, 0, unroll=2)
            lax.fori_loop(0, ts // CONV_W_ROWS, w_grad_rows, 0)
        o_ref[:, 2 * c:] = dz_ref[...]

    halo_next = pl.BlockSpec((HALO, c), lambda i: (jnp.minimum((i + 1) * hb, last_halo), 0))
    halo_a = pl.BlockSpec((HALO, c), lambda i: (jnp.maximum(i * hb - 1, 0), 0))
    halo_b = pl.BlockSpec((HALO, c), lambda i: (jnp.maximum(i * hb - 1, 0), 1))
    (dproj, dw), moved = _hosted_call(
        body, ex, name, (t // ts,),
        [_rows(ts, c), halo_next, _rows(ts, c), _rows(ts, c, 0), _rows(ts, c, 1), halo_a, halo_b,
         _const((CONV_WIDTH, c))],
        [_rows(ts, c3), _const((CONV_WIDTH * V7X_SUBLANES, c))],
        [jax.ShapeDtypeStruct((t, c3), BF16), jax.ShapeDtypeStruct((CONV_WIDTH * V7X_SUBLANES, c), F32)],
        [pltpu.VMEM((HALO + ts, c), F32), pltpu.VMEM((ts + HALO, c), F32),
         _shift_scratch(ts, cc), _shift_scratch(ts, cc)],
        (dy, dy, dz, proj, proj, proj, proj, conv_w))
    return dproj, dw.reshape(CONV_WIDTH, V7X_SUBLANES, c).sum(axis=1), moved


def _rope_tables(seq):
    half = ROPE_DIM // 2
    inv = ROPE_THETA ** (-jnp.arange(half, dtype=F32) * (2.0 / ROPE_DIM))
    ang = jnp.arange(seq).astype(F32)[:, None] * inv[None, :]
    cos, sin = jnp.cos(ang), jnp.sin(ang)
    zeros = jnp.zeros((seq, HEAD_DIM - ROPE_DIM), F32)
    zh = jnp.zeros((seq, half), F32)
    a = jnp.concatenate([cos, cos, zeros + 1.0], axis=1)
    b = jnp.concatenate([zh, sin, zeros], axis=1)
    c = jnp.concatenate([-sin, zh, zeros], axis=1)
    rep = V7X_LANES // HEAD_DIM
    return tuple(jnp.tile(v, (1, rep)) for v in (a, b, c))


def _head_ones(d):
    head = jnp.arange(d) // HEAD_DIM
    return (head[:, None] == head[None, :]).astype(BF16)


def _rope(ch, ta, tb, tc):
    return ta * ch + tb * pltpu.roll(ch, ROPE_DIM // 2, 1) + tc * pltpu.roll(ch, V7X_LANES - ROPE_DIM // 2, 1)


def _rope_t(ch, ta, tb, tc):
    return ta * ch + pltpu.roll(tb * ch, V7X_LANES - ROPE_DIM // 2, 1) + pltpu.roll(tc * ch, ROPE_DIM // 2, 1)


def _norm_rope_bwd(xhat, r, dout, gain, ta, tb, tc, e_ref):
    dxn = _rope_t(dout, ta, tb, tc)
    dxh = dxn * gain
    dx = r * (dxh - xhat * (_segsum(dxh * xhat, e_ref) * (1.0 / HEAD_DIM)))
    return dx, _colsum8(dxn * xhat)


def _norm_rope_rows(dst_ref, src_ref, gain, ta_ref, tb_ref, tc_ref, e_ref, seq, xhat_ref=None, r_ref=None):
    for r0 in range(0, seq, ATTN_PIECE):
        rows = slice(r0, r0 + ATTN_PIECE)
        xv = src_ref[rows, :]
        r = lax.rsqrt(_segsum(xv * xv, e_ref) * (1.0 / HEAD_DIM) + EPS)
        xhat = xv * r
        if xhat_ref is not None:
            xhat_ref[rows, :] = xhat
            r_ref[rows, :] = r
        dst_ref[rows, :] = _rope(xhat * gain, ta_ref[rows, :], tb_ref[rows, :], tc_ref[rows, :])


ATTN_PIECE = 256
ATTN_UNROLL = 16
CHUNK_UNROLL = 4


def _pieces(dil, seq):
    length = seq // dil
    rows = min(length, ATTN_PIECE)
    return [(r + dil * ci * rows, r * length + ci * rows, rows) for r in range(dil) for ci in range(length // rows)]


def _strided(ref, start, rows, dil):
    if dil == 1:
        return ref[pl.ds(start, rows), :]
    return ref[pl.ds(start, rows, stride=dil), :]


def _strided_set(ref, start, rows, dil, val):
    if dil == 1:
        ref[pl.ds(start, rows), :] = val
    else:
        ref[pl.ds(start, rows, stride=dil), :] = val


def _nt(a, b):
    return lax.dot_general(a, b, (((1,), (1,)), ((), ())), preferred_element_type=F32)


def _tn(a, b):
    return lax.dot_general(a, b, (((0,), (0,)), ((), ())), preferred_element_type=F32)


def _set_bias(bias_ref):
    qi = lax.broadcasted_iota(jnp.int32, (2 * SPAN, 2 * SPAN), 0) & (SPAN - 1)
    kj = lax.broadcasted_iota(jnp.int32, (2 * SPAN, 2 * SPAN), 1)
    band = jnp.logical_and(kj >= qi, (kj - SPAN) <= qi)
    bias_ref[1] = jnp.where(band, 0.0, NEG_INF)
    bias_ref[0] = jnp.where(jnp.logical_and(band, kj >= SPAN), 0.0, NEG_INF)


def _block_keys(bias_ref, j, qs, nb):
    if nb == 1:
        return pl.ds(pl.multiple_of(qs + SPAN, SPAN), SPAN), bias_ref[1, :, SPAN:]
    return pl.ds(qs, 2 * SPAN), bias_ref[jnp.minimum(j & (nb - 1), 1)]


def _stack_heads(v, head0):
    zero = jnp.zeros_like(v)
    return jnp.concatenate([jnp.where(head0, v, zero), jnp.where(head0, zero, v)], axis=0)


def _unstack_heads(v2, head0):
    return jnp.where(head0, v2[:SPAN], v2[SPAN:])


def _head_cols(v, lane=0):
    return jnp.concatenate([v[:, lane:lane + 1], v[:, HEAD_DIM + lane:HEAD_DIM + lane + 1]], axis=0)


def _attn_fwd(proj_b, kv, gains, tables, ones, bsz, seq, name):
    t, d4 = proj_b.shape
    d = d4 // 4
    nhp = d // V7X_LANES
    nblk = seq // SPAN
    scale = HEAD_DIM ** -0.5
    n_groups = len(DILATIONS)

    def body(q0_ref, q1_ref, q2_ref, k_ref, v_ref, gate_ref, gain_ref, ta_ref, tb_ref, tc_ref, e_ref,
             o_ref, l_ref, ao_ref, qd, kd, vd, od, ld, on0, on1, on2, ln0, ln1, ln2, kn, qn, bias):
        head0 = lax.broadcasted_iota(jnp.int32, (SPAN, V7X_LANES), 1) < HEAD_DIM

        @pl.when(jnp.logical_and(pl.program_id(0) == 0, pl.program_id(1) == 0))
        def _():
            _set_bias(bias)

        _norm_rope_rows(kn, k_ref, gain_ref[n_groups:n_groups + 1, :], ta_ref, tb_ref, tc_ref, e_ref, seq)
        kd[0:SPAN, :] = jnp.zeros((SPAN, V7X_LANES), BF16)
        vd[0:SPAN, :] = jnp.zeros((SPAN, V7X_LANES), BF16)
        for g, (q_ref, on, ln) in enumerate(((q0_ref, on0, ln0), (q1_ref, on1, ln1), (q2_ref, on2, ln2))):
            dil = DILATIONS[g]
            nb = seq // dil // SPAN
            _norm_rope_rows(qn, q_ref, gain_ref[g:g + 1, :], ta_ref, tb_ref, tc_ref, e_ref, seq)
            for ns, rs, rows in _pieces(dil, seq):
                qd[rs:rs + rows, :] = _strided(qn, ns, rows, dil).astype(BF16)
                kd[SPAN + rs:SPAN + rs + rows, :] = _strided(kn, ns, rows, dil).astype(BF16)
                vd[SPAN + rs:SPAN + rs + rows, :] = _strided(v_ref, ns, rows, dil).astype(BF16)

            def block(j, carry):
                qs = pl.multiple_of(j * SPAN, SPAN)
                q2 = _stack_heads(qd[pl.ds(qs, SPAN), :], head0)
                keys, mask = _block_keys(bias, j, qs, nb)
                kk = kd[keys, :]
                vv = vd[keys, :]
                s = _nt(q2, kk) * scale + mask
                mx = jnp.max(s, axis=1, keepdims=True)
                p = jnp.exp(s - mx)
                den = jnp.sum(p, axis=1, keepdims=True)
                o2 = jnp.dot(p.astype(BF16), vv, preferred_element_type=F32) / den
                l2 = jnp.broadcast_to(mx + jnp.log(den), (2 * SPAN, V7X_LANES))
                od[pl.ds(qs, SPAN), :] = _unstack_heads(o2, head0)
                ld[pl.ds(qs, SPAN), :] = _unstack_heads(l2, head0)
                return carry

            lax.fori_loop(0, nblk, block, 0, unroll=ATTN_UNROLL)
            for ns, rs, rows in _pieces(dil, seq):
                _strided_set(on, ns, rows, dil, od[rs:rs + rows, :])
                _strided_set(ln, ns, rows, dil, ld[rs:rs + rows, :])

        def merge(ci, carry):
            rows = pl.ds(pl.multiple_of(ci * ATTN_PIECE, ATTN_PIECE), ATTN_PIECE)
            ls = [ln0[rows, :], ln1[rows, :], ln2[rows, :]]
            mx = jnp.maximum(jnp.maximum(ls[0], ls[1]), ls[2])
            es = [jnp.exp(v - mx) for v in ls]
            den = es[0] + es[1] + es[2]
            ov = (es[0] * on0[rows, :] + es[1] * on1[rows, :] + es[2] * on2[rows, :]) / den
            gate = gate_ref[rows, :]
            o_ref[rows, :] = ov
            l_ref[rows, :] = mx + jnp.log(den)
            ao_ref[rows, :] = (ov * gate * _sig(gate)).astype(BF16)
            return carry

        lax.fori_loop(0, seq // ATTN_PIECE, merge, 0)

    blk = (None, seq, V7X_LANES)
    pview = proj_b.reshape(bsz, seq, d4)
    kview = kv.reshape(bsz, seq, 2 * d)
    out_spec = pl.BlockSpec(blk, lambda b, h: (b, 0, h))
    tab = pl.BlockSpec((seq, V7X_LANES), lambda b, h: (0, 0))
    nat = pltpu.VMEM((seq, V7X_LANES), F32)
    o, lse, ao = pl.pallas_call(
        body, name=name, grid=(bsz, nhp),
        in_specs=[pl.BlockSpec(blk, lambda b, h: (b, 0, h)),
                  pl.BlockSpec(blk, lambda b, h: (b, 0, nhp + h)),
                  pl.BlockSpec(blk, lambda b, h: (b, 0, 2 * nhp + h)),
                  pl.BlockSpec(blk, lambda b, h: (b, 0, h)),
                  pl.BlockSpec(blk, lambda b, h: (b, 0, nhp + h)),
                  pl.BlockSpec(blk, lambda b, h: (b, 0, 3 * nhp + h)),
                  pl.BlockSpec((n_groups + 1, V7X_LANES), lambda b, h: (0, 0)),
                  tab, tab, tab,
                  pl.BlockSpec((V7X_LANES, V7X_LANES), lambda b, h: (0, 0))],
        out_specs=[out_spec, out_spec, out_spec],
        out_shape=[jax.ShapeDtypeStruct((bsz, seq, d), F32), jax.ShapeDtypeStruct((bsz, seq, d), F32),
                   jax.ShapeDtypeStruct((bsz, seq, d), BF16)],
        scratch_shapes=[pltpu.VMEM((seq, V7X_LANES), BF16), pltpu.VMEM((SPAN + seq, V7X_LANES), BF16),
                        pltpu.VMEM((SPAN + seq, V7X_LANES), BF16), nat, nat, nat, nat, nat, nat, nat, nat, nat, nat,
                        pltpu.VMEM((2, 2 * SPAN, 2 * SPAN), F32)],
        compiler_params=_params(2),
    )(pview, pview, pview, kview, kview, pview, gains, *tables, ones)
    return o.reshape(t, d), lse.reshape(t, d), ao.reshape(t, d)


def _attn_bwd(proj_b, kv, dao, o, lse, gains, tables, ones, bsz, seq, name):
    t, d4 = proj_b.shape
    d = d4 // 4
    nhp = d // V7X_LANES
    nblk = seq // SPAN
    scale = HEAD_DIM ** -0.5
    n_groups = len(DILATIONS)
    n_chunks = seq // ATTN_PIECE

    def body(q_ref, k_ref, v_ref, gate_ref, dao_ref, o_ref, l_ref, gain_ref, ta_ref, tb_ref, tc_ref, e_ref,
             dproj_ref, dkv_ref, dg_ref, qd, kd, vd, dod, std, dqd, dkd, dvd, dqn, dk0, dk1, dk2, dv0, dv1, dv2,
             kn, kxh, krr, qn, qxh, qrr, don, stn, bias):
        head0 = lax.broadcasted_iota(jnp.int32, (SPAN, V7X_LANES), 1) < HEAD_DIM
        g = pl.program_id(2)

        @pl.when(jnp.logical_and(jnp.logical_and(pl.program_id(0) == 0, pl.program_id(1) == 0), g == 0))
        def _():
            _set_bias(bias)
            dg_ref[...] = jnp.zeros_like(dg_ref)

        @pl.when(g == 0)
        def _():
            first_half = (lax.broadcasted_iota(jnp.int32, (ATTN_PIECE, V7X_LANES), 1) & (HEAD_DIM - 1)) < HEAD_DIM // 2
            _norm_rope_rows(kn, k_ref, gain_ref[n_groups:n_groups + 1, :], ta_ref, tb_ref, tc_ref, e_ref, seq,
                            kxh, krr)
            for r0 in range(0, seq, ATTN_PIECE):
                rows = slice(r0, r0 + ATTN_PIECE)
                gate = gate_ref[rows, :]
                dov = dao_ref[rows, :].astype(F32) * gate * _sig(gate)
                don[rows, :] = dov
                stn[rows, :] = jnp.where(first_half, l_ref[rows, :], _segsum(dov * o_ref[rows, :], e_ref))

        def norm_bwd_chunks(xhat_ref, r_ref, dn_refs, out_ref, gi):
            def chunk(ci, carry):
                rows = pl.ds(pl.multiple_of(ci * ATTN_PIECE, ATTN_PIECE), ATTN_PIECE)
                dn = functools.reduce(lambda u, w: u + w, [r_[rows, :] for r_ in dn_refs])
                dx, part = _norm_rope_bwd(xhat_ref[rows, :], r_ref[rows, :], dn, gain_ref[gi:gi + 1, :],
                                          ta_ref[rows, :], tb_ref[rows, :], tc_ref[rows, :], e_ref)
                out_ref[rows, :] = dx.astype(BF16)
                dg_ref[gi] += part
                return carry
            lax.fori_loop(0, n_chunks, chunk, 0, unroll=CHUNK_UNROLL)

        def group(gi):
            dil = DILATIONS[gi]
            nb = seq // dil // SPAN
            kd[0:SPAN, :] = jnp.zeros((SPAN, V7X_LANES), BF16)
            vd[0:SPAN, :] = jnp.zeros((SPAN, V7X_LANES), BF16)
            dkd[...] = jnp.zeros_like(dkd)
            dvd[...] = jnp.zeros_like(dvd)
            _norm_rope_rows(qn, q_ref, gain_ref[gi:gi + 1, :], ta_ref, tb_ref, tc_ref, e_ref, seq, qxh, qrr)
            for ns, rs, rows in _pieces(dil, seq):
                qd[rs:rs + rows, :] = _strided(qn, ns, rows, dil).astype(BF16)
                kd[SPAN + rs:SPAN + rs + rows, :] = _strided(kn, ns, rows, dil).astype(BF16)
                vd[SPAN + rs:SPAN + rs + rows, :] = _strided(v_ref, ns, rows, dil).astype(BF16)
                dod[rs:rs + rows, :] = _strided(don, ns, rows, dil).astype(BF16)
                std[rs:rs + rows, :] = _strided(stn, ns, rows, dil)

            def block(j, carry):
                qs = pl.multiple_of(j * SPAN, SPAN)
                q2 = _stack_heads(qd[pl.ds(qs, SPAN), :], head0)
                do2 = _stack_heads(dod[pl.ds(qs, SPAN), :], head0)
                keys, mask = _block_keys(bias, j, qs, nb)
                kk = kd[keys, :]
                vv = vd[keys, :]
                s = _nt(q2, kk) * scale + mask
                stv = std[pl.ds(qs, SPAN), :]
                p = jnp.exp(s - _head_cols(stv))
                ds = (p * (_nt(do2, vv) - _head_cols(stv, HEAD_DIM // 2)) * scale).astype(BF16)
                dqd[pl.ds(qs, SPAN), :] = _unstack_heads(jnp.dot(ds, kk, preferred_element_type=F32), head0)
                dkd[keys, :] += _tn(ds, q2)
                dvd[keys, :] += _tn(p.astype(BF16), do2)
                return carry

            lax.fori_loop(0, nblk, block, 0, unroll=ATTN_UNROLL)
            for ns, rs, rows in _pieces(dil, seq):
                _strided_set(dqn, ns, rows, dil, dqd[rs:rs + rows, :])
                _strided_set((dk0, dk1, dk2)[gi], ns, rows, dil, dkd[SPAN + rs:SPAN + rs + rows, :])
                _strided_set((dv0, dv1, dv2)[gi], ns, rows, dil, dvd[SPAN + rs:SPAN + rs + rows, :])
            norm_bwd_chunks(qxh, qrr, [dqn], dproj_ref, gi)

        for gi in range(n_groups):
            @pl.when(g == gi)
            def _():
                group(gi)

        @pl.when(g == n_groups - 1)
        def _():
            norm_bwd_chunks(kxh, krr, [dk0, dk1, dk2], dkv_ref, n_groups)

        @pl.when(g == n_groups)
        def _():
            def chunk(ci, carry):
                rows = pl.ds(pl.multiple_of(ci * ATTN_PIECE, ATTN_PIECE), ATTN_PIECE)
                gate = gate_ref[rows, :]
                sg = _sig(gate)
                dproj_ref[rows, :] = (dao_ref[rows, :].astype(F32) * o_ref[rows, :]
                                      * (sg * (1.0 + gate * (1.0 - sg)))).astype(BF16)
                dkv_ref[rows, :] = (dv0[rows, :] + dv1[rows, :] + dv2[rows, :]).astype(BF16)
                return carry
            lax.fori_loop(0, n_chunks, chunk, 0, unroll=CHUNK_UNROLL)

    blk = (None, seq, V7X_LANES)
    pview = proj_b.reshape(bsz, seq, d4)
    kview = kv.reshape(bsz, seq, 2 * d)
    dview = (bsz, seq, d)
    d_spec = pl.BlockSpec(blk, lambda b, h, g: (b, 0, h))
    tab = pl.BlockSpec((seq, V7X_LANES), lambda b, h, g: (0, 0))
    nat = pltpu.VMEM((seq, V7X_LANES), F32)
    natb = pltpu.VMEM((seq, V7X_LANES), BF16)
    pad = pltpu.VMEM((SPAN + seq, V7X_LANES), F32)
    padb = pltpu.VMEM((SPAN + seq, V7X_LANES), BF16)
    dproj, dkv, dg = pl.pallas_call(
        body, name=name, grid=(bsz, nhp, n_groups + 1),
        in_specs=[pl.BlockSpec(blk, lambda b, h, g: (b, 0, jnp.minimum(g, n_groups - 1) * nhp + h)),
                  pl.BlockSpec(blk, lambda b, h, g: (b, 0, h)),
                  pl.BlockSpec(blk, lambda b, h, g: (b, 0, nhp + h)),
                  pl.BlockSpec(blk, lambda b, h, g: (b, 0, n_groups * nhp + h)),
                  d_spec, d_spec, d_spec,
                  pl.BlockSpec((n_groups + 1, V7X_LANES), lambda b, h, g: (0, 0)),
                  tab, tab, tab,
                  pl.BlockSpec((V7X_LANES, V7X_LANES), lambda b, h, g: (0, 0))],
        out_specs=[pl.BlockSpec(blk, lambda b, h, g: (b, 0, g * nhp + h)),
                   pl.BlockSpec(blk, lambda b, h, g: (b, 0, (g // n_groups) * nhp + h)),
                   pl.BlockSpec((n_groups + 1, V7X_SUBLANES, V7X_LANES), lambda b, h, g: (0, 0, 0))],
        out_shape=[jax.ShapeDtypeStruct((bsz, seq, d4), BF16), jax.ShapeDtypeStruct((bsz, seq, 2 * d), BF16),
                   jax.ShapeDtypeStruct((n_groups + 1, V7X_SUBLANES, V7X_LANES), F32)],
        scratch_shapes=[natb, padb, padb, natb, nat, nat, pad, pad] + [nat] * 15 + [
                        pltpu.VMEM((2, 2 * SPAN, 2 * SPAN), F32)],
        compiler_params=_params(3),
    )(pview, kview, kview, pview, dao.reshape(dview), o.reshape(dview), lse.reshape(dview), gains, *tables, ones)
    dgain = dg.sum(axis=1).reshape(n_groups + 1, V7X_LANES // HEAD_DIM, HEAD_DIM).sum(axis=1)
    return dproj.reshape(t, d4), dkv.reshape(t, 2 * d), dgain


def _mesh_position():
    x, y, c = lax.axis_index("x"), lax.axis_index("y"), lax.axis_index("c")
    return x, y, c


def _peer(x, y, c, rel):
    return (1 - x if rel & 4 else x, 1 - y if rel & 2 else y, 1 - c if rel & 1 else c)


class _Exchange:
    def __init__(self, srcs, gather):
        self.srcs = list(srcs)
        self.gather = gather
        n = self.n = len(self.srcs)
        hbm = pl.BlockSpec(memory_space=pltpu.HBM)
        self.in_specs = [hbm] * n
        self.out_specs = [hbm] * n
        self.out_shape = [jax.ShapeDtypeStruct(((N_DEV,) + a.shape) if gather else a.shape, a.dtype)
                          for a in self.srcs]
        self.scratch = [pltpu.SemaphoreType.DMA((n * (N_DEV - 1),)), pltpu.SemaphoreType.DMA((n * (N_DEV - 1),)),
                        pltpu.SemaphoreType.DMA((n,))]

    def _copies(self, ins, outs, sems):
        send_sems, recv_sems, local_sems = sems
        x, y, c = _mesh_position()
        me = 4 * x + 2 * y + c
        remote, local = [], []
        for a in range(self.n):
            mine = ins[a] if self.gather else ins[a].at[me]
            local.append(pltpu.make_async_copy(mine, outs[a].at[me], local_sems.at[a]))
            for rel in range(1, N_DEV):
                px, py, pc = _peer(x, y, c, rel)
                s = a * (N_DEV - 1) + rel - 1
                src = ins[a] if self.gather else ins[a].at[4 * px + 2 * py + pc]
                remote.append(pltpu.make_async_remote_copy(
                    src_ref=src, dst_ref=outs[a].at[me], send_sem=send_sems.at[s], recv_sem=recv_sems.at[s],
                    device_id=(px, py, pc), device_id_type=pl.DeviceIdType.MESH))
        return remote, local

    def start(self, ins, outs, sems):
        remote, local = self._copies(ins, outs, sems)
        for cp in local + remote:
            cp.start()

    def wait(self, ins, outs, sems):
        remote, local = self._copies(ins, outs, sems)
        for cp in remote:
            cp.wait_recv()
        for cp in remote:
            cp.wait_send()
        for cp in local:
            cp.wait()


def _gather_chip_once(arrs, name):
    n = len(arrs)
    per = N_DEV - 1

    def body(*refs):
        ins, outs = refs[:n], refs[n:2 * n]
        send_sems, recv_sems, local_sems = refs[2 * n:]
        x, y, c = _mesh_position()
        me, sibling = (x, y, c), (x, y, 1 - c)
        chips = [(1 - x, y), (x, 1 - y), (1 - x, 1 - y)]

        def copy(a, k, block, to, src=None):
            bx, by, bc = block
            dst = outs[a].at[4 * bx + 2 * by + bc]
            return pltpu.make_async_remote_copy(
                src_ref=dst if src is None else src, dst_ref=dst, send_sem=send_sems.at[a * per + k],
                recv_sem=recv_sems.at[a * per + k], device_id=to, device_id_type=pl.DeviceIdType.MESH)

        local, sent = [], []
        for a in range(n):
            mine = pltpu.make_async_copy(ins[a], outs[a].at[4 * x + 2 * y + c], local_sems.at[a])
            mine.start()
            local.append(mine)
            first = [copy(a, 0, me, sibling, src=ins[a])]
            first += [copy(a, 1 + j, me, chip + (c,), src=ins[a]) for j, chip in enumerate(chips)]
            for cp in first:
                cp.start()
            sent += first
        for a in range(n):
            for j, chip in enumerate(chips):
                copy(a, 1 + j, chip + (c,), me).wait_recv()
                passed = copy(a, 4 + j, chip + (c,), sibling)
                passed.start()
                sent.append(passed)
        for a in range(n):
            copy(a, 0, sibling, me).wait_recv()
            for j, chip in enumerate(chips):
                copy(a, 4 + j, chip + (1 - c,), me).wait_recv()
        for cp in sent:
            cp.wait_send()
        for cp in local:
            cp.wait()

    hbm = pl.BlockSpec(memory_space=pltpu.HBM)
    return pl.pallas_call(
        body, name=name, in_specs=[hbm] * n, out_specs=[hbm] * n,
        out_shape=[jax.ShapeDtypeStruct((N_DEV,) + a.shape, a.dtype) for a in arrs],
        scratch_shapes=[pltpu.SemaphoreType.DMA((n * per,)), pltpu.SemaphoreType.DMA((n * per,)),
                        pltpu.SemaphoreType.DMA((n,))],
    )(*arrs)


def _run_exchange(ex, name):
    n = ex.n

    def body(*refs):
        ins, outs, sems = refs[:n], refs[n:2 * n], refs[2 * n:]
        ex.start(ins, outs, sems)
        ex.wait(ins, outs, sems)

    return pl.pallas_call(body, name=name, in_specs=ex.in_specs, out_specs=ex.out_specs, out_shape=ex.out_shape,
                          scratch_shapes=ex.scratch)(*ex.srcs)


def _hosted_call(body, ex, name, grid, in_specs, out_specs, out_shape, scratch_shapes, args):
    if ex is None:
        outs = pl.pallas_call(body, name=name, grid=grid, in_specs=in_specs, out_specs=out_specs, out_shape=out_shape,
                              scratch_shapes=scratch_shapes, compiler_params=_params(len(grid)))(*args)
        return list(outs), []
    n_in, n_out, n_scr, n = len(in_specs), len(out_specs), len(scratch_shapes), ex.n

    def hosted(*refs):
        h_in, e_in = refs[:n_in], refs[n_in:n_in + n]
        o0 = n_in + n
        h_out, e_out = refs[o0:o0 + n_out], refs[o0 + n_out:o0 + n_out + n]
        s0 = o0 + n_out + n
        h_scr, e_scr = refs[s0:s0 + n_scr], refs[s0 + n_scr:]
        ids = [pl.program_id(a) for a in range(len(grid))]
        first = functools.reduce(jnp.logical_and, [i == 0 for i in ids])
        last = functools.reduce(jnp.logical_and, [i == g - 1 for i, g in zip(ids, grid)])

        @pl.when(first)
        def _():
            ex.start(e_in, e_out, e_scr)

        body(*h_in, *h_out, *h_scr)

        @pl.when(last)
        def _():
            ex.wait(e_in, e_out, e_scr)

    outs = pl.pallas_call(
        hosted, name=name, grid=grid, in_specs=list(in_specs) + ex.in_specs,
        out_specs=list(out_specs) + ex.out_specs, out_shape=list(out_shape) + ex.out_shape,
        scratch_shapes=list(scratch_shapes) + ex.scratch, compiler_params=_params(len(grid)),
    )(*args, *ex.srcs)
    return list(outs[:n_out]), list(outs[n_out:])


def _sum_adamw(parts, w, m, v, name):
    _, r, wd = parts.shape
    tr = _pick(r, ADAM_ROWS, 8)
    c1 = 1.0 - ADAM_B1 ** ADAM_STEP
    c2 = 1.0 - ADAM_B2 ** ADAM_STEP

    def body(p_ref, w_ref, m_ref, v_ref, g_ref, d_ref, nm_ref, nv_ref):
        g = p_ref[0].astype(F32)
        for s in range(1, N_DEV):
            g = g + p_ref[s].astype(F32)
        nm = ADAM_B1 * m_ref[...] + (1.0 - ADAM_B1) * g
        nv = ADAM_B2 * v_ref[...] + (1.0 - ADAM_B2) * (g * g)
        g_ref[...] = g
        nm_ref[...] = nm
        nv_ref[...] = nv
        d_ref[...] = -ADAM_LR * ((nm / c1) / (jnp.sqrt(nv / c2) + ADAM_EPS) + ADAM_WD * w_ref[...])

    row = pl.BlockSpec((tr, wd), lambda i: (i, 0))
    return pl.pallas_call(
        body, name=name, grid=(r // tr,),
        in_specs=[pl.BlockSpec((N_DEV, tr, wd), lambda i: (0, i, 0)), row, row, row],
        out_specs=[row] * 4, out_shape=[jax.ShapeDtypeStruct((r, wd), F32)] * 4,
        compiler_params=_params(1),
    )(parts, w, m, v)


def _pack_rows(size, row_mult):
    rows = -(-size // PACK_LANES)
    return -(-rows // row_mult) * row_mult


def _pack(flats, row_mult, dtype, total_mult=None):
    out = []
    for f in flats:
        size = f.shape[-1]
        rows = _pack_rows(size, row_mult)
        pad = [(0, 0)] * (f.ndim - 1) + [(0, rows * PACK_LANES - size)]
        out.append(jnp.pad(f.astype(dtype), pad).reshape(f.shape[:-1] + (rows, PACK_LANES)))
    if total_mult is not None:
        total = sum(o.shape[-2] for o in out)
        extra = -(-total // total_mult) * total_mult - total
        if extra:
            out.append(jnp.zeros(out[0].shape[:-2] + (extra, PACK_LANES), dtype))
    return jnp.concatenate(out, axis=-2)


def _unpack(buf, sizes, row_mult):
    out, row = [], 0
    for size in sizes:
        rows = _pack_rows(size, row_mult)
        part = buf[..., row:row + rows, :]
        out.append(part.reshape(buf.shape[:-2] + (rows * PACK_LANES,))[..., :size])
        row += rows
    return out


def _to_slots(full, axis):
    if axis is None:
        return jnp.broadcast_to(full.reshape(1, -1), (N_DEV, full.size))
    shape = full.shape
    split = full.reshape(shape[:axis] + (N_DEV, shape[axis] // N_DEV) + shape[axis + 1:])
    return jnp.moveaxis(split, axis, 0).reshape(N_DEV, -1)


def _from_slots(slots, axis, block_shape):
    split = jnp.moveaxis(slots, 0, axis)
    shape = list(block_shape)
    shape[axis] *= N_DEV
    return split.reshape(shape)


def kernel(x, p, norm_g, w_in_a, conv_w, conv_b, ln_g, ln_b, w_out_a, kv_norm_g, w_kv, k_norm_g, w_in_b, q_norm_g, w_out_b, ple_norm_g, w_ple_gate, w_ple_proj, loss_target, m_norm_g, m_w_in_a, m_conv_w, m_conv_b, m_ln_g, m_ln_b, m_w_out_a, m_kv_norm_g, m_w_kv, m_k_norm_g, m_w_in_b, m_q_norm_g, m_w_out_b, m_ple_norm_g, m_w_ple_gate, m_w_ple_proj, v_norm_g, v_w_in_a, v_conv_w, v_conv_b, v_ln_g, v_ln_b, v_w_out_a, v_kv_norm_g, v_w_kv, v_k_norm_g, v_w_in_b, v_q_norm_g, v_w_out_b, v_ple_norm_g, v_w_ple_gate, v_w_ple_proj):
    weights = dict(zip(WEIGHT_NAMES, (norm_g, w_in_a, conv_w, conv_b, ln_g, ln_b, w_out_a, kv_norm_g, w_kv, k_norm_g,
                                      w_in_b, q_norm_g, w_out_b, ple_norm_g, w_ple_gate, w_ple_proj)))
    mom_m = dict(zip(WEIGHT_NAMES, (m_norm_g, m_w_in_a, m_conv_w, m_conv_b, m_ln_g, m_ln_b, m_w_out_a, m_kv_norm_g,
                                    m_w_kv, m_k_norm_g, m_w_in_b, m_q_norm_g, m_w_out_b, m_ple_norm_g, m_w_ple_gate,
                                    m_w_ple_proj)))
    mom_v = dict(zip(WEIGHT_NAMES, (v_norm_g, v_w_in_a, v_conv_w, v_conv_b, v_ln_g, v_ln_b, v_w_out_a, v_kv_norm_g,
                                    v_w_kv, v_k_norm_g, v_w_in_b, v_q_norm_g, v_w_out_b, v_ple_norm_g, v_w_ple_gate,
                                    v_w_ple_proj)))
    bsz, seq, d = x.shape
    t = bsz * seq
    assert seq % (max(DILATIONS) * SPAN) == 0 and d % V7X_LANES == 0

    full = {}

    def rows2d(a):
        return a.reshape(-1, a.shape[-1])

    def packed(source, names, dtype, total_mult=None):
        return _pack([source[n].reshape(-1) for n in names], 16, dtype, total_mult)

    def gathered(names, bufs):
        for n, buf in zip(names, bufs):
            full[n] = _from_slots(buf.reshape((N_DEV,) + weights[n].shape), SHARD_AXIS[n], weights[n].shape)

    w1_all, wv_all = _gather_chip_once([rows2d(weights['w_in_a']).astype(BF16),
                                        _pack([weights[n].reshape(-1) for n in VECTOR_WEIGHTS], 8, F32)],
                                       "gather_first")
    gathered(GROUP_FIRST, [w1_all])
    for n, slots in zip(VECTOR_WEIGHTS, _unpack(wv_all, [weights[n].size for n in VECTOR_WEIGHTS], 8)):
        full[n] = _from_slots(slots.reshape((N_DEV,) + weights[n].shape), SHARD_AXIS[n], weights[n].shape)
    gather_rest = _Exchange([rows2d(weights[n]).astype(BF16) for n in GROUP_REST], gather=True)
    wa_in = full['w_in_a'][0]
    cw, cb, lg, lb = full['conv_w'][0], full['conv_b'], full['ln_g'], full['ln_b']

    tables = _rope_tables(seq)
    ones = _head_ones(V7X_LANES)
    rep = V7X_LANES // HEAD_DIM
    head_gain = jnp.concatenate([jnp.tile(q_norm_g[0], (1, rep)), jnp.tile(k_norm_g, rep)[None]], axis=0)

    x0 = x.reshape(t, d)
    p0, p1 = p[0].reshape(t, -1), p[1].reshape(t, -1)
    target = loss_target.reshape(t, d)
    g_norm0, g_norm1 = norm_g[0:1], norm_g[1:2]
    g_ple0, g_ple1 = ple_norm_g[0:1], ple_norm_g[1:2]
    g_kv = kv_norm_g.reshape(1, d)

    (u0,) = _rmsnorm_fwd(x0, [g_norm0], "norm0")
    proj_a = _matmul(u0, wa_in, 'nn', "in_a")
    m_act, y_conv, w2_all = _conv_fwd(proj_a, cw, cb, lg, lb, seq, "conv_fwd", ex=gather_rest)
    gathered(GROUP_REST, w2_all)
    wa_out = full['w_out_a'][0]
    wkv = full['w_kv']
    wb_in, wb_out = full['w_in_b'][0], full['w_out_b'][0]
    wg, wp = full['w_ple_gate'], full['w_ple_proj']
    h0, pg0 = _matmul(m_act, wa_out, 'nn', "out_a", add=x0, norm_gain=g_ple0)
    gl0 = _matmul(pg0, wg[0], 'nn', "ple_gate0", out_dtype=BF16)
    pp0 = _matmul(p0, wp[0], 'nn', "ple_proj0", out_dtype=BF16)

    x1, (kvn, u1) = _ple_norm_fwd(h0, gl0, pp0, [g_kv, g_norm1], "ple0_norm1")
    kv = _matmul(kvn, wkv, 'nn', "kv")
    proj_b = _matmul(u1, wb_in, 'nn', "in_b")
    o_att, lse, ao = _attn_fwd(proj_b, kv, head_gain, tables, ones, bsz, seq, "attn_fwd")
    h1, pg1 = _matmul(ao, wb_out, 'nn', "out_b", add=x1, norm_gain=g_ple1)
    gl1 = _matmul(pg1, wg[1], 'nn', "ple_gate1", out_dtype=BF16)
    pp1 = _matmul(p1, wp[1], 'nn', "ple_proj1", out_dtype=BF16)

    dx2, dgl1, dpp1, loss_part = _ple_loss(h1, gl1, pp1, target, "ple1_loss")
    loss = lax.psum(jnp.sum(loss_part), ("x", "y", "c"))

    grads = {}
    slot = {}

    dwp1 = _matmul(p1, dpp1, 'tn', "d_ple_proj1", out_dtype=BF16, slot_cols=d // N_DEV)
    dwg1 = _matmul(pg1, dgl1, 'tn', "d_ple_gate1", out_dtype=BF16)
    dpg1 = _matmul(dgl1, wg[1], 'nt', "d_ple_norm1", out_dtype=BF16)
    dh1, (dg_ple1,) = _rmsnorm_bwd(h1, [g_ple1], [dpg1], dx2, "ple_norm1_bwd")
    slot['w_out_b'] = _matmul(ao, dh1, 'tn', "d_out_b", out_dtype=BF16).reshape(N_DEV, -1, d)
    dao = _matmul(dh1, wb_out, 'nt', "d_ao", out_dtype=BF16)
    dproj_b, dkv, dg_head = _attn_bwd(proj_b, kv, dao, o_att, lse, head_gain, tables, ones, bsz, seq, "attn_bwd")
    slot['w_in_b'] = _matmul(u1, dproj_b, 'tn', "d_in_b", out_dtype=BF16, slot_cols=4 * d // N_DEV)
    du1 = _matmul(dproj_b, wb_in, 'nt', "d_u1", out_dtype=BF16)
    slot['w_kv'] = _matmul(kvn, dkv, 'tn', "d_kv", out_dtype=BF16, slot_cols=2 * d // N_DEV)
    dkvn = _matmul(dkv, wkv, 'nt', "d_kvn", out_dtype=BF16)
    dx1, (dg_kv, dg_norm1), dgl0, dpp0 = _rmsnorm_bwd(x1, [g_kv, g_norm1], [dkvn, du1], dh1, "norm1_bwd",
                                                      ple=(gl0, pp0))

    dwp0 = _matmul(p0, dpp0, 'tn', "d_ple_proj0", out_dtype=BF16, slot_cols=d // N_DEV)
    dwg0 = _matmul(pg0, dgl0, 'tn', "d_ple_gate0", out_dtype=BF16)
    dpg0 = _matmul(dgl0, wg[0], 'nt', "d_ple_norm0", out_dtype=BF16)
    dh0, (dg_ple0,) = _rmsnorm_bwd(h0, [g_ple0], [dpg0], dx1, "ple_norm0_bwd")
    slot['w_out_a'] = _matmul(m_act, dh0, 'tn', "d_out_a", out_dtype=BF16).reshape(N_DEV, -1, d)
    dm = _matmul(dh0, wa_out, 'nt', "d_m", out_dtype=BF16)
    dy_conv, dz, d_lg, d_lb, d_cb = _ln_gate_bwd(dm, y_conv, proj_a, lg, lb, "ln_gate_bwd")
    slot['w_ple_gate'] = jnp.stack([dwg0.reshape(N_DEV, -1, d), dwg1.reshape(N_DEV, -1, d)],
                                   axis=1).reshape(N_DEV, -1, d)
    slot['w_ple_proj'] = jnp.stack([dwp0, dwp1], axis=1).reshape(N_DEV, -1, d // N_DEV)

    dproj_a, d_cw, parts_rest = _conv_bwd(dy_conv, dz, proj_a, cw, seq, "conv_bwd",
                                          ex=_Exchange([slot[n] for n in GROUP_REST], gather=False))
    slot['w_in_a'] = _matmul(u0, dproj_a, 'tn', "d_in_a", out_dtype=BF16, slot_cols=wa_in.shape[1] // N_DEV)
    du0, parts_first = _matmul(dproj_a, wa_in, 'nt', "d_u0", out_dtype=BF16,
                               ex=_Exchange([slot['w_in_a']], gather=False))
    dx0, (dg_norm0,) = _rmsnorm_bwd(x0, [g_norm0], [du0], dh0, "norm0_bwd", dx_dtype=F32)

    grads['norm_g'] = jnp.stack([dg_norm0, dg_norm1])
    grads['conv_w'] = d_cw[None]
    grads['conv_b'] = d_cb[None]
    grads['ln_g'] = d_lg[None]
    grads['ln_b'] = d_lb[None]
    grads['kv_norm_g'] = dg_kv
    grads['k_norm_g'] = dg_head[3]
    grads['q_norm_g'] = dg_head[0:3][None]
    grads['ple_norm_g'] = jnp.stack([dg_ple0, dg_ple1])
    small_pack = _pack([_to_slots(grads[n], SHARD_AXIS[n]) for n in GROUP_SMALL], 16, BF16)
    (parts_small,) = _run_exchange(_Exchange([small_pack], gather=False), "exchange_small")

    updated = {}
    for n, parts in zip(GROUP_REST + GROUP_FIRST, parts_rest + parts_first):
        outs = _sum_adamw(parts, rows2d(weights[n]), rows2d(mom_m[n]), rows2d(mom_v[n]), "sum_adamw_" + n)
        for kind, buf in enumerate(outs):
            updated[kind, n] = buf.reshape(weights[n].shape)
    outs = _sum_adamw(parts_small, packed(weights, GROUP_SMALL, F32), packed(mom_m, GROUP_SMALL, F32),
                      packed(mom_v, GROUP_SMALL, F32), "sum_adamw_small")
    sizes = [weights[n].size for n in GROUP_SMALL]
    for kind, buf in enumerate(outs):
        for n, flat in zip(GROUP_SMALL, _unpack(buf, sizes, 16)):
            updated[kind, n] = flat.reshape(weights[n].shape)
    result = [loss, dx0.reshape(bsz, seq, d)]
    for kind in range(4):
        result.extend(updated[kind, n] for n in WEIGHT_NAMES)
    return tuple(result)
```

```python
import functools

import jax
import jax.numpy as jnp
from jax import lax
from jax.experimental import pallas as pl
from jax.experimental.pallas import tpu as pltpu

F32 = jnp.float32
BF16 = jnp.bfloat16

N_DEV = 8
HEAD_DIM = 64
ROPE_DIM = 16
ROPE_THETA = 500000.0
EPS = 1e-6
NEG_INF = -1e30
SPAN = 128
DILATIONS = (1, 4, 16)
CONV_WIDTH = 31
HALO = 32
CONV_ROWS = 32
CONV_W_ROWS = 64
PACK_LANES = 1024
V7X_LANES = 128
V7X_SUBLANES = 8
VMEM_LIMIT_BYTES = 56 * 1024 * 1024

ADAM_LR = 0.001
ADAM_B1 = 0.9
ADAM_B2 = 0.999
ADAM_EPS = 1e-08
ADAM_WD = 0.01
ADAM_STEP = 10
ADAM_ROWS = 256

WEIGHT_NAMES = ('norm_g', 'w_in_a', 'conv_w', 'conv_b', 'ln_g', 'ln_b', 'w_out_a', 'kv_norm_g', 'w_kv',
                'k_norm_g', 'w_in_b', 'q_norm_g', 'w_out_b', 'ple_norm_g', 'w_ple_gate', 'w_ple_proj')
SHARD_AXIS = {'norm_g': None, 'w_in_a': 2, 'conv_w': 2, 'conv_b': 1, 'ln_g': 1, 'ln_b': 1, 'w_out_a': 1,
              'kv_norm_g': None, 'w_kv': 1, 'k_norm_g': None, 'w_in_b': 2, 'q_norm_g': None, 'w_out_b': 1,
              'ple_norm_g': None, 'w_ple_gate': 1, 'w_ple_proj': 2}
VECTOR_WEIGHTS = ('conv_w', 'conv_b', 'ln_g', 'ln_b')
GROUP_FIRST = ('w_in_a',)
GROUP_REST = ('w_out_a', 'w_kv', 'w_in_b', 'w_out_b', 'w_ple_gate', 'w_ple_proj')
GROUP_SMALL = ('norm_g', 'conv_w', 'conv_b', 'ln_g', 'ln_b', 'kv_norm_g', 'k_norm_g', 'q_norm_g', 'ple_norm_g')


def _pick(n, target, mult):
    t = (min(target, n) // mult) * mult
    while t >= mult:
        if n % t == 0:
            return t
        t -= mult
    return n


def _params(n_grid):
    return pltpu.CompilerParams(dimension_semantics=("arbitrary",) * n_grid, vmem_limit_bytes=VMEM_LIMIT_BYTES)


def _sig(x):
    return 0.5 * jnp.tanh(0.5 * x) + 0.5


def _colsum8(v):
    r, w = v.shape
    return v.reshape(r // V7X_SUBLANES, V7X_SUBLANES, w).sum(axis=0)


def _rows(tm, w, col=0):
    return pl.BlockSpec((tm, w), lambda i: (i, col))


def _const(shape):
    nd = len(shape)
    return pl.BlockSpec(shape, lambda i: (0,) * nd)


def _segsum(v, e_ref):
    hi = v.astype(BF16)
    lo = (v - hi.astype(F32)).astype(BF16)
    e = e_ref[...]
    return jnp.dot(hi, e, preferred_element_type=F32) + jnp.dot(lo, e, preferred_element_type=F32)


MM_TILE = 1024
MM_TILE_K = 2048
MM_TILE_WIDE = 2048
MM_TILE_K_TN = 4096


def _matmul(a, b, mode, name, out_dtype=F32, add=None, ex=None, slot_cols=None, norm_gain=None):
    if mode == 'nn':
        (m, k), (_, n) = a.shape, b.shape
    elif mode == 'nt':
        (m, k), (n, _) = a.shape, b.shape
    else:
        (k, m), (_, n) = a.shape, b.shape
    out_struct = jax.ShapeDtypeStruct((m, n), out_dtype)
    n_slots = 0
    if mode == 'tn':
        tm, tn, tk = _pick(m, MM_TILE, 128), _pick(n, MM_TILE, 128), _pick(k, MM_TILE_K_TN, 128)
        o_spec = pl.BlockSpec((tm, tn), lambda i, j, kk: (i, j))
        if slot_cols is not None:
            assert n == N_DEV * slot_cols
            n_slots = max(s for s in (1, 2, 4, 8) if s == 1 or slot_cols * s <= MM_TILE)
            tn = slot_cols * n_slots
            o_spec = pl.BlockSpec((n_slots, tm, slot_cols), lambda i, j, kk: (j, i, 0))
            out_struct = jax.ShapeDtypeStruct((N_DEV, m, slot_cols), out_dtype)
        grid = (m // tm, n // tn, k // tk)
        a_spec = pl.BlockSpec((tk, tm), lambda i, j, kk: (kk, i))
        b_spec = pl.BlockSpec((tk, tn), lambda i, j, kk: (kk, j))
        dims = (((0,), (0,)), ((), ()))
    else:
        tn_max = MM_TILE_WIDE if k <= MM_TILE else MM_TILE
        tn = _pick(n, tn_max, 128)
        tm_max = MM_TILE_WIDE if (k <= MM_TILE and tn <= MM_TILE and add is None and norm_gain is None) else MM_TILE
        tm, tk = _pick(m, tm_max, 128), _pick(k, MM_TILE_K, 128)
        grid = (n // tn, m // tm, k // tk)
        a_spec = pl.BlockSpec((tm, tk), lambda j, i, kk: (i, kk))
        o_spec = pl.BlockSpec((tm, tn), lambda j, i, kk: (i, j))
        if mode == 'nn':
            b_spec = pl.BlockSpec((tk, tn), lambda j, i, kk: (kk, j))
            dims = (((1,), (0,)), ((), ()))
        else:
            b_spec = pl.BlockSpec((tn, tk), lambda j, i, kk: (j, kk))
            dims = (((1,), (1,)), ((), ()))
    nk = grid[2]
    has_add = add is not None
    has_norm = norm_gain is not None
    assert not has_norm or (tn == n and mode != 'tn')

    def body(*refs):
        a_ref, b_ref = refs[0], refs[1]
        add_ref = refs[2] if has_add else None
        gain_ref = refs[2 + has_add] if has_norm else None
        o_ref = refs[2 + has_add + has_norm]
        norm_ref = refs[3 + has_add + has_norm] if has_norm else None
        part = lax.dot_general(a_ref[...].astype(BF16), b_ref[...].astype(BF16), dims, preferred_element_type=F32)

        def finish(total):
            if has_add:
                total = total + add_ref[...]
            if n_slots:
                for s in range(n_slots):
                    o_ref[s] = total[:, s * slot_cols:(s + 1) * slot_cols].astype(out_dtype)
            else:
                o_ref[...] = total.astype(out_dtype)
            if has_norm:
                y = total * lax.rsqrt(jnp.mean(total * total, axis=-1, keepdims=True) + EPS)
                norm_ref[...] = (y * gain_ref[...]).astype(BF16)

        if nk == 1:
            finish(part)
        else:
            acc_ref = refs[3 + has_add + 2 * has_norm]
            kk = pl.program_id(2)

            @pl.when(kk == 0)
            def _():
                acc_ref[...] = part

            @pl.when(kk > 0)
            def _():
                acc_ref[...] += part

            @pl.when(kk == nk - 1)
            def _():
                finish(acc_ref[...])

    in_specs = [a_spec, b_spec] + ([o_spec] if has_add else [])
    args = [a, b] + ([add] if has_add else [])
    out_specs, out_structs = [o_spec], [out_struct]
    if has_norm:
        in_specs.append(pl.BlockSpec((1, n), lambda j, i, kk: (0, 0)))
        args.append(norm_gain)
        out_specs.append(o_spec)
        out_structs.append(jax.ShapeDtypeStruct((m, n), BF16))
    scratch = [pltpu.VMEM((tm, tn), F32)] if nk > 1 else []
    outs, moved = _hosted_call(body, ex, name, grid, in_specs, out_specs, out_structs, scratch, args)
    out = outs[0] if not has_norm else tuple(outs)
    return out if ex is None else (out, moved)


def _rmsnorm_fwd(x, gains, name):
    t, d = x.shape
    tm = _pick(t, 512, 8)
    n = len(gains)

    def body(*refs):
        x_ref, g_refs, o_refs = refs[0], refs[1:1 + n], refs[1 + n:]
        xv = x_ref[...]
        y = xv * lax.rsqrt(jnp.mean(xv * xv, axis=-1, keepdims=True) + EPS)
        for g_ref, o_ref in zip(g_refs, o_refs):
            o_ref[...] = (y * g_ref[...]).astype(BF16)

    return pl.pallas_call(
        body, name=name, grid=(t // tm,),
        in_specs=[_rows(tm, d)] + [_const((1, d))] * n,
        out_specs=[_rows(tm, d)] * n,
        out_shape=[jax.ShapeDtypeStruct((t, d), BF16)] * n,
        compiler_params=_params(1),
    )(x, *gains)


def _ple_grads(dx, gl, pp):
    sg = _sig(gl)
    return (dx * pp * sg * (1.0 - sg)).astype(BF16), (dx * sg).astype(BF16)


def _rmsnorm_bwd(x, gains, dys, add, name, ple=None, dx_dtype=BF16):
    t, d = x.shape
    tm = _pick(t, 512, 16)
    n = len(gains)
    n_ple = 0 if ple is None else 2

    def body(*refs):
        x_ref, add_ref = refs[0], refs[1]
        g_refs, dy_refs = refs[2:2 + n], refs[2 + n:2 + 2 * n]
        ple_refs = refs[2 + 2 * n:2 + 2 * n + n_ple]
        outs = refs[2 + 2 * n + n_ple:]
        dx_ref, dg_refs, dple_refs = outs[0], outs[1:1 + n], outs[1 + n:]
        i = pl.program_id(0)
        xv = x_ref[...]
        r = lax.rsqrt(jnp.mean(xv * xv, axis=-1, keepdims=True) + EPS)
        xhat = xv * r
        dx = add_ref[...].astype(F32)
        for g_ref, dy_ref, dg_ref in zip(g_refs, dy_refs, dg_refs):
            dy = dy_ref[...].astype(F32)
            dyg = dy * g_ref[...]
            dx = dx + r * (dyg - xhat * jnp.mean(dyg * xhat, axis=-1, keepdims=True))
            part = _colsum8(dy * xhat)

            @pl.when(i == 0)
            def _():
                dg_ref[...] = part

            @pl.when(i > 0)
            def _():
                dg_ref[...] += part

        dx_ref[...] = dx.astype(dx_dtype)
        if n_ple:
            dple_refs[0][...], dple_refs[1][...] = _ple_grads(dx, ple_refs[0][...].astype(F32),
                                                               ple_refs[1][...].astype(F32))

    outs = pl.pallas_call(
        body, name=name, grid=(t // tm,),
        in_specs=[_rows(tm, d), _rows(tm, d)] + [_const((1, d))] * n + [_rows(tm, d)] * (n + n_ple),
        out_specs=[_rows(tm, d)] + [_const((V7X_SUBLANES, d))] * n + [_rows(tm, d)] * n_ple,
        out_shape=([jax.ShapeDtypeStruct((t, d), dx_dtype)] + [jax.ShapeDtypeStruct((V7X_SUBLANES, d), F32)] * n
                   + [jax.ShapeDtypeStruct((t, d), BF16)] * n_ple),
        compiler_params=_params(1),
    )(x, add, *gains, *dys, *(ple or ()))
    dgs = [o.sum(axis=0) for o in outs[1:1 + n]]
    return (outs[0], dgs) if ple is None else (outs[0], dgs, outs[1 + n], outs[2 + n])


def _ple_norm_fwd(h, gl, pp, gains, name):
    t, d = h.shape
    tm = _pick(t, 512, 16)
    n = len(gains)

    def body(*refs):
        h_ref, gl_ref, pp_ref = refs[:3]
        g_refs, x_ref, o_refs = refs[3:3 + n], refs[3 + n], refs[4 + n:]
        xv = h_ref[...] + _sig(gl_ref[...].astype(F32)) * pp_ref[...].astype(F32)
        x_ref[...] = xv
        y = xv * lax.rsqrt(jnp.mean(xv * xv, axis=-1, keepdims=True) + EPS)
        for g_ref, o_ref in zip(g_refs, o_refs):
            o_ref[...] = (y * g_ref[...]).astype(BF16)

    outs = pl.pallas_call(
        body, name=name, grid=(t // tm,),
        in_specs=[_rows(tm, d)] * 3 + [_const((1, d))] * n, out_specs=[_rows(tm, d)] * (1 + n),
        out_shape=[jax.ShapeDtypeStruct((t, d), F32)] + [jax.ShapeDtypeStruct((t, d), BF16)] * n,
        compiler_params=_params(1),
    )(h, gl, pp, *gains)
    return outs[0], outs[1:]


def _ple_loss(h, gl, pp, target, name):
    t, d = h.shape
    tm = _pick(t, 512, 16)
    inv_d = 1.0 / d

    def body(h_ref, gl_ref, pp_ref, t_ref, dy_ref, dgl_ref, dpp_ref, l_ref):
        i = pl.program_id(0)
        gl, pp = gl_ref[...].astype(F32), pp_ref[...].astype(F32)
        e = h_ref[...] + _sig(gl) * pp - t_ref[...]
        dy = e * inv_d
        dy_ref[...] = dy.astype(BF16)
        dgl_ref[...], dpp_ref[...] = _ple_grads(dy, gl, pp)
        part = _colsum8(e * e) * (0.5 * inv_d)

        @pl.when(i == 0)
        def _():
            l_ref[...] = part

        @pl.when(i > 0)
        def _():
            l_ref[...] += part

    return pl.pallas_call(
        body, name=name, grid=(t // tm,), in_specs=[_rows(tm, d)] * 4,
        out_specs=[_rows(tm, d)] * 3 + [_const((V7X_SUBLANES, d))],
        out_shape=[jax.ShapeDtypeStruct((t, d), BF16), jax.ShapeDtypeStruct((t, d), BF16),
                   jax.ShapeDtypeStruct((t, d), BF16), jax.ShapeDtypeStruct((V7X_SUBLANES, d), F32)],
        compiler_params=_params(1),
    )(h, gl, pp, target)


def _shift_scratch(ts, cc):
    return pltpu.VMEM((V7X_SUBLANES, ts + HALO - V7X_SUBLANES, cc), F32)


def _shifted_copies(sh_ref, win_ref, cs, ts):
    rows = ts + HALO - V7X_SUBLANES
    for s in range(1, V7X_SUBLANES):
        sh_ref[s] = win_ref[pl.ds(s, rows), cs]


def _tap(sh_ref, win_ref, cs, offset, rows, r0):
    s = offset % V7X_SUBLANES
    start = pl.multiple_of(r0 + (offset - s), V7X_SUBLANES)
    if s == 0:
        return win_ref[pl.ds(start, rows), cs]
    return sh_ref[s, pl.ds(start, rows), :]


def _conv_fwd(proj, conv_w, conv_b, ln_g, ln_b, seq, name, ex=None):
    t, c3 = proj.shape
    c = c3 // 3
    ts = _pick(seq, 256, HALO)
    nsb = seq // ts
    cc = _pick(c, 512, V7X_LANES)
    hb = ts // HALO

    def body(a_ref, b_ref, z_ref, ap_ref, bp_ref, w_ref, cb_ref, g_ref, be_ref, m_ref, y_ref, win_ref, sh_ref):
        i = pl.program_id(0)
        first = (i % nsb) == 0
        win_ref[0:HALO, :] = jnp.where(first, 0.0, ap_ref[...] * _sig(bp_ref[...]))
        win_ref[HALO:, :] = a_ref[...] * _sig(b_ref[...])
        for ci in range(c // cc):
            cs = slice(ci * cc, (ci + 1) * cc)
            _shifted_copies(sh_ref, win_ref, cs, ts)

            def out_rows(rb, carry, cs=cs):
                r0 = rb * CONV_ROWS
                acc = jnp.zeros((CONV_ROWS, cc), F32) + cb_ref[:, cs]
                for k in range(CONV_WIDTH):
                    acc = acc + w_ref[k:k + 1, cs] * _tap(sh_ref, win_ref, cs, HALO - (CONV_WIDTH - 1) + k,
                                                           CONV_ROWS, r0)
                y_ref[pl.ds(pl.multiple_of(r0, CONV_ROWS), CONV_ROWS), cs] = acc
                return carry

            lax.fori_loop(0, ts // CONV_ROWS, out_rows, 0, unroll=2)
        y = y_ref[...]
        mu = jnp.mean(y, axis=-1, keepdims=True)
        xc = y - mu
        rstd = lax.rsqrt(jnp.mean(xc * xc, axis=-1, keepdims=True) + EPS)
        ln = xc * rstd * g_ref[...] + be_ref[...]
        zz = z_ref[...]
        m_ref[...] = (ln * _sig(ln) * zz * _sig(zz)).astype(BF16)

    halo_a = pl.BlockSpec((HALO, c), lambda i: (jnp.maximum(i * hb - 1, 0), 0))
    halo_b = pl.BlockSpec((HALO, c), lambda i: (jnp.maximum(i * hb - 1, 0), 1))
    (m_act, y), moved = _hosted_call(
        body, ex, name, (t // ts,),
        [_rows(ts, c, 0), _rows(ts, c, 1), _rows(ts, c, 2), halo_a, halo_b,
         _const((CONV_WIDTH, c)), _const((1, c)), _const((1, c)), _const((1, c))],
        [_rows(ts, c), _rows(ts, c)],
        [jax.ShapeDtypeStruct((t, c), BF16), jax.ShapeDtypeStruct((t, c), F32)],
        [pltpu.VMEM((HALO + ts, c), F32), _shift_scratch(ts, cc)],
        (proj, proj, proj, proj, proj, conv_w, conv_b, ln_g, ln_b))
    return m_act, y, moved


def _ln_gate_bwd(dm, y, proj, ln_g, ln_b, name):
    t, c = y.shape
    tm = _pick(t, 256, 8)

    def body(dm_ref, y_ref, z_ref, g_ref, be_ref, dy_ref, dz_ref, dg_ref, db_ref, dcb_ref):
        i = pl.program_id(0)
        yv = y_ref[...]
        mu = jnp.mean(yv, axis=-1, keepdims=True)
        xc = yv - mu
        rstd = lax.rsqrt(jnp.mean(xc * xc, axis=-1, keepdims=True) + EPS)
        xhat = xc * rstd
        g = g_ref[...]
        ln = xhat * g + be_ref[...]
        sl = _sig(ln)
        zz = z_ref[...]
        sz = _sig(zz)
        dmv = dm_ref[...].astype(F32)
        dz_ref[...] = (dmv * (ln * sl) * (sz * (1.0 + zz * (1.0 - sz)))).astype(BF16)
        dln = dmv * (zz * sz) * (sl * (1.0 + ln * (1.0 - sl)))
        dxh = dln * g
        dyv = rstd * (dxh - jnp.mean(dxh, axis=-1, keepdims=True)
                      - xhat * jnp.mean(dxh * xhat, axis=-1, keepdims=True))
        dy_ref[...] = dyv
        parts = (_colsum8(dln * xhat), _colsum8(dln), _colsum8(dyv))

        @pl.when(i == 0)
        def _():
            for ref, part in zip((dg_ref, db_ref, dcb_ref), parts):
                ref[...] = part

        @pl.when(i > 0)
        def _():
            for ref, part in zip((dg_ref, db_ref, dcb_ref), parts):
                ref[...] += part

    acc = jax.ShapeDtypeStruct((V7X_SUBLANES, c), F32)
    outs = pl.pallas_call(
        body, name=name, grid=(t // tm,),
        in_specs=[_rows(tm, c), _rows(tm, c), _rows(tm, c, 2), _const((1, c)), _const((1, c))],
        out_specs=[_rows(tm, c), _rows(tm, c)] + [_const((V7X_SUBLANES, c))] * 3,
        out_shape=[jax.ShapeDtypeStruct((t, c), F32), jax.ShapeDtypeStruct((t, c), BF16), acc, acc, acc],
        compiler_params=_params(1),
    )(dm, y, proj, ln_g, ln_b)
    return outs[0], outs[1], outs[2].sum(axis=0), outs[3].sum(axis=0), outs[4].sum(axis=0)


def _conv_bwd(dy, dz, proj, conv_w, seq, name, ex=None):
    t, c3 = proj.shape
    c = c3 // 3
    ts = _pick(seq, 256, HALO)
    nsb = seq // ts
    cc = _pick(c, 512, V7X_LANES)
    hb = ts // HALO
    last_halo = t // HALO - 1
    back = CONV_WIDTH - 1

    def body(dy_ref, dyn_ref, dz_ref, a_ref, b_ref, ap_ref, bp_ref, w_ref, o_ref, dw_ref, win_ref, dwin_ref,
             sh_ref, dsh_ref):
        i = pl.program_id(0)
        first = (i % nsb) == 0
        last = (i % nsb) == nsb - 1
        win_ref[0:HALO, :] = jnp.where(first, 0.0, ap_ref[...] * _sig(bp_ref[...]))
        win_ref[HALO:, :] = a_ref[...] * _sig(b_ref[...])
        dwin_ref[0:ts, :] = dy_ref[...]
        dwin_ref[ts:, :] = jnp.where(last, 0.0, dyn_ref[...])

        @pl.when(i == 0)
        def _():
            dw_ref[...] = jnp.zeros_like(dw_ref)

        for ci in range(c // cc):
            cs = slice(ci * cc, (ci + 1) * cc)
            _shifted_copies(sh_ref, win_ref, cs, ts)
            _shifted_copies(dsh_ref, dwin_ref, cs, ts)

            def in_grad_rows(rb, carry, cs=cs, ci=ci):
                r0 = rb * CONV_ROWS
                rows = pl.ds(pl.multiple_of(r0, CONV_ROWS), CONV_ROWS)
                dglu = jnp.zeros((CONV_ROWS, cc), F32)
                for k in range(CONV_WIDTH):
                    dglu = dglu + w_ref[k:k + 1, cs] * _tap(dsh_ref, dwin_ref, cs, back - k, CONV_ROWS, r0)
                sbc = _sig(b_ref[rows, cs])
                o_ref[rows, cs] = (dglu * sbc).astype(BF16)
                o_ref[rows, c + ci * cc:c + (ci + 1) * cc] = (dglu * a_ref[rows, cs] * sbc * (1.0 - sbc)).astype(BF16)
                return carry

            def w_grad_rows(rb, carry, cs=cs):
                r0 = rb * CONV_W_ROWS
                dcur = dwin_ref[pl.ds(pl.multiple_of(r0, CONV_W_ROWS), CONV_W_ROWS), cs]
                for k in range(CONV_WIDTH):
                    dw_ref[k * V7X_SUBLANES:(k + 1) * V7X_SUBLANES, cs] += _colsum8(
                        dcur * _tap(sh_ref, win_ref, cs, HALO - back + k, CONV_W_ROWS, r0))
                return carry

            lax.fori_loop(0, ts // CONV_ROWS, in_grad_rows, 0, unroll=2)
            lax.fori_loop(0, ts // CONV_W_ROWS, w_grad_rows, 0)
        o_ref[:, 2 * c:] = dz_ref[...]

    halo_next = pl.BlockSpec((HALO, c), lambda i: (jnp.minimum((i + 1) * hb, last_halo), 0))
    halo_a = pl.BlockSpec((HALO, c), lambda i: (jnp.maximum(i * hb - 1, 0), 0))
    halo_b = pl.BlockSpec((HALO, c), lambda i: (jnp.maximum(i * hb - 1, 0), 1))
    (dproj, dw), moved = _hosted_call(
        body, ex, name, (t // ts,),
        [_rows(ts, c), halo_next, _rows(ts, c), _rows(ts, c, 0), _rows(ts, c, 1), halo_a, halo_b,
         _const((CONV_WIDTH, c))],
        [_rows(ts, c3), _const((CONV_WIDTH * V7X_SUBLANES, c))],
        [jax.ShapeDtypeStruct((t, c3), BF16), jax.ShapeDtypeStruct((CONV_WIDTH * V7X_SUBLANES, c), F32)],
        [pltpu.VMEM((HALO + ts, c), F32), pltpu.VMEM((ts + HALO, c), F32),
         _shift_scratch(ts, cc), _shift_scratch(ts, cc)],
        (dy, dy, dz, proj, proj, proj, proj, conv_w))
    return dproj, dw.reshape(CONV_WIDTH, V7X_SUBLANES, c).sum(axis=1), moved


def _rope_tables(seq):
    half = ROPE_DIM // 2
    inv = ROPE_THETA ** (-jnp.arange(half, dtype=F32) * (2.0 / ROPE_DIM))
    ang = jnp.arange(seq).astype(F32)[:, None] * inv[None, :]
    cos, sin = jnp.cos(ang), jnp.sin(ang)
    zeros = jnp.zeros((seq, HEAD_DIM - ROPE_DIM), F32)
    zh = jnp.zeros((seq, half), F32)
    a = jnp.concatenate([cos, cos, zeros + 1.0], axis=1)
    b = jnp.concatenate([zh, sin, zeros], axis=1)
    c = jnp.concatenate([-sin, zh, zeros], axis=1)
    rep = V7X_LANES // HEAD_DIM
    return tuple(jnp.tile(v, (1, rep)) for v in (a, b, c))


def _head_ones(d):
    head = jnp.arange(d) // HEAD_DIM
    return (head[:, None] == head[None, :]).astype(BF16)


def _rope(ch, ta, tb, tc):
    return ta * ch + tb * pltpu.roll(ch, ROPE_DIM // 2, 1) + tc * pltpu.roll(ch, V7X_LANES - ROPE_DIM // 2, 1)


def _rope_t(ch, ta, tb, tc):
    return ta * ch + pltpu.roll(tb * ch, V7X_LANES - ROPE_DIM // 2, 1) + pltpu.roll(tc * ch, ROPE_DIM // 2, 1)


def _norm_rope_bwd(xhat, r, dout, gain, ta, tb, tc, e_ref):
    dxn = _rope_t(dout, ta, tb, tc)
    dxh = dxn * gain
    dx = r * (dxh - xhat * (_segsum(dxh * xhat, e_ref) * (1.0 / HEAD_DIM)))
    return dx, _colsum8(dxn * xhat)


def _norm_rope_rows(dst_ref, src_ref, gain, ta_ref, tb_ref, tc_ref, e_ref, seq, xhat_ref=None, r_ref=None):
    for r0 in range(0, seq, ATTN_PIECE):
        rows = slice(r0, r0 + ATTN_PIECE)
        xv = src_ref[rows, :]
        r = lax.rsqrt(_segsum(xv * xv, e_ref) * (1.0 / HEAD_DIM) + EPS)
        xhat = xv * r
        if xhat_ref is not None:
            xhat_ref[rows, :] = xhat
            r_ref[rows, :] = r
        dst_ref[rows, :] = _rope(xhat * gain, ta_ref[rows, :], tb_ref[rows, :], tc_ref[rows, :])


ATTN_PIECE = 256
ATTN_UNROLL = 16
CHUNK_UNROLL = 4


def _pieces(dil, seq):
    length = seq // dil
    rows = min(length, ATTN_PIECE)
    return [(r + dil * ci * rows, r * length + ci * rows, rows) for r in range(dil) for ci in range(length // rows)]


def _strided(ref, start, rows, dil):
    if dil == 1:
        return ref[pl.ds(start, rows), :]
    return ref[pl.ds(start, rows, stride=dil), :]


def _strided_set(ref, start, rows, dil, val):
    if dil == 1:
        ref[pl.ds(start, rows), :] = val
    else:
        ref[pl.ds(start, rows, stride=dil), :] = val


def _nt(a, b):
    return lax.dot_general(a, b, (((1,), (1,)), ((), ())), preferred_element_type=F32)


def _tn(a, b):
    return lax.dot_general(a, b, (((0,), (0,)), ((), ())), preferred_element_type=F32)


def _set_bias(bias_ref):
    qi = lax.broadcasted_iota(jnp.int32, (2 * SPAN, 2 * SPAN), 0) & (SPAN - 1)
    kj = lax.broadcasted_iota(jnp.int32, (2 * SPAN, 2 * SPAN), 1)
    band = jnp.logical_and(kj >= qi, (kj - SPAN) <= qi)
    bias_ref[1] = jnp.where(band, 0.0, NEG_INF)
    bias_ref[0] = jnp.where(jnp.logical_and(band, kj >= SPAN), 0.0, NEG_INF)


def _block_keys(bias_ref, j, qs, nb):
    if nb == 1:
        return pl.ds(pl.multiple_of(qs + SPAN, SPAN), SPAN), bias_ref[1, :, SPAN:]
    return pl.ds(qs, 2 * SPAN), bias_ref[jnp.minimum(j & (nb - 1), 1)]


def _stack_heads(v, head0):
    zero = jnp.zeros_like(v)
    return jnp.concatenate([jnp.where(head0, v, zero), jnp.where(head0, zero, v)], axis=0)


def _unstack_heads(v2, head0):
    return jnp.where(head0, v2[:SPAN], v2[SPAN:])


def _head_cols(v, lane=0):
    return jnp.concatenate([v[:, lane:lane + 1], v[:, HEAD_DIM + lane:HEAD_DIM + lane + 1]], axis=0)


def _attn_fwd(proj_b, kv, gains, tables, ones, bsz, seq, name):
    t, d4 = proj_b.shape
    d = d4 // 4
    nhp = d // V7X_LANES
    nblk = seq // SPAN
    scale = HEAD_DIM ** -0.5
    n_groups = len(DILATIONS)

    def body(q0_ref, q1_ref, q2_ref, k_ref, v_ref, gate_ref, gain_ref, ta_ref, tb_ref, tc_ref, e_ref,
             o_ref, l_ref, ao_ref, qd, kd, vd, od, ld, on0, on1, on2, ln0, ln1, ln2, kn, qn, bias):
        head0 = lax.broadcasted_iota(jnp.int32, (SPAN, V7X_LANES), 1) < HEAD_DIM

        @pl.when(jnp.logical_and(pl.program_id(0) == 0, pl.program_id(1) == 0))
        def _():
            _set_bias(bias)

        _norm_rope_rows(kn, k_ref, gain_ref[n_groups:n_groups + 1, :], ta_ref, tb_ref, tc_ref, e_ref, seq)
        kd[0:SPAN, :] = jnp.zeros((SPAN, V7X_LANES), BF16)
        vd[0:SPAN, :] = jnp.zeros((SPAN, V7X_LANES), BF16)
        for g, (q_ref, on, ln) in enumerate(((q0_ref, on0, ln0), (q1_ref, on1, ln1), (q2_ref, on2, ln2))):
            dil = DILATIONS[g]
            nb = seq // dil // SPAN
            _norm_rope_rows(qn, q_ref, gain_ref[g:g + 1, :], ta_ref, tb_ref, tc_ref, e_ref, seq)
            for ns, rs, rows in _pieces(dil, seq):
                qd[rs:rs + rows, :] = _strided(qn, ns, rows, dil).astype(BF16)
                kd[SPAN + rs:SPAN + rs + rows, :] = _strided(kn, ns, rows, dil).astype(BF16)
                vd[SPAN + rs:SPAN + rs + rows, :] = _strided(v_ref, ns, rows, dil).astype(BF16)

            def block(j, carry):
                qs = pl.multiple_of(j * SPAN, SPAN)
                q2 = _stack_heads(qd[pl.ds(qs, SPAN), :], head0)
                keys, mask = _block_keys(bias, j, qs, nb)
                kk = kd[keys, :]
                vv = vd[keys, :]
                s = _nt(q2, kk) * scale + mask
                mx = jnp.max(s, axis=1, keepdims=True)
                p = jnp.exp(s - mx)
                den = jnp.sum(p, axis=1, keepdims=True)
                o2 = jnp.dot(p.astype(BF16), vv, preferred_element_type=F32) / den
                l2 = jnp.broadcast_to(mx + jnp.log(den), (2 * SPAN, V7X_LANES))
                od[pl.ds(qs, SPAN), :] = _unstack_heads(o2, head0)
                ld[pl.ds(qs, SPAN), :] = _unstack_heads(l2, head0)
                return carry

            lax.fori_loop(0, nblk, block, 0, unroll=ATTN_UNROLL)
            for ns, rs, rows in _pieces(dil, seq):
                _strided_set(on, ns, rows, dil, od[rs:rs + rows, :])
                _strided_set(ln, ns, rows, dil, ld[rs:rs + rows, :])

        def merge(ci, carry):
            rows = pl.ds(pl.multiple_of(ci * ATTN_PIECE, ATTN_PIECE), ATTN_PIECE)
            ls = [ln0[rows, :], ln1[rows, :], ln2[rows, :]]
            mx = jnp.maximum(jnp.maximum(ls[0], ls[1]), ls[2])
            es = [jnp.exp(v - mx) for v in ls]
            den = es[0] + es[1] + es[2]
            ov = (es[0] * on0[rows, :] + es[1] * on1[rows, :] + es[2] * on2[rows, :]) / den
            gate = gate_ref[rows, :]
            o_ref[rows, :] = ov
            l_ref[rows, :] = mx + jnp.log(den)
            ao_ref[rows, :] = (ov * gate * _sig(gate)).astype(BF16)
            return carry

        lax.fori_loop(0, seq // ATTN_PIECE, merge, 0)

    blk = (None, seq, V7X_LANES)
    pview = proj_b.reshape(bsz, seq, d4)
    kview = kv.reshape(bsz, seq, 2 * d)
    out_spec = pl.BlockSpec(blk, lambda b, h: (b, 0, h))
    tab = pl.BlockSpec((seq, V7X_LANES), lambda b, h: (0, 0))
    nat = pltpu.VMEM((seq, V7X_LANES), F32)
    o, lse, ao = pl.pallas_call(
        body, name=name, grid=(bsz, nhp),
        in_specs=[pl.BlockSpec(blk, lambda b, h: (b, 0, h)),
                  pl.BlockSpec(blk, lambda b, h: (b, 0, nhp + h)),
                  pl.BlockSpec(blk, lambda b, h: (b, 0, 2 * nhp + h)),
                  pl.BlockSpec(blk, lambda b, h: (b, 0, h)),
                  pl.BlockSpec(blk, lambda b, h: (b, 0, nhp + h)),
                  pl.BlockSpec(blk, lambda b, h: (b, 0, 3 * nhp + h)),
                  pl.BlockSpec((n_groups + 1, V7X_LANES), lambda b, h: (0, 0)),
                  tab, tab, tab,
                  pl.BlockSpec((V7X_LANES, V7X_LANES), lambda b, h: (0, 0))],
        out_specs=[out_spec, out_spec, out_spec],
        out_shape=[jax.ShapeDtypeStruct((bsz, seq, d), F32), jax.ShapeDtypeStruct((bsz, seq, d), F32),
                   jax.ShapeDtypeStruct((bsz, seq, d), BF16)],
        scratch_shapes=[pltpu.VMEM((seq, V7X_LANES), BF16), pltpu.VMEM((SPAN + seq, V7X_LANES), BF16),
                        pltpu.VMEM((SPAN + seq, V7X_LANES), BF16), nat, nat, nat, nat, nat, nat, nat, nat, nat, nat,
                        pltpu.VMEM((2, 2 * SPAN, 2 * SPAN), F32)],
        compiler_params=_params(2),
    )(pview, pview, pview, kview, kview, pview, gains, *tables, ones)
    return o.reshape(t, d), lse.reshape(t, d), ao.reshape(t, d)


def _attn_bwd(proj_b, kv, dao, o, lse, gains, tables, ones, bsz, seq, name):
    t, d4 = proj_b.shape
    d = d4 // 4
    nhp = d // V7X_LANES
    nblk = seq // SPAN
    scale = HEAD_DIM ** -0.5
    n_groups = len(DILATIONS)
    n_chunks = seq // ATTN_PIECE

    def body(q_ref, k_ref, v_ref, gate_ref, dao_ref, o_ref, l_ref, gain_ref, ta_ref, tb_ref, tc_ref, e_ref,
             dproj_ref, dkv_ref, dg_ref, qd, kd, vd, dod, std, dqd, dkd, dvd, dqn, dk0, dk1, dk2, dv0, dv1, dv2,
             kn, kxh, krr, qn, qxh, qrr, don, stn, bias):
        head0 = lax.broadcasted_iota(jnp.int32, (SPAN, V7X_LANES), 1) < HEAD_DIM
        g = pl.program_id(2)

        @pl.when(jnp.logical_and(jnp.logical_and(pl.program_id(0) == 0, pl.program_id(1) == 0), g == 0))
        def _():
            _set_bias(bias)
            dg_ref[...] = jnp.zeros_like(dg_ref)

        @pl.when(g == 0)
        def _():
            first_half = (lax.broadcasted_iota(jnp.int32, (ATTN_PIECE, V7X_LANES), 1) & (HEAD_DIM - 1)) < HEAD_DIM // 2
            _norm_rope_rows(kn, k_ref, gain_ref[n_groups:n_groups + 1, :], ta_ref, tb_ref, tc_ref, e_ref, seq,
                            kxh, krr)
            for r0 in range(0, seq, ATTN_PIECE):
                rows = slice(r0, r0 + ATTN_PIECE)
                gate = gate_ref[rows, :]
                dov = dao_ref[rows, :].astype(F32) * gate * _sig(gate)
                don[rows, :] = dov
                stn[rows, :] = jnp.where(first_half, l_ref[rows, :], _segsum(dov * o_ref[rows, :], e_ref))

        def norm_bwd_chunks(xhat_ref, r_ref, dn_refs, out_ref, gi):
            def chunk(ci, carry):
                rows = pl.ds(pl.multiple_of(ci * ATTN_PIECE, ATTN_PIECE), ATTN_PIECE)
                dn = functools.reduce(lambda u, w: u + w, [r_[rows, :] for r_ in dn_refs])
                dx, part = _norm_rope_bwd(xhat_ref[rows, :], r_ref[rows, :], dn, gain_ref[gi:gi + 1, :],
                                          ta_ref[rows, :], tb_ref[rows, :], tc_ref[rows, :], e_ref)
                out_ref[rows, :] = dx.astype(BF16)
                dg_ref[gi] += part
                return carry
            lax.fori_loop(0, n_chunks, chunk, 0, unroll=CHUNK_UNROLL)

        def group(gi):
            dil = DILATIONS[gi]
            nb = seq // dil // SPAN
            kd[0:SPAN, :] = jnp.zeros((SPAN, V7X_LANES), BF16)
            vd[0:SPAN, :] = jnp.zeros((SPAN, V7X_LANES), BF16)
            dkd[...] = jnp.zeros_like(dkd)
            dvd[...] = jnp.zeros_like(dvd)
            _norm_rope_rows(qn, q_ref, gain_ref[gi:gi + 1, :], ta_ref, tb_ref, tc_ref, e_ref, seq, qxh, qrr)
            for ns, rs, rows in _pieces(dil, seq):
                qd[rs:rs + rows, :] = _strided(qn, ns, rows, dil).astype(BF16)
                kd[SPAN + rs:SPAN + rs + rows, :] = _strided(kn, ns, rows, dil).astype(BF16)
                vd[SPAN + rs:SPAN + rs + rows, :] = _strided(v_ref, ns, rows, dil).astype(BF16)
                dod[rs:rs + rows, :] = _strided(don, ns, rows, dil).astype(BF16)
                std[rs:rs + rows, :] = _strided(stn, ns, rows, dil)

            def block(j, carry):
                qs = pl.multiple_of(j * SPAN, SPAN)
                q2 = _stack_heads(qd[pl.ds(qs, SPAN), :], head0)
                do2 = _stack_heads(dod[pl.ds(qs, SPAN), :], head0)
                keys, mask = _block_keys(bias, j, qs, nb)
                kk = kd[keys, :]
                vv = vd[keys, :]
                s = _nt(q2, kk) * scale + mask
                stv = std[pl.ds(qs, SPAN), :]
                p = jnp.exp(s - _head_cols(stv))
                ds = (p * (_nt(do2, vv) - _head_cols(stv, HEAD_DIM // 2)) * scale).astype(BF16)
                dqd[pl.ds(qs, SPAN), :] = _unstack_heads(jnp.dot(ds, kk, preferred_element_type=F32), head0)
                dkd[keys, :] += _tn(ds, q2)
                dvd[keys, :] += _tn(p.astype(BF16), do2)
                return carry

            lax.fori_loop(0, nblk, block, 0, unroll=ATTN_UNROLL)
            for ns, rs, rows in _pieces(dil, seq):
                _strided_set(dqn, ns, rows, dil, dqd[rs:rs + rows, :])
                _strided_set((dk0, dk1, dk2)[gi], ns, rows, dil, dkd[SPAN + rs:SPAN + rs + rows, :])
                _strided_set((dv0, dv1, dv2)[gi], ns, rows, dil, dvd[SPAN + rs:SPAN + rs + rows, :])
            norm_bwd_chunks(qxh, qrr, [dqn], dproj_ref, gi)

        for gi in range(n_groups):
            @pl.when(g == gi)
            def _():
                group(gi)

        @pl.when(g == n_groups - 1)
        def _():
            norm_bwd_chunks(kxh, krr, [dk0, dk1, dk2], dkv_ref, n_groups)

        @pl.when(g == n_groups)
        def _():
            def chunk(ci, carry):
                rows = pl.ds(pl.multiple_of(ci * ATTN_PIECE, ATTN_PIECE), ATTN_PIECE)
                gate = gate_ref[rows, :]
                sg = _sig(gate)
                dproj_ref[rows, :] = (dao_ref[rows, :].astype(F32) * o_ref[rows, :]
                                      * (sg * (1.0 + gate * (1.0 - sg)))).astype(BF16)
                dkv_ref[rows, :] = (dv0[rows, :] + dv1[rows, :] + dv2[rows, :]).astype(BF16)
                return carry
            lax.fori_loop(0, n_chunks, chunk, 0, unroll=CHUNK_UNROLL)

    blk = (None, seq, V7X_LANES)
    pview = proj_b.reshape(bsz, seq, d4)
    kview = kv.reshape(bsz, seq, 2 * d)
    dview = (bsz, seq, d)
    d_spec = pl.BlockSpec(blk, lambda b, h, g: (b, 0, h))
    tab = pl.BlockSpec((seq, V7X_LANES), lambda b, h, g: (0, 0))
    nat = pltpu.VMEM((seq, V7X_LANES), F32)
    natb = pltpu.VMEM((seq, V7X_LANES), BF16)
    pad = pltpu.VMEM((SPAN + seq, V7X_LANES), F32)
    padb = pltpu.VMEM((SPAN + seq, V7X_LANES), BF16)
    dproj, dkv, dg = pl.pallas_call(
        body, name=name, grid=(bsz, nhp, n_groups + 1),
        in_specs=[pl.BlockSpec(blk, lambda b, h, g: (b, 0, jnp.minimum(g, n_groups - 1) * nhp + h)),
                  pl.BlockSpec(blk, lambda b, h, g: (b, 0, h)),
                  pl.BlockSpec(blk, lambda b, h, g: (b, 0, nhp + h)),
                  pl.BlockSpec(blk, lambda b, h, g: (b, 0, n_groups * nhp + h)),
                  d_spec, d_spec, d_spec,
                  pl.BlockSpec((n_groups + 1, V7X_LANES), lambda b, h, g: (0, 0)),
                  tab, tab, tab,
                  pl.BlockSpec((V7X_LANES, V7X_LANES), lambda b, h, g: (0, 0))],
        out_specs=[pl.BlockSpec(blk, lambda b, h, g: (b, 0, g * nhp + h)),
                   pl.BlockSpec(blk, lambda b, h, g: (b, 0, (g // n_groups) * nhp + h)),
                   pl.BlockSpec((n_groups + 1, V7X_SUBLANES, V7X_LANES), lambda b, h, g: (0, 0, 0))],
        out_shape=[jax.ShapeDtypeStruct((bsz, seq, d4), BF16), jax.ShapeDtypeStruct((bsz, seq, 2 * d), BF16),
                   jax.ShapeDtypeStruct((n_groups + 1, V7X_SUBLANES, V7X_LANES), F32)],
        scratch_shapes=[natb, padb, padb, natb, nat, nat, pad, pad] + [nat] * 15 + [
                        pltpu.VMEM((2, 2 * SPAN, 2 * SPAN), F32)],
        compiler_params=_params(3),
    )(pview, kview, kview, pview, dao.reshape(dview), o.reshape(dview), lse.reshape(dview), gains, *tables, ones)
    dgain = dg.sum(axis=1).reshape(n_groups + 1, V7X_LANES // HEAD_DIM, HEAD_DIM).sum(axis=1)
    return dproj.reshape(t, d4), dkv.reshape(t, 2 * d), dgain


def _mesh_position():
    x, y, c = lax.axis_index("x"), lax.axis_index("y"), lax.axis_index("c")
    return x, y, c


def _peer(x, y, c, rel):
    return (1 - x if rel & 4 else x, 1 - y if rel & 2 else y, 1 - c if rel & 1 else c)


class _Exchange:
    def __init__(self, srcs, gather):
        self.srcs = list(srcs)
        self.gather = gather
        n = self.n = len(self.srcs)
        hbm = pl.BlockSpec(memory_space=pltpu.HBM)
        self.in_specs = [hbm] * n
        self.out_specs = [hbm] * n
        self.out_shape = [jax.ShapeDtypeStruct(((N_DEV,) + a.shape) if gather else a.shape, a.dtype)
                          for a in self.srcs]
        self.scratch = [pltpu.SemaphoreType.DMA((n * (N_DEV - 1),)), pltpu.SemaphoreType.DMA((n * (N_DEV - 1),)),
                        pltpu.SemaphoreType.DMA((n,))]

    def _copies(self, ins, outs, sems):
        send_sems, recv_sems, local_sems = sems
        x, y, c = _mesh_position()
        me = 4 * x + 2 * y + c
        remote, local = [], []
        for a in range(self.n):
            mine = ins[a] if self.gather else ins[a].at[me]
            local.append(pltpu.make_async_copy(mine, outs[a].at[me], local_sems.at[a]))
            for rel in range(1, N_DEV):
                px, py, pc = _peer(x, y, c, rel)
                s = a * (N_DEV - 1) + rel - 1
                src = ins[a] if self.gather else ins[a].at[4 * px + 2 * py + pc]
                remote.append(pltpu.make_async_remote_copy(
                    src_ref=src, dst_ref=outs[a].at[me], send_sem=send_sems.at[s], recv_sem=recv_sems.at[s],
                    device_id=(px, py, pc), device_id_type=pl.DeviceIdType.MESH))
        return remote, local

    def start(self, ins, outs, sems):
        remote, local = self._copies(ins, outs, sems)
        for cp in local + remote:
            cp.start()

    def wait(self, ins, outs, sems):
        remote, local = self._copies(ins, outs, sems)
        for cp in remote:
            cp.wait_recv()
        for cp in remote:
            cp.wait_send()
        for cp in local:
            cp.wait()


def _gather_chip_once(arrs, name):
    n = len(arrs)
    per = N_DEV - 1

    def body(*refs):
        ins, outs = refs[:n], refs[n:2 * n]
        send_sems, recv_sems, local_sems = refs[2 * n:]
        x, y, c = _mesh_position()
        me, sibling = (x, y, c), (x, y, 1 - c)
        chips = [(1 - x, y), (x, 1 - y), (1 - x, 1 - y)]

        def copy(a, k, block, to, src=None):
            bx, by, bc = block
            dst = outs[a].at[4 * bx + 2 * by + bc]
            return pltpu.make_async_remote_copy(
                src_ref=dst if src is None else src, dst_ref=dst, send_sem=send_sems.at[a * per + k],
                recv_sem=recv_sems.at[a * per + k], device_id=to, device_id_type=pl.DeviceIdType.MESH)

        local, sent = [], []
        for a in range(n):
            mine = pltpu.make_async_copy(ins[a], outs[a].at[4 * x + 2 * y + c], local_sems.at[a])
            mine.start()
            local.append(mine)
            first = [copy(a, 0, me, sibling, src=ins[a])]
            first += [copy(a, 1 + j, me, chip + (c,), src=ins[a]) for j, chip in enumerate(chips)]
            for cp in first:
                cp.start()
            sent += first
        for a in range(n):
            for j, chip in enumerate(chips):
                copy(a, 1 + j, chip + (c,), me).wait_recv()
                passed = copy(a, 4 + j, chip + (c,), sibling)
                passed.start()
                sent.append(passed)
        for a in range(n):
            copy(a, 0, sibling, me).wait_recv()
            for j, chip in enumerate(chips):
                copy(a, 4 + j, chip + (1 - c,), me).wait_recv()
        for cp in sent:
            cp.wait_send()
        for cp in local:
            cp.wait()

    hbm = pl.BlockSpec(memory_space=pltpu.HBM)
    return pl.pallas_call(
        body, name=name, in_specs=[hbm] * n, out_specs=[hbm] * n,
        out_shape=[jax.ShapeDtypeStruct((N_DEV,) + a.shape, a.dtype) for a in arrs],
        scratch_shapes=[pltpu.SemaphoreType.DMA((n * per,)), pltpu.SemaphoreType.DMA((n * per,)),
                        pltpu.SemaphoreType.DMA((n,))],
    )(*arrs)


def _run_exchange(ex, name):
    n = ex.n

    def body(*refs):
        ins, outs, sems = refs[:n], refs[n:2 * n], refs[2 * n:]
        ex.start(ins, outs, sems)
        ex.wait(ins, outs, sems)

    return pl.pallas_call(body, name=name, in_specs=ex.in_specs, out_specs=ex.out_specs, out_shape=ex.out_shape,
                          scratch_shapes=ex.scratch)(*ex.srcs)


def _hosted_call(body, ex, name, grid, in_specs, out_specs, out_shape, scratch_shapes, args):
    if ex is None:
        outs = pl.pallas_call(body, name=name, grid=grid, in_specs=in_specs, out_specs=out_specs, out_shape=out_shape,
                              scratch_shapes=scratch_shapes, compiler_params=_params(len(grid)))(*args)
        return list(outs), []
    n_in, n_out, n_scr, n = len(in_specs), len(out_specs), len(scratch_shapes), ex.n

    def hosted(*refs):
        h_in, e_in = refs[:n_in], refs[n_in:n_in + n]
        o0 = n_in + n
        h_out, e_out = refs[o0:o0 + n_out], refs[o0 + n_out:o0 + n_out + n]
        s0 = o0 + n_out + n
        h_scr, e_scr = refs[s0:s0 + n_scr], refs[s0 + n_scr:]
        ids = [pl.program_id(a) for a in range(len(grid))]
        first = functools.reduce(jnp.logical_and, [i == 0 for i in ids])
        last = functools.reduce(jnp.logical_and, [i == g - 1 for i, g in zip(ids, grid)])

        @pl.when(first)
        def _():
            ex.start(e_in, e_out, e_scr)

        body(*h_in, *h_out, *h_scr)

        @pl.when(last)
        def _():
            ex.wait(e_in, e_out, e_scr)

    outs = pl.pallas_call(
        hosted, name=name, grid=grid, in_specs=list(in_specs) + ex.in_specs,
        out_specs=list(out_specs) + ex.out_specs, out_shape=list(out_shape) + ex.out_shape,
        scratch_shapes=list(scratch_shapes) + ex.scratch, compiler_params=_params(len(grid)),
    )(*args, *ex.srcs)
    return list(outs[:n_out]), list(outs[n_out:])


def _sum_adamw(parts, w, m, v, name):
    _, r, wd = parts.shape
    tr = _pick(r, ADAM_ROWS, 8)
    c1 = 1.0 - ADAM_B1 ** ADAM_STEP
    c2 = 1.0 - ADAM_B2 ** ADAM_STEP

    def body(p_ref, w_ref, m_ref, v_ref, g_ref, d_ref, nm_ref, nv_ref):
        g = p_ref[0].astype(F32)
        for s in range(1, N_DEV):
            g = g + p_ref[s].astype(F32)
        nm = ADAM_B1 * m_ref[...] + (1.0 - ADAM_B1) * g
        nv = ADAM_B2 * v_ref[...] + (1.0 - ADAM_B2) * (g * g)
        g_ref[...] = g
        nm_ref[...] = nm
        nv_ref[...] = nv
        d_ref[...] = -ADAM_LR * ((nm / c1) / (jnp.sqrt(nv / c2) + ADAM_EPS) + ADAM_WD * w_ref[...])

    row = pl.BlockSpec((tr, wd), lambda i: (i, 0))
    return pl.pallas_call(
        body, name=name, grid=(r // tr,),
        in_specs=[pl.BlockSpec((N_DEV, tr, wd), lambda i: (0, i, 0)), row, row, row],
        out_specs=[row] * 4, out_shape=[jax.ShapeDtypeStruct((r, wd), F32)] * 4,
        compiler_params=_params(1),
    )(parts, w, m, v)


def _pack_rows(size, row_mult):
    rows = -(-size // PACK_LANES)
    return -(-rows // row_mult) * row_mult


def _pack(flats, row_mult, dtype, total_mult=None):
    out = []
    for f in flats:
        size = f.shape[-1]
        rows = _pack_rows(size, row_mult)
        pad = [(0, 0)] * (f.ndim - 1) + [(0, rows * PACK_LANES - size)]
        out.append(jnp.pad(f.astype(dtype), pad).reshape(f.shape[:-1] + (rows, PACK_LANES)))
    if total_mult is not None:
        total = sum(o.shape[-2] for o in out)
        extra = -(-total // total_mult) * total_mult - total
        if extra:
            out.append(jnp.zeros(out[0].shape[:-2] + (extra, PACK_LANES), dtype))
    return jnp.concatenate(out, axis=-2)


def _unpack(buf, sizes, row_mult):
    out, row = [], 0
    for size in sizes:
        rows = _pack_rows(size, row_mult)
        part = buf[..., row:row + rows, :]
        out.append(part.reshape(buf.shape[:-2] + (rows * PACK_LANES,))[..., :size])
        row += rows
    return out


def _to_slots(full, axis):
    if axis is None:
        return jnp.broadcast_to(full.reshape(1, -1), (N_DEV, full.size))
    shape = full.shape
    split = full.reshape(shape[:axis] + (N_DEV, shape[axis] // N_DEV) + shape[axis + 1:])
    return jnp.moveaxis(split, axis, 0).reshape(N_DEV, -1)


def _from_slots(slots, axis, block_shape):
    split = jnp.moveaxis(slots, 0, axis)
    shape = list(block_shape)
    shape[axis] *= N_DEV
    return split.reshape(shape)


def kernel(x, p, norm_g, w_in_a, conv_w, conv_b, ln_g, ln_b, w_out_a, kv_norm_g, w_kv, k_norm_g, w_in_b, q_norm_g, w_out_b, ple_norm_g, w_ple_gate, w_ple_proj, loss_target, m_norm_g, m_w_in_a, m_conv_w, m_conv_b, m_ln_g, m_ln_b, m_w_out_a, m_kv_norm_g, m_w_kv, m_k_norm_g, m_w_in_b, m_q_norm_g, m_w_out_b, m_ple_norm_g, m_w_ple_gate, m_w_ple_proj, v_norm_g, v_w_in_a, v_conv_w, v_conv_b, v_ln_g, v_ln_b, v_w_out_a, v_kv_norm_g, v_w_kv, v_k_norm_g, v_w_in_b, v_q_norm_g, v_w_out_b, v_ple_norm_g, v_w_ple_gate, v_w_ple_proj):
    weights = dict(zip(WEIGHT_NAMES, (norm_g, w_in_a, conv_w, conv_b, ln_g, ln_b, w_out_a, kv_norm_g, w_kv, k_norm_g,
                                      w_in_b, q_norm_g, w_out_b, ple_norm_g, w_ple_gate, w_ple_proj)))
    mom_m = dict(zip(WEIGHT_NAMES, (m_norm_g, m_w_in_a, m_conv_w, m_conv_b, m_ln_g, m_ln_b, m_w_out_a, m_kv_norm_g,
                                    m_w_kv, m_k_norm_g, m_w_in_b, m_q_norm_g, m_w_out_b, m_ple_norm_g, m_w_ple_gate,
                                    m_w_ple_proj)))
    mom_v = dict(zip(WEIGHT_NAMES, (v_norm_g, v_w_in_a, v_conv_w, v_conv_b, v_ln_g, v_ln_b, v_w_out_a, v_kv_norm_g,
                                    v_w_kv, v_k_norm_g, v_w_in_b, v_q_norm_g, v_w_out_b, v_ple_norm_g, v_w_ple_gate,
                                    v_w_ple_proj)))
    bsz, seq, d = x.shape
    t = bsz * seq
    assert seq % (max(DILATIONS) * SPAN) == 0 and d % V7X_LANES == 0

    full = {}

    def rows2d(a):
        return a.reshape(-1, a.shape[-1])

    def packed(source, names, dtype, total_mult=None):
        return _pack([source[n].reshape(-1) for n in names], 16, dtype, total_mult)

    def gathered(names, bufs):
        for n, buf in zip(names, bufs):
            full[n] = _from_slots(buf.reshape((N_DEV,) + weights[n].shape), SHARD_AXIS[n], weights[n].shape)

    w1_all, wv_all = _gather_chip_once([rows2d(weights['w_in_a']).astype(BF16),
                                        _pack([weights[n].reshape(-1) for n in VECTOR_WEIGHTS], 8, F32)],
                                       "gather_first")
    gathered(GROUP_FIRST, [w1_all])
    for n, slots in zip(VECTOR_WEIGHTS, _unpack(wv_all, [weights[n].size for n in VECTOR_WEIGHTS], 8)):
        full[n] = _from_slots(slots.reshape((N_DEV,) + weights[n].shape), SHARD_AXIS[n], weights[n].shape)
    gather_rest = _Exchange([rows2d(weights[n]).astype(BF16) for n in GROUP_REST], gather=True)
    wa_in = full['w_in_a'][0]
    cw, cb, lg, lb = full['conv_w'][0], full['conv_b'], full['ln_g'], full['ln_b']

    tables = _rope_tables(seq)
    ones = _head_ones(V7X_LANES)
    rep = V7X_LANES // HEAD_DIM
    head_gain = jnp.concatenate([jnp.tile(q_norm_g[0], (1, rep)), jnp.tile(k_norm_g, rep)[None]], axis=0)

    x0 = x.reshape(t, d)
    p0, p1 = p[0].reshape(t, -1), p[1].reshape(t, -1)
    target = loss_target.reshape(t, d)
    g_norm0, g_norm1 = norm_g[0:1], norm_g[1:2]
    g_ple0, g_ple1 = ple_norm_g[0:1], ple_norm_g[1:2]
    g_kv = kv_norm_g.reshape(1, d)

    (u0,) = _rmsnorm_fwd(x0, [g_norm0], "norm0")
    proj_a = _matmul(u0, wa_in, 'nn', "in_a")
    m_act, y_conv, w2_all = _conv_fwd(proj_a, cw, cb, lg, lb, seq, "conv_fwd", ex=gather_rest)
    gathered(GROUP_REST, w2_all)
    wa_out = full['w_out_a'][0]
    wkv = full['w_kv']
    wb_in, wb_out = full['w_in_b'][0], full['w_out_b'][0]
    wg, wp = full['w_ple_gate'], full['w_ple_proj']
    h0, pg0 = _matmul(m_act, wa_out, 'nn', "out_a", add=x0, norm_gain=g_ple0)
    gl0 = _matmul(pg0, wg[0], 'nn', "ple_gate0", out_dtype=BF16)
    pp0 = _matmul(p0, wp[0], 'nn', "ple_proj0", out_dtype=BF16)

    x1, (kvn, u1) = _ple_norm_fwd(h0, gl0, pp0, [g_kv, g_norm1], "ple0_norm1")
    kv = _matmul(kvn, wkv, 'nn', "kv")
    proj_b = _matmul(u1, wb_in, 'nn', "in_b")
    o_att, lse, ao = _attn_fwd(proj_b, kv, head_gain, tables, ones, bsz, seq, "attn_fwd")
    h1, pg1 = _matmul(ao, wb_out, 'nn', "out_b", add=x1, norm_gain=g_ple1)
    gl1 = _matmul(pg1, wg[1], 'nn', "ple_gate1", out_dtype=BF16)
    pp1 = _matmul(p1, wp[1], 'nn', "ple_proj1", out_dtype=BF16)

    dx2, dgl1, dpp1, loss_part = _ple_loss(h1, gl1, pp1, target, "ple1_loss")
    loss = lax.psum(jnp.sum(loss_part), ("x", "y", "c"))

    grads = {}
    slot = {}

    dwp1 = _matmul(p1, dpp1, 'tn', "d_ple_proj1", out_dtype=BF16, slot_cols=d // N_DEV)
    dwg1 = _matmul(pg1, dgl1, 'tn', "d_ple_gate1", out_dtype=BF16)
    dpg1 = _matmul(dgl1, wg[1], 'nt', "d_ple_norm1", out_dtype=BF16)
    dh1, (dg_ple1,) = _rmsnorm_bwd(h1, [g_ple1], [dpg1], dx2, "ple_norm1_bwd")
    slot['w_out_b'] = _matmul(ao, dh1, 'tn', "d_out_b", out_dtype=BF16).reshape(N_DEV, -1, d)
    dao = _matmul(dh1, wb_out, 'nt', "d_ao", out_dtype=BF16)
    dproj_b, dkv, dg_head = _attn_bwd(proj_b, kv, dao, o_att, lse, head_gain, tables, ones, bsz, seq, "attn_bwd")
    slot['w_in_b'] = _matmul(u1, dproj_b, 'tn', "d_in_b", out_dtype=BF16, slot_cols=4 * d // N_DEV)
    du1 = _matmul(dproj_b, wb_in, 'nt', "d_u1", out_dtype=BF16)
    slot['w_kv'] = _matmul(kvn, dkv, 'tn', "d_kv", out_dtype=BF16, slot_cols=2 * d // N_DEV)
    dkvn = _matmul(dkv, wkv, 'nt', "d_kvn", out_dtype=BF16)
    dx1, (dg_kv, dg_norm1), dgl0, dpp0 = _rmsnorm_bwd(x1, [g_kv, g_norm1], [dkvn, du1], dh1, "norm1_bwd",
                                                      ple=(gl0, pp0))

    dwp0 = _matmul(p0, dpp0, 'tn', "d_ple_proj0", out_dtype=BF16, slot_cols=d // N_DEV)
    dwg0 = _matmul(pg0, dgl0, 'tn', "d_ple_gate0", out_dtype=BF16)
    dpg0 = _matmul(dgl0, wg[0], 'nt', "d_ple_norm0", out_dtype=BF16)
    dh0, (dg_ple0,) = _rmsnorm_bwd(h0, [g_ple0], [dpg0], dx1, "ple_norm0_bwd")
    slot['w_out_a'] = _matmul(m_act, dh0, 'tn', "d_out_a", out_dtype=BF16).reshape(N_DEV, -1, d)
    dm = _matmul(dh0, wa_out, 'nt', "d_m", out_dtype=BF16)
    dy_conv, dz, d_lg, d_lb, d_cb = _ln_gate_bwd(dm, y_conv, proj_a, lg, lb, "ln_gate_bwd")
    slot['w_ple_gate'] = jnp.stack([dwg0.reshape(N_DEV, -1, d), dwg1.reshape(N_DEV, -1, d)],
                                   axis=1).reshape(N_DEV, -1, d)
    slot['w_ple_proj'] = jnp.stack([dwp0, dwp1], axis=1).reshape(N_DEV, -1, d // N_DEV)

    dproj_a, d_cw, parts_rest = _conv_bwd(dy_conv, dz, proj_a, cw, seq, "conv_bwd",
                                          ex=_Exchange([slot[n] for n in GROUP_REST], gather=False))
    slot['w_in_a'] = _matmul(u0, dproj_a, 'tn', "d_in_a", out_dtype=BF16, slot_cols=wa_in.shape[1] // N_DEV)
    du0, parts_first = _matmul(dproj_a, wa_in, 'nt', "d_u0", out_dtype=BF16,
                               ex=_Exchange([slot['w_in_a']], gather=False))
    dx0, (dg_norm0,) = _rmsnorm_bwd(x0, [g_norm0], [du0], dh0, "norm0_bwd", dx_dtype=F32)

    grads['norm_g'] = jnp.stack([dg_norm0, dg_norm1])
    grads['conv_w'] = d_cw[None]
    grads['conv_b'] = d_cb[None]
    grads['ln_g'] = d_lg[None]
    grads['ln_b'] = d_lb[None]
    grads['kv_norm_g'] = dg_kv
    grads['k_norm_g'] = dg_head[3]
    grads['q_norm_g'] = dg_head[0:3][None]
    grads['ple_norm_g'] = jnp.stack([dg_ple0, dg_ple1])
    small_pack = _pack([_to_slots(grads[n], SHARD_AXIS[n]) for n in GROUP_SMALL], 16, BF16)
    (parts_small,) = _run_exchange(_Exchange([small_pack], gather=False), "exchange_small")

    updated = {}
    for n, parts in zip(GROUP_REST + GROUP_FIRST, parts_rest + parts_first):
        outs = _sum_adamw(parts, rows2d(weights[n]), rows2d(mom_m[n]), rows2d(mom_v[n]), "sum_adamw_" + n)
        for kind, buf in enumerate(outs):
            updated[kind, n] = buf.reshape(weights[n].shape)
    outs = _sum_adamw(parts_small, packed(weights, GROUP_SMALL, F32), packed(mom_m, GROUP_SMALL, F32),
                      packed(mom_v, GROUP_SMALL, F32), "sum_adamw_small")
    sizes = [weights[n].size for n in GROUP_SMALL]
    for kind, buf in enumerate(outs):
        for n, flat in zip(GROUP_SMALL, _unpack(buf, sizes, 16)):
            updated[kind, n] = flat.reshape(weights[n].shape)
    result = [loss, dx0.reshape(bsz, seq, d)]
    for kind in range(4):
        result.extend(updated[kind, n] for n in WEIGHT_NAMES)
    return tuple(result)
```

```python
import functools

import jax
import jax.numpy as jnp
from jax import lax
from jax.experimental import pallas as pl
from jax.experimental.pallas import tpu as pltpu

F32 = jnp.float32
BF16 = jnp.bfloat16

N_DEV = 8
HEAD_DIM = 64
ROPE_DIM = 16
ROPE_THETA = 500000.0
EPS = 1e-6
NEG_INF = -1e30
SPAN = 128
DILATIONS = (1, 4, 16)
CONV_WIDTH = 31
HALO = 32
CONV_ROWS = 32
CONV_W_ROWS = 64
PACK_LANES = 1024
V7X_LANES = 128
V7X_SUBLANES = 8
VMEM_LIMIT_BYTES = 56 * 1024 * 1024

ADAM_LR = 0.001
ADAM_B1 = 0.9
ADAM_B2 = 0.999
ADAM_EPS = 1e-08
ADAM_WD = 0.01
ADAM_STEP = 10
ADAM_ROWS = 256

WEIGHT_NAMES = ('norm_g', 'w_in_a', 'conv_w', 'conv_b', 'ln_g', 'ln_b', 'w_out_a', 'kv_norm_g', 'w_kv',
                'k_norm_g', 'w_in_b', 'q_norm_g', 'w_out_b', 'ple_norm_g', 'w_ple_gate', 'w_ple_proj')
SHARD_AXIS = {'norm_g': None, 'w_in_a': 2, 'conv_w': 2, 'conv_b': 1, 'ln_g': 1, 'ln_b': 1, 'w_out_a': 1,
              'kv_norm_g': None, 'w_kv': 1, 'k_norm_g': None, 'w_in_b': 2, 'q_norm_g': None, 'w_out_b': 1,
              'ple_norm_g': None, 'w_ple_gate': 1, 'w_ple_proj': 2}
VECTOR_WEIGHTS = ('conv_w', 'conv_b', 'ln_g', 'ln_b')
GROUP_FIRST = ('w_in_a',)
GROUP_REST = ('w_out_a', 'w_kv', 'w_in_b', 'w_out_b', 'w_ple_gate', 'w_ple_proj')
GROUP_SMALL = ('norm_g', 'conv_w', 'conv_b', 'ln_g', 'ln_b', 'kv_norm_g', 'k_norm_g', 'q_norm_g', 'ple_norm_g')


def _pick(n, target, mult):
    t = (min(target, n) // mult) * mult
    while t >= mult:
        if n % t == 0:
            return t
        t -= mult
    return n


def _params(n_grid):
    return pltpu.CompilerParams(dimension_semantics=("arbitrary",) * n_grid, vmem_limit_bytes=VMEM_LIMIT_BYTES)


def _sig(x):
    return 0.5 * jnp.tanh(0.5 * x) + 0.5


def _colsum8(v):
    r, w = v.shape
    return v.reshape(r // V7X_SUBLANES, V7X_SUBLANES, w).sum(axis=0)


def _rows(tm, w, col=0):
    return pl.BlockSpec((tm, w), lambda i: (i, col))


def _const(shape):
    nd = len(shape)
    return pl.BlockSpec(shape, lambda i: (0,) * nd)


def _segsum(v, e_ref):
    hi = v.astype(BF16)
    lo = (v - hi.astype(F32)).astype(BF16)
    e = e_ref[...]
    return jnp.dot(hi, e, preferred_element_type=F32) + jnp.dot(lo, e, preferred_element_type=F32)


MM_TILE = 1024
MM_TILE_K = 2048
MM_TILE_WIDE = 2048
MM_TILE_K_TN = 4096


def _matmul(a, b, mode, name, out_dtype=F32, add=None, ex=None, slot_cols=None, norm_gain=None):
    if mode == 'nn':
        (m, k), (_, n) = a.shape, b.shape
    elif mode == 'nt':
        (m, k), (n, _) = a.shape, b.shape
    else:
        (k, m), (_, n) = a.shape, b.shape
    out_struct = jax.ShapeDtypeStruct((m, n), out_dtype)
    n_slots = 0
    if mode == 'tn':
        tm, tn, tk = _pick(m, MM_TILE, 128), _pick(n, MM_TILE, 128), _pick(k, MM_TILE_K_TN, 128)
        o_spec = pl.BlockSpec((tm, tn), lambda i, j, kk: (i, j))
        if slot_cols is not None:
            assert n == N_DEV * slot_cols
            n_slots = max(s for s in (1, 2, 4, 8) if s == 1 or slot_cols * s <= MM_TILE)
            tn = slot_cols * n_slots
            o_spec = pl.BlockSpec((n_slots, tm, slot_cols), lambda i, j, kk: (j, i, 0))
            out_struct = jax.ShapeDtypeStruct((N_DEV, m, slot_cols), out_dtype)
        grid = (m // tm, n // tn, k // tk)
        a_spec = pl.BlockSpec((tk, tm), lambda i, j, kk: (kk, i))
        b_spec = pl.BlockSpec((tk, tn), lambda i, j, kk: (kk, j))
        dims = (((0,), (0,)), ((), ()))
    else:
        tn_max = MM_TILE_WIDE if k <= MM_TILE else MM_TILE
        tn = _pick(n, tn_max, 128)
        tm_max = MM_TILE_WIDE if (tn <= MM_TILE and add is None and norm_gain is None) else MM_TILE
        tm, tk = _pick(m, tm_max, 128), _pick(k, MM_TILE_K, 128)
        grid = (n // tn, m // tm, k // tk)
        a_spec = pl.BlockSpec((tm, tk), lambda j, i, kk: (i, kk))
        o_spec = pl.BlockSpec((tm, tn), lambda j, i, kk: (i, j))
        if mode == 'nn':
            b_spec = pl.BlockSpec((tk, tn), lambda j, i, kk: (kk, j))
            dims = (((1,), (0,)), ((), ()))
        else:
            b_spec = pl.BlockSpec((tn, tk), lambda j, i, kk: (j, kk))
            dims = (((1,), (1,)), ((), ()))
    nk = grid[2]
    has_add = add is not None
    has_norm = norm_gain is not None
    assert not has_norm or (tn == n and mode != 'tn')

    def body(*refs):
        a_ref, b_ref = refs[0], refs[1]
        add_ref = refs[2] if has_add else None
        gain_ref = refs[2 + has_add] if has_norm else None
        o_ref = refs[2 + has_add + has_norm]
        norm_ref = refs[3 + has_add + has_norm] if has_norm else None
        part = lax.dot_general(a_ref[...].astype(BF16), b_ref[...].astype(BF16), dims, preferred_element_type=F32)

        def finish(total):
            if has_add:
                total = total + add_ref[...]
            if n_slots:
                for s in range(n_slots):
                    o_ref[s] = total[:, s * slot_cols:(s + 1) * slot_cols].astype(out_dtype)
            else:
                o_ref[...] = total.astype(out_dtype)
            if has_norm:
                y = total * lax.rsqrt(jnp.mean(total * total, axis=-1, keepdims=True) + EPS)
                norm_ref[...] = (y * gain_ref[...]).astype(BF16)

        if nk == 1:
            finish(part)
        else:
            acc_ref = refs[3 + has_add + 2 * has_norm]
            kk = pl.program_id(2)

            @pl.when(kk == 0)
            def _():
                acc_ref[...] = part

            @pl.when(kk > 0)
            def _():
                acc_ref[...] += part

            @pl.when(kk == nk - 1)
            def _():
                finish(acc_ref[...])

    in_specs = [a_spec, b_spec] + ([o_spec] if has_add else [])
    args = [a, b] + ([add] if has_add else [])
    out_specs, out_structs = [o_spec], [out_struct]
    if has_norm:
        in_specs.append(pl.BlockSpec((1, n), lambda j, i, kk: (0, 0)))
        args.append(norm_gain)
        out_specs.append(o_spec)
        out_structs.append(jax.ShapeDtypeStruct((m, n), BF16))
    scratch = [pltpu.VMEM((tm, tn), F32)] if nk > 1 else []
    outs, moved = _hosted_call(body, ex, name, grid, in_specs, out_specs, out_structs, scratch, args)
    out = outs[0] if not has_norm else tuple(outs)
    return out if ex is None else (out, moved)


def _rmsnorm_fwd(x, gains, name):
    t, d = x.shape
    tm = _pick(t, 512, 8)
    n = len(gains)

    def body(*refs):
        x_ref, g_refs, o_refs = refs[0], refs[1:1 + n], refs[1 + n:]
        xv = x_ref[...]
        y = xv * lax.rsqrt(jnp.mean(xv * xv, axis=-1, keepdims=True) + EPS)
        for g_ref, o_ref in zip(g_refs, o_refs):
            o_ref[...] = (y * g_ref[...]).astype(BF16)

    return pl.pallas_call(
        body, name=name, grid=(t // tm,),
        in_specs=[_rows(tm, d)] + [_const((1, d))] * n,
        out_specs=[_rows(tm, d)] * n,
        out_shape=[jax.ShapeDtypeStruct((t, d), BF16)] * n,
        compiler_params=_params(1),
    )(x, *gains)


def _ple_grads(dx, gl, pp):
    sg = _sig(gl)
    return (dx * pp * sg * (1.0 - sg)).astype(BF16), (dx * sg).astype(BF16)


def _rmsnorm_bwd(x, gains, dys, add, name, ple=None, dx_dtype=BF16):
    t, d = x.shape
    tm = _pick(t, 512, 16)
    n = len(gains)
    n_ple = 0 if ple is None else 2

    def body(*refs):
        x_ref, add_ref = refs[0], refs[1]
        g_refs, dy_refs = refs[2:2 + n], refs[2 + n:2 + 2 * n]
        ple_refs = refs[2 + 2 * n:2 + 2 * n + n_ple]
        outs = refs[2 + 2 * n + n_ple:]
        dx_ref, dg_refs, dple_refs = outs[0], outs[1:1 + n], outs[1 + n:]
        i = pl.program_id(0)
        xv = x_ref[...]
        r = lax.rsqrt(jnp.mean(xv * xv, axis=-1, keepdims=True) + EPS)
        xhat = xv * r
        dx = add_ref[...].astype(F32)
        for g_ref, dy_ref, dg_ref in zip(g_refs, dy_refs, dg_refs):
            dy = dy_ref[...].astype(F32)
            dyg = dy * g_ref[...]
            dx = dx + r * (dyg - xhat * jnp.mean(dyg * xhat, axis=-1, keepdims=True))
            part = _colsum8(dy * xhat)

            @pl.when(i == 0)
            def _():
                dg_ref[...] = part

            @pl.when(i > 0)
            def _():
                dg_ref[...] += part

        dx_ref[...] = dx.astype(dx_dtype)
        if n_ple:
            dple_refs[0][...], dple_refs[1][...] = _ple_grads(dx, ple_refs[0][...].astype(F32),
                                                               ple_refs[1][...].astype(F32))

    outs = pl.pallas_call(
        body, name=name, grid=(t // tm,),
        in_specs=[_rows(tm, d), _rows(tm, d)] + [_const((1, d))] * n + [_rows(tm, d)] * (n + n_ple),
        out_specs=[_rows(tm, d)] + [_const((V7X_SUBLANES, d))] * n + [_rows(tm, d)] * n_ple,
        out_shape=([jax.ShapeDtypeStruct((t, d), dx_dtype)] + [jax.ShapeDtypeStruct((V7X_SUBLANES, d), F32)] * n
                   + [jax.ShapeDtypeStruct((t, d), BF16)] * n_ple),
        compiler_params=_params(1),
    )(x, add, *gains, *dys, *(ple or ()))
    dgs = [o.sum(axis=0) for o in outs[1:1 + n]]
    return (outs[0], dgs) if ple is None else (outs[0], dgs, outs[1 + n], outs[2 + n])


def _ple_norm_fwd(h, gl, pp, gains, name):
    t, d = h.shape
    tm = _pick(t, 512, 16)
    n = len(gains)

    def body(*refs):
        h_ref, gl_ref, pp_ref = refs[:3]
        g_refs, x_ref, o_refs = refs[3:3 + n], refs[3 + n], refs[4 + n:]
        xv = h_ref[...] + _sig(gl_ref[...].astype(F32)) * pp_ref[...].astype(F32)
        x_ref[...] = xv
        y = xv * lax.rsqrt(jnp.mean(xv * xv, axis=-1, keepdims=True) + EPS)
        for g_ref, o_ref in zip(g_refs, o_refs):
            o_ref[...] = (y * g_ref[...]).astype(BF16)

    outs = pl.pallas_call(
        body, name=name, grid=(t // tm,),
        in_specs=[_rows(tm, d)] * 3 + [_const((1, d))] * n, out_specs=[_rows(tm, d)] * (1 + n),
        out_shape=[jax.ShapeDtypeStruct((t, d), F32)] + [jax.ShapeDtypeStruct((t, d), BF16)] * n,
        compiler_params=_params(1),
    )(h, gl, pp, *gains)
    return outs[0], outs[1:]


def _ple_loss(h, gl, pp, target, name):
    t, d = h.shape
    tm = _pick(t, 512, 16)
    inv_d = 1.0 / d

    def body(h_ref, gl_ref, pp_ref, t_ref, dy_ref, dgl_ref, dpp_ref, l_ref):
        i = pl.program_id(0)
        gl, pp = gl_ref[...].astype(F32), pp_ref[...].astype(F32)
        e = h_ref[...] + _sig(gl) * pp - t_ref[...]
        dy = e * inv_d
        dy_ref[...] = dy.astype(BF16)
        dgl_ref[...], dpp_ref[...] = _ple_grads(dy, gl, pp)
        part = _colsum8(e * e) * (0.5 * inv_d)

        @pl.when(i == 0)
        def _():
            l_ref[...] = part

        @pl.when(i > 0)
        def _():
            l_ref[...] += part

    return pl.pallas_call(
        body, name=name, grid=(t // tm,), in_specs=[_rows(tm, d)] * 4,
        out_specs=[_rows(tm, d)] * 3 + [_const((V7X_SUBLANES, d))],
        out_shape=[jax.ShapeDtypeStruct((t, d), BF16), jax.ShapeDtypeStruct((t, d), BF16),
                   jax.ShapeDtypeStruct((t, d), BF16), jax.ShapeDtypeStruct((V7X_SUBLANES, d), F32)],
        compiler_params=_params(1),
    )(h, gl, pp, target)


def _shift_scratch(ts, cc):
    return pltpu.VMEM((V7X_SUBLANES, ts + HALO - V7X_SUBLANES, cc), F32)


def _shifted_copies(sh_ref, win_ref, cs, ts):
    rows = ts + HALO - V7X_SUBLANES
    for s in range(1, V7X_SUBLANES):
        sh_ref[s] = win_ref[pl.ds(s, rows), cs]


def _tap(sh_ref, win_ref, cs, offset, rows, r0):
    s = offset % V7X_SUBLANES
    start = pl.multiple_of(r0 + (offset - s), V7X_SUBLANES)
    if s == 0:
        return win_ref[pl.ds(start, rows), cs]
    return sh_ref[s, pl.ds(start, rows), :]


def _conv_fwd(proj, conv_w, conv_b, ln_g, ln_b, seq, name, ex=None):
    t, c3 = proj.shape
    c = c3 // 3
    ts = _pick(seq, 256, HALO)
    nsb = seq // ts
    cc = _pick(c, 512, V7X_LANES)
    hb = ts // HALO

    def body(a_ref, b_ref, z_ref, ap_ref, bp_ref, w_ref, cb_ref, g_ref, be_ref, m_ref, y_ref, win_ref, sh_ref):
        i = pl.program_id(0)
        first = (i % nsb) == 0
        win_ref[0:HALO, :] = jnp.where(first, 0.0, ap_ref[...] * _sig(bp_ref[...]))
        win_ref[HALO:, :] = a_ref[...] * _sig(b_ref[...])
        for ci in range(c // cc):
            cs = slice(ci * cc, (ci + 1) * cc)
            _shifted_copies(sh_ref, win_ref, cs, ts)

            def out_rows(rb, carry, cs=cs):
                r0 = rb * CONV_ROWS
                acc = jnp.zeros((CONV_ROWS, cc), F32) + cb_ref[:, cs]
                for k in range(CONV_WIDTH):
                    acc = acc + w_ref[k:k + 1, cs] * _tap(sh_ref, win_ref, cs, HALO - (CONV_WIDTH - 1) + k,
                                                           CONV_ROWS, r0)
                y_ref[pl.ds(pl.multiple_of(r0, CONV_ROWS), CONV_ROWS), cs] = acc
                return carry

            lax.fori_loop(0, ts // CONV_ROWS, out_rows, 0, unroll=2)
        y = y_ref[...]
        mu = jnp.mean(y, axis=-1, keepdims=True)
        xc = y - mu
        rstd = lax.rsqrt(jnp.mean(xc * xc, axis=-1, keepdims=True) + EPS)
        ln = xc * rstd * g_ref[...] + be_ref[...]
        zz = z_ref[...]
        m_ref[...] = (ln * _sig(ln) * zz * _sig(zz)).astype(BF16)

    halo_a = pl.BlockSpec((HALO, c), lambda i: (jnp.maximum(i * hb - 1, 0), 0))
    halo_b = pl.BlockSpec((HALO, c), lambda i: (jnp.maximum(i * hb - 1, 0), 1))
    (m_act, y), moved = _hosted_call(
        body, ex, name, (t // ts,),
        [_rows(ts, c, 0), _rows(ts, c, 1), _rows(ts, c, 2), halo_a, halo_b,
         _const((CONV_WIDTH, c)), _const((1, c)), _const((1, c)), _const((1, c))],
        [_rows(ts, c), _rows(ts, c)],
        [jax.ShapeDtypeStruct((t, c), BF16), jax.ShapeDtypeStruct((t, c), F32)],
        [pltpu.VMEM((HALO + ts, c), F32), _shift_scratch(ts, cc)],
        (proj, proj, proj, proj, proj, conv_w, conv_b, ln_g, ln_b))
    return m_act, y, moved


def _ln_gate_bwd(dm, y, proj, ln_g, ln_b, name):
    t, c = y.shape
    tm = _pick(t, 256, 8)

    def body(dm_ref, y_ref, z_ref, g_ref, be_ref, dy_ref, dz_ref, dg_ref, db_ref, dcb_ref):
        i = pl.program_id(0)
        yv = y_ref[...]
        mu = jnp.mean(yv, axis=-1, keepdims=True)
        xc = yv - mu
        rstd = lax.rsqrt(jnp.mean(xc * xc, axis=-1, keepdims=True) + EPS)
        xhat = xc * rstd
        g = g_ref[...]
        ln = xhat * g + be_ref[...]
        sl = _sig(ln)
        zz = z_ref[...]
        sz = _sig(zz)
        dmv = dm_ref[...].astype(F32)
        dz_ref[...] = (dmv * (ln * sl) * (sz * (1.0 + zz * (1.0 - sz)))).astype(BF16)
        dln = dmv * (zz * sz) * (sl * (1.0 + ln * (1.0 - sl)))
        dxh = dln * g
        dyv = rstd * (dxh - jnp.mean(dxh, axis=-1, keepdims=True)
                      - xhat * jnp.mean(dxh * xhat, axis=-1, keepdims=True))
        dy_ref[...] = dyv
        parts = (_colsum8(dln * xhat), _colsum8(dln), _colsum8(dyv))

        @pl.when(i == 0)
        def _():
            for ref, part in zip((dg_ref, db_ref, dcb_ref), parts):
                ref[...] = part

        @pl.when(i > 0)
        def _():
            for ref, part in zip((dg_ref, db_ref, dcb_ref), parts):
                ref[...] += part

    acc = jax.ShapeDtypeStruct((V7X_SUBLANES, c), F32)
    outs = pl.pallas_call(
        body, name=name, grid=(t // tm,),
        in_specs=[_rows(tm, c), _rows(tm, c), _rows(tm, c, 2), _const((1, c)), _const((1, c))],
        out_specs=[_rows(tm, c), _rows(tm, c)] + [_const((V7X_SUBLANES, c))] * 3,
        out_shape=[jax.ShapeDtypeStruct((t, c), F32), jax.ShapeDtypeStruct((t, c), BF16), acc, acc, acc],
        compiler_params=_params(1),
    )(dm, y, proj, ln_g, ln_b)
    return outs[0], outs[1], outs[2].sum(axis=0), outs[3].sum(axis=0), outs[4].sum(axis=0)


def _conv_bwd(dy, dz, proj, conv_w, seq, name, ex=None):
    t, c3 = proj.shape
    c = c3 // 3
    ts = _pick(seq, 256, HALO)
    nsb = seq // ts
    cc = _pick(c, 512, V7X_LANES)
    hb = ts // HALO
    last_halo = t // HALO - 1
    back = CONV_WIDTH - 1

    def body(dy_ref, dyn_ref, dz_ref, a_ref, b_ref, ap_ref, bp_ref, w_ref, o_ref, dw_ref, win_ref, dwin_ref,
             sh_ref, dsh_ref):
        i = pl.program_id(0)
        first = (i % nsb) == 0
        last = (i % nsb) == nsb - 1
        win_ref[0:HALO, :] = jnp.where(first, 0.0, ap_ref[...] * _sig(bp_ref[...]))
        win_ref[HALO:, :] = a_ref[...] * _sig(b_ref[...])
        dwin_ref[0:ts, :] = dy_ref[...]
        dwin_ref[ts:, :] = jnp.where(last, 0.0, dyn_ref[...])

        @pl.when(i == 0)
        def _():
            dw_ref[...] = jnp.zeros_like(dw_ref)

        for ci in range(c // cc):
            cs = slice(ci * cc, (ci + 1) * cc)
            _shifted_copies(sh_ref, win_ref, cs, ts)
            _shifted_copies(dsh_ref, dwin_ref, cs, ts)

            def in_grad_rows(rb, carry, cs=cs, ci=ci):
                r0 = rb * CONV_ROWS
                rows = pl.ds(pl.multiple_of(r0, CONV_ROWS), CONV_ROWS)
                dglu = jnp.zeros((CONV_ROWS, cc), F32)
                for k in range(CONV_WIDTH):
                    dglu = dglu + w_ref[k:k + 1, cs] * _tap(dsh_ref, dwin_ref, cs, back - k, CONV_ROWS, r0)
                sbc = _sig(b_ref[rows, cs])
                o_ref[rows, cs] = (dglu * sbc).astype(BF16)
                o_ref[rows, c + ci * cc:c + (ci + 1) * cc] = (dglu * a_ref[rows, cs] * sbc * (1.0 - sbc)).astype(BF16)
                return carry

            def w_grad_rows(rb, carry, cs=cs):
                r0 = rb * CONV_W_ROWS
                dcur = dwin_ref[pl.ds(pl.multiple_of(r0, CONV_W_ROWS), CONV_W_ROWS), cs]
                for k in range(CONV_WIDTH):
                    dw_ref[k * V7X_SUBLANES:(k + 1) * V7X_SUBLANES, cs] += _colsum8(
                        dcur * _tap(sh_ref, win_ref, cs, HALO - back + k, CONV_W_ROWS, r0))
                return carry

            lax.fori_loop(0, ts // CONV_ROWS, in_grad_rows, 0, unroll=2)
            lax.fori_loop(0, ts // CONV_W_ROWS, w_grad_rows, 0)
        o_ref[:, 2 * c:] = dz_ref[...]

    halo_next = pl.BlockSpec((HALO, c), lambda i: (jnp.minimum((i + 1) * hb, last_halo), 0))
    halo_a = pl.BlockSpec((HALO, c), lambda i: (jnp.maximum(i * hb - 1, 0), 0))
    halo_b = pl.BlockSpec((HALO, c), lambda i: (jnp.maximum(i * hb - 1, 0), 1))
    (dproj, dw), moved = _hosted_call(
        body, ex, name, (t // ts,),
        [_rows(ts, c), halo_next, _rows(ts, c), _rows(ts, c, 0), _rows(ts, c, 1), halo_a, halo_b,
         _const((CONV_WIDTH, c))],
        [_rows(ts, c3), _const((CONV_WIDTH * V7X_SUBLANES, c))],
        [jax.ShapeDtypeStruct((t, c3), BF16), jax.ShapeDtypeStruct((CONV_WIDTH * V7X_SUBLANES, c), F32)],
        [pltpu.VMEM((HALO + ts, c), F32), pltpu.VMEM((ts + HALO, c), F32),
         _shift_scratch(ts, cc), _shift_scratch(ts, cc)],
        (dy, dy, dz, proj, proj, proj, proj, conv_w))
    return dproj, dw.reshape(CONV_WIDTH, V7X_SUBLANES, c).sum(axis=1), moved


def _rope_tables(seq):
    half = ROPE_DIM // 2
    inv = ROPE_THETA ** (-jnp.arange(half, dtype=F32) * (2.0 / ROPE_DIM))
    ang = jnp.arange(seq).astype(F32)[:, None] * inv[None, :]
    cos, sin = jnp.cos(ang), jnp.sin(ang)
    zeros = jnp.zeros((seq, HEAD_DIM - ROPE_DIM), F32)
    zh = jnp.zeros((seq, half), F32)
    a = jnp.concatenate([cos, cos, zeros + 1.0], axis=1)
    b = jnp.concatenate([zh, sin, zeros], axis=1)
    c = jnp.concatenate([-sin, zh, zeros], axis=1)
    rep = V7X_LANES // HEAD_DIM
    return tuple(jnp.tile(v, (1, rep)) for v in (a, b, c))


def _head_ones(d):
    head = jnp.arange(d) // HEAD_DIM
    return (head[:, None] == head[None, :]).astype(BF16)


def _rope(ch, ta, tb, tc):
    return ta * ch + tb * pltpu.roll(ch, ROPE_DIM // 2, 1) + tc * pltpu.roll(ch, V7X_LANES - ROPE_DIM // 2, 1)


def _rope_t(ch, ta, tb, tc):
    return ta * ch + pltpu.roll(tb * ch, V7X_LANES - ROPE_DIM // 2, 1) + pltpu.roll(tc * ch, ROPE_DIM // 2, 1)


def _norm_rope_bwd(xhat, r, dout, gain, ta, tb, tc, e_ref):
    dxn = _rope_t(dout, ta, tb, tc)
    dxh = dxn * gain
    dx = r * (dxh - xhat * (_segsum(dxh * xhat, e_ref) * (1.0 / HEAD_DIM)))
    return dx, _colsum8(dxn * xhat)


def _norm_rope_rows(dst_ref, src_ref, gain, ta_ref, tb_ref, tc_ref, e_ref, seq, xhat_ref=None, r_ref=None):
    for r0 in range(0, seq, ATTN_PIECE):
        rows = slice(r0, r0 + ATTN_PIECE)
        xv = src_ref[rows, :]
        r = lax.rsqrt(_segsum(xv * xv, e_ref) * (1.0 / HEAD_DIM) + EPS)
        xhat = xv * r
        if xhat_ref is not None:
            xhat_ref[rows, :] = xhat
            r_ref[rows, :] = r
        dst_ref[rows, :] = _rope(xhat * gain, ta_ref[rows, :], tb_ref[rows, :], tc_ref[rows, :])


ATTN_PIECE = 256
ATTN_UNROLL = 16
CHUNK_UNROLL = 4


def _pieces(dil, seq):
    length = seq // dil
    rows = min(length, ATTN_PIECE)
    return [(r + dil * ci * rows, r * length + ci * rows, rows) for r in range(dil) for ci in range(length // rows)]


def _strided(ref, start, rows, dil):
    if dil == 1:
        return ref[pl.ds(start, rows), :]
    return ref[pl.ds(start, rows, stride=dil), :]


def _strided_set(ref, start, rows, dil, val):
    if dil == 1:
        ref[pl.ds(start, rows), :] = val
    else:
        ref[pl.ds(start, rows, stride=dil), :] = val


def _nt(a, b):
    return lax.dot_general(a, b, (((1,), (1,)), ((), ())), preferred_element_type=F32)


def _tn(a, b):
    return lax.dot_general(a, b, (((0,), (0,)), ((), ())), preferred_element_type=F32)


def _set_bias(bias_ref):
    qi = lax.broadcasted_iota(jnp.int32, (2 * SPAN, 2 * SPAN), 0) & (SPAN - 1)
    kj = lax.broadcasted_iota(jnp.int32, (2 * SPAN, 2 * SPAN), 1)
    band = jnp.logical_and(kj >= qi, (kj - SPAN) <= qi)
    bias_ref[1] = jnp.where(band, 0.0, NEG_INF)
    bias_ref[0] = jnp.where(jnp.logical_and(band, kj >= SPAN), 0.0, NEG_INF)


def _block_keys(bias_ref, j, qs, nb):
    if nb == 1:
        return pl.ds(pl.multiple_of(qs + SPAN, SPAN), SPAN), bias_ref[1, :, SPAN:]
    return pl.ds(qs, 2 * SPAN), bias_ref[jnp.minimum(j & (nb - 1), 1)]


def _stack_heads(v, head0):
    zero = jnp.zeros_like(v)
    return jnp.concatenate([jnp.where(head0, v, zero), jnp.where(head0, zero, v)], axis=0)


def _unstack_heads(v2, head0):
    return jnp.where(head0, v2[:SPAN], v2[SPAN:])


def _head_cols(v, lane=0):
    return jnp.concatenate([v[:, lane:lane + 1], v[:, HEAD_DIM + lane:HEAD_DIM + lane + 1]], axis=0)


def _attn_fwd(proj_b, kv, gains, tables, ones, bsz, seq, name):
    t, d4 = proj_b.shape
    d = d4 // 4
    nhp = d // V7X_LANES
    nblk = seq // SPAN
    scale = HEAD_DIM ** -0.5
    n_groups = len(DILATIONS)

    def body(q0_ref, q1_ref, q2_ref, k_ref, v_ref, gate_ref, gain_ref, ta_ref, tb_ref, tc_ref, e_ref,
             o_ref, l_ref, ao_ref, qd, kd, vd, od, ld, on0, on1, on2, ln0, ln1, ln2, kn, qn, bias):
        head0 = lax.broadcasted_iota(jnp.int32, (SPAN, V7X_LANES), 1) < HEAD_DIM

        @pl.when(jnp.logical_and(pl.program_id(0) == 0, pl.program_id(1) == 0))
        def _():
            _set_bias(bias)

        _norm_rope_rows(kn, k_ref, gain_ref[n_groups:n_groups + 1, :], ta_ref, tb_ref, tc_ref, e_ref, seq)
        kd[0:SPAN, :] = jnp.zeros((SPAN, V7X_LANES), BF16)
        vd[0:SPAN, :] = jnp.zeros((SPAN, V7X_LANES), BF16)
        for g, (q_ref, on, ln) in enumerate(((q0_ref, on0, ln0), (q1_ref, on1, ln1), (q2_ref, on2, ln2))):
            dil = DILATIONS[g]
            nb = seq // dil // SPAN
            _norm_rope_rows(qn, q_ref, gain_ref[g:g + 1, :], ta_ref, tb_ref, tc_ref, e_ref, seq)
            for ns, rs, rows in _pieces(dil, seq):
                qd[rs:rs + rows, :] = _strided(qn, ns, rows, dil).astype(BF16)
                kd[SPAN + rs:SPAN + rs + rows, :] = _strided(kn, ns, rows, dil).astype(BF16)
                vd[SPAN + rs:SPAN + rs + rows, :] = _strided(v_ref, ns, rows, dil).astype(BF16)

            def block(j, carry):
                qs = pl.multiple_of(j * SPAN, SPAN)
                q2 = _stack_heads(qd[pl.ds(qs, SPAN), :], head0)
                keys, mask = _block_keys(bias, j, qs, nb)
                kk = kd[keys, :]
                vv = vd[keys, :]
                s = _nt(q2, kk) * scale + mask
                mx = jnp.max(s, axis=1, keepdims=True)
                p = jnp.exp(s - mx)
                den = jnp.sum(p, axis=1, keepdims=True)
                o2 = jnp.dot(p.astype(BF16), vv, preferred_element_type=F32) / den
                l2 = jnp.broadcast_to(mx + jnp.log(den), (2 * SPAN, V7X_LANES))
                od[pl.ds(qs, SPAN), :] = _unstack_heads(o2, head0)
                ld[pl.ds(qs, SPAN), :] = _unstack_heads(l2, head0)
                return carry

            lax.fori_loop(0, nblk, block, 0, unroll=ATTN_UNROLL)
            for ns, rs, rows in _pieces(dil, seq):
                _strided_set(on, ns, rows, dil, od[rs:rs + rows, :])
                _strided_set(ln, ns, rows, dil, ld[rs:rs + rows, :])

        def merge(ci, carry):
            rows = pl.ds(pl.multiple_of(ci * ATTN_PIECE, ATTN_PIECE), ATTN_PIECE)
            ls = [ln0[rows, :], ln1[rows, :], ln2[rows, :]]
            mx = jnp.maximum(jnp.maximum(ls[0], ls[1]), ls[2])
            es = [jnp.exp(v - mx) for v in ls]
            den = es[0] + es[1] + es[2]
            ov = (es[0] * on0[rows, :] + es[1] * on1[rows, :] + es[2] * on2[rows, :]) / den
            gate = gate_ref[rows, :]
            o_ref[rows, :] = ov
            l_ref[rows, :] = mx + jnp.log(den)
            ao_ref[rows, :] = (ov * gate * _sig(gate)).astype(BF16)
            return carry

        lax.fori_loop(0, seq // ATTN_PIECE, merge, 0)

    blk = (None, seq, V7X_LANES)
    pview = proj_b.reshape(bsz, seq, d4)
    kview = kv.reshape(bsz, seq, 2 * d)
    out_spec = pl.BlockSpec(blk, lambda b, h: (b, 0, h))
    tab = pl.BlockSpec((seq, V7X_LANES), lambda b, h: (0, 0))
    nat = pltpu.VMEM((seq, V7X_LANES), F32)
    o, lse, ao = pl.pallas_call(
        body, name=name, grid=(bsz, nhp),
        in_specs=[pl.BlockSpec(blk, lambda b, h: (b, 0, h)),
                  pl.BlockSpec(blk, lambda b, h: (b, 0, nhp + h)),
                  pl.BlockSpec(blk, lambda b, h: (b, 0, 2 * nhp + h)),
                  pl.BlockSpec(blk, lambda b, h: (b, 0, h)),
                  pl.BlockSpec(blk, lambda b, h: (b, 0, nhp + h)),
                  pl.BlockSpec(blk, lambda b, h: (b, 0, 3 * nhp + h)),
                  pl.BlockSpec((n_groups + 1, V7X_LANES), lambda b, h: (0, 0)),
                  tab, tab, tab,
                  pl.BlockSpec((V7X_LANES, V7X_LANES), lambda b, h: (0, 0))],
        out_specs=[out_spec, out_spec, out_spec],
        out_shape=[jax.ShapeDtypeStruct((bsz, seq, d), F32), jax.ShapeDtypeStruct((bsz, seq, d), F32),
                   jax.ShapeDtypeStruct((bsz, seq, d), BF16)],
        scratch_shapes=[pltpu.VMEM((seq, V7X_LANES), BF16), pltpu.VMEM((SPAN + seq, V7X_LANES), BF16),
                        pltpu.VMEM((SPAN + seq, V7X_LANES), BF16), nat, nat, nat, nat, nat, nat, nat, nat, nat, nat,
                        pltpu.VMEM((2, 2 * SPAN, 2 * SPAN), F32)],
        compiler_params=_params(2),
    )(pview, pview, pview, kview, kview, pview, gains, *tables, ones)
    return o.reshape(t, d), lse.reshape(t, d), ao.reshape(t, d)


def _attn_bwd(proj_b, kv, dao, o, lse, gains, tables, ones, bsz, seq, name):
    t, d4 = proj_b.shape
    d = d4 // 4
    nhp = d // V7X_LANES
    nblk = seq // SPAN
    scale = HEAD_DIM ** -0.5
    n_groups = len(DILATIONS)
    n_chunks = seq // ATTN_PIECE

    def body(q_ref, k_ref, v_ref, gate_ref, dao_ref, o_ref, l_ref, gain_ref, ta_ref, tb_ref, tc_ref, e_ref,
             dproj_ref, dkv_ref, dg_ref, qd, kd, vd, dod, std, dqd, dkd, dvd, dqn, dk0, dk1, dk2, dv0, dv1, dv2,
             kn, kxh, krr, qn, qxh, qrr, don, stn, bias):
        head0 = lax.broadcasted_iota(jnp.int32, (SPAN, V7X_LANES), 1) < HEAD_DIM
        g = pl.program_id(2)

        @pl.when(jnp.logical_and(jnp.logical_and(pl.program_id(0) == 0, pl.program_id(1) == 0), g == 0))
        def _():
            _set_bias(bias)
            dg_ref[...] = jnp.zeros_like(dg_ref)

        @pl.when(g == 0)
        def _():
            first_half = (lax.broadcasted_iota(jnp.int32, (ATTN_PIECE, V7X_LANES), 1) & (HEAD_DIM - 1)) < HEAD_DIM // 2
            _norm_rope_rows(kn, k_ref, gain_ref[n_groups:n_groups + 1, :], ta_ref, tb_ref, tc_ref, e_ref, seq,
                            kxh, krr)
            for r0 in range(0, seq, ATTN_PIECE):
                rows = slice(r0, r0 + ATTN_PIECE)
                gate = gate_ref[rows, :]
                dov = dao_ref[rows, :].astype(F32) * gate * _sig(gate)
                don[rows, :] = dov
                stn[rows, :] = jnp.where(first_half, l_ref[rows, :], _segsum(dov * o_ref[rows, :], e_ref))

        def norm_bwd_chunks(xhat_ref, r_ref, dn_refs, out_ref, gi):
            def chunk(ci, carry):
                rows = pl.ds(pl.multiple_of(ci * ATTN_PIECE, ATTN_PIECE), ATTN_PIECE)
                dn = functools.reduce(lambda u, w: u + w, [r_[rows, :] for r_ in dn_refs])
                dx, part = _norm_rope_bwd(xhat_ref[rows, :], r_ref[rows, :], dn, gain_ref[gi:gi + 1, :],
                                          ta_ref[rows, :], tb_ref[rows, :], tc_ref[rows, :], e_ref)
                out_ref[rows, :] = dx.astype(BF16)
                dg_ref[gi] += part
                return carry
            lax.fori_loop(0, n_chunks, chunk, 0, unroll=CHUNK_UNROLL)

        def group(gi):
            dil = DILATIONS[gi]
            nb = seq // dil // SPAN
            kd[0:SPAN, :] = jnp.zeros((SPAN, V7X_LANES), BF16)
            vd[0:SPAN, :] = jnp.zeros((SPAN, V7X_LANES), BF16)
            dkd[...] = jnp.zeros_like(dkd)
            dvd[...] = jnp.zeros_like(dvd)
            _norm_rope_rows(qn, q_ref, gain_ref[gi:gi + 1, :], ta_ref, tb_ref, tc_ref, e_ref, seq, qxh, qrr)
            for ns, rs, rows in _pieces(dil, seq):
                qd[rs:rs + rows, :] = _strided(qn, ns, rows, dil).astype(BF16)
                kd[SPAN + rs:SPAN + rs + rows, :] = _strided(kn, ns, rows, dil).astype(BF16)
                vd[SPAN + rs:SPAN + rs + rows, :] = _strided(v_ref, ns, rows, dil).astype(BF16)
                dod[rs:rs + rows, :] = _strided(don, ns, rows, dil).astype(BF16)
                std[rs:rs + rows, :] = _strided(stn, ns, rows, dil)

            def block(j, carry):
                qs = pl.multiple_of(j * SPAN, SPAN)
                q2 = _stack_heads(qd[pl.ds(qs, SPAN), :], head0)
                do2 = _stack_heads(dod[pl.ds(qs, SPAN), :], head0)
                keys, mask = _block_keys(bias, j, qs, nb)
                kk = kd[keys, :]
                vv = vd[keys, :]
                s = _nt(q2, kk) * scale + mask
                stv = std[pl.ds(qs, SPAN), :]
                p = jnp.exp(s - _head_cols(stv))
                ds = (p * (_nt(do2, vv) - _head_cols(stv, HEAD_DIM // 2)) * scale).astype(BF16)
                dqd[pl.ds(qs, SPAN), :] = _unstack_heads(jnp.dot(ds, kk, preferred_element_type=F32), head0)
                dkd[keys, :] += _tn(ds, q2)
                dvd[keys, :] += _tn(p.astype(BF16), do2)
                return carry

            lax.fori_loop(0, nblk, block, 0, unroll=ATTN_UNROLL)
            for ns, rs, rows in _pieces(dil, seq):
                _strided_set(dqn, ns, rows, dil, dqd[rs:rs + rows, :])
                _strided_set((dk0, dk1, dk2)[gi], ns, rows, dil, dkd[SPAN + rs:SPAN + rs + rows, :])
                _strided_set((dv0, dv1, dv2)[gi], ns, rows, dil, dvd[SPAN + rs:SPAN + rs + rows, :])
            norm_bwd_chunks(qxh, qrr, [dqn], dproj_ref, gi)

        for gi in range(n_groups):
            @pl.when(g == gi)
            def _():
                group(gi)

        @pl.when(g == n_groups - 1)
        def _():
            norm_bwd_chunks(kxh, krr, [dk0, dk1, dk2], dkv_ref, n_groups)

        @pl.when(g == n_groups)
        def _():
            def chunk(ci, carry):
                rows = pl.ds(pl.multiple_of(ci * ATTN_PIECE, ATTN_PIECE), ATTN_PIECE)
                gate = gate_ref[rows, :]
                sg = _sig(gate)
                dproj_ref[rows, :] = (dao_ref[rows, :].astype(F32) * o_ref[rows, :]
                                      * (sg * (1.0 + gate * (1.0 - sg)))).astype(BF16)
                dkv_ref[rows, :] = (dv0[rows, :] + dv1[rows, :] + dv2[rows, :]).astype(BF16)
                return carry
            lax.fori_loop(0, n_chunks, chunk, 0, unroll=CHUNK_UNROLL)

    blk = (None, seq, V7X_LANES)
    pview = proj_b.reshape(bsz, seq, d4)
    kview = kv.reshape(bsz, seq, 2 * d)
    dview = (bsz, seq, d)
    d_spec = pl.BlockSpec(blk, lambda b, h, g: (b, 0, h))
    tab = pl.BlockSpec((seq, V7X_LANES), lambda b, h, g: (0, 0))
    nat = pltpu.VMEM((seq, V7X_LANES), F32)
    natb = pltpu.VMEM((seq, V7X_LANES), BF16)
    pad = pltpu.VMEM((SPAN + seq, V7X_LANES), F32)
    padb = pltpu.VMEM((SPAN + seq, V7X_LANES), BF16)
    dproj, dkv, dg = pl.pallas_call(
        body, name=name, grid=(bsz, nhp, n_groups + 1),
        in_specs=[pl.BlockSpec(blk, lambda b, h, g: (b, 0, jnp.minimum(g, n_groups - 1) * nhp + h)),
                  pl.BlockSpec(blk, lambda b, h, g: (b, 0, h)),
                  pl.BlockSpec(blk, lambda b, h, g: (b, 0, nhp + h)),
                  pl.BlockSpec(blk, lambda b, h, g: (b, 0, n_groups * nhp + h)),
                  d_spec, d_spec, d_spec,
                  pl.BlockSpec((n_groups + 1, V7X_LANES), lambda b, h, g: (0, 0)),
                  tab, tab, tab,
                  pl.BlockSpec((V7X_LANES, V7X_LANES), lambda b, h, g: (0, 0))],
        out_specs=[pl.BlockSpec(blk, lambda b, h, g: (b, 0, g * nhp + h)),
                   pl.BlockSpec(blk, lambda b, h, g: (b, 0, (g // n_groups) * nhp + h)),
                   pl.BlockSpec((n_groups + 1, V7X_SUBLANES, V7X_LANES), lambda b, h, g: (0, 0, 0))],
        out_shape=[jax.ShapeDtypeStruct((bsz, seq, d4), BF16), jax.ShapeDtypeStruct((bsz, seq, 2 * d), BF16),
                   jax.ShapeDtypeStruct((n_groups + 1, V7X_SUBLANES, V7X_LANES), F32)],
        scratch_shapes=[natb, padb, padb, natb, nat, nat, pad, pad] + [nat] * 15 + [
                        pltpu.VMEM((2, 2 * SPAN, 2 * SPAN), F32)],
        compiler_params=_params(3),
    )(pview, kview, kview, pview, dao.reshape(dview), o.reshape(dview), lse.reshape(dview), gains, *tables, ones)
    dgain = dg.sum(axis=1).reshape(n_groups + 1, V7X_LANES // HEAD_DIM, HEAD_DIM).sum(axis=1)
    return dproj.reshape(t, d4), dkv.reshape(t, 2 * d), dgain


def _mesh_position():
    x, y, c = lax.axis_index("x"), lax.axis_index("y"), lax.axis_index("c")
    return x, y, c


def _peer(x, y, c, rel):
    return (1 - x if rel & 4 else x, 1 - y if rel & 2 else y, 1 - c if rel & 1 else c)


class _Exchange:
    def __init__(self, srcs, gather):
        self.srcs = list(srcs)
        self.gather = gather
        n = self.n = len(self.srcs)
        hbm = pl.BlockSpec(memory_space=pltpu.HBM)
        self.in_specs = [hbm] * n
        self.out_specs = [hbm] * n
        self.out_shape = [jax.ShapeDtypeStruct(((N_DEV,) + a.shape) if gather else a.shape, a.dtype)
                          for a in self.srcs]
        self.scratch = [pltpu.SemaphoreType.DMA((n * (N_DEV - 1),)), pltpu.SemaphoreType.DMA((n * (N_DEV - 1),)),
                        pltpu.SemaphoreType.DMA((n,))]

    def _copies(self, ins, outs, sems):
        send_sems, recv_sems, local_sems = sems
        x, y, c = _mesh_position()
        me = 4 * x + 2 * y + c
        remote, local = [], []
        for a in range(self.n):
            mine = ins[a] if self.gather else ins[a].at[me]
            local.append(pltpu.make_async_copy(mine, outs[a].at[me], local_sems.at[a]))
            for rel in range(1, N_DEV):
                px, py, pc = _peer(x, y, c, rel)
                s = a * (N_DEV - 1) + rel - 1
                src = ins[a] if self.gather else ins[a].at[4 * px + 2 * py + pc]
                remote.append(pltpu.make_async_remote_copy(
                    src_ref=src, dst_ref=outs[a].at[me], send_sem=send_sems.at[s], recv_sem=recv_sems.at[s],
                    device_id=(px, py, pc), device_id_type=pl.DeviceIdType.MESH))
        return remote, local

    def start(self, ins, outs, sems):
        remote, local = self._copies(ins, outs, sems)
        for cp in local + remote:
            cp.start()

    def wait(self, ins, outs, sems):
        remote, local = self._copies(ins, outs, sems)
        for cp in remote:
            cp.wait_recv()
        for cp in remote:
            cp.wait_send()
        for cp in local:
            cp.wait()


def _gather_chip_once(arrs, name):
    n = len(arrs)
    per = N_DEV - 1

    def body(*refs):
        ins, outs = refs[:n], refs[n:2 * n]
        send_sems, recv_sems, local_sems = refs[2 * n:]
        x, y, c = _mesh_position()
        me, sibling = (x, y, c), (x, y, 1 - c)
        chips = [(1 - x, y), (x, 1 - y), (1 - x, 1 - y)]

        def copy(a, k, block, to, src=None):
            bx, by, bc = block
            dst = outs[a].at[4 * bx + 2 * by + bc]
            return pltpu.make_async_remote_copy(
                src_ref=dst if src is None else src, dst_ref=dst, send_sem=send_sems.at[a * per + k],
                recv_sem=recv_sems.at[a * per + k], device_id=to, device_id_type=pl.DeviceIdType.MESH)

        local, sent = [], []
        for a in range(n):
            mine = pltpu.make_async_copy(ins[a], outs[a].at[4 * x + 2 * y + c], local_sems.at[a])
            mine.start()
            local.append(mine)
            first = [copy(a, 0, me, sibling, src=ins[a])]
            first += [copy(a, 1 + j, me, chip + (c,), src=ins[a]) for j, chip in enumerate(chips)]
            for cp in first:
                cp.start()
            sent += first
        for a in range(n):
            for j, chip in enumerate(chips):
                copy(a, 1 + j, chip + (c,), me).wait_recv()
                passed = copy(a, 4 + j, chip + (c,), sibling)
                passed.start()
                sent.append(passed)
        for a in range(n):
            copy(a, 0, sibling, me).wait_recv()
            for j, chip in enumerate(chips):
                copy(a, 4 + j, chip + (1 - c,), me).wait_recv()
        for cp in sent:
            cp.wait_send()
        for cp in local:
            cp.wait()

    hbm = pl.BlockSpec(memory_space=pltpu.HBM)
    return pl.pallas_call(
        body, name=name, in_specs=[hbm] * n, out_specs=[hbm] * n,
        out_shape=[jax.ShapeDtypeStruct((N_DEV,) + a.shape, a.dtype) for a in arrs],
        scratch_shapes=[pltpu.SemaphoreType.DMA((n * per,)), pltpu.SemaphoreType.DMA((n * per,)),
                        pltpu.SemaphoreType.DMA((n,))],
    )(*arrs)


def _run_exchange(ex, name):
    n = ex.n

    def body(*refs):
        ins, outs, sems = refs[:n], refs[n:2 * n], refs[2 * n:]
        ex.start(ins, outs, sems)
        ex.wait(ins, outs, sems)

    return pl.pallas_call(body, name=name, in_specs=ex.in_specs, out_specs=ex.out_specs, out_shape=ex.out_shape,
                          scratch_shapes=ex.scratch)(*ex.srcs)


def _hosted_call(body, ex, name, grid, in_specs, out_specs, out_shape, scratch_shapes, args):
    if ex is None:
        outs = pl.pallas_call(body, name=name, grid=grid, in_specs=in_specs, out_specs=out_specs, out_shape=out_shape,
                              scratch_shapes=scratch_shapes, compiler_params=_params(len(grid)))(*args)
        return list(outs), []
    n_in, n_out, n_scr, n = len(in_specs), len(out_specs), len(scratch_shapes), ex.n

    def hosted(*refs):
        h_in, e_in = refs[:n_in], refs[n_in:n_in + n]
        o0 = n_in + n
        h_out, e_out = refs[o0:o0 + n_out], refs[o0 + n_out:o0 + n_out + n]
        s0 = o0 + n_out + n
        h_scr, e_scr = refs[s0:s0 + n_scr], refs[s0 + n_scr:]
        ids = [pl.program_id(a) for a in range(len(grid))]
        first = functools.reduce(jnp.logical_and, [i == 0 for i in ids])
        last = functools.reduce(jnp.logical_and, [i == g - 1 for i, g in zip(ids, grid)])

        @pl.when(first)
        def _():
            ex.start(e_in, e_out, e_scr)

        body(*h_in, *h_out, *h_scr)

        @pl.when(last)
        def _():
            ex.wait(e_in, e_out, e_scr)

    outs = pl.pallas_call(
        hosted, name=name, grid=grid, in_specs=list(in_specs) + ex.in_specs,
        out_specs=list(out_specs) + ex.out_specs, out_shape=list(out_shape) + ex.out_shape,
        scratch_shapes=list(scratch_shapes) + ex.scratch, compiler_params=_params(len(grid)),
    )(*args, *ex.srcs)
    return list(outs[:n_out]), list(outs[n_out:])


def _sum_adamw(parts, w, m, v, name):
    _, r, wd = parts.shape
    tr = _pick(r, ADAM_ROWS, 8)
    c1 = 1.0 - ADAM_B1 ** ADAM_STEP
    c2 = 1.0 - ADAM_B2 ** ADAM_STEP

    def body(p_ref, w_ref, m_ref, v_ref, g_ref, d_ref, nm_ref, nv_ref):
        g = p_ref[0].astype(F32)
        for s in range(1, N_DEV):
            g = g + p_ref[s].astype(F32)
        nm = ADAM_B1 * m_ref[...] + (1.0 - ADAM_B1) * g
        nv = ADAM_B2 * v_ref[...] + (1.0 - ADAM_B2) * (g * g)
        g_ref[...] = g
        nm_ref[...] = nm
        nv_ref[...] = nv
        d_ref[...] = -ADAM_LR * ((nm / c1) / (jnp.sqrt(nv / c2) + ADAM_EPS) + ADAM_WD * w_ref[...])

    row = pl.BlockSpec((tr, wd), lambda i: (i, 0))
    return pl.pallas_call(
        body, name=name, grid=(r // tr,),
        in_specs=[pl.BlockSpec((N_DEV, tr, wd), lambda i: (0, i, 0)), row, row, row],
        out_specs=[row] * 4, out_shape=[jax.ShapeDtypeStruct((r, wd), F32)] * 4,
        compiler_params=_params(1),
    )(parts, w, m, v)


def _pack_rows(size, row_mult):
    rows = -(-size // PACK_LANES)
    return -(-rows // row_mult) * row_mult


def _pack(flats, row_mult, dtype, total_mult=None):
    out = []
    for f in flats:
        size = f.shape[-1]
        rows = _pack_rows(size, row_mult)
        pad = [(0, 0)] * (f.ndim - 1) + [(0, rows * PACK_LANES - size)]
        out.append(jnp.pad(f.astype(dtype), pad).reshape(f.shape[:-1] + (rows, PACK_LANES)))
    if total_mult is not None:
        total = sum(o.shape[-2] for o in out)
        extra = -(-total // total_mult) * total_mult - total
        if extra:
            out.append(jnp.zeros(out[0].shape[:-2] + (extra, PACK_LANES), dtype))
    return jnp.concatenate(out, axis=-2)


def _unpack(buf, sizes, row_mult):
    out, row = [], 0
    for size in sizes:
        rows = _pack_rows(size, row_mult)
        part = buf[..., row:row + rows, :]
        out.append(part.reshape(buf.shape[:-2] + (rows * PACK_LANES,))[..., :size])
        row += rows
    return out


def _to_slots(full, axis):
    if axis is None:
        return jnp.broadcast_to(full.reshape(1, -1), (N_DEV, full.size))
    shape = full.shape
    split = full.reshape(shape[:axis] + (N_DEV, shape[axis] // N_DEV) + shape[axis + 1:])
    return jnp.moveaxis(split, axis, 0).reshape(N_DEV, -1)


def _from_slots(slots, axis, block_shape):
    split = jnp.moveaxis(slots, 0, axis)
    shape = list(block_shape)
    shape[axis] *= N_DEV
    return split.reshape(shape)


def kernel(x, p, norm_g, w_in_a, conv_w, conv_b, ln_g, ln_b, w_out_a, kv_norm_g, w_kv, k_norm_g, w_in_b, q_norm_g, w_out_b, ple_norm_g, w_ple_gate, w_ple_proj, loss_target, m_norm_g, m_w_in_a, m_conv_w, m_conv_b, m_ln_g, m_ln_b, m_w_out_a, m_kv_norm_g, m_w_kv, m_k_norm_g, m_w_in_b, m_q_norm_g, m_w_out_b, m_ple_norm_g, m_w_ple_gate, m_w_ple_proj, v_norm_g, v_w_in_a, v_conv_w, v_conv_b, v_ln_g, v_ln_b, v_w_out_a, v_kv_norm_g, v_w_kv, v_k_norm_g, v_w_in_b, v_q_norm_g, v_w_out_b, v_ple_norm_g, v_w_ple_gate, v_w_ple_proj):
    weights = dict(zip(WEIGHT_NAMES, (norm_g, w_in_a, conv_w, conv_b, ln_g, ln_b, w_out_a, kv_norm_g, w_kv, k_norm_g,
                                      w_in_b, q_norm_g, w_out_b, ple_norm_g, w_ple_gate, w_ple_proj)))
    mom_m = dict(zip(WEIGHT_NAMES, (m_norm_g, m_w_in_a, m_conv_w, m_conv_b, m_ln_g, m_ln_b, m_w_out_a, m_kv_norm_g,
                                    m_w_kv, m_k_norm_g, m_w_in_b, m_q_norm_g, m_w_out_b, m_ple_norm_g, m_w_ple_gate,
                                    m_w_ple_proj)))
    mom_v = dict(zip(WEIGHT_NAMES, (v_norm_g, v_w_in_a, v_conv_w, v_conv_b, v_ln_g, v_ln_b, v_w_out_a, v_kv_norm_g,
                                    v_w_kv, v_k_norm_g, v_w_in_b, v_q_norm_g, v_w_out_b, v_ple_norm_g, v_w_ple_gate,
                                    v_w_ple_proj)))
    bsz, seq, d = x.shape
    t = bsz * seq
    assert seq % (max(DILATIONS) * SPAN) == 0 and d % V7X_LANES == 0

    full = {}

    def rows2d(a):
        return a.reshape(-1, a.shape[-1])

    def packed(source, names, dtype, total_mult=None):
        return _pack([source[n].reshape(-1) for n in names], 16, dtype, total_mult)

    def gathered(names, bufs):
        for n, buf in zip(names, bufs):
            full[n] = _from_slots(buf.reshape((N_DEV,) + weights[n].shape), SHARD_AXIS[n], weights[n].shape)

    w1_all, wv_all = _gather_chip_once([rows2d(weights['w_in_a']).astype(BF16),
                                        _pack([weights[n].reshape(-1) for n in VECTOR_WEIGHTS], 8, F32)],
                                       "gather_first")
    gathered(GROUP_FIRST, [w1_all])
    for n, slots in zip(VECTOR_WEIGHTS, _unpack(wv_all, [weights[n].size for n in VECTOR_WEIGHTS], 8)):
        full[n] = _from_slots(slots.reshape((N_DEV,) + weights[n].shape), SHARD_AXIS[n], weights[n].shape)
    gather_rest = _Exchange([rows2d(weights[n]).astype(BF16) for n in GROUP_REST], gather=True)
    wa_in = full['w_in_a'][0]
    cw, cb, lg, lb = full['conv_w'][0], full['conv_b'], full['ln_g'], full['ln_b']

    tables = _rope_tables(seq)
    ones = _head_ones(V7X_LANES)
    rep = V7X_LANES // HEAD_DIM
    head_gain = jnp.concatenate([jnp.tile(q_norm_g[0], (1, rep)), jnp.tile(k_norm_g, rep)[None]], axis=0)

    x0 = x.reshape(t, d)
    p0, p1 = p[0].reshape(t, -1), p[1].reshape(t, -1)
    target = loss_target.reshape(t, d)
    g_norm0, g_norm1 = norm_g[0:1], norm_g[1:2]
    g_ple0, g_ple1 = ple_norm_g[0:1], ple_norm_g[1:2]
    g_kv = kv_norm_g.reshape(1, d)

    (u0,) = _rmsnorm_fwd(x0, [g_norm0], "norm0")
    proj_a = _matmul(u0, wa_in, 'nn', "in_a")
    m_act, y_conv, w2_all = _conv_fwd(proj_a, cw, cb, lg, lb, seq, "conv_fwd", ex=gather_rest)
    gathered(GROUP_REST, w2_all)
    wa_out = full['w_out_a'][0]
    wkv = full['w_kv']
    wb_in, wb_out = full['w_in_b'][0], full['w_out_b'][0]
    wg, wp = full['w_ple_gate'], full['w_ple_proj']
    h0, pg0 = _matmul(m_act, wa_out, 'nn', "out_a", add=x0, norm_gain=g_ple0)
    gl0 = _matmul(pg0, wg[0], 'nn', "ple_gate0", out_dtype=BF16)
    pp0 = _matmul(p0, wp[0], 'nn', "ple_proj0", out_dtype=BF16)

    x1, (kvn, u1) = _ple_norm_fwd(h0, gl0, pp0, [g_kv, g_norm1], "ple0_norm1")
    kv = _matmul(kvn, wkv, 'nn', "kv")
    proj_b = _matmul(u1, wb_in, 'nn', "in_b")
    o_att, lse, ao = _attn_fwd(proj_b, kv, head_gain, tables, ones, bsz, seq, "attn_fwd")
    h1, pg1 = _matmul(ao, wb_out, 'nn', "out_b", add=x1, norm_gain=g_ple1)
    gl1 = _matmul(pg1, wg[1], 'nn', "ple_gate1", out_dtype=BF16)
    pp1 = _matmul(p1, wp[1], 'nn', "ple_proj1", out_dtype=BF16)

    dx2, dgl1, dpp1, loss_part = _ple_loss(h1, gl1, pp1, target, "ple1_loss")
    loss = lax.psum(jnp.sum(loss_part), ("x", "y", "c"))

    grads = {}
    slot = {}

    dwp1 = _matmul(p1, dpp1, 'tn', "d_ple_proj1", out_dtype=BF16, slot_cols=d // N_DEV)
    dwg1 = _matmul(pg1, dgl1, 'tn', "d_ple_gate1", out_dtype=BF16)
    dpg1 = _matmul(dgl1, wg[1], 'nt', "d_ple_norm1", out_dtype=BF16)
    dh1, (dg_ple1,) = _rmsnorm_bwd(h1, [g_ple1], [dpg1], dx2, "ple_norm1_bwd")
    slot['w_out_b'] = _matmul(ao, dh1, 'tn', "d_out_b", out_dtype=BF16).reshape(N_DEV, -1, d)
    dao = _matmul(dh1, wb_out, 'nt', "d_ao", out_dtype=BF16)
    dproj_b, dkv, dg_head = _attn_bwd(proj_b, kv, dao, o_att, lse, head_gain, tables, ones, bsz, seq, "attn_bwd")
    slot['w_in_b'] = _matmul(u1, dproj_b, 'tn', "d_in_b", out_dtype=BF16, slot_cols=4 * d // N_DEV)
    du1 = _matmul(dproj_b, wb_in, 'nt', "d_u1", out_dtype=BF16)
    slot['w_kv'] = _matmul(kvn, dkv, 'tn', "d_kv", out_dtype=BF16, slot_cols=2 * d // N_DEV)
    dkvn = _matmul(dkv, wkv, 'nt', "d_kvn", out_dtype=BF16)
    dx1, (dg_kv, dg_norm1), dgl0, dpp0 = _rmsnorm_bwd(x1, [g_kv, g_norm1], [dkvn, du1], dh1, "norm1_bwd",
                                                      ple=(gl0, pp0))

    dwp0 = _matmul(p0, dpp0, 'tn', "d_ple_proj0", out_dtype=BF16, slot_cols=d // N_DEV)
    dwg0 = _matmul(pg0, dgl0, 'tn', "d_ple_gate0", out_dtype=BF16)
    dpg0 = _matmul(dgl0, wg[0], 'nt', "d_ple_norm0", out_dtype=BF16)
    dh0, (dg_ple0,) = _rmsnorm_bwd(h0, [g_ple0], [dpg0], dx1, "ple_norm0_bwd")
    slot['w_out_a'] = _matmul(m_act, dh0, 'tn', "d_out_a", out_dtype=BF16).reshape(N_DEV, -1, d)
    dm = _matmul(dh0, wa_out, 'nt', "d_m", out_dtype=BF16)
    dy_conv, dz, d_lg, d_lb, d_cb = _ln_gate_bwd(dm, y_conv, proj_a, lg, lb, "ln_gate_bwd")
    slot['w_ple_gate'] = jnp.stack([dwg0.reshape(N_DEV, -1, d), dwg1.reshape(N_DEV, -1, d)],
                                   axis=1).reshape(N_DEV, -1, d)
    slot['w_ple_proj'] = jnp.stack([dwp0, dwp1], axis=1).reshape(N_DEV, -1, d // N_DEV)

    dproj_a, d_cw, parts_rest = _conv_bwd(dy_conv, dz, proj_a, cw, seq, "conv_bwd",
                                          ex=_Exchange([slot[n] for n in GROUP_REST], gather=False))
    slot['w_in_a'] = _matmul(u0, dproj_a, 'tn', "d_in_a", out_dtype=BF16, slot_cols=wa_in.shape[1] // N_DEV)
    du0, parts_first = _matmul(dproj_a, wa_in, 'nt', "d_u0", out_dtype=BF16,
                               ex=_Exchange([slot['w_in_a']], gather=False))
    dx0, (dg_norm0,) = _rmsnorm_bwd(x0, [g_norm0], [du0], dh0, "norm0_bwd", dx_dtype=F32)

    grads['norm_g'] = jnp.stack([dg_norm0, dg_norm1])
    grads['conv_w'] = d_cw[None]
    grads['conv_b'] = d_cb[None]
    grads['ln_g'] = d_lg[None]
    grads['ln_b'] = d_lb[None]
    grads['kv_norm_g'] = dg_kv
    grads['k_norm_g'] = dg_head[3]
    grads['q_norm_g'] = dg_head[0:3][None]
    grads['ple_norm_g'] = jnp.stack([dg_ple0, dg_ple1])
    small_pack = _pack([_to_slots(grads[n], SHARD_AXIS[n]) for n in GROUP_SMALL], 16, BF16)
    (parts_small,) = _run_exchange(_Exchange([small_pack], gather=False), "exchange_small")

    updated = {}
    for n, parts in zip(GROUP_REST + GROUP_FIRST, parts_rest + parts_first):
        outs = _sum_adamw(parts, rows2d(weights[n]), rows2d(mom_m[n]), rows2d(mom_v[n]), "sum_adamw_" + n)
        for kind, buf in enumerate(outs):
            updated[kind, n] = buf.reshape(weights[n].shape)
    outs = _sum_adamw(parts_small, packed(weights, GROUP_SMALL, F32), packed(mom_m, GROUP_SMALL, F32),
                      packed(mom_v, GROUP_SMALL, F32), "sum_adamw_small")
    sizes = [weights[n].size for n in GROUP_SMALL]
    for kind, buf in enumerate(outs):
        for n, flat in zip(GROUP_SMALL, _unpack(buf, sizes, 16)):
            updated[kind, n] = flat.reshape(weights[n].shape)
    result = [loss, dx0.reshape(bsz, seq, d)]
    for kind in range(4):
        result.extend(updated[kind, n] for n in WEIGHT_NAMES)
    return tuple(result)
```

```python
import functools

import jax
import jax.numpy as jnp
from jax import lax
from jax.experimental import pallas as pl
from jax.experimental.pallas import tpu as pltpu

F32 = jnp.float32
BF16 = jnp.bfloat16

N_DEV = 8
HEAD_DIM = 64
ROPE_DIM = 16
ROPE_THETA = 500000.0
EPS = 1e-6
NEG_INF = -1e30
SPAN = 128
DILATIONS = (1, 4, 16)
CONV_WIDTH = 31
HALO = 32
CONV_ROWS = 32
CONV_W_ROWS = 64
CONV_UNROLL = 4
PACK_LANES = 1024
V7X_LANES = 128
V7X_SUBLANES = 8
VMEM_LIMIT_BYTES = 56 * 1024 * 1024

ADAM_LR = 0.001
ADAM_B1 = 0.9
ADAM_B2 = 0.999
ADAM_EPS = 1e-08
ADAM_WD = 0.01
ADAM_STEP = 10
ADAM_ROWS = 256

WEIGHT_NAMES = ('norm_g', 'w_in_a', 'conv_w', 'conv_b', 'ln_g', 'ln_b', 'w_out_a', 'kv_norm_g', 'w_kv',
                'k_norm_g', 'w_in_b', 'q_norm_g', 'w_out_b', 'ple_norm_g', 'w_ple_gate', 'w_ple_proj')
SHARD_AXIS = {'norm_g': None, 'w_in_a': 2, 'conv_w': 2, 'conv_b': 1, 'ln_g': 1, 'ln_b': 1, 'w_out_a': 1,
              'kv_norm_g': None, 'w_kv': 1, 'k_norm_g': None, 'w_in_b': 2, 'q_norm_g': None, 'w_out_b': 1,
              'ple_norm_g': None, 'w_ple_gate': 1, 'w_ple_proj': 2}
VECTOR_WEIGHTS = ('conv_w', 'conv_b', 'ln_g', 'ln_b')
GROUP_FIRST = ('w_in_a',)
GROUP_REST = ('w_out_a', 'w_kv', 'w_in_b', 'w_out_b', 'w_ple_gate', 'w_ple_proj')
GROUP_SMALL = ('norm_g', 'conv_w', 'conv_b', 'ln_g', 'ln_b', 'kv_norm_g', 'k_norm_g', 'q_norm_g', 'ple_norm_g')


def _pick(n, target, mult):
    t = (min(target, n) // mult) * mult
    while t >= mult:
        if n % t == 0:
            return t
        t -= mult
    return n


def _params(n_grid):
    return pltpu.CompilerParams(dimension_semantics=("arbitrary",) * n_grid, vmem_limit_bytes=VMEM_LIMIT_BYTES)


def _sig(x):
    return 0.5 * jnp.tanh(0.5 * x) + 0.5


def _colsum8(v):
    r, w = v.shape
    return v.reshape(r // V7X_SUBLANES, V7X_SUBLANES, w).sum(axis=0)


def _rows(tm, w, col=0):
    return pl.BlockSpec((tm, w), lambda i: (i, col))


def _const(shape):
    nd = len(shape)
    return pl.BlockSpec(shape, lambda i: (0,) * nd)


def _segsum(v, e_ref):
    hi = v.astype(BF16)
    lo = (v - hi.astype(F32)).astype(BF16)
    e = e_ref[...]
    return jnp.dot(hi, e, preferred_element_type=F32) + jnp.dot(lo, e, preferred_element_type=F32)


MM_TILE = 1024
MM_TILE_K = 2048
MM_TILE_WIDE = 2048
MM_TILE_K_TN = 4096


def _matmul(a, b, mode, name, out_dtype=F32, add=None, ex=None, slot_cols=None, norm_gain=None):
    if mode == 'nn':
        (m, k), (_, n) = a.shape, b.shape
    elif mode == 'nt':
        (m, k), (n, _) = a.shape, b.shape
    else:
        (k, m), (_, n) = a.shape, b.shape
    out_struct = jax.ShapeDtypeStruct((m, n), out_dtype)
    n_slots = 0
    if mode == 'tn':
        tm, tn, tk = _pick(m, MM_TILE, 128), _pick(n, MM_TILE, 128), _pick(k, MM_TILE_K_TN, 128)
        o_spec = pl.BlockSpec((tm, tn), lambda i, j, kk: (i, j))
        if slot_cols is not None:
            assert n == N_DEV * slot_cols
            n_slots = max(s for s in (1, 2, 4, 8) if s == 1 or slot_cols * s <= MM_TILE)
            tn = slot_cols * n_slots
            o_spec = pl.BlockSpec((n_slots, tm, slot_cols), lambda i, j, kk: (j, i, 0))
            out_struct = jax.ShapeDtypeStruct((N_DEV, m, slot_cols), out_dtype)
        grid = (m // tm, n // tn, k // tk)
        a_spec = pl.BlockSpec((tk, tm), lambda i, j, kk: (kk, i))
        b_spec = pl.BlockSpec((tk, tn), lambda i, j, kk: (kk, j))
        dims = (((0,), (0,)), ((), ()))
    else:
        tn_max = MM_TILE_WIDE if k <= MM_TILE else MM_TILE
        tn = _pick(n, tn_max, 128)
        tm_max = MM_TILE_WIDE if (k <= MM_TILE and tn <= MM_TILE and add is None and norm_gain is None) else MM_TILE
        tm, tk = _pick(m, tm_max, 128), _pick(k, MM_TILE_K, 128)
        grid = (n // tn, m // tm, k // tk)
        a_spec = pl.BlockSpec((tm, tk), lambda j, i, kk: (i, kk))
        o_spec = pl.BlockSpec((tm, tn), lambda j, i, kk: (i, j))
        if mode == 'nn':
            b_spec = pl.BlockSpec((tk, tn), lambda j, i, kk: (kk, j))
            dims = (((1,), (0,)), ((), ()))
        else:
            b_spec = pl.BlockSpec((tn, tk), lambda j, i, kk: (j, kk))
            dims = (((1,), (1,)), ((), ()))
    nk = grid[2]
    has_add = add is not None
    has_norm = norm_gain is not None
    assert not has_norm or (tn == n and mode != 'tn')

    def body(*refs):
        a_ref, b_ref = refs[0], refs[1]
        add_ref = refs[2] if has_add else None
        gain_ref = refs[2 + has_add] if has_norm else None
        o_ref = refs[2 + has_add + has_norm]
        norm_ref = refs[3 + has_add + has_norm] if has_norm else None
        part = lax.dot_general(a_ref[...].astype(BF16), b_ref[...].astype(BF16), dims, preferred_element_type=F32)

        def finish(total):
            if has_add:
                total = total + add_ref[...]
            if n_slots:
                for s in range(n_slots):
                    o_ref[s] = total[:, s * slot_cols:(s + 1) * slot_cols].astype(out_dtype)
            else:
                o_ref[...] = total.astype(out_dtype)
            if has_norm:
                y = total * lax.rsqrt(jnp.mean(total * total, axis=-1, keepdims=True) + EPS)
                norm_ref[...] = (y * gain_ref[...]).astype(BF16)

        if nk == 1:
            finish(part)
        else:
            acc_ref = refs[3 + has_add + 2 * has_norm]
            kk = pl.program_id(2)

            @pl.when(kk == 0)
            def _():
                acc_ref[...] = part

            @pl.when(kk > 0)
            def _():
                acc_ref[...] += part

            @pl.when(kk == nk - 1)
            def _():
                finish(acc_ref[...])

    in_specs = [a_spec, b_spec] + ([o_spec] if has_add else [])
    args = [a, b] + ([add] if has_add else [])
    out_specs, out_structs = [o_spec], [out_struct]
    if has_norm:
        in_specs.append(pl.BlockSpec((1, n), lambda j, i, kk: (0, 0)))
        args.append(norm_gain)
        out_specs.append(o_spec)
        out_structs.append(jax.ShapeDtypeStruct((m, n), BF16))
    scratch = [pltpu.VMEM((tm, tn), F32)] if nk > 1 else []
    outs, moved = _hosted_call(body, ex, name, grid, in_specs, out_specs, out_structs, scratch, args)
    out = outs[0] if not has_norm else tuple(outs)
    return out if ex is None else (out, moved)


def _rmsnorm_fwd(x, gains, name):
    t, d = x.shape
    tm = _pick(t, 512, 8)
    n = len(gains)

    def body(*refs):
        x_ref, g_refs, o_refs = refs[0], refs[1:1 + n], refs[1 + n:]
        xv = x_ref[...]
        y = xv * lax.rsqrt(jnp.mean(xv * xv, axis=-1, keepdims=True) + EPS)
        for g_ref, o_ref in zip(g_refs, o_refs):
            o_ref[...] = (y * g_ref[...]).astype(BF16)

    return pl.pallas_call(
        body, name=name, grid=(t // tm,),
        in_specs=[_rows(tm, d)] + [_const((1, d))] * n,
        out_specs=[_rows(tm, d)] * n,
        out_shape=[jax.ShapeDtypeStruct((t, d), BF16)] * n,
        compiler_params=_params(1),
    )(x, *gains)


def _ple_grads(dx, gl, pp):
    sg = _sig(gl)
    return (dx * pp * sg * (1.0 - sg)).astype(BF16), (dx * sg).astype(BF16)


def _rmsnorm_bwd(x, gains, dys, add, name, ple=None, dx_dtype=BF16):
    t, d = x.shape
    tm = _pick(t, 512, 16)
    n = len(gains)
    n_ple = 0 if ple is None else 2

    def body(*refs):
        x_ref, add_ref = refs[0], refs[1]
        g_refs, dy_refs = refs[2:2 + n], refs[2 + n:2 + 2 * n]
        ple_refs = refs[2 + 2 * n:2 + 2 * n + n_ple]
        outs = refs[2 + 2 * n + n_ple:]
        dx_ref, dg_refs, dple_refs = outs[0], outs[1:1 + n], outs[1 + n:]
        i = pl.program_id(0)
        xv = x_ref[...]
        r = lax.rsqrt(jnp.mean(xv * xv, axis=-1, keepdims=True) + EPS)
        xhat = xv * r
        dx = add_ref[...].astype(F32)
        for g_ref, dy_ref, dg_ref in zip(g_refs, dy_refs, dg_refs):
            dy = dy_ref[...].astype(F32)
            dyg = dy * g_ref[...]
            dx = dx + r * (dyg - xhat * jnp.mean(dyg * xhat, axis=-1, keepdims=True))
            part = _colsum8(dy * xhat)

            @pl.when(i == 0)
            def _():
                dg_ref[...] = part

            @pl.when(i > 0)
            def _():
                dg_ref[...] += part

        dx_ref[...] = dx.astype(dx_dtype)
        if n_ple:
            dple_refs[0][...], dple_refs[1][...] = _ple_grads(dx, ple_refs[0][...].astype(F32),
                                                               ple_refs[1][...].astype(F32))

    outs = pl.pallas_call(
        body, name=name, grid=(t // tm,),
        in_specs=[_rows(tm, d), _rows(tm, d)] + [_const((1, d))] * n + [_rows(tm, d)] * (n + n_ple),
        out_specs=[_rows(tm, d)] + [_const((V7X_SUBLANES, d))] * n + [_rows(tm, d)] * n_ple,
        out_shape=([jax.ShapeDtypeStruct((t, d), dx_dtype)] + [jax.ShapeDtypeStruct((V7X_SUBLANES, d), F32)] * n
                   + [jax.ShapeDtypeStruct((t, d), BF16)] * n_ple),
        compiler_params=_params(1),
    )(x, add, *gains, *dys, *(ple or ()))
    dgs = [o.sum(axis=0) for o in outs[1:1 + n]]
    return (outs[0], dgs) if ple is None else (outs[0], dgs, outs[1 + n], outs[2 + n])


def _ple_norm_fwd(h, gl, pp, gains, name):
    t, d = h.shape
    tm = _pick(t, 512, 16)
    n = len(gains)

    def body(*refs):
        h_ref, gl_ref, pp_ref = refs[:3]
        g_refs, x_ref, o_refs = refs[3:3 + n], refs[3 + n], refs[4 + n:]
        xv = h_ref[...] + _sig(gl_ref[...].astype(F32)) * pp_ref[...].astype(F32)
        x_ref[...] = xv
        y = xv * lax.rsqrt(jnp.mean(xv * xv, axis=-1, keepdims=True) + EPS)
        for g_ref, o_ref in zip(g_refs, o_refs):
            o_ref[...] = (y * g_ref[...]).astype(BF16)

    outs = pl.pallas_call(
        body, name=name, grid=(t // tm,),
        in_specs=[_rows(tm, d)] * 3 + [_const((1, d))] * n, out_specs=[_rows(tm, d)] * (1 + n),
        out_shape=[jax.ShapeDtypeStruct((t, d), F32)] + [jax.ShapeDtypeStruct((t, d), BF16)] * n,
        compiler_params=_params(1),
    )(h, gl, pp, *gains)
    return outs[0], outs[1:]


def _ple_loss(h, gl, pp, target, name):
    t, d = h.shape
    tm = _pick(t, 512, 16)
    inv_d = 1.0 / d

    def body(h_ref, gl_ref, pp_ref, t_ref, dy_ref, dgl_ref, dpp_ref, l_ref):
        i = pl.program_id(0)
        gl, pp = gl_ref[...].astype(F32), pp_ref[...].astype(F32)
        e = h_ref[...] + _sig(gl) * pp - t_ref[...]
        dy = e * inv_d
        dy_ref[...] = dy.astype(BF16)
        dgl_ref[...], dpp_ref[...] = _ple_grads(dy, gl, pp)
        part = _colsum8(e * e) * (0.5 * inv_d)

        @pl.when(i == 0)
        def _():
            l_ref[...] = part

        @pl.when(i > 0)
        def _():
            l_ref[...] += part

    return pl.pallas_call(
        body, name=name, grid=(t // tm,), in_specs=[_rows(tm, d)] * 4,
        out_specs=[_rows(tm, d)] * 3 + [_const((V7X_SUBLANES, d))],
        out_shape=[jax.ShapeDtypeStruct((t, d), BF16), jax.ShapeDtypeStruct((t, d), BF16),
                   jax.ShapeDtypeStruct((t, d), BF16), jax.ShapeDtypeStruct((V7X_SUBLANES, d), F32)],
        compiler_params=_params(1),
    )(h, gl, pp, target)


def _shift_scratch(ts, cc):
    return pltpu.VMEM((V7X_SUBLANES, ts + HALO - V7X_SUBLANES, cc), F32)


def _shifted_copies(sh_ref, win_ref, cs, ts):
    rows = ts + HALO - V7X_SUBLANES
    for s in range(1, V7X_SUBLANES):
        sh_ref[s] = win_ref[pl.ds(s, rows), cs]


def _tap(sh_ref, win_ref, cs, offset, rows, r0):
    s = offset % V7X_SUBLANES
    start = pl.multiple_of(r0 + (offset - s), V7X_SUBLANES)
    if s == 0:
        return win_ref[pl.ds(start, rows), cs]
    return sh_ref[s, pl.ds(start, rows), :]


def _conv_fwd(proj, conv_w, conv_b, ln_g, ln_b, seq, name, ex=None):
    t, c3 = proj.shape
    c = c3 // 3
    ts = _pick(seq, 256, HALO)
    nsb = seq // ts
    cc = _pick(c, 512, V7X_LANES)
    hb = ts // HALO

    def body(a_ref, b_ref, z_ref, ap_ref, bp_ref, w_ref, cb_ref, g_ref, be_ref, m_ref, y_ref, win_ref, sh_ref):
        i = pl.program_id(0)
        first = (i % nsb) == 0
        win_ref[0:HALO, :] = jnp.where(first, 0.0, ap_ref[...] * _sig(bp_ref[...]))
        win_ref[HALO:, :] = a_ref[...] * _sig(b_ref[...])
        for ci in range(c // cc):
            cs = slice(ci * cc, (ci + 1) * cc)
            _shifted_copies(sh_ref, win_ref, cs, ts)

            def out_rows(rb, carry, cs=cs):
                r0 = rb * CONV_ROWS
                acc = jnp.zeros((CONV_ROWS, cc), F32) + cb_ref[:, cs]
                for k in range(CONV_WIDTH):
                    acc = acc + w_ref[k:k + 1, cs] * _tap(sh_ref, win_ref, cs, HALO - (CONV_WIDTH - 1) + k,
                                                           CONV_ROWS, r0)
                y_ref[pl.ds(pl.multiple_of(r0, CONV_ROWS), CONV_ROWS), cs] = acc
                return carry

            lax.fori_loop(0, ts // CONV_ROWS, out_rows, 0, unroll=CONV_UNROLL)
        y = y_ref[...]
        mu = jnp.mean(y, axis=-1, keepdims=True)
        xc = y - mu
        rstd = lax.rsqrt(jnp.mean(xc * xc, axis=-1, keepdims=True) + EPS)
        ln = xc * rstd * g_ref[...] + be_ref[...]
        zz = z_ref[...]
        m_ref[...] = (ln * _sig(ln) * zz * _sig(zz)).astype(BF16)

    halo_a = pl.BlockSpec((HALO, c), lambda i: (jnp.maximum(i * hb - 1, 0), 0))
    halo_b = pl.BlockSpec((HALO, c), lambda i: (jnp.maximum(i * hb - 1, 0), 1))
    (m_act, y), moved = _hosted_call(
        body, ex, name, (t // ts,),
        [_rows(ts, c, 0), _rows(ts, c, 1), _rows(ts, c, 2), halo_a, halo_b,
         _const((CONV_WIDTH, c)), _const((1, c)), _const((1, c)), _const((1, c))],
        [_rows(ts, c), _rows(ts, c)],
        [jax.ShapeDtypeStruct((t, c), BF16), jax.ShapeDtypeStruct((t, c), F32)],
        [pltpu.VMEM((HALO + ts, c), F32), _shift_scratch(ts, cc)],
        (proj, proj, proj, proj, proj, conv_w, conv_b, ln_g, ln_b))
    return m_act, y, moved


def _ln_gate_bwd(dm, y, proj, ln_g, ln_b, name):
    t, c = y.shape
    tm = _pick(t, 256, 8)

    def body(dm_ref, y_ref, z_ref, g_ref, be_ref, dy_ref, dz_ref, dg_ref, db_ref, dcb_ref):
        i = pl.program_id(0)
        yv = y_ref[...]
        mu = jnp.mean(yv, axis=-1, keepdims=True)
        xc = yv - mu
        rstd = lax.rsqrt(jnp.mean(xc * xc, axis=-1, keepdims=True) + EPS)
        xhat = xc * rstd
        g = g_ref[...]
        ln = xhat * g + be_ref[...]
        sl = _sig(ln)
        zz = z_ref[...]
        sz = _sig(zz)
        dmv = dm_ref[...].astype(F32)
        dz_ref[...] = (dmv * (ln * sl) * (sz * (1.0 + zz * (1.0 - sz)))).astype(BF16)
        dln = dmv * (zz * sz) * (sl * (1.0 + ln * (1.0 - sl)))
        dxh = dln * g
        dyv = rstd * (dxh - jnp.mean(dxh, axis=-1, keepdims=True)
                      - xhat * jnp.mean(dxh * xhat, axis=-1, keepdims=True))
        dy_ref[...] = dyv
        parts = (_colsum8(dln * xhat), _colsum8(dln), _colsum8(dyv))

        @pl.when(i == 0)
        def _():
            for ref, part in zip((dg_ref, db_ref, dcb_ref), parts):
                ref[...] = part

        @pl.when(i > 0)
        def _():
            for ref, part in zip((dg_ref, db_ref, dcb_ref), parts):
                ref[...] += part

    acc = jax.ShapeDtypeStruct((V7X_SUBLANES, c), F32)
    outs = pl.pallas_call(
        body, name=name, grid=(t // tm,),
        in_specs=[_rows(tm, c), _rows(tm, c), _rows(tm, c, 2), _const((1, c)), _const((1, c))],
        out_specs=[_rows(tm, c), _rows(tm, c)] + [_const((V7X_SUBLANES, c))] * 3,
        out_shape=[jax.ShapeDtypeStruct((t, c), F32), jax.ShapeDtypeStruct((t, c), BF16), acc, acc, acc],
        compiler_params=_params(1),
    )(dm, y, proj, ln_g, ln_b)
    return outs[0], outs[1], outs[2].sum(axis=0), outs[3].sum(axis=0), outs[4].sum(axis=0)


def _conv_bwd(dy, dz, proj, conv_w, seq, name, ex=None):
    t, c3 = proj.shape
    c = c3 // 3
    ts = _pick(seq, 256, HALO)
    nsb = seq // ts
    cc = _pick(c, 512, V7X_LANES)
    hb = ts // HALO
    last_halo = t // HALO - 1
    back = CONV_WIDTH - 1

    def body(dy_ref, dyn_ref, dz_ref, a_ref, b_ref, ap_ref, bp_ref, w_ref, o_ref, dw_ref, win_ref, dwin_ref,
             sh_ref, dsh_ref):
        i = pl.program_id(0)
        first = (i % nsb) == 0
        last = (i % nsb) == nsb - 1
        win_ref[0:HALO, :] = jnp.where(first, 0.0, ap_ref[...] * _sig(bp_ref[...]))
        win_ref[HALO:, :] = a_ref[...] * _sig(b_ref[...])
        dwin_ref[0:ts, :] = dy_ref[...]
        dwin_ref[ts:, :] = jnp.where(last, 0.0, dyn_ref[...])

        @pl.when(i == 0)
        def _():
            dw_ref[...] = jnp.zeros_like(dw_ref)

        for ci in range(c // cc):
            cs = slice(ci * cc, (ci + 1) * cc)
            _shifted_copies(sh_ref, win_ref, cs, ts)
            _shifted_copies(dsh_ref, dwin_ref, cs, ts)

            def in_grad_rows(rb, carry, cs=cs, ci=ci):
                r0 = rb * CONV_ROWS
                rows = pl.ds(pl.multiple_of(r0, CONV_ROWS), CONV_ROWS)
                dglu = jnp.zeros((CONV_ROWS, cc), F32)
                for k in range(CONV_WIDTH):
                    dglu = dglu + w_ref[k:k + 1, cs] * _tap(dsh_ref, dwin_ref, cs, back - k, CONV_ROWS, r0)
                sbc = _sig(b_ref[rows, cs])
                o_ref[rows, cs] = (dglu * sbc).astype(BF16)
                o_ref[rows, c + ci * cc:c + (ci + 1) * cc] = (dglu * a_ref[rows, cs] * sbc * (1.0 - sbc)).astype(BF16)
                return carry

            def w_grad_rows(rb, carry, cs=cs):
                r0 = rb * CONV_W_ROWS
                dcur = dwin_ref[pl.ds(pl.multiple_of(r0, CONV_W_ROWS), CONV_W_ROWS), cs]
                for k in range(CONV_WIDTH):
                    dw_ref[k * V7X_SUBLANES:(k + 1) * V7X_SUBLANES, cs] += _colsum8(
                        dcur * _tap(sh_ref, win_ref, cs, HALO - back + k, CONV_W_ROWS, r0))
                return carry

            lax.fori_loop(0, ts // CONV_ROWS, in_grad_rows, 0, unroll=CONV_UNROLL)
            lax.fori_loop(0, ts // CONV_W_ROWS, w_grad_rows, 0)
        o_ref[:, 2 * c:] = dz_ref[...]

    halo_next = pl.BlockSpec((HALO, c), lambda i: (jnp.minimum((i + 1) * hb, last_halo), 0))
    halo_a = pl.BlockSpec((HALO, c), lambda i: (jnp.maximum(i * hb - 1, 0), 0))
    halo_b = pl.BlockSpec((HALO, c), lambda i: (jnp.maximum(i * hb - 1, 0), 1))
    (dproj, dw), moved = _hosted_call(
        body, ex, name, (t // ts,),
        [_rows(ts, c), halo_next, _rows(ts, c), _rows(ts, c, 0), _rows(ts, c, 1), halo_a, halo_b,
         _const((CONV_WIDTH, c))],
        [_rows(ts, c3), _const((CONV_WIDTH * V7X_SUBLANES, c))],
        [jax.ShapeDtypeStruct((t, c3), BF16), jax.ShapeDtypeStruct((CONV_WIDTH * V7X_SUBLANES, c), F32)],
        [pltpu.VMEM((HALO + ts, c), F32), pltpu.VMEM((ts + HALO, c), F32),
         _shift_scratch(ts, cc), _shift_scratch(ts, cc)],
        (dy, dy, dz, proj, proj, proj, proj, conv_w))
    return dproj, dw.reshape(CONV_WIDTH, V7X_SUBLANES, c).sum(axis=1), moved


def _rope_tables(seq):
    half = ROPE_DIM // 2
    inv = ROPE_THETA ** (-jnp.arange(half, dtype=F32) * (2.0 / ROPE_DIM))
    ang = jnp.arange(seq).astype(F32)[:, None] * inv[None, :]
    cos, sin = jnp.cos(ang), jnp.sin(ang)
    zeros = jnp.zeros((seq, HEAD_DIM - ROPE_DIM), F32)
    zh = jnp.zeros((seq, half), F32)
    a = jnp.concatenate([cos, cos, zeros + 1.0], axis=1)
    b = jnp.concatenate([zh, sin, zeros], axis=1)
    c = jnp.concatenate([-sin, zh, zeros], axis=1)
    rep = V7X_LANES // HEAD_DIM
    return tuple(jnp.tile(v, (1, rep)) for v in (a, b, c))


def _head_ones(d):
    head = jnp.arange(d) // HEAD_DIM
    return (head[:, None] == head[None, :]).astype(BF16)


def _rope(ch, ta, tb, tc):
    return ta * ch + tb * pltpu.roll(ch, ROPE_DIM // 2, 1) + tc * pltpu.roll(ch, V7X_LANES - ROPE_DIM // 2, 1)


def _rope_t(ch, ta, tb, tc):
    return ta * ch + pltpu.roll(tb * ch, V7X_LANES - ROPE_DIM // 2, 1) + pltpu.roll(tc * ch, ROPE_DIM // 2, 1)


def _norm_rope_bwd(xhat, r, dout, gain, ta, tb, tc, e_ref):
    dxn = _rope_t(dout, ta, tb, tc)
    dxh = dxn * gain
    dx = r * (dxh - xhat * (_segsum(dxh * xhat, e_ref) * (1.0 / HEAD_DIM)))
    return dx, _colsum8(dxn * xhat)


def _norm_rope_rows(dst_ref, src_ref, gain, ta_ref, tb_ref, tc_ref, e_ref, seq, xhat_ref=None, r_ref=None):
    for r0 in range(0, seq, ATTN_PIECE):
        rows = slice(r0, r0 + ATTN_PIECE)
        xv = src_ref[rows, :]
        r = lax.rsqrt(_segsum(xv * xv, e_ref) * (1.0 / HEAD_DIM) + EPS)
        xhat = xv * r
        if xhat_ref is not None:
            xhat_ref[rows, :] = xhat
            r_ref[rows, :] = r
        dst_ref[rows, :] = _rope(xhat * gain, ta_ref[rows, :], tb_ref[rows, :], tc_ref[rows, :])


ATTN_PIECE = 256
ATTN_UNROLL = 16
CHUNK_UNROLL = 8


def _pieces(dil, seq):
    length = seq // dil
    rows = min(length, ATTN_PIECE)
    return [(r + dil * ci * rows, r * length + ci * rows, rows) for r in range(dil) for ci in range(length // rows)]


def _strided(ref, start, rows, dil):
    if dil == 1:
        return ref[pl.ds(start, rows), :]
    return ref[pl.ds(start, rows, stride=dil), :]


def _strided_set(ref, start, rows, dil, val):
    if dil == 1:
        ref[pl.ds(start, rows), :] = val
    else:
        ref[pl.ds(start, rows, stride=dil), :] = val


def _nt(a, b):
    return lax.dot_general(a, b, (((1,), (1,)), ((), ())), preferred_element_type=F32)


def _tn(a, b):
    return lax.dot_general(a, b, (((0,), (0,)), ((), ())), preferred_element_type=F32)


def _set_bias(bias_ref):
    qi = lax.broadcasted_iota(jnp.int32, (2 * SPAN, 2 * SPAN), 0) & (SPAN - 1)
    kj = lax.broadcasted_iota(jnp.int32, (2 * SPAN, 2 * SPAN), 1)
    band = jnp.logical_and(kj >= qi, (kj - SPAN) <= qi)
    bias_ref[1] = jnp.where(band, 0.0, NEG_INF)
    bias_ref[0] = jnp.where(jnp.logical_and(band, kj >= SPAN), 0.0, NEG_INF)


def _block_keys(bias_ref, j, qs, nb):
    if nb == 1:
        return pl.ds(pl.multiple_of(qs + SPAN, SPAN), SPAN), bias_ref[1, :, SPAN:]
    return pl.ds(qs, 2 * SPAN), bias_ref[jnp.minimum(j & (nb - 1), 1)]


def _stack_heads(v, head0):
    zero = jnp.zeros_like(v)
    return jnp.concatenate([jnp.where(head0, v, zero), jnp.where(head0, zero, v)], axis=0)


def _unstack_heads(v2, head0):
    return jnp.where(head0, v2[:SPAN], v2[SPAN:])


def _head_cols(v, lane=0):
    return jnp.concatenate([v[:, lane:lane + 1], v[:, HEAD_DIM + lane:HEAD_DIM + lane + 1]], axis=0)


def _attn_fwd(proj_b, kv, gains, tables, ones, bsz, seq, name):
    t, d4 = proj_b.shape
    d = d4 // 4
    nhp = d // V7X_LANES
    nblk = seq // SPAN
    scale = HEAD_DIM ** -0.5
    n_groups = len(DILATIONS)

    def body(q0_ref, q1_ref, q2_ref, k_ref, v_ref, gate_ref, gain_ref, ta_ref, tb_ref, tc_ref, e_ref,
             o_ref, l_ref, ao_ref, qd, kd, vd, od, ld, on0, on1, on2, ln0, ln1, ln2, kn, qn, bias):
        head0 = lax.broadcasted_iota(jnp.int32, (SPAN, V7X_LANES), 1) < HEAD_DIM

        @pl.when(jnp.logical_and(pl.program_id(0) == 0, pl.program_id(1) == 0))
        def _():
            _set_bias(bias)

        _norm_rope_rows(kn, k_ref, gain_ref[n_groups:n_groups + 1, :], ta_ref, tb_ref, tc_ref, e_ref, seq)
        kd[0:SPAN, :] = jnp.zeros((SPAN, V7X_LANES), BF16)
        vd[0:SPAN, :] = jnp.zeros((SPAN, V7X_LANES), BF16)
        for g, (q_ref, on, ln) in enumerate(((q0_ref, on0, ln0), (q1_ref, on1, ln1), (q2_ref, on2, ln2))):
            dil = DILATIONS[g]
            nb = seq // dil // SPAN
            _norm_rope_rows(qn, q_ref, gain_ref[g:g + 1, :], ta_ref, tb_ref, tc_ref, e_ref, seq)
            for ns, rs, rows in _pieces(dil, seq):
                qd[rs:rs + rows, :] = _strided(qn, ns, rows, dil).astype(BF16)
                kd[SPAN + rs:SPAN + rs + rows, :] = _strided(kn, ns, rows, dil).astype(BF16)
                vd[SPAN + rs:SPAN + rs + rows, :] = _strided(v_ref, ns, rows, dil).astype(BF16)

            def block(j, carry):
                qs = pl.multiple_of(j * SPAN, SPAN)
                q2 = _stack_heads(qd[pl.ds(qs, SPAN), :], head0)
                keys, mask = _block_keys(bias, j, qs, nb)
                kk = kd[keys, :]
                vv = vd[keys, :]
                s = _nt(q2, kk) * scale + mask
                mx = jnp.max(s, axis=1, keepdims=True)
                p = jnp.exp(s - mx)
                den = jnp.sum(p, axis=1, keepdims=True)
                o2 = jnp.dot(p.astype(BF16), vv, preferred_element_type=F32) / den
                l2 = jnp.broadcast_to(mx + jnp.log(den), (2 * SPAN, V7X_LANES))
                od[pl.ds(qs, SPAN), :] = _unstack_heads(o2, head0)
                ld[pl.ds(qs, SPAN), :] = _unstack_heads(l2, head0)
                return carry

            lax.fori_loop(0, nblk, block, 0, unroll=ATTN_UNROLL)
            for ns, rs, rows in _pieces(dil, seq):
                _strided_set(on, ns, rows, dil, od[rs:rs + rows, :])
                _strided_set(ln, ns, rows, dil, ld[rs:rs + rows, :])

        def merge(ci, carry):
            rows = pl.ds(pl.multiple_of(ci * ATTN_PIECE, ATTN_PIECE), ATTN_PIECE)
            ls = [ln0[rows, :], ln1[rows, :], ln2[rows, :]]
            mx = jnp.maximum(jnp.maximum(ls[0], ls[1]), ls[2])
            es = [jnp.exp(v - mx) for v in ls]
            den = es[0] + es[1] + es[2]
            ov = (es[0] * on0[rows, :] + es[1] * on1[rows, :] + es[2] * on2[rows, :]) / den
            gate = gate_ref[rows, :]
            o_ref[rows, :] = ov
            l_ref[rows, :] = mx + jnp.log(den)
            ao_ref[rows, :] = (ov * gate * _sig(gate)).astype(BF16)
            return carry

        lax.fori_loop(0, seq // ATTN_PIECE, merge, 0)

    blk = (None, seq, V7X_LANES)
    pview = proj_b.reshape(bsz, seq, d4)
    kview = kv.reshape(bsz, seq, 2 * d)
    out_spec = pl.BlockSpec(blk, lambda b, h: (b, 0, h))
    tab = pl.BlockSpec((seq, V7X_LANES), lambda b, h: (0, 0))
    nat = pltpu.VMEM((seq, V7X_LANES), F32)
    o, lse, ao = pl.pallas_call(
        body, name=name, grid=(bsz, nhp),
        in_specs=[pl.BlockSpec(blk, lambda b, h: (b, 0, h)),
                  pl.BlockSpec(blk, lambda b, h: (b, 0, nhp + h)),
                  pl.BlockSpec(blk, lambda b, h: (b, 0, 2 * nhp + h)),
                  pl.BlockSpec(blk, lambda b, h: (b, 0, h)),
                  pl.BlockSpec(blk, lambda b, h: (b, 0, nhp + h)),
                  pl.BlockSpec(blk, lambda b, h: (b, 0, 3 * nhp + h)),
                  pl.BlockSpec((n_groups + 1, V7X_LANES), lambda b, h: (0, 0)),
                  tab, tab, tab,
                  pl.BlockSpec((V7X_LANES, V7X_LANES), lambda b, h: (0, 0))],
        out_specs=[out_spec, out_spec, out_spec],
        out_shape=[jax.ShapeDtypeStruct((bsz, seq, d), F32), jax.ShapeDtypeStruct((bsz, seq, d), F32),
                   jax.ShapeDtypeStruct((bsz, seq, d), BF16)],
        scratch_shapes=[pltpu.VMEM((seq, V7X_LANES), BF16), pltpu.VMEM((SPAN + seq, V7X_LANES), BF16),
                        pltpu.VMEM((SPAN + seq, V7X_LANES), BF16), nat, nat, nat, nat, nat, nat, nat, nat, nat, nat,
                        pltpu.VMEM((2, 2 * SPAN, 2 * SPAN), F32)],
        compiler_params=_params(2),
    )(pview, pview, pview, kview, kview, pview, gains, *tables, ones)
    return o.reshape(t, d), lse.reshape(t, d), ao.reshape(t, d)


def _attn_bwd(proj_b, kv, dao, o, lse, gains, tables, ones, bsz, seq, name):
    t, d4 = proj_b.shape
    d = d4 // 4
    nhp = d // V7X_LANES
    nblk = seq // SPAN
    scale = HEAD_DIM ** -0.5
    n_groups = len(DILATIONS)
    n_chunks = seq // ATTN_PIECE

    def body(q_ref, k_ref, v_ref, gate_ref, dao_ref, o_ref, l_ref, gain_ref, ta_ref, tb_ref, tc_ref, e_ref,
             dproj_ref, dkv_ref, dg_ref, qd, kd, vd, dod, std, dqd, dkd, dvd, dqn, dk0, dk1, dk2, dv0, dv1, dv2,
             kn, kxh, krr, qn, qxh, qrr, don, stn, bias):
        head0 = lax.broadcasted_iota(jnp.int32, (SPAN, V7X_LANES), 1) < HEAD_DIM
        g = pl.program_id(2)

        @pl.when(jnp.logical_and(jnp.logical_and(pl.program_id(0) == 0, pl.program_id(1) == 0), g == 0))
        def _():
            _set_bias(bias)
            dg_ref[...] = jnp.zeros_like(dg_ref)

        @pl.when(g == 0)
        def _():
            first_half = (lax.broadcasted_iota(jnp.int32, (ATTN_PIECE, V7X_LANES), 1) & (HEAD_DIM - 1)) < HEAD_DIM // 2
            _norm_rope_rows(kn, k_ref, gain_ref[n_groups:n_groups + 1, :], ta_ref, tb_ref, tc_ref, e_ref, seq,
                            kxh, krr)
            for r0 in range(0, seq, ATTN_PIECE):
                rows = slice(r0, r0 + ATTN_PIECE)
                gate = gate_ref[rows, :]
                dov = dao_ref[rows, :].astype(F32) * gate * _sig(gate)
                don[rows, :] = dov
                stn[rows, :] = jnp.where(first_half, l_ref[rows, :], _segsum(dov * o_ref[rows, :], e_ref))

        def norm_bwd_chunks(xhat_ref, r_ref, dn_refs, out_ref, gi):
            def chunk(ci, carry):
                rows = pl.ds(pl.multiple_of(ci * ATTN_PIECE, ATTN_PIECE), ATTN_PIECE)
                dn = functools.reduce(lambda u, w: u + w, [r_[rows, :] for r_ in dn_refs])
                dx, part = _norm_rope_bwd(xhat_ref[rows, :], r_ref[rows, :], dn, gain_ref[gi:gi + 1, :],
                                          ta_ref[rows, :], tb_ref[rows, :], tc_ref[rows, :], e_ref)
                out_ref[rows, :] = dx.astype(BF16)
                dg_ref[gi] += part
                return carry
            lax.fori_loop(0, n_chunks, chunk, 0, unroll=CHUNK_UNROLL)

        def group(gi):
            dil = DILATIONS[gi]
            nb = seq // dil // SPAN
            kd[0:SPAN, :] = jnp.zeros((SPAN, V7X_LANES), BF16)
            vd[0:SPAN, :] = jnp.zeros((SPAN, V7X_LANES), BF16)
            dkd[...] = jnp.zeros_like(dkd)
            dvd[...] = jnp.zeros_like(dvd)
            _norm_rope_rows(qn, q_ref, gain_ref[gi:gi + 1, :], ta_ref, tb_ref, tc_ref, e_ref, seq, qxh, qrr)
            for ns, rs, rows in _pieces(dil, seq):
                qd[rs:rs + rows, :] = _strided(qn, ns, rows, dil).astype(BF16)
                kd[SPAN + rs:SPAN + rs + rows, :] = _strided(kn, ns, rows, dil).astype(BF16)
                vd[SPAN + rs:SPAN + rs + rows, :] = _strided(v_ref, ns, rows, dil).astype(BF16)
                dod[rs:rs + rows, :] = _strided(don, ns, rows, dil).astype(BF16)
                std[rs:rs + rows, :] = _strided(stn, ns, rows, dil)

            def block(j, carry):
                qs = pl.multiple_of(j * SPAN, SPAN)
                q2 = _stack_heads(qd[pl.ds(qs, SPAN), :], head0)
                do2 = _stack_heads(dod[pl.ds(qs, SPAN), :], head0)
                keys, mask = _block_keys(bias, j, qs, nb)
                kk = kd[keys, :]
                vv = vd[keys, :]
                s = _nt(q2, kk) * scale + mask
                stv = std[pl.ds(qs, SPAN), :]
                p = jnp.exp(s - _head_cols(stv))
                ds = (p * (_nt(do2, vv) - _head_cols(stv, HEAD_DIM // 2)) * scale).astype(BF16)
                dqd[pl.ds(qs, SPAN), :] = _unstack_heads(jnp.dot(ds, kk, preferred_element_type=F32), head0)
                dkd[keys, :] += _tn(ds, q2)
                dvd[keys, :] += _tn(p.astype(BF16), do2)
                return carry

            lax.fori_loop(0, nblk, block, 0, unroll=ATTN_UNROLL)
            for ns, rs, rows in _pieces(dil, seq):
                _strided_set(dqn, ns, rows, dil, dqd[rs:rs + rows, :])
                _strided_set((dk0, dk1, dk2)[gi], ns, rows, dil, dkd[SPAN + rs:SPAN + rs + rows, :])
                _strided_set((dv0, dv1, dv2)[gi], ns, rows, dil, dvd[SPAN + rs:SPAN + rs + rows, :])
            norm_bwd_chunks(qxh, qrr, [dqn], dproj_ref, gi)

        for gi in range(n_groups):
            @pl.when(g == gi)
            def _():
                group(gi)

        @pl.when(g == n_groups - 1)
        def _():
            norm_bwd_chunks(kxh, krr, [dk0, dk1, dk2], dkv_ref, n_groups)

        @pl.when(g == n_groups)
        def _():
            def chunk(ci, carry):
                rows = pl.ds(pl.multiple_of(ci * ATTN_PIECE, ATTN_PIECE), ATTN_PIECE)
                gate = gate_ref[rows, :]
                sg = _sig(gate)
                dproj_ref[rows, :] = (dao_ref[rows, :].astype(F32) * o_ref[rows, :]
                                      * (sg * (1.0 + gate * (1.0 - sg)))).astype(BF16)
                dkv_ref[rows, :] = (dv0[rows, :] + dv1[rows, :] + dv2[rows, :]).astype(BF16)
                return carry
            lax.fori_loop(0, n_chunks, chunk, 0, unroll=CHUNK_UNROLL)

    blk = (None, seq, V7X_LANES)
    pview = proj_b.reshape(bsz, seq, d4)
    kview = kv.reshape(bsz, seq, 2 * d)
    dview = (bsz, seq, d)
    d_spec = pl.BlockSpec(blk, lambda b, h, g: (b, 0, h))
    tab = pl.BlockSpec((seq, V7X_LANES), lambda b, h, g: (0, 0))
    nat = pltpu.VMEM((seq, V7X_LANES), F32)
    natb = pltpu.VMEM((seq, V7X_LANES), BF16)
    pad = pltpu.VMEM((SPAN + seq, V7X_LANES), F32)
    padb = pltpu.VMEM((SPAN + seq, V7X_LANES), BF16)
    dproj, dkv, dg = pl.pallas_call(
        body, name=name, grid=(bsz, nhp, n_groups + 1),
        in_specs=[pl.BlockSpec(blk, lambda b, h, g: (b, 0, jnp.minimum(g, n_groups - 1) * nhp + h)),
                  pl.BlockSpec(blk, lambda b, h, g: (b, 0, h)),
                  pl.BlockSpec(blk, lambda b, h, g: (b, 0, nhp + h)),
                  pl.BlockSpec(blk, lambda b, h, g: (b, 0, n_groups * nhp + h)),
                  d_spec, d_spec, d_spec,
                  pl.BlockSpec((n_groups + 1, V7X_LANES), lambda b, h, g: (0, 0)),
                  tab, tab, tab,
                  pl.BlockSpec((V7X_LANES, V7X_LANES), lambda b, h, g: (0, 0))],
        out_specs=[pl.BlockSpec(blk, lambda b, h, g: (b, 0, g * nhp + h)),
                   pl.BlockSpec(blk, lambda b, h, g: (b, 0, (g // n_groups) * nhp + h)),
                   pl.BlockSpec((n_groups + 1, V7X_SUBLANES, V7X_LANES), lambda b, h, g: (0, 0, 0))],
        out_shape=[jax.ShapeDtypeStruct((bsz, seq, d4), BF16), jax.ShapeDtypeStruct((bsz, seq, 2 * d), BF16),
                   jax.ShapeDtypeStruct((n_groups + 1, V7X_SUBLANES, V7X_LANES), F32)],
        scratch_shapes=[natb, padb, padb, natb, nat, nat, pad, pad] + [nat] * 15 + [
                        pltpu.VMEM((2, 2 * SPAN, 2 * SPAN), F32)],
        compiler_params=_params(3),
    )(pview, kview, kview, pview, dao.reshape(dview), o.reshape(dview), lse.reshape(dview), gains, *tables, ones)
    dgain = dg.sum(axis=1).reshape(n_groups + 1, V7X_LANES // HEAD_DIM, HEAD_DIM).sum(axis=1)
    return dproj.reshape(t, d4), dkv.reshape(t, 2 * d), dgain


def _mesh_position():
    x, y, c = lax.axis_index("x"), lax.axis_index("y"), lax.axis_index("c")
    return x, y, c


def _peer(x, y, c, rel):
    return (1 - x if rel & 4 else x, 1 - y if rel & 2 else y, 1 - c if rel & 1 else c)


class _Exchange:
    def __init__(self, srcs, gather):
        self.srcs = list(srcs)
        self.gather = gather
        n = self.n = len(self.srcs)
        hbm = pl.BlockSpec(memory_space=pltpu.HBM)
        self.in_specs = [hbm] * n
        self.out_specs = [hbm] * n
        self.out_shape = [jax.ShapeDtypeStruct(((N_DEV,) + a.shape) if gather else a.shape, a.dtype)
                          for a in self.srcs]
        self.scratch = [pltpu.SemaphoreType.DMA((n * (N_DEV - 1),)), pltpu.SemaphoreType.DMA((n * (N_DEV - 1),)),
                        pltpu.SemaphoreType.DMA((n,))]

    def _copies(self, ins, outs, sems):
        send_sems, recv_sems, local_sems = sems
        x, y, c = _mesh_position()
        me = 4 * x + 2 * y + c
        remote, local = [], []
        for a in range(self.n):
            mine = ins[a] if self.gather else ins[a].at[me]
            local.append(pltpu.make_async_copy(mine, outs[a].at[me], local_sems.at[a]))
            for rel in range(1, N_DEV):
                px, py, pc = _peer(x, y, c, rel)
                s = a * (N_DEV - 1) + rel - 1
                src = ins[a] if self.gather else ins[a].at[4 * px + 2 * py + pc]
                remote.append(pltpu.make_async_remote_copy(
                    src_ref=src, dst_ref=outs[a].at[me], send_sem=send_sems.at[s], recv_sem=recv_sems.at[s],
                    device_id=(px, py, pc), device_id_type=pl.DeviceIdType.MESH))
        return remote, local

    def start(self, ins, outs, sems):
        remote, local = self._copies(ins, outs, sems)
        for cp in local + remote:
            cp.start()

    def wait(self, ins, outs, sems):
        remote, local = self._copies(ins, outs, sems)
        for cp in remote:
            cp.wait_recv()
        for cp in remote:
            cp.wait_send()
        for cp in local:
            cp.wait()


def _gather_chip_once(arrs, name):
    n = len(arrs)
    per = N_DEV - 1

    def body(*refs):
        ins, outs = refs[:n], refs[n:2 * n]
        send_sems, recv_sems, local_sems = refs[2 * n:]
        x, y, c = _mesh_position()
        me, sibling = (x, y, c), (x, y, 1 - c)
        chips = [(1 - x, y), (x, 1 - y), (1 - x, 1 - y)]

        def copy(a, k, block, to, src=None):
            bx, by, bc = block
            dst = outs[a].at[4 * bx + 2 * by + bc]
            return pltpu.make_async_remote_copy(
                src_ref=dst if src is None else src, dst_ref=dst, send_sem=send_sems.at[a * per + k],
                recv_sem=recv_sems.at[a * per + k], device_id=to, device_id_type=pl.DeviceIdType.MESH)

        local, sent = [], []
        for a in range(n):
            mine = pltpu.make_async_copy(ins[a], outs[a].at[4 * x + 2 * y + c], local_sems.at[a])
            mine.start()
            local.append(mine)
            first = [copy(a, 0, me, sibling, src=ins[a])]
            first += [copy(a, 1 + j, me, chip + (c,), src=ins[a]) for j, chip in enumerate(chips)]
            for cp in first:
                cp.start()
            sent += first
        for a in range(n):
            for j, chip in enumerate(chips):
                copy(a, 1 + j, chip + (c,), me).wait_recv()
                passed = copy(a, 4 + j, chip + (c,), sibling)
                passed.start()
                sent.append(passed)
        for a in range(n):
            copy(a, 0, sibling, me).wait_recv()
            for j, chip in enumerate(chips):
                copy(a, 4 + j, chip + (1 - c,), me).wait_recv()
        for cp in sent:
            cp.wait_send()
        for cp in local:
            cp.wait()

    hbm = pl.BlockSpec(memory_space=pltpu.HBM)
    return pl.pallas_call(
        body, name=name, in_specs=[hbm] * n, out_specs=[hbm] * n,
        out_shape=[jax.ShapeDtypeStruct((N_DEV,) + a.shape, a.dtype) for a in arrs],
        scratch_shapes=[pltpu.SemaphoreType.DMA((n * per,)), pltpu.SemaphoreType.DMA((n * per,)),
                        pltpu.SemaphoreType.DMA((n,))],
    )(*arrs)


def _run_exchange(ex, name):
    n = ex.n

    def body(*refs):
        ins, outs, sems = refs[:n], refs[n:2 * n], refs[2 * n:]
        ex.start(ins, outs, sems)
        ex.wait(ins, outs, sems)

    return pl.pallas_call(body, name=name, in_specs=ex.in_specs, out_specs=ex.out_specs, out_shape=ex.out_shape,
                          scratch_shapes=ex.scratch)(*ex.srcs)


def _hosted_call(body, ex, name, grid, in_specs, out_specs, out_shape, scratch_shapes, args):
    if ex is None:
        outs = pl.pallas_call(body, name=name, grid=grid, in_specs=in_specs, out_specs=out_specs, out_shape=out_shape,
                              scratch_shapes=scratch_shapes, compiler_params=_params(len(grid)))(*args)
        return list(outs), []
    n_in, n_out, n_scr, n = len(in_specs), len(out_specs), len(scratch_shapes), ex.n

    def hosted(*refs):
        h_in, e_in = refs[:n_in], refs[n_in:n_in + n]
        o0 = n_in + n
        h_out, e_out = refs[o0:o0 + n_out], refs[o0 + n_out:o0 + n_out + n]
        s0 = o0 + n_out + n
        h_scr, e_scr = refs[s0:s0 + n_scr], refs[s0 + n_scr:]
        ids = [pl.program_id(a) for a in range(len(grid))]
        first = functools.reduce(jnp.logical_and, [i == 0 for i in ids])
        last = functools.reduce(jnp.logical_and, [i == g - 1 for i, g in zip(ids, grid)])

        @pl.when(first)
        def _():
            ex.start(e_in, e_out, e_scr)

        body(*h_in, *h_out, *h_scr)

        @pl.when(last)
        def _():
            ex.wait(e_in, e_out, e_scr)

    outs = pl.pallas_call(
        hosted, name=name, grid=grid, in_specs=list(in_specs) + ex.in_specs,
        out_specs=list(out_specs) + ex.out_specs, out_shape=list(out_shape) + ex.out_shape,
        scratch_shapes=list(scratch_shapes) + ex.scratch, compiler_params=_params(len(grid)),
    )(*args, *ex.srcs)
    return list(outs[:n_out]), list(outs[n_out:])


def _sum_adamw(parts, w, m, v, name):
    _, r, wd = parts.shape
    tr = _pick(r, ADAM_ROWS, 8)
    c1 = 1.0 - ADAM_B1 ** ADAM_STEP
    c2 = 1.0 - ADAM_B2 ** ADAM_STEP

    def body(p_ref, w_ref, m_ref, v_ref, g_ref, d_ref, nm_ref, nv_ref):
        g = p_ref[0].astype(F32)
        for s in range(1, N_DEV):
            g = g + p_ref[s].astype(F32)
        nm = ADAM_B1 * m_ref[...] + (1.0 - ADAM_B1) * g
        nv = ADAM_B2 * v_ref[...] + (1.0 - ADAM_B2) * (g * g)
        g_ref[...] = g
        nm_ref[...] = nm
        nv_ref[...] = nv
        d_ref[...] = -ADAM_LR * ((nm / c1) / (jnp.sqrt(nv / c2) + ADAM_EPS) + ADAM_WD * w_ref[...])

    row = pl.BlockSpec((tr, wd), lambda i: (i, 0))
    return pl.pallas_call(
        body, name=name, grid=(r // tr,),
        in_specs=[pl.BlockSpec((N_DEV, tr, wd), lambda i: (0, i, 0)), row, row, row],
        out_specs=[row] * 4, out_shape=[jax.ShapeDtypeStruct((r, wd), F32)] * 4,
        compiler_params=_params(1),
    )(parts, w, m, v)


def _pack_rows(size, row_mult):
    rows = -(-size // PACK_LANES)
    return -(-rows // row_mult) * row_mult


def _pack(flats, row_mult, dtype, total_mult=None):
    out = []
    for f in flats:
        size = f.shape[-1]
        rows = _pack_rows(size, row_mult)
        pad = [(0, 0)] * (f.ndim - 1) + [(0, rows * PACK_LANES - size)]
        out.append(jnp.pad(f.astype(dtype), pad).reshape(f.shape[:-1] + (rows, PACK_LANES)))
    if total_mult is not None:
        total = sum(o.shape[-2] for o in out)
        extra = -(-total // total_mult) * total_mult - total
        if extra:
            out.append(jnp.zeros(out[0].shape[:-2] + (extra, PACK_LANES), dtype))
    return jnp.concatenate(out, axis=-2)


def _unpack(buf, sizes, row_mult):
    out, row = [], 0
    for size in sizes:
        rows = _pack_rows(size, row_mult)
        part = buf[..., row:row + rows, :]
        out.append(part.reshape(buf.shape[:-2] + (rows * PACK_LANES,))[..., :size])
        row += rows
    return out


def _to_slots(full, axis):
    if axis is None:
        return jnp.broadcast_to(full.reshape(1, -1), (N_DEV, full.size))
    shape = full.shape
    split = full.reshape(shape[:axis] + (N_DEV, shape[axis] // N_DEV) + shape[axis + 1:])
    return jnp.moveaxis(split, axis, 0).reshape(N_DEV, -1)


def _from_slots(slots, axis, block_shape):
    split = jnp.moveaxis(slots, 0, axis)
    shape = list(block_shape)
    shape[axis] *= N_DEV
    return split.reshape(shape)


def kernel(x, p, norm_g, w_in_a, conv_w, conv_b, ln_g, ln_b, w_out_a, kv_norm_g, w_kv, k_norm_g, w_in_b, q_norm_g, w_out_b, ple_norm_g, w_ple_gate, w_ple_proj, loss_target, m_norm_g, m_w_in_a, m_conv_w, m_conv_b, m_ln_g, m_ln_b, m_w_out_a, m_kv_norm_g, m_w_kv, m_k_norm_g, m_w_in_b, m_q_norm_g, m_w_out_b, m_ple_norm_g, m_w_ple_gate, m_w_ple_proj, v_norm_g, v_w_in_a, v_conv_w, v_conv_b, v_ln_g, v_ln_b, v_w_out_a, v_kv_norm_g, v_w_kv, v_k_norm_g, v_w_in_b, v_q_norm_g, v_w_out_b, v_ple_norm_g, v_w_ple_gate, v_w_ple_proj):
    weights = dict(zip(WEIGHT_NAMES, (norm_g, w_in_a, conv_w, conv_b, ln_g, ln_b, w_out_a, kv_norm_g, w_kv, k_norm_g,
                                      w_in_b, q_norm_g, w_out_b, ple_norm_g, w_ple_gate, w_ple_proj)))
    mom_m = dict(zip(WEIGHT_NAMES, (m_norm_g, m_w_in_a, m_conv_w, m_conv_b, m_ln_g, m_ln_b, m_w_out_a, m_kv_norm_g,
                                    m_w_kv, m_k_norm_g, m_w_in_b, m_q_norm_g, m_w_out_b, m_ple_norm_g, m_w_ple_gate,
                                    m_w_ple_proj)))
    mom_v = dict(zip(WEIGHT_NAMES, (v_norm_g, v_w_in_a, v_conv_w, v_conv_b, v_ln_g, v_ln_b, v_w_out_a, v_kv_norm_g,
                                    v_w_kv, v_k_norm_g, v_w_in_b, v_q_norm_g, v_w_out_b, v_ple_norm_g, v_w_ple_gate,
                                    v_w_ple_proj)))
    bsz, seq, d = x.shape
    t = bsz * seq
    assert seq % (max(DILATIONS) * SPAN) == 0 and d % V7X_LANES == 0

    full = {}

    def rows2d(a):
        return a.reshape(-1, a.shape[-1])

    def packed(source, names, dtype, total_mult=None):
        return _pack([source[n].reshape(-1) for n in names], 16, dtype, total_mult)

    def gathered(names, bufs):
        for n, buf in zip(names, bufs):
            full[n] = _from_slots(buf.reshape((N_DEV,) + weights[n].shape), SHARD_AXIS[n], weights[n].shape)

    w1_all, wv_all = _gather_chip_once([rows2d(weights['w_in_a']).astype(BF16),
                                        _pack([weights[n].reshape(-1) for n in VECTOR_WEIGHTS], 8, F32)],
                                       "gather_first")
    gathered(GROUP_FIRST, [w1_all])
    for n, slots in zip(VECTOR_WEIGHTS, _unpack(wv_all, [weights[n].size for n in VECTOR_WEIGHTS], 8)):
        full[n] = _from_slots(slots.reshape((N_DEV,) + weights[n].shape), SHARD_AXIS[n], weights[n].shape)
    gather_rest = _Exchange([rows2d(weights[n]).astype(BF16) for n in GROUP_REST], gather=True)
    wa_in = full['w_in_a'][0]
    cw, cb, lg, lb = full['conv_w'][0], full['conv_b'], full['ln_g'], full['ln_b']

    tables = _rope_tables(seq)
    ones = _head_ones(V7X_LANES)
    rep = V7X_LANES // HEAD_DIM
    head_gain = jnp.concatenate([jnp.tile(q_norm_g[0], (1, rep)), jnp.tile(k_norm_g, rep)[None]], axis=0)

    x0 = x.reshape(t, d)
    p0, p1 = p[0].reshape(t, -1), p[1].reshape(t, -1)
    target = loss_target.reshape(t, d)
    g_norm0, g_norm1 = norm_g[0:1], norm_g[1:2]
    g_ple0, g_ple1 = ple_norm_g[0:1], ple_norm_g[1:2]
    g_kv = kv_norm_g.reshape(1, d)

    (u0,) = _rmsnorm_fwd(x0, [g_norm0], "norm0")
    proj_a = _matmul(u0, wa_in, 'nn', "in_a")
    m_act, y_conv, w2_all = _conv_fwd(proj_a, cw, cb, lg, lb, seq, "conv_fwd", ex=gather_rest)
    gathered(GROUP_REST, w2_all)
    wa_out = full['w_out_a'][0]
    wkv = full['w_kv']
    wb_in, wb_out = full['w_in_b'][0], full['w_out_b'][0]
    wg, wp = full['w_ple_gate'], full['w_ple_proj']
    h0, pg0 = _matmul(m_act, wa_out, 'nn', "out_a", add=x0, norm_gain=g_ple0)
    gl0 = _matmul(pg0, wg[0], 'nn', "ple_gate0", out_dtype=BF16)
    pp0 = _matmul(p0, wp[0], 'nn', "ple_proj0", out_dtype=BF16)

    x1, (kvn, u1) = _ple_norm_fwd(h0, gl0, pp0, [g_kv, g_norm1], "ple0_norm1")
    kv = _matmul(kvn, wkv, 'nn', "kv")
    proj_b = _matmul(u1, wb_in, 'nn', "in_b")
    o_att, lse, ao = _attn_fwd(proj_b, kv, head_gain, tables, ones, bsz, seq, "attn_fwd")
    h1, pg1 = _matmul(ao, wb_out, 'nn', "out_b", add=x1, norm_gain=g_ple1)
    gl1 = _matmul(pg1, wg[1], 'nn', "ple_gate1", out_dtype=BF16)
    pp1 = _matmul(p1, wp[1], 'nn', "ple_proj1", out_dtype=BF16)

    dx2, dgl1, dpp1, loss_part = _ple_loss(h1, gl1, pp1, target, "ple1_loss")
    loss = lax.psum(jnp.sum(loss_part), ("x", "y", "c"))

    grads = {}
    slot = {}

    dwp1 = _matmul(p1, dpp1, 'tn', "d_ple_proj1", out_dtype=BF16, slot_cols=d // N_DEV)
    dwg1 = _matmul(pg1, dgl1, 'tn', "d_ple_gate1", out_dtype=BF16)
    dpg1 = _matmul(dgl1, wg[1], 'nt', "d_ple_norm1", out_dtype=BF16)
    dh1, (dg_ple1,) = _rmsnorm_bwd(h1, [g_ple1], [dpg1], dx2, "ple_norm1_bwd")
    slot['w_out_b'] = _matmul(ao, dh1, 'tn', "d_out_b", out_dtype=BF16).reshape(N_DEV, -1, d)
    dao = _matmul(dh1, wb_out, 'nt', "d_ao", out_dtype=BF16)
    dproj_b, dkv, dg_head = _attn_bwd(proj_b, kv, dao, o_att, lse, head_gain, tables, ones, bsz, seq, "attn_bwd")
    slot['w_in_b'] = _matmul(u1, dproj_b, 'tn', "d_in_b", out_dtype=BF16, slot_cols=4 * d // N_DEV)
    du1 = _matmul(dproj_b, wb_in, 'nt', "d_u1", out_dtype=BF16)
    slot['w_kv'] = _matmul(kvn, dkv, 'tn', "d_kv", out_dtype=BF16, slot_cols=2 * d // N_DEV)
    dkvn = _matmul(dkv, wkv, 'nt', "d_kvn", out_dtype=BF16)
    dx1, (dg_kv, dg_norm1), dgl0, dpp0 = _rmsnorm_bwd(x1, [g_kv, g_norm1], [dkvn, du1], dh1, "norm1_bwd",
                                                      ple=(gl0, pp0))

    dwp0 = _matmul(p0, dpp0, 'tn', "d_ple_proj0", out_dtype=BF16, slot_cols=d // N_DEV)
    dwg0 = _matmul(pg0, dgl0, 'tn', "d_ple_gate0", out_dtype=BF16)
    dpg0 = _matmul(dgl0, wg[0], 'nt', "d_ple_norm0", out_dtype=BF16)
    dh0, (dg_ple0,) = _rmsnorm_bwd(h0, [g_ple0], [dpg0], dx1, "ple_norm0_bwd")
    slot['w_out_a'] = _matmul(m_act, dh0, 'tn', "d_out_a", out_dtype=BF16).reshape(N_DEV, -1, d)
    dm = _matmul(dh0, wa_out, 'nt', "d_m", out_dtype=BF16)
    dy_conv, dz, d_lg, d_lb, d_cb = _ln_gate_bwd(dm, y_conv, proj_a, lg, lb, "ln_gate_bwd")
    slot['w_ple_gate'] = jnp.stack([dwg0.reshape(N_DEV, -1, d), dwg1.reshape(N_DEV, -1, d)],
                                   axis=1).reshape(N_DEV, -1, d)
    slot['w_ple_proj'] = jnp.stack([dwp0, dwp1], axis=1).reshape(N_DEV, -1, d // N_DEV)

    dproj_a, d_cw, parts_rest = _conv_bwd(dy_conv, dz, proj_a, cw, seq, "conv_bwd",
                                          ex=_Exchange([slot[n] for n in GROUP_REST], gather=False))
    slot['w_in_a'] = _matmul(u0, dproj_a, 'tn', "d_in_a", out_dtype=BF16, slot_cols=wa_in.shape[1] // N_DEV)
    du0, parts_first = _matmul(dproj_a, wa_in, 'nt', "d_u0", out_dtype=BF16,
                               ex=_Exchange([slot['w_in_a']], gather=False))
    dx0, (dg_norm0,) = _rmsnorm_bwd(x0, [g_norm0], [du0], dh0, "norm0_bwd", dx_dtype=F32)

    grads['norm_g'] = jnp.stack([dg_norm0, dg_norm1])
    grads['conv_w'] = d_cw[None]
    grads['conv_b'] = d_cb[None]
    grads['ln_g'] = d_lg[None]
    grads['ln_b'] = d_lb[None]
    grads['kv_norm_g'] = dg_kv
    grads['k_norm_g'] = dg_head[3]
    grads['q_norm_g'] = dg_head[0:3][None]
    grads['ple_norm_g'] = jnp.stack([dg_ple0, dg_ple1])
    small_pack = _pack([_to_slots(grads[n], SHARD_AXIS[n]) for n in GROUP_SMALL], 16, BF16)
    (parts_small,) = _run_exchange(_Exchange([small_pack], gather=False), "exchange_small")

    updated = {}
    for n, parts in zip(GROUP_REST + GROUP_FIRST, parts_rest + parts_first):
        outs = _sum_adamw(parts, rows2d(weights[n]), rows2d(mom_m[n]), rows2d(mom_v[n]), "sum_adamw_" + n)
        for kind, buf in enumerate(outs):
            updated[kind, n] = buf.reshape(weights[n].shape)
    outs = _sum_adamw(parts_small, packed(weights, GROUP_SMALL, F32), packed(mom_m, GROUP_SMALL, F32),
                      packed(mom_v, GROUP_SMALL, F32), "sum_adamw_small")
    sizes = [weights[n].size for n in GROUP_SMALL]
    for kind, buf in enumerate(outs):
        for n, flat in zip(GROUP_SMALL, _unpack(buf, sizes, 16)):
            updated[kind, n] = flat.reshape(weights[n].shape)
    result = [loss, dx0.reshape(bsz, seq, d)]
    for kind in range(4):
        result.extend(updated[kind, n] for n in WEIGHT_NAMES)
    return tuple(result)
```

```python
import functools

import jax
import jax.numpy as jnp
from jax import lax
from jax.experimental import pallas as pl
from jax.experimental.pallas import tpu as pltpu

F32 = jnp.float32
BF16 = jnp.bfloat16

N_DEV = 8
HEAD_DIM = 64
ROPE_DIM = 16
ROPE_THETA = 500000.0
EPS = 1e-6
NEG_INF = -1e30
SPAN = 128
DILATIONS = (1, 4, 16)
CONV_WIDTH = 31
HALO = 32
CONV_ROWS = 32
CONV_W_ROWS = 64
CONV_UNROLL = 4
CONV_BWD_UNROLL = 8
PACK_LANES = 1024
V7X_LANES = 128
V7X_SUBLANES = 8
VMEM_LIMIT_BYTES = 56 * 1024 * 1024

ADAM_LR = 0.001
ADAM_B1 = 0.9
ADAM_B2 = 0.999
ADAM_EPS = 1e-08
ADAM_WD = 0.01
ADAM_STEP = 10
ADAM_ROWS = 256

WEIGHT_NAMES = ('norm_g', 'w_in_a', 'conv_w', 'conv_b', 'ln_g', 'ln_b', 'w_out_a', 'kv_norm_g', 'w_kv',
                'k_norm_g', 'w_in_b', 'q_norm_g', 'w_out_b', 'ple_norm_g', 'w_ple_gate', 'w_ple_proj')
SHARD_AXIS = {'norm_g': None, 'w_in_a': 2, 'conv_w': 2, 'conv_b': 1, 'ln_g': 1, 'ln_b': 1, 'w_out_a': 1,
              'kv_norm_g': None, 'w_kv': 1, 'k_norm_g': None, 'w_in_b': 2, 'q_norm_g': None, 'w_out_b': 1,
              'ple_norm_g': None, 'w_ple_gate': 1, 'w_ple_proj': 2}
VECTOR_WEIGHTS = ('conv_w', 'conv_b', 'ln_g', 'ln_b')
GROUP_FIRST = ('w_in_a',)
GROUP_REST = ('w_out_a', 'w_kv', 'w_in_b', 'w_out_b', 'w_ple_gate', 'w_ple_proj')
GROUP_SMALL = ('norm_g', 'conv_w', 'conv_b', 'ln_g', 'ln_b', 'kv_norm_g', 'k_norm_g', 'q_norm_g', 'ple_norm_g')


def _pick(n, target, mult):
    t = (min(target, n) // mult) * mult
    while t >= mult:
        if n % t == 0:
            return t
        t -= mult
    return n


def _params(n_grid):
    return pltpu.CompilerParams(dimension_semantics=("arbitrary",) * n_grid, vmem_limit_bytes=VMEM_LIMIT_BYTES)


def _sig(x):
    return 0.5 * jnp.tanh(0.5 * x) + 0.5


def _colsum8(v):
    r, w = v.shape
    return v.reshape(r // V7X_SUBLANES, V7X_SUBLANES, w).sum(axis=0)


def _rows(tm, w, col=0):
    return pl.BlockSpec((tm, w), lambda i: (i, col))


def _const(shape):
    nd = len(shape)
    return pl.BlockSpec(shape, lambda i: (0,) * nd)


def _segsum(v, e_ref):
    hi = v.astype(BF16)
    lo = (v - hi.astype(F32)).astype(BF16)
    e = e_ref[...]
    return jnp.dot(hi, e, preferred_element_type=F32) + jnp.dot(lo, e, preferred_element_type=F32)


MM_TILE = 1024
MM_TILE_K = 2048
MM_TILE_WIDE = 2048
MM_TILE_K_TN = 4096


def _matmul(a, b, mode, name, out_dtype=F32, add=None, ex=None, slot_cols=None, norm_gain=None):
    if mode == 'nn':
        (m, k), (_, n) = a.shape, b.shape
    elif mode == 'nt':
        (m, k), (n, _) = a.shape, b.shape
    else:
        (k, m), (_, n) = a.shape, b.shape
    out_struct = jax.ShapeDtypeStruct((m, n), out_dtype)
    n_slots = 0
    if mode == 'tn':
        tm, tn, tk = _pick(m, MM_TILE, 128), _pick(n, MM_TILE, 128), _pick(k, MM_TILE_K_TN, 128)
        o_spec = pl.BlockSpec((tm, tn), lambda i, j, kk: (i, j))
        if slot_cols is not None:
            assert n == N_DEV * slot_cols
            n_slots = max(s for s in (1, 2, 4, 8) if s == 1 or slot_cols * s <= MM_TILE)
            tn = slot_cols * n_slots
            o_spec = pl.BlockSpec((n_slots, tm, slot_cols), lambda i, j, kk: (j, i, 0))
            out_struct = jax.ShapeDtypeStruct((N_DEV, m, slot_cols), out_dtype)
        grid = (m // tm, n // tn, k // tk)
        a_spec = pl.BlockSpec((tk, tm), lambda i, j, kk: (kk, i))
        b_spec = pl.BlockSpec((tk, tn), lambda i, j, kk: (kk, j))
        dims = (((0,), (0,)), ((), ()))
    else:
        tn_max = MM_TILE_WIDE if k <= MM_TILE else MM_TILE
        tn = _pick(n, tn_max, 128)
        tm_max = MM_TILE_WIDE if (k <= MM_TILE and tn <= MM_TILE and add is None and norm_gain is None) else MM_TILE
        tm, tk = _pick(m, tm_max, 128), _pick(k, MM_TILE_K, 128)
        grid = (n // tn, m // tm, k // tk)
        a_spec = pl.BlockSpec((tm, tk), lambda j, i, kk: (i, kk))
        o_spec = pl.BlockSpec((tm, tn), lambda j, i, kk: (i, j))
        if mode == 'nn':
            b_spec = pl.BlockSpec((tk, tn), lambda j, i, kk: (kk, j))
            dims = (((1,), (0,)), ((), ()))
        else:
            b_spec = pl.BlockSpec((tn, tk), lambda j, i, kk: (j, kk))
            dims = (((1,), (1,)), ((), ()))
    nk = grid[2]
    has_add = add is not None
    has_norm = norm_gain is not None
    assert not has_norm or (tn == n and mode != 'tn')

    def body(*refs):
        a_ref, b_ref = refs[0], refs[1]
        add_ref = refs[2] if has_add else None
        gain_ref = refs[2 + has_add] if has_norm else None
        o_ref = refs[2 + has_add + has_norm]
        norm_ref = refs[3 + has_add + has_norm] if has_norm else None
        part = lax.dot_general(a_ref[...].astype(BF16), b_ref[...].astype(BF16), dims, preferred_element_type=F32)

        def finish(total):
            if has_add:
                total = total + add_ref[...]
            if n_slots:
                for s in range(n_slots):
                    o_ref[s] = total[:, s * slot_cols:(s + 1) * slot_cols].astype(out_dtype)
            else:
                o_ref[...] = total.astype(out_dtype)
            if has_norm:
                y = total * lax.rsqrt(jnp.mean(total * total, axis=-1, keepdims=True) + EPS)
                norm_ref[...] = (y * gain_ref[...]).astype(BF16)

        if nk == 1:
            finish(part)
        else:
            acc_ref = refs[3 + has_add + 2 * has_norm]
            kk = pl.program_id(2)

            @pl.when(kk == 0)
            def _():
                acc_ref[...] = part

            @pl.when(kk > 0)
            def _():
                acc_ref[...] += part

            @pl.when(kk == nk - 1)
            def _():
                finish(acc_ref[...])

    in_specs = [a_spec, b_spec] + ([o_spec] if has_add else [])
    args = [a, b] + ([add] if has_add else [])
    out_specs, out_structs = [o_spec], [out_struct]
    if has_norm:
        in_specs.append(pl.BlockSpec((1, n), lambda j, i, kk: (0, 0)))
        args.append(norm_gain)
        out_specs.append(o_spec)
        out_structs.append(jax.ShapeDtypeStruct((m, n), BF16))
    scratch = [pltpu.VMEM((tm, tn), F32)] if nk > 1 else []
    outs, moved = _hosted_call(body, ex, name, grid, in_specs, out_specs, out_structs, scratch, args)
    out = outs[0] if not has_norm else tuple(outs)
    return out if ex is None else (out, moved)


def _rmsnorm_fwd(x, gains, name):
    t, d = x.shape
    tm = _pick(t, 512, 8)
    n = len(gains)

    def body(*refs):
        x_ref, g_refs, o_refs = refs[0], refs[1:1 + n], refs[1 + n:]
        xv = x_ref[...]
        y = xv * lax.rsqrt(jnp.mean(xv * xv, axis=-1, keepdims=True) + EPS)
        for g_ref, o_ref in zip(g_refs, o_refs):
            o_ref[...] = (y * g_ref[...]).astype(BF16)

    return pl.pallas_call(
        body, name=name, grid=(t // tm,),
        in_specs=[_rows(tm, d)] + [_const((1, d))] * n,
        out_specs=[_rows(tm, d)] * n,
        out_shape=[jax.ShapeDtypeStruct((t, d), BF16)] * n,
        compiler_params=_params(1),
    )(x, *gains)


def _ple_grads(dx, gl, pp):
    sg = _sig(gl)
    return (dx * pp * sg * (1.0 - sg)).astype(BF16), (dx * sg).astype(BF16)


def _rmsnorm_bwd(x, gains, dys, add, name, ple=None, dx_dtype=BF16):
    t, d = x.shape
    tm = _pick(t, 512, 16)
    n = len(gains)
    n_ple = 0 if ple is None else 2

    def body(*refs):
        x_ref, add_ref = refs[0], refs[1]
        g_refs, dy_refs = refs[2:2 + n], refs[2 + n:2 + 2 * n]
        ple_refs = refs[2 + 2 * n:2 + 2 * n + n_ple]
        outs = refs[2 + 2 * n + n_ple:]
        dx_ref, dg_refs, dple_refs = outs[0], outs[1:1 + n], outs[1 + n:]
        i = pl.program_id(0)
        xv = x_ref[...]
        r = lax.rsqrt(jnp.mean(xv * xv, axis=-1, keepdims=True) + EPS)
        xhat = xv * r
        dx = add_ref[...].astype(F32)
        for g_ref, dy_ref, dg_ref in zip(g_refs, dy_refs, dg_refs):
            dy = dy_ref[...].astype(F32)
            dyg = dy * g_ref[...]
            dx = dx + r * (dyg - xhat * jnp.mean(dyg * xhat, axis=-1, keepdims=True))
            part = _colsum8(dy * xhat)

            @pl.when(i == 0)
            def _():
                dg_ref[...] = part

            @pl.when(i > 0)
            def _():
                dg_ref[...] += part

        dx_ref[...] = dx.astype(dx_dtype)
        if n_ple:
            dple_refs[0][...], dple_refs[1][...] = _ple_grads(dx, ple_refs[0][...].astype(F32),
                                                               ple_refs[1][...].astype(F32))

    outs = pl.pallas_call(
        body, name=name, grid=(t // tm,),
        in_specs=[_rows(tm, d), _rows(tm, d)] + [_const((1, d))] * n + [_rows(tm, d)] * (n + n_ple),
        out_specs=[_rows(tm, d)] + [_const((V7X_SUBLANES, d))] * n + [_rows(tm, d)] * n_ple,
        out_shape=([jax.ShapeDtypeStruct((t, d), dx_dtype)] + [jax.ShapeDtypeStruct((V7X_SUBLANES, d), F32)] * n
                   + [jax.ShapeDtypeStruct((t, d), BF16)] * n_ple),
        compiler_params=_params(1),
    )(x, add, *gains, *dys, *(ple or ()))
    dgs = [o.sum(axis=0) for o in outs[1:1 + n]]
    return (outs[0], dgs) if ple is None else (outs[0], dgs, outs[1 + n], outs[2 + n])


def _ple_norm_fwd(h, gl, pp, gains, name):
    t, d = h.shape
    tm = _pick(t, 512, 16)
    n = len(gains)

    def body(*refs):
        h_ref, gl_ref, pp_ref = refs[:3]
        g_refs, x_ref, o_refs = refs[3:3 + n], refs[3 + n], refs[4 + n:]
        xv = h_ref[...] + _sig(gl_ref[...].astype(F32)) * pp_ref[...].astype(F32)
        x_ref[...] = xv
        y = xv * lax.rsqrt(jnp.mean(xv * xv, axis=-1, keepdims=True) + EPS)
        for g_ref, o_ref in zip(g_refs, o_refs):
            o_ref[...] = (y * g_ref[...]).astype(BF16)

    outs = pl.pallas_call(
        body, name=name, grid=(t // tm,),
        in_specs=[_rows(tm, d)] * 3 + [_const((1, d))] * n, out_specs=[_rows(tm, d)] * (1 + n),
        out_shape=[jax.ShapeDtypeStruct((t, d), F32)] + [jax.ShapeDtypeStruct((t, d), BF16)] * n,
        compiler_params=_params(1),
    )(h, gl, pp, *gains)
    return outs[0], outs[1:]


def _ple_loss(h, gl, pp, target, name):
    t, d = h.shape
    tm = _pick(t, 512, 16)
    inv_d = 1.0 / d

    def body(h_ref, gl_ref, pp_ref, t_ref, dy_ref, dgl_ref, dpp_ref, l_ref):
        i = pl.program_id(0)
        gl, pp = gl_ref[...].astype(F32), pp_ref[...].astype(F32)
        e = h_ref[...] + _sig(gl) * pp - t_ref[...]
        dy = e * inv_d
        dy_ref[...] = dy.astype(BF16)
        dgl_ref[...], dpp_ref[...] = _ple_grads(dy, gl, pp)
        part = _colsum8(e * e) * (0.5 * inv_d)

        @pl.when(i == 0)
        def _():
            l_ref[...] = part

        @pl.when(i > 0)
        def _():
            l_ref[...] += part

    return pl.pallas_call(
        body, name=name, grid=(t // tm,), in_specs=[_rows(tm, d)] * 4,
        out_specs=[_rows(tm, d)] * 3 + [_const((V7X_SUBLANES, d))],
        out_shape=[jax.ShapeDtypeStruct((t, d), BF16), jax.ShapeDtypeStruct((t, d), BF16),
                   jax.ShapeDtypeStruct((t, d), BF16), jax.ShapeDtypeStruct((V7X_SUBLANES, d), F32)],
        compiler_params=_params(1),
    )(h, gl, pp, target)


def _shift_scratch(ts, cc):
    return pltpu.VMEM((V7X_SUBLANES, ts + HALO - V7X_SUBLANES, cc), F32)


def _shifted_copies(sh_ref, win_ref, cs, ts):
    rows = ts + HALO - V7X_SUBLANES
    for s in range(1, V7X_SUBLANES):
        sh_ref[s] = win_ref[pl.ds(s, rows), cs]


def _tap(sh_ref, win_ref, cs, offset, rows, r0):
    s = offset % V7X_SUBLANES
    start = pl.multiple_of(r0 + (offset - s), V7X_SUBLANES)
    if s == 0:
        return win_ref[pl.ds(start, rows), cs]
    return sh_ref[s, pl.ds(start, rows), :]


def _conv_fwd(proj, conv_w, conv_b, ln_g, ln_b, seq, name, ex=None):
    t, c3 = proj.shape
    c = c3 // 3
    ts = _pick(seq, 256, HALO)
    nsb = seq // ts
    cc = _pick(c, 512, V7X_LANES)
    hb = ts // HALO

    def body(a_ref, b_ref, z_ref, ap_ref, bp_ref, w_ref, cb_ref, g_ref, be_ref, m_ref, y_ref, win_ref, sh_ref):
        i = pl.program_id(0)
        first = (i % nsb) == 0
        win_ref[0:HALO, :] = jnp.where(first, 0.0, ap_ref[...] * _sig(bp_ref[...]))
        win_ref[HALO:, :] = a_ref[...] * _sig(b_ref[...])
        for ci in range(c // cc):
            cs = slice(ci * cc, (ci + 1) * cc)
            _shifted_copies(sh_ref, win_ref, cs, ts)

            def out_rows(rb, carry, cs=cs):
                r0 = rb * CONV_ROWS
                acc = jnp.zeros((CONV_ROWS, cc), F32) + cb_ref[:, cs]
                for k in range(CONV_WIDTH):
                    acc = acc + w_ref[k:k + 1, cs] * _tap(sh_ref, win_ref, cs, HALO - (CONV_WIDTH - 1) + k,
                                                           CONV_ROWS, r0)
                y_ref[pl.ds(pl.multiple_of(r0, CONV_ROWS), CONV_ROWS), cs] = acc
                return carry

            lax.fori_loop(0, ts // CONV_ROWS, out_rows, 0, unroll=CONV_UNROLL)
        y = y_ref[...]
        mu = jnp.mean(y, axis=-1, keepdims=True)
        xc = y - mu
        rstd = lax.rsqrt(jnp.mean(xc * xc, axis=-1, keepdims=True) + EPS)
        ln = xc * rstd * g_ref[...] + be_ref[...]
        zz = z_ref[...]
        m_ref[...] = (ln * _sig(ln) * zz * _sig(zz)).astype(BF16)

    halo_a = pl.BlockSpec((HALO, c), lambda i: (jnp.maximum(i * hb - 1, 0), 0))
    halo_b = pl.BlockSpec((HALO, c), lambda i: (jnp.maximum(i * hb - 1, 0), 1))
    (m_act, y), moved = _hosted_call(
        body, ex, name, (t // ts,),
        [_rows(ts, c, 0), _rows(ts, c, 1), _rows(ts, c, 2), halo_a, halo_b,
         _const((CONV_WIDTH, c)), _const((1, c)), _const((1, c)), _const((1, c))],
        [_rows(ts, c), _rows(ts, c)],
        [jax.ShapeDtypeStruct((t, c), BF16), jax.ShapeDtypeStruct((t, c), F32)],
        [pltpu.VMEM((HALO + ts, c), F32), _shift_scratch(ts, cc)],
        (proj, proj, proj, proj, proj, conv_w, conv_b, ln_g, ln_b))
    return m_act, y, moved


def _ln_gate_bwd(dm, y, proj, ln_g, ln_b, name):
    t, c = y.shape
    tm = _pick(t, 256, 8)

    def body(dm_ref, y_ref, z_ref, g_ref, be_ref, dy_ref, dz_ref, dg_ref, db_ref, dcb_ref):
        i = pl.program_id(0)
        yv = y_ref[...]
        mu = jnp.mean(yv, axis=-1, keepdims=True)
        xc = yv - mu
        rstd = lax.rsqrt(jnp.mean(xc * xc, axis=-1, keepdims=True) + EPS)
        xhat = xc * rstd
        g = g_ref[...]
        ln = xhat * g + be_ref[...]
        sl = _sig(ln)
        zz = z_ref[...]
        sz = _sig(zz)
        dmv = dm_ref[...].astype(F32)
        dz_ref[...] = (dmv * (ln * sl) * (sz * (1.0 + zz * (1.0 - sz)))).astype(BF16)
        dln = dmv * (zz * sz) * (sl * (1.0 + ln * (1.0 - sl)))
        dxh = dln * g
        dyv = rstd * (dxh - jnp.mean(dxh, axis=-1, keepdims=True)
                      - xhat * jnp.mean(dxh * xhat, axis=-1, keepdims=True))
        dy_ref[...] = dyv
        parts = (_colsum8(dln * xhat), _colsum8(dln), _colsum8(dyv))

        @pl.when(i == 0)
        def _():
            for ref, part in zip((dg_ref, db_ref, dcb_ref), parts):
                ref[...] = part

        @pl.when(i > 0)
        def _():
            for ref, part in zip((dg_ref, db_ref, dcb_ref), parts):
                ref[...] += part

    acc = jax.ShapeDtypeStruct((V7X_SUBLANES, c), F32)
    outs = pl.pallas_call(
        body, name=name, grid=(t // tm,),
        in_specs=[_rows(tm, c), _rows(tm, c), _rows(tm, c, 2), _const((1, c)), _const((1, c))],
        out_specs=[_rows(tm, c), _rows(tm, c)] + [_const((V7X_SUBLANES, c))] * 3,
        out_shape=[jax.ShapeDtypeStruct((t, c), F32), jax.ShapeDtypeStruct((t, c), BF16), acc, acc, acc],
        compiler_params=_params(1),
    )(dm, y, proj, ln_g, ln_b)
    return outs[0], outs[1], outs[2].sum(axis=0), outs[3].sum(axis=0), outs[4].sum(axis=0)


def _conv_bwd(dy, dz, proj, conv_w, seq, name, ex=None):
    t, c3 = proj.shape
    c = c3 // 3
    ts = _pick(seq, 256, HALO)
    nsb = seq // ts
    cc = _pick(c, 512, V7X_LANES)
    hb = ts // HALO
    last_halo = t // HALO - 1
    back = CONV_WIDTH - 1

    def body(dy_ref, dyn_ref, dz_ref, a_ref, b_ref, ap_ref, bp_ref, w_ref, o_ref, dw_ref, win_ref, dwin_ref,
             sh_ref, dsh_ref):
        i = pl.program_id(0)
        first = (i % nsb) == 0
        last = (i % nsb) == nsb - 1
        win_ref[0:HALO, :] = jnp.where(first, 0.0, ap_ref[...] * _sig(bp_ref[...]))
        win_ref[HALO:, :] = a_ref[...] * _sig(b_ref[...])
        dwin_ref[0:ts, :] = dy_ref[...]
        dwin_ref[ts:, :] = jnp.where(last, 0.0, dyn_ref[...])

        @pl.when(i == 0)
        def _():
            dw_ref[...] = jnp.zeros_like(dw_ref)

        for ci in range(c // cc):
            cs = slice(ci * cc, (ci + 1) * cc)
            _shifted_copies(sh_ref, win_ref, cs, ts)
            _shifted_copies(dsh_ref, dwin_ref, cs, ts)

            def in_grad_rows(rb, carry, cs=cs, ci=ci):
                r0 = rb * CONV_ROWS
                rows = pl.ds(pl.multiple_of(r0, CONV_ROWS), CONV_ROWS)
                dglu = jnp.zeros((CONV_ROWS, cc), F32)
                for k in range(CONV_WIDTH):
                    dglu = dglu + w_ref[k:k + 1, cs] * _tap(dsh_ref, dwin_ref, cs, back - k, CONV_ROWS, r0)
                sbc = _sig(b_ref[rows, cs])
                o_ref[rows, cs] = (dglu * sbc).astype(BF16)
                o_ref[rows, c + ci * cc:c + (ci + 1) * cc] = (dglu * a_ref[rows, cs] * sbc * (1.0 - sbc)).astype(BF16)
                return carry

            def w_grad_rows(rb, carry, cs=cs):
                r0 = rb * CONV_W_ROWS
                dcur = dwin_ref[pl.ds(pl.multiple_of(r0, CONV_W_ROWS), CONV_W_ROWS), cs]
                for k in range(CONV_WIDTH):
                    dw_ref[k * V7X_SUBLANES:(k + 1) * V7X_SUBLANES, cs] += _colsum8(
                        dcur * _tap(sh_ref, win_ref, cs, HALO - back + k, CONV_W_ROWS, r0))
                return carry

            lax.fori_loop(0, ts // CONV_ROWS, in_grad_rows, 0, unroll=CONV_BWD_UNROLL)
            lax.fori_loop(0, ts // CONV_W_ROWS, w_grad_rows, 0)
        o_ref[:, 2 * c:] = dz_ref[...]

    halo_next = pl.BlockSpec((HALO, c), lambda i: (jnp.minimum((i + 1) * hb, last_halo), 0))
    halo_a = pl.BlockSpec((HALO, c), lambda i: (jnp.maximum(i * hb - 1, 0), 0))
    halo_b = pl.BlockSpec((HALO, c), lambda i: (jnp.maximum(i * hb - 1, 0), 1))
    (dproj, dw), moved = _hosted_call(
        body, ex, name, (t // ts,),
        [_rows(ts, c), halo_next, _rows(ts, c), _rows(ts, c, 0), _rows(ts, c, 1), halo_a, halo_b,
         _const((CONV_WIDTH, c))],
        [_rows(ts, c3), _const((CONV_WIDTH * V7X_SUBLANES, c))],
        [jax.ShapeDtypeStruct((t, c3), BF16), jax.ShapeDtypeStruct((CONV_WIDTH * V7X_SUBLANES, c), F32)],
        [pltpu.VMEM((HALO + ts, c), F32), pltpu.VMEM((ts + HALO, c), F32),
         _shift_scratch(ts, cc), _shift_scratch(ts, cc)],
        (dy, dy, dz, proj, proj, proj, proj, conv_w))
    return dproj, dw.reshape(CONV_WIDTH, V7X_SUBLANES, c).sum(axis=1), moved


def _rope_tables(seq):
    half = ROPE_DIM // 2
    inv = ROPE_THETA ** (-jnp.arange(half, dtype=F32) * (2.0 / ROPE_DIM))
    ang = jnp.arange(seq).astype(F32)[:, None] * inv[None, :]
    cos, sin = jnp.cos(ang), jnp.sin(ang)
    zeros = jnp.zeros((seq, HEAD_DIM - ROPE_DIM), F32)
    zh = jnp.zeros((seq, half), F32)
    a = jnp.concatenate([cos, cos, zeros + 1.0], axis=1)
    b = jnp.concatenate([zh, sin, zeros], axis=1)
    c = jnp.concatenate([-sin, zh, zeros], axis=1)
    rep = V7X_LANES // HEAD_DIM
    return tuple(jnp.tile(v, (1, rep)) for v in (a, b, c))


def _head_ones(d):
    head = jnp.arange(d) // HEAD_DIM
    return (head[:, None] == head[None, :]).astype(BF16)


def _rope(ch, ta, tb, tc):
    return ta * ch + tb * pltpu.roll(ch, ROPE_DIM // 2, 1) + tc * pltpu.roll(ch, V7X_LANES - ROPE_DIM // 2, 1)


def _rope_t(ch, ta, tb, tc):
    return ta * ch + pltpu.roll(tb * ch, V7X_LANES - ROPE_DIM // 2, 1) + pltpu.roll(tc * ch, ROPE_DIM // 2, 1)


def _norm_rope_bwd(xhat, r, dout, gain, ta, tb, tc, e_ref):
    dxn = _rope_t(dout, ta, tb, tc)
    dxh = dxn * gain
    dx = r * (dxh - xhat * (_segsum(dxh * xhat, e_ref) * (1.0 / HEAD_DIM)))
    return dx, _colsum8(dxn * xhat)


def _norm_rope_rows(dst_ref, src_ref, gain, ta_ref, tb_ref, tc_ref, e_ref, seq, xhat_ref=None, r_ref=None):
    for r0 in range(0, seq, ATTN_PIECE):
        rows = slice(r0, r0 + ATTN_PIECE)
        xv = src_ref[rows, :]
        r = lax.rsqrt(_segsum(xv * xv, e_ref) * (1.0 / HEAD_DIM) + EPS)
        xhat = xv * r
        if xhat_ref is not None:
            xhat_ref[rows, :] = xhat
            r_ref[rows, :] = r
        dst_ref[rows, :] = _rope(xhat * gain, ta_ref[rows, :], tb_ref[rows, :], tc_ref[rows, :])


ATTN_PIECE = 256
ATTN_UNROLL = 16
CHUNK_UNROLL = 8


def _pieces(dil, seq):
    length = seq // dil
    rows = min(length, ATTN_PIECE)
    return [(r + dil * ci * rows, r * length + ci * rows, rows) for r in range(dil) for ci in range(length // rows)]


def _strided(ref, start, rows, dil):
    if dil == 1:
        return ref[pl.ds(start, rows), :]
    return ref[pl.ds(start, rows, stride=dil), :]


def _strided_set(ref, start, rows, dil, val):
    if dil == 1:
        ref[pl.ds(start, rows), :] = val
    else:
        ref[pl.ds(start, rows, stride=dil), :] = val


def _nt(a, b):
    return lax.dot_general(a, b, (((1,), (1,)), ((), ())), preferred_element_type=F32)


def _tn(a, b):
    return lax.dot_general(a, b, (((0,), (0,)), ((), ())), preferred_element_type=F32)


def _set_bias(bias_ref):
    qi = lax.broadcasted_iota(jnp.int32, (2 * SPAN, 2 * SPAN), 0) & (SPAN - 1)
    kj = lax.broadcasted_iota(jnp.int32, (2 * SPAN, 2 * SPAN), 1)
    band = jnp.logical_and(kj >= qi, (kj - SPAN) <= qi)
    bias_ref[1] = jnp.where(band, 0.0, NEG_INF)
    bias_ref[0] = jnp.where(jnp.logical_and(band, kj >= SPAN), 0.0, NEG_INF)


def _block_keys(bias_ref, j, qs, nb):
    if nb == 1:
        return pl.ds(pl.multiple_of(qs + SPAN, SPAN), SPAN), bias_ref[1, :, SPAN:]
    return pl.ds(qs, 2 * SPAN), bias_ref[jnp.minimum(j & (nb - 1), 1)]


def _stack_heads(v, head0):
    zero = jnp.zeros_like(v)
    return jnp.concatenate([jnp.where(head0, v, zero), jnp.where(head0, zero, v)], axis=0)


def _unstack_heads(v2, head0):
    return jnp.where(head0, v2[:SPAN], v2[SPAN:])


def _head_cols(v, lane=0):
    return jnp.concatenate([v[:, lane:lane + 1], v[:, HEAD_DIM + lane:HEAD_DIM + lane + 1]], axis=0)


def _attn_fwd(proj_b, kv, gains, tables, ones, bsz, seq, name):
    t, d4 = proj_b.shape
    d = d4 // 4
    nhp = d // V7X_LANES
    nblk = seq // SPAN
    scale = HEAD_DIM ** -0.5
    n_groups = len(DILATIONS)

    def body(q0_ref, q1_ref, q2_ref, k_ref, v_ref, gate_ref, gain_ref, ta_ref, tb_ref, tc_ref, e_ref,
             o_ref, l_ref, ao_ref, qd, kd, vd, od, ld, on0, on1, on2, ln0, ln1, ln2, kn, qn, bias):
        head0 = lax.broadcasted_iota(jnp.int32, (SPAN, V7X_LANES), 1) < HEAD_DIM

        @pl.when(jnp.logical_and(pl.program_id(0) == 0, pl.program_id(1) == 0))
        def _():
            _set_bias(bias)

        _norm_rope_rows(kn, k_ref, gain_ref[n_groups:n_groups + 1, :], ta_ref, tb_ref, tc_ref, e_ref, seq)
        kd[0:SPAN, :] = jnp.zeros((SPAN, V7X_LANES), BF16)
        vd[0:SPAN, :] = jnp.zeros((SPAN, V7X_LANES), BF16)
        for g, (q_ref, on, ln) in enumerate(((q0_ref, on0, ln0), (q1_ref, on1, ln1), (q2_ref, on2, ln2))):
            dil = DILATIONS[g]
            nb = seq // dil // SPAN
            _norm_rope_rows(qn, q_ref, gain_ref[g:g + 1, :], ta_ref, tb_ref, tc_ref, e_ref, seq)
            for ns, rs, rows in _pieces(dil, seq):
                qd[rs:rs + rows, :] = _strided(qn, ns, rows, dil).astype(BF16)
                kd[SPAN + rs:SPAN + rs + rows, :] = _strided(kn, ns, rows, dil).astype(BF16)
                vd[SPAN + rs:SPAN + rs + rows, :] = _strided(v_ref, ns, rows, dil).astype(BF16)

            def block(j, carry):
                qs = pl.multiple_of(j * SPAN, SPAN)
                q2 = _stack_heads(qd[pl.ds(qs, SPAN), :], head0)
                keys, mask = _block_keys(bias, j, qs, nb)
                kk = kd[keys, :]
                vv = vd[keys, :]
                s = _nt(q2, kk) * scale + mask
                mx = jnp.max(s, axis=1, keepdims=True)
                p = jnp.exp(s - mx)
                den = jnp.sum(p, axis=1, keepdims=True)
                o2 = jnp.dot(p.astype(BF16), vv, preferred_element_type=F32) / den
                l2 = jnp.broadcast_to(mx + jnp.log(den), (2 * SPAN, V7X_LANES))
                od[pl.ds(qs, SPAN), :] = _unstack_heads(o2, head0)
                ld[pl.ds(qs, SPAN), :] = _unstack_heads(l2, head0)
                return carry

            lax.fori_loop(0, nblk, block, 0, unroll=ATTN_UNROLL)
            for ns, rs, rows in _pieces(dil, seq):
                _strided_set(on, ns, rows, dil, od[rs:rs + rows, :])
                _strided_set(ln, ns, rows, dil, ld[rs:rs + rows, :])

        def merge(ci, carry):
            rows = pl.ds(pl.multiple_of(ci * ATTN_PIECE, ATTN_PIECE), ATTN_PIECE)
            ls = [ln0[rows, :], ln1[rows, :], ln2[rows, :]]
            mx = jnp.maximum(jnp.maximum(ls[0], ls[1]), ls[2])
            es = [jnp.exp(v - mx) for v in ls]
            den = es[0] + es[1] + es[2]
            ov = (es[0] * on0[rows, :] + es[1] * on1[rows, :] + es[2] * on2[rows, :]) / den
            gate = gate_ref[rows, :]
            o_ref[rows, :] = ov
            l_ref[rows, :] = mx + jnp.log(den)
            ao_ref[rows, :] = (ov * gate * _sig(gate)).astype(BF16)
            return carry

        lax.fori_loop(0, seq // ATTN_PIECE, merge, 0)

    blk = (None, seq, V7X_LANES)
    pview = proj_b.reshape(bsz, seq, d4)
    kview = kv.reshape(bsz, seq, 2 * d)
    out_spec = pl.BlockSpec(blk, lambda b, h: (b, 0, h))
    tab = pl.BlockSpec((seq, V7X_LANES), lambda b, h: (0, 0))
    nat = pltpu.VMEM((seq, V7X_LANES), F32)
    o, lse, ao = pl.pallas_call(
        body, name=name, grid=(bsz, nhp),
        in_specs=[pl.BlockSpec(blk, lambda b, h: (b, 0, h)),
                  pl.BlockSpec(blk, lambda b, h: (b, 0, nhp + h)),
                  pl.BlockSpec(blk, lambda b, h: (b, 0, 2 * nhp + h)),
                  pl.BlockSpec(blk, lambda b, h: (b, 0, h)),
                  pl.BlockSpec(blk, lambda b, h: (b, 0, nhp + h)),
                  pl.BlockSpec(blk, lambda b, h: (b, 0, 3 * nhp + h)),
                  pl.BlockSpec((n_groups + 1, V7X_LANES), lambda b, h: (0, 0)),
                  tab, tab, tab,
                  pl.BlockSpec((V7X_LANES, V7X_LANES), lambda b, h: (0, 0))],
        out_specs=[out_spec, out_spec, out_spec],
        out_shape=[jax.ShapeDtypeStruct((bsz, seq, d), F32), jax.ShapeDtypeStruct((bsz, seq, d), F32),
                   jax.ShapeDtypeStruct((bsz, seq, d), BF16)],
        scratch_shapes=[pltpu.VMEM((seq, V7X_LANES), BF16), pltpu.VMEM((SPAN + seq, V7X_LANES), BF16),
                        pltpu.VMEM((SPAN + seq, V7X_LANES), BF16), nat, nat, nat, nat, nat, nat, nat, nat, nat, nat,
                        pltpu.VMEM((2, 2 * SPAN, 2 * SPAN), F32)],
        compiler_params=_params(2),
    )(pview, pview, pview, kview, kview, pview, gains, *tables, ones)
    return o.reshape(t, d), lse.reshape(t, d), ao.reshape(t, d)


def _attn_bwd(proj_b, kv, dao, o, lse, gains, tables, ones, bsz, seq, name):
    t, d4 = proj_b.shape
    d = d4 // 4
    nhp = d // V7X_LANES
    nblk = seq // SPAN
    scale = HEAD_DIM ** -0.5
    n_groups = len(DILATIONS)
    n_chunks = seq // ATTN_PIECE

    def body(q_ref, k_ref, v_ref, gate_ref, dao_ref, o_ref, l_ref, gain_ref, ta_ref, tb_ref, tc_ref, e_ref,
             dproj_ref, dkv_ref, dg_ref, qd, kd, vd, dod, std, dqd, dkd, dvd, dqn, dk0, dk1, dk2, dv0, dv1, dv2,
             kn, kxh, krr, qn, qxh, qrr, don, stn, bias):
        head0 = lax.broadcasted_iota(jnp.int32, (SPAN, V7X_LANES), 1) < HEAD_DIM
        g = pl.program_id(2)

        @pl.when(jnp.logical_and(jnp.logical_and(pl.program_id(0) == 0, pl.program_id(1) == 0), g == 0))
        def _():
            _set_bias(bias)
            dg_ref[...] = jnp.zeros_like(dg_ref)

        @pl.when(g == 0)
        def _():
            first_half = (lax.broadcasted_iota(jnp.int32, (ATTN_PIECE, V7X_LANES), 1) & (HEAD_DIM - 1)) < HEAD_DIM // 2
            _norm_rope_rows(kn, k_ref, gain_ref[n_groups:n_groups + 1, :], ta_ref, tb_ref, tc_ref, e_ref, seq,
                            kxh, krr)
            for r0 in range(0, seq, ATTN_PIECE):
                rows = slice(r0, r0 + ATTN_PIECE)
                gate = gate_ref[rows, :]
                dov = dao_ref[rows, :].astype(F32) * gate * _sig(gate)
                don[rows, :] = dov
                stn[rows, :] = jnp.where(first_half, l_ref[rows, :], _segsum(dov * o_ref[rows, :], e_ref))

        def norm_bwd_chunks(xhat_ref, r_ref, dn_refs, out_ref, gi):
            def chunk(ci, carry):
                rows = pl.ds(pl.multiple_of(ci * ATTN_PIECE, ATTN_PIECE), ATTN_PIECE)
                dn = functools.reduce(lambda u, w: u + w, [r_[rows, :] for r_ in dn_refs])
                dx, part = _norm_rope_bwd(xhat_ref[rows, :], r_ref[rows, :], dn, gain_ref[gi:gi + 1, :],
                                          ta_ref[rows, :], tb_ref[rows, :], tc_ref[rows, :], e_ref)
                out_ref[rows, :] = dx.astype(BF16)
                dg_ref[gi] += part
                return carry
            lax.fori_loop(0, n_chunks, chunk, 0, unroll=CHUNK_UNROLL)

        def group(gi):
            dil = DILATIONS[gi]
            nb = seq // dil // SPAN
            kd[0:SPAN, :] = jnp.zeros((SPAN, V7X_LANES), BF16)
            vd[0:SPAN, :] = jnp.zeros((SPAN, V7X_LANES), BF16)
            dkd[...] = jnp.zeros_like(dkd)
            dvd[...] = jnp.zeros_like(dvd)
            _norm_rope_rows(qn, q_ref, gain_ref[gi:gi + 1, :], ta_ref, tb_ref, tc_ref, e_ref, seq, qxh, qrr)
            for ns, rs, rows in _pieces(dil, seq):
                qd[rs:rs + rows, :] = _strided(qn, ns, rows, dil).astype(BF16)
                kd[SPAN + rs:SPAN + rs + rows, :] = _strided(kn, ns, rows, dil).astype(BF16)
                vd[SPAN + rs:SPAN + rs + rows, :] = _strided(v_ref, ns, rows, dil).astype(BF16)
                dod[rs:rs + rows, :] = _strided(don, ns, rows, dil).astype(BF16)
                std[rs:rs + rows, :] = _strided(stn, ns, rows, dil)

            def block(j, carry):
                qs = pl.multiple_of(j * SPAN, SPAN)
                q2 = _stack_heads(qd[pl.ds(qs, SPAN), :], head0)
                do2 = _stack_heads(dod[pl.ds(qs, SPAN), :], head0)
                keys, mask = _block_keys(bias, j, qs, nb)
                kk = kd[keys, :]
                vv = vd[keys, :]
                s = _nt(q2, kk) * scale + mask
                stv = std[pl.ds(qs, SPAN), :]
                p = jnp.exp(s - _head_cols(stv))
                ds = (p * (_nt(do2, vv) - _head_cols(stv, HEAD_DIM // 2)) * scale).astype(BF16)
                dqd[pl.ds(qs, SPAN), :] = _unstack_heads(jnp.dot(ds, kk, preferred_element_type=F32), head0)
                dkd[keys, :] += _tn(ds, q2)
                dvd[keys, :] += _tn(p.astype(BF16), do2)
                return carry

            lax.fori_loop(0, nblk, block, 0, unroll=ATTN_UNROLL)
            for ns, rs, rows in _pieces(dil, seq):
                _strided_set(dqn, ns, rows, dil, dqd[rs:rs + rows, :])
                _strided_set((dk0, dk1, dk2)[gi], ns, rows, dil, dkd[SPAN + rs:SPAN + rs + rows, :])
                _strided_set((dv0, dv1, dv2)[gi], ns, rows, dil, dvd[SPAN + rs:SPAN + rs + rows, :])
            norm_bwd_chunks(qxh, qrr, [dqn], dproj_ref, gi)

        for gi in range(n_groups):
            @pl.when(g == gi)
            def _():
                group(gi)

        @pl.when(g == n_groups - 1)
        def _():
            norm_bwd_chunks(kxh, krr, [dk0, dk1, dk2], dkv_ref, n_groups)

        @pl.when(g == n_groups)
        def _():
            def chunk(ci, carry):
                rows = pl.ds(pl.multiple_of(ci * ATTN_PIECE, ATTN_PIECE), ATTN_PIECE)
                gate = gate_ref[rows, :]
                sg = _sig(gate)
                dproj_ref[rows, :] = (dao_ref[rows, :].astype(F32) * o_ref[rows, :]
                                      * (sg * (1.0 + gate * (1.0 - sg)))).astype(BF16)
                dkv_ref[rows, :] = (dv0[rows, :] + dv1[rows, :] + dv2[rows, :]).astype(BF16)
                return carry
            lax.fori_loop(0, n_chunks, chunk, 0, unroll=CHUNK_UNROLL)

    blk = (None, seq, V7X_LANES)
    pview = proj_b.reshape(bsz, seq, d4)
    kview = kv.reshape(bsz, seq, 2 * d)
    dview = (bsz, seq, d)
    d_spec = pl.BlockSpec(blk, lambda b, h, g: (b, 0, h))
    tab = pl.BlockSpec((seq, V7X_LANES), lambda b, h, g: (0, 0))
    nat = pltpu.VMEM((seq, V7X_LANES), F32)
    natb = pltpu.VMEM((seq, V7X_LANES), BF16)
    pad = pltpu.VMEM((SPAN + seq, V7X_LANES), F32)
    padb = pltpu.VMEM((SPAN + seq, V7X_LANES), BF16)
    dproj, dkv, dg = pl.pallas_call(
        body, name=name, grid=(bsz, nhp, n_groups + 1),
        in_specs=[pl.BlockSpec(blk, lambda b, h, g: (b, 0, jnp.minimum(g, n_groups - 1) * nhp + h)),
                  pl.BlockSpec(blk, lambda b, h, g: (b, 0, h)),
                  pl.BlockSpec(blk, lambda b, h, g: (b, 0, nhp + h)),
                  pl.BlockSpec(blk, lambda b, h, g: (b, 0, n_groups * nhp + h)),
                  d_spec, d_spec, d_spec,
                  pl.BlockSpec((n_groups + 1, V7X_LANES), lambda b, h, g: (0, 0)),
                  tab, tab, tab,
                  pl.BlockSpec((V7X_LANES, V7X_LANES), lambda b, h, g: (0, 0))],
        out_specs=[pl.BlockSpec(blk, lambda b, h, g: (b, 0, g * nhp + h)),
                   pl.BlockSpec(blk, lambda b, h, g: (b, 0, (g // n_groups) * nhp + h)),
                   pl.BlockSpec((n_groups + 1, V7X_SUBLANES, V7X_LANES), lambda b, h, g: (0, 0, 0))],
        out_shape=[jax.ShapeDtypeStruct((bsz, seq, d4), BF16), jax.ShapeDtypeStruct((bsz, seq, 2 * d), BF16),
                   jax.ShapeDtypeStruct((n_groups + 1, V7X_SUBLANES, V7X_LANES), F32)],
        scratch_shapes=[natb, padb, padb, natb, nat, nat, pad, pad] + [nat] * 15 + [
                        pltpu.VMEM((2, 2 * SPAN, 2 * SPAN), F32)],
        compiler_params=_params(3),
    )(pview, kview, kview, pview, dao.reshape(dview), o.reshape(dview), lse.reshape(dview), gains, *tables, ones)
    dgain = dg.sum(axis=1).reshape(n_groups + 1, V7X_LANES // HEAD_DIM, HEAD_DIM).sum(axis=1)
    return dproj.reshape(t, d4), dkv.reshape(t, 2 * d), dgain


def _mesh_position():
    x, y, c = lax.axis_index("x"), lax.axis_index("y"), lax.axis_index("c")
    return x, y, c


def _peer(x, y, c, rel):
    return (1 - x if rel & 4 else x, 1 - y if rel & 2 else y, 1 - c if rel & 1 else c)


class _Exchange:
    def __init__(self, srcs, gather):
        self.srcs = list(srcs)
        self.gather = gather
        n = self.n = len(self.srcs)
        hbm = pl.BlockSpec(memory_space=pltpu.HBM)
        self.in_specs = [hbm] * n
        self.out_specs = [hbm] * n
        self.out_shape = [jax.ShapeDtypeStruct(((N_DEV,) + a.shape) if gather else a.shape, a.dtype)
                          for a in self.srcs]
        self.scratch = [pltpu.SemaphoreType.DMA((n * (N_DEV - 1),)), pltpu.SemaphoreType.DMA((n * (N_DEV - 1),)),
                        pltpu.SemaphoreType.DMA((n,))]

    def _copies(self, ins, outs, sems):
        send_sems, recv_sems, local_sems = sems
        x, y, c = _mesh_position()
        me = 4 * x + 2 * y + c
        remote, local = [], []
        for a in range(self.n):
            mine = ins[a] if self.gather else ins[a].at[me]
            local.append(pltpu.make_async_copy(mine, outs[a].at[me], local_sems.at[a]))
            for rel in range(1, N_DEV):
                px, py, pc = _peer(x, y, c, rel)
                s = a * (N_DEV - 1) + rel - 1
                src = ins[a] if self.gather else ins[a].at[4 * px + 2 * py + pc]
                remote.append(pltpu.make_async_remote_copy(
                    src_ref=src, dst_ref=outs[a].at[me], send_sem=send_sems.at[s], recv_sem=recv_sems.at[s],
                    device_id=(px, py, pc), device_id_type=pl.DeviceIdType.MESH))
        return remote, local

    def start(self, ins, outs, sems):
        remote, local = self._copies(ins, outs, sems)
        for cp in local + remote:
            cp.start()

    def wait(self, ins, outs, sems):
        remote, local = self._copies(ins, outs, sems)
        for cp in remote:
            cp.wait_recv()
        for cp in remote:
            cp.wait_send()
        for cp in local:
            cp.wait()


def _gather_chip_once(arrs, name):
    n = len(arrs)
    per = N_DEV - 1

    def body(*refs):
        ins, outs = refs[:n], refs[n:2 * n]
        send_sems, recv_sems, local_sems = refs[2 * n:]
        x, y, c = _mesh_position()
        me, sibling = (x, y, c), (x, y, 1 - c)
        chips = [(1 - x, y), (x, 1 - y), (1 - x, 1 - y)]

        def copy(a, k, block, to, src=None):
            bx, by, bc = block
            dst = outs[a].at[4 * bx + 2 * by + bc]
            return pltpu.make_async_remote_copy(
                src_ref=dst if src is None else src, dst_ref=dst, send_sem=send_sems.at[a * per + k],
                recv_sem=recv_sems.at[a * per + k], device_id=to, device_id_type=pl.DeviceIdType.MESH)

        local, sent = [], []
        for a in range(n):
            mine = pltpu.make_async_copy(ins[a], outs[a].at[4 * x + 2 * y + c], local_sems.at[a])
            mine.start()
            local.append(mine)
            first = [copy(a, 0, me, sibling, src=ins[a])]
            first += [copy(a, 1 + j, me, chip + (c,), src=ins[a]) for j, chip in enumerate(chips)]
            for cp in first:
                cp.start()
            sent += first
        for a in range(n):
            for j, chip in enumerate(chips):
                copy(a, 1 + j, chip + (c,), me).wait_recv()
                passed = copy(a, 4 + j, chip + (c,), sibling)
                passed.start()
                sent.append(passed)
        for a in range(n):
            copy(a, 0, sibling, me).wait_recv()
            for j, chip in enumerate(chips):
                copy(a, 4 + j, chip + (1 - c,), me).wait_recv()
        for cp in sent:
            cp.wait_send()
        for cp in local:
            cp.wait()

    hbm = pl.BlockSpec(memory_space=pltpu.HBM)
    return pl.pallas_call(
        body, name=name, in_specs=[hbm] * n, out_specs=[hbm] * n,
        out_shape=[jax.ShapeDtypeStruct((N_DEV,) + a.shape, a.dtype) for a in arrs],
        scratch_shapes=[pltpu.SemaphoreType.DMA((n * per,)), pltpu.SemaphoreType.DMA((n * per,)),
                        pltpu.SemaphoreType.DMA((n,))],
    )(*arrs)


def _run_exchange(ex, name):
    n = ex.n

    def body(*refs):
        ins, outs, sems = refs[:n], refs[n:2 * n], refs[2 * n:]
        ex.start(ins, outs, sems)
        ex.wait(ins, outs, sems)

    return pl.pallas_call(body, name=name, in_specs=ex.in_specs, out_specs=ex.out_specs, out_shape=ex.out_shape,
                          scratch_shapes=ex.scratch)(*ex.srcs)


def _hosted_call(body, ex, name, grid, in_specs, out_specs, out_shape, scratch_shapes, args):
    if ex is None:
        outs = pl.pallas_call(body, name=name, grid=grid, in_specs=in_specs, out_specs=out_specs, out_shape=out_shape,
                              scratch_shapes=scratch_shapes, compiler_params=_params(len(grid)))(*args)
        return list(outs), []
    n_in, n_out, n_scr, n = len(in_specs), len(out_specs), len(scratch_shapes), ex.n

    def hosted(*refs):
        h_in, e_in = refs[:n_in], refs[n_in:n_in + n]
        o0 = n_in + n
        h_out, e_out = refs[o0:o0 + n_out], refs[o0 + n_out:o0 + n_out + n]
        s0 = o0 + n_out + n
        h_scr, e_scr = refs[s0:s0 + n_scr], refs[s0 + n_scr:]
        ids = [pl.program_id(a) for a in range(len(grid))]
        first = functools.reduce(jnp.logical_and, [i == 0 for i in ids])
        last = functools.reduce(jnp.logical_and, [i == g - 1 for i, g in zip(ids, grid)])

        @pl.when(first)
        def _():
            ex.start(e_in, e_out, e_scr)

        body(*h_in, *h_out, *h_scr)

        @pl.when(last)
        def _():
            ex.wait(e_in, e_out, e_scr)

    outs = pl.pallas_call(
        hosted, name=name, grid=grid, in_specs=list(in_specs) + ex.in_specs,
        out_specs=list(out_specs) + ex.out_specs, out_shape=list(out_shape) + ex.out_shape,
        scratch_shapes=list(scratch_shapes) + ex.scratch, compiler_params=_params(len(grid)),
    )(*args, *ex.srcs)
    return list(outs[:n_out]), list(outs[n_out:])


def _sum_adamw(parts, w, m, v, name):
    _, r, wd = parts.shape
    tr = _pick(r, ADAM_ROWS, 8)
    c1 = 1.0 - ADAM_B1 ** ADAM_STEP
    c2 = 1.0 - ADAM_B2 ** ADAM_STEP

    def body(p_ref, w_ref, m_ref, v_ref, g_ref, d_ref, nm_ref, nv_ref):
        g = p_ref[0].astype(F32)
        for s in range(1, N_DEV):
            g = g + p_ref[s].astype(F32)
        nm = ADAM_B1 * m_ref[...] + (1.0 - ADAM_B1) * g
        nv = ADAM_B2 * v_ref[...] + (1.0 - ADAM_B2) * (g * g)
        g_ref[...] = g
        nm_ref[...] = nm
        nv_ref[...] = nv
        d_ref[...] = -ADAM_LR * ((nm / c1) / (jnp.sqrt(nv / c2) + ADAM_EPS) + ADAM_WD * w_ref[...])

    row = pl.BlockSpec((tr, wd), lambda i: (i, 0))
    return pl.pallas_call(
        body, name=name, grid=(r // tr,),
        in_specs=[pl.BlockSpec((N_DEV, tr, wd), lambda i: (0, i, 0)), row, row, row],
        out_specs=[row] * 4, out_shape=[jax.ShapeDtypeStruct((r, wd), F32)] * 4,
        compiler_params=_params(1),
    )(parts, w, m, v)


def _pack_rows(size, row_mult):
    rows = -(-size // PACK_LANES)
    return -(-rows // row_mult) * row_mult


def _pack(flats, row_mult, dtype, total_mult=None):
    out = []
    for f in flats:
        size = f.shape[-1]
        rows = _pack_rows(size, row_mult)
        pad = [(0, 0)] * (f.ndim - 1) + [(0, rows * PACK_LANES - size)]
        out.append(jnp.pad(f.astype(dtype), pad).reshape(f.shape[:-1] + (rows, PACK_LANES)))
    if total_mult is not None:
        total = sum(o.shape[-2] for o in out)
        extra = -(-total // total_mult) * total_mult - total
        if extra:
            out.append(jnp.zeros(out[0].shape[:-2] + (extra, PACK_LANES), dtype))
    return jnp.concatenate(out, axis=-2)


def _unpack(buf, sizes, row_mult):
    out, row = [], 0
    for size in sizes:
        rows = _pack_rows(size, row_mult)
        part = buf[..., row:row + rows, :]
        out.append(part.reshape(buf.shape[:-2] + (rows * PACK_LANES,))[..., :size])
        row += rows
    return out


def _to_slots(full, axis):
    if axis is None:
        return jnp.broadcast_to(full.reshape(1, -1), (N_DEV, full.size))
    shape = full.shape
    split = full.reshape(shape[:axis] + (N_DEV, shape[axis] // N_DEV) + shape[axis + 1:])
    return jnp.moveaxis(split, axis, 0).reshape(N_DEV, -1)


def _from_slots(slots, axis, block_shape):
    split = jnp.moveaxis(slots, 0, axis)
    shape = list(block_shape)
    shape[axis] *= N_DEV
    return split.reshape(shape)


def kernel(x, p, norm_g, w_in_a, conv_w, conv_b, ln_g, ln_b, w_out_a, kv_norm_g, w_kv, k_norm_g, w_in_b, q_norm_g, w_out_b, ple_norm_g, w_ple_gate, w_ple_proj, loss_target, m_norm_g, m_w_in_a, m_conv_w, m_conv_b, m_ln_g, m_ln_b, m_w_out_a, m_kv_norm_g, m_w_kv, m_k_norm_g, m_w_in_b, m_q_norm_g, m_w_out_b, m_ple_norm_g, m_w_ple_gate, m_w_ple_proj, v_norm_g, v_w_in_a, v_conv_w, v_conv_b, v_ln_g, v_ln_b, v_w_out_a, v_kv_norm_g, v_w_kv, v_k_norm_g, v_w_in_b, v_q_norm_g, v_w_out_b, v_ple_norm_g, v_w_ple_gate, v_w_ple_proj):
    weights = dict(zip(WEIGHT_NAMES, (norm_g, w_in_a, conv_w, conv_b, ln_g, ln_b, w_out_a, kv_norm_g, w_kv, k_norm_g,
                                      w_in_b, q_norm_g, w_out_b, ple_norm_g, w_ple_gate, w_ple_proj)))
    mom_m = dict(zip(WEIGHT_NAMES, (m_norm_g, m_w_in_a, m_conv_w, m_conv_b, m_ln_g, m_ln_b, m_w_out_a, m_kv_norm_g,
                                    m_w_kv, m_k_norm_g, m_w_in_b, m_q_norm_g, m_w_out_b, m_ple_norm_g, m_w_ple_gate,
                                    m_w_ple_proj)))
    mom_v = dict(zip(WEIGHT_NAMES, (v_norm_g, v_w_in_a, v_conv_w, v_conv_b, v_ln_g, v_ln_b, v_w_out_a, v_kv_norm_g,
                                    v_w_kv, v_k_norm_g, v_w_in_b, v_q_norm_g, v_w_out_b, v_ple_norm_g, v_w_ple_gate,
                                    v_w_ple_proj)))
    bsz, seq, d = x.shape
    t = bsz * seq
    assert seq % (max(DILATIONS) * SPAN) == 0 and d % V7X_LANES == 0

    full = {}

    def rows2d(a):
        return a.reshape(-1, a.shape[-1])

    def packed(source, names, dtype, total_mult=None):
        return _pack([source[n].reshape(-1) for n in names], 16, dtype, total_mult)

    def gathered(names, bufs):
        for n, buf in zip(names, bufs):
            full[n] = _from_slots(buf.reshape((N_DEV,) + weights[n].shape), SHARD_AXIS[n], weights[n].shape)

    w1_all, wv_all = _gather_chip_once([rows2d(weights['w_in_a']).astype(BF16),
                                        _pack([weights[n].reshape(-1) for n in VECTOR_WEIGHTS], 8, F32)],
                                       "gather_first")
    gathered(GROUP_FIRST, [w1_all])
    for n, slots in zip(VECTOR_WEIGHTS, _unpack(wv_all, [weights[n].size for n in VECTOR_WEIGHTS], 8)):
        full[n] = _from_slots(slots.reshape((N_DEV,) + weights[n].shape), SHARD_AXIS[n], weights[n].shape)
    gather_rest = _Exchange([rows2d(weights[n]).astype(BF16) for n in GROUP_REST], gather=True)
    wa_in = full['w_in_a'][0]
    cw, cb, lg, lb = full['conv_w'][0], full['conv_b'], full['ln_g'], full['ln_b']

    tables = _rope_tables(seq)
    ones = _head_ones(V7X_LANES)
    rep = V7X_LANES // HEAD_DIM
    head_gain = jnp.concatenate([jnp.tile(q_norm_g[0], (1, rep)), jnp.tile(k_norm_g, rep)[None]], axis=0)

    x0 = x.reshape(t, d)
    p0, p1 = p[0].reshape(t, -1), p[1].reshape(t, -1)
    target = loss_target.reshape(t, d)
    g_norm0, g_norm1 = norm_g[0:1], norm_g[1:2]
    g_ple0, g_ple1 = ple_norm_g[0:1], ple_norm_g[1:2]
    g_kv = kv_norm_g.reshape(1, d)

    (u0,) = _rmsnorm_fwd(x0, [g_norm0], "norm0")
    proj_a = _matmul(u0, wa_in, 'nn', "in_a")
    m_act, y_conv, w2_all = _conv_fwd(proj_a, cw, cb, lg, lb, seq, "conv_fwd", ex=gather_rest)
    gathered(GROUP_REST, w2_all)
    wa_out = full['w_out_a'][0]
    wkv = full['w_kv']
    wb_in, wb_out = full['w_in_b'][0], full['w_out_b'][0]
    wg, wp = full['w_ple_gate'], full['w_ple_proj']
    h0, pg0 = _matmul(m_act, wa_out, 'nn', "out_a", add=x0, norm_gain=g_ple0)
    gl0 = _matmul(pg0, wg[0], 'nn', "ple_gate0", out_dtype=BF16)
    pp0 = _matmul(p0, wp[0], 'nn', "ple_proj0", out_dtype=BF16)

    x1, (kvn, u1) = _ple_norm_fwd(h0, gl0, pp0, [g_kv, g_norm1], "ple0_norm1")
    kv = _matmul(kvn, wkv, 'nn', "kv")
    proj_b = _matmul(u1, wb_in, 'nn', "in_b")
    o_att, lse, ao = _attn_fwd(proj_b, kv, head_gain, tables, ones, bsz, seq, "attn_fwd")
    h1, pg1 = _matmul(ao, wb_out, 'nn', "out_b", add=x1, norm_gain=g_ple1)
    gl1 = _matmul(pg1, wg[1], 'nn', "ple_gate1", out_dtype=BF16)
    pp1 = _matmul(p1, wp[1], 'nn', "ple_proj1", out_dtype=BF16)

    dx2, dgl1, dpp1, loss_part = _ple_loss(h1, gl1, pp1, target, "ple1_loss")
    loss = lax.psum(jnp.sum(loss_part), ("x", "y", "c"))

    grads = {}
    slot = {}

    dwp1 = _matmul(p1, dpp1, 'tn', "d_ple_proj1", out_dtype=BF16, slot_cols=d // N_DEV)
    dwg1 = _matmul(pg1, dgl1, 'tn', "d_ple_gate1", out_dtype=BF16)
    dpg1 = _matmul(dgl1, wg[1], 'nt', "d_ple_norm1", out_dtype=BF16)
    dh1, (dg_ple1,) = _rmsnorm_bwd(h1, [g_ple1], [dpg1], dx2, "ple_norm1_bwd")
    slot['w_out_b'] = _matmul(ao, dh1, 'tn', "d_out_b", out_dtype=BF16).reshape(N_DEV, -1, d)
    dao = _matmul(dh1, wb_out, 'nt', "d_ao", out_dtype=BF16)
    dproj_b, dkv, dg_head = _attn_bwd(proj_b, kv, dao, o_att, lse, head_gain, tables, ones, bsz, seq, "attn_bwd")
    slot['w_in_b'] = _matmul(u1, dproj_b, 'tn', "d_in_b", out_dtype=BF16, slot_cols=4 * d // N_DEV)
    du1 = _matmul(dproj_b, wb_in, 'nt', "d_u1", out_dtype=BF16)
    slot['w_kv'] = _matmul(kvn, dkv, 'tn', "d_kv", out_dtype=BF16, slot_cols=2 * d // N_DEV)
    dkvn = _matmul(dkv, wkv, 'nt', "d_kvn", out_dtype=BF16)
    dx1, (dg_kv, dg_norm1), dgl0, dpp0 = _rmsnorm_bwd(x1, [g_kv, g_norm1], [dkvn, du1], dh1, "norm1_bwd",
                                                      ple=(gl0, pp0))

    dwp0 = _matmul(p0, dpp0, 'tn', "d_ple_proj0", out_dtype=BF16, slot_cols=d // N_DEV)
    dwg0 = _matmul(pg0, dgl0, 'tn', "d_ple_gate0", out_dtype=BF16)
    dpg0 = _matmul(dgl0, wg[0], 'nt', "d_ple_norm0", out_dtype=BF16)
    dh0, (dg_ple0,) = _rmsnorm_bwd(h0, [g_ple0], [dpg0], dx1, "ple_norm0_bwd")
    slot['w_out_a'] = _matmul(m_act, dh0, 'tn', "d_out_a", out_dtype=BF16).reshape(N_DEV, -1, d)
    dm = _matmul(dh0, wa_out, 'nt', "d_m", out_dtype=BF16)
    dy_conv, dz, d_lg, d_lb, d_cb = _ln_gate_bwd(dm, y_conv, proj_a, lg, lb, "ln_gate_bwd")
    slot['w_ple_gate'] = jnp.stack([dwg0.reshape(N_DEV, -1, d), dwg1.reshape(N_DEV, -1, d)],
                                   axis=1).reshape(N_DEV, -1, d)
    slot['w_ple_proj'] = jnp.stack([dwp0, dwp1], axis=1).reshape(N_DEV, -1, d // N_DEV)

    dproj_a, d_cw, parts_rest = _conv_bwd(dy_conv, dz, proj_a, cw, seq, "conv_bwd",
                                          ex=_Exchange([slot[n] for n in GROUP_REST], gather=False))
    slot['w_in_a'] = _matmul(u0, dproj_a, 'tn', "d_in_a", out_dtype=BF16, slot_cols=wa_in.shape[1] // N_DEV)
    du0, parts_first = _matmul(dproj_a, wa_in, 'nt', "d_u0", out_dtype=BF16,
                               ex=_Exchange([slot['w_in_a']], gather=False))
    dx0, (dg_norm0,) = _rmsnorm_bwd(x0, [g_norm0], [du0], dh0, "norm0_bwd", dx_dtype=F32)

    grads['norm_g'] = jnp.stack([dg_norm0, dg_norm1])
    grads['conv_w'] = d_cw[None]
    grads['conv_b'] = d_cb[None]
    grads['ln_g'] = d_lg[None]
    grads['ln_b'] = d_lb[None]
    grads['kv_norm_g'] = dg_kv
    grads['k_norm_g'] = dg_head[3]
    grads['q_norm_g'] = dg_head[0:3][None]
    grads['ple_norm_g'] = jnp.stack([dg_ple0, dg_ple1])
    small_pack = _pack([_to_slots(grads[n], SHARD_AXIS[n]) for n in GROUP_SMALL], 16, BF16)
    (parts_small,) = _run_exchange(_Exchange([small_pack], gather=False), "exchange_small")

    updated = {}
    for n, parts in zip(GROUP_REST + GROUP_FIRST, parts_rest + parts_first):
        outs = _sum_adamw(parts, rows2d(weights[n]), rows2d(mom_m[n]), rows2d(mom_v[n]), "sum_adamw_" + n)
        for kind, buf in enumerate(outs):
            updated[kind, n] = buf.reshape(weights[n].shape)
    outs = _sum_adamw(parts_small, packed(weights, GROUP_SMALL, F32), packed(mom_m, GROUP_SMALL, F32),
                      packed(mom_v, GROUP_SMALL, F32), "sum_adamw_small")
    sizes = [weights[n].size for n in GROUP_SMALL]
    for kind, buf in enumerate(outs):
        for n, flat in zip(GROUP_SMALL, _unpack(buf, sizes, 16)):
            updated[kind, n] = flat.reshape(weights[n].shape)
    result = [loss, dx0.reshape(bsz, seq, d)]
    for kind in range(4):
        result.extend(updated[kind, n] for n in WEIGHT_NAMES)
    return tuple(result)
```

```python
import functools

import jax
import jax.numpy as jnp
from jax import lax
from jax.experimental import pallas as pl
from jax.experimental.pallas import tpu as pltpu

F32 = jnp.float32
BF16 = jnp.bfloat16

N_DEV = 8
HEAD_DIM = 64
ROPE_DIM = 16
ROPE_THETA = 500000.0
EPS = 1e-6
NEG_INF = -1e30
SPAN = 128
DILATIONS = (1, 4, 16)
CONV_WIDTH = 31
HALO = 32
CONV_ROWS = 32
CONV_W_ROWS = 64
CONV_UNROLL = 4
CONV_BWD_UNROLL = 8
PACK_LANES = 1024
V7X_LANES = 128
V7X_SUBLANES = 8
VMEM_LIMIT_BYTES = 56 * 1024 * 1024

ADAM_LR = 0.001
ADAM_B1 = 0.9
ADAM_B2 = 0.999
ADAM_EPS = 1e-08
ADAM_WD = 0.01
ADAM_STEP = 10
ADAM_ROWS = 256

WEIGHT_NAMES = ('norm_g', 'w_in_a', 'conv_w', 'conv_b', 'ln_g', 'ln_b', 'w_out_a', 'kv_norm_g', 'w_kv',
                'k_norm_g', 'w_in_b', 'q_norm_g', 'w_out_b', 'ple_norm_g', 'w_ple_gate', 'w_ple_proj')
SHARD_AXIS = {'norm_g': None, 'w_in_a': 2, 'conv_w': 2, 'conv_b': 1, 'ln_g': 1, 'ln_b': 1, 'w_out_a': 1,
              'kv_norm_g': None, 'w_kv': 1, 'k_norm_g': None, 'w_in_b': 2, 'q_norm_g': None, 'w_out_b': 1,
              'ple_norm_g': None, 'w_ple_gate': 1, 'w_ple_proj': 2}
VECTOR_WEIGHTS = ('conv_w', 'conv_b', 'ln_g', 'ln_b')
GROUP_FIRST = ('w_in_a',)
GROUP_REST = ('w_out_a', 'w_kv', 'w_in_b', 'w_out_b', 'w_ple_gate', 'w_ple_proj')
GROUP_SMALL = ('norm_g', 'conv_w', 'conv_b', 'ln_g', 'ln_b', 'kv_norm_g', 'k_norm_g', 'q_norm_g', 'ple_norm_g')


def _pick(n, target, mult):
    t = (min(target, n) // mult) * mult
    while t >= mult:
        if n % t == 0:
            return t
        t -= mult
    return n


def _params(n_grid):
    return pltpu.CompilerParams(dimension_semantics=("arbitrary",) * n_grid, vmem_limit_bytes=VMEM_LIMIT_BYTES)


def _sig(x):
    return 0.5 * jnp.tanh(0.5 * x) + 0.5


def _colsum8(v):
    r, w = v.shape
    return v.reshape(r // V7X_SUBLANES, V7X_SUBLANES, w).sum(axis=0)


def _rows(tm, w, col=0):
    return pl.BlockSpec((tm, w), lambda i: (i, col))


def _const(shape):
    nd = len(shape)
    return pl.BlockSpec(shape, lambda i: (0,) * nd)


def _segsum(v, e_ref):
    hi = v.astype(BF16)
    lo = (v - hi.astype(F32)).astype(BF16)
    e = e_ref[...]
    return jnp.dot(hi, e, preferred_element_type=F32) + jnp.dot(lo, e, preferred_element_type=F32)


MM_TILE = 1024
MM_TILE_K = 3072
MM_TILE_WIDE = 2048
MM_TILE_K_TN = 4096


def _matmul(a, b, mode, name, out_dtype=F32, add=None, ex=None, slot_cols=None, norm_gain=None):
    if mode == 'nn':
        (m, k), (_, n) = a.shape, b.shape
    elif mode == 'nt':
        (m, k), (n, _) = a.shape, b.shape
    else:
        (k, m), (_, n) = a.shape, b.shape
    out_struct = jax.ShapeDtypeStruct((m, n), out_dtype)
    n_slots = 0
    if mode == 'tn':
        tm, tn, tk = _pick(m, MM_TILE, 128), _pick(n, MM_TILE, 128), _pick(k, MM_TILE_K_TN, 128)
        o_spec = pl.BlockSpec((tm, tn), lambda i, j, kk: (i, j))
        if slot_cols is not None:
            assert n == N_DEV * slot_cols
            n_slots = max(s for s in (1, 2, 4, 8) if s == 1 or slot_cols * s <= MM_TILE)
            tn = slot_cols * n_slots
            o_spec = pl.BlockSpec((n_slots, tm, slot_cols), lambda i, j, kk: (j, i, 0))
            out_struct = jax.ShapeDtypeStruct((N_DEV, m, slot_cols), out_dtype)
        grid = (m // tm, n // tn, k // tk)
        a_spec = pl.BlockSpec((tk, tm), lambda i, j, kk: (kk, i))
        b_spec = pl.BlockSpec((tk, tn), lambda i, j, kk: (kk, j))
        dims = (((0,), (0,)), ((), ()))
    else:
        tn_max = MM_TILE_WIDE if k <= MM_TILE else MM_TILE
        tn = _pick(n, tn_max, 128)
        tm_max = MM_TILE_WIDE if (k <= MM_TILE and tn <= MM_TILE and add is None and norm_gain is None) else MM_TILE
        tm, tk = _pick(m, tm_max, 128), _pick(k, MM_TILE_K, 128)
        grid = (n // tn, m // tm, k // tk)
        a_spec = pl.BlockSpec((tm, tk), lambda j, i, kk: (i, kk))
        o_spec = pl.BlockSpec((tm, tn), lambda j, i, kk: (i, j))
        if mode == 'nn':
            b_spec = pl.BlockSpec((tk, tn), lambda j, i, kk: (kk, j))
            dims = (((1,), (0,)), ((), ()))
        else:
            b_spec = pl.BlockSpec((tn, tk), lambda j, i, kk: (j, kk))
            dims = (((1,), (1,)), ((), ()))
    nk = grid[2]
    has_add = add is not None
    has_norm = norm_gain is not None
    assert not has_norm or (tn == n and mode != 'tn')

    def body(*refs):
        a_ref, b_ref = refs[0], refs[1]
        add_ref = refs[2] if has_add else None
        gain_ref = refs[2 + has_add] if has_norm else None
        o_ref = refs[2 + has_add + has_norm]
        norm_ref = refs[3 + has_add + has_norm] if has_norm else None
        part = lax.dot_general(a_ref[...].astype(BF16), b_ref[...].astype(BF16), dims, preferred_element_type=F32)

        def finish(total):
            if has_add:
                total = total + add_ref[...]
            if n_slots:
                for s in range(n_slots):
                    o_ref[s] = total[:, s * slot_cols:(s + 1) * slot_cols].astype(out_dtype)
            else:
                o_ref[...] = total.astype(out_dtype)
            if has_norm:
                y = total * lax.rsqrt(jnp.mean(total * total, axis=-1, keepdims=True) + EPS)
                norm_ref[...] = (y * gain_ref[...]).astype(BF16)

        if nk == 1:
            finish(part)
        else:
            acc_ref = refs[3 + has_add + 2 * has_norm]
            kk = pl.program_id(2)

            @pl.when(kk == 0)
            def _():
                acc_ref[...] = part

            @pl.when(kk > 0)
            def _():
                acc_ref[...] += part

            @pl.when(kk == nk - 1)
            def _():
                finish(acc_ref[...])

    in_specs = [a_spec, b_spec] + ([o_spec] if has_add else [])
    args = [a, b] + ([add] if has_add else [])
    out_specs, out_structs = [o_spec], [out_struct]
    if has_norm:
        in_specs.append(pl.BlockSpec((1, n), lambda j, i, kk: (0, 0)))
        args.append(norm_gain)
        out_specs.append(o_spec)
        out_structs.append(jax.ShapeDtypeStruct((m, n), BF16))
    scratch = [pltpu.VMEM((tm, tn), F32)] if nk > 1 else []
    outs, moved = _hosted_call(body, ex, name, grid, in_specs, out_specs, out_structs, scratch, args)
    out = outs[0] if not has_norm else tuple(outs)
    return out if ex is None else (out, moved)


def _rmsnorm_fwd(x, gains, name):
    t, d = x.shape
    tm = _pick(t, 512, 8)
    n = len(gains)

    def body(*refs):
        x_ref, g_refs, o_refs = refs[0], refs[1:1 + n], refs[1 + n:]
        xv = x_ref[...]
        y = xv * lax.rsqrt(jnp.mean(xv * xv, axis=-1, keepdims=True) + EPS)
        for g_ref, o_ref in zip(g_refs, o_refs):
            o_ref[...] = (y * g_ref[...]).astype(BF16)

    return pl.pallas_call(
        body, name=name, grid=(t // tm,),
        in_specs=[_rows(tm, d)] + [_const((1, d))] * n,
        out_specs=[_rows(tm, d)] * n,
        out_shape=[jax.ShapeDtypeStruct((t, d), BF16)] * n,
        compiler_params=_params(1),
    )(x, *gains)


def _ple_grads(dx, gl, pp):
    sg = _sig(gl)
    return (dx * pp * sg * (1.0 - sg)).astype(BF16), (dx * sg).astype(BF16)


def _rmsnorm_bwd(x, gains, dys, add, name, ple=None, dx_dtype=BF16):
    t, d = x.shape
    tm = _pick(t, 512, 16)
    n = len(gains)
    n_ple = 0 if ple is None else 2

    def body(*refs):
        x_ref, add_ref = refs[0], refs[1]
        g_refs, dy_refs = refs[2:2 + n], refs[2 + n:2 + 2 * n]
        ple_refs = refs[2 + 2 * n:2 + 2 * n + n_ple]
        outs = refs[2 + 2 * n + n_ple:]
        dx_ref, dg_refs, dple_refs = outs[0], outs[1:1 + n], outs[1 + n:]
        i = pl.program_id(0)
        xv = x_ref[...]
        r = lax.rsqrt(jnp.mean(xv * xv, axis=-1, keepdims=True) + EPS)
        xhat = xv * r
        dx = add_ref[...].astype(F32)
        for g_ref, dy_ref, dg_ref in zip(g_refs, dy_refs, dg_refs):
            dy = dy_ref[...].astype(F32)
            dyg = dy * g_ref[...]
            dx = dx + r * (dyg - xhat * jnp.mean(dyg * xhat, axis=-1, keepdims=True))
            part = _colsum8(dy * xhat)

            @pl.when(i == 0)
            def _():
                dg_ref[...] = part

            @pl.when(i > 0)
            def _():
                dg_ref[...] += part

        dx_ref[...] = dx.astype(dx_dtype)
        if n_ple:
            dple_refs[0][...], dple_refs[1][...] = _ple_grads(dx, ple_refs[0][...].astype(F32),
                                                               ple_refs[1][...].astype(F32))

    outs = pl.pallas_call(
        body, name=name, grid=(t // tm,),
        in_specs=[_rows(tm, d), _rows(tm, d)] + [_const((1, d))] * n + [_rows(tm, d)] * (n + n_ple),
        out_specs=[_rows(tm, d)] + [_const((V7X_SUBLANES, d))] * n + [_rows(tm, d)] * n_ple,
        out_shape=([jax.ShapeDtypeStruct((t, d), dx_dtype)] + [jax.ShapeDtypeStruct((V7X_SUBLANES, d), F32)] * n
                   + [jax.ShapeDtypeStruct((t, d), BF16)] * n_ple),
        compiler_params=_params(1),
    )(x, add, *gains, *dys, *(ple or ()))
    dgs = [o.sum(axis=0) for o in outs[1:1 + n]]
    return (outs[0], dgs) if ple is None else (outs[0], dgs, outs[1 + n], outs[2 + n])


def _ple_norm_fwd(h, gl, pp, gains, name):
    t, d = h.shape
    tm = _pick(t, 512, 16)
    n = len(gains)

    def body(*refs):
        h_ref, gl_ref, pp_ref = refs[:3]
        g_refs, x_ref, o_refs = refs[3:3 + n], refs[3 + n], refs[4 + n:]
        xv = h_ref[...] + _sig(gl_ref[...].astype(F32)) * pp_ref[...].astype(F32)
        x_ref[...] = xv
        y = xv * lax.rsqrt(jnp.mean(xv * xv, axis=-1, keepdims=True) + EPS)
        for g_ref, o_ref in zip(g_refs, o_refs):
            o_ref[...] = (y * g_ref[...]).astype(BF16)

    outs = pl.pallas_call(
        body, name=name, grid=(t // tm,),
        in_specs=[_rows(tm, d)] * 3 + [_const((1, d))] * n, out_specs=[_rows(tm, d)] * (1 + n),
        out_shape=[jax.ShapeDtypeStruct((t, d), F32)] + [jax.ShapeDtypeStruct((t, d), BF16)] * n,
        compiler_params=_params(1),
    )(h, gl, pp, *gains)
    return outs[0], outs[1:]


def _ple_loss(h, gl, pp, target, name):
    t, d = h.shape
    tm = _pick(t, 512, 16)
    inv_d = 1.0 / d

    def body(h_ref, gl_ref, pp_ref, t_ref, dy_ref, dgl_ref, dpp_ref, l_ref):
        i = pl.program_id(0)
        gl, pp = gl_ref[...].astype(F32), pp_ref[...].astype(F32)
        e = h_ref[...] + _sig(gl) * pp - t_ref[...]
        dy = e * inv_d
        dy_ref[...] = dy.astype(BF16)
        dgl_ref[...], dpp_ref[...] = _ple_grads(dy, gl, pp)
        part = _colsum8(e * e) * (0.5 * inv_d)

        @pl.when(i == 0)
        def _():
            l_ref[...] = part

        @pl.when(i > 0)
        def _():
            l_ref[...] += part

    return pl.pallas_call(
        body, name=name, grid=(t // tm,), in_specs=[_rows(tm, d)] * 4,
        out_specs=[_rows(tm, d)] * 3 + [_const((V7X_SUBLANES, d))],
        out_shape=[jax.ShapeDtypeStruct((t, d), BF16), jax.ShapeDtypeStruct((t, d), BF16),
                   jax.ShapeDtypeStruct((t, d), BF16), jax.ShapeDtypeStruct((V7X_SUBLANES, d), F32)],
        compiler_params=_params(1),
    )(h, gl, pp, target)


def _shift_scratch(ts, cc):
    return pltpu.VMEM((V7X_SUBLANES, ts + HALO - V7X_SUBLANES, cc), F32)


def _shifted_copies(sh_ref, win_ref, cs, ts):
    rows = ts + HALO - V7X_SUBLANES
    for s in range(1, V7X_SUBLANES):
        sh_ref[s] = win_ref[pl.ds(s, rows), cs]


def _tap(sh_ref, win_ref, cs, offset, rows, r0):
    s = offset % V7X_SUBLANES
    start = pl.multiple_of(r0 + (offset - s), V7X_SUBLANES)
    if s == 0:
        return win_ref[pl.ds(start, rows), cs]
    return sh_ref[s, pl.ds(start, rows), :]


def _conv_fwd(proj, conv_w, conv_b, ln_g, ln_b, seq, name, ex=None):
    t, c3 = proj.shape
    c = c3 // 3
    ts = _pick(seq, 256, HALO)
    nsb = seq // ts
    cc = _pick(c, 512, V7X_LANES)
    hb = ts // HALO

    def body(a_ref, b_ref, z_ref, ap_ref, bp_ref, w_ref, cb_ref, g_ref, be_ref, m_ref, y_ref, win_ref, sh_ref):
        i = pl.program_id(0)
        first = (i % nsb) == 0
        win_ref[0:HALO, :] = jnp.where(first, 0.0, ap_ref[...] * _sig(bp_ref[...]))
        win_ref[HALO:, :] = a_ref[...] * _sig(b_ref[...])
        for ci in range(c // cc):
            cs = slice(ci * cc, (ci + 1) * cc)
            _shifted_copies(sh_ref, win_ref, cs, ts)

            def out_rows(rb, carry, cs=cs):
                r0 = rb * CONV_ROWS
                acc = jnp.zeros((CONV_ROWS, cc), F32) + cb_ref[:, cs]
                for k in range(CONV_WIDTH):
                    acc = acc + w_ref[k:k + 1, cs] * _tap(sh_ref, win_ref, cs, HALO - (CONV_WIDTH - 1) + k,
                                                           CONV_ROWS, r0)
                y_ref[pl.ds(pl.multiple_of(r0, CONV_ROWS), CONV_ROWS), cs] = acc
                return carry

            lax.fori_loop(0, ts // CONV_ROWS, out_rows, 0, unroll=CONV_UNROLL)
        y = y_ref[...]
        mu = jnp.mean(y, axis=-1, keepdims=True)
        xc = y - mu
        rstd = lax.rsqrt(jnp.mean(xc * xc, axis=-1, keepdims=True) + EPS)
        ln = xc * rstd * g_ref[...] + be_ref[...]
        zz = z_ref[...]
        m_ref[...] = (ln * _sig(ln) * zz * _sig(zz)).astype(BF16)

    halo_a = pl.BlockSpec((HALO, c), lambda i: (jnp.maximum(i * hb - 1, 0), 0))
    halo_b = pl.BlockSpec((HALO, c), lambda i: (jnp.maximum(i * hb - 1, 0), 1))
    (m_act, y), moved = _hosted_call(
        body, ex, name, (t // ts,),
        [_rows(ts, c, 0), _rows(ts, c, 1), _rows(ts, c, 2), halo_a, halo_b,
         _const((CONV_WIDTH, c)), _const((1, c)), _const((1, c)), _const((1, c))],
        [_rows(ts, c), _rows(ts, c)],
        [jax.ShapeDtypeStruct((t, c), BF16), jax.ShapeDtypeStruct((t, c), F32)],
        [pltpu.VMEM((HALO + ts, c), F32), _shift_scratch(ts, cc)],
        (proj, proj, proj, proj, proj, conv_w, conv_b, ln_g, ln_b))
    return m_act, y, moved


def _ln_gate_bwd(dm, y, proj, ln_g, ln_b, name):
    t, c = y.shape
    tm = _pick(t, 256, 8)

    def body(dm_ref, y_ref, z_ref, g_ref, be_ref, dy_ref, dz_ref, dg_ref, db_ref, dcb_ref):
        i = pl.program_id(0)
        yv = y_ref[...]
        mu = jnp.mean(yv, axis=-1, keepdims=True)
        xc = yv - mu
        rstd = lax.rsqrt(jnp.mean(xc * xc, axis=-1, keepdims=True) + EPS)
        xhat = xc * rstd
        g = g_ref[...]
        ln = xhat * g + be_ref[...]
        sl = _sig(ln)
        zz = z_ref[...]
        sz = _sig(zz)
        dmv = dm_ref[...].astype(F32)
        dz_ref[...] = (dmv * (ln * sl) * (sz * (1.0 + zz * (1.0 - sz)))).astype(BF16)
        dln = dmv * (zz * sz) * (sl * (1.0 + ln * (1.0 - sl)))
        dxh = dln * g
        dyv = rstd * (dxh - jnp.mean(dxh, axis=-1, keepdims=True)
                      - xhat * jnp.mean(dxh * xhat, axis=-1, keepdims=True))
        dy_ref[...] = dyv
        parts = (_colsum8(dln * xhat), _colsum8(dln), _colsum8(dyv))

        @pl.when(i == 0)
        def _():
            for ref, part in zip((dg_ref, db_ref, dcb_ref), parts):
                ref[...] = part

        @pl.when(i > 0)
        def _():
            for ref, part in zip((dg_ref, db_ref, dcb_ref), parts):
                ref[...] += part

    acc = jax.ShapeDtypeStruct((V7X_SUBLANES, c), F32)
    outs = pl.pallas_call(
        body, name=name, grid=(t // tm,),
        in_specs=[_rows(tm, c), _rows(tm, c), _rows(tm, c, 2), _const((1, c)), _const((1, c))],
        out_specs=[_rows(tm, c), _rows(tm, c)] + [_const((V7X_SUBLANES, c))] * 3,
        out_shape=[jax.ShapeDtypeStruct((t, c), F32), jax.ShapeDtypeStruct((t, c), BF16), acc, acc, acc],
        compiler_params=_params(1),
    )(dm, y, proj, ln_g, ln_b)
    return outs[0], outs[1], outs[2].sum(axis=0), outs[3].sum(axis=0), outs[4].sum(axis=0)


def _conv_bwd(dy, dz, proj, conv_w, seq, name, ex=None):
    t, c3 = proj.shape
    c = c3 // 3
    ts = _pick(seq, 256, HALO)
    nsb = seq // ts
    cc = _pick(c, 512, V7X_LANES)
    hb = ts // HALO
    last_halo = t // HALO - 1
    back = CONV_WIDTH - 1

    def body(dy_ref, dyn_ref, dz_ref, a_ref, b_ref, ap_ref, bp_ref, w_ref, o_ref, dw_ref, win_ref, dwin_ref,
             sh_ref, dsh_ref):
        i = pl.program_id(0)
        first = (i % nsb) == 0
        last = (i % nsb) == nsb - 1
        win_ref[0:HALO, :] = jnp.where(first, 0.0, ap_ref[...] * _sig(bp_ref[...]))
        win_ref[HALO:, :] = a_ref[...] * _sig(b_ref[...])
        dwin_ref[0:ts, :] = dy_ref[...]
        dwin_ref[ts:, :] = jnp.where(last, 0.0, dyn_ref[...])

        @pl.when(i == 0)
        def _():
            dw_ref[...] = jnp.zeros_like(dw_ref)

        for ci in range(c // cc):
            cs = slice(ci * cc, (ci + 1) * cc)
            _shifted_copies(sh_ref, win_ref, cs, ts)
            _shifted_copies(dsh_ref, dwin_ref, cs, ts)

            def in_grad_rows(rb, carry, cs=cs, ci=ci):
                r0 = rb * CONV_ROWS
                rows = pl.ds(pl.multiple_of(r0, CONV_ROWS), CONV_ROWS)
                dglu = jnp.zeros((CONV_ROWS, cc), F32)
                for k in range(CONV_WIDTH):
                    dglu = dglu + w_ref[k:k + 1, cs] * _tap(dsh_ref, dwin_ref, cs, back - k, CONV_ROWS, r0)
                sbc = _sig(b_ref[rows, cs])
                o_ref[rows, cs] = (dglu * sbc).astype(BF16)
                o_ref[rows, c + ci * cc:c + (ci + 1) * cc] = (dglu * a_ref[rows, cs] * sbc * (1.0 - sbc)).astype(BF16)
                return carry

            def w_grad_rows(rb, carry, cs=cs):
                r0 = rb * CONV_W_ROWS
                dcur = dwin_ref[pl.ds(pl.multiple_of(r0, CONV_W_ROWS), CONV_W_ROWS), cs]
                for k in range(CONV_WIDTH):
                    dw_ref[k * V7X_SUBLANES:(k + 1) * V7X_SUBLANES, cs] += _colsum8(
                        dcur * _tap(sh_ref, win_ref, cs, HALO - back + k, CONV_W_ROWS, r0))
                return carry

            lax.fori_loop(0, ts // CONV_ROWS, in_grad_rows, 0, unroll=CONV_BWD_UNROLL)
            lax.fori_loop(0, ts // CONV_W_ROWS, w_grad_rows, 0)
        o_ref[:, 2 * c:] = dz_ref[...]

    halo_next = pl.BlockSpec((HALO, c), lambda i: (jnp.minimum((i + 1) * hb, last_halo), 0))
    halo_a = pl.BlockSpec((HALO, c), lambda i: (jnp.maximum(i * hb - 1, 0), 0))
    halo_b = pl.BlockSpec((HALO, c), lambda i: (jnp.maximum(i * hb - 1, 0), 1))
    (dproj, dw), moved = _hosted_call(
        body, ex, name, (t // ts,),
        [_rows(ts, c), halo_next, _rows(ts, c), _rows(ts, c, 0), _rows(ts, c, 1), halo_a, halo_b,
         _const((CONV_WIDTH, c))],
        [_rows(ts, c3), _const((CONV_WIDTH * V7X_SUBLANES, c))],
        [jax.ShapeDtypeStruct((t, c3), BF16), jax.ShapeDtypeStruct((CONV_WIDTH * V7X_SUBLANES, c), F32)],
        [pltpu.VMEM((HALO + ts, c), F32), pltpu.VMEM((ts + HALO, c), F32),
         _shift_scratch(ts, cc), _shift_scratch(ts, cc)],
        (dy, dy, dz, proj, proj, proj, proj, conv_w))
    return dproj, dw.reshape(CONV_WIDTH, V7X_SUBLANES, c).sum(axis=1), moved


def _rope_tables(seq):
    half = ROPE_DIM // 2
    inv = ROPE_THETA ** (-jnp.arange(half, dtype=F32) * (2.0 / ROPE_DIM))
    ang = jnp.arange(seq).astype(F32)[:, None] * inv[None, :]
    cos, sin = jnp.cos(ang), jnp.sin(ang)
    zeros = jnp.zeros((seq, HEAD_DIM - ROPE_DIM), F32)
    zh = jnp.zeros((seq, half), F32)
    a = jnp.concatenate([cos, cos, zeros + 1.0], axis=1)
    b = jnp.concatenate([zh, sin, zeros], axis=1)
    c = jnp.concatenate([-sin, zh, zeros], axis=1)
    rep = V7X_LANES // HEAD_DIM
    return tuple(jnp.tile(v, (1, rep)) for v in (a, b, c))


def _head_ones(d):
    head = jnp.arange(d) // HEAD_DIM
    return (head[:, None] == head[None, :]).astype(BF16)


def _rope(ch, ta, tb, tc):
    return ta * ch + tb * pltpu.roll(ch, ROPE_DIM // 2, 1) + tc * pltpu.roll(ch, V7X_LANES - ROPE_DIM // 2, 1)


def _rope_t(ch, ta, tb, tc):
    return ta * ch + pltpu.roll(tb * ch, V7X_LANES - ROPE_DIM // 2, 1) + pltpu.roll(tc * ch, ROPE_DIM // 2, 1)


def _norm_rope_bwd(xhat, r, dout, gain, ta, tb, tc, e_ref):
    dxn = _rope_t(dout, ta, tb, tc)
    dxh = dxn * gain
    dx = r * (dxh - xhat * (_segsum(dxh * xhat, e_ref) * (1.0 / HEAD_DIM)))
    return dx, _colsum8(dxn * xhat)


def _norm_rope_rows(dst_ref, src_ref, gain, ta_ref, tb_ref, tc_ref, e_ref, seq, xhat_ref=None, r_ref=None):
    for r0 in range(0, seq, ATTN_PIECE):
        rows = slice(r0, r0 + ATTN_PIECE)
        xv = src_ref[rows, :]
        r = lax.rsqrt(_segsum(xv * xv, e_ref) * (1.0 / HEAD_DIM) + EPS)
        xhat = xv * r
        if xhat_ref is not None:
            xhat_ref[rows, :] = xhat
            r_ref[rows, :] = r
        dst_ref[rows, :] = _rope(xhat * gain, ta_ref[rows, :], tb_ref[rows, :], tc_ref[rows, :])


ATTN_PIECE = 256
ATTN_UNROLL = 16
CHUNK_UNROLL = 8


def _pieces(dil, seq):
    length = seq // dil
    rows = min(length, ATTN_PIECE)
    return [(r + dil * ci * rows, r * length + ci * rows, rows) for r in range(dil) for ci in range(length // rows)]


def _strided(ref, start, rows, dil):
    if dil == 1:
        return ref[pl.ds(start, rows), :]
    return ref[pl.ds(start, rows, stride=dil), :]


def _strided_set(ref, start, rows, dil, val):
    if dil == 1:
        ref[pl.ds(start, rows), :] = val
    else:
        ref[pl.ds(start, rows, stride=dil), :] = val


def _nt(a, b):
    return lax.dot_general(a, b, (((1,), (1,)), ((), ())), preferred_element_type=F32)


def _tn(a, b):
    return lax.dot_general(a, b, (((0,), (0,)), ((), ())), preferred_element_type=F32)


def _set_bias(bias_ref):
    qi = lax.broadcasted_iota(jnp.int32, (2 * SPAN, 2 * SPAN), 0) & (SPAN - 1)
    kj = lax.broadcasted_iota(jnp.int32, (2 * SPAN, 2 * SPAN), 1)
    band = jnp.logical_and(kj >= qi, (kj - SPAN) <= qi)
    bias_ref[1] = jnp.where(band, 0.0, NEG_INF)
    bias_ref[0] = jnp.where(jnp.logical_and(band, kj >= SPAN), 0.0, NEG_INF)


def _block_keys(bias_ref, j, qs, nb):
    if nb == 1:
        return pl.ds(pl.multiple_of(qs + SPAN, SPAN), SPAN), bias_ref[1, :, SPAN:]
    return pl.ds(qs, 2 * SPAN), bias_ref[jnp.minimum(j & (nb - 1), 1)]


def _stack_heads(v, head0):
    zero = jnp.zeros_like(v)
    return jnp.concatenate([jnp.where(head0, v, zero), jnp.where(head0, zero, v)], axis=0)


def _unstack_heads(v2, head0):
    return jnp.where(head0, v2[:SPAN], v2[SPAN:])


def _head_cols(v, lane=0):
    return jnp.concatenate([v[:, lane:lane + 1], v[:, HEAD_DIM + lane:HEAD_DIM + lane + 1]], axis=0)


def _attn_fwd(proj_b, kv, gains, tables, ones, bsz, seq, name):
    t, d4 = proj_b.shape
    d = d4 // 4
    nhp = d // V7X_LANES
    nblk = seq // SPAN
    scale = HEAD_DIM ** -0.5
    n_groups = len(DILATIONS)

    def body(q0_ref, q1_ref, q2_ref, k_ref, v_ref, gate_ref, gain_ref, ta_ref, tb_ref, tc_ref, e_ref,
             o_ref, l_ref, ao_ref, qd, kd, vd, od, ld, on0, on1, on2, ln0, ln1, ln2, kn, qn, bias):
        head0 = lax.broadcasted_iota(jnp.int32, (SPAN, V7X_LANES), 1) < HEAD_DIM

        @pl.when(jnp.logical_and(pl.program_id(0) == 0, pl.program_id(1) == 0))
        def _():
            _set_bias(bias)

        _norm_rope_rows(kn, k_ref, gain_ref[n_groups:n_groups + 1, :], ta_ref, tb_ref, tc_ref, e_ref, seq)
        kd[0:SPAN, :] = jnp.zeros((SPAN, V7X_LANES), BF16)
        vd[0:SPAN, :] = jnp.zeros((SPAN, V7X_LANES), BF16)
        for g, (q_ref, on, ln) in enumerate(((q0_ref, on0, ln0), (q1_ref, on1, ln1), (q2_ref, on2, ln2))):
            dil = DILATIONS[g]
            nb = seq // dil // SPAN
            _norm_rope_rows(qn, q_ref, gain_ref[g:g + 1, :], ta_ref, tb_ref, tc_ref, e_ref, seq)
            for ns, rs, rows in _pieces(dil, seq):
                qd[rs:rs + rows, :] = _strided(qn, ns, rows, dil).astype(BF16)
                kd[SPAN + rs:SPAN + rs + rows, :] = _strided(kn, ns, rows, dil).astype(BF16)
                vd[SPAN + rs:SPAN + rs + rows, :] = _strided(v_ref, ns, rows, dil).astype(BF16)

            def block(j, carry):
                qs = pl.multiple_of(j * SPAN, SPAN)
                q2 = _stack_heads(qd[pl.ds(qs, SPAN), :], head0)
                keys, mask = _block_keys(bias, j, qs, nb)
                kk = kd[keys, :]
                vv = vd[keys, :]
                s = _nt(q2, kk) * scale + mask
                mx = jnp.max(s, axis=1, keepdims=True)
                p = jnp.exp(s - mx)
                den = jnp.sum(p, axis=1, keepdims=True)
                o2 = jnp.dot(p.astype(BF16), vv, preferred_element_type=F32) / den
                l2 = jnp.broadcast_to(mx + jnp.log(den), (2 * SPAN, V7X_LANES))
                od[pl.ds(qs, SPAN), :] = _unstack_heads(o2, head0)
                ld[pl.ds(qs, SPAN), :] = _unstack_heads(l2, head0)
                return carry

            lax.fori_loop(0, nblk, block, 0, unroll=ATTN_UNROLL)
            for ns, rs, rows in _pieces(dil, seq):
                _strided_set(on, ns, rows, dil, od[rs:rs + rows, :])
                _strided_set(ln, ns, rows, dil, ld[rs:rs + rows, :])

        def merge(ci, carry):
            rows = pl.ds(pl.multiple_of(ci * ATTN_PIECE, ATTN_PIECE), ATTN_PIECE)
            ls = [ln0[rows, :], ln1[rows, :], ln2[rows, :]]
            mx = jnp.maximum(jnp.maximum(ls[0], ls[1]), ls[2])
            es = [jnp.exp(v - mx) for v in ls]
            den = es[0] + es[1] + es[2]
            ov = (es[0] * on0[rows, :] + es[1] * on1[rows, :] + es[2] * on2[rows, :]) / den
            gate = gate_ref[rows, :]
            o_ref[rows, :] = ov
            l_ref[rows, :] = mx + jnp.log(den)
            ao_ref[rows, :] = (ov * gate * _sig(gate)).astype(BF16)
            return carry

        lax.fori_loop(0, seq // ATTN_PIECE, merge, 0)

    blk = (None, seq, V7X_LANES)
    pview = proj_b.reshape(bsz, seq, d4)
    kview = kv.reshape(bsz, seq, 2 * d)
    out_spec = pl.BlockSpec(blk, lambda b, h: (b, 0, h))
    tab = pl.BlockSpec((seq, V7X_LANES), lambda b, h: (0, 0))
    nat = pltpu.VMEM((seq, V7X_LANES), F32)
    o, lse, ao = pl.pallas_call(
        body, name=name, grid=(bsz, nhp),
        in_specs=[pl.BlockSpec(blk, lambda b, h: (b, 0, h)),
                  pl.BlockSpec(blk, lambda b, h: (b, 0, nhp + h)),
                  pl.BlockSpec(blk, lambda b, h: (b, 0, 2 * nhp + h)),
                  pl.BlockSpec(blk, lambda b, h: (b, 0, h)),
                  pl.BlockSpec(blk, lambda b, h: (b, 0, nhp + h)),
                  pl.BlockSpec(blk, lambda b, h: (b, 0, 3 * nhp + h)),
                  pl.BlockSpec((n_groups + 1, V7X_LANES), lambda b, h: (0, 0)),
                  tab, tab, tab,
                  pl.BlockSpec((V7X_LANES, V7X_LANES), lambda b, h: (0, 0))],
        out_specs=[out_spec, out_spec, out_spec],
        out_shape=[jax.ShapeDtypeStruct((bsz, seq, d), F32), jax.ShapeDtypeStruct((bsz, seq, d), F32),
                   jax.ShapeDtypeStruct((bsz, seq, d), BF16)],
        scratch_shapes=[pltpu.VMEM((seq, V7X_LANES), BF16), pltpu.VMEM((SPAN + seq, V7X_LANES), BF16),
                        pltpu.VMEM((SPAN + seq, V7X_LANES), BF16), nat, nat, nat, nat, nat, nat, nat, nat, nat, nat,
                        pltpu.VMEM((2, 2 * SPAN, 2 * SPAN), F32)],
        compiler_params=_params(2),
    )(pview, pview, pview, kview, kview, pview, gains, *tables, ones)
    return o.reshape(t, d), lse.reshape(t, d), ao.reshape(t, d)


def _attn_bwd(proj_b, kv, dao, o, lse, gains, tables, ones, bsz, seq, name):
    t, d4 = proj_b.shape
    d = d4 // 4
    nhp = d // V7X_LANES
    nblk = seq // SPAN
    scale = HEAD_DIM ** -0.5
    n_groups = len(DILATIONS)
    n_chunks = seq // ATTN_PIECE

    def body(q_ref, k_ref, v_ref, gate_ref, dao_ref, o_ref, l_ref, gain_ref, ta_ref, tb_ref, tc_ref, e_ref,
             dproj_ref, dkv_ref, dg_ref, qd, kd, vd, dod, std, dqd, dkd, dvd, dqn, dk0, dk1, dk2, dv0, dv1, dv2,
             kn, kxh, krr, qn, qxh, qrr, don, stn, bias):
        head0 = lax.broadcasted_iota(jnp.int32, (SPAN, V7X_LANES), 1) < HEAD_DIM
        g = pl.program_id(2)

        @pl.when(jnp.logical_and(jnp.logical_and(pl.program_id(0) == 0, pl.program_id(1) == 0), g == 0))
        def _():
            _set_bias(bias)
            dg_ref[...] = jnp.zeros_like(dg_ref)

        @pl.when(g == 0)
        def _():
            first_half = (lax.broadcasted_iota(jnp.int32, (ATTN_PIECE, V7X_LANES), 1) & (HEAD_DIM - 1)) < HEAD_DIM // 2
            _norm_rope_rows(kn, k_ref, gain_ref[n_groups:n_groups + 1, :], ta_ref, tb_ref, tc_ref, e_ref, seq,
                            kxh, krr)
            for r0 in range(0, seq, ATTN_PIECE):
                rows = slice(r0, r0 + ATTN_PIECE)
                gate = gate_ref[rows, :]
                dov = dao_ref[rows, :].astype(F32) * gate * _sig(gate)
                don[rows, :] = dov
                stn[rows, :] = jnp.where(first_half, l_ref[rows, :], _segsum(dov * o_ref[rows, :], e_ref))

        def norm_bwd_chunks(xhat_ref, r_ref, dn_refs, out_ref, gi):
            def chunk(ci, carry):
                rows = pl.ds(pl.multiple_of(ci * ATTN_PIECE, ATTN_PIECE), ATTN_PIECE)
                dn = functools.reduce(lambda u, w: u + w, [r_[rows, :] for r_ in dn_refs])
                dx, part = _norm_rope_bwd(xhat_ref[rows, :], r_ref[rows, :], dn, gain_ref[gi:gi + 1, :],
                                          ta_ref[rows, :], tb_ref[rows, :], tc_ref[rows, :], e_ref)
                out_ref[rows, :] = dx.astype(BF16)
                dg_ref[gi] += part
                return carry
            lax.fori_loop(0, n_chunks, chunk, 0, unroll=CHUNK_UNROLL)

        def group(gi):
            dil = DILATIONS[gi]
            nb = seq // dil // SPAN
            kd[0:SPAN, :] = jnp.zeros((SPAN, V7X_LANES), BF16)
            vd[0:SPAN, :] = jnp.zeros((SPAN, V7X_LANES), BF16)
            dkd[...] = jnp.zeros_like(dkd)
            dvd[...] = jnp.zeros_like(dvd)
            _norm_rope_rows(qn, q_ref, gain_ref[gi:gi + 1, :], ta_ref, tb_ref, tc_ref, e_ref, seq, qxh, qrr)
            for ns, rs, rows in _pieces(dil, seq):
                qd[rs:rs + rows, :] = _strided(qn, ns, rows, dil).astype(BF16)
                kd[SPAN + rs:SPAN + rs + rows, :] = _strided(kn, ns, rows, dil).astype(BF16)
                vd[SPAN + rs:SPAN + rs + rows, :] = _strided(v_ref, ns, rows, dil).astype(BF16)
                dod[rs:rs + rows, :] = _strided(don, ns, rows, dil).astype(BF16)
                std[rs:rs + rows, :] = _strided(stn, ns, rows, dil)

            def block(j, carry):
                qs = pl.multiple_of(j * SPAN, SPAN)
                q2 = _stack_heads(qd[pl.ds(qs, SPAN), :], head0)
                do2 = _stack_heads(dod[pl.ds(qs, SPAN), :], head0)
                keys, mask = _block_keys(bias, j, qs, nb)
                kk = kd[keys, :]
                vv = vd[keys, :]
                s = _nt(q2, kk) * scale + mask
                stv = std[pl.ds(qs, SPAN), :]
                p = jnp.exp(s - _head_cols(stv))
                ds = (p * (_nt(do2, vv) - _head_cols(stv, HEAD_DIM // 2)) * scale).astype(BF16)
                dqd[pl.ds(qs, SPAN), :] = _unstack_heads(jnp.dot(ds, kk, preferred_element_type=F32), head0)
                dkd[keys, :] += _tn(ds, q2)
                dvd[keys, :] += _tn(p.astype(BF16), do2)
                return carry

            lax.fori_loop(0, nblk, block, 0, unroll=ATTN_UNROLL)
            for ns, rs, rows in _pieces(dil, seq):
                _strided_set(dqn, ns, rows, dil, dqd[rs:rs + rows, :])
                _strided_set((dk0, dk1, dk2)[gi], ns, rows, dil, dkd[SPAN + rs:SPAN + rs + rows, :])
                _strided_set((dv0, dv1, dv2)[gi], ns, rows, dil, dvd[SPAN + rs:SPAN + rs + rows, :])
            norm_bwd_chunks(qxh, qrr, [dqn], dproj_ref, gi)

        for gi in range(n_groups):
            @pl.when(g == gi)
            def _():
                group(gi)

        @pl.when(g == n_groups - 1)
        def _():
            norm_bwd_chunks(kxh, krr, [dk0, dk1, dk2], dkv_ref, n_groups)

        @pl.when(g == n_groups)
        def _():
            def chunk(ci, carry):
                rows = pl.ds(pl.multiple_of(ci * ATTN_PIECE, ATTN_PIECE), ATTN_PIECE)
                gate = gate_ref[rows, :]
                sg = _sig(gate)
                dproj_ref[rows, :] = (dao_ref[rows, :].astype(F32) * o_ref[rows, :]
                                      * (sg * (1.0 + gate * (1.0 - sg)))).astype(BF16)
                dkv_ref[rows, :] = (dv0[rows, :] + dv1[rows, :] + dv2[rows, :]).astype(BF16)
                return carry
            lax.fori_loop(0, n_chunks, chunk, 0, unroll=CHUNK_UNROLL)

    blk = (None, seq, V7X_LANES)
    pview = proj_b.reshape(bsz, seq, d4)
    kview = kv.reshape(bsz, seq, 2 * d)
    dview = (bsz, seq, d)
    d_spec = pl.BlockSpec(blk, lambda b, h, g: (b, 0, h))
    tab = pl.BlockSpec((seq, V7X_LANES), lambda b, h, g: (0, 0))
    nat = pltpu.VMEM((seq, V7X_LANES), F32)
    natb = pltpu.VMEM((seq, V7X_LANES), BF16)
    pad = pltpu.VMEM((SPAN + seq, V7X_LANES), F32)
    padb = pltpu.VMEM((SPAN + seq, V7X_LANES), BF16)
    dproj, dkv, dg = pl.pallas_call(
        body, name=name, grid=(bsz, nhp, n_groups + 1),
        in_specs=[pl.BlockSpec(blk, lambda b, h, g: (b, 0, jnp.minimum(g, n_groups - 1) * nhp + h)),
                  pl.BlockSpec(blk, lambda b, h, g: (b, 0, h)),
                  pl.BlockSpec(blk, lambda b, h, g: (b, 0, nhp + h)),
                  pl.BlockSpec(blk, lambda b, h, g: (b, 0, n_groups * nhp + h)),
                  d_spec, d_spec, d_spec,
                  pl.BlockSpec((n_groups + 1, V7X_LANES), lambda b, h, g: (0, 0)),
                  tab, tab, tab,
                  pl.BlockSpec((V7X_LANES, V7X_LANES), lambda b, h, g: (0, 0))],
        out_specs=[pl.BlockSpec(blk, lambda b, h, g: (b, 0, g * nhp + h)),
                   pl.BlockSpec(blk, lambda b, h, g: (b, 0, (g // n_groups) * nhp + h)),
                   pl.BlockSpec((n_groups + 1, V7X_SUBLANES, V7X_LANES), lambda b, h, g: (0, 0, 0))],
        out_shape=[jax.ShapeDtypeStruct((bsz, seq, d4), BF16), jax.ShapeDtypeStruct((bsz, seq, 2 * d), BF16),
                   jax.ShapeDtypeStruct((n_groups + 1, V7X_SUBLANES, V7X_LANES), F32)],
        scratch_shapes=[natb, padb, padb, natb, nat, nat, pad, pad] + [nat] * 15 + [
                        pltpu.VMEM((2, 2 * SPAN, 2 * SPAN), F32)],
        compiler_params=_params(3),
    )(pview, kview, kview, pview, dao.reshape(dview), o.reshape(dview), lse.reshape(dview), gains, *tables, ones)
    dgain = dg.sum(axis=1).reshape(n_groups + 1, V7X_LANES // HEAD_DIM, HEAD_DIM).sum(axis=1)
    return dproj.reshape(t, d4), dkv.reshape(t, 2 * d), dgain


def _mesh_position():
    x, y, c = lax.axis_index("x"), lax.axis_index("y"), lax.axis_index("c")
    return x, y, c


def _peer(x, y, c, rel):
    return (1 - x if rel & 4 else x, 1 - y if rel & 2 else y, 1 - c if rel & 1 else c)


class _Exchange:
    def __init__(self, srcs, gather):
        self.srcs = list(srcs)
        self.gather = gather
        n = self.n = len(self.srcs)
        hbm = pl.BlockSpec(memory_space=pltpu.HBM)
        self.in_specs = [hbm] * n
        self.out_specs = [hbm] * n
        self.out_shape = [jax.ShapeDtypeStruct(((N_DEV,) + a.shape) if gather else a.shape, a.dtype)
                          for a in self.srcs]
        self.scratch = [pltpu.SemaphoreType.DMA((n * (N_DEV - 1),)), pltpu.SemaphoreType.DMA((n * (N_DEV - 1),)),
                        pltpu.SemaphoreType.DMA((n,))]

    def _copies(self, ins, outs, sems):
        send_sems, recv_sems, local_sems = sems
        x, y, c = _mesh_position()
        me = 4 * x + 2 * y + c
        remote, local = [], []
        for a in range(self.n):
            mine = ins[a] if self.gather else ins[a].at[me]
            local.append(pltpu.make_async_copy(mine, outs[a].at[me], local_sems.at[a]))
            for rel in range(1, N_DEV):
                px, py, pc = _peer(x, y, c, rel)
                s = a * (N_DEV - 1) + rel - 1
                src = ins[a] if self.gather else ins[a].at[4 * px + 2 * py + pc]
                remote.append(pltpu.make_async_remote_copy(
                    src_ref=src, dst_ref=outs[a].at[me], send_sem=send_sems.at[s], recv_sem=recv_sems.at[s],
                    device_id=(px, py, pc), device_id_type=pl.DeviceIdType.MESH))
        return remote, local

    def start(self, ins, outs, sems):
        remote, local = self._copies(ins, outs, sems)
        for cp in local + remote:
            cp.start()

    def wait(self, ins, outs, sems):
        remote, local = self._copies(ins, outs, sems)
        for cp in remote:
            cp.wait_recv()
        for cp in remote:
            cp.wait_send()
        for cp in local:
            cp.wait()


def _gather_chip_once(arrs, name):
    n = len(arrs)
    per = N_DEV - 1

    def body(*refs):
        ins, outs = refs[:n], refs[n:2 * n]
        send_sems, recv_sems, local_sems = refs[2 * n:]
        x, y, c = _mesh_position()
        me, sibling = (x, y, c), (x, y, 1 - c)
        chips = [(1 - x, y), (x, 1 - y), (1 - x, 1 - y)]

        def copy(a, k, block, to, src=None):
            bx, by, bc = block
            dst = outs[a].at[4 * bx + 2 * by + bc]
            return pltpu.make_async_remote_copy(
                src_ref=dst if src is None else src, dst_ref=dst, send_sem=send_sems.at[a * per + k],
                recv_sem=recv_sems.at[a * per + k], device_id=to, device_id_type=pl.DeviceIdType.MESH)

        local, sent = [], []
        for a in range(n):
            mine = pltpu.make_async_copy(ins[a], outs[a].at[4 * x + 2 * y + c], local_sems.at[a])
            mine.start()
            local.append(mine)
            first = [copy(a, 0, me, sibling, src=ins[a])]
            first += [copy(a, 1 + j, me, chip + (c,), src=ins[a]) for j, chip in enumerate(chips)]
            for cp in first:
                cp.start()
            sent += first
        for a in range(n):
            for j, chip in enumerate(chips):
                copy(a, 1 + j, chip + (c,), me).wait_recv()
                passed = copy(a, 4 + j, chip + (c,), sibling)
                passed.start()
                sent.append(passed)
        for a in range(n):
            copy(a, 0, sibling, me).wait_recv()
            for j, chip in enumerate(chips):
                copy(a, 4 + j, chip + (1 - c,), me).wait_recv()
        for cp in sent:
            cp.wait_send()
        for cp in local:
            cp.wait()

    hbm = pl.BlockSpec(memory_space=pltpu.HBM)
    return pl.pallas_call(
        body, name=name, in_specs=[hbm] * n, out_specs=[hbm] * n,
        out_shape=[jax.ShapeDtypeStruct((N_DEV,) + a.shape, a.dtype) for a in arrs],
        scratch_shapes=[pltpu.SemaphoreType.DMA((n * per,)), pltpu.SemaphoreType.DMA((n * per,)),
                        pltpu.SemaphoreType.DMA((n,))],
    )(*arrs)


def _run_exchange(ex, name):
    n = ex.n

    def body(*refs):
        ins, outs, sems = refs[:n], refs[n:2 * n], refs[2 * n:]
        ex.start(ins, outs, sems)
        ex.wait(ins, outs, sems)

    return pl.pallas_call(body, name=name, in_specs=ex.in_specs, out_specs=ex.out_specs, out_shape=ex.out_shape,
                          scratch_shapes=ex.scratch)(*ex.srcs)


def _hosted_call(body, ex, name, grid, in_specs, out_specs, out_shape, scratch_shapes, args):
    if ex is None:
        outs = pl.pallas_call(body, name=name, grid=grid, in_specs=in_specs, out_specs=out_specs, out_shape=out_shape,
                              scratch_shapes=scratch_shapes, compiler_params=_params(len(grid)))(*args)
        return list(outs), []
    n_in, n_out, n_scr, n = len(in_specs), len(out_specs), len(scratch_shapes), ex.n

    def hosted(*refs):
        h_in, e_in = refs[:n_in], refs[n_in:n_in + n]
        o0 = n_in + n
        h_out, e_out = refs[o0:o0 + n_out], refs[o0 + n_out:o0 + n_out + n]
        s0 = o0 + n_out + n
        h_scr, e_scr = refs[s0:s0 + n_scr], refs[s0 + n_scr:]
        ids = [pl.program_id(a) for a in range(len(grid))]
        first = functools.reduce(jnp.logical_and, [i == 0 for i in ids])
        last = functools.reduce(jnp.logical_and, [i == g - 1 for i, g in zip(ids, grid)])

        @pl.when(first)
        def _():
            ex.start(e_in, e_out, e_scr)

        body(*h_in, *h_out, *h_scr)

        @pl.when(last)
        def _():
            ex.wait(e_in, e_out, e_scr)

    outs = pl.pallas_call(
        hosted, name=name, grid=grid, in_specs=list(in_specs) + ex.in_specs,
        out_specs=list(out_specs) + ex.out_specs, out_shape=list(out_shape) + ex.out_shape,
        scratch_shapes=list(scratch_shapes) + ex.scratch, compiler_params=_params(len(grid)),
    )(*args, *ex.srcs)
    return list(outs[:n_out]), list(outs[n_out:])


def _sum_adamw(parts, w, m, v, name):
    _, r, wd = parts.shape
    tr = _pick(r, ADAM_ROWS, 8)
    c1 = 1.0 - ADAM_B1 ** ADAM_STEP
    c2 = 1.0 - ADAM_B2 ** ADAM_STEP

    def body(p_ref, w_ref, m_ref, v_ref, g_ref, d_ref, nm_ref, nv_ref):
        g = p_ref[0].astype(F32)
        for s in range(1, N_DEV):
            g = g + p_ref[s].astype(F32)
        nm = ADAM_B1 * m_ref[...] + (1.0 - ADAM_B1) * g
        nv = ADAM_B2 * v_ref[...] + (1.0 - ADAM_B2) * (g * g)
        g_ref[...] = g
        nm_ref[...] = nm
        nv_ref[...] = nv
        d_ref[...] = -ADAM_LR * ((nm / c1) / (jnp.sqrt(nv / c2) + ADAM_EPS) + ADAM_WD * w_ref[...])

    row = pl.BlockSpec((tr, wd), lambda i: (i, 0))
    return pl.pallas_call(
        body, name=name, grid=(r // tr,),
        in_specs=[pl.BlockSpec((N_DEV, tr, wd), lambda i: (0, i, 0)), row, row, row],
        out_specs=[row] * 4, out_shape=[jax.ShapeDtypeStruct((r, wd), F32)] * 4,
        compiler_params=_params(1),
    )(parts, w, m, v)


def _pack_rows(size, row_mult):
    rows = -(-size // PACK_LANES)
    return -(-rows // row_mult) * row_mult


def _pack(flats, row_mult, dtype, total_mult=None):
    out = []
    for f in flats:
        size = f.shape[-1]
        rows = _pack_rows(size, row_mult)
        pad = [(0, 0)] * (f.ndim - 1) + [(0, rows * PACK_LANES - size)]
        out.append(jnp.pad(f.astype(dtype), pad).reshape(f.shape[:-1] + (rows, PACK_LANES)))
    if total_mult is not None:
        total = sum(o.shape[-2] for o in out)
        extra = -(-total // total_mult) * total_mult - total
        if extra:
            out.append(jnp.zeros(out[0].shape[:-2] + (extra, PACK_LANES), dtype))
    return jnp.concatenate(out, axis=-2)


def _unpack(buf, sizes, row_mult):
    out, row = [], 0
    for size in sizes:
        rows = _pack_rows(size, row_mult)
        part = buf[..., row:row + rows, :]
        out.append(part.reshape(buf.shape[:-2] + (rows * PACK_LANES,))[..., :size])
        row += rows
    return out


def _to_slots(full, axis):
    if axis is None:
        return jnp.broadcast_to(full.reshape(1, -1), (N_DEV, full.size))
    shape = full.shape
    split = full.reshape(shape[:axis] + (N_DEV, shape[axis] // N_DEV) + shape[axis + 1:])
    return jnp.moveaxis(split, axis, 0).reshape(N_DEV, -1)


def _from_slots(slots, axis, block_shape):
    split = jnp.moveaxis(slots, 0, axis)
    shape = list(block_shape)
    shape[axis] *= N_DEV
    return split.reshape(shape)


def kernel(x, p, norm_g, w_in_a, conv_w, conv_b, ln_g, ln_b, w_out_a, kv_norm_g, w_kv, k_norm_g, w_in_b, q_norm_g, w_out_b, ple_norm_g, w_ple_gate, w_ple_proj, loss_target, m_norm_g, m_w_in_a, m_conv_w, m_conv_b, m_ln_g, m_ln_b, m_w_out_a, m_kv_norm_g, m_w_kv, m_k_norm_g, m_w_in_b, m_q_norm_g, m_w_out_b, m_ple_norm_g, m_w_ple_gate, m_w_ple_proj, v_norm_g, v_w_in_a, v_conv_w, v_conv_b, v_ln_g, v_ln_b, v_w_out_a, v_kv_norm_g, v_w_kv, v_k_norm_g, v_w_in_b, v_q_norm_g, v_w_out_b, v_ple_norm_g, v_w_ple_gate, v_w_ple_proj):
    weights = dict(zip(WEIGHT_NAMES, (norm_g, w_in_a, conv_w, conv_b, ln_g, ln_b, w_out_a, kv_norm_g, w_kv, k_norm_g,
                                      w_in_b, q_norm_g, w_out_b, ple_norm_g, w_ple_gate, w_ple_proj)))
    mom_m = dict(zip(WEIGHT_NAMES, (m_norm_g, m_w_in_a, m_conv_w, m_conv_b, m_ln_g, m_ln_b, m_w_out_a, m_kv_norm_g,
                                    m_w_kv, m_k_norm_g, m_w_in_b, m_q_norm_g, m_w_out_b, m_ple_norm_g, m_w_ple_gate,
                                    m_w_ple_proj)))
    mom_v = dict(zip(WEIGHT_NAMES, (v_norm_g, v_w_in_a, v_conv_w, v_conv_b, v_ln_g, v_ln_b, v_w_out_a, v_kv_norm_g,
                                    v_w_kv, v_k_norm_g, v_w_in_b, v_q_norm_g, v_w_out_b, v_ple_norm_g, v_w_ple_gate,
                                    v_w_ple_proj)))
    bsz, seq, d = x.shape
    t = bsz * seq
    assert seq % (max(DILATIONS) * SPAN) == 0 and d % V7X_LANES == 0

    full = {}

    def rows2d(a):
        return a.reshape(-1, a.shape[-1])

    def packed(source, names, dtype, total_mult=None):
        return _pack([source[n].reshape(-1) for n in names], 16, dtype, total_mult)

    def gathered(names, bufs):
        for n, buf in zip(names, bufs):
            full[n] = _from_slots(buf.reshape((N_DEV,) + weights[n].shape), SHARD_AXIS[n], weights[n].shape)

    w1_all, wv_all = _gather_chip_once([rows2d(weights['w_in_a']).astype(BF16),
                                        _pack([weights[n].reshape(-1) for n in VECTOR_WEIGHTS], 8, F32)],
                                       "gather_first")
    gathered(GROUP_FIRST, [w1_all])
    for n, slots in zip(VECTOR_WEIGHTS, _unpack(wv_all, [weights[n].size for n in VECTOR_WEIGHTS], 8)):
        full[n] = _from_slots(slots.reshape((N_DEV,) + weights[n].shape), SHARD_AXIS[n], weights[n].shape)
    gather_rest = _Exchange([rows2d(weights[n]).astype(BF16) for n in GROUP_REST], gather=True)
    wa_in = full['w_in_a'][0]
    cw, cb, lg, lb = full['conv_w'][0], full['conv_b'], full['ln_g'], full['ln_b']

    tables = _rope_tables(seq)
    ones = _head_ones(V7X_LANES)
    rep = V7X_LANES // HEAD_DIM
    head_gain = jnp.concatenate([jnp.tile(q_norm_g[0], (1, rep)), jnp.tile(k_norm_g, rep)[None]], axis=0)

    x0 = x.reshape(t, d)
    p0, p1 = p[0].reshape(t, -1), p[1].reshape(t, -1)
    target = loss_target.reshape(t, d)
    g_norm0, g_norm1 = norm_g[0:1], norm_g[1:2]
    g_ple0, g_ple1 = ple_norm_g[0:1], ple_norm_g[1:2]
    g_kv = kv_norm_g.reshape(1, d)

    (u0,) = _rmsnorm_fwd(x0, [g_norm0], "norm0")
    proj_a = _matmul(u0, wa_in, 'nn', "in_a")
    m_act, y_conv, w2_all = _conv_fwd(proj_a, cw, cb, lg, lb, seq, "conv_fwd", ex=gather_rest)
    gathered(GROUP_REST, w2_all)
    wa_out = full['w_out_a'][0]
    wkv = full['w_kv']
    wb_in, wb_out = full['w_in_b'][0], full['w_out_b'][0]
    wg, wp = full['w_ple_gate'], full['w_ple_proj']
    h0, pg0 = _matmul(m_act, wa_out, 'nn', "out_a", add=x0, norm_gain=g_ple0)
    gl0 = _matmul(pg0, wg[0], 'nn', "ple_gate0", out_dtype=BF16)
    pp0 = _matmul(p0, wp[0], 'nn', "ple_proj0", out_dtype=BF16)

    x1, (kvn, u1) = _ple_norm_fwd(h0, gl0, pp0, [g_kv, g_norm1], "ple0_norm1")
    kv = _matmul(kvn, wkv, 'nn', "kv")
    proj_b = _matmul(u1, wb_in, 'nn', "in_b")
    o_att, lse, ao = _attn_fwd(proj_b, kv, head_gain, tables, ones, bsz, seq, "attn_fwd")
    h1, pg1 = _matmul(ao, wb_out, 'nn', "out_b", add=x1, norm_gain=g_ple1)
    gl1 = _matmul(pg1, wg[1], 'nn', "ple_gate1", out_dtype=BF16)
    pp1 = _matmul(p1, wp[1], 'nn', "ple_proj1", out_dtype=BF16)

    dx2, dgl1, dpp1, loss_part = _ple_loss(h1, gl1, pp1, target, "ple1_loss")
    loss = lax.psum(jnp.sum(loss_part), ("x", "y", "c"))

    grads = {}
    slot = {}

    dwp1 = _matmul(p1, dpp1, 'tn', "d_ple_proj1", out_dtype=BF16, slot_cols=d // N_DEV)
    dwg1 = _matmul(pg1, dgl1, 'tn', "d_ple_gate1", out_dtype=BF16)
    dpg1 = _matmul(dgl1, wg[1], 'nt', "d_ple_norm1", out_dtype=BF16)
    dh1, (dg_ple1,) = _rmsnorm_bwd(h1, [g_ple1], [dpg1], dx2, "ple_norm1_bwd")
    slot['w_out_b'] = _matmul(ao, dh1, 'tn', "d_out_b", out_dtype=BF16).reshape(N_DEV, -1, d)
    dao = _matmul(dh1, wb_out, 'nt', "d_ao", out_dtype=BF16)
    dproj_b, dkv, dg_head = _attn_bwd(proj_b, kv, dao, o_att, lse, head_gain, tables, ones, bsz, seq, "attn_bwd")
    slot['w_in_b'] = _matmul(u1, dproj_b, 'tn', "d_in_b", out_dtype=BF16, slot_cols=4 * d // N_DEV)
    du1 = _matmul(dproj_b, wb_in, 'nt', "d_u1", out_dtype=BF16)
    slot['w_kv'] = _matmul(kvn, dkv, 'tn', "d_kv", out_dtype=BF16, slot_cols=2 * d // N_DEV)
    dkvn = _matmul(dkv, wkv, 'nt', "d_kvn", out_dtype=BF16)
    dx1, (dg_kv, dg_norm1), dgl0, dpp0 = _rmsnorm_bwd(x1, [g_kv, g_norm1], [dkvn, du1], dh1, "norm1_bwd",
                                                      ple=(gl0, pp0))

    dwp0 = _matmul(p0, dpp0, 'tn', "d_ple_proj0", out_dtype=BF16, slot_cols=d // N_DEV)
    dwg0 = _matmul(pg0, dgl0, 'tn', "d_ple_gate0", out_dtype=BF16)
    dpg0 = _matmul(dgl0, wg[0], 'nt', "d_ple_norm0", out_dtype=BF16)
    dh0, (dg_ple0,) = _rmsnorm_bwd(h0, [g_ple0], [dpg0], dx1, "ple_norm0_bwd")
    slot['w_out_a'] = _matmul(m_act, dh0, 'tn', "d_out_a", out_dtype=BF16).reshape(N_DEV, -1, d)
    dm = _matmul(dh0, wa_out, 'nt', "d_m", out_dtype=BF16)
    dy_conv, dz, d_lg, d_lb, d_cb = _ln_gate_bwd(dm, y_conv, proj_a, lg, lb, "ln_gate_bwd")
    slot['w_ple_gate'] = jnp.stack([dwg0.reshape(N_DEV, -1, d), dwg1.reshape(N_DEV, -1, d)],
                                   axis=1).reshape(N_DEV, -1, d)
    slot['w_ple_proj'] = jnp.stack([dwp0, dwp1], axis=1).reshape(N_DEV, -1, d // N_DEV)

    dproj_a, d_cw, parts_rest = _conv_bwd(dy_conv, dz, proj_a, cw, seq, "conv_bwd",
                                          ex=_Exchange([slot[n] for n in GROUP_REST], gather=False))
    slot['w_in_a'] = _matmul(u0, dproj_a, 'tn', "d_in_a", out_dtype=BF16, slot_cols=wa_in.shape[1] // N_DEV)
    du0, parts_first = _matmul(dproj_a, wa_in, 'nt', "d_u0", out_dtype=BF16,
                               ex=_Exchange([slot['w_in_a']], gather=False))
    dx0, (dg_norm0,) = _rmsnorm_bwd(x0, [g_norm0], [du0], dh0, "norm0_bwd", dx_dtype=F32)

    grads['norm_g'] = jnp.stack([dg_norm0, dg_norm1])
    grads['conv_w'] = d_cw[None]
    grads['conv_b'] = d_cb[None]
    grads['ln_g'] = d_lg[None]
    grads['ln_b'] = d_lb[None]
    grads['kv_norm_g'] = dg_kv
    grads['k_norm_g'] = dg_head[3]
    grads['q_norm_g'] = dg_head[0:3][None]
    grads['ple_norm_g'] = jnp.stack([dg_ple0, dg_ple1])
    small_pack = _pack([_to_slots(grads[n], SHARD_AXIS[n]) for n in GROUP_SMALL], 16, BF16)
    (parts_small,) = _run_exchange(_Exchange([small_pack], gather=False), "exchange_small")

    updated = {}
    for n, parts in zip(GROUP_REST + GROUP_FIRST, parts_rest + parts_first):
        outs = _sum_adamw(parts, rows2d(weights[n]), rows2d(mom_m[n]), rows2d(mom_v[n]), "sum_adamw_" + n)
        for kind, buf in enumerate(outs):
            updated[kind, n] = buf.reshape(weights[n].shape)
    outs = _sum_adamw(parts_small, packed(weights, GROUP_SMALL, F32), packed(mom_m, GROUP_SMALL, F32),
                      packed(mom_v, GROUP_SMALL, F32), "sum_adamw_small")
    sizes = [weights[n].size for n in GROUP_SMALL]
    for kind, buf in enumerate(outs):
        for n, flat in zip(GROUP_SMALL, _unpack(buf, sizes, 16)):
            updated[kind, n] = flat.reshape(weights[n].shape)
    result = [loss, dx0.reshape(bsz, seq, d)]
    for kind in range(4):
        result.extend(updated[kind, n] for n in WEIGHT_NAMES)
    return tuple(result)
```
